```python
import jax, jax.numpy as jnp
from jax import lax
import numpy as np

D_MODEL = 1024
BATCH = 8
SEQ = 4096
DEPTH = 1

MIX_WIDTH = D_MODEL
CONV_WIDTH = MIX_WIDTH // 2
CONV_GROUPS = 8
CONV_KERNEL = 31
GM_WIDTH = MIX_WIDTH - CONV_WIDTH
GM_HEADS = 8
GM_HEAD_DIM = GM_WIDTH // GM_HEADS
CHUNK = 128
IN_COLS = 2 * CONV_WIDTH + 2 * GM_WIDTH
MEM_LEN = 256
XA_HEADS = 4
XA_HEAD_DIM = D_MODEL // XA_HEADS
FFN_HIDDEN = ((8 * D_MODEL // 3 + 255) // 256) * 256
RMS_EPS = 1e-6
LN_EPS = 1e-5

kernel_name = "hybrid_conv_gmlp_xattn_block"


def rmsnorm(x, g):
    xf = x.astype(jnp.float32)
    y = xf * lax.rsqrt(jnp.mean(xf * xf, axis=-1, keepdims=True) + RMS_EPS)
    return (y * g.astype(jnp.float32)).astype(x.dtype)


def layernorm(x, g, b):
    xf = x.astype(jnp.float32)
    mu = jnp.mean(xf, axis=-1, keepdims=True)
    var = jnp.mean(jnp.square(xf - mu), axis=-1, keepdims=True)
    y = (xf - mu) * lax.rsqrt(var + LN_EPS)
    return (y * g.astype(jnp.float32) + b.astype(jnp.float32)).astype(x.dtype)


def causal_depthwise_conv(a, w, b):
    k, c = w.shape
    a_pad = jnp.pad(a, ((0, 0), (k - 1, 0), (0, 0)))
    y = lax.conv_general_dilated(
        a_pad, w[:, None, :], window_strides=(1,), padding='VALID',
        dimension_numbers=('NWC', 'WIO', 'NWC'), feature_group_count=c)
    return y + b


def conformer_conv_group(za, zg, conv_w, conv_b, ln_g, ln_b):
    a = za * jax.nn.sigmoid(zg)
    a = causal_depthwise_conv(a, conv_w, conv_b)
    a = layernorm(a, ln_g, ln_b)
    return jax.nn.silu(a)


def gmlp_group(zu, zv, ln_g, ln_b, w_s, b_s):
    u = jax.nn.gelu(zu)
    v = layernorm(jax.nn.gelu(zv), ln_g, ln_b)
    bsz, s, _ = v.shape
    vh = v.reshape(bsz, s // CHUNK, CHUNK, GM_HEADS, GM_HEAD_DIM)
    mask = jnp.tril(jnp.ones((CHUNK, CHUNK), dtype=bool))
    ws = jnp.where(mask[None], w_s, jnp.zeros_like(w_s))
    mixed = jnp.einsum('hts,bnshd->bnthd', ws, vh)
    mixed = mixed + b_s.T[None, None, :, :, None]
    return u * mixed.reshape(bsz, s, GM_WIDTH)


def cross_attention(hn, mn, wq, wkv, wo):
    bsz, s, _ = hn.shape
    q = (hn @ wq).reshape(bsz, s, XA_HEADS, XA_HEAD_DIM)
    kv = mn @ wkv
    k, v = jnp.split(kv, 2, axis=-1)
    k = k.reshape(bsz, MEM_LEN, XA_HEADS, XA_HEAD_DIM)
    v = v.reshape(bsz, MEM_LEN, XA_HEADS, XA_HEAD_DIM)
    scale = XA_HEAD_DIM ** -0.5
    scores = jnp.einsum('bshd,bmhd->bhsm', q, k).astype(jnp.float32) * scale
    p = jax.nn.softmax(scores, axis=-1).astype(v.dtype)
    o = jnp.einsum('bhsm,bmhd->bshd', p, v).reshape(bsz, s, D_MODEL)
    return o @ wo


def swiglu(hn, w_gate_up, w_down):
    gu = hn @ w_gate_up
    g, u = jnp.split(gu, 2, axis=-1)
    return (jax.nn.silu(g) * u) @ w_down


def _fwd_setup_inputs(seed: int = 0) -> dict:
    key = jax.random.key(seed)
    ks = jax.random.split(key, 24)
    f32 = jnp.float32

    def nrm(k, shape, scale):
        return jax.random.normal(k, shape, f32) * scale

    def gain(k, n):
        return jnp.ones((n,), f32) + 0.05 * jax.random.normal(k, (n,), f32)

    return {
        "x": nrm(ks[0], (BATCH, SEQ, D_MODEL), 1.0),
        "mem": nrm(ks[1], (BATCH, MEM_LEN, D_MODEL), 1.0),
        "norm_mix_g": gain(ks[2], D_MODEL),
        "w_in": nrm(ks[3], (D_MODEL, IN_COLS), D_MODEL ** -0.5),
        "b_in": nrm(ks[4], (IN_COLS,), 0.02),
        "conv_w": nrm(ks[5], (CONV_KERNEL, CONV_WIDTH), CONV_KERNEL ** -0.5),
        "conv_b": nrm(ks[6], (CONV_WIDTH,), 0.02),
        "conv_ln_g": gain(ks[7], CONV_WIDTH),
        "conv_ln_b": nrm(ks[8], (CONV_WIDTH,), 0.02),
        "gm_ln_g": gain(ks[9], GM_WIDTH),
        "gm_ln_b": nrm(ks[10], (GM_WIDTH,), 0.02),
        "gm_w_s": nrm(ks[11], (GM_HEADS, CHUNK, CHUNK), CHUNK ** -0.5),
        "gm_b_s": jnp.ones((GM_HEADS, CHUNK), f32) + 0.1 * jax.random.normal(ks[12], (GM_HEADS, CHUNK), f32),
        "w_out": nrm(ks[13], (MIX_WIDTH, D_MODEL), MIX_WIDTH ** -0.5),
        "norm_xa_g": gain(ks[14], D_MODEL),
        "mem_norm_g": gain(ks[15], D_MODEL),
        "xa_wq": nrm(ks[16], (D_MODEL, D_MODEL), D_MODEL ** -0.5),
        "xa_wkv": nrm(ks[17], (D_MODEL, 2 * D_MODEL), D_MODEL ** -0.5),
        "xa_wo": nrm(ks[18], (D_MODEL, D_MODEL), D_MODEL ** -0.5),
        "norm_ffn_g": gain(ks[19], D_MODEL),
        "ffn_w_gate_up": nrm(ks[20], (D_MODEL, 2 * FFN_HIDDEN), D_MODEL ** -0.5),
        "ffn_w_down": nrm(ks[21], (FFN_HIDDEN, D_MODEL), FFN_HIDDEN ** -0.5),
        "final_norm_g": gain(ks[22], D_MODEL),
    }


def _fwd_reference(x, mem, norm_mix_g, w_in, b_in, conv_w, conv_b, conv_ln_g, conv_ln_b,
              gm_ln_g, gm_ln_b, gm_w_s, gm_b_s, w_out, norm_xa_g, mem_norm_g,
              xa_wq, xa_wkv, xa_wo, norm_ffn_g, ffn_w_gate_up, ffn_w_down,
              final_norm_g):
    h = x
    mn = rmsnorm(mem, mem_norm_g)
    for _ in range(DEPTH):
        hn = rmsnorm(h, norm_mix_g)
        z = hn @ w_in + b_in
        za, zg, zu, zv = jnp.split(
            z, [CONV_WIDTH, 2 * CONV_WIDTH, 2 * CONV_WIDTH + GM_WIDTH], axis=-1)
        conv_out = conformer_conv_group(za, zg, conv_w, conv_b, conv_ln_g, conv_ln_b)
        gm_out = gmlp_group(zu, zv, gm_ln_g, gm_ln_b, gm_w_s, gm_b_s)
        h = h + jnp.concatenate([conv_out, gm_out], axis=-1) @ w_out
        h = h + cross_attention(rmsnorm(h, norm_xa_g), mn, xa_wq, xa_wkv, xa_wo)
        h = h + swiglu(rmsnorm(h, norm_ffn_g), ffn_w_gate_up, ffn_w_down)
    return rmsnorm(h, final_norm_g)


import jax as _jax
import jax.numpy as _jnp

TWIN_FORMAT = 'train_step'
FWD_PARAMS = ['x', 'mem', 'norm_mix_g', 'w_in', 'b_in', 'conv_w', 'conv_b', 'conv_ln_g', 'conv_ln_b', 'gm_ln_g', 'gm_ln_b', 'gm_w_s', 'gm_b_s', 'w_out', 'norm_xa_g', 'mem_norm_g', 'xa_wq', 'xa_wkv', 'xa_wo', 'norm_ffn_g', 'ffn_w_gate_up', 'ffn_w_down', 'final_norm_g']
TWIN_WEIGHTS = ['norm_mix_g', 'w_in', 'b_in', 'conv_w', 'conv_b', 'conv_ln_g', 'conv_ln_b', 'gm_ln_g', 'gm_ln_b', 'gm_w_s', 'gm_b_s', 'w_out', 'norm_xa_g', 'mem_norm_g', 'xa_wq', 'xa_wkv', 'xa_wo', 'norm_ffn_g', 'ffn_w_gate_up', 'ffn_w_down', 'final_norm_g']
TWIN_DIFF_INPUT = 'x'
TWIN_INPUTS = ['x', 'mem', 'norm_mix_g', 'w_in', 'b_in', 'conv_w', 'conv_b', 'conv_ln_g', 'conv_ln_b', 'gm_ln_g', 'gm_ln_b', 'gm_w_s', 'gm_b_s', 'w_out', 'norm_xa_g', 'mem_norm_g', 'xa_wq', 'xa_wkv', 'xa_wo', 'norm_ffn_g', 'ffn_w_gate_up', 'ffn_w_down', 'final_norm_g', 'loss_target', 'm_norm_mix_g', 'm_w_in', 'm_b_in', 'm_conv_w', 'm_conv_b', 'm_conv_ln_g', 'm_conv_ln_b', 'm_gm_ln_g', 'm_gm_ln_b', 'm_gm_w_s', 'm_gm_b_s', 'm_w_out', 'm_norm_xa_g', 'm_mem_norm_g', 'm_xa_wq', 'm_xa_wkv', 'm_xa_wo', 'm_norm_ffn_g', 'm_ffn_w_gate_up', 'm_ffn_w_down', 'm_final_norm_g', 'v_norm_mix_g', 'v_w_in', 'v_b_in', 'v_conv_w', 'v_conv_b', 'v_conv_ln_g', 'v_conv_ln_b', 'v_gm_ln_g', 'v_gm_ln_b', 'v_gm_w_s', 'v_gm_b_s', 'v_w_out', 'v_norm_xa_g', 'v_mem_norm_g', 'v_xa_wq', 'v_xa_wkv', 'v_xa_wo', 'v_norm_ffn_g', 'v_ffn_w_gate_up', 'v_ffn_w_down', 'v_final_norm_g']
TWIN_OUTPUTS = ['loss', 'grad_x', 'grad_norm_mix_g', 'grad_w_in', 'grad_b_in', 'grad_conv_w', 'grad_conv_b', 'grad_conv_ln_g', 'grad_conv_ln_b', 'grad_gm_ln_g', 'grad_gm_ln_b', 'grad_gm_w_s', 'grad_gm_b_s', 'grad_w_out', 'grad_norm_xa_g', 'grad_mem_norm_g', 'grad_xa_wq', 'grad_xa_wkv', 'grad_xa_wo', 'grad_norm_ffn_g', 'grad_ffn_w_gate_up', 'grad_ffn_w_down', 'grad_final_norm_g', 'delta_norm_mix_g', 'delta_w_in', 'delta_b_in', 'delta_conv_w', 'delta_conv_b', 'delta_conv_ln_g', 'delta_conv_ln_b', 'delta_gm_ln_g', 'delta_gm_ln_b', 'delta_gm_w_s', 'delta_gm_b_s', 'delta_w_out', 'delta_norm_xa_g', 'delta_mem_norm_g', 'delta_xa_wq', 'delta_xa_wkv', 'delta_xa_wo', 'delta_norm_ffn_g', 'delta_ffn_w_gate_up', 'delta_ffn_w_down', 'delta_final_norm_g', 'new_m_norm_mix_g', 'new_m_w_in', 'new_m_b_in', 'new_m_conv_w', 'new_m_conv_b', 'new_m_conv_ln_g', 'new_m_conv_ln_b', 'new_m_gm_ln_g', 'new_m_gm_ln_b', 'new_m_gm_w_s', 'new_m_gm_b_s', 'new_m_w_out', 'new_m_norm_xa_g', 'new_m_mem_norm_g', 'new_m_xa_wq', 'new_m_xa_wkv', 'new_m_xa_wo', 'new_m_norm_ffn_g', 'new_m_ffn_w_gate_up', 'new_m_ffn_w_down', 'new_m_final_norm_g', 'new_v_norm_mix_g', 'new_v_w_in', 'new_v_b_in', 'new_v_conv_w', 'new_v_conv_b', 'new_v_conv_ln_g', 'new_v_conv_ln_b', 'new_v_gm_ln_g', 'new_v_gm_ln_b', 'new_v_gm_w_s', 'new_v_gm_b_s', 'new_v_w_out', 'new_v_norm_xa_g', 'new_v_mem_norm_g', 'new_v_xa_wq', 'new_v_xa_wkv', 'new_v_xa_wo', 'new_v_norm_ffn_g', 'new_v_ffn_w_gate_up', 'new_v_ffn_w_down', 'new_v_final_norm_g']
TWIN_LEAF_KINDS = {'loss': 'loss', 'grad_x': 'grad_x', 'grad_norm_mix_g': 'grad_w', 'grad_w_in': 'grad_w', 'grad_b_in': 'grad_w', 'grad_conv_w': 'grad_w', 'grad_conv_b': 'grad_w', 'grad_conv_ln_g': 'grad_w', 'grad_conv_ln_b': 'grad_w', 'grad_gm_ln_g': 'grad_w', 'grad_gm_ln_b': 'grad_w', 'grad_gm_w_s': 'grad_w', 'grad_gm_b_s': 'grad_w', 'grad_w_out': 'grad_w', 'grad_norm_xa_g': 'grad_w', 'grad_mem_norm_g': 'grad_w', 'grad_xa_wq': 'grad_w', 'grad_xa_wkv': 'grad_w', 'grad_xa_wo': 'grad_w', 'grad_norm_ffn_g': 'grad_w', 'grad_ffn_w_gate_up': 'grad_w', 'grad_ffn_w_down': 'grad_w', 'grad_final_norm_g': 'grad_w', 'delta_norm_mix_g': 'delta_w', 'delta_w_in': 'delta_w', 'delta_b_in': 'delta_w', 'delta_conv_w': 'delta_w', 'delta_conv_b': 'delta_w', 'delta_conv_ln_g': 'delta_w', 'delta_conv_ln_b': 'delta_w', 'delta_gm_ln_g': 'delta_w', 'delta_gm_ln_b': 'delta_w', 'delta_gm_w_s': 'delta_w', 'delta_gm_b_s': 'delta_w', 'delta_w_out': 'delta_w', 'delta_norm_xa_g': 'delta_w', 'delta_mem_norm_g': 'delta_w', 'delta_xa_wq': 'delta_w', 'delta_xa_wkv': 'delta_w', 'delta_xa_wo': 'delta_w', 'delta_norm_ffn_g': 'delta_w', 'delta_ffn_w_gate_up': 'delta_w', 'delta_ffn_w_down': 'delta_w', 'delta_final_norm_g': 'delta_w', 'new_m_norm_mix_g': 'new_m', 'new_m_w_in': 'new_m', 'new_m_b_in': 'new_m', 'new_m_conv_w': 'new_m', 'new_m_conv_b': 'new_m', 'new_m_conv_ln_g': 'new_m', 'new_m_conv_ln_b': 'new_m', 'new_m_gm_ln_g': 'new_m', 'new_m_gm_ln_b': 'new_m', 'new_m_gm_w_s': 'new_m', 'new_m_gm_b_s': 'new_m', 'new_m_w_out': 'new_m', 'new_m_norm_xa_g': 'new_m', 'new_m_mem_norm_g': 'new_m', 'new_m_xa_wq': 'new_m', 'new_m_xa_wkv': 'new_m', 'new_m_xa_wo': 'new_m', 'new_m_norm_ffn_g': 'new_m', 'new_m_ffn_w_gate_up': 'new_m', 'new_m_ffn_w_down': 'new_m', 'new_m_final_norm_g': 'new_m', 'new_v_norm_mix_g': 'new_v', 'new_v_w_in': 'new_v', 'new_v_b_in': 'new_v', 'new_v_conv_w': 'new_v', 'new_v_conv_b': 'new_v', 'new_v_conv_ln_g': 'new_v', 'new_v_conv_ln_b': 'new_v', 'new_v_gm_ln_g': 'new_v', 'new_v_gm_ln_b': 'new_v', 'new_v_gm_w_s': 'new_v', 'new_v_gm_b_s': 'new_v', 'new_v_w_out': 'new_v', 'new_v_norm_xa_g': 'new_v', 'new_v_mem_norm_g': 'new_v', 'new_v_xa_wq': 'new_v', 'new_v_xa_wkv': 'new_v', 'new_v_xa_wo': 'new_v', 'new_v_norm_ffn_g': 'new_v', 'new_v_ffn_w_gate_up': 'new_v', 'new_v_ffn_w_down': 'new_v', 'new_v_final_norm_g': 'new_v'}


def _forward(args):
    return _fwd_reference(*[args[k] for k in FWD_PARAMS])


def _output_shape():
    def fwd():
        inp = _fwd_setup_inputs(0)
        return _fwd_reference(*[inp[k] for k in FWD_PARAMS])
    out = _jax.eval_shape(fwd)
    return out.shape, out.dtype

N_MICROBATCH = 1
ADAM_LR = 0.001
ADAM_B1 = 0.9
ADAM_B2 = 0.999
ADAM_EPS = 1e-08
ADAM_WD = 0.01
ADAM_STEP = 10
PER_EXAMPLE_BATCH_AXIS = {'x': 0, 'mem': 0, 'loss_target': 0}
SHARED_INPUTS = []
_WEIGHT_DTYPES = {'norm_mix_g': _jnp.float32, 'w_in': _jnp.float32, 'b_in': _jnp.float32, 'conv_w': _jnp.float32, 'conv_b': _jnp.float32, 'conv_ln_g': _jnp.float32, 'conv_ln_b': _jnp.float32, 'gm_ln_g': _jnp.float32, 'gm_ln_b': _jnp.float32, 'gm_w_s': _jnp.float32, 'gm_b_s': _jnp.float32, 'w_out': _jnp.float32, 'norm_xa_g': _jnp.float32, 'mem_norm_g': _jnp.float32, 'xa_wq': _jnp.float32, 'xa_wkv': _jnp.float32, 'xa_wo': _jnp.float32, 'norm_ffn_g': _jnp.float32, 'ffn_w_gate_up': _jnp.float32, 'ffn_w_down': _jnp.float32, 'final_norm_g': _jnp.float32}
MOMENT_SCALE = {'norm_mix_g': 1.362728e-01, 'w_in': 9.527770e-02, 'b_in': 1.574710e-01, 'conv_w': 9.987099e-02, 'conv_b': 3.523564e-01, 'conv_ln_g': 1.662046e-01, 'conv_ln_b': 2.060715e-01, 'gm_ln_g': 7.475249e-02, 'gm_ln_b': 6.941585e-02, 'gm_w_s': 5.152605e-02, 'gm_b_s': 7.604697e-02, 'w_out': 1.406925e-01, 'norm_xa_g': 1.687193e-02, 'mem_norm_g': 2.623548e-02, 'xa_wq': 1.693419e-02, 'xa_wkv': 1.773161e-02, 'xa_wo': 1.853058e-02, 'norm_ffn_g': 1.149655e-01, 'ffn_w_gate_up': 4.858282e-02, 'ffn_w_down': 8.042471e-02, 'final_norm_g': 3.209769e+01}


def _to_microbatches(a, axis):
    t = _jnp.moveaxis(a, axis, 0)
    t = t.reshape((N_MICROBATCH, t.shape[0] // N_MICROBATCH) + t.shape[1:])
    return _jnp.moveaxis(t, 1, axis + 1)


def setup_inputs(seed: int = 0) -> dict:
    inp = _fwd_setup_inputs(seed)
    key = _jax.random.fold_in(_jax.random.key(seed), 7919)
    shape, _ = _output_shape()
    out = dict(inp)
    out["loss_target"] = _jax.random.normal(_jax.random.fold_in(key, 0), shape, _jnp.float32)
    for i, name in enumerate(TWIN_WEIGHTS):
        w = inp[name].astype(_jnp.float32)
        if MOMENT_SCALE is None:
            s = _jnp.sqrt(_jnp.mean(_jnp.square(w)) + 1e-30)
        else:
            s = MOMENT_SCALE[name]
        km, kv = _jax.random.split(_jax.random.fold_in(key, i + 1))
        out[name] = w
        out["m_" + name] = s * _jax.random.normal(km, w.shape, _jnp.float32)
        out["v_" + name] = (s * s) * _jax.random.uniform(kv, w.shape, _jnp.float32, 0.5, 1.5)
    if N_MICROBATCH > 1:
        for name, axis in PER_EXAMPLE_BATCH_AXIS.items():
            out[name] = _to_microbatches(out[name], axis)
    return {'x': out['x'], 'mem': out['mem'], 'norm_mix_g': out['norm_mix_g'], 'w_in': out['w_in'], 'b_in': out['b_in'], 'conv_w': out['conv_w'], 'conv_b': out['conv_b'], 'conv_ln_g': out['conv_ln_g'], 'conv_ln_b': out['conv_ln_b'], 'gm_ln_g': out['gm_ln_g'], 'gm_ln_b': out['gm_ln_b'], 'gm_w_s': out['gm_w_s'], 'gm_b_s': out['gm_b_s'], 'w_out': out['w_out'], 'norm_xa_g': out['norm_xa_g'], 'mem_norm_g': out['mem_norm_g'], 'xa_wq': out['xa_wq'], 'xa_wkv': out['xa_wkv'], 'xa_wo': out['xa_wo'], 'norm_ffn_g': out['norm_ffn_g'], 'ffn_w_gate_up': out['ffn_w_gate_up'], 'ffn_w_down': out['ffn_w_down'], 'final_norm_g': out['final_norm_g'], 'loss_target': out['loss_target'], 'm_norm_mix_g': out['m_norm_mix_g'], 'm_w_in': out['m_w_in'], 'm_b_in': out['m_b_in'], 'm_conv_w': out['m_conv_w'], 'm_conv_b': out['m_conv_b'], 'm_conv_ln_g': out['m_conv_ln_g'], 'm_conv_ln_b': out['m_conv_ln_b'], 'm_gm_ln_g': out['m_gm_ln_g'], 'm_gm_ln_b': out['m_gm_ln_b'], 'm_gm_w_s': out['m_gm_w_s'], 'm_gm_b_s': out['m_gm_b_s'], 'm_w_out': out['m_w_out'], 'm_norm_xa_g': out['m_norm_xa_g'], 'm_mem_norm_g': out['m_mem_norm_g'], 'm_xa_wq': out['m_xa_wq'], 'm_xa_wkv': out['m_xa_wkv'], 'm_xa_wo': out['m_xa_wo'], 'm_norm_ffn_g': out['m_norm_ffn_g'], 'm_ffn_w_gate_up': out['m_ffn_w_gate_up'], 'm_ffn_w_down': out['m_ffn_w_down'], 'm_final_norm_g': out['m_final_norm_g'], 'v_norm_mix_g': out['v_norm_mix_g'], 'v_w_in': out['v_w_in'], 'v_b_in': out['v_b_in'], 'v_conv_w': out['v_conv_w'], 'v_conv_b': out['v_conv_b'], 'v_conv_ln_g': out['v_conv_ln_g'], 'v_conv_ln_b': out['v_conv_ln_b'], 'v_gm_ln_g': out['v_gm_ln_g'], 'v_gm_ln_b': out['v_gm_ln_b'], 'v_gm_w_s': out['v_gm_w_s'], 'v_gm_b_s': out['v_gm_b_s'], 'v_w_out': out['v_w_out'], 'v_norm_xa_g': out['v_norm_xa_g'], 'v_mem_norm_g': out['v_mem_norm_g'], 'v_xa_wq': out['v_xa_wq'], 'v_xa_wkv': out['v_xa_wkv'], 'v_xa_wo': out['v_xa_wo'], 'v_norm_ffn_g': out['v_norm_ffn_g'], 'v_ffn_w_gate_up': out['v_ffn_w_gate_up'], 'v_ffn_w_down': out['v_ffn_w_down'], 'v_final_norm_g': out['v_final_norm_g']}


def _loss(weights, diff, rest, loss_target):
    with _jax.named_scope("forward"):
        args = {**rest, TWIN_DIFF_INPUT: diff, **{k: w.astype(_WEIGHT_DTYPES[k]) for k, w in weights.items()}}
        y = _forward(args)
    with _jax.named_scope("loss_head"):
        err = _jnp.square(y.astype(_jnp.float32) - loss_target)
        return 0.5 * _jnp.sum(_jnp.mean(err, axis=-1)) if err.ndim else 0.5 * err


def _adamw(w, g, m, v):
    m = ADAM_B1 * m + (1.0 - ADAM_B1) * g
    v = ADAM_B2 * v + (1.0 - ADAM_B2) * _jnp.square(g)
    m_hat = m / (1.0 - ADAM_B1 ** ADAM_STEP)
    v_hat = v / (1.0 - ADAM_B2 ** ADAM_STEP)
    delta = -ADAM_LR * (m_hat / (_jnp.sqrt(v_hat) + ADAM_EPS) + ADAM_WD * w)
    return delta, m, v


def reference(x, mem, norm_mix_g, w_in, b_in, conv_w, conv_b, conv_ln_g, conv_ln_b, gm_ln_g, gm_ln_b, gm_w_s, gm_b_s, w_out, norm_xa_g, mem_norm_g, xa_wq, xa_wkv, xa_wo, norm_ffn_g, ffn_w_gate_up, ffn_w_down, final_norm_g, loss_target, m_norm_mix_g, m_w_in, m_b_in, m_conv_w, m_conv_b, m_conv_ln_g, m_conv_ln_b, m_gm_ln_g, m_gm_ln_b, m_gm_w_s, m_gm_b_s, m_w_out, m_norm_xa_g, m_mem_norm_g, m_xa_wq, m_xa_wkv, m_xa_wo, m_norm_ffn_g, m_ffn_w_gate_up, m_ffn_w_down, m_final_norm_g, v_norm_mix_g, v_w_in, v_b_in, v_conv_w, v_conv_b, v_conv_ln_g, v_conv_ln_b, v_gm_ln_g, v_gm_ln_b, v_gm_w_s, v_gm_b_s, v_w_out, v_norm_xa_g, v_mem_norm_g, v_xa_wq, v_xa_wkv, v_xa_wo, v_norm_ffn_g, v_ffn_w_gate_up, v_ffn_w_down, v_final_norm_g):
    given = dict(x=x, mem=mem, norm_mix_g=norm_mix_g, w_in=w_in, b_in=b_in, conv_w=conv_w, conv_b=conv_b, conv_ln_g=conv_ln_g, conv_ln_b=conv_ln_b, gm_ln_g=gm_ln_g, gm_ln_b=gm_ln_b, gm_w_s=gm_w_s, gm_b_s=gm_b_s, w_out=w_out, norm_xa_g=norm_xa_g, mem_norm_g=mem_norm_g, xa_wq=xa_wq, xa_wkv=xa_wkv, xa_wo=xa_wo, norm_ffn_g=norm_ffn_g, ffn_w_gate_up=ffn_w_gate_up, ffn_w_down=ffn_w_down, final_norm_g=final_norm_g, loss_target=loss_target, m_norm_mix_g=m_norm_mix_g, m_w_in=m_w_in, m_b_in=m_b_in, m_conv_w=m_conv_w, m_conv_b=m_conv_b, m_conv_ln_g=m_conv_ln_g, m_conv_ln_b=m_conv_ln_b, m_gm_ln_g=m_gm_ln_g, m_gm_ln_b=m_gm_ln_b, m_gm_w_s=m_gm_w_s, m_gm_b_s=m_gm_b_s, m_w_out=m_w_out, m_norm_xa_g=m_norm_xa_g, m_mem_norm_g=m_mem_norm_g, m_xa_wq=m_xa_wq, m_xa_wkv=m_xa_wkv, m_xa_wo=m_xa_wo, m_norm_ffn_g=m_norm_ffn_g, m_ffn_w_gate_up=m_ffn_w_gate_up, m_ffn_w_down=m_ffn_w_down, m_final_norm_g=m_final_norm_g, v_norm_mix_g=v_norm_mix_g, v_w_in=v_w_in, v_b_in=v_b_in, v_conv_w=v_conv_w, v_conv_b=v_conv_b, v_conv_ln_g=v_conv_ln_g, v_conv_ln_b=v_conv_ln_b, v_gm_ln_g=v_gm_ln_g, v_gm_ln_b=v_gm_ln_b, v_gm_w_s=v_gm_w_s, v_gm_b_s=v_gm_b_s, v_w_out=v_w_out, v_norm_xa_g=v_norm_xa_g, v_mem_norm_g=v_mem_norm_g, v_xa_wq=v_xa_wq, v_xa_wkv=v_xa_wkv, v_xa_wo=v_xa_wo, v_norm_ffn_g=v_norm_ffn_g, v_ffn_w_gate_up=v_ffn_w_gate_up, v_ffn_w_down=v_ffn_w_down, v_final_norm_g=v_final_norm_g)
    weights = {n: given[n] for n in TWIN_WEIGHTS}
    shared = {n: given[n] for n in SHARED_INPUTS}
    per_example = {n: given[n] for n in ['x', 'mem']}
    grad_fn = _jax.value_and_grad(_loss, argnums=(0, 1))

    def one_microbatch(ex, loss_target):
        ex = dict(ex)
        diff = ex.pop(TWIN_DIFF_INPUT)
        return grad_fn(weights, diff, {**shared, **ex}, loss_target)

    if N_MICROBATCH == 1:
        loss, (grad_w, grad_x) = one_microbatch(per_example, given["loss_target"])
    else:
        def body(carry, xs):
            loss_sum, grad_sum = carry
            l_k, (gw_k, gx_k) = one_microbatch(xs[0], xs[1])
            with _jax.named_scope("update"):
                return (loss_sum + l_k, _jax.tree.map(_jnp.add, grad_sum, gw_k)), gx_k

        init = (_jnp.zeros((), _jnp.float32), _jax.tree.map(_jnp.zeros_like, weights))
        (loss, grad_w), grad_x = _jax.lax.scan(body, init, (per_example, given["loss_target"]))
    with _jax.named_scope("update"):
        delta_w, new_m, new_v = {}, {}, {}
        for n in TWIN_WEIGHTS:
            delta_w[n], new_m[n], new_v[n] = _adamw(weights[n], grad_w[n], given["m_" + n], given["v_" + n])
    return (loss, grad_x, *[grad_w[n] for n in TWIN_WEIGHTS], *[delta_w[n] for n in TWIN_WEIGHTS],
            *[new_m[n] for n in TWIN_WEIGHTS], *[new_v[n] for n in TWIN_WEIGHTS])
```

```python
import functools

import jax
import jax.numpy as jnp
from jax import lax
from jax.experimental import pallas as pl
from jax.experimental.pallas import tpu as pltpu

F32 = jnp.float32
BF16 = jnp.bfloat16
SDS = jax.ShapeDtypeStruct

N_DEV = 8
RMS_EPS = 1e-6
LN_EPS = 1e-5
CONV_K = 31
CONV_PAD = 32
CHUNK = 128
GM_HEADS = 8
XA_HEADS = 4
XA_DH = 256
GELU_K0 = 0.7978845608028654
GELU_K1 = 0.044715
ADAM_LR = 0.001
ADAM_B1 = 0.9
ADAM_B2 = 0.999
ADAM_EPS = 1e-08
ADAM_WD = 0.01
ADAM_STEP = 10
VMEM_LIMIT = 60 * 1024 * 1024

NN = (((1,), (0,)), ((), ()))
NT = (((1,), (1,)), ((), ()))
TN = (((0,), (0,)), ((), ()))


def _dot(a, b, dims=NN):
    return lax.dot_general(a, b, dims, preferred_element_type=F32)


def _bf(x):
    return x.astype(BF16)


def _cparams(*sem):
    return pltpu.CompilerParams(dimension_semantics=tuple(sem) if sem else None, vmem_limit_bytes=VMEM_LIMIT)


def _row(ts, w, col=0):
    return pl.BlockSpec((ts, w), lambda i: (i, col))


def _const(shape):
    nd = len(shape)
    return pl.BlockSpec(shape, lambda i: (0,) * nd, pipeline_mode=pl.Buffered(1))


def _acc(shape):
    nd = len(shape)
    return pl.BlockSpec(shape, lambda i: (0,) * nd)


def _rms_fwd(x, g):
    r = lax.rsqrt(jnp.mean(x * x, axis=-1, keepdims=True) + RMS_EPS)
    xh = x * r
    return xh * g, xh, r


def _rms_bwd(dy, xh, r, g):
    gdy = dy * g
    dx = r * (gdy - xh * jnp.mean(gdy * xh, axis=-1, keepdims=True))
    dg = jnp.sum(dy * xh, axis=0, keepdims=True)
    return dx, dg


def _ln_fwd(x, g, b):
    mu = jnp.mean(x, axis=-1, keepdims=True)
    xc = x - mu
    rs = lax.rsqrt(jnp.mean(xc * xc, axis=-1, keepdims=True) + LN_EPS)
    xh = xc * rs
    return xh * g + b, xh, rs


def _ln_bwd(dy, xh, rs, g):
    dxh = dy * g
    dx = rs * (dxh - jnp.mean(dxh, axis=-1, keepdims=True) - xh * jnp.mean(dxh * xh, axis=-1, keepdims=True))
    return dx, jnp.sum(dy * xh, axis=0, keepdims=True), jnp.sum(dy, axis=0, keepdims=True)


def _gelu(x):
    t = jnp.tanh(GELU_K0 * (x + GELU_K1 * (x * x * x)))
    return 0.5 * x * (1.0 + t), t


def _gelu_grad(x, t):
    return 0.5 * (1.0 + t) + 0.5 * x * (1.0 - t * t) * (GELU_K0 * (1.0 + 3.0 * GELU_K1 * x * x))


def _silu_grad(x, sg):
    return sg * (1.0 + x * (1.0 - sg))


def _accumulate(ref, val):
    @pl.when(pl.program_id(0) == 0)
    def _():
        ref[...] = jnp.zeros_like(ref)
    ref[...] += val


def _mix_masks():
    row = lax.broadcasted_iota(jnp.int32, (CHUNK, CHUNK), 0)
    col = lax.broadcasted_iota(jnp.int32, (CHUNK, CHUNK), 1)
    return row >= col, row <= col, col < (CHUNK // 2)


def _mix_fwd(vb, ws_ref, bst_ref, mixed_scr, ts):
    tril, _, lo = _mix_masks()
    for j in range(GM_HEADS // 2):
        w0 = _bf(jnp.where(tril, ws_ref[2 * j], 0.0))
        w1 = _bf(jnp.where(tril, ws_ref[2 * j + 1], 0.0))
        bias = jnp.where(lo, bst_ref[:, 2 * j:2 * j + 1], bst_ref[:, 2 * j + 1:2 * j + 2])
        for n in range(ts // CHUNK):
            v = vb[n * CHUNK:(n + 1) * CHUNK, j * 128:(j + 1) * 128]
            mixed_scr[n * CHUNK:(n + 1) * CHUNK, j * 128:(j + 1) * 128] = jnp.where(lo, _dot(w0, v), _dot(w1, v)) + bias


def _exchange(srcs, scatter, name):
    n = len(srcs)

    def body(*refs):
        src_refs, out_refs = refs[:n], refs[n:2 * n]
        send_sems, recv_sems, local_sems = refs[2 * n:]
        x, y, c = lax.axis_index("x"), lax.axis_index("y"), lax.axis_index("c")
        me = 4 * x + 2 * y + c

        def peer_of(mask):
            px = x if not (mask >> 2) & 1 else 1 - x
            py = y if not (mask >> 1) & 1 else 1 - y
            pc = c if not mask & 1 else 1 - c
            return (px, py, pc), 4 * px + 2 * py + pc

        def remote(k, mask):
            peer, pidx = peer_of(mask)
            return pltpu.make_async_remote_copy(
                src_ref=src_refs[k].at[pidx] if scatter else src_refs[k],
                dst_ref=out_refs[k].at[me],
                send_sem=send_sems.at[k, mask - 1], recv_sem=recv_sems.at[k, mask - 1],
                device_id=peer, device_id_type=pl.DeviceIdType.MESH)

        def arrival(k, mask):
            peer, pidx = peer_of(mask)
            return pltpu.make_async_remote_copy(
                src_ref=src_refs[k].at[pidx] if scatter else src_refs[k],
                dst_ref=out_refs[k].at[pidx],
                send_sem=send_sems.at[k, mask - 1], recv_sem=recv_sems.at[k, mask - 1],
                device_id=peer, device_id_type=pl.DeviceIdType.MESH)

        sends, locals_ = [], []
        for k in range(n):
            for mask in range(1, N_DEV):
                cp = remote(k, mask)
                cp.start()
                sends.append(cp)
            lc = pltpu.make_async_copy(src_refs[k].at[me] if scatter else src_refs[k], out_refs[k].at[me],
                                       local_sems.at[k])
            lc.start()
            locals_.append(lc)
        for k in range(n):
            for mask in range(1, N_DEV):
                arrival(k, mask).wait_recv()
        for cp in sends:
            cp.wait_send()
        for lc in locals_:
            lc.wait()

    outs = [SDS((N_DEV,) + tuple(s.shape[1:] if scatter else s.shape), s.dtype) for s in srcs]
    hbm = pl.BlockSpec(memory_space=pl.ANY)
    return pl.pallas_call(
        body, name=name, out_shape=outs, in_specs=[hbm] * n, out_specs=[hbm] * n,
        scratch_shapes=[pltpu.SemaphoreType.DMA((n, N_DEV - 1)), pltpu.SemaphoreType.DMA((n, N_DEV - 1)),
                        pltpu.SemaphoreType.DMA((n,))],
    )(*srcs)


def _kv_proj(mem, g_mem, wkv3):
    m = mem.shape[0]

    def body(mem_ref, g_ref, w_ref, kv_ref, mn_ref):
        y, _, _ = _rms_fwd(mem_ref[...], g_ref[...])
        yb = _bf(y)
        mn_ref[...] = yb
        for b in range(N_DEV):
            kv_ref[:, 256 * b:256 * (b + 1)] = _dot(yb, w_ref[b])

    return pl.pallas_call(body, name="kv_proj", out_shape=(SDS((m, 2048), F32), SDS(mem.shape, BF16)),
                          compiler_params=_cparams())(mem, g_mem, wkv3)


def _fwd_in(x, g_mix, w_in3, b_in, ts):
    s, d = x.shape

    def body(x_ref, g_ref, w_ref, b_ref, z_ref, a_ref):
        hn, _, _ = _rms_fwd(x_ref[...], g_ref[...])
        hb = _bf(hn)
        for b in range(N_DEV):
            z_ref[:, 256 * b:256 * (b + 1)] = _dot(hb, w_ref[b]) + b_ref[:, 256 * b:256 * (b + 1)]
        a_ref[...] = z_ref[:, 0:512] * jax.nn.sigmoid(z_ref[:, 512:1024])

    return pl.pallas_call(
        body, name="fwd_in", grid=(s // ts,),
        in_specs=[_row(ts, d), _const(g_mix.shape), _const(w_in3.shape), _const(b_in.shape)],
        out_specs=(_row(ts, 2048), _row(ts, 512)),
        out_shape=(SDS((s, 2048), F32), SDS((s, 512), F32)),
        compiler_params=_cparams("arbitrary"))(x, g_mix, w_in3, b_in)


def _conv_fwd(a, w, b):
    s, cw = a.shape
    rc = 256 if s % 256 == 0 else 128

    def body(a_ref, w_ref, b_ref, c_ref, pad):
        pad[0:CONV_PAD, :] = jnp.zeros((CONV_PAD, 128), F32)
        pad[CONV_PAD:, :] = a_ref[...]

        def chunk(i, carry):
            r0 = pl.multiple_of(i * rc, rc)
            win = pad[pl.ds(r0, rc + CONV_PAD), :]
            acc = jnp.zeros((rc, 128), F32) + b_ref[...]
            for k in range(CONV_K):
                off = CONV_PAD - (CONV_K - 1) + k
                acc = acc + w_ref[k:k + 1, :] * win[off:off + rc, :]
            c_ref[pl.ds(r0, rc), :] = acc
            return carry

        lax.fori_loop(0, s // rc, chunk, 0)

    blk = lambda r: pl.BlockSpec((r, 128), lambda j: (0, j))
    return pl.pallas_call(
        body, name="conv_fwd", grid=(cw // 128,),
        in_specs=[blk(s), blk(CONV_K), blk(1)], out_specs=blk(s), out_shape=SDS((s, cw), F32),
        scratch_shapes=[pltpu.VMEM((s + CONV_PAD, 128), F32)],
        compiler_params=_cparams("arbitrary"))(a, w, b)


def _fwd_out(x, c, z, cln_g, cln_b, gln_g, gln_b, ws, bst, w_out, ts):
    s, d = x.shape

    def body(x_ref, c_ref, zuv_ref, clg, clb, glg, glb, ws_ref, bst_ref, wo_ref, h1_ref, mixed_scr):
        cl, _, _ = _ln_fwd(c_ref[...], clg[...], clb[...])
        co = cl * jax.nn.sigmoid(cl)
        u, _ = _gelu(zuv_ref[:, 0:512])
        vg, _ = _gelu(zuv_ref[:, 512:1024])
        vln, _, _ = _ln_fwd(vg, glg[...], glb[...])
        _mix_fwd(_bf(vln), ws_ref, bst_ref, mixed_scr, ts)
        gm = u * mixed_scr[...]
        h1_ref[...] = x_ref[...] + _dot(_bf(co), wo_ref[0:512, :]) + _dot(_bf(gm), wo_ref[512:1024, :])

    return pl.pallas_call(
        body, name="fwd_out", grid=(s // ts,),
        in_specs=[_row(ts, d), _row(ts, 512), _row(ts, 1024, 1), _const(cln_g.shape), _const(cln_b.shape),
                  _const(gln_g.shape), _const(gln_b.shape), _const(ws.shape), _const(bst.shape), _const(w_out.shape)],
        out_specs=_row(ts, d), out_shape=SDS((s, d), F32),
        scratch_shapes=[pltpu.VMEM((ts, 512), F32)],
        compiler_params=_cparams("arbitrary"))(x, c, z, cln_g, cln_b, gln_g, gln_b, ws, bst, w_out)


def _softmax_rows(sc):
    m = jnp.max(sc, axis=-1, keepdims=True)
    e = jnp.exp(sc - m)
    return e / jnp.sum(e, axis=-1, keepdims=True)


def _fwd_xa(h1, g_xa, wq, kv, wo, ts):
    s, d = h1.shape
    scale = XA_DH ** -0.5

    def body(h_ref, g_ref, wq_ref, kv_ref, wo_ref, h2_ref, o_scr):
        hn, _, _ = _rms_fwd(h_ref[...], g_ref[...])
        q = _dot(_bf(hn), wq_ref[...])
        for h in range(XA_HEADS):
            qh = _bf(q[:, XA_DH * h:XA_DH * (h + 1)])
            kh = _bf(kv_ref[:, XA_DH * h:XA_DH * (h + 1)])
            vh = _bf(kv_ref[:, d + XA_DH * h:d + XA_DH * (h + 1)])
            p = _softmax_rows(_dot(qh, kh, NT) * scale)
            o_scr[:, XA_DH * h:XA_DH * (h + 1)] = _dot(_bf(p), vh)
        h2_ref[...] = h_ref[...] + _dot(_bf(o_scr[...]), wo_ref[...])

    return pl.pallas_call(
        body, name="fwd_xa", grid=(s // ts,),
        in_specs=[_row(ts, d), _const(g_xa.shape), _const(wq.shape), _const(kv.shape), _const(wo.shape)],
        out_specs=_row(ts, d), out_shape=SDS((s, d), F32),
        scratch_shapes=[pltpu.VMEM((ts, d), F32)],
        compiler_params=_cparams("arbitrary"))(h1, g_xa, wq, kv, wo)


def _fwd_ffn(h2, g_ffn, wgut, wdown, g_final, target, ts):
    s, d = h2.shape
    hid = wdown.shape[0]
    hc = hid // 2

    def body(h_ref, g_ref, wgu_ref, wd_ref, gf_ref, t_ref, dh3_ref, loss_ref, dgf_ref):
        hn, _, _ = _rms_fwd(h_ref[...], g_ref[...])
        hb = _bf(hn)
        h3 = h_ref[...]
        for n in range(2):
            g = _dot(hb, wgu_ref[hc * n:hc * (n + 1), :], NT)
            u = _dot(hb, wgu_ref[hid + hc * n:hid + hc * (n + 1), :], NT)
            act = g * jax.nn.sigmoid(g) * u
            h3 = h3 + _dot(_bf(act), wd_ref[hc * n:hc * (n + 1), :])
        y, xh, r = _rms_fwd(h3, gf_ref[...])
        diff = y - t_ref[...]
        part = 0.5 * jnp.sum(jnp.mean(diff * diff, axis=-1, keepdims=True), axis=0, keepdims=True)
        _accumulate(loss_ref, jnp.zeros(loss_ref.shape, F32) + part)
        dh3, dgf = _rms_bwd(diff * (1.0 / d), xh, r, gf_ref[...])
        dh3_ref[...] = dh3
        _accumulate(dgf_ref, dgf)

    return pl.pallas_call(
        body, name="fwd_ffn", grid=(s // ts,),
        in_specs=[_row(ts, d), _const(g_ffn.shape), _const(wgut.shape), _const(wdown.shape), _const(g_final.shape),
                  _row(ts, d)],
        out_specs=(_row(ts, d), _acc((1, 128)), _acc((1, d))),
        out_shape=(SDS((s, d), F32), SDS((1, 128), F32), SDS((1, d), F32)),
        compiler_params=_cparams("arbitrary"))(h2, g_ffn, wgut, wdown, g_final, target)


def _bwd_ffn(h2, dh3, g_ffn, wgut, wdown, ts):
    s, d = h2.shape
    hid = wdown.shape[0]
    hc = hid // 2

    def body(h_ref, dh3_ref, g_ref, wgu_ref, wd_ref, dh2_ref, act_ref, dgu_ref, hn_ref, dg_ref):
        hn, xh, r = _rms_fwd(h_ref[...], g_ref[...])
        hb = _bf(hn)
        hn_ref[...] = hb
        db = _bf(dh3_ref[...])
        dhn = jnp.zeros((ts, d), F32)
        for n in range(2):
            wg = wgu_ref[hc * n:hc * (n + 1), :]
            wu = wgu_ref[hid + hc * n:hid + hc * (n + 1), :]
            g = _dot(hb, wg, NT)
            u = _dot(hb, wu, NT)
            sg = jax.nn.sigmoid(g)
            sl = g * sg
            act_ref[:, hc * n:hc * (n + 1)] = _bf(sl * u)
            dact = _dot(db, wd_ref[hc * n:hc * (n + 1), :], NT)
            dgb = _bf(dact * u * _silu_grad(g, sg))
            dub = _bf(dact * sl)
            dgu_ref[:, hc * n:hc * (n + 1)] = dgb
            dgu_ref[:, hid + hc * n:hid + hc * (n + 1)] = dub
            dhn = dhn + _dot(dgb, wg) + _dot(dub, wu)
        dx, dg = _rms_bwd(dhn, xh, r, g_ref[...])
        dh2_ref[...] = dh3_ref[...] + dx
        _accumulate(dg_ref, dg)

    return pl.pallas_call(
        body, name="bwd_ffn", grid=(s // ts,),
        in_specs=[_row(ts, d), _row(ts, d), _const(g_ffn.shape), _const(wgut.shape), _const(wdown.shape)],
        out_specs=(_row(ts, d), _row(ts, hid), _row(ts, 2 * hid), _row(ts, d), _acc((1, d))),
        out_shape=(SDS((s, d), F32), SDS((s, hid), BF16), SDS((s, 2 * hid), BF16), SDS((s, d), BF16),
                   SDS((1, d), F32)),
        compiler_params=_cparams("arbitrary"))(h2, dh3, g_ffn, wgut, wdown)


def _bwd_xa(h1, dh2, g_xa, wq, wo, kv, ts):
    s, d = h1.shape
    scale = XA_DH ** -0.5

    def body(h_ref, dh2_ref, g_ref, wq_ref, wo_ref, kv_ref, dh1_ref, dq_ref, o_ref, hn_ref, dkv_ref, dg_ref,
             dq_scr):
        hn, xh, r = _rms_fwd(h_ref[...], g_ref[...])
        hb = _bf(hn)
        hn_ref[...] = hb
        q = _dot(hb, wq_ref[...])
        do = _dot(_bf(dh2_ref[...]), wo_ref[...], NT)

        @pl.when(pl.program_id(0) == 0)
        def _():
            dkv_ref[...] = jnp.zeros_like(dkv_ref)

        for h in range(XA_HEADS):
            lo, hi = XA_DH * h, XA_DH * (h + 1)
            qh = _bf(q[:, lo:hi])
            kh = _bf(kv_ref[:, lo:hi])
            vh = _bf(kv_ref[:, d + lo:d + hi])
            p = _softmax_rows(_dot(qh, kh, NT) * scale)
            pb = _bf(p)
            o_ref[:, lo:hi] = _bf(_dot(pb, vh))
            doh = _bf(do[:, lo:hi])
            dp = _dot(doh, vh, NT)
            ds = p * (dp - jnp.sum(p * dp, axis=-1, keepdims=True)) * scale
            dsb = _bf(ds)
            dq_scr[:, lo:hi] = _dot(dsb, kh)
            dkv_ref[:, lo:hi] += _dot(dsb, qh, TN)
            dkv_ref[:, d + lo:d + hi] += _dot(pb, doh, TN)
        dqb = _bf(dq_scr[...])
        dq_ref[...] = dqb
        dx, dg = _rms_bwd(_dot(dqb, wq_ref[...], NT), xh, r, g_ref[...])
        dh1_ref[...] = dh2_ref[...] + dx
        _accumulate(dg_ref, dg)

    return pl.pallas_call(
        body, name="bwd_xa", grid=(s // ts,),
        in_specs=[_row(ts, d), _row(ts, d), _const(g_xa.shape), _const(wq.shape), _const(wo.shape), _const(kv.shape)],
        out_specs=(_row(ts, d), _row(ts, d), _row(ts, d), _row(ts, d), _acc(kv.shape), _acc((1, d))),
        out_shape=(SDS((s, d), F32), SDS((s, d), BF16), SDS((s, d), BF16), SDS((s, d), BF16), SDS(kv.shape, F32),
                   SDS((1, d), F32)),
        scratch_shapes=[pltpu.VMEM((ts, d), F32)],
        compiler_params=_cparams("arbitrary"))(h1, dh2, g_xa, wq, wo, kv)


def _bwd_kv(dkv, mn, mem, g_mem, wkv3):
    d = mem.shape[1]

    def body(dkv_ref, mn_ref, mem_ref, g_ref, w_ref, dw_ref, dg_ref):
        dkvb = _bf(dkv_ref[...])
        dmn = jnp.zeros(mem_ref.shape, F32)
        for b in range(N_DEV):
            blk = dkvb[:, 256 * b:256 * (b + 1)]
            dmn = dmn + _dot(blk, w_ref[b], NT)
            dw_ref[b] = _bf(_dot(mn_ref[...], blk, TN))
        _, xh, r = _rms_fwd(mem_ref[...], g_ref[...])
        _, dg = _rms_bwd(dmn, xh, r, g_ref[...])
        dg_ref[...] = dg

    return pl.pallas_call(body, name="bwd_kv", out_shape=(SDS(wkv3.shape, BF16), SDS((1, d), F32)),
                          compiler_params=_cparams())(dkv, mn, mem, g_mem, wkv3)


def _bwd_out(dh1, c, z, cln_g, cln_b, gln_g, gln_b, ws, wst, bst, w_out, ts):
    s, d = dh1.shape
    nh = GM_HEADS

    def body(dh1_ref, c_ref, zuv_ref, clg, clb, glg, glb, ws_ref, wst_ref, bst_ref, wo_ref,
             cat_ref, dc_ref, dzuv_ref, dws_ref, dbst_ref, dclg_ref, dclb_ref, dglg_ref, dglb_ref, dbin_ref,
             mixed_scr, dv_scr):
        cl, chat, crs = _ln_fwd(c_ref[...], clg[...], clb[...])
        sg = jax.nn.sigmoid(cl)
        zu = zuv_ref[:, 0:512]
        zv = zuv_ref[:, 512:1024]
        u, tu = _gelu(zu)
        vg, tv = _gelu(zv)
        vln, vhat, vrs = _ln_fwd(vg, glg[...], glb[...])
        vb = _bf(vln)
        _mix_fwd(vb, ws_ref, bst_ref, mixed_scr, ts)
        mixed = mixed_scr[...]
        cat_ref[:, 0:512] = _bf(cl * sg)
        cat_ref[:, 512:1024] = _bf(u * mixed)
        dcat = _dot(_bf(dh1_ref[...]), wo_ref[...], NT)
        dgm = dcat[:, 512:1024]
        dc, dclg, dclb = _ln_bwd(dcat[:, 0:512] * _silu_grad(cl, sg), chat, crs, clg[...])
        dc_ref[...] = dc
        dzu = dgm * mixed * _gelu_grad(zu, tu)
        dm = dgm * u

        @pl.when(pl.program_id(0) == 0)
        def _():
            dws_ref[...] = jnp.zeros_like(dws_ref)
            dbst_ref[...] = jnp.zeros_like(dbst_ref)

        tril, triu, lo = _mix_masks()
        head = lax.broadcasted_iota(jnp.int32, (1, nh), 1)
        for j in range(nh // 2):
            w0t = _bf(jnp.where(triu, wst_ref[2 * j], 0.0))
            w1t = _bf(jnp.where(triu, wst_ref[2 * j + 1], 0.0))
            for n in range(ts // CHUNK):
                rows = slice(n * CHUNK, (n + 1) * CHUNK)
                lanes = slice(j * 128, (j + 1) * 128)
                dmc = dm[rows, lanes]
                dmb = _bf(dmc)
                dv_scr[rows, lanes] = jnp.where(lo, _dot(w0t, dmb), _dot(w1t, dmb))
                vc = vb[rows, lanes]
                d0 = jnp.where(lo, dmc, 0.0)
                d1 = dmc - d0
                dws_ref[2 * j] += jnp.where(tril, _dot(_bf(d0), vc, NT), 0.0)
                dws_ref[2 * j + 1] += jnp.where(tril, _dot(_bf(d1), vc, NT), 0.0)
                dbst_ref[...] += (jnp.sum(d0, axis=1, keepdims=True) * (head == 2 * j).astype(F32)
                                  + jnp.sum(d1, axis=1, keepdims=True) * (head == 2 * j + 1).astype(F32))
        dvg, dglg, dglb = _ln_bwd(dv_scr[...], vhat, vrs, glg[...])
        dzv = dvg * _gelu_grad(zv, tv)
        dzuv_ref[:, 0:512] = _bf(dzu)
        dzuv_ref[:, 512:1024] = _bf(dzv)
        _accumulate(dclg_ref, dclg)
        _accumulate(dclb_ref, dclb)
        _accumulate(dglg_ref, dglg)
        _accumulate(dglb_ref, dglb)
        _accumulate(dbin_ref, jnp.concatenate([jnp.sum(dzu, axis=0, keepdims=True),
                                               jnp.sum(dzv, axis=0, keepdims=True)], axis=1))

    vec = (1, 512)
    return pl.pallas_call(
        body, name="bwd_out", grid=(s // ts,),
        in_specs=[_row(ts, d), _row(ts, 512), _row(ts, 1024, 1), _const(cln_g.shape), _const(cln_b.shape),
                  _const(gln_g.shape), _const(gln_b.shape), _const(ws.shape), _const(wst.shape), _const(bst.shape),
                  _const(w_out.shape)],
        out_specs=(_row(ts, d), _row(ts, 512), _row(ts, 1024), _acc(ws.shape), _acc(bst.shape), _acc(vec), _acc(vec),
                   _acc(vec), _acc(vec), _acc((1, 1024))),
        out_shape=(SDS((s, d), BF16), SDS((s, 512), F32), SDS((s, 1024), BF16), SDS(ws.shape, F32),
                   SDS(bst.shape, F32), SDS(vec, F32), SDS(vec, F32), SDS(vec, F32), SDS(vec, F32), SDS((1, 1024), F32)),
        scratch_shapes=[pltpu.VMEM((ts, 512), F32), pltpu.VMEM((ts, 512), F32)],
        compiler_params=_cparams("arbitrary"))(dh1, c, z, cln_g, cln_b, gln_g, gln_b, ws, wst, bst, w_out)


def _conv_bwd(dc, a, w):
    s, cw = a.shape
    rc = 256 if s % 256 == 0 else 128

    def body(dc_ref, a_ref, w_ref, da_ref, dw_ref, db_ref, pad_a, pad_d):
        pad_a[0:CONV_PAD, :] = jnp.zeros((CONV_PAD, 128), F32)
        pad_a[CONV_PAD:, :] = a_ref[...]
        pad_d[0:s, :] = dc_ref[...]
        pad_d[s:, :] = jnp.zeros((CONV_PAD, 128), F32)
        dw_ref[...] = jnp.zeros_like(dw_ref)
        db_ref[...] = jnp.zeros_like(db_ref)

        def chunk(i, carry):
            r0 = pl.multiple_of(i * rc, rc)
            win_d = pad_d[pl.ds(r0, rc + CONV_PAD), :]
            win_a = pad_a[pl.ds(r0, rc + CONV_PAD), :]
            dcc = win_d[0:rc, :]
            acc = jnp.zeros((rc, 128), F32)
            for k in range(CONV_K):
                acc = acc + w_ref[k:k + 1, :] * win_d[CONV_K - 1 - k:CONV_K - 1 - k + rc, :]
                off = CONV_PAD - (CONV_K - 1) + k
                dw_ref[k:k + 1, :] += jnp.sum(dcc * win_a[off:off + rc, :], axis=0, keepdims=True)
            da_ref[pl.ds(r0, rc), :] = acc
            db_ref[...] += jnp.sum(dcc, axis=0, keepdims=True)
            return carry

        lax.fori_loop(0, s // rc, chunk, 0)

    blk = lambda r: pl.BlockSpec((r, 128), lambda j: (0, j))
    return pl.pallas_call(
        body, name="conv_bwd", grid=(cw // 128,),
        in_specs=[blk(s), blk(s), blk(CONV_K)], out_specs=(blk(s), blk(CONV_PAD), blk(1)),
        out_shape=(SDS((s, cw), F32), SDS((CONV_PAD, cw), F32), SDS((1, cw), F32)),
        scratch_shapes=[pltpu.VMEM((s + CONV_PAD, 128), F32), pltpu.VMEM((s + CONV_PAD, 128), F32)],
        compiler_params=_cparams("arbitrary"))(dc, a, w)


def _bwd_in(x, dh1, da, z, dzuv, g_mix, w_in3, ts):
    s, d = x.shape

    def body(x_ref, dh1_ref, da_ref, zag_ref, dzuv_ref, g_ref, w_ref, dx_ref, dz_ref, hn_ref, dbin_ref, dg_ref):
        za = zag_ref[:, 0:512]
        sg = jax.nn.sigmoid(zag_ref[:, 512:1024])
        da_ = da_ref[...]
        dza = da_ * sg
        dzg = da_ * za * sg * (1.0 - sg)
        dz_ref[:, 0:512] = _bf(dza)
        dz_ref[:, 512:1024] = _bf(dzg)
        dz_ref[:, 1024:2048] = dzuv_ref[...]
        dhn = jnp.zeros((ts, d), F32)
        for b in range(N_DEV):
            dhn = dhn + _dot(dz_ref[:, 256 * b:256 * (b + 1)], w_ref[b], NT)
        hn, xh, r = _rms_fwd(x_ref[...], g_ref[...])
        hn_ref[...] = _bf(hn)
        dxn, dg = _rms_bwd(dhn, xh, r, g_ref[...])
        dx_ref[...] = dh1_ref[...] + dxn
        _accumulate(dg_ref, dg)
        _accumulate(dbin_ref, jnp.concatenate([jnp.sum(dza, axis=0, keepdims=True),
                                               jnp.sum(dzg, axis=0, keepdims=True)], axis=1))

    return pl.pallas_call(
        body, name="bwd_in", grid=(s // ts,),
        in_specs=[_row(ts, d), _row(ts, d), _row(ts, 512), _row(ts, 1024, 0), _row(ts, 1024), _const(g_mix.shape),
                  _const(w_in3.shape)],
        out_specs=(_row(ts, d), _row(ts, 2048), _row(ts, d), _acc((1, 1024)), _acc((1, d))),
        out_shape=(SDS((s, d), F32), SDS((s, 2048), BF16), SDS((s, d), BF16), SDS((1, 1024), F32), SDS((1, d), F32)),
        compiler_params=_cparams("arbitrary"))(x, dh1, da, z, dzuv, g_mix, w_in3)


def _tn_matmul(a, b, tm, name, col_blocks=False):
    s, m = a.shape
    n = b.shape[1]
    tn = 256 if col_blocks else n
    if col_blocks:
        tm = m
    ts = min(s, 512)
    n_s = s // ts

    def body(a_ref, b_ref, o_ref, acc):
        k = pl.program_id(2)

        @pl.when(k == 0)
        def _():
            acc[...] = jnp.zeros_like(acc)

        acc[...] += _dot(_bf(a_ref[...]), _bf(b_ref[...]), TN)

        @pl.when(k == n_s - 1)
        def _():
            if col_blocks:
                o_ref[0] = _bf(acc[...])
            else:
                o_ref[...] = _bf(acc[...])

    if col_blocks:
        out_shape = SDS((n // tn, m, tn), BF16)
        out_spec = pl.BlockSpec((1, m, tn), lambda i, j, k: (j, 0, 0))
    else:
        out_shape = SDS((m, n), BF16)
        out_spec = pl.BlockSpec((tm, tn), lambda i, j, k: (i, j))
    return pl.pallas_call(
        body, name=name, grid=(m // tm, n // tn, n_s),
        in_specs=[pl.BlockSpec((ts, tm), lambda i, j, k: (k, i)), pl.BlockSpec((ts, tn), lambda i, j, k: (k, j))],
        out_specs=out_spec, out_shape=out_shape,
        scratch_shapes=[pltpu.VMEM((tm, tn), F32)],
        compiler_params=_cparams("parallel", "parallel", "arbitrary"))(a, b)


def _sum8(parts, name):
    _, r, c = parts.shape
    tr = r
    for cand in (512, 256, 352, 128, 8):
        if r % cand == 0 and cand <= r:
            tr = cand
            break

    def body(p_ref, o_ref):
        acc = p_ref[0].astype(F32)
        for dev in range(1, N_DEV):
            acc = acc + p_ref[dev].astype(F32)
        o_ref[...] = acc

    return pl.pallas_call(
        body, name=name, grid=(r // tr,),
        in_specs=[pl.BlockSpec((N_DEV, tr, c), lambda i: (0, i, 0))],
        out_specs=pl.BlockSpec((tr, c), lambda i: (i, 0)), out_shape=SDS((r, c), F32),
        compiler_params=_cparams("parallel"))(parts)


def _adamw(w, g, m, v, name):
    r, c = w.shape
    tr = r
    for cand in (256, 128, 176, 8):
        if r % cand == 0 and cand <= r:
            tr = cand
            break

    def body(w_ref, g_ref, m_ref, v_ref, d_ref, m2_ref, v2_ref):
        g_ = g_ref[...]
        m2 = ADAM_B1 * m_ref[...] + (1.0 - ADAM_B1) * g_
        v2 = ADAM_B2 * v_ref[...] + (1.0 - ADAM_B2) * (g_ * g_)
        m_hat = m2 / (1.0 - ADAM_B1 ** ADAM_STEP)
        v_hat = v2 / (1.0 - ADAM_B2 ** ADAM_STEP)
        d_ref[...] = -ADAM_LR * (m_hat / (jnp.sqrt(v_hat) + ADAM_EPS) + ADAM_WD * w_ref[...])
        m2_ref[...] = m2
        v2_ref[...] = v2

    blk = pl.BlockSpec((tr, c), lambda i: (i, 0))
    return pl.pallas_call(
        body, name=name, grid=(r // tr,), in_specs=[blk] * 4, out_specs=(blk,) * 3,
        out_shape=(SDS((r, c), F32),) * 3, compiler_params=_cparams("parallel"))(w, g, m, v)


def _pack(arrs):
    flat = jnp.concatenate([a.reshape(-1) for a in arrs])
    pad = (-flat.shape[0]) % 1024
    return jnp.pad(flat, (0, pad)).reshape(-1, 128)


def _unpack(packed, shapes):
    flat = packed.reshape(-1)
    out, off = [], 0
    for shp in shapes:
        size = 1
        for dim in shp:
            size *= dim
        out.append(flat[off:off + size].reshape(shp))
        off += size
    return out


SMALL = ("norm_mix_g", "b_in", "conv_w", "conv_b", "conv_ln_g", "conv_ln_b", "gm_ln_g", "gm_ln_b", "gm_w_s", "gm_b_s",
         "norm_xa_g", "mem_norm_g", "norm_ffn_g", "final_norm_g")
BIG = ("w_in", "w_out", "xa_wq", "xa_wkv", "xa_wo", "ffn_w_gate_up", "ffn_w_down")
WEIGHTS = ("norm_mix_g", "w_in", "b_in", "conv_w", "conv_b", "conv_ln_g", "conv_ln_b", "gm_ln_g", "gm_ln_b", "gm_w_s",
           "gm_b_s", "w_out", "norm_xa_g", "mem_norm_g", "xa_wq", "xa_wkv", "xa_wo", "norm_ffn_g", "ffn_w_gate_up",
           "ffn_w_down", "final_norm_g")


def kernel(x, mem, norm_mix_g, w_in, b_in, conv_w, conv_b, conv_ln_g, conv_ln_b, gm_ln_g, gm_ln_b, gm_w_s, gm_b_s, w_out, norm_xa_g, mem_norm_g, xa_wq, xa_wkv, xa_wo, norm_ffn_g, ffn_w_gate_up, ffn_w_down, final_norm_g, loss_target, m_norm_mix_g, m_w_in, m_b_in, m_conv_w, m_conv_b, m_conv_ln_g, m_conv_ln_b, m_gm_ln_g, m_gm_ln_b, m_gm_w_s, m_gm_b_s, m_w_out, m_norm_xa_g, m_mem_norm_g, m_xa_wq, m_xa_wkv, m_xa_wo, m_norm_ffn_g, m_ffn_w_gate_up, m_ffn_w_down, m_final_norm_g, v_norm_mix_g, v_w_in, v_b_in, v_conv_w, v_conv_b, v_conv_ln_g, v_conv_ln_b, v_gm_ln_g, v_gm_ln_b, v_gm_w_s, v_gm_b_s, v_w_out, v_norm_xa_g, v_mem_norm_g, v_xa_wq, v_xa_wkv, v_xa_wo, v_norm_ffn_g, v_ffn_w_gate_up, v_ffn_w_down, v_final_norm_g):
    w = dict(norm_mix_g=norm_mix_g, w_in=w_in, b_in=b_in, conv_w=conv_w, conv_b=conv_b, conv_ln_g=conv_ln_g,
             conv_ln_b=conv_ln_b, gm_ln_g=gm_ln_g, gm_ln_b=gm_ln_b, gm_w_s=gm_w_s, gm_b_s=gm_b_s, w_out=w_out,
             norm_xa_g=norm_xa_g, mem_norm_g=mem_norm_g, xa_wq=xa_wq, xa_wkv=xa_wkv, xa_wo=xa_wo,
             norm_ffn_g=norm_ffn_g, ffn_w_gate_up=ffn_w_gate_up, ffn_w_down=ffn_w_down, final_norm_g=final_norm_g)
    mom = dict(norm_mix_g=m_norm_mix_g, w_in=m_w_in, b_in=m_b_in, conv_w=m_conv_w, conv_b=m_conv_b,
               conv_ln_g=m_conv_ln_g, conv_ln_b=m_conv_ln_b, gm_ln_g=m_gm_ln_g, gm_ln_b=m_gm_ln_b, gm_w_s=m_gm_w_s,
               gm_b_s=m_gm_b_s, w_out=m_w_out, norm_xa_g=m_norm_xa_g, mem_norm_g=m_mem_norm_g, xa_wq=m_xa_wq,
               xa_wkv=m_xa_wkv, xa_wo=m_xa_wo, norm_ffn_g=m_norm_ffn_g, ffn_w_gate_up=m_ffn_w_gate_up,
               ffn_w_down=m_ffn_w_down, final_norm_g=m_final_norm_g)
    var = dict(norm_mix_g=v_norm_mix_g, w_in=v_w_in, b_in=v_b_in, conv_w=v_conv_w, conv_b=v_conv_b,
               conv_ln_g=v_conv_ln_g, conv_ln_b=v_conv_ln_b, gm_ln_g=v_gm_ln_g, gm_ln_b=v_gm_ln_b, gm_w_s=v_gm_w_s,
               gm_b_s=v_gm_b_s, w_out=v_w_out, norm_xa_g=v_norm_xa_g, mem_norm_g=v_mem_norm_g, xa_wq=v_xa_wq,
               xa_wkv=v_xa_wkv, xa_wo=v_xa_wo, norm_ffn_g=v_norm_ffn_g, ffn_w_gate_up=v_ffn_w_gate_up,
               ffn_w_down=v_ffn_w_down, final_norm_g=v_final_norm_g)

    me = 4 * lax.axis_index("x") + 2 * lax.axis_index("y") + lax.axis_index("c")
    s, d = x.shape[1], x.shape[2]
    xs = x.reshape(s, d)
    mems = mem.reshape(mem.shape[1], d)
    tgt = loss_target.reshape(s, d)
    ts = min(512, s)
    ts_ffn = min(256, s)
    row = lambda a: a.reshape(1, -1)

    conv_w_pad = jnp.pad(conv_w, ((0, CONV_PAD - CONV_K), (0, 128 - conv_w.shape[1])))
    shards = [_bf(w_in), _bf(w_out), _bf(xa_wq), _bf(xa_wkv), _bf(xa_wo), _bf(ffn_w_gate_up.T), _bf(ffn_w_down),
              conv_w_pad]
    w_in3, w_out3, wq3, wkv3, wo3, wgut3, wdown3, conv_w8 = _exchange(shards, False, "gather_weights")
    w_out_f = w_out3.reshape(-1, d)
    wq_f = wq3.reshape(-1, d)
    wo_f = wo3.reshape(-1, d)
    wgut_f = wgut3.reshape(-1, d)
    wdown_f = wdown3.reshape(-1, d)
    conv_w_f = conv_w8[:, :CONV_K, :conv_w.shape[1]].transpose(1, 0, 2).reshape(CONV_K, -1)
    bst = gm_b_s.T
    wst = jnp.swapaxes(gm_w_s, 1, 2)
    cw = conv_w_f.shape[1]

    kv, mn = _kv_proj(mems, row(mem_norm_g), wkv3)
    z, a = _fwd_in(xs, row(norm_mix_g), w_in3, row(b_in), ts)
    c = _conv_fwd(a, conv_w_f, row(conv_b))
    h1 = _fwd_out(xs, c, z, row(conv_ln_g), row(conv_ln_b), row(gm_ln_g), row(gm_ln_b), gm_w_s, bst, w_out_f, ts)
    h2 = _fwd_xa(h1, row(norm_xa_g), wq_f, kv, wo_f, ts)
    dh3, loss_p, d_final_g = _fwd_ffn(h2, row(norm_ffn_g), wgut_f, wdown_f, row(final_norm_g), tgt, ts_ffn)

    dh2, act, dgu, hn2, d_ffn_g = _bwd_ffn(h2, dh3, row(norm_ffn_g), wgut_f, wdown_f, ts_ffn)
    dwgut = _tn_matmul(dgu, hn2, 512, "dw_gate_up")
    dwdown = _tn_matmul(act, dh3, 256, "dw_down")
    dh1, dq, o, hn1, dkv, d_xa_g = _bwd_xa(h1, dh2, row(norm_xa_g), wq_f, wo_f, kv, ts)
    dwq = _tn_matmul(hn1, dq, 256, "dw_q")
    dwo = _tn_matmul(o, dh2, 256, "dw_o")
    dwkv3, d_mem_g = _bwd_kv(dkv, mn, mems, row(mem_norm_g), wkv3)
    (cat, dc, dzuv, dws, dbst, d_cln_g, d_cln_b, d_gln_g, d_gln_b, dbin_uv) = _bwd_out(
        dh1, c, z, row(conv_ln_g), row(conv_ln_b), row(gm_ln_g), row(gm_ln_b), gm_w_s, wst, bst, w_out_f, ts)
    dwout = _tn_matmul(cat, dh1, 256, "dw_out")
    da, dconv_w, dconv_b = _conv_bwd(dc, a, conv_w_f)
    dx, dz, hn0, dbin_ag, d_mix_g = _bwd_in(xs, dh1, da, z, dzuv, row(norm_mix_g), w_in3, ts)
    dw_in3 = _tn_matmul(hn0, dz, 0, "dw_in", col_blocks=True)

    partials = [dw_in3, dwout.reshape(N_DEV, -1, d), dwq.reshape(N_DEV, -1, d), dwkv3, dwo.reshape(N_DEV, -1, d),
                dwgut.reshape(N_DEV, -1, d), dwdown.reshape(N_DEV, -1, d)]
    received = _exchange(partials, True, "scatter_grads")
    grads = {}
    for name, parts in zip(BIG, received):
        g = _sum8(parts, "sum_" + name)
        grads[name] = g.T if name == "ffn_w_gate_up" else g

    small_parts = dict(
        norm_mix_g=d_mix_g, b_in=jnp.concatenate([dbin_ag, dbin_uv], axis=1), conv_w=dconv_w[:CONV_K], conv_b=dconv_b,
        conv_ln_g=d_cln_g, conv_ln_b=d_cln_b, gm_ln_g=d_gln_g, gm_ln_b=d_gln_b, gm_w_s=dws, gm_b_s=dbst.T,
        norm_xa_g=d_xa_g, mem_norm_g=d_mem_g, norm_ffn_g=d_ffn_g, final_norm_g=d_final_g)
    packed = _pack([small_parts[nm] for nm in SMALL] + [loss_p])
    (gathered,) = _exchange([packed], False, "gather_small_grads")
    totals = _sum8(gathered, "sum_small")
    shapes = [(CONV_K, cw) if nm == "conv_w" else w[nm].shape for nm in SMALL] + [(1, 128)]
    unpacked = _unpack(totals, shapes)
    loss = unpacked[-1][0, 0]
    for nm, g in zip(SMALL, unpacked[:-1]):
        grads[nm] = lax.dynamic_slice_in_dim(g, me * conv_w.shape[1], conv_w.shape[1], axis=1) if nm == "conv_w" else g

    delta, new_m, new_v = {}, {}, {}
    for name in BIG:
        delta[name], new_m[name], new_v[name] = _adamw(w[name], grads[name], mom[name], var[name], "adamw_" + name)
    small_shapes = [w[nm].shape for nm in SMALL]
    sd, sm, sv = _adamw(_pack([w[nm] for nm in SMALL]), _pack([grads[nm] for nm in SMALL]),
                        _pack([mom[nm] for nm in SMALL]), _pack([var[nm] for nm in SMALL]), "adamw_small")
    for packed_out, dst in ((sd, delta), (sm, new_m), (sv, new_v)):
        for nm, val in zip(SMALL, _unpack(packed_out, small_shapes)):
            dst[nm] = val

    return (loss, dx.reshape(x.shape), *[grads[nm] for nm in WEIGHTS], *[delta[nm] for nm in WEIGHTS],
            *[new_m[nm] for nm in WEIGHTS], *[new_v[nm] for nm in WEIGHTS])
```

```python
import functools

import jax
import jax.numpy as jnp
from jax import lax
from jax.experimental import pallas as pl
from jax.experimental.pallas import tpu as pltpu

F32 = jnp.float32
BF16 = jnp.bfloat16
SDS = jax.ShapeDtypeStruct

N_DEV = 8
RMS_EPS = 1e-6
LN_EPS = 1e-5
CONV_K = 31
CONV_PAD = 32
CHUNK = 128
GM_HEADS = 8
XA_HEADS = 4
XA_DH = 256
GELU_K0 = 0.7978845608028654
GELU_K1 = 0.044715
ADAM_LR = 0.001
ADAM_B1 = 0.9
ADAM_B2 = 0.999
ADAM_EPS = 1e-08
ADAM_WD = 0.01
ADAM_STEP = 10
VMEM_LIMIT = 60 * 1024 * 1024

NN = (((1,), (0,)), ((), ()))
NT = (((1,), (1,)), ((), ()))
TN = (((0,), (0,)), ((), ()))


def _dot(a, b, dims=NN):
    return lax.dot_general(a, b, dims, preferred_element_type=F32)


def _bf(x):
    return x.astype(BF16)


def _cparams(*sem):
    return pltpu.CompilerParams(dimension_semantics=tuple(sem) if sem else None, vmem_limit_bytes=VMEM_LIMIT)


def _row(ts, w, col=0):
    return pl.BlockSpec((ts, w), lambda i: (i, col))


def _const(shape):
    nd = len(shape)
    return pl.BlockSpec(shape, lambda i: (0,) * nd, pipeline_mode=pl.Buffered(1))


def _acc(shape):
    nd = len(shape)
    return pl.BlockSpec(shape, lambda i: (0,) * nd)


def _rms_fwd(x, g):
    r = lax.rsqrt(jnp.mean(x * x, axis=-1, keepdims=True) + RMS_EPS)
    xh = x * r
    return xh * g, xh, r


def _rms_bwd(dy, xh, r, g):
    gdy = dy * g
    dx = r * (gdy - xh * jnp.mean(gdy * xh, axis=-1, keepdims=True))
    dg = jnp.sum(dy * xh, axis=0, keepdims=True)
    return dx, dg


def _ln_fwd(x, g, b):
    mu = jnp.mean(x, axis=-1, keepdims=True)
    xc = x - mu
    rs = lax.rsqrt(jnp.mean(xc * xc, axis=-1, keepdims=True) + LN_EPS)
    xh = xc * rs
    return xh * g + b, xh, rs


def _ln_bwd(dy, xh, rs, g):
    dxh = dy * g
    dx = rs * (dxh - jnp.mean(dxh, axis=-1, keepdims=True) - xh * jnp.mean(dxh * xh, axis=-1, keepdims=True))
    return dx, jnp.sum(dy * xh, axis=0, keepdims=True), jnp.sum(dy, axis=0, keepdims=True)


def _gelu(x):
    t = jnp.tanh(GELU_K0 * (x + GELU_K1 * (x * x * x)))
    return 0.5 * x * (1.0 + t), t


def _gelu_grad(x, t):
    return 0.5 * (1.0 + t) + 0.5 * x * (1.0 - t * t) * (GELU_K0 * (1.0 + 3.0 * GELU_K1 * x * x))


def _silu_grad(x, sg):
    return sg * (1.0 + x * (1.0 - sg))


def _accumulate(ref, val):
    @pl.when(pl.program_id(0) == 0)
    def _():
        ref[...] = jnp.zeros_like(ref)
    ref[...] += val


def _mix_masks():
    row = lax.broadcasted_iota(jnp.int32, (CHUNK, CHUNK), 0)
    col = lax.broadcasted_iota(jnp.int32, (CHUNK, CHUNK), 1)
    return row >= col, row <= col, col < (CHUNK // 2)


def _mix_fwd(vb, ws_ref, bst_ref, mixed_scr, ts):
    tril, _, lo = _mix_masks()
    for j in range(GM_HEADS // 2):
        w0 = _bf(jnp.where(tril, ws_ref[2 * j], 0.0))
        w1 = _bf(jnp.where(tril, ws_ref[2 * j + 1], 0.0))
        bias = jnp.where(lo, bst_ref[:, 2 * j:2 * j + 1], bst_ref[:, 2 * j + 1:2 * j + 2])
        for n in range(ts // CHUNK):
            v = vb[n * CHUNK:(n + 1) * CHUNK, j * 128:(j + 1) * 128]
            mixed_scr[n * CHUNK:(n + 1) * CHUNK, j * 128:(j + 1) * 128] = jnp.where(lo, _dot(w0, v), _dot(w1, v)) + bias


def _exchange(srcs, scatter, name):
    n = len(srcs)

    def body(*refs):
        src_refs, out_refs = refs[:n], refs[n:2 * n]
        send_sems, recv_sems, local_sems = refs[2 * n:]
        x, y, c = lax.axis_index("x"), lax.axis_index("y"), lax.axis_index("c")
        me = 4 * x + 2 * y + c

        def peer_of(mask):
            px = x if not (mask >> 2) & 1 else 1 - x
            py = y if not (mask >> 1) & 1 else 1 - y
            pc = c if not mask & 1 else 1 - c
            return (px, py, pc), 4 * px + 2 * py + pc

        def remote(k, mask):
            peer, pidx = peer_of(mask)
            return pltpu.make_async_remote_copy(
                src_ref=src_refs[k].at[pidx] if scatter else src_refs[k],
                dst_ref=out_refs[k].at[me],
                send_sem=send_sems.at[k, mask - 1], recv_sem=recv_sems.at[k, mask - 1],
                device_id=peer, device_id_type=pl.DeviceIdType.MESH)

        def arrival(k, mask):
            peer, pidx = peer_of(mask)
            return pltpu.make_async_remote_copy(
                src_ref=src_refs[k].at[pidx] if scatter else src_refs[k],
                dst_ref=out_refs[k].at[pidx],
                send_sem=send_sems.at[k, mask - 1], recv_sem=recv_sems.at[k, mask - 1],
                device_id=peer, device_id_type=pl.DeviceIdType.MESH)

        sends, locals_ = [], []
        for k in range(n):
            for mask in range(1, N_DEV):
                cp = remote(k, mask)
                cp.start()
                sends.append(cp)
            lc = pltpu.make_async_copy(src_refs[k].at[me] if scatter else src_refs[k], out_refs[k].at[me],
                                       local_sems.at[k])
            lc.start()
            locals_.append(lc)
        for k in range(n):
            for mask in range(1, N_DEV):
                arrival(k, mask).wait_recv()
        for cp in sends:
            cp.wait_send()
        for lc in locals_:
            lc.wait()

    outs = [SDS((N_DEV,) + tuple(s.shape[1:] if scatter else s.shape), s.dtype) for s in srcs]
    hbm = pl.BlockSpec(memory_space=pl.ANY)
    return pl.pallas_call(
        body, name=name, out_shape=outs, in_specs=[hbm] * n, out_specs=[hbm] * n,
        scratch_shapes=[pltpu.SemaphoreType.DMA((n, N_DEV - 1)), pltpu.SemaphoreType.DMA((n, N_DEV - 1)),
                        pltpu.SemaphoreType.DMA((n,))],
    )(*srcs)


def _peer_of(mask):
    x, y, c = lax.axis_index("x"), lax.axis_index("y"), lax.axis_index("c")
    px = 1 - x if (mask >> 2) & 1 else x
    py = 1 - y if (mask >> 1) & 1 else y
    pc = 1 - c if mask & 1 else c
    return (px, py, pc), 4 * px + 2 * py + pc


def _split_copy(src_ref, land_ref, send_sem, recv_sem, mask, scatter, outgoing):
    x, y, c = lax.axis_index("x"), lax.axis_index("y"), lax.axis_index("c")
    me = 4 * x + 2 * y + c
    peer, pidx = _peer_of(mask)
    return pltpu.make_async_remote_copy(
        src_ref=src_ref.at[pidx] if scatter else src_ref,
        dst_ref=land_ref.at[me if outgoing else pidx],
        send_sem=send_sem.at[mask - 1], recv_sem=recv_sem.at[mask - 1],
        device_id=peer, device_id_type=pl.DeviceIdType.MESH)


_HBM = pl.BlockSpec(memory_space=pltpu.HBM)
_SEM = pl.BlockSpec(memory_space=pltpu.SEMAPHORE)
_EFFECT = pltpu.SideEffectType.DATAFLOW_SIDE_EFFECTING


def _exchange_start(srcs, scatter, name):
    n = len(srcs)
    lands = [pltpu.with_memory_space_constraint(
        lax.empty((N_DEV,) + tuple(s.shape[1:] if scatter else s.shape), s.dtype), pltpu.HBM) for s in srcs]
    srcs = [pltpu.with_memory_space_constraint(s, pltpu.HBM) for s in srcs]

    def body(*refs):
        src_refs, land_refs = refs[:n], refs[n:2 * n]
        send_sems, recv_sems = refs[2 * n:3 * n], refs[3 * n:4 * n]
        token = refs[-1]
        for k in range(n):
            for mask in range(1, N_DEV):
                _split_copy(src_refs[k], land_refs[k], send_sems[k], recv_sems[k], mask, scatter, True).start()
        token[...] = jnp.zeros_like(token)

    sem = pltpu.SemaphoreType.DMA((N_DEV - 1,))
    out = pl.pallas_call(
        body, name=name,
        out_shape=tuple([sem] * (2 * n) + [pltpu.HBM(s.shape, s.dtype) for s in srcs]
                        + [pltpu.HBM(l.shape, l.dtype) for l in lands] + [SDS((8, 128), F32)]),
        in_specs=[_HBM] * (2 * n),
        out_specs=tuple([_SEM] * (2 * n) + [_HBM] * (2 * n) + [pl.BlockSpec(memory_space=pltpu.VMEM)]),
        input_output_aliases={i: 2 * n + i for i in range(2 * n)},
        compiler_params=pltpu.CompilerParams(has_side_effects=_EFFECT),
    )(*srcs, *lands)
    return out[:n], out[n:2 * n], out[2 * n:3 * n], out[3 * n:4 * n], out[-1]


def _exchange_wait(send_sems, recv_sems, srcs_thru, lands_thru, scatter, after, name):
    n = len(srcs_thru)

    def body(*refs):
        src_refs, land_refs = refs[:n], refs[n:2 * n]
        send_refs, recv_refs = refs[2 * n:3 * n], refs[3 * n:4 * n]
        for k in range(n):
            for mask in range(1, N_DEV):
                _split_copy(src_refs[k], land_refs[k], send_refs[k], recv_refs[k], mask, scatter, True).wait_send()
                _split_copy(src_refs[k], land_refs[k], send_refs[k], recv_refs[k], mask, scatter, False).wait_recv()

    out = pl.pallas_call(
        body, name=name,
        out_shape=tuple([pltpu.HBM(s.shape, s.dtype) for s in srcs_thru]
                        + [pltpu.HBM(l.shape, l.dtype) for l in lands_thru]),
        in_specs=[_HBM] * (2 * n) + [_SEM] * (2 * n) + [pl.BlockSpec(memory_space=pl.ANY)],
        out_specs=tuple([_HBM] * (2 * n)),
        input_output_aliases={i: i for i in range(2 * n)},
        compiler_params=pltpu.CompilerParams(has_side_effects=_EFFECT),
    )(*srcs_thru, *lands_thru, *send_sems, *recv_sems, after)
    return out[:n], out[n:]


def _with_own(landed, own, me):
    return lax.dynamic_update_slice_in_dim(landed, own[None], me, axis=0)


def _kv_proj(mem, g_mem, wkv3):
    m = mem.shape[0]

    def body(mem_ref, g_ref, w_ref, kv_ref, mn_ref):
        y, _, _ = _rms_fwd(mem_ref[...], g_ref[...])
        yb = _bf(y)
        mn_ref[...] = yb
        for b in range(N_DEV):
            kv_ref[:, 256 * b:256 * (b + 1)] = _dot(yb, w_ref[b])

    return pl.pallas_call(body, name="kv_proj", out_shape=(SDS((m, 2048), F32), SDS(mem.shape, BF16)),
                          compiler_params=_cparams())(mem, g_mem, wkv3)


def _fwd_in(x, g_mix, w_in3, b_in, ts):
    s, d = x.shape

    def body(x_ref, g_ref, w_ref, b_ref, z_ref, a_ref):
        hn, _, _ = _rms_fwd(x_ref[...], g_ref[...])
        hb = _bf(hn)
        for b in range(N_DEV):
            z_ref[:, 256 * b:256 * (b + 1)] = _dot(hb, w_ref[b]) + b_ref[:, 256 * b:256 * (b + 1)]
        a_ref[...] = z_ref[:, 0:512] * jax.nn.sigmoid(z_ref[:, 512:1024])

    return pl.pallas_call(
        body, name="fwd_in", grid=(s // ts,),
        in_specs=[_row(ts, d), _const(g_mix.shape), _const(w_in3.shape), _const(b_in.shape)],
        out_specs=(_row(ts, 2048), _row(ts, 512)),
        out_shape=(SDS((s, 2048), F32), SDS((s, 512), F32)),
        compiler_params=_cparams("arbitrary"))(x, g_mix, w_in3, b_in)


def _conv_fwd(a, w, b):
    s, cw = a.shape
    rc = 256 if s % 256 == 0 else 128

    def body(a_ref, w_ref, b_ref, c_ref, pad):
        pad[0:CONV_PAD, :] = jnp.zeros((CONV_PAD, 128), F32)
        pad[CONV_PAD:, :] = a_ref[...]

        def chunk(i, carry):
            r0 = pl.multiple_of(i * rc, rc)
            win = pad[pl.ds(r0, rc + CONV_PAD), :]
            acc = jnp.zeros((rc, 128), F32) + b_ref[...]
            for k in range(CONV_K):
                off = CONV_PAD - (CONV_K - 1) + k
                acc = acc + w_ref[k:k + 1, :] * win[off:off + rc, :]
            c_ref[pl.ds(r0, rc), :] = acc
            return carry

        lax.fori_loop(0, s // rc, chunk, 0)

    blk = lambda r: pl.BlockSpec((r, 128), lambda j: (0, j))
    return pl.pallas_call(
        body, name="conv_fwd", grid=(cw // 128,),
        in_specs=[blk(s), blk(CONV_K), blk(1)], out_specs=blk(s), out_shape=SDS((s, cw), F32),
        scratch_shapes=[pltpu.VMEM((s + CONV_PAD, 128), F32)],
        compiler_params=_cparams("arbitrary"))(a, w, b)


def _fwd_out(x, c, z, cln_g, cln_b, gln_g, gln_b, ws, bst, w_out, ts):
    s, d = x.shape

    def body(x_ref, c_ref, zuv_ref, clg, clb, glg, glb, ws_ref, bst_ref, wo_ref, h1_ref, mixed_scr):
        cl, _, _ = _ln_fwd(c_ref[...], clg[...], clb[...])
        co = cl * jax.nn.sigmoid(cl)
        u, _ = _gelu(zuv_ref[:, 0:512])
        vg, _ = _gelu(zuv_ref[:, 512:1024])
        vln, _, _ = _ln_fwd(vg, glg[...], glb[...])
        _mix_fwd(_bf(vln), ws_ref, bst_ref, mixed_scr, ts)
        gm = u * mixed_scr[...]
        h1_ref[...] = x_ref[...] + _dot(_bf(co), wo_ref[0:512, :]) + _dot(_bf(gm), wo_ref[512:1024, :])

    return pl.pallas_call(
        body, name="fwd_out", grid=(s // ts,),
        in_specs=[_row(ts, d), _row(ts, 512), _row(ts, 1024, 1), _const(cln_g.shape), _const(cln_b.shape),
                  _const(gln_g.shape), _const(gln_b.shape), _const(ws.shape), _const(bst.shape), _const(w_out.shape)],
        out_specs=_row(ts, d), out_shape=SDS((s, d), F32),
        scratch_shapes=[pltpu.VMEM((ts, 512), F32)],
        compiler_params=_cparams("arbitrary"))(x, c, z, cln_g, cln_b, gln_g, gln_b, ws, bst, w_out)


def _softmax_rows(sc):
    m = jnp.max(sc, axis=-1, keepdims=True)
    e = jnp.exp(sc - m)
    return e / jnp.sum(e, axis=-1, keepdims=True)


def _fwd_xa(h1, g_xa, wq, kv, wo, ts):
    s, d = h1.shape
    scale = XA_DH ** -0.5

    def body(h_ref, g_ref, wq_ref, kv_ref, wo_ref, h2_ref, o_scr):
        hn, _, _ = _rms_fwd(h_ref[...], g_ref[...])
        q = _dot(_bf(hn), wq_ref[...])
        for h in range(XA_HEADS):
            qh = _bf(q[:, XA_DH * h:XA_DH * (h + 1)])
            kh = _bf(kv_ref[:, XA_DH * h:XA_DH * (h + 1)])
            vh = _bf(kv_ref[:, d + XA_DH * h:d + XA_DH * (h + 1)])
            p = _softmax_rows(_dot(qh, kh, NT) * scale)
            o_scr[:, XA_DH * h:XA_DH * (h + 1)] = _dot(_bf(p), vh)
        h2_ref[...] = h_ref[...] + _dot(_bf(o_scr[...]), wo_ref[...])

    return pl.pallas_call(
        body, name="fwd_xa", grid=(s // ts,),
        in_specs=[_row(ts, d), _const(g_xa.shape), _const(wq.shape), _const(kv.shape), _const(wo.shape)],
        out_specs=_row(ts, d), out_shape=SDS((s, d), F32),
        scratch_shapes=[pltpu.VMEM((ts, d), F32)],
        compiler_params=_cparams("arbitrary"))(h1, g_xa, wq, kv, wo)


def _fwd_ffn(h2, g_ffn, wgut, wdown, g_final, target, ts):
    s, d = h2.shape
    hid = wdown.shape[0]
    hc = hid // 2

    def body(h_ref, g_ref, wgu_ref, wd_ref, gf_ref, t_ref, dh3_ref, loss_ref, dgf_ref):
        hn, _, _ = _rms_fwd(h_ref[...], g_ref[...])
        hb = _bf(hn)
        h3 = h_ref[...]
        for n in range(2):
            g = _dot(hb, wgu_ref[hc * n:hc * (n + 1), :], NT)
            u = _dot(hb, wgu_ref[hid + hc * n:hid + hc * (n + 1), :], NT)
            act = g * jax.nn.sigmoid(g) * u
            h3 = h3 + _dot(_bf(act), wd_ref[hc * n:hc * (n + 1), :])
        y, xh, r = _rms_fwd(h3, gf_ref[...])
        diff = y - t_ref[...]
        part = 0.5 * jnp.sum(jnp.mean(diff * diff, axis=-1, keepdims=True), axis=0, keepdims=True)
        _accumulate(loss_ref, jnp.zeros(loss_ref.shape, F32) + part)
        dh3, dgf = _rms_bwd(diff * (1.0 / d), xh, r, gf_ref[...])
        dh3_ref[...] = dh3
        _accumulate(dgf_ref, dgf)

    return pl.pallas_call(
        body, name="fwd_ffn", grid=(s // ts,),
        in_specs=[_row(ts, d), _const(g_ffn.shape), _const(wgut.shape), _const(wdown.shape), _const(g_final.shape),
                  _row(ts, d)],
        out_specs=(_row(ts, d), _acc((1, 128)), _acc((1, d))),
        out_shape=(SDS((s, d), F32), SDS((1, 128), F32), SDS((1, d), F32)),
        compiler_params=_cparams("arbitrary"))(h2, g_ffn, wgut, wdown, g_final, target)


def _bwd_ffn(h2, dh3, g_ffn, wgut, wdown, ts):
    s, d = h2.shape
    hid = wdown.shape[0]
    hc = hid // 2

    def body(h_ref, dh3_ref, g_ref, wgu_ref, wd_ref, dh2_ref, act_ref, dgu_ref, hn_ref, dg_ref):
        hn, xh, r = _rms_fwd(h_ref[...], g_ref[...])
        hb = _bf(hn)
        hn_ref[...] = hb
        db = _bf(dh3_ref[...])
        dhn = jnp.zeros((ts, d), F32)
        for n in range(2):
            wg = wgu_ref[hc * n:hc * (n + 1), :]
            wu = wgu_ref[hid + hc * n:hid + hc * (n + 1), :]
            g = _dot(hb, wg, NT)
            u = _dot(hb, wu, NT)
            sg = jax.nn.sigmoid(g)
            sl = g * sg
            act_ref[:, hc * n:hc * (n + 1)] = _bf(sl * u)
            dact = _dot(db, wd_ref[hc * n:hc * (n + 1), :], NT)
            dgb = _bf(dact * u * _silu_grad(g, sg))
            dub = _bf(dact * sl)
            dgu_ref[:, hc * n:hc * (n + 1)] = dgb
            dgu_ref[:, hid + hc * n:hid + hc * (n + 1)] = dub
            dhn = dhn + _dot(dgb, wg) + _dot(dub, wu)
        dx, dg = _rms_bwd(dhn, xh, r, g_ref[...])
        dh2_ref[...] = dh3_ref[...] + dx
        _accumulate(dg_ref, dg)

    return pl.pallas_call(
        body, name="bwd_ffn", grid=(s // ts,),
        in_specs=[_row(ts, d), _row(ts, d), _const(g_ffn.shape), _const(wgut.shape), _const(wdown.shape)],
        out_specs=(_row(ts, d), _row(ts, hid), _row(ts, 2 * hid), _row(ts, d), _acc((1, d))),
        out_shape=(SDS((s, d), F32), SDS((s, hid), BF16), SDS((s, 2 * hid), BF16), SDS((s, d), BF16),
                   SDS((1, d), F32)),
        compiler_params=_cparams("arbitrary"))(h2, dh3, g_ffn, wgut, wdown)


def _bwd_xa(h1, dh2, g_xa, wq, wo, kv, ts):
    s, d = h1.shape
    scale = XA_DH ** -0.5

    def body(h_ref, dh2_ref, g_ref, wq_ref, wo_ref, kv_ref, dh1_ref, dq_ref, o_ref, hn_ref, dkv_ref, dg_ref,
             dq_scr):
        hn, xh, r = _rms_fwd(h_ref[...], g_ref[...])
        hb = _bf(hn)
        hn_ref[...] = hb
        q = _dot(hb, wq_ref[...])
        do = _dot(_bf(dh2_ref[...]), wo_ref[...], NT)

        @pl.when(pl.program_id(0) == 0)
        def _():
            dkv_ref[...] = jnp.zeros_like(dkv_ref)

        for h in range(XA_HEADS):
            lo, hi = XA_DH * h, XA_DH * (h + 1)
            qh = _bf(q[:, lo:hi])
            kh = _bf(kv_ref[:, lo:hi])
            vh = _bf(kv_ref[:, d + lo:d + hi])
            p = _softmax_rows(_dot(qh, kh, NT) * scale)
            pb = _bf(p)
            o_ref[:, lo:hi] = _bf(_dot(pb, vh))
            doh = _bf(do[:, lo:hi])
            dp = _dot(doh, vh, NT)
            ds = p * (dp - jnp.sum(p * dp, axis=-1, keepdims=True)) * scale
            dsb = _bf(ds)
            dq_scr[:, lo:hi] = _dot(dsb, kh)
            dkv_ref[:, lo:hi] += _dot(dsb, qh, TN)
            dkv_ref[:, d + lo:d + hi] += _dot(pb, doh, TN)
        dqb = _bf(dq_scr[...])
        dq_ref[...] = dqb
        dx, dg = _rms_bwd(_dot(dqb, wq_ref[...], NT), xh, r, g_ref[...])
        dh1_ref[...] = dh2_ref[...] + dx
        _accumulate(dg_ref, dg)

    return pl.pallas_call(
        body, name="bwd_xa", grid=(s // ts,),
        in_specs=[_row(ts, d), _row(ts, d), _const(g_xa.shape), _const(wq.shape), _const(wo.shape), _const(kv.shape)],
        out_specs=(_row(ts, d), _row(ts, d), _row(ts, d), _row(ts, d), _acc(kv.shape), _acc((1, d))),
        out_shape=(SDS((s, d), F32), SDS((s, d), BF16), SDS((s, d), BF16), SDS((s, d), BF16), SDS(kv.shape, F32),
                   SDS((1, d), F32)),
        scratch_shapes=[pltpu.VMEM((ts, d), F32)],
        compiler_params=_cparams("arbitrary"))(h1, dh2, g_xa, wq, wo, kv)


def _bwd_kv(dkv, mn, mem, g_mem, wkv3):
    d = mem.shape[1]

    def body(dkv_ref, mn_ref, mem_ref, g_ref, w_ref, dw_ref, dg_ref):
        dkvb = _bf(dkv_ref[...])
        dmn = jnp.zeros(mem_ref.shape, F32)
        for b in range(N_DEV):
            blk = dkvb[:, 256 * b:256 * (b + 1)]
            dmn = dmn + _dot(blk, w_ref[b], NT)
            dw_ref[b] = _bf(_dot(mn_ref[...], blk, TN))
        _, xh, r = _rms_fwd(mem_ref[...], g_ref[...])
        _, dg = _rms_bwd(dmn, xh, r, g_ref[...])
        dg_ref[...] = dg

    return pl.pallas_call(body, name="bwd_kv", out_shape=(SDS(wkv3.shape, BF16), SDS((1, d), F32)),
                          compiler_params=_cparams())(dkv, mn, mem, g_mem, wkv3)


def _bwd_out(dh1, c, z, cln_g, cln_b, gln_g, gln_b, ws, wst, bst, w_out, ts):
    s, d = dh1.shape
    nh = GM_HEADS

    def body(dh1_ref, c_ref, zuv_ref, clg, clb, glg, glb, ws_ref, wst_ref, bst_ref, wo_ref,
             cat_ref, dc_ref, dzuv_ref, dws_ref, dbst_ref, dclg_ref, dclb_ref, dglg_ref, dglb_ref, dbin_ref,
             mixed_scr, dv_scr):
        cl, chat, crs = _ln_fwd(c_ref[...], clg[...], clb[...])
        sg = jax.nn.sigmoid(cl)
        zu = zuv_ref[:, 0:512]
        zv = zuv_ref[:, 512:1024]
        u, tu = _gelu(zu)
        vg, tv = _gelu(zv)
        vln, vhat, vrs = _ln_fwd(vg, glg[...], glb[...])
        vb = _bf(vln)
        _mix_fwd(vb, ws_ref, bst_ref, mixed_scr, ts)
        mixed = mixed_scr[...]
        cat_ref[:, 0:512] = _bf(cl * sg)
        cat_ref[:, 512:1024] = _bf(u * mixed)
        dcat = _dot(_bf(dh1_ref[...]), wo_ref[...], NT)
        dgm = dcat[:, 512:1024]
        dc, dclg, dclb = _ln_bwd(dcat[:, 0:512] * _silu_grad(cl, sg), chat, crs, clg[...])
        dc_ref[...] = dc
        dzu = dgm * mixed * _gelu_grad(zu, tu)
        dm = dgm * u

        @pl.when(pl.program_id(0) == 0)
        def _():
            dws_ref[...] = jnp.zeros_like(dws_ref)
            dbst_ref[...] = jnp.zeros_like(dbst_ref)

        tril, triu, lo = _mix_masks()
        head = lax.broadcasted_iota(jnp.int32, (1, nh), 1)
        for j in range(nh // 2):
            w0t = _bf(jnp.where(triu, wst_ref[2 * j], 0.0))
            w1t = _bf(jnp.where(triu, wst_ref[2 * j + 1], 0.0))
            for n in range(ts // CHUNK):
                rows = slice(n * CHUNK, (n + 1) * CHUNK)
                lanes = slice(j * 128, (j + 1) * 128)
                dmc = dm[rows, lanes]
                dmb = _bf(dmc)
                dv_scr[rows, lanes] = jnp.where(lo, _dot(w0t, dmb), _dot(w1t, dmb))
                vc = vb[rows, lanes]
                d0 = jnp.where(lo, dmc, 0.0)
                d1 = dmc - d0
                dws_ref[2 * j] += jnp.where(tril, _dot(_bf(d0), vc, NT), 0.0)
                dws_ref[2 * j + 1] += jnp.where(tril, _dot(_bf(d1), vc, NT), 0.0)
                dbst_ref[...] += (jnp.sum(d0, axis=1, keepdims=True) * (head == 2 * j).astype(F32)
                                  + jnp.sum(d1, axis=1, keepdims=True) * (head == 2 * j + 1).astype(F32))
        dvg, dglg, dglb = _ln_bwd(dv_scr[...], vhat, vrs, glg[...])
        dzv = dvg * _gelu_grad(zv, tv)
        dzuv_ref[:, 0:512] = _bf(dzu)
        dzuv_ref[:, 512:1024] = _bf(dzv)
        _accumulate(dclg_ref, dclg)
        _accumulate(dclb_ref, dclb)
        _accumulate(dglg_ref, dglg)
        _accumulate(dglb_ref, dglb)
        _accumulate(dbin_ref, jnp.concatenate([jnp.sum(dzu, axis=0, keepdims=True),
                                               jnp.sum(dzv, axis=0, keepdims=True)], axis=1))

    vec = (1, 512)
    return pl.pallas_call(
        body, name="bwd_out", grid=(s // ts,),
        in_specs=[_row(ts, d), _row(ts, 512), _row(ts, 1024, 1), _const(cln_g.shape), _const(cln_b.shape),
                  _const(gln_g.shape), _const(gln_b.shape), _const(ws.shape), _const(wst.shape), _const(bst.shape),
                  _const(w_out.shape)],
        out_specs=(_row(ts, d), _row(ts, 512), _row(ts, 1024), _acc(ws.shape), _acc(bst.shape), _acc(vec), _acc(vec),
                   _acc(vec), _acc(vec), _acc((1, 1024))),
        out_shape=(SDS((s, d), BF16), SDS((s, 512), F32), SDS((s, 1024), BF16), SDS(ws.shape, F32),
                   SDS(bst.shape, F32), SDS(vec, F32), SDS(vec, F32), SDS(vec, F32), SDS(vec, F32), SDS((1, 1024), F32)),
        scratch_shapes=[pltpu.VMEM((ts, 512), F32), pltpu.VMEM((ts, 512), F32)],
        compiler_params=_cparams("arbitrary"))(dh1, c, z, cln_g, cln_b, gln_g, gln_b, ws, wst, bst, w_out)


def _conv_bwd(dc, a, w):
    s, cw = a.shape
    rc = 256 if s % 256 == 0 else 128

    def body(dc_ref, a_ref, w_ref, da_ref, dw_ref, db_ref, pad_a, pad_d):
        pad_a[0:CONV_PAD, :] = jnp.zeros((CONV_PAD, 128), F32)
        pad_a[CONV_PAD:, :] = a_ref[...]
        pad_d[0:s, :] = dc_ref[...]
        pad_d[s:, :] = jnp.zeros((CONV_PAD, 128), F32)
        dw_ref[...] = jnp.zeros_like(dw_ref)
        db_ref[...] = jnp.zeros_like(db_ref)

        def chunk(i, carry):
            r0 = pl.multiple_of(i * rc, rc)
            win_d = pad_d[pl.ds(r0, rc + CONV_PAD), :]
            win_a = pad_a[pl.ds(r0, rc + CONV_PAD), :]
            dcc = win_d[0:rc, :]
            acc = jnp.zeros((rc, 128), F32)
            for k in range(CONV_K):
                acc = acc + w_ref[k:k + 1, :] * win_d[CONV_K - 1 - k:CONV_K - 1 - k + rc, :]
                off = CONV_PAD - (CONV_K - 1) + k
                dw_ref[k:k + 1, :] += jnp.sum(dcc * win_a[off:off + rc, :], axis=0, keepdims=True)
            da_ref[pl.ds(r0, rc), :] = acc
            db_ref[...] += jnp.sum(dcc, axis=0, keepdims=True)
            return carry

        lax.fori_loop(0, s // rc, chunk, 0)

    blk = lambda r: pl.BlockSpec((r, 128), lambda j: (0, j))
    return pl.pallas_call(
        body, name="conv_bwd", grid=(cw // 128,),
        in_specs=[blk(s), blk(s), blk(CONV_K)], out_specs=(blk(s), blk(CONV_PAD), blk(1)),
        out_shape=(SDS((s, cw), F32), SDS((CONV_PAD, cw), F32), SDS((1, cw), F32)),
        scratch_shapes=[pltpu.VMEM((s + CONV_PAD, 128), F32), pltpu.VMEM((s + CONV_PAD, 128), F32)],
        compiler_params=_cparams("arbitrary"))(dc, a, w)


def _bwd_in(x, dh1, da, z, dzuv, g_mix, w_in3, ts):
    s, d = x.shape

    def body(x_ref, dh1_ref, da_ref, zag_ref, dzuv_ref, g_ref, w_ref, dx_ref, dz_ref, hn_ref, dbin_ref, dg_ref):
        za = zag_ref[:, 0:512]
        sg = jax.nn.sigmoid(zag_ref[:, 512:1024])
        da_ = da_ref[...]
        dza = da_ * sg
        dzg = da_ * za * sg * (1.0 - sg)
        dz_ref[:, 0:512] = _bf(dza)
        dz_ref[:, 512:1024] = _bf(dzg)
        dz_ref[:, 1024:2048] = dzuv_ref[...]
        dhn = jnp.zeros((ts, d), F32)
        for b in range(N_DEV):
            dhn = dhn + _dot(dz_ref[:, 256 * b:256 * (b + 1)], w_ref[b], NT)
        hn, xh, r = _rms_fwd(x_ref[...], g_ref[...])
        hn_ref[...] = _bf(hn)
        dxn, dg = _rms_bwd(dhn, xh, r, g_ref[...])
        dx_ref[...] = dh1_ref[...] + dxn
        _accumulate(dg_ref, dg)
        _accumulate(dbin_ref, jnp.concatenate([jnp.sum(dza, axis=0, keepdims=True),
                                               jnp.sum(dzg, axis=0, keepdims=True)], axis=1))

    return pl.pallas_call(
        body, name="bwd_in", grid=(s // ts,),
        in_specs=[_row(ts, d), _row(ts, d), _row(ts, 512), _row(ts, 1024, 0), _row(ts, 1024), _const(g_mix.shape),
                  _const(w_in3.shape)],
        out_specs=(_row(ts, d), _row(ts, 2048), _row(ts, d), _acc((1, 1024)), _acc((1, d))),
        out_shape=(SDS((s, d), F32), SDS((s, 2048), BF16), SDS((s, d), BF16), SDS((1, 1024), F32), SDS((1, d), F32)),
        compiler_params=_cparams("arbitrary"))(x, dh1, da, z, dzuv, g_mix, w_in3)


def _tn_matmul(a, b, tm, name, col_blocks=False):
    s, m = a.shape
    n = b.shape[1]
    tn = 256 if col_blocks else n
    if col_blocks:
        tm = m
    ts = min(s, 512)
    n_s = s // ts

    def body(a_ref, b_ref, o_ref, acc):
        k = pl.program_id(2)

        @pl.when(k == 0)
        def _():
            acc[...] = jnp.zeros_like(acc)

        acc[...] += _dot(_bf(a_ref[...]), _bf(b_ref[...]), TN)

        @pl.when(k == n_s - 1)
        def _():
            if col_blocks:
                o_ref[0] = _bf(acc[...])
            else:
                o_ref[...] = _bf(acc[...])

    if col_blocks:
        out_shape = SDS((n // tn, m, tn), BF16)
        out_spec = pl.BlockSpec((1, m, tn), lambda i, j, k: (j, 0, 0))
    else:
        out_shape = SDS((m, n), BF16)
        out_spec = pl.BlockSpec((tm, tn), lambda i, j, k: (i, j))
    return pl.pallas_call(
        body, name=name, grid=(m // tm, n // tn, n_s),
        in_specs=[pl.BlockSpec((ts, tm), lambda i, j, k: (k, i)), pl.BlockSpec((ts, tn), lambda i, j, k: (k, j))],
        out_specs=out_spec, out_shape=out_shape,
        scratch_shapes=[pltpu.VMEM((tm, tn), F32)],
        compiler_params=_cparams("parallel", "parallel", "arbitrary"))(a, b)


def _sum8(parts, name):
    _, r, c = parts.shape
    tr = r
    for cand in (512, 256, 352, 128, 8):
        if r % cand == 0 and cand <= r:
            tr = cand
            break

    def body(p_ref, o_ref):
        acc = p_ref[0].astype(F32)
        for dev in range(1, N_DEV):
            acc = acc + p_ref[dev].astype(F32)
        o_ref[...] = acc

    return pl.pallas_call(
        body, name=name, grid=(r // tr,),
        in_specs=[pl.BlockSpec((N_DEV, tr, c), lambda i: (0, i, 0))],
        out_specs=pl.BlockSpec((tr, c), lambda i: (i, 0)), out_shape=SDS((r, c), F32),
        compiler_params=_cparams("parallel"))(parts)


def _adamw(w, g, m, v, name):
    r, c = w.shape
    tr = r
    for cand in (256, 128, 176, 8):
        if r % cand == 0 and cand <= r:
            tr = cand
            break

    def body(w_ref, g_ref, m_ref, v_ref, d_ref, m2_ref, v2_ref):
        g_ = g_ref[...]
        m2 = ADAM_B1 * m_ref[...] + (1.0 - ADAM_B1) * g_
        v2 = ADAM_B2 * v_ref[...] + (1.0 - ADAM_B2) * (g_ * g_)
        m_hat = m2 / (1.0 - ADAM_B1 ** ADAM_STEP)
        v_hat = v2 / (1.0 - ADAM_B2 ** ADAM_STEP)
        d_ref[...] = -ADAM_LR * (m_hat / (jnp.sqrt(v_hat) + ADAM_EPS) + ADAM_WD * w_ref[...])
        m2_ref[...] = m2
        v2_ref[...] = v2

    blk = pl.BlockSpec((tr, c), lambda i: (i, 0))
    return pl.pallas_call(
        body, name=name, grid=(r // tr,), in_specs=[blk] * 4, out_specs=(blk,) * 3,
        out_shape=(SDS((r, c), F32),) * 3, compiler_params=_cparams("parallel"))(w, g, m, v)


def _pack(arrs):
    flat = jnp.concatenate([a.reshape(-1) for a in arrs])
    pad = (-flat.shape[0]) % (256 * 128)
    return jnp.pad(flat, (0, pad)).reshape(-1, 128)


def _unpack(packed, shapes):
    flat = packed.reshape(-1)
    out, off = [], 0
    for shp in shapes:
        size = 1
        for dim in shp:
            size *= dim
        out.append(flat[off:off + size].reshape(shp))
        off += size
    return out


SMALL = ("norm_mix_g", "b_in", "conv_w", "conv_b", "conv_ln_g", "conv_ln_b", "gm_ln_g", "gm_ln_b", "gm_w_s", "gm_b_s",
         "norm_xa_g", "mem_norm_g", "norm_ffn_g", "final_norm_g")
BIG = ("w_in", "w_out", "xa_wq", "xa_wkv", "xa_wo", "ffn_w_gate_up", "ffn_w_down")
WEIGHTS = ("norm_mix_g", "w_in", "b_in", "conv_w", "conv_b", "conv_ln_g", "conv_ln_b", "gm_ln_g", "gm_ln_b", "gm_w_s",
           "gm_b_s", "w_out", "norm_xa_g", "mem_norm_g", "xa_wq", "xa_wkv", "xa_wo", "norm_ffn_g", "ffn_w_gate_up",
           "ffn_w_down", "final_norm_g")


def kernel(x, mem, norm_mix_g, w_in, b_in, conv_w, conv_b, conv_ln_g, conv_ln_b, gm_ln_g, gm_ln_b, gm_w_s, gm_b_s, w_out, norm_xa_g, mem_norm_g, xa_wq, xa_wkv, xa_wo, norm_ffn_g, ffn_w_gate_up, ffn_w_down, final_norm_g, loss_target, m_norm_mix_g, m_w_in, m_b_in, m_conv_w, m_conv_b, m_conv_ln_g, m_conv_ln_b, m_gm_ln_g, m_gm_ln_b, m_gm_w_s, m_gm_b_s, m_w_out, m_norm_xa_g, m_mem_norm_g, m_xa_wq, m_xa_wkv, m_xa_wo, m_norm_ffn_g, m_ffn_w_gate_up, m_ffn_w_down, m_final_norm_g, v_norm_mix_g, v_w_in, v_b_in, v_conv_w, v_conv_b, v_conv_ln_g, v_conv_ln_b, v_gm_ln_g, v_gm_ln_b, v_gm_w_s, v_gm_b_s, v_w_out, v_norm_xa_g, v_mem_norm_g, v_xa_wq, v_xa_wkv, v_xa_wo, v_norm_ffn_g, v_ffn_w_gate_up, v_ffn_w_down, v_final_norm_g):
    w = dict(norm_mix_g=norm_mix_g, w_in=w_in, b_in=b_in, conv_w=conv_w, conv_b=conv_b, conv_ln_g=conv_ln_g,
             conv_ln_b=conv_ln_b, gm_ln_g=gm_ln_g, gm_ln_b=gm_ln_b, gm_w_s=gm_w_s, gm_b_s=gm_b_s, w_out=w_out,
             norm_xa_g=norm_xa_g, mem_norm_g=mem_norm_g, xa_wq=xa_wq, xa_wkv=xa_wkv, xa_wo=xa_wo,
             norm_ffn_g=norm_ffn_g, ffn_w_gate_up=ffn_w_gate_up, ffn_w_down=ffn_w_down, final_norm_g=final_norm_g)
    mom = dict(norm_mix_g=m_norm_mix_g, w_in=m_w_in, b_in=m_b_in, conv_w=m_conv_w, conv_b=m_conv_b,
               conv_ln_g=m_conv_ln_g, conv_ln_b=m_conv_ln_b, gm_ln_g=m_gm_ln_g, gm_ln_b=m_gm_ln_b, gm_w_s=m_gm_w_s,
               gm_b_s=m_gm_b_s, w_out=m_w_out, norm_xa_g=m_norm_xa_g, mem_norm_g=m_mem_norm_g, xa_wq=m_xa_wq,
               xa_wkv=m_xa_wkv, xa_wo=m_xa_wo, norm_ffn_g=m_norm_ffn_g, ffn_w_gate_up=m_ffn_w_gate_up,
               ffn_w_down=m_ffn_w_down, final_norm_g=m_final_norm_g)
    var = dict(norm_mix_g=v_norm_mix_g, w_in=v_w_in, b_in=v_b_in, conv_w=v_conv_w, conv_b=v_conv_b,
               conv_ln_g=v_conv_ln_g, conv_ln_b=v_conv_ln_b, gm_ln_g=v_gm_ln_g, gm_ln_b=v_gm_ln_b, gm_w_s=v_gm_w_s,
               gm_b_s=v_gm_b_s, w_out=v_w_out, norm_xa_g=v_norm_xa_g, mem_norm_g=v_mem_norm_g, xa_wq=v_xa_wq,
               xa_wkv=v_xa_wkv, xa_wo=v_xa_wo, norm_ffn_g=v_norm_ffn_g, ffn_w_gate_up=v_ffn_w_gate_up,
               ffn_w_down=v_ffn_w_down, final_norm_g=v_final_norm_g)

    me = 4 * lax.axis_index("x") + 2 * lax.axis_index("y") + lax.axis_index("c")
    s, d = x.shape[1], x.shape[2]
    xs = x.reshape(s, d)
    mems = mem.reshape(mem.shape[1], d)
    tgt = loss_target.reshape(s, d)
    ts = min(512, s)
    ts_ffn = min(256, s)
    row = lambda a: a.reshape(1, -1)

    conv_w_pad = jnp.pad(conv_w, ((0, CONV_PAD - CONV_K), (0, 128 - conv_w.shape[1])))
    shards = [_bf(w_in), conv_w_pad, _bf(xa_wkv), _bf(w_out), _bf(xa_wq), _bf(xa_wo), _bf(ffn_w_gate_up.T),
              _bf(ffn_w_down)]
    g_send, g_recv, g_src, g_land, g_tok = _exchange_start(shards, False, "gather_start")

    def gathered(idx, after, name):
        pick = lambda seq: [seq[i] for i in idx]
        srcs, lands = _exchange_wait(pick(g_send), pick(g_recv), pick(g_src), pick(g_land), False, after, name)
        return [_with_own(land, src, me) for src, land in zip(srcs, lands)]

    bst = gm_b_s.T
    wst = jnp.swapaxes(gm_w_s, 1, 2)

    w_in3, conv_w8 = gathered((0, 1), g_tok, "gather_wait_in")
    conv_w_f = conv_w8[:, :CONV_K, :conv_w.shape[1]].transpose(1, 0, 2).reshape(CONV_K, -1)
    cw = conv_w_f.shape[1]
    z, a = _fwd_in(xs, row(norm_mix_g), w_in3, row(b_in), ts)
    c = _conv_fwd(a, conv_w_f, row(conv_b))
    wkv3, w_out3 = gathered((2, 3), c, "gather_wait_out")
    w_out_f = w_out3.reshape(-1, d)
    kv, mn = _kv_proj(mems, row(mem_norm_g), wkv3)
    h1 = _fwd_out(xs, c, z, row(conv_ln_g), row(conv_ln_b), row(gm_ln_g), row(gm_ln_b), gm_w_s, bst, w_out_f, ts)
    wq3, wo3 = gathered((4, 5), h1, "gather_wait_xa")
    wq_f = wq3.reshape(-1, d)
    wo_f = wo3.reshape(-1, d)
    h2 = _fwd_xa(h1, row(norm_xa_g), wq_f, kv, wo_f, ts)
    wgut3, wdown3 = gathered((6, 7), h2, "gather_wait_ffn")
    wgut_f = wgut3.reshape(-1, d)
    wdown_f = wdown3.reshape(-1, d)
    dh3, loss_p, d_final_g = _fwd_ffn(h2, row(norm_ffn_g), wgut_f, wdown_f, row(final_norm_g), tgt, ts_ffn)

    dh2, act, dgu, hn2, d_ffn_g = _bwd_ffn(h2, dh3, row(norm_ffn_g), wgut_f, wdown_f, ts_ffn)
    dwgut = _tn_matmul(dgu, hn2, 512, "dw_gate_up")
    dwdown = _tn_matmul(act, dh3, 256, "dw_down")
    blocks = lambda m: m.reshape(N_DEV, -1, d)
    ex_ffn = _exchange_start([blocks(dwgut), blocks(dwdown)], True, "scatter_start_ffn")
    dh1, dq, o, hn1, dkv, d_xa_g = _bwd_xa(h1, dh2, row(norm_xa_g) + ex_ffn[4][0:1, 0:1], wq_f, wo_f, kv, ts)
    dwq = _tn_matmul(hn1, dq, 256, "dw_q")
    dwo = _tn_matmul(o, dh2, 256, "dw_o")
    dwkv3, d_mem_g = _bwd_kv(dkv, mn, mems, row(mem_norm_g), wkv3)
    ex_xa = _exchange_start([blocks(dwq), blocks(dwo), dwkv3], True, "scatter_start_xa")
    (cat, dc, dzuv, dws, dbst, d_cln_g, d_cln_b, d_gln_g, d_gln_b, dbin_uv) = _bwd_out(
        dh1, c, z, row(conv_ln_g) + ex_xa[4][0:1, 0:1], row(conv_ln_b), row(gm_ln_g), row(gm_ln_b), gm_w_s, wst, bst,
        w_out_f, ts)
    dwout = _tn_matmul(cat, dh1, 256, "dw_out")
    da, dconv_w, dconv_b = _conv_bwd(dc, a, conv_w_f)
    dx, dz, hn0, dbin_ag, d_mix_g = _bwd_in(xs, dh1, da, z, dzuv, row(norm_mix_g), w_in3, ts)
    dw_in3 = _tn_matmul(hn0, dz, 0, "dw_in", col_blocks=True)
    ex_mix = _exchange_start([blocks(dwout), dw_in3], True, "scatter_start_mix")

    grads = {}

    def reduced(ex, names, after, name):
        srcs, lands = _exchange_wait(ex[0], ex[1], ex[2], ex[3], True, after, name)
        for nm, src, land in zip(names, srcs, lands):
            own = lax.dynamic_index_in_dim(src, me, axis=0, keepdims=False)
            g = _sum8(_with_own(land, own, me), "sum_" + nm)
            grads[nm] = g.T if nm == "ffn_w_gate_up" else g

    small_parts = dict(
        norm_mix_g=d_mix_g, b_in=jnp.concatenate([dbin_ag, dbin_uv], axis=1), conv_w=dconv_w[:CONV_K], conv_b=dconv_b,
        conv_ln_g=d_cln_g, conv_ln_b=d_cln_b, gm_ln_g=d_gln_g, gm_ln_b=d_gln_b, gm_w_s=dws, gm_b_s=dbst.T,
        norm_xa_g=d_xa_g, mem_norm_g=d_mem_g, norm_ffn_g=d_ffn_g, final_norm_g=d_final_g)
    packed = _pack([small_parts[nm] for nm in SMALL] + [loss_p]) + ex_mix[4][0:1, 0:1]
    (small_all,) = _exchange([packed], False, "gather_small_grads")
    totals = _sum8(small_all, "sum_small")
    shapes = [(CONV_K, cw) if nm == "conv_w" else w[nm].shape for nm in SMALL] + [(1, 128)]
    unpacked = _unpack(totals, shapes)
    loss = unpacked[-1][0, 0]
    for nm, g in zip(SMALL, unpacked[:-1]):
        grads[nm] = lax.dynamic_slice_in_dim(g, me * conv_w.shape[1], conv_w.shape[1], axis=1) if nm == "conv_w" else g

    delta, new_m, new_v = {}, {}, {}
    small_shapes = [w[nm].shape for nm in SMALL]
    sd, sm, sv = _adamw(_pack([w[nm] for nm in SMALL]), _pack([grads[nm] for nm in SMALL]),
                        _pack([mom[nm] for nm in SMALL]), _pack([var[nm] for nm in SMALL]), "adamw_small")
    for packed_out, dst in ((sd, delta), (sm, new_m), (sv, new_v)):
        for nm, val in zip(SMALL, _unpack(packed_out, small_shapes)):
            dst[nm] = val
    after = sd
    for ex, names, tag in ((ex_ffn, ("ffn_w_gate_up", "ffn_w_down"), "ffn"), (ex_xa, ("xa_wq", "xa_wo", "xa_wkv"), "xa"),
                           (ex_mix, ("w_out", "w_in"), "mix")):
        reduced(ex, names, after, "scatter_wait_" + tag)
        for name in names:
            delta[name], new_m[name], new_v[name] = _adamw(w[name], grads[name], mom[name], var[name], "adamw_" + name)
        after = new_v[names[-1]]

    return (loss, dx.reshape(x.shape), *[grads[nm] for nm in WEIGHTS], *[delta[nm] for nm in WEIGHTS],
            *[new_m[nm] for nm in WEIGHTS], *[new_v[nm] for nm in WEIGHTS])
```

```python
import functools

import jax
import jax.numpy as jnp
from jax import lax
from jax.experimental import pallas as pl
from jax.experimental.pallas import tpu as pltpu

F32 = jnp.float32
BF16 = jnp.bfloat16
SDS = jax.ShapeDtypeStruct

N_DEV = 8
RMS_EPS = 1e-6
LN_EPS = 1e-5
CONV_K = 31
CONV_PAD = 32
CHUNK = 128
GM_HEADS = 8
XA_HEADS = 4
XA_DH = 256
GELU_K0 = 0.7978845608028654
GELU_K1 = 0.044715
ADAM_LR = 0.001
ADAM_B1 = 0.9
ADAM_B2 = 0.999
ADAM_EPS = 1e-08
ADAM_WD = 0.01
ADAM_STEP = 10
VMEM_LIMIT = 60 * 1024 * 1024

NN = (((1,), (0,)), ((), ()))
NT = (((1,), (1,)), ((), ()))
TN = (((0,), (0,)), ((), ()))


def _dot(a, b, dims=NN):
    return lax.dot_general(a, b, dims, preferred_element_type=F32)


def _bf(x):
    return x.astype(BF16)


def _cparams(*sem):
    return pltpu.CompilerParams(dimension_semantics=tuple(sem) if sem else None, vmem_limit_bytes=VMEM_LIMIT)


def _row(ts, w, col=0):
    return pl.BlockSpec((ts, w), lambda i: (i, col))


def _const(shape):
    nd = len(shape)
    return pl.BlockSpec(shape, lambda i: (0,) * nd, pipeline_mode=pl.Buffered(1))


def _acc(shape):
    nd = len(shape)
    return pl.BlockSpec(shape, lambda i: (0,) * nd)


def _rms_fwd(x, g):
    r = lax.rsqrt(jnp.mean(x * x, axis=-1, keepdims=True) + RMS_EPS)
    xh = x * r
    return xh * g, xh, r


def _rms_bwd(dy, xh, r, g):
    gdy = dy * g
    dx = r * (gdy - xh * jnp.mean(gdy * xh, axis=-1, keepdims=True))
    dg = jnp.sum(dy * xh, axis=0, keepdims=True)
    return dx, dg


def _ln_fwd(x, g, b):
    mu = jnp.mean(x, axis=-1, keepdims=True)
    xc = x - mu
    rs = lax.rsqrt(jnp.mean(xc * xc, axis=-1, keepdims=True) + LN_EPS)
    xh = xc * rs
    return xh * g + b, xh, rs


def _ln_bwd(dy, xh, rs, g):
    dxh = dy * g
    dx = rs * (dxh - jnp.mean(dxh, axis=-1, keepdims=True) - xh * jnp.mean(dxh * xh, axis=-1, keepdims=True))
    return dx, jnp.sum(dy * xh, axis=0, keepdims=True), jnp.sum(dy, axis=0, keepdims=True)


def _gelu(x):
    t = jnp.tanh(GELU_K0 * (x + GELU_K1 * (x * x * x)))
    return 0.5 * x * (1.0 + t), t


def _gelu_grad(x, t):
    return 0.5 * (1.0 + t) + 0.5 * x * (1.0 - t * t) * (GELU_K0 * (1.0 + 3.0 * GELU_K1 * x * x))


def _silu_grad(x, sg):
    return sg * (1.0 + x * (1.0 - sg))


def _accumulate(ref, val):
    @pl.when(pl.program_id(0) == 0)
    def _():
        ref[...] = jnp.zeros_like(ref)
    ref[...] += val


def _mix_masks():
    row = lax.broadcasted_iota(jnp.int32, (CHUNK, CHUNK), 0)
    col = lax.broadcasted_iota(jnp.int32, (CHUNK, CHUNK), 1)
    return row >= col, row <= col, col < (CHUNK // 2)


def _mix_fwd(vb, ws_ref, bst_ref, mixed_scr, ts):
    tril, _, lo = _mix_masks()
    for j in range(GM_HEADS // 2):
        w0 = _bf(jnp.where(tril, ws_ref[2 * j], 0.0))
        w1 = _bf(jnp.where(tril, ws_ref[2 * j + 1], 0.0))
        bias = jnp.where(lo, bst_ref[:, 2 * j:2 * j + 1], bst_ref[:, 2 * j + 1:2 * j + 2])
        for n in range(ts // CHUNK):
            v = vb[n * CHUNK:(n + 1) * CHUNK, j * 128:(j + 1) * 128]
            mixed_scr[n * CHUNK:(n + 1) * CHUNK, j * 128:(j + 1) * 128] = jnp.where(lo, _dot(w0, v), _dot(w1, v)) + bias


def _exchange(srcs, scatter, name):
    n = len(srcs)

    def body(*refs):
        src_refs, out_refs = refs[:n], refs[n:2 * n]
        send_sems, recv_sems, local_sems = refs[2 * n:]
        x, y, c = lax.axis_index("x"), lax.axis_index("y"), lax.axis_index("c")
        me = 4 * x + 2 * y + c

        def peer_of(mask):
            px = x if not (mask >> 2) & 1 else 1 - x
            py = y if not (mask >> 1) & 1 else 1 - y
            pc = c if not mask & 1 else 1 - c
            return (px, py, pc), 4 * px + 2 * py + pc

        def remote(k, mask):
            peer, pidx = peer_of(mask)
            return pltpu.make_async_remote_copy(
                src_ref=src_refs[k].at[pidx] if scatter else src_refs[k],
                dst_ref=out_refs[k].at[me],
                send_sem=send_sems.at[k, mask - 1], recv_sem=recv_sems.at[k, mask - 1],
                device_id=peer, device_id_type=pl.DeviceIdType.MESH)

        def arrival(k, mask):
            peer, pidx = peer_of(mask)
            return pltpu.make_async_remote_copy(
                src_ref=src_refs[k].at[pidx] if scatter else src_refs[k],
                dst_ref=out_refs[k].at[pidx],
                send_sem=send_sems.at[k, mask - 1], recv_sem=recv_sems.at[k, mask - 1],
                device_id=peer, device_id_type=pl.DeviceIdType.MESH)

        sends, locals_ = [], []
        for k in range(n):
            for mask in range(1, N_DEV):
                cp = remote(k, mask)
                cp.start()
                sends.append(cp)
            lc = pltpu.make_async_copy(src_refs[k].at[me] if scatter else src_refs[k], out_refs[k].at[me],
                                       local_sems.at[k])
            lc.start()
            locals_.append(lc)
        for k in range(n):
            for mask in range(1, N_DEV):
                arrival(k, mask).wait_recv()
        for cp in sends:
            cp.wait_send()
        for lc in locals_:
            lc.wait()

    outs = [SDS((N_DEV,) + tuple(s.shape[1:] if scatter else s.shape), s.dtype) for s in srcs]
    hbm = pl.BlockSpec(memory_space=pl.ANY)
    return pl.pallas_call(
        body, name=name, out_shape=outs, in_specs=[hbm] * n, out_specs=[hbm] * n,
        scratch_shapes=[pltpu.SemaphoreType.DMA((n, N_DEV - 1)), pltpu.SemaphoreType.DMA((n, N_DEV - 1)),
                        pltpu.SemaphoreType.DMA((n,))],
    )(*srcs)


def _peer_of(mask):
    x, y, c = lax.axis_index("x"), lax.axis_index("y"), lax.axis_index("c")
    px = 1 - x if (mask >> 2) & 1 else x
    py = 1 - y if (mask >> 1) & 1 else y
    pc = 1 - c if mask & 1 else c
    return (px, py, pc), 4 * px + 2 * py + pc


def _split_copy(src_ref, land_ref, send_sem, recv_sem, mask, scatter, outgoing):
    x, y, c = lax.axis_index("x"), lax.axis_index("y"), lax.axis_index("c")
    me = 4 * x + 2 * y + c
    peer, pidx = _peer_of(mask)
    return pltpu.make_async_remote_copy(
        src_ref=src_ref.at[pidx] if scatter else src_ref,
        dst_ref=land_ref.at[me if outgoing else pidx],
        send_sem=send_sem.at[mask - 1], recv_sem=recv_sem.at[mask - 1],
        device_id=peer, device_id_type=pl.DeviceIdType.MESH)


_HBM = pl.BlockSpec(memory_space=pltpu.HBM)
_SEM = pl.BlockSpec(memory_space=pltpu.SEMAPHORE)
_EFFECT = pltpu.SideEffectType.DATAFLOW_SIDE_EFFECTING


def _exchange_start(srcs, scatter, name):
    n = len(srcs)
    lands = [pltpu.with_memory_space_constraint(
        lax.empty((N_DEV,) + tuple(s.shape[1:] if sc else s.shape), s.dtype), pltpu.HBM)
        for s, sc in zip(srcs, scatter)]
    srcs = [pltpu.with_memory_space_constraint(s, pltpu.HBM) for s in srcs]

    def body(*refs):
        src_refs, land_refs = refs[:n], refs[n:2 * n]
        send_sems, recv_sems = refs[2 * n:3 * n], refs[3 * n:4 * n]
        token = refs[-1]
        for k in range(n):
            for mask in range(1, N_DEV):
                _split_copy(src_refs[k], land_refs[k], send_sems[k], recv_sems[k], mask, scatter[k], True).start()
        token[...] = jnp.zeros_like(token)

    sem = pltpu.SemaphoreType.DMA((N_DEV - 1,))
    out = pl.pallas_call(
        body, name=name,
        out_shape=tuple([sem] * (2 * n) + [pltpu.HBM(s.shape, s.dtype) for s in srcs]
                        + [pltpu.HBM(l.shape, l.dtype) for l in lands] + [SDS((8, 128), F32)]),
        in_specs=[_HBM] * (2 * n),
        out_specs=tuple([_SEM] * (2 * n) + [_HBM] * (2 * n) + [pl.BlockSpec(memory_space=pltpu.VMEM)]),
        input_output_aliases={i: 2 * n + i for i in range(2 * n)},
        compiler_params=pltpu.CompilerParams(has_side_effects=_EFFECT),
    )(*srcs, *lands)
    return out[:n], out[n:2 * n], out[2 * n:3 * n], out[3 * n:4 * n], out[-1]


def _exchange_wait(send_sems, recv_sems, srcs_thru, lands_thru, scatter, after, name):
    n = len(srcs_thru)

    def body(*refs):
        src_refs, land_refs = refs[:n], refs[n:2 * n]
        send_refs, recv_refs = refs[2 * n:3 * n], refs[3 * n:4 * n]
        for k in range(n):
            for mask in range(1, N_DEV):
                _split_copy(src_refs[k], land_refs[k], send_refs[k], recv_refs[k], mask, scatter[k], True).wait_send()
                _split_copy(src_refs[k], land_refs[k], send_refs[k], recv_refs[k], mask, scatter[k], False).wait_recv()

    out = pl.pallas_call(
        body, name=name,
        out_shape=tuple([pltpu.HBM(s.shape, s.dtype) for s in srcs_thru]
                        + [pltpu.HBM(l.shape, l.dtype) for l in lands_thru]),
        in_specs=[_HBM] * (2 * n) + [_SEM] * (2 * n) + [pl.BlockSpec(memory_space=pl.ANY)],
        out_specs=tuple([_HBM] * (2 * n)),
        input_output_aliases={i: i for i in range(2 * n)},
        compiler_params=pltpu.CompilerParams(has_side_effects=_EFFECT),
    )(*srcs_thru, *lands_thru, *send_sems, *recv_sems, after)
    return out[:n], out[n:]


def _with_own(landed, own, me):
    return lax.dynamic_update_slice_in_dim(landed, own[None], me, axis=0)


def _kv_proj(mem, g_mem, wkv3):
    m = mem.shape[0]

    def body(mem_ref, g_ref, w_ref, kv_ref, mn_ref):
        y, _, _ = _rms_fwd(mem_ref[...], g_ref[...])
        yb = _bf(y)
        mn_ref[...] = yb
        for b in range(N_DEV):
            kv_ref[:, 256 * b:256 * (b + 1)] = _dot(yb, w_ref[b])

    return pl.pallas_call(body, name="kv_proj", out_shape=(SDS((m, 2048), F32), SDS(mem.shape, BF16)),
                          compiler_params=_cparams())(mem, g_mem, wkv3)


def _fwd_in(x, g_mix, w_in3, b_in, ts):
    s, d = x.shape

    def body(x_ref, g_ref, w_ref, b_ref, z_ref, a_ref):
        hn, _, _ = _rms_fwd(x_ref[...], g_ref[...])
        hb = _bf(hn)
        for b in range(N_DEV):
            z_ref[:, 256 * b:256 * (b + 1)] = _dot(hb, w_ref[b]) + b_ref[:, 256 * b:256 * (b + 1)]
        a_ref[...] = z_ref[:, 0:512] * jax.nn.sigmoid(z_ref[:, 512:1024])

    return pl.pallas_call(
        body, name="fwd_in", grid=(s // ts,),
        in_specs=[_row(ts, d), _const(g_mix.shape), _const(w_in3.shape), _const(b_in.shape)],
        out_specs=(_row(ts, 2048), _row(ts, 512)),
        out_shape=(SDS((s, 2048), F32), SDS((s, 512), F32)),
        compiler_params=_cparams("arbitrary"))(x, g_mix, w_in3, b_in)


def _conv_fwd(a, w, b):
    s, cw = a.shape
    rc = 256 if s % 256 == 0 else 128

    def body(a_ref, w_ref, b_ref, c_ref, pad):
        pad[0:CONV_PAD, :] = jnp.zeros((CONV_PAD, 128), F32)
        pad[CONV_PAD:, :] = a_ref[...]

        def chunk(i, carry):
            r0 = pl.multiple_of(i * rc, rc)
            win = pad[pl.ds(r0, rc + CONV_PAD), :]
            acc = jnp.zeros((rc, 128), F32) + b_ref[...]
            for k in range(CONV_K):
                off = CONV_PAD - (CONV_K - 1) + k
                acc = acc + w_ref[k:k + 1, :] * win[off:off + rc, :]
            c_ref[pl.ds(r0, rc), :] = acc
            return carry

        lax.fori_loop(0, s // rc, chunk, 0)

    blk = lambda r: pl.BlockSpec((r, 128), lambda j: (0, j))
    return pl.pallas_call(
        body, name="conv_fwd", grid=(cw // 128,),
        in_specs=[blk(s), blk(CONV_K), blk(1)], out_specs=blk(s), out_shape=SDS((s, cw), F32),
        scratch_shapes=[pltpu.VMEM((s + CONV_PAD, 128), F32)],
        compiler_params=_cparams("arbitrary"))(a, w, b)


def _fwd_out(x, c, z, cln_g, cln_b, gln_g, gln_b, ws, bst, w_out, ts):
    s, d = x.shape

    def body(x_ref, c_ref, zuv_ref, clg, clb, glg, glb, ws_ref, bst_ref, wo_ref, h1_ref, mixed_scr):
        cl, _, _ = _ln_fwd(c_ref[...], clg[...], clb[...])
        co = cl * jax.nn.sigmoid(cl)
        u, _ = _gelu(zuv_ref[:, 0:512])
        vg, _ = _gelu(zuv_ref[:, 512:1024])
        vln, _, _ = _ln_fwd(vg, glg[...], glb[...])
        _mix_fwd(_bf(vln), ws_ref, bst_ref, mixed_scr, ts)
        gm = u * mixed_scr[...]
        h1_ref[...] = x_ref[...] + _dot(_bf(co), wo_ref[0:512, :]) + _dot(_bf(gm), wo_ref[512:1024, :])

    return pl.pallas_call(
        body, name="fwd_out", grid=(s // ts,),
        in_specs=[_row(ts, d), _row(ts, 512), _row(ts, 1024, 1), _const(cln_g.shape), _const(cln_b.shape),
                  _const(gln_g.shape), _const(gln_b.shape), _const(ws.shape), _const(bst.shape), _const(w_out.shape)],
        out_specs=_row(ts, d), out_shape=SDS((s, d), F32),
        scratch_shapes=[pltpu.VMEM((ts, 512), F32)],
        compiler_params=_cparams("arbitrary"))(x, c, z, cln_g, cln_b, gln_g, gln_b, ws, bst, w_out)


def _softmax_rows(sc):
    m = jnp.max(sc, axis=-1, keepdims=True)
    e = jnp.exp(sc - m)
    return e / jnp.sum(e, axis=-1, keepdims=True)


def _fwd_xa(h1, g_xa, wq, kv, wo, ts):
    s, d = h1.shape
    scale = XA_DH ** -0.5

    def body(h_ref, g_ref, wq_ref, kv_ref, wo_ref, h2_ref, o_scr):
        hn, _, _ = _rms_fwd(h_ref[...], g_ref[...])
        q = _dot(_bf(hn), wq_ref[...])
        for h in range(XA_HEADS):
            qh = _bf(q[:, XA_DH * h:XA_DH * (h + 1)])
            kh = _bf(kv_ref[:, XA_DH * h:XA_DH * (h + 1)])
            vh = _bf(kv_ref[:, d + XA_DH * h:d + XA_DH * (h + 1)])
            p = _softmax_rows(_dot(qh, kh, NT) * scale)
            o_scr[:, XA_DH * h:XA_DH * (h + 1)] = _dot(_bf(p), vh)
        h2_ref[...] = h_ref[...] + _dot(_bf(o_scr[...]), wo_ref[...])

    return pl.pallas_call(
        body, name="fwd_xa", grid=(s // ts,),
        in_specs=[_row(ts, d), _const(g_xa.shape), _const(wq.shape), _const(kv.shape), _const(wo.shape)],
        out_specs=_row(ts, d), out_shape=SDS((s, d), F32),
        scratch_shapes=[pltpu.VMEM((ts, d), F32)],
        compiler_params=_cparams("arbitrary"))(h1, g_xa, wq, kv, wo)


def _fwd_ffn(h2, g_ffn, wgut, wdown, g_final, target, ts):
    s, d = h2.shape
    hid = wdown.shape[0]
    hc = hid // 2

    def body(h_ref, g_ref, wgu_ref, wd_ref, gf_ref, t_ref, dh3_ref, dh3b_ref, loss_ref, dgf_ref):
        hn, _, _ = _rms_fwd(h_ref[...], g_ref[...])
        hb = _bf(hn)
        h3 = h_ref[...]
        for n in range(2):
            g = _dot(hb, wgu_ref[hc * n:hc * (n + 1), :], NT)
            u = _dot(hb, wgu_ref[hid + hc * n:hid + hc * (n + 1), :], NT)
            act = g * jax.nn.sigmoid(g) * u
            h3 = h3 + _dot(_bf(act), wd_ref[hc * n:hc * (n + 1), :])
        y, xh, r = _rms_fwd(h3, gf_ref[...])
        diff = y - t_ref[...]
        part = 0.5 * jnp.sum(jnp.mean(diff * diff, axis=-1, keepdims=True), axis=0, keepdims=True)
        _accumulate(loss_ref, jnp.zeros(loss_ref.shape, F32) + part)
        dh3, dgf = _rms_bwd(diff * (1.0 / d), xh, r, gf_ref[...])
        dh3_ref[...] = dh3
        dh3b_ref[...] = _bf(dh3)
        _accumulate(dgf_ref, dgf)

    return pl.pallas_call(
        body, name="fwd_ffn", grid=(s // ts,),
        in_specs=[_row(ts, d), _const(g_ffn.shape), _const(wgut.shape), _const(wdown.shape), _const(g_final.shape),
                  _row(ts, d)],
        out_specs=(_row(ts, d), _row(ts, d), _acc((1, 128)), _acc((1, d))),
        out_shape=(SDS((s, d), F32), SDS((s, d), BF16), SDS((1, 128), F32), SDS((1, d), F32)),
        compiler_params=_cparams("arbitrary"))(h2, g_ffn, wgut, wdown, g_final, target)


def _bwd_ffn(h2, dh3, g_ffn, wgut, wdown, ts):
    s, d = h2.shape
    hid = wdown.shape[0]
    hc = hid // 2

    def body(h_ref, dh3_ref, g_ref, wgu_ref, wd_ref, dh2_ref, dh2b_ref, act_ref, dgu_ref, hn_ref, dg_ref):
        hn, xh, r = _rms_fwd(h_ref[...], g_ref[...])
        hb = _bf(hn)
        hn_ref[...] = hb
        db = _bf(dh3_ref[...])
        dhn = jnp.zeros((ts, d), F32)
        for n in range(2):
            wg = wgu_ref[hc * n:hc * (n + 1), :]
            wu = wgu_ref[hid + hc * n:hid + hc * (n + 1), :]
            g = _dot(hb, wg, NT)
            u = _dot(hb, wu, NT)
            sg = jax.nn.sigmoid(g)
            sl = g * sg
            act_ref[:, hc * n:hc * (n + 1)] = _bf(sl * u)
            dact = _dot(db, wd_ref[hc * n:hc * (n + 1), :], NT)
            dgb = _bf(dact * u * _silu_grad(g, sg))
            dub = _bf(dact * sl)
            dgu_ref[:, hc * n:hc * (n + 1)] = dgb
            dgu_ref[:, hid + hc * n:hid + hc * (n + 1)] = dub
            dhn = dhn + _dot(dgb, wg) + _dot(dub, wu)
        dx, dg = _rms_bwd(dhn, xh, r, g_ref[...])
        dh2 = dh3_ref[...] + dx
        dh2_ref[...] = dh2
        dh2b_ref[...] = _bf(dh2)
        _accumulate(dg_ref, dg)

    return pl.pallas_call(
        body, name="bwd_ffn", grid=(s // ts,),
        in_specs=[_row(ts, d), _row(ts, d), _const(g_ffn.shape), _const(wgut.shape), _const(wdown.shape)],
        out_specs=(_row(ts, d), _row(ts, d), _row(ts, hid), _row(ts, 2 * hid), _row(ts, d), _acc((1, d))),
        out_shape=(SDS((s, d), F32), SDS((s, d), BF16), SDS((s, hid), BF16), SDS((s, 2 * hid), BF16),
                   SDS((s, d), BF16), SDS((1, d), F32)),
        compiler_params=_cparams("arbitrary"))(h2, dh3, g_ffn, wgut, wdown)


def _bwd_xa(h1, dh2, g_xa, wq, wo, kv, ts):
    s, d = h1.shape
    scale = XA_DH ** -0.5

    def body(h_ref, dh2_ref, g_ref, wq_ref, wo_ref, kv_ref, dh1_ref, dh1b_ref, dq_ref, o_ref, hn_ref, dkv_ref,
             dg_ref, dq_scr):
        hn, xh, r = _rms_fwd(h_ref[...], g_ref[...])
        hb = _bf(hn)
        hn_ref[...] = hb
        q = _dot(hb, wq_ref[...])
        do = _dot(_bf(dh2_ref[...]), wo_ref[...], NT)

        @pl.when(pl.program_id(0) == 0)
        def _():
            dkv_ref[...] = jnp.zeros_like(dkv_ref)

        for h in range(XA_HEADS):
            lo, hi = XA_DH * h, XA_DH * (h + 1)
            qh = _bf(q[:, lo:hi])
            kh = _bf(kv_ref[:, lo:hi])
            vh = _bf(kv_ref[:, d + lo:d + hi])
            p = _softmax_rows(_dot(qh, kh, NT) * scale)
            pb = _bf(p)
            o_ref[:, lo:hi] = _bf(_dot(pb, vh))
            doh = _bf(do[:, lo:hi])
            dp = _dot(doh, vh, NT)
            ds = p * (dp - jnp.sum(p * dp, axis=-1, keepdims=True)) * scale
            dsb = _bf(ds)
            dq_scr[:, lo:hi] = _dot(dsb, kh)
            dkv_ref[:, lo:hi] += _dot(dsb, qh, TN)
            dkv_ref[:, d + lo:d + hi] += _dot(pb, doh, TN)
        dqb = _bf(dq_scr[...])
        dq_ref[...] = dqb
        dx, dg = _rms_bwd(_dot(dqb, wq_ref[...], NT), xh, r, g_ref[...])
        dh1 = dh2_ref[...] + dx
        dh1_ref[...] = dh1
        dh1b_ref[...] = _bf(dh1)
        _accumulate(dg_ref, dg)

    return pl.pallas_call(
        body, name="bwd_xa", grid=(s // ts,),
        in_specs=[_row(ts, d), _row(ts, d), _const(g_xa.shape), _const(wq.shape), _const(wo.shape), _const(kv.shape)],
        out_specs=(_row(ts, d), _row(ts, d), _row(ts, d), _row(ts, d), _row(ts, d), _acc(kv.shape), _acc((1, d))),
        out_shape=(SDS((s, d), F32), SDS((s, d), BF16), SDS((s, d), BF16), SDS((s, d), BF16), SDS((s, d), BF16),
                   SDS(kv.shape, F32), SDS((1, d), F32)),
        scratch_shapes=[pltpu.VMEM((ts, d), F32)],
        compiler_params=_cparams("arbitrary"))(h1, dh2, g_xa, wq, wo, kv)


def _bwd_kv(dkv, mn, mem, g_mem, wkv3):
    d = mem.shape[1]

    def body(dkv_ref, mn_ref, mem_ref, g_ref, w_ref, dw_ref, dg_ref):
        dkvb = _bf(dkv_ref[...])
        dmn = jnp.zeros(mem_ref.shape, F32)
        for b in range(N_DEV):
            blk = dkvb[:, 256 * b:256 * (b + 1)]
            dmn = dmn + _dot(blk, w_ref[b], NT)
            dw_ref[b] = _bf(_dot(mn_ref[...], blk, TN))
        _, xh, r = _rms_fwd(mem_ref[...], g_ref[...])
        _, dg = _rms_bwd(dmn, xh, r, g_ref[...])
        dg_ref[...] = dg

    return pl.pallas_call(body, name="bwd_kv", out_shape=(SDS(wkv3.shape, BF16), SDS((1, d), F32)),
                          compiler_params=_cparams())(dkv, mn, mem, g_mem, wkv3)


def _bwd_out(dh1, c, z, cln_g, cln_b, gln_g, gln_b, ws, wst, bst, w_out, ts):
    s, d = dh1.shape
    nh = GM_HEADS

    def body(dh1_ref, c_ref, zuv_ref, clg, clb, glg, glb, ws_ref, wst_ref, bst_ref, wo_ref,
             cat_ref, dc_ref, dzuv_ref, dws_ref, dbst_ref, dclg_ref, dclb_ref, dglg_ref, dglb_ref, dbin_ref,
             mixed_scr, dv_scr):
        cl, chat, crs = _ln_fwd(c_ref[...], clg[...], clb[...])
        sg = jax.nn.sigmoid(cl)
        zu = zuv_ref[:, 0:512]
        zv = zuv_ref[:, 512:1024]
        u, tu = _gelu(zu)
        vg, tv = _gelu(zv)
        vln, vhat, vrs = _ln_fwd(vg, glg[...], glb[...])
        vb = _bf(vln)
        _mix_fwd(vb, ws_ref, bst_ref, mixed_scr, ts)
        mixed = mixed_scr[...]
        cat_ref[:, 0:512] = _bf(cl * sg)
        cat_ref[:, 512:1024] = _bf(u * mixed)
        dcat = _dot(_bf(dh1_ref[...]), wo_ref[...], NT)
        dgm = dcat[:, 512:1024]
        dc, dclg, dclb = _ln_bwd(dcat[:, 0:512] * _silu_grad(cl, sg), chat, crs, clg[...])
        dc_ref[...] = dc
        dzu = dgm * mixed * _gelu_grad(zu, tu)
        dm = dgm * u

        @pl.when(pl.program_id(0) == 0)
        def _():
            dws_ref[...] = jnp.zeros_like(dws_ref)
            dbst_ref[...] = jnp.zeros_like(dbst_ref)

        tril, triu, lo = _mix_masks()
        head = lax.broadcasted_iota(jnp.int32, (1, nh), 1)
        for j in range(nh // 2):
            w0t = _bf(jnp.where(triu, wst_ref[2 * j], 0.0))
            w1t = _bf(jnp.where(triu, wst_ref[2 * j + 1], 0.0))
            for n in range(ts // CHUNK):
                rows = slice(n * CHUNK, (n + 1) * CHUNK)
                lanes = slice(j * 128, (j + 1) * 128)
                dmc = dm[rows, lanes]
                dmb = _bf(dmc)
                dv_scr[rows, lanes] = jnp.where(lo, _dot(w0t, dmb), _dot(w1t, dmb))
                vc = vb[rows, lanes]
                d0 = jnp.where(lo, dmc, 0.0)
                d1 = dmc - d0
                dws_ref[2 * j] += jnp.where(tril, _dot(_bf(d0), vc, NT), 0.0)
                dws_ref[2 * j + 1] += jnp.where(tril, _dot(_bf(d1), vc, NT), 0.0)
                dbst_ref[...] += (jnp.sum(d0, axis=1, keepdims=True) * (head == 2 * j).astype(F32)
                                  + jnp.sum(d1, axis=1, keepdims=True) * (head == 2 * j + 1).astype(F32))
        dvg, dglg, dglb = _ln_bwd(dv_scr[...], vhat, vrs, glg[...])
        dzv = dvg * _gelu_grad(zv, tv)
        dzuv_ref[:, 0:512] = _bf(dzu)
        dzuv_ref[:, 512:1024] = _bf(dzv)
        _accumulate(dclg_ref, dclg)
        _accumulate(dclb_ref, dclb)
        _accumulate(dglg_ref, dglg)
        _accumulate(dglb_ref, dglb)
        _accumulate(dbin_ref, jnp.concatenate([jnp.sum(dzu, axis=0, keepdims=True),
                                               jnp.sum(dzv, axis=0, keepdims=True)], axis=1))

    vec = (1, 512)
    return pl.pallas_call(
        body, name="bwd_out", grid=(s // ts,),
        in_specs=[_row(ts, d), _row(ts, 512), _row(ts, 1024, 1), _const(cln_g.shape), _const(cln_b.shape),
                  _const(gln_g.shape), _const(gln_b.shape), _const(ws.shape), _const(wst.shape), _const(bst.shape),
                  _const(w_out.shape)],
        out_specs=(_row(ts, d), _row(ts, 512), _row(ts, 1024), _acc(ws.shape), _acc(bst.shape), _acc(vec), _acc(vec),
                   _acc(vec), _acc(vec), _acc((1, 1024))),
        out_shape=(SDS((s, d), BF16), SDS((s, 512), F32), SDS((s, 1024), BF16), SDS(ws.shape, F32),
                   SDS(bst.shape, F32), SDS(vec, F32), SDS(vec, F32), SDS(vec, F32), SDS(vec, F32), SDS((1, 1024), F32)),
        scratch_shapes=[pltpu.VMEM((ts, 512), F32), pltpu.VMEM((ts, 512), F32)],
        compiler_params=_cparams("arbitrary"))(dh1, c, z, cln_g, cln_b, gln_g, gln_b, ws, wst, bst, w_out)


def _conv_bwd(dc, a, w):
    s, cw = a.shape
    rc = 256 if s % 256 == 0 else 128

    def body(dc_ref, a_ref, w_ref, da_ref, dw_ref, db_ref, pad_a, pad_d):
        pad_a[0:CONV_PAD, :] = jnp.zeros((CONV_PAD, 128), F32)
        pad_a[CONV_PAD:, :] = a_ref[...]
        pad_d[0:s, :] = dc_ref[...]
        pad_d[s:, :] = jnp.zeros((CONV_PAD, 128), F32)
        dw_ref[...] = jnp.zeros_like(dw_ref)
        db_ref[...] = jnp.zeros_like(db_ref)

        def chunk(i, carry):
            r0 = pl.multiple_of(i * rc, rc)
            win_d = pad_d[pl.ds(r0, rc + CONV_PAD), :]
            win_a = pad_a[pl.ds(r0, rc + CONV_PAD), :]
            dcc = win_d[0:rc, :]
            acc = jnp.zeros((rc, 128), F32)
            for k in range(CONV_K):
                acc = acc + w_ref[k:k + 1, :] * win_d[CONV_K - 1 - k:CONV_K - 1 - k + rc, :]
                off = CONV_PAD - (CONV_K - 1) + k
                dw_ref[k:k + 1, :] += jnp.sum(dcc * win_a[off:off + rc, :], axis=0, keepdims=True)
            da_ref[pl.ds(r0, rc), :] = acc
            db_ref[...] += jnp.sum(dcc, axis=0, keepdims=True)
            return carry

        lax.fori_loop(0, s // rc, chunk, 0)

    blk = lambda r: pl.BlockSpec((r, 128), lambda j: (0, j))
    return pl.pallas_call(
        body, name="conv_bwd", grid=(cw // 128,),
        in_specs=[blk(s), blk(s), blk(CONV_K)], out_specs=(blk(s), blk(CONV_PAD), blk(1)),
        out_shape=(SDS((s, cw), F32), SDS((CONV_PAD, cw), F32), SDS((1, cw), F32)),
        scratch_shapes=[pltpu.VMEM((s + CONV_PAD, 128), F32), pltpu.VMEM((s + CONV_PAD, 128), F32)],
        compiler_params=_cparams("arbitrary"))(dc, a, w)


def _bwd_in(x, dh1, da, z, dzuv, g_mix, w_in3, ts):
    s, d = x.shape

    def body(x_ref, dh1_ref, da_ref, zag_ref, dzuv_ref, g_ref, w_ref, dx_ref, dz_ref, hn_ref, dbin_ref, dg_ref):
        za = zag_ref[:, 0:512]
        sg = jax.nn.sigmoid(zag_ref[:, 512:1024])
        da_ = da_ref[...]
        dza = da_ * sg
        dzg = da_ * za * sg * (1.0 - sg)
        dz_ref[:, 0:512] = _bf(dza)
        dz_ref[:, 512:1024] = _bf(dzg)
        dz_ref[:, 1024:2048] = dzuv_ref[...]
        dhn = jnp.zeros((ts, d), F32)
        for b in range(N_DEV):
            dhn = dhn + _dot(dz_ref[:, 256 * b:256 * (b + 1)], w_ref[b], NT)
        hn, xh, r = _rms_fwd(x_ref[...], g_ref[...])
        hn_ref[...] = _bf(hn)
        dxn, dg = _rms_bwd(dhn, xh, r, g_ref[...])
        dx_ref[...] = dh1_ref[...] + dxn
        _accumulate(dg_ref, dg)
        _accumulate(dbin_ref, jnp.concatenate([jnp.sum(dza, axis=0, keepdims=True),
                                               jnp.sum(dzg, axis=0, keepdims=True)], axis=1))

    return pl.pallas_call(
        body, name="bwd_in", grid=(s // ts,),
        in_specs=[_row(ts, d), _row(ts, d), _row(ts, 512), _row(ts, 1024, 0), _row(ts, 1024), _const(g_mix.shape),
                  _const(w_in3.shape)],
        out_specs=(_row(ts, d), _row(ts, 2048), _row(ts, d), _acc((1, 1024)), _acc((1, d))),
        out_shape=(SDS((s, d), F32), SDS((s, 2048), BF16), SDS((s, d), BF16), SDS((1, 1024), F32), SDS((1, d), F32)),
        compiler_params=_cparams("arbitrary"))(x, dh1, da, z, dzuv, g_mix, w_in3)


def _tn_matmul(a, b, tm, name, col_blocks=False, behind=None):
    s, m = a.shape
    n = b.shape[1]
    cb = 256
    tn = 2 * cb if col_blocks else n
    if col_blocks:
        tm = m
    ts = min(s, 512)
    n_s = s // ts

    def body(a_ref, b_ref, *rest):
        o_ref, acc = rest[-2:]
        k = pl.program_id(2)

        @pl.when(k == 0)
        def _():
            acc[...] = jnp.zeros_like(acc)

        acc[...] += _dot(_bf(a_ref[...]), _bf(b_ref[...]), TN)

        @pl.when(k == n_s - 1)
        def _():
            if col_blocks:
                for blk in range(tn // cb):
                    o_ref[blk] = _bf(acc[:, cb * blk:cb * (blk + 1)])
            else:
                o_ref[...] = _bf(acc[...])

    if col_blocks:
        out_shape = SDS((n // cb, m, cb), BF16)
        out_spec = pl.BlockSpec((tn // cb, m, cb), lambda i, j, k: (j, 0, 0))
    else:
        out_shape = SDS((m, n), BF16)
        out_spec = pl.BlockSpec((tm, tn), lambda i, j, k: (i, j))
    extra = [] if behind is None else [behind]
    return pl.pallas_call(
        body, name=name, grid=(m // tm, n // tn, n_s),
        in_specs=[pl.BlockSpec((ts, tm), lambda i, j, k: (k, i)), pl.BlockSpec((ts, tn), lambda i, j, k: (k, j))]
        + [pl.BlockSpec(memory_space=pl.ANY)] * len(extra),
        out_specs=out_spec, out_shape=out_shape,
        scratch_shapes=[pltpu.VMEM((tm, tn), F32)],
        compiler_params=_cparams("parallel", "parallel", "arbitrary"))(a, b, *extra)


def _sum8(parts, name):
    _, r, c = parts.shape
    tr = r
    for cand in (512, 256, 352, 128, 8):
        if r % cand == 0 and cand <= r:
            tr = cand
            break

    def body(p_ref, o_ref):
        acc = p_ref[0].astype(F32)
        for dev in range(1, N_DEV):
            acc = acc + p_ref[dev].astype(F32)
        o_ref[...] = acc

    return pl.pallas_call(
        body, name=name, grid=(r // tr,),
        in_specs=[pl.BlockSpec((N_DEV, tr, c), lambda i: (0, i, 0))],
        out_specs=pl.BlockSpec((tr, c), lambda i: (i, 0)), out_shape=SDS((r, c), F32),
        compiler_params=_cparams("parallel"))(parts)


def _adamw(w, g, m, v, name):
    r, c = w.shape
    tr = r
    for cand in (256, 128, 176, 8):
        if r % cand == 0 and cand <= r:
            tr = cand
            break

    def body(w_ref, g_ref, m_ref, v_ref, d_ref, m2_ref, v2_ref):
        g_ = g_ref[...]
        m2 = ADAM_B1 * m_ref[...] + (1.0 - ADAM_B1) * g_
        v2 = ADAM_B2 * v_ref[...] + (1.0 - ADAM_B2) * (g_ * g_)
        m_hat = m2 / (1.0 - ADAM_B1 ** ADAM_STEP)
        v_hat = v2 / (1.0 - ADAM_B2 ** ADAM_STEP)
        d_ref[...] = -ADAM_LR * (m_hat / (jnp.sqrt(v_hat) + ADAM_EPS) + ADAM_WD * w_ref[...])
        m2_ref[...] = m2
        v2_ref[...] = v2

    blk = pl.BlockSpec((tr, c), lambda i: (i, 0))
    return pl.pallas_call(
        body, name=name, grid=(r // tr,), in_specs=[blk] * 4, out_specs=(blk,) * 3,
        out_shape=(SDS((r, c), F32),) * 3, compiler_params=_cparams("parallel"))(w, g, m, v)


def _pack(arrs):
    flat = jnp.concatenate([a.reshape(-1) for a in arrs])
    pad = (-flat.shape[0]) % (128 * 128)
    return jnp.pad(flat, (0, pad)).reshape(-1, 128)


def _unpack(packed, shapes):
    flat = packed.reshape(-1)
    out, off = [], 0
    for shp in shapes:
        size = 1
        for dim in shp:
            size *= dim
        out.append(flat[off:off + size].reshape(shp))
        off += size
    return out


SMALL = ("norm_mix_g", "b_in", "conv_w", "conv_b", "conv_ln_g", "conv_ln_b", "gm_ln_g", "gm_ln_b", "gm_w_s", "gm_b_s",
         "norm_xa_g", "mem_norm_g", "norm_ffn_g", "final_norm_g")
BIG = ("w_in", "w_out", "xa_wq", "xa_wkv", "xa_wo", "ffn_w_gate_up", "ffn_w_down")
WEIGHTS = ("norm_mix_g", "w_in", "b_in", "conv_w", "conv_b", "conv_ln_g", "conv_ln_b", "gm_ln_g", "gm_ln_b", "gm_w_s",
           "gm_b_s", "w_out", "norm_xa_g", "mem_norm_g", "xa_wq", "xa_wkv", "xa_wo", "norm_ffn_g", "ffn_w_gate_up",
           "ffn_w_down", "final_norm_g")


def kernel(x, mem, norm_mix_g, w_in, b_in, conv_w, conv_b, conv_ln_g, conv_ln_b, gm_ln_g, gm_ln_b, gm_w_s, gm_b_s, w_out, norm_xa_g, mem_norm_g, xa_wq, xa_wkv, xa_wo, norm_ffn_g, ffn_w_gate_up, ffn_w_down, final_norm_g, loss_target, m_norm_mix_g, m_w_in, m_b_in, m_conv_w, m_conv_b, m_conv_ln_g, m_conv_ln_b, m_gm_ln_g, m_gm_ln_b, m_gm_w_s, m_gm_b_s, m_w_out, m_norm_xa_g, m_mem_norm_g, m_xa_wq, m_xa_wkv, m_xa_wo, m_norm_ffn_g, m_ffn_w_gate_up, m_ffn_w_down, m_final_norm_g, v_norm_mix_g, v_w_in, v_b_in, v_conv_w, v_conv_b, v_conv_ln_g, v_conv_ln_b, v_gm_ln_g, v_gm_ln_b, v_gm_w_s, v_gm_b_s, v_w_out, v_norm_xa_g, v_mem_norm_g, v_xa_wq, v_xa_wkv, v_xa_wo, v_norm_ffn_g, v_ffn_w_gate_up, v_ffn_w_down, v_final_norm_g):
    w = dict(norm_mix_g=norm_mix_g, w_in=w_in, b_in=b_in, conv_w=conv_w, conv_b=conv_b, conv_ln_g=conv_ln_g,
             conv_ln_b=conv_ln_b, gm_ln_g=gm_ln_g, gm_ln_b=gm_ln_b, gm_w_s=gm_w_s, gm_b_s=gm_b_s, w_out=w_out,
             norm_xa_g=norm_xa_g, mem_norm_g=mem_norm_g, xa_wq=xa_wq, xa_wkv=xa_wkv, xa_wo=xa_wo,
             norm_ffn_g=norm_ffn_g, ffn_w_gate_up=ffn_w_gate_up, ffn_w_down=ffn_w_down, final_norm_g=final_norm_g)
    mom = dict(norm_mix_g=m_norm_mix_g, w_in=m_w_in, b_in=m_b_in, conv_w=m_conv_w, conv_b=m_conv_b,
               conv_ln_g=m_conv_ln_g, conv_ln_b=m_conv_ln_b, gm_ln_g=m_gm_ln_g, gm_ln_b=m_gm_ln_b, gm_w_s=m_gm_w_s,
               gm_b_s=m_gm_b_s, w_out=m_w_out, norm_xa_g=m_norm_xa_g, mem_norm_g=m_mem_norm_g, xa_wq=m_xa_wq,
               xa_wkv=m_xa_wkv, xa_wo=m_xa_wo, norm_ffn_g=m_norm_ffn_g, ffn_w_gate_up=m_ffn_w_gate_up,
               ffn_w_down=m_ffn_w_down, final_norm_g=m_final_norm_g)
    var = dict(norm_mix_g=v_norm_mix_g, w_in=v_w_in, b_in=v_b_in, conv_w=v_conv_w, conv_b=v_conv_b,
               conv_ln_g=v_conv_ln_g, conv_ln_b=v_conv_ln_b, gm_ln_g=v_gm_ln_g, gm_ln_b=v_gm_ln_b, gm_w_s=v_gm_w_s,
               gm_b_s=v_gm_b_s, w_out=v_w_out, norm_xa_g=v_norm_xa_g, mem_norm_g=v_mem_norm_g, xa_wq=v_xa_wq,
               xa_wkv=v_xa_wkv, xa_wo=v_xa_wo, norm_ffn_g=v_norm_ffn_g, ffn_w_gate_up=v_ffn_w_gate_up,
               ffn_w_down=v_ffn_w_down, final_norm_g=v_final_norm_g)

    me = 4 * lax.axis_index("x") + 2 * lax.axis_index("y") + lax.axis_index("c")
    s, d = x.shape[1], x.shape[2]
    xs = x.reshape(s, d)
    mems = mem.reshape(mem.shape[1], d)
    tgt = loss_target.reshape(s, d)
    ts = min(512, s)
    ts_ffn = min(256, s)
    row = lambda a: a.reshape(1, -1)

    conv_w_pad = jnp.pad(conv_w, ((0, CONV_PAD - CONV_K), (0, 128 - conv_w.shape[1])))
    shards = [_bf(w_in), conv_w_pad, _bf(xa_wkv), _bf(w_out), _bf(xa_wq), _bf(xa_wo), _bf(ffn_w_gate_up.T),
              _bf(ffn_w_down)]
    g_send, g_recv, g_src, g_land, g_tok = _exchange_start(shards, [False] * len(shards), "gather_start")

    def gathered(idx, after, name):
        pick = lambda seq: [seq[i] for i in idx]
        srcs, lands = _exchange_wait(pick(g_send), pick(g_recv), pick(g_src), pick(g_land), [False] * len(idx), after,
                                     name)
        return [_with_own(land, src, me) for src, land in zip(srcs, lands)]

    bst = gm_b_s.T
    wst = jnp.swapaxes(gm_w_s, 1, 2)

    w_in3, conv_w8 = gathered((0, 1), g_tok, "gather_wait_in")
    conv_w_f = conv_w8[:, :CONV_K, :conv_w.shape[1]].transpose(1, 0, 2).reshape(CONV_K, -1)
    cw = conv_w_f.shape[1]
    z, a = _fwd_in(xs, row(norm_mix_g), w_in3, row(b_in), ts)
    c = _conv_fwd(a, conv_w_f, row(conv_b))
    wkv3, w_out3 = gathered((2, 3), c, "gather_wait_out")
    w_out_f = w_out3.reshape(-1, d)
    kv, mn = _kv_proj(mems, row(mem_norm_g), wkv3)
    h1 = _fwd_out(xs, c, z, row(conv_ln_g), row(conv_ln_b), row(gm_ln_g), row(gm_ln_b), gm_w_s, bst, w_out_f, ts)
    wq3, wo3 = gathered((4, 5), h1, "gather_wait_xa")
    wq_f = wq3.reshape(-1, d)
    wo_f = wo3.reshape(-1, d)
    h2 = _fwd_xa(h1, row(norm_xa_g), wq_f, kv, wo_f, ts)
    wgut3, wdown3 = gathered((6, 7), h2, "gather_wait_ffn")
    wgut_f = wgut3.reshape(-1, d)
    wdown_f = wdown3.reshape(-1, d)
    dh3, dh3b, loss_p, d_final_g = _fwd_ffn(h2, row(norm_ffn_g), wgut_f, wdown_f, row(final_norm_g), tgt, ts_ffn)

    blocks = lambda m: m.reshape(N_DEV, -1, d)
    tok = lambda ex: ex[4][0:1, 0:1]
    dh2, dh2b, act, dgu, hn2, d_ffn_g = _bwd_ffn(h2, dh3, row(norm_ffn_g), wgut_f, wdown_f, ts_ffn)
    dwgut = _tn_matmul(dgu, hn2, 1408, "dw_gate_up")
    dwdown = _tn_matmul(act, dh3b, 1408, "dw_down")
    ex_ffn = _exchange_start([blocks(dwgut), blocks(dwdown)], [True, True], "scatter_start_ffn")
    dh1, dh1b, dq, o, hn1, dkv, d_xa_g = _bwd_xa(h1, dh2, row(norm_xa_g) + tok(ex_ffn), wq_f, wo_f, kv, ts)
    dwq = _tn_matmul(hn1, dq, 1024, "dw_q")
    dwo = _tn_matmul(o, dh2b, 1024, "dw_o")
    dwkv3, d_mem_g = _bwd_kv(dkv, mn, mems, row(mem_norm_g), wkv3)
    ex_xa = _exchange_start([blocks(dwq), blocks(dwo), dwkv3], [True] * 3, "scatter_start_xa")
    (cat, dc, dzuv, dws, dbst, d_cln_g, d_cln_b, d_gln_g, d_gln_b, dbin_uv) = _bwd_out(
        dh1, c, z, row(conv_ln_g) + tok(ex_xa), row(conv_ln_b), row(gm_ln_g), row(gm_ln_b), gm_w_s, wst, bst,
        w_out_f, ts)
    dwout = _tn_matmul(cat, dh1b, 1024, "dw_out")
    early = dict(b_in_uv=dbin_uv, conv_ln_g=d_cln_g, conv_ln_b=d_cln_b, gm_ln_g=d_gln_g, gm_ln_b=d_gln_b, gm_w_s=dws,
                 gm_b_s=dbst.T, norm_xa_g=d_xa_g, mem_norm_g=d_mem_g, norm_ffn_g=d_ffn_g, final_norm_g=d_final_g,
                 loss=loss_p)
    ex_out = _exchange_start([blocks(dwout), _pack(list(early.values()))], [True, False], "scatter_start_out")
    da, dconv_w, dconv_b = _conv_bwd(dc, a, conv_w_f + tok(ex_out))
    dx, dz, hn0, dbin_ag, d_mix_g = _bwd_in(xs, dh1, da, z, dzuv, row(norm_mix_g), w_in3, ts)
    late = dict(norm_mix_g=d_mix_g, b_in_ag=dbin_ag, conv_w=dconv_w[:CONV_K], conv_b=dconv_b)
    ex_late = _exchange_start([_pack(list(late.values()))], [False], "gather_start_late")
    dw_in3 = _tn_matmul(hn0, dz, 0, "dw_in", col_blocks=True, behind=ex_late[4])
    ex_in = _exchange_start([dw_in3], [True], "scatter_start_in")

    def small_totals(ex, k, after, name):
        srcs, lands = _exchange_wait([ex[0][k]], [ex[1][k]], [ex[2][k]], [ex[3][k]], [False], after, name)
        return _sum8(_with_own(lands[0], srcs[0], me), "sum_" + name)

    early_tot = small_totals(ex_out, 1, ex_in[4], "small_early")
    late_tot = small_totals(ex_late, 0, early_tot, "small_late")
    early_g = dict(zip(early, _unpack(early_tot, [v.shape for v in early.values()])))
    late_g = dict(zip(late, _unpack(late_tot, [v.shape for v in late.values()])))
    loss = early_g["loss"][0, 0]
    grads = {}
    for nm in SMALL:
        if nm == "b_in":
            g = jnp.concatenate([late_g["b_in_ag"], early_g["b_in_uv"]], axis=1)
        elif nm == "conv_w":
            g = lax.dynamic_slice_in_dim(late_g[nm], me * conv_w.shape[1], conv_w.shape[1], axis=1)
        else:
            g = late_g[nm] if nm in late_g else early_g[nm]
        grads[nm] = g.reshape(w[nm].shape)

    delta, new_m, new_v = {}, {}, {}
    small_shapes = [w[nm].shape for nm in SMALL]
    sd, sm, sv = _adamw(_pack([w[nm] for nm in SMALL]), _pack([grads[nm] for nm in SMALL]),
                        _pack([mom[nm] for nm in SMALL]), _pack([var[nm] for nm in SMALL]), "adamw_small")
    for packed_out, dst in ((sd, delta), (sm, new_m), (sv, new_v)):
        for nm, val in zip(SMALL, _unpack(packed_out, small_shapes)):
            dst[nm] = val

    def reduced(srcs, lands, names):
        for nm, src, land in zip(names, srcs, lands):
            own = lax.dynamic_index_in_dim(src, me, axis=0, keepdims=False)
            g = _sum8(_with_own(land, own, me), "sum_" + nm)
            grads[nm] = g.T if nm == "ffn_w_gate_up" else g
            delta[nm], new_m[nm], new_v[nm] = _adamw(w[nm], grads[nm], mom[nm], var[nm], "adamw_" + nm)
        return new_v[names[-1]]

    after = sd
    for ex, names, tag in ((ex_ffn, ("ffn_w_gate_up", "ffn_w_down"), "ffn"), (ex_xa, ("xa_wq", "xa_wo", "xa_wkv"), "xa")):
        srcs, lands = _exchange_wait(ex[0], ex[1], ex[2], ex[3], [True] * len(names), after, "scatter_wait_" + tag)
        after = reduced(srcs, lands, names)
    srcs, lands = _exchange_wait([ex_out[0][0]], [ex_out[1][0]], [ex_out[2][0]], [ex_out[3][0]], [True], after,
                                 "scatter_wait_out")
    after = reduced(srcs, lands, ("w_out",))
    srcs, lands = _exchange_wait(ex_in[0], ex_in[1], ex_in[2], ex_in[3], [True], after, "scatter_wait_in")
    reduced(srcs, lands, ("w_in",))

    return (loss, dx.reshape(x.shape), *[grads[nm] for nm in WEIGHTS], *[delta[nm] for nm in WEIGHTS],
            *[new_m[nm] for nm in WEIGHTS], *[new_v[nm] for nm in WEIGHTS])
```

```python
import functools

import jax
import jax.numpy as jnp
from jax import lax
from jax.experimental import pallas as pl
from jax.experimental.pallas import tpu as pltpu

F32 = jnp.float32
BF16 = jnp.bfloat16
SDS = jax.ShapeDtypeStruct

N_DEV = 8
RMS_EPS = 1e-6
LN_EPS = 1e-5
CONV_K = 31
CONV_PAD = 32
CHUNK = 128
GM_HEADS = 8
XA_HEADS = 4
XA_DH = 256
GELU_K0 = 0.7978845608028654
GELU_K1 = 0.044715
ADAM_LR = 0.001
ADAM_B1 = 0.9
ADAM_B2 = 0.999
ADAM_EPS = 1e-08
ADAM_WD = 0.01
ADAM_STEP = 10
VMEM_LIMIT = 60 * 1024 * 1024

NN = (((1,), (0,)), ((), ()))
NT = (((1,), (1,)), ((), ()))
TN = (((0,), (0,)), ((), ()))


def _dot(a, b, dims=NN):
    return lax.dot_general(a, b, dims, preferred_element_type=F32)


def _bf(x):
    return x.astype(BF16)


def _cparams(*sem):
    return pltpu.CompilerParams(dimension_semantics=tuple(sem) if sem else None, vmem_limit_bytes=VMEM_LIMIT)


def _row(ts, w, col=0):
    return pl.BlockSpec((ts, w), lambda i: (i, col))


def _const(shape):
    nd = len(shape)
    return pl.BlockSpec(shape, lambda i: (0,) * nd, pipeline_mode=pl.Buffered(1))


def _acc(shape):
    nd = len(shape)
    return pl.BlockSpec(shape, lambda i: (0,) * nd)


def _rms_fwd(x, g):
    r = lax.rsqrt(jnp.mean(x * x, axis=-1, keepdims=True) + RMS_EPS)
    xh = x * r
    return xh * g, xh, r


def _rms_bwd(dy, xh, r, g):
    gdy = dy * g
    dx = r * (gdy - xh * jnp.mean(gdy * xh, axis=-1, keepdims=True))
    dg = jnp.sum(dy * xh, axis=0, keepdims=True)
    return dx, dg


def _ln_fwd(x, g, b):
    mu = jnp.mean(x, axis=-1, keepdims=True)
    xc = x - mu
    rs = lax.rsqrt(jnp.mean(xc * xc, axis=-1, keepdims=True) + LN_EPS)
    xh = xc * rs
    return xh * g + b, xh, rs


def _ln_bwd(dy, xh, rs, g):
    dxh = dy * g
    dx = rs * (dxh - jnp.mean(dxh, axis=-1, keepdims=True) - xh * jnp.mean(dxh * xh, axis=-1, keepdims=True))
    return dx, jnp.sum(dy * xh, axis=0, keepdims=True), jnp.sum(dy, axis=0, keepdims=True)


def _gelu(x):
    t = jnp.tanh(GELU_K0 * (x + GELU_K1 * (x * x * x)))
    return 0.5 * x * (1.0 + t), t


def _gelu_grad(x, t):
    return 0.5 * (1.0 + t) + 0.5 * x * (1.0 - t * t) * (GELU_K0 * (1.0 + 3.0 * GELU_K1 * x * x))


def _silu_grad(x, sg):
    return sg * (1.0 + x * (1.0 - sg))


def _accumulate(ref, val):
    @pl.when(pl.program_id(0) == 0)
    def _():
        ref[...] = jnp.zeros_like(ref)
    ref[...] += val


def _mix_masks():
    row = lax.broadcasted_iota(jnp.int32, (CHUNK, CHUNK), 0)
    col = lax.broadcasted_iota(jnp.int32, (CHUNK, CHUNK), 1)
    return row >= col, row <= col, col < (CHUNK // 2)


def _mix_fwd(vb, ws_ref, bst_ref, mixed_scr, ts):
    tril, _, lo = _mix_masks()
    for j in range(GM_HEADS // 2):
        w0 = _bf(jnp.where(tril, ws_ref[2 * j], 0.0))
        w1 = _bf(jnp.where(tril, ws_ref[2 * j + 1], 0.0))
        bias = jnp.where(lo, bst_ref[:, 2 * j:2 * j + 1], bst_ref[:, 2 * j + 1:2 * j + 2])
        for n in range(ts // CHUNK):
            v = vb[n * CHUNK:(n + 1) * CHUNK, j * 128:(j + 1) * 128]
            mixed_scr[n * CHUNK:(n + 1) * CHUNK, j * 128:(j + 1) * 128] = jnp.where(lo, _dot(w0, v), _dot(w1, v)) + bias


def _exchange(srcs, scatter, name):
    n = len(srcs)

    def body(*refs):
        src_refs, out_refs = refs[:n], refs[n:2 * n]
        send_sems, recv_sems, local_sems = refs[2 * n:]
        x, y, c = lax.axis_index("x"), lax.axis_index("y"), lax.axis_index("c")
        me = 4 * x + 2 * y + c

        def peer_of(mask):
            px = x if not (mask >> 2) & 1 else 1 - x
            py = y if not (mask >> 1) & 1 else 1 - y
            pc = c if not mask & 1 else 1 - c
            return (px, py, pc), 4 * px + 2 * py + pc

        def remote(k, mask):
            peer, pidx = peer_of(mask)
            return pltpu.make_async_remote_copy(
                src_ref=src_refs[k].at[pidx] if scatter else src_refs[k],
                dst_ref=out_refs[k].at[me],
                send_sem=send_sems.at[k, mask - 1], recv_sem=recv_sems.at[k, mask - 1],
                device_id=peer, device_id_type=pl.DeviceIdType.MESH)

        def arrival(k, mask):
            peer, pidx = peer_of(mask)
            return pltpu.make_async_remote_copy(
                src_ref=src_refs[k].at[pidx] if scatter else src_refs[k],
                dst_ref=out_refs[k].at[pidx],
                send_sem=send_sems.at[k, mask - 1], recv_sem=recv_sems.at[k, mask - 1],
                device_id=peer, device_id_type=pl.DeviceIdType.MESH)

        sends, locals_ = [], []
        for k in range(n):
            for mask in range(1, N_DEV):
                cp = remote(k, mask)
                cp.start()
                sends.append(cp)
            lc = pltpu.make_async_copy(src_refs[k].at[me] if scatter else src_refs[k], out_refs[k].at[me],
                                       local_sems.at[k])
            lc.start()
            locals_.append(lc)
        for k in range(n):
            for mask in range(1, N_DEV):
                arrival(k, mask).wait_recv()
        for cp in sends:
            cp.wait_send()
        for lc in locals_:
            lc.wait()

    outs = [SDS((N_DEV,) + tuple(s.shape[1:] if scatter else s.shape), s.dtype) for s in srcs]
    hbm = pl.BlockSpec(memory_space=pl.ANY)
    return pl.pallas_call(
        body, name=name, out_shape=outs, in_specs=[hbm] * n, out_specs=[hbm] * n,
        scratch_shapes=[pltpu.SemaphoreType.DMA((n, N_DEV - 1)), pltpu.SemaphoreType.DMA((n, N_DEV - 1)),
                        pltpu.SemaphoreType.DMA((n,))],
    )(*srcs)


def _peer_of(mask):
    x, y, c = lax.axis_index("x"), lax.axis_index("y"), lax.axis_index("c")
    px = 1 - x if (mask >> 2) & 1 else x
    py = 1 - y if (mask >> 1) & 1 else y
    pc = 1 - c if mask & 1 else c
    return (px, py, pc), 4 * px + 2 * py + pc


ALL_PEERS = tuple(range(1, N_DEV))
OTHER_CHIPS = (2, 4, 6)
SIBLING = 1


def _split_copy(src_ref, land_ref, send_sem, recv_sem, mask, slot, scatter, outgoing):
    x, y, c = lax.axis_index("x"), lax.axis_index("y"), lax.axis_index("c")
    me = 4 * x + 2 * y + c
    peer, pidx = _peer_of(mask)
    return pltpu.make_async_remote_copy(
        src_ref=src_ref.at[pidx] if scatter else src_ref,
        dst_ref=land_ref.at[me if outgoing else pidx],
        send_sem=send_sem.at[slot], recv_sem=recv_sem.at[slot],
        device_id=peer, device_id_type=pl.DeviceIdType.MESH)


_HBM = pl.BlockSpec(memory_space=pltpu.HBM)
_SEM = pl.BlockSpec(memory_space=pltpu.SEMAPHORE)
_EFFECT = pltpu.SideEffectType.DATAFLOW_SIDE_EFFECTING


def _exchange_start(srcs, scatter, name, masks=ALL_PEERS):
    n = len(srcs)
    lands = [pltpu.with_memory_space_constraint(
        lax.empty((N_DEV,) + tuple(s.shape[1:] if sc else s.shape), s.dtype), pltpu.HBM)
        for s, sc in zip(srcs, scatter)]
    srcs = [pltpu.with_memory_space_constraint(s, pltpu.HBM) for s in srcs]

    def body(*refs):
        src_refs, land_refs = refs[:n], refs[n:2 * n]
        send_sems, recv_sems = refs[2 * n:3 * n], refs[3 * n:4 * n]
        token = refs[-1]
        for k in range(n):
            for slot, mask in enumerate(masks):
                _split_copy(src_refs[k], land_refs[k], send_sems[k], recv_sems[k], mask, slot, scatter[k], True).start()
        token[...] = jnp.zeros_like(token)

    sem = pltpu.SemaphoreType.DMA((len(masks),))
    out = pl.pallas_call(
        body, name=name,
        out_shape=tuple([sem] * (2 * n) + [pltpu.HBM(s.shape, s.dtype) for s in srcs]
                        + [pltpu.HBM(l.shape, l.dtype) for l in lands] + [SDS((8, 128), F32)]),
        in_specs=[_HBM] * (2 * n),
        out_specs=tuple([_SEM] * (2 * n) + [_HBM] * (2 * n) + [pl.BlockSpec(memory_space=pltpu.VMEM)]),
        input_output_aliases={i: 2 * n + i for i in range(2 * n)},
        compiler_params=pltpu.CompilerParams(has_side_effects=_EFFECT),
    )(*srcs, *lands)
    return out[:n], out[n:2 * n], out[2 * n:3 * n], out[3 * n:4 * n], out[-1]


def _exchange_wait(send_sems, recv_sems, srcs_thru, lands_thru, scatter, after, name, masks=ALL_PEERS):
    n = len(srcs_thru)

    def body(*refs):
        src_refs, land_refs = refs[:n], refs[n:2 * n]
        send_refs, recv_refs = refs[2 * n:3 * n], refs[3 * n:4 * n]
        for k in range(n):
            for slot, mask in enumerate(masks):
                args = (src_refs[k], land_refs[k], send_refs[k], recv_refs[k], mask, slot, scatter[k])
                _split_copy(*args, True).wait_send()
                _split_copy(*args, False).wait_recv()

    out = pl.pallas_call(
        body, name=name,
        out_shape=tuple([pltpu.HBM(s.shape, s.dtype) for s in srcs_thru]
                        + [pltpu.HBM(l.shape, l.dtype) for l in lands_thru]),
        in_specs=[_HBM] * (2 * n) + [_SEM] * (2 * n) + [pl.BlockSpec(memory_space=pl.ANY)],
        out_specs=tuple([_HBM] * (2 * n)),
        input_output_aliases={i: i for i in range(2 * n)},
        compiler_params=pltpu.CompilerParams(has_side_effects=_EFFECT),
    )(*srcs_thru, *lands_thru, *send_sems, *recv_sems, after)
    return out[:n], out[n:]


def _sibling_swap(srcs, lands, name):
    n = len(srcs)

    def body(*refs):
        src_refs, land_refs = refs[:n], refs[2 * n:3 * n]
        send_sems, recv_sems, local_sems = refs[3 * n:]
        x, y, c = lax.axis_index("x"), lax.axis_index("y"), lax.axis_index("c")
        me = 4 * x + 2 * y + c
        sibling, sib_idx = _peer_of(SIBLING)

        def copy(k, slot, src, dst):
            return pltpu.make_async_remote_copy(
                src_ref=src, dst_ref=dst, send_sem=send_sems.at[k, slot], recv_sem=recv_sems.at[k, slot],
                device_id=sibling, device_id_type=pl.DeviceIdType.MESH)

        sends, arrivals, locals_ = [], [], []
        for k in range(n):
            land = land_refs[k]
            sends.append(copy(k, 0, src_refs[k], land.at[me]))
            arrivals.append(copy(k, 0, src_refs[k], land.at[sib_idx]))
            for slot, mask in enumerate(OTHER_CHIPS, start=1):
                _, mine = _peer_of(mask)
                _, theirs = _peer_of(mask | SIBLING)
                sends.append(copy(k, slot, land.at[mine], land.at[mine]))
                arrivals.append(copy(k, slot, land.at[theirs], land.at[theirs]))
            locals_.append(pltpu.make_async_copy(src_refs[k], land.at[me], local_sems.at[k]))
        for cp in sends + locals_:
            cp.start()
        for cp in arrivals:
            cp.wait_recv()
        for cp in sends:
            cp.wait_send()
        for cp in locals_:
            cp.wait()

    hbm = pl.BlockSpec(memory_space=pl.ANY)
    return pl.pallas_call(
        body, name=name, out_shape=[SDS(l.shape, l.dtype) for l in lands],
        in_specs=[hbm] * (2 * n), out_specs=[hbm] * n,
        input_output_aliases={n + i: i for i in range(n)},
        scratch_shapes=[pltpu.SemaphoreType.DMA((n, 4)), pltpu.SemaphoreType.DMA((n, 4)),
                        pltpu.SemaphoreType.DMA((n,))],
    )(*srcs, *lands)


def _with_own(landed, own, me):
    return lax.dynamic_update_slice_in_dim(landed, own[None], me, axis=0)


def _kv_proj(mem, g_mem, wkv3):
    m = mem.shape[0]

    def body(mem_ref, g_ref, w_ref, kv_ref, mn_ref):
        y, _, _ = _rms_fwd(mem_ref[...], g_ref[...])
        yb = _bf(y)
        mn_ref[...] = yb
        for b in range(N_DEV):
            kv_ref[:, 256 * b:256 * (b + 1)] = _dot(yb, w_ref[b])

    return pl.pallas_call(body, name="kv_proj", out_shape=(SDS((m, 2048), F32), SDS(mem.shape, BF16)),
                          compiler_params=_cparams())(mem, g_mem, wkv3)


def _fwd_in(x, g_mix, w_in3, b_in, ts):
    s, d = x.shape

    def body(x_ref, g_ref, w_ref, b_ref, z_ref, a_ref):
        hn, _, _ = _rms_fwd(x_ref[...], g_ref[...])
        hb = _bf(hn)
        for b in range(N_DEV):
            z_ref[:, 256 * b:256 * (b + 1)] = _dot(hb, w_ref[b]) + b_ref[:, 256 * b:256 * (b + 1)]
        a_ref[...] = z_ref[:, 0:512] * jax.nn.sigmoid(z_ref[:, 512:1024])

    return pl.pallas_call(
        body, name="fwd_in", grid=(s // ts,),
        in_specs=[_row(ts, d), _const(g_mix.shape), _const(w_in3.shape), _const(b_in.shape)],
        out_specs=(_row(ts, 2048), _row(ts, 512)),
        out_shape=(SDS((s, 2048), F32), SDS((s, 512), F32)),
        compiler_params=_cparams("arbitrary"))(x, g_mix, w_in3, b_in)


def _conv_fwd(a, w, b):
    s, cw = a.shape
    rc = 256 if s % 256 == 0 else 128

    def body(a_ref, w_ref, b_ref, c_ref, pad):
        pad[0:CONV_PAD, :] = jnp.zeros((CONV_PAD, 128), F32)
        pad[CONV_PAD:, :] = a_ref[...]

        def chunk(i, carry):
            r0 = pl.multiple_of(i * rc, rc)
            win = pad[pl.ds(r0, rc + CONV_PAD), :]
            acc = jnp.zeros((rc, 128), F32) + b_ref[...]
            for k in range(CONV_K):
                off = CONV_PAD - (CONV_K - 1) + k
                acc = acc + w_ref[k:k + 1, :] * win[off:off + rc, :]
            c_ref[pl.ds(r0, rc), :] = acc
            return carry

        lax.fori_loop(0, s // rc, chunk, 0)

    blk = lambda r: pl.BlockSpec((r, 128), lambda j: (0, j))
    return pl.pallas_call(
        body, name="conv_fwd", grid=(cw // 128,),
        in_specs=[blk(s), blk(CONV_K), blk(1)], out_specs=blk(s), out_shape=SDS((s, cw), F32),
        scratch_shapes=[pltpu.VMEM((s + CONV_PAD, 128), F32)],
        compiler_params=_cparams("arbitrary"))(a, w, b)


def _fwd_out(x, c, z, cln_g, cln_b, gln_g, gln_b, ws, bst, w_out, ts):
    s, d = x.shape

    def body(x_ref, c_ref, zuv_ref, clg, clb, glg, glb, ws_ref, bst_ref, wo_ref, h1_ref, mixed_scr):
        cl, _, _ = _ln_fwd(c_ref[...], clg[...], clb[...])
        co = cl * jax.nn.sigmoid(cl)
        u, _ = _gelu(zuv_ref[:, 0:512])
        vg, _ = _gelu(zuv_ref[:, 512:1024])
        vln, _, _ = _ln_fwd(vg, glg[...], glb[...])
        _mix_fwd(_bf(vln), ws_ref, bst_ref, mixed_scr, ts)
        gm = u * mixed_scr[...]
        h1_ref[...] = x_ref[...] + _dot(_bf(co), wo_ref[0:512, :]) + _dot(_bf(gm), wo_ref[512:1024, :])

    return pl.pallas_call(
        body, name="fwd_out", grid=(s // ts,),
        in_specs=[_row(ts, d), _row(ts, 512), _row(ts, 1024, 1), _const(cln_g.shape), _const(cln_b.shape),
                  _const(gln_g.shape), _const(gln_b.shape), _const(ws.shape), _const(bst.shape), _const(w_out.shape)],
        out_specs=_row(ts, d), out_shape=SDS((s, d), F32),
        scratch_shapes=[pltpu.VMEM((ts, 512), F32)],
        compiler_params=_cparams("arbitrary"))(x, c, z, cln_g, cln_b, gln_g, gln_b, ws, bst, w_out)


def _softmax_rows(sc):
    m = jnp.max(sc, axis=-1, keepdims=True)
    e = jnp.exp(sc - m)
    return e / jnp.sum(e, axis=-1, keepdims=True)


def _fwd_xa(h1, g_xa, wq, kv, wo, ts):
    s, d = h1.shape
    scale = XA_DH ** -0.5

    def body(h_ref, g_ref, wq_ref, kv_ref, wo_ref, h2_ref, o_scr):
        hn, _, _ = _rms_fwd(h_ref[...], g_ref[...])
        q = _dot(_bf(hn), wq_ref[...])
        for h in range(XA_HEADS):
            qh = _bf(q[:, XA_DH * h:XA_DH * (h + 1)])
            kh = _bf(kv_ref[:, XA_DH * h:XA_DH * (h + 1)])
            vh = _bf(kv_ref[:, d + XA_DH * h:d + XA_DH * (h + 1)])
            p = _softmax_rows(_dot(qh, kh, NT) * scale)
            o_scr[:, XA_DH * h:XA_DH * (h + 1)] = _dot(_bf(p), vh)
        h2_ref[...] = h_ref[...] + _dot(_bf(o_scr[...]), wo_ref[...])

    return pl.pallas_call(
        body, name="fwd_xa", grid=(s // ts,),
        in_specs=[_row(ts, d), _const(g_xa.shape), _const(wq.shape), _const(kv.shape), _const(wo.shape)],
        out_specs=_row(ts, d), out_shape=SDS((s, d), F32),
        scratch_shapes=[pltpu.VMEM((ts, d), F32)],
        compiler_params=_cparams("arbitrary"))(h1, g_xa, wq, kv, wo)


def _fwd_ffn(h2, g_ffn, wgut, wdown, g_final, target, ts):
    s, d = h2.shape
    hid = wdown.shape[0]
    hc = hid // 2

    def body(h_ref, g_ref, wgu_ref, wd_ref, gf_ref, t_ref, dh3_ref, dh3b_ref, loss_ref, dgf_ref):
        hn, _, _ = _rms_fwd(h_ref[...], g_ref[...])
        hb = _bf(hn)
        h3 = h_ref[...]
        for n in range(2):
            g = _dot(hb, wgu_ref[hc * n:hc * (n + 1), :], NT)
            u = _dot(hb, wgu_ref[hid + hc * n:hid + hc * (n + 1), :], NT)
            act = g * jax.nn.sigmoid(g) * u
            h3 = h3 + _dot(_bf(act), wd_ref[hc * n:hc * (n + 1), :])
        y, xh, r = _rms_fwd(h3, gf_ref[...])
        diff = y - t_ref[...]
        part = 0.5 * jnp.sum(jnp.mean(diff * diff, axis=-1, keepdims=True), axis=0, keepdims=True)
        _accumulate(loss_ref, jnp.zeros(loss_ref.shape, F32) + part)
        dh3, dgf = _rms_bwd(diff * (1.0 / d), xh, r, gf_ref[...])
        dh3_ref[...] = dh3
        dh3b_ref[...] = _bf(dh3)
        _accumulate(dgf_ref, dgf)

    return pl.pallas_call(
        body, name="fwd_ffn", grid=(s // ts,),
        in_specs=[_row(ts, d), _const(g_ffn.shape), _const(wgut.shape), _const(wdown.shape), _const(g_final.shape),
                  _row(ts, d)],
        out_specs=(_row(ts, d), _row(ts, d), _acc((1, 128)), _acc((1, d))),
        out_shape=(SDS((s, d), F32), SDS((s, d), BF16), SDS((1, 128), F32), SDS((1, d), F32)),
        compiler_params=_cparams("arbitrary"))(h2, g_ffn, wgut, wdown, g_final, target)


def _bwd_ffn(h2, dh3, g_ffn, wgut, wdown, ts):
    s, d = h2.shape
    hid = wdown.shape[0]
    hc = hid // 2

    def body(h_ref, dh3_ref, g_ref, wgu_ref, wd_ref, dh2_ref, dh2b_ref, act_ref, dgu_ref, hn_ref, dg_ref):
        hn, xh, r = _rms_fwd(h_ref[...], g_ref[...])
        hb = _bf(hn)
        hn_ref[...] = hb
        db = _bf(dh3_ref[...])
        dhn = jnp.zeros((ts, d), F32)
        for n in range(2):
            wg = wgu_ref[hc * n:hc * (n + 1), :]
            wu = wgu_ref[hid + hc * n:hid + hc * (n + 1), :]
            g = _dot(hb, wg, NT)
            u = _dot(hb, wu, NT)
            sg = jax.nn.sigmoid(g)
            sl = g * sg
            act_ref[:, hc * n:hc * (n + 1)] = _bf(sl * u)
            dact = _dot(db, wd_ref[hc * n:hc * (n + 1), :], NT)
            dgb = _bf(dact * u * _silu_grad(g, sg))
            dub = _bf(dact * sl)
            dgu_ref[:, hc * n:hc * (n + 1)] = dgb
            dgu_ref[:, hid + hc * n:hid + hc * (n + 1)] = dub
            dhn = dhn + _dot(dgb, wg) + _dot(dub, wu)
        dx, dg = _rms_bwd(dhn, xh, r, g_ref[...])
        dh2 = dh3_ref[...] + dx
        dh2_ref[...] = dh2
        dh2b_ref[...] = _bf(dh2)
        _accumulate(dg_ref, dg)

    return pl.pallas_call(
        body, name="bwd_ffn", grid=(s // ts,),
        in_specs=[_row(ts, d), _row(ts, d), _const(g_ffn.shape), _const(wgut.shape), _const(wdown.shape)],
        out_specs=(_row(ts, d), _row(ts, d), _row(ts, hid), _row(ts, 2 * hid), _row(ts, d), _acc((1, d))),
        out_shape=(SDS((s, d), F32), SDS((s, d), BF16), SDS((s, hid), BF16), SDS((s, 2 * hid), BF16),
                   SDS((s, d), BF16), SDS((1, d), F32)),
        compiler_params=_cparams("arbitrary"))(h2, dh3, g_ffn, wgut, wdown)


def _bwd_xa(h1, dh2, g_xa, wq, wo, kv, ts):
    s, d = h1.shape
    scale = XA_DH ** -0.5

    def body(h_ref, dh2_ref, g_ref, wq_ref, wo_ref, kv_ref, dh1_ref, dh1b_ref, dq_ref, o_ref, hn_ref, dkv_ref,
             dg_ref, dq_scr):
        hn, xh, r = _rms_fwd(h_ref[...], g_ref[...])
        hb = _bf(hn)
        hn_ref[...] = hb
        q = _dot(hb, wq_ref[...])
        do = _dot(_bf(dh2_ref[...]), wo_ref[...], NT)

        @pl.when(pl.program_id(0) == 0)
        def _():
            dkv_ref[...] = jnp.zeros_like(dkv_ref)

        for h in range(XA_HEADS):
            lo, hi = XA_DH * h, XA_DH * (h + 1)
            qh = _bf(q[:, lo:hi])
            kh = _bf(kv_ref[:, lo:hi])
            vh = _bf(kv_ref[:, d + lo:d + hi])
            p = _softmax_rows(_dot(qh, kh, NT) * scale)
            pb = _bf(p)
            o_ref[:, lo:hi] = _bf(_dot(pb, vh))
            doh = _bf(do[:, lo:hi])
            dp = _dot(doh, vh, NT)
            ds = p * (dp - jnp.sum(p * dp, axis=-1, keepdims=True)) * scale
            dsb = _bf(ds)
            dq_scr[:, lo:hi] = _dot(dsb, kh)
            dkv_ref[:, lo:hi] += _dot(dsb, qh, TN)
            dkv_ref[:, d + lo:d + hi] += _dot(pb, doh, TN)
        dqb = _bf(dq_scr[...])
        dq_ref[...] = dqb
        dx, dg = _rms_bwd(_dot(dqb, wq_ref[...], NT), xh, r, g_ref[...])
        dh1 = dh2_ref[...] + dx
        dh1_ref[...] = dh1
        dh1b_ref[...] = _bf(dh1)
        _accumulate(dg_ref, dg)

    return pl.pallas_call(
        body, name="bwd_xa", grid=(s // ts,),
        in_specs=[_row(ts, d), _row(ts, d), _const(g_xa.shape), _const(wq.shape), _const(wo.shape), _const(kv.shape)],
        out_specs=(_row(ts, d), _row(ts, d), _row(ts, d), _row(ts, d), _row(ts, d), _acc(kv.shape), _acc((1, d))),
        out_shape=(SDS((s, d), F32), SDS((s, d), BF16), SDS((s, d), BF16), SDS((s, d), BF16), SDS((s, d), BF16),
                   SDS(kv.shape, F32), SDS((1, d), F32)),
        scratch_shapes=[pltpu.VMEM((ts, d), F32)],
        compiler_params=_cparams("arbitrary"))(h1, dh2, g_xa, wq, wo, kv)


def _bwd_kv(dkv, mn, mem, g_mem, wkv3):
    d = mem.shape[1]

    def body(dkv_ref, mn_ref, mem_ref, g_ref, w_ref, dw_ref, dg_ref):
        dkvb = _bf(dkv_ref[...])
        dmn = jnp.zeros(mem_ref.shape, F32)
        for b in range(N_DEV):
            blk = dkvb[:, 256 * b:256 * (b + 1)]
            dmn = dmn + _dot(blk, w_ref[b], NT)
            dw_ref[b] = _bf(_dot(mn_ref[...], blk, TN))
        _, xh, r = _rms_fwd(mem_ref[...], g_ref[...])
        _, dg = _rms_bwd(dmn, xh, r, g_ref[...])
        dg_ref[...] = dg

    return pl.pallas_call(body, name="bwd_kv", out_shape=(SDS(wkv3.shape, BF16), SDS((1, d), F32)),
                          compiler_params=_cparams())(dkv, mn, mem, g_mem, wkv3)


def _bwd_out(dh1, c, z, cln_g, cln_b, gln_g, gln_b, ws, wst, bst, w_out, ts):
    s, d = dh1.shape
    nh = GM_HEADS

    def body(dh1_ref, c_ref, zuv_ref, clg, clb, glg, glb, ws_ref, wst_ref, bst_ref, wo_ref,
             cat_ref, dc_ref, dzuv_ref, dws_ref, dbst_ref, dclg_ref, dclb_ref, dglg_ref, dglb_ref, dbin_ref,
             mixed_scr, dv_scr):
        cl, chat, crs = _ln_fwd(c_ref[...], clg[...], clb[...])
        sg = jax.nn.sigmoid(cl)
        zu = zuv_ref[:, 0:512]
        zv = zuv_ref[:, 512:1024]
        u, tu = _gelu(zu)
        vg, tv = _gelu(zv)
        vln, vhat, vrs = _ln_fwd(vg, glg[...], glb[...])
        vb = _bf(vln)
        _mix_fwd(vb, ws_ref, bst_ref, mixed_scr, ts)
        mixed = mixed_scr[...]
        cat_ref[:, 0:512] = _bf(cl * sg)
        cat_ref[:, 512:1024] = _bf(u * mixed)
        dcat = _dot(_bf(dh1_ref[...]), wo_ref[...], NT)
        dgm = dcat[:, 512:1024]
        dc, dclg, dclb = _ln_bwd(dcat[:, 0:512] * _silu_grad(cl, sg), chat, crs, clg[...])
        dc_ref[...] = dc
        dzu = dgm * mixed * _gelu_grad(zu, tu)
        dm = dgm * u

        @pl.when(pl.program_id(0) == 0)
        def _():
            dws_ref[...] = jnp.zeros_like(dws_ref)
            dbst_ref[...] = jnp.zeros_like(dbst_ref)

        tril, triu, lo = _mix_masks()
        head = lax.broadcasted_iota(jnp.int32, (1, nh), 1)
        for j in range(nh // 2):
            w0t = _bf(jnp.where(triu, wst_ref[2 * j], 0.0))
            w1t = _bf(jnp.where(triu, wst_ref[2 * j + 1], 0.0))
            for n in range(ts // CHUNK):
                rows = slice(n * CHUNK, (n + 1) * CHUNK)
                lanes = slice(j * 128, (j + 1) * 128)
                dmc = dm[rows, lanes]
                dmb = _bf(dmc)
                dv_scr[rows, lanes] = jnp.where(lo, _dot(w0t, dmb), _dot(w1t, dmb))
                vc = vb[rows, lanes]
                d0 = jnp.where(lo, dmc, 0.0)
                d1 = dmc - d0
                dws_ref[2 * j] += jnp.where(tril, _dot(_bf(d0), vc, NT), 0.0)
                dws_ref[2 * j + 1] += jnp.where(tril, _dot(_bf(d1), vc, NT), 0.0)
                dbst_ref[...] += (jnp.sum(d0, axis=1, keepdims=True) * (head == 2 * j).astype(F32)
                                  + jnp.sum(d1, axis=1, keepdims=True) * (head == 2 * j + 1).astype(F32))
        dvg, dglg, dglb = _ln_bwd(dv_scr[...], vhat, vrs, glg[...])
        dzv = dvg * _gelu_grad(zv, tv)
        dzuv_ref[:, 0:512] = _bf(dzu)
        dzuv_ref[:, 512:1024] = _bf(dzv)
        _accumulate(dclg_ref, dclg)
        _accumulate(dclb_ref, dclb)
        _accumulate(dglg_ref, dglg)
        _accumulate(dglb_ref, dglb)
        _accumulate(dbin_ref, jnp.concatenate([jnp.sum(dzu, axis=0, keepdims=True),
                                               jnp.sum(dzv, axis=0, keepdims=True)], axis=1))

    vec = (1, 512)
    return pl.pallas_call(
        body, name="bwd_out", grid=(s // ts,),
        in_specs=[_row(ts, d), _row(ts, 512), _row(ts, 1024, 1), _const(cln_g.shape), _const(cln_b.shape),
                  _const(gln_g.shape), _const(gln_b.shape), _const(ws.shape), _const(wst.shape), _const(bst.shape),
                  _const(w_out.shape)],
        out_specs=(_row(ts, d), _row(ts, 512), _row(ts, 1024), _acc(ws.shape), _acc(bst.shape), _acc(vec), _acc(vec),
                   _acc(vec), _acc(vec), _acc((1, 1024))),
        out_shape=(SDS((s, d), BF16), SDS((s, 512), F32), SDS((s, 1024), BF16), SDS(ws.shape, F32),
                   SDS(bst.shape, F32), SDS(vec, F32), SDS(vec, F32), SDS(vec, F32), SDS(vec, F32), SDS((1, 1024), F32)),
        scratch_shapes=[pltpu.VMEM((ts, 512), F32), pltpu.VMEM((ts, 512), F32)],
        compiler_params=_cparams("arbitrary"))(dh1, c, z, cln_g, cln_b, gln_g, gln_b, ws, wst, bst, w_out)


def _conv_bwd(dc, a, w):
    s, cw = a.shape
    rc = 256 if s % 256 == 0 else 128

    def body(dc_ref, a_ref, w_ref, da_ref, dw_ref, db_ref, pad_a, pad_d):
        pad_a[0:CONV_PAD, :] = jnp.zeros((CONV_PAD, 128), F32)
        pad_a[CONV_PAD:, :] = a_ref[...]
        pad_d[0:s, :] = dc_ref[...]
        pad_d[s:, :] = jnp.zeros((CONV_PAD, 128), F32)
        dw_ref[...] = jnp.zeros_like(dw_ref)
        db_ref[...] = jnp.zeros_like(db_ref)

        def chunk(i, carry):
            r0 = pl.multiple_of(i * rc, rc)
            win_d = pad_d[pl.ds(r0, rc + CONV_PAD), :]
            win_a = pad_a[pl.ds(r0, rc + CONV_PAD), :]
            dcc = win_d[0:rc, :]
            acc = jnp.zeros((rc, 128), F32)
            for k in range(CONV_K):
                acc = acc + w_ref[k:k + 1, :] * win_d[CONV_K - 1 - k:CONV_K - 1 - k + rc, :]
                off = CONV_PAD - (CONV_K - 1) + k
                dw_ref[k:k + 1, :] += jnp.sum(dcc * win_a[off:off + rc, :], axis=0, keepdims=True)
            da_ref[pl.ds(r0, rc), :] = acc
            db_ref[...] += jnp.sum(dcc, axis=0, keepdims=True)
            return carry

        lax.fori_loop(0, s // rc, chunk, 0)

    blk = lambda r: pl.BlockSpec((r, 128), lambda j: (0, j))
    return pl.pallas_call(
        body, name="conv_bwd", grid=(cw // 128,),
        in_specs=[blk(s), blk(s), blk(CONV_K)], out_specs=(blk(s), blk(CONV_PAD), blk(1)),
        out_shape=(SDS((s, cw), F32), SDS((CONV_PAD, cw), F32), SDS((1, cw), F32)),
        scratch_shapes=[pltpu.VMEM((s + CONV_PAD, 128), F32), pltpu.VMEM((s + CONV_PAD, 128), F32)],
        compiler_params=_cparams("arbitrary"))(dc, a, w)


def _bwd_in(x, dh1, da, z, dzuv, g_mix, w_in3, ts):
    s, d = x.shape

    def body(x_ref, dh1_ref, da_ref, zag_ref, dzuv_ref, g_ref, w_ref, dx_ref, dz_ref, hn_ref, dbin_ref, dg_ref):
        za = zag_ref[:, 0:512]
        sg = jax.nn.sigmoid(zag_ref[:, 512:1024])
        da_ = da_ref[...]
        dza = da_ * sg
        dzg = da_ * za * sg * (1.0 - sg)
        dz_ref[:, 0:512] = _bf(dza)
        dz_ref[:, 512:1024] = _bf(dzg)
        dz_ref[:, 1024:2048] = dzuv_ref[...]
        dhn = jnp.zeros((ts, d), F32)
        for b in range(N_DEV):
            dhn = dhn + _dot(dz_ref[:, 256 * b:256 * (b + 1)], w_ref[b], NT)
        hn, xh, r = _rms_fwd(x_ref[...], g_ref[...])
        hn_ref[...] = _bf(hn)
        dxn, dg = _rms_bwd(dhn, xh, r, g_ref[...])
        dx_ref[...] = dh1_ref[...] + dxn
        _accumulate(dg_ref, dg)
        _accumulate(dbin_ref, jnp.concatenate([jnp.sum(dza, axis=0, keepdims=True),
                                               jnp.sum(dzg, axis=0, keepdims=True)], axis=1))

    return pl.pallas_call(
        body, name="bwd_in", grid=(s // ts,),
        in_specs=[_row(ts, d), _row(ts, d), _row(ts, 512), _row(ts, 1024, 0), _row(ts, 1024), _const(g_mix.shape),
                  _const(w_in3.shape)],
        out_specs=(_row(ts, d), _row(ts, 2048), _row(ts, d), _acc((1, 1024)), _acc((1, d))),
        out_shape=(SDS((s, d), F32), SDS((s, 2048), BF16), SDS((s, d), BF16), SDS((1, 1024), F32), SDS((1, d), F32)),
        compiler_params=_cparams("arbitrary"))(x, dh1, da, z, dzuv, g_mix, w_in3)


def _tn_matmul(a, b, tm, name, col_blocks=False, behind=None):
    s, m = a.shape
    n = b.shape[1]
    cb = 256
    tn = 2 * cb if col_blocks else n
    if col_blocks:
        tm = m
    ts = min(s, 512)
    n_s = s // ts

    def body(a_ref, b_ref, *rest):
        o_ref, acc = rest[-2:]
        k = pl.program_id(2)

        @pl.when(k == 0)
        def _():
            acc[...] = jnp.zeros_like(acc)

        acc[...] += _dot(_bf(a_ref[...]), _bf(b_ref[...]), TN)

        @pl.when(k == n_s - 1)
        def _():
            if col_blocks:
                for blk in range(tn // cb):
                    o_ref[blk] = _bf(acc[:, cb * blk:cb * (blk + 1)])
            else:
                o_ref[...] = _bf(acc[...])

    if col_blocks:
        out_shape = SDS((n // cb, m, cb), BF16)
        out_spec = pl.BlockSpec((tn // cb, m, cb), lambda i, j, k: (j, 0, 0))
    else:
        out_shape = SDS((m, n), BF16)
        out_spec = pl.BlockSpec((tm, tn), lambda i, j, k: (i, j))
    extra = [] if behind is None else [behind]
    return pl.pallas_call(
        body, name=name, grid=(m // tm, n // tn, n_s),
        in_specs=[pl.BlockSpec((ts, tm), lambda i, j, k: (k, i)), pl.BlockSpec((ts, tn), lambda i, j, k: (k, j))]
        + [pl.BlockSpec(memory_space=pl.ANY)] * len(extra),
        out_specs=out_spec, out_shape=out_shape,
        scratch_shapes=[pltpu.VMEM((tm, tn), F32)],
        compiler_params=_cparams("parallel", "parallel", "arbitrary"))(a, b, *extra)


def _sum8(parts, name):
    _, r, c = parts.shape
    tr = r
    for cand in (512, 256, 352, 128, 8):
        if r % cand == 0 and cand <= r:
            tr = cand
            break

    def body(p_ref, o_ref):
        acc = p_ref[0].astype(F32)
        for dev in range(1, N_DEV):
            acc = acc + p_ref[dev].astype(F32)
        o_ref[...] = acc

    return pl.pallas_call(
        body, name=name, grid=(r // tr,),
        in_specs=[pl.BlockSpec((N_DEV, tr, c), lambda i: (0, i, 0))],
        out_specs=pl.BlockSpec((tr, c), lambda i: (i, 0)), out_shape=SDS((r, c), F32),
        compiler_params=_cparams("parallel"))(parts)


def _adamw(w, g, m, v, name):
    r, c = w.shape
    tr = r
    for cand in (256, 128, 176, 8):
        if r % cand == 0 and cand <= r:
            tr = cand
            break

    def body(w_ref, g_ref, m_ref, v_ref, d_ref, m2_ref, v2_ref):
        g_ = g_ref[...]
        m2 = ADAM_B1 * m_ref[...] + (1.0 - ADAM_B1) * g_
        v2 = ADAM_B2 * v_ref[...] + (1.0 - ADAM_B2) * (g_ * g_)
        m_hat = m2 / (1.0 - ADAM_B1 ** ADAM_STEP)
        v_hat = v2 / (1.0 - ADAM_B2 ** ADAM_STEP)
        d_ref[...] = -ADAM_LR * (m_hat / (jnp.sqrt(v_hat) + ADAM_EPS) + ADAM_WD * w_ref[...])
        m2_ref[...] = m2
        v2_ref[...] = v2

    blk = pl.BlockSpec((tr, c), lambda i: (i, 0))
    return pl.pallas_call(
        body, name=name, grid=(r // tr,), in_specs=[blk] * 4, out_specs=(blk,) * 3,
        out_shape=(SDS((r, c), F32),) * 3, compiler_params=_cparams("parallel"))(w, g, m, v)


def _pack(arrs):
    flat = jnp.concatenate([a.reshape(-1) for a in arrs])
    pad = (-flat.shape[0]) % (128 * 128)
    return jnp.pad(flat, (0, pad)).reshape(-1, 128)


def _unpack(packed, shapes):
    flat = packed.reshape(-1)
    out, off = [], 0
    for shp in shapes:
        size = 1
        for dim in shp:
            size *= dim
        out.append(flat[off:off + size].reshape(shp))
        off += size
    return out


SMALL = ("norm_mix_g", "b_in", "conv_w", "conv_b", "conv_ln_g", "conv_ln_b", "gm_ln_g", "gm_ln_b", "gm_w_s", "gm_b_s",
         "norm_xa_g", "mem_norm_g", "norm_ffn_g", "final_norm_g")
BIG = ("w_in", "w_out", "xa_wq", "xa_wkv", "xa_wo", "ffn_w_gate_up", "ffn_w_down")
WEIGHTS = ("norm_mix_g", "w_in", "b_in", "conv_w", "conv_b", "conv_ln_g", "conv_ln_b", "gm_ln_g", "gm_ln_b", "gm_w_s",
           "gm_b_s", "w_out", "norm_xa_g", "mem_norm_g", "xa_wq", "xa_wkv", "xa_wo", "norm_ffn_g", "ffn_w_gate_up",
           "ffn_w_down", "final_norm_g")


def kernel(x, mem, norm_mix_g, w_in, b_in, conv_w, conv_b, conv_ln_g, conv_ln_b, gm_ln_g, gm_ln_b, gm_w_s, gm_b_s, w_out, norm_xa_g, mem_norm_g, xa_wq, xa_wkv, xa_wo, norm_ffn_g, ffn_w_gate_up, ffn_w_down, final_norm_g, loss_target, m_norm_mix_g, m_w_in, m_b_in, m_conv_w, m_conv_b, m_conv_ln_g, m_conv_ln_b, m_gm_ln_g, m_gm_ln_b, m_gm_w_s, m_gm_b_s, m_w_out, m_norm_xa_g, m_mem_norm_g, m_xa_wq, m_xa_wkv, m_xa_wo, m_norm_ffn_g, m_ffn_w_gate_up, m_ffn_w_down, m_final_norm_g, v_norm_mix_g, v_w_in, v_b_in, v_conv_w, v_conv_b, v_conv_ln_g, v_conv_ln_b, v_gm_ln_g, v_gm_ln_b, v_gm_w_s, v_gm_b_s, v_w_out, v_norm_xa_g, v_mem_norm_g, v_xa_wq, v_xa_wkv, v_xa_wo, v_norm_ffn_g, v_ffn_w_gate_up, v_ffn_w_down, v_final_norm_g):
    w = dict(norm_mix_g=norm_mix_g, w_in=w_in, b_in=b_in, conv_w=conv_w, conv_b=conv_b, conv_ln_g=conv_ln_g,
             conv_ln_b=conv_ln_b, gm_ln_g=gm_ln_g, gm_ln_b=gm_ln_b, gm_w_s=gm_w_s, gm_b_s=gm_b_s, w_out=w_out,
             norm_xa_g=norm_xa_g, mem_norm_g=mem_norm_g, xa_wq=xa_wq, xa_wkv=xa_wkv, xa_wo=xa_wo,
             norm_ffn_g=norm_ffn_g, ffn_w_gate_up=ffn_w_gate_up, ffn_w_down=ffn_w_down, final_norm_g=final_norm_g)
    mom = dict(norm_mix_g=m_norm_mix_g, w_in=m_w_in, b_in=m_b_in, conv_w=m_conv_w, conv_b=m_conv_b,
               conv_ln_g=m_conv_ln_g, conv_ln_b=m_conv_ln_b, gm_ln_g=m_gm_ln_g, gm_ln_b=m_gm_ln_b, gm_w_s=m_gm_w_s,
               gm_b_s=m_gm_b_s, w_out=m_w_out, norm_xa_g=m_norm_xa_g, mem_norm_g=m_mem_norm_g, xa_wq=m_xa_wq,
               xa_wkv=m_xa_wkv, xa_wo=m_xa_wo, norm_ffn_g=m_norm_ffn_g, ffn_w_gate_up=m_ffn_w_gate_up,
               ffn_w_down=m_ffn_w_down, final_norm_g=m_final_norm_g)
    var = dict(norm_mix_g=v_norm_mix_g, w_in=v_w_in, b_in=v_b_in, conv_w=v_conv_w, conv_b=v_conv_b,
               conv_ln_g=v_conv_ln_g, conv_ln_b=v_conv_ln_b, gm_ln_g=v_gm_ln_g, gm_ln_b=v_gm_ln_b, gm_w_s=v_gm_w_s,
               gm_b_s=v_gm_b_s, w_out=v_w_out, norm_xa_g=v_norm_xa_g, mem_norm_g=v_mem_norm_g, xa_wq=v_xa_wq,
               xa_wkv=v_xa_wkv, xa_wo=v_xa_wo, norm_ffn_g=v_norm_ffn_g, ffn_w_gate_up=v_ffn_w_gate_up,
               ffn_w_down=v_ffn_w_down, final_norm_g=v_final_norm_g)

    me = 4 * lax.axis_index("x") + 2 * lax.axis_index("y") + lax.axis_index("c")
    s, d = x.shape[1], x.shape[2]
    xs = x.reshape(s, d)
    mems = mem.reshape(mem.shape[1], d)
    tgt = loss_target.reshape(s, d)
    ts = min(512, s)
    ts_ffn = min(256, s)
    row = lambda a: a.reshape(1, -1)

    conv_w_pad = jnp.pad(conv_w, ((0, CONV_PAD - CONV_K), (0, 128 - conv_w.shape[1])))
    shards = [_bf(w_in), conv_w_pad, _bf(xa_wkv), _bf(w_out), _bf(xa_wq), _bf(xa_wo), _bf(ffn_w_gate_up.T),
              _bf(ffn_w_down)]
    g_send, g_recv, g_src, g_land, g_tok = _exchange_start(shards, [False] * len(shards), "gather_start", OTHER_CHIPS)

    def gathered(idx, after, name):
        pick = lambda seq: [seq[i] for i in idx]
        srcs, lands = _exchange_wait(pick(g_send), pick(g_recv), pick(g_src), pick(g_land), [False] * len(idx), after,
                                     "gather_wait_" + name, OTHER_CHIPS)
        return _sibling_swap(srcs, lands, "gather_swap_" + name)

    bst = gm_b_s.T
    wst = jnp.swapaxes(gm_w_s, 1, 2)

    w_in3, conv_w8 = gathered((0, 1), g_tok, "in")
    conv_w_f = conv_w8[:, :CONV_K, :conv_w.shape[1]].transpose(1, 0, 2).reshape(CONV_K, -1)
    cw = conv_w_f.shape[1]
    z, a = _fwd_in(xs, row(norm_mix_g), w_in3, row(b_in), ts)
    c = _conv_fwd(a, conv_w_f, row(conv_b))
    wkv3, w_out3 = gathered((2, 3), c, "out")
    w_out_f = w_out3.reshape(-1, d)
    kv, mn = _kv_proj(mems, row(mem_norm_g), wkv3)
    h1 = _fwd_out(xs, c, z, row(conv_ln_g), row(conv_ln_b), row(gm_ln_g), row(gm_ln_b), gm_w_s, bst, w_out_f, ts)
    wq3, wo3 = gathered((4, 5), h1, "xa")
    wq_f = wq3.reshape(-1, d)
    wo_f = wo3.reshape(-1, d)
    h2 = _fwd_xa(h1, row(norm_xa_g), wq_f, kv, wo_f, ts)
    wgut3, wdown3 = gathered((6, 7), h2, "ffn")
    wgut_f = wgut3.reshape(-1, d)
    wdown_f = wdown3.reshape(-1, d)
    dh3, dh3b, loss_p, d_final_g = _fwd_ffn(h2, row(norm_ffn_g), wgut_f, wdown_f, row(final_norm_g), tgt, ts_ffn)

    blocks = lambda m: m.reshape(N_DEV, -1, d)
    tok = lambda ex: ex[4][0:1, 0:1]
    dh2, dh2b, act, dgu, hn2, d_ffn_g = _bwd_ffn(h2, dh3, row(norm_ffn_g), wgut_f, wdown_f, ts_ffn)
    dwgut = _tn_matmul(dgu, hn2, 1408, "dw_gate_up")
    dwdown = _tn_matmul(act, dh3b, 1408, "dw_down")
    ex_ffn = _exchange_start([blocks(dwgut), blocks(dwdown)], [True, True], "scatter_start_ffn")
    dh1, dh1b, dq, o, hn1, dkv, d_xa_g = _bwd_xa(h1, dh2, row(norm_xa_g) + tok(ex_ffn), wq_f, wo_f, kv, ts)
    dwq = _tn_matmul(hn1, dq, 1024, "dw_q")
    dwo = _tn_matmul(o, dh2b, 1024, "dw_o")
    dwkv3, d_mem_g = _bwd_kv(dkv, mn, mems, row(mem_norm_g), wkv3)
    ex_xa = _exchange_start([blocks(dwq), blocks(dwo), dwkv3], [True] * 3, "scatter_start_xa")
    (cat, dc, dzuv, dws, dbst, d_cln_g, d_cln_b, d_gln_g, d_gln_b, dbin_uv) = _bwd_out(
        dh1, c, z, row(conv_ln_g) + tok(ex_xa), row(conv_ln_b), row(gm_ln_g), row(gm_ln_b), gm_w_s, wst, bst,
        w_out_f, ts)
    dwout = _tn_matmul(cat, dh1b, 1024, "dw_out")
    early = dict(b_in_uv=dbin_uv, conv_ln_g=d_cln_g, conv_ln_b=d_cln_b, gm_ln_g=d_gln_g, gm_ln_b=d_gln_b, gm_w_s=dws,
                 gm_b_s=dbst.T, norm_xa_g=d_xa_g, mem_norm_g=d_mem_g, norm_ffn_g=d_ffn_g, final_norm_g=d_final_g,
                 loss=loss_p)
    ex_out = _exchange_start([blocks(dwout), _pack(list(early.values()))], [True, False], "scatter_start_out")
    da, dconv_w, dconv_b = _conv_bwd(dc, a, conv_w_f + tok(ex_out))
    dx, dz, hn0, dbin_ag, d_mix_g = _bwd_in(xs, dh1, da, z, dzuv, row(norm_mix_g), w_in3, ts)
    late = dict(norm_mix_g=d_mix_g, b_in_ag=dbin_ag, conv_w=dconv_w[:CONV_K], conv_b=dconv_b)
    ex_late = _exchange_start([_pack(list(late.values()))], [False], "gather_start_late")
    dw_in3 = _tn_matmul(hn0, dz, 0, "dw_in", col_blocks=True, behind=ex_late[4])
    ex_in = _exchange_start([dw_in3], [True], "scatter_start_in")

    def small_totals(ex, k, after, name):
        srcs, lands = _exchange_wait([ex[0][k]], [ex[1][k]], [ex[2][k]], [ex[3][k]], [False], after, name)
        return _sum8(_with_own(lands[0], srcs[0], me), "sum_" + name)

    early_tot = small_totals(ex_out, 1, ex_in[4], "small_early")
    late_tot = small_totals(ex_late, 0, early_tot, "small_late")
    early_g = dict(zip(early, _unpack(early_tot, [v.shape for v in early.values()])))
    late_g = dict(zip(late, _unpack(late_tot, [v.shape for v in late.values()])))
    loss = early_g["loss"][0, 0]
    grads = {}
    for nm in SMALL:
        if nm == "b_in":
            g = jnp.concatenate([late_g["b_in_ag"], early_g["b_in_uv"]], axis=1)
        elif nm == "conv_w":
            g = lax.dynamic_slice_in_dim(late_g[nm], me * conv_w.shape[1], conv_w.shape[1], axis=1)
        else:
            g = late_g[nm] if nm in late_g else early_g[nm]
        grads[nm] = g.reshape(w[nm].shape)

    delta, new_m, new_v = {}, {}, {}
    small_shapes = [w[nm].shape for nm in SMALL]
    sd, sm, sv = _adamw(_pack([w[nm] for nm in SMALL]), _pack([grads[nm] for nm in SMALL]),
                        _pack([mom[nm] for nm in SMALL]), _pack([var[nm] for nm in SMALL]), "adamw_small")
    for packed_out, dst in ((sd, delta), (sm, new_m), (sv, new_v)):
        for nm, val in zip(SMALL, _unpack(packed_out, small_shapes)):
            dst[nm] = val

    def reduced(srcs, lands, names):
        for nm, src, land in zip(names, srcs, lands):
            own = lax.dynamic_index_in_dim(src, me, axis=0, keepdims=False)
            g = _sum8(_with_own(land, own, me), "sum_" + nm)
            grads[nm] = g.T if nm == "ffn_w_gate_up" else g
            delta[nm], new_m[nm], new_v[nm] = _adamw(w[nm], grads[nm], mom[nm], var[nm], "adamw_" + nm)
        return new_v[names[-1]]

    after = sd
    for ex, names, tag in ((ex_ffn, ("ffn_w_gate_up", "ffn_w_down"), "ffn"), (ex_xa, ("xa_wq", "xa_wo", "xa_wkv"), "xa")):
        srcs, lands = _exchange_wait(ex[0], ex[1], ex[2], ex[3], [True] * len(names), after, "scatter_wait_" + tag)
        after = reduced(srcs, lands, names)
    srcs, lands = _exchange_wait([ex_out[0][0]], [ex_out[1][0]], [ex_out[2][0]], [ex_out[3][0]], [True], after,
                                 "scatter_wait_out")
    after = reduced(srcs, lands, ("w_out",))
    srcs, lands = _exchange_wait(ex_in[0], ex_in[1], ex_in[2], ex_in[3], [True], after, "scatter_wait_in")
    reduced(srcs, lands, ("w_in",))

    return (loss, dx.reshape(x.shape), *[grads[nm] for nm in WEIGHTS], *[delta[nm] for nm in WEIGHTS],
            *[new_m[nm] for nm in WEIGHTS], *[new_v[nm] for nm in WEIGHTS])
```

```python
import functools

import jax
import jax.numpy as jnp
from jax import lax
from jax.experimental import pallas as pl
from jax.experimental.pallas import tpu as pltpu

F32 = jnp.float32
BF16 = jnp.bfloat16
SDS = jax.ShapeDtypeStruct

N_DEV = 8
RMS_EPS = 1e-6
LN_EPS = 1e-5
CONV_K = 31
CONV_PAD = 32
CHUNK = 128
GM_HEADS = 8
XA_HEADS = 4
XA_DH = 256
GELU_K0 = 0.7978845608028654
GELU_K1 = 0.044715
ADAM_LR = 0.001
ADAM_B1 = 0.9
ADAM_B2 = 0.999
ADAM_EPS = 1e-08
ADAM_WD = 0.01
ADAM_STEP = 10
VMEM_LIMIT = 60 * 1024 * 1024

NN = (((1,), (0,)), ((), ()))
NT = (((1,), (1,)), ((), ()))
TN = (((0,), (0,)), ((), ()))


def _dot(a, b, dims=NN):
    return lax.dot_general(a, b, dims, preferred_element_type=F32)


def _bf(x):
    return x.astype(BF16)


def _cparams(*sem):
    return pltpu.CompilerParams(dimension_semantics=tuple(sem) if sem else None, vmem_limit_bytes=VMEM_LIMIT)


def _row(ts, w, col=0):
    return pl.BlockSpec((ts, w), lambda i: (i, col))


def _const(shape):
    nd = len(shape)
    return pl.BlockSpec(shape, lambda i: (0,) * nd, pipeline_mode=pl.Buffered(1))


def _acc(shape):
    nd = len(shape)
    return pl.BlockSpec(shape, lambda i: (0,) * nd)


def _rms_fwd(x, g):
    r = lax.rsqrt(jnp.mean(x * x, axis=-1, keepdims=True) + RMS_EPS)
    xh = x * r
    return xh * g, xh, r


def _rms_bwd(dy, xh, r, g):
    gdy = dy * g
    dx = r * (gdy - xh * jnp.mean(gdy * xh, axis=-1, keepdims=True))
    dg = jnp.sum(dy * xh, axis=0, keepdims=True)
    return dx, dg


def _ln_fwd(x, g, b):
    mu = jnp.mean(x, axis=-1, keepdims=True)
    xc = x - mu
    rs = lax.rsqrt(jnp.mean(xc * xc, axis=-1, keepdims=True) + LN_EPS)
    xh = xc * rs
    return xh * g + b, xh, rs


def _ln_bwd(dy, xh, rs, g):
    dxh = dy * g
    dx = rs * (dxh - jnp.mean(dxh, axis=-1, keepdims=True) - xh * jnp.mean(dxh * xh, axis=-1, keepdims=True))
    return dx, jnp.sum(dy * xh, axis=0, keepdims=True), jnp.sum(dy, axis=0, keepdims=True)


def _gelu(x):
    t = jnp.tanh(GELU_K0 * (x + GELU_K1 * (x * x * x)))
    return 0.5 * x * (1.0 + t), t


def _gelu_grad(x, t):
    return 0.5 * (1.0 + t) + 0.5 * x * (1.0 - t * t) * (GELU_K0 * (1.0 + 3.0 * GELU_K1 * x * x))


def _silu_grad(x, sg):
    return sg * (1.0 + x * (1.0 - sg))


def _by_residue(offsets):
    groups = {}
    for off in offsets:
        groups.setdefault(off % 8, []).append(off)
    return [(res, sorted(offs)) for res, offs in sorted(groups.items())]


def _accumulate(ref, val):
    @pl.when(pl.program_id(0) == 0)
    def _():
        ref[...] = jnp.zeros_like(ref)
    ref[...] += val


def _mix_masks():
    row = lax.broadcasted_iota(jnp.int32, (CHUNK, CHUNK), 0)
    col = lax.broadcasted_iota(jnp.int32, (CHUNK, CHUNK), 1)
    return row >= col, row <= col, col < (CHUNK // 2)


def _mix_fwd(vb, ws_ref, bst_ref, mixed_scr, ts):
    tril, _, lo = _mix_masks()
    for j in range(GM_HEADS // 2):
        w0 = _bf(jnp.where(tril, ws_ref[2 * j], 0.0))
        w1 = _bf(jnp.where(tril, ws_ref[2 * j + 1], 0.0))
        bias = jnp.where(lo, bst_ref[:, 2 * j:2 * j + 1], bst_ref[:, 2 * j + 1:2 * j + 2])
        for n in range(ts // CHUNK):
            v = vb[n * CHUNK:(n + 1) * CHUNK, j * 128:(j + 1) * 128]
            mixed_scr[n * CHUNK:(n + 1) * CHUNK, j * 128:(j + 1) * 128] = jnp.where(lo, _dot(w0, v), _dot(w1, v)) + bias


def _exchange(srcs, scatter, name):
    n = len(srcs)

    def body(*refs):
        src_refs, out_refs = refs[:n], refs[n:2 * n]
        send_sems, recv_sems, local_sems = refs[2 * n:]
        x, y, c = lax.axis_index("x"), lax.axis_index("y"), lax.axis_index("c")
        me = 4 * x + 2 * y + c

        def peer_of(mask):
            px = x if not (mask >> 2) & 1 else 1 - x
            py = y if not (mask >> 1) & 1 else 1 - y
            pc = c if not mask & 1 else 1 - c
            return (px, py, pc), 4 * px + 2 * py + pc

        def remote(k, mask):
            peer, pidx = peer_of(mask)
            return pltpu.make_async_remote_copy(
                src_ref=src_refs[k].at[pidx] if scatter else src_refs[k],
                dst_ref=out_refs[k].at[me],
                send_sem=send_sems.at[k, mask - 1], recv_sem=recv_sems.at[k, mask - 1],
                device_id=peer, device_id_type=pl.DeviceIdType.MESH)

        def arrival(k, mask):
            peer, pidx = peer_of(mask)
            return pltpu.make_async_remote_copy(
                src_ref=src_refs[k].at[pidx] if scatter else src_refs[k],
                dst_ref=out_refs[k].at[pidx],
                send_sem=send_sems.at[k, mask - 1], recv_sem=recv_sems.at[k, mask - 1],
                device_id=peer, device_id_type=pl.DeviceIdType.MESH)

        sends, locals_ = [], []
        for k in range(n):
            for mask in range(1, N_DEV):
                cp = remote(k, mask)
                cp.start()
                sends.append(cp)
            lc = pltpu.make_async_copy(src_refs[k].at[me] if scatter else src_refs[k], out_refs[k].at[me],
                                       local_sems.at[k])
            lc.start()
            locals_.append(lc)
        for k in range(n):
            for mask in range(1, N_DEV):
                arrival(k, mask).wait_recv()
        for cp in sends:
            cp.wait_send()
        for lc in locals_:
            lc.wait()

    outs = [SDS((N_DEV,) + tuple(s.shape[1:] if scatter else s.shape), s.dtype) for s in srcs]
    hbm = pl.BlockSpec(memory_space=pl.ANY)
    return pl.pallas_call(
        body, name=name, out_shape=outs, in_specs=[hbm] * n, out_specs=[hbm] * n,
        scratch_shapes=[pltpu.SemaphoreType.DMA((n, N_DEV - 1)), pltpu.SemaphoreType.DMA((n, N_DEV - 1)),
                        pltpu.SemaphoreType.DMA((n,))],
    )(*srcs)


def _peer_of(mask):
    x, y, c = lax.axis_index("x"), lax.axis_index("y"), lax.axis_index("c")
    px = 1 - x if (mask >> 2) & 1 else x
    py = 1 - y if (mask >> 1) & 1 else y
    pc = 1 - c if mask & 1 else c
    return (px, py, pc), 4 * px + 2 * py + pc


ALL_PEERS = tuple(range(1, N_DEV))
OTHER_CHIPS = (2, 4, 6)
SIBLING = 1


def _split_copy(src_ref, land_ref, send_sem, recv_sem, mask, slot, scatter, outgoing):
    x, y, c = lax.axis_index("x"), lax.axis_index("y"), lax.axis_index("c")
    me = 4 * x + 2 * y + c
    peer, pidx = _peer_of(mask)
    return pltpu.make_async_remote_copy(
        src_ref=src_ref.at[pidx] if scatter else src_ref,
        dst_ref=land_ref.at[me if outgoing else pidx],
        send_sem=send_sem.at[slot], recv_sem=recv_sem.at[slot],
        device_id=peer, device_id_type=pl.DeviceIdType.MESH)


_HBM = pl.BlockSpec(memory_space=pltpu.HBM)
_SEM = pl.BlockSpec(memory_space=pltpu.SEMAPHORE)
_EFFECT = pltpu.SideEffectType.DATAFLOW_SIDE_EFFECTING


def _exchange_start(srcs, scatter, name, masks=ALL_PEERS):
    n = len(srcs)
    lands = [pltpu.with_memory_space_constraint(
        lax.empty((N_DEV,) + tuple(s.shape[1:] if sc else s.shape), s.dtype), pltpu.HBM)
        for s, sc in zip(srcs, scatter)]
    srcs = [pltpu.with_memory_space_constraint(s, pltpu.HBM) for s in srcs]

    def body(*refs):
        src_refs, land_refs = refs[:n], refs[n:2 * n]
        send_sems, recv_sems = refs[2 * n:3 * n], refs[3 * n:4 * n]
        token = refs[-1]
        for k in range(n):
            for slot, mask in enumerate(masks):
                _split_copy(src_refs[k], land_refs[k], send_sems[k], recv_sems[k], mask, slot, scatter[k], True).start()
        token[...] = jnp.zeros_like(token)

    sem = pltpu.SemaphoreType.DMA((len(masks),))
    out = pl.pallas_call(
        body, name=name,
        out_shape=tuple([sem] * (2 * n) + [pltpu.HBM(s.shape, s.dtype) for s in srcs]
                        + [pltpu.HBM(l.shape, l.dtype) for l in lands] + [SDS((8, 128), F32)]),
        in_specs=[_HBM] * (2 * n),
        out_specs=tuple([_SEM] * (2 * n) + [_HBM] * (2 * n) + [pl.BlockSpec(memory_space=pltpu.VMEM)]),
        input_output_aliases={i: 2 * n + i for i in range(2 * n)},
        compiler_params=pltpu.CompilerParams(has_side_effects=_EFFECT),
    )(*srcs, *lands)
    return out[:n], out[n:2 * n], out[2 * n:3 * n], out[3 * n:4 * n], out[-1]


def _exchange_wait(send_sems, recv_sems, srcs_thru, lands_thru, scatter, after, name, masks=ALL_PEERS):
    n = len(srcs_thru)

    def body(*refs):
        src_refs, land_refs = refs[:n], refs[n:2 * n]
        send_refs, recv_refs = refs[2 * n:3 * n], refs[3 * n:4 * n]
        for k in range(n):
            for slot, mask in enumerate(masks):
                args = (src_refs[k], land_refs[k], send_refs[k], recv_refs[k], mask, slot, scatter[k])
                _split_copy(*args, True).wait_send()
                _split_copy(*args, False).wait_recv()

    out = pl.pallas_call(
        body, name=name,
        out_shape=tuple([pltpu.HBM(s.shape, s.dtype) for s in srcs_thru]
                        + [pltpu.HBM(l.shape, l.dtype) for l in lands_thru]),
        in_specs=[_HBM] * (2 * n) + [_SEM] * (2 * n) + [pl.BlockSpec(memory_space=pl.ANY)],
        out_specs=tuple([_HBM] * (2 * n)),
        input_output_aliases={i: i for i in range(2 * n)},
        compiler_params=pltpu.CompilerParams(has_side_effects=_EFFECT),
    )(*srcs_thru, *lands_thru, *send_sems, *recv_sems, after)
    return out[:n], out[n:]


def _forward_start(lands, name):
    n = len(lands)

    def body(*refs):
        land_refs, send_sems, recv_sems, token = refs[:n], refs[n:2 * n], refs[2 * n:3 * n], refs[-1]
        sibling, _ = _peer_of(SIBLING)
        for k in range(n):
            for slot, mask in enumerate(OTHER_CHIPS):
                _, mine = _peer_of(mask)
                pltpu.make_async_remote_copy(
                    src_ref=land_refs[k].at[mine], dst_ref=land_refs[k].at[mine], send_sem=send_sems[k].at[slot],
                    recv_sem=recv_sems[k].at[slot], device_id=sibling, device_id_type=pl.DeviceIdType.MESH).start()
        token[...] = jnp.zeros_like(token)

    sem = pltpu.SemaphoreType.DMA((len(OTHER_CHIPS),))
    out = pl.pallas_call(
        body, name=name,
        out_shape=tuple([sem] * (2 * n) + [pltpu.HBM(l.shape, l.dtype) for l in lands] + [SDS((8, 128), F32)]),
        in_specs=[_HBM] * n,
        out_specs=tuple([_SEM] * (2 * n) + [_HBM] * n + [pl.BlockSpec(memory_space=pltpu.VMEM)]),
        input_output_aliases={i: 2 * n + i for i in range(n)},
        compiler_params=pltpu.CompilerParams(has_side_effects=_EFFECT),
    )(*lands)
    return out[:n], out[n:2 * n], out[2 * n:3 * n], out[-1]


def _forward_wait(send_sems, recv_sems, lands, after, name):
    n = len(lands)

    def body(*refs):
        land_refs, send_refs, recv_refs = refs[:n], refs[n:2 * n], refs[2 * n:3 * n]
        sibling, _ = _peer_of(SIBLING)
        for k in range(n):
            for slot, mask in enumerate(OTHER_CHIPS):
                _, mine = _peer_of(mask)
                _, theirs = _peer_of(mask | SIBLING)
                for block, wait in ((mine, "wait_send"), (theirs, "wait_recv")):
                    getattr(pltpu.make_async_remote_copy(
                        src_ref=land_refs[k].at[block], dst_ref=land_refs[k].at[block],
                        send_sem=send_refs[k].at[slot], recv_sem=recv_refs[k].at[slot], device_id=sibling,
                        device_id_type=pl.DeviceIdType.MESH), wait)()

    return pl.pallas_call(
        body, name=name, out_shape=tuple(pltpu.HBM(l.shape, l.dtype) for l in lands),
        in_specs=[_HBM] * n + [_SEM] * (2 * n) + [pl.BlockSpec(memory_space=pl.ANY)],
        out_specs=tuple([_HBM] * n), input_output_aliases={i: i for i in range(n)},
        compiler_params=pltpu.CompilerParams(has_side_effects=_EFFECT),
    )(*lands, *send_sems, *recv_sems, after)


def _with_own(landed, own, me):
    return lax.dynamic_update_slice_in_dim(landed, own[None], me, axis=0)


def _kv_proj(mem, g_mem, wkv3):
    m = mem.shape[0]

    def body(mem_ref, g_ref, w_ref, kv_ref, mn_ref):
        y, _, _ = _rms_fwd(mem_ref[...], g_ref[...])
        yb = _bf(y)
        mn_ref[...] = yb
        for b in range(N_DEV):
            kv_ref[:, 256 * b:256 * (b + 1)] = _dot(yb, w_ref[b])

    return pl.pallas_call(body, name="kv_proj", out_shape=(SDS((m, 2048), F32), SDS(mem.shape, BF16)),
                          compiler_params=_cparams())(mem, g_mem, wkv3)


def _fwd_in(x, g_mix, w_in3, b_in, ts):
    s, d = x.shape

    def body(x_ref, g_ref, w_ref, b_ref, z_ref, a_ref):
        hn, _, _ = _rms_fwd(x_ref[...], g_ref[...])
        hb = _bf(hn)
        for b in range(N_DEV):
            z_ref[:, 256 * b:256 * (b + 1)] = _dot(hb, w_ref[b]) + b_ref[:, 256 * b:256 * (b + 1)]
        a_ref[...] = z_ref[:, 0:512] * jax.nn.sigmoid(z_ref[:, 512:1024])

    return pl.pallas_call(
        body, name="fwd_in", grid=(s // ts,),
        in_specs=[_row(ts, d), _const(g_mix.shape), _const(w_in3.shape), _const(b_in.shape)],
        out_specs=(_row(ts, 2048), _row(ts, 512)),
        out_shape=(SDS((s, 2048), F32), SDS((s, 512), F32)),
        compiler_params=_cparams("arbitrary"))(x, g_mix, w_in3, b_in)


def _conv_fwd(a, w, b):
    s, cw = a.shape
    rc = 256 if s % 256 == 0 else 128

    def body(a_ref, w_ref, b_ref, c_ref, pad):
        pad[0:CONV_PAD, :] = jnp.zeros((CONV_PAD, 128), F32)
        pad[CONV_PAD:, :] = a_ref[...]

        def chunk(i, carry):
            r0 = pl.multiple_of(i * rc, rc)
            acc = jnp.zeros((rc, 128), F32) + b_ref[...]
            for res, offs in _by_residue(range(CONV_PAD - CONV_K + 1, CONV_PAD + 1)):
                shifted = pad[pl.ds(r0 + res, rc + offs[-1] - res), :]
                for off in offs:
                    k = off - (CONV_PAD - CONV_K + 1)
                    acc = acc + w_ref[k:k + 1, :] * shifted[off - res:off - res + rc, :]
            c_ref[pl.ds(r0, rc), :] = acc
            return carry

        lax.fori_loop(0, s // rc, chunk, 0)

    blk = lambda r: pl.BlockSpec((r, 128), lambda j: (0, j))
    return pl.pallas_call(
        body, name="conv_fwd", grid=(cw // 128,),
        in_specs=[blk(s), blk(CONV_K), blk(1)], out_specs=blk(s), out_shape=SDS((s, cw), F32),
        scratch_shapes=[pltpu.VMEM((s + CONV_PAD, 128), F32)],
        compiler_params=_cparams("arbitrary"))(a, w, b)


def _fwd_out(x, c, z, cln_g, cln_b, gln_g, gln_b, ws, bst, w_out, ts):
    s, d = x.shape

    def body(x_ref, c_ref, zuv_ref, clg, clb, glg, glb, ws_ref, bst_ref, wo_ref, h1_ref, mixed_scr):
        cl, _, _ = _ln_fwd(c_ref[...], clg[...], clb[...])
        co = cl * jax.nn.sigmoid(cl)
        u, _ = _gelu(zuv_ref[:, 0:512])
        vg, _ = _gelu(zuv_ref[:, 512:1024])
        vln, _, _ = _ln_fwd(vg, glg[...], glb[...])
        _mix_fwd(_bf(vln), ws_ref, bst_ref, mixed_scr, ts)
        gm = u * mixed_scr[...]
        h1_ref[...] = x_ref[...] + _dot(_bf(co), wo_ref[0:512, :]) + _dot(_bf(gm), wo_ref[512:1024, :])

    return pl.pallas_call(
        body, name="fwd_out", grid=(s // ts,),
        in_specs=[_row(ts, d), _row(ts, 512), _row(ts, 1024, 1), _const(cln_g.shape), _const(cln_b.shape),
                  _const(gln_g.shape), _const(gln_b.shape), _const(ws.shape), _const(bst.shape), _const(w_out.shape)],
        out_specs=_row(ts, d), out_shape=SDS((s, d), F32),
        scratch_shapes=[pltpu.VMEM((ts, 512), F32)],
        compiler_params=_cparams("arbitrary"))(x, c, z, cln_g, cln_b, gln_g, gln_b, ws, bst, w_out)


def _softmax_rows(sc):
    m = jnp.max(sc, axis=-1, keepdims=True)
    e = jnp.exp(sc - m)
    return e / jnp.sum(e, axis=-1, keepdims=True)


def _fwd_xa(h1, g_xa, wq, kv, wo, ts):
    s, d = h1.shape
    scale = XA_DH ** -0.5

    def body(h_ref, g_ref, wq_ref, kv_ref, wo_ref, h2_ref, o_scr):
        hn, _, _ = _rms_fwd(h_ref[...], g_ref[...])
        q = _dot(_bf(hn), wq_ref[...])
        for h in range(XA_HEADS):
            qh = _bf(q[:, XA_DH * h:XA_DH * (h + 1)])
            kh = _bf(kv_ref[:, XA_DH * h:XA_DH * (h + 1)])
            vh = _bf(kv_ref[:, d + XA_DH * h:d + XA_DH * (h + 1)])
            p = _softmax_rows(_dot(qh, kh, NT) * scale)
            o_scr[:, XA_DH * h:XA_DH * (h + 1)] = _dot(_bf(p), vh)
        h2_ref[...] = h_ref[...] + _dot(_bf(o_scr[...]), wo_ref[...])

    return pl.pallas_call(
        body, name="fwd_xa", grid=(s // ts,),
        in_specs=[_row(ts, d), _const(g_xa.shape), _const(wq.shape), _const(kv.shape), _const(wo.shape)],
        out_specs=_row(ts, d), out_shape=SDS((s, d), F32),
        scratch_shapes=[pltpu.VMEM((ts, d), F32)],
        compiler_params=_cparams("arbitrary"))(h1, g_xa, wq, kv, wo)


def _fwd_ffn(h2, g_ffn, wgut, wdown, g_final, target, ts):
    s, d = h2.shape
    hid = wdown.shape[0]
    hc = hid // 2

    def body(h_ref, g_ref, wgu_ref, wd_ref, gf_ref, t_ref, dh3_ref, dh3b_ref, loss_ref, dgf_ref):
        hn, _, _ = _rms_fwd(h_ref[...], g_ref[...])
        hb = _bf(hn)
        h3 = h_ref[...]
        for n in range(2):
            g = _dot(hb, wgu_ref[hc * n:hc * (n + 1), :], NT)
            u = _dot(hb, wgu_ref[hid + hc * n:hid + hc * (n + 1), :], NT)
            act = g * jax.nn.sigmoid(g) * u
            h3 = h3 + _dot(_bf(act), wd_ref[hc * n:hc * (n + 1), :])
        y, xh, r = _rms_fwd(h3, gf_ref[...])
        diff = y - t_ref[...]
        part = 0.5 * jnp.sum(jnp.mean(diff * diff, axis=-1, keepdims=True), axis=0, keepdims=True)
        _accumulate(loss_ref, jnp.zeros(loss_ref.shape, F32) + part)
        dh3, dgf = _rms_bwd(diff * (1.0 / d), xh, r, gf_ref[...])
        dh3_ref[...] = dh3
        dh3b_ref[...] = _bf(dh3)
        _accumulate(dgf_ref, dgf)

    return pl.pallas_call(
        body, name="fwd_ffn", grid=(s // ts,),
        in_specs=[_row(ts, d), _const(g_ffn.shape), _const(wgut.shape), _const(wdown.shape), _const(g_final.shape),
                  _row(ts, d)],
        out_specs=(_row(ts, d), _row(ts, d), _acc((1, 128)), _acc((1, d))),
        out_shape=(SDS((s, d), F32), SDS((s, d), BF16), SDS((1, 128), F32), SDS((1, d), F32)),
        compiler_params=_cparams("arbitrary"))(h2, g_ffn, wgut, wdown, g_final, target)


def _bwd_ffn(h2, dh3, g_ffn, wgut, wdown, ts):
    s, d = h2.shape
    hid = wdown.shape[0]
    hc = hid // 2

    def body(h_ref, dh3_ref, g_ref, wgu_ref, wd_ref, dh2_ref, dh2b_ref, act_ref, dgu_ref, hn_ref, dg_ref):
        hn, xh, r = _rms_fwd(h_ref[...], g_ref[...])
        hb = _bf(hn)
        hn_ref[...] = hb
        db = _bf(dh3_ref[...])
        dhn = jnp.zeros((ts, d), F32)
        for n in range(2):
            wg = wgu_ref[hc * n:hc * (n + 1), :]
            wu = wgu_ref[hid + hc * n:hid + hc * (n + 1), :]
            g = _dot(hb, wg, NT)
            u = _dot(hb, wu, NT)
            sg = jax.nn.sigmoid(g)
            sl = g * sg
            act_ref[:, hc * n:hc * (n + 1)] = _bf(sl * u)
            dact = _dot(db, wd_ref[hc * n:hc * (n + 1), :], NT)
            dgb = _bf(dact * u * _silu_grad(g, sg))
            dub = _bf(dact * sl)
            dgu_ref[:, hc * n:hc * (n + 1)] = dgb
            dgu_ref[:, hid + hc * n:hid + hc * (n + 1)] = dub
            dhn = dhn + _dot(dgb, wg) + _dot(dub, wu)
        dx, dg = _rms_bwd(dhn, xh, r, g_ref[...])
        dh2 = dh3_ref[...] + dx
        dh2_ref[...] = dh2
        dh2b_ref[...] = _bf(dh2)
        _accumulate(dg_ref, dg)

    return pl.pallas_call(
        body, name="bwd_ffn", grid=(s // ts,),
        in_specs=[_row(ts, d), _row(ts, d), _const(g_ffn.shape), _const(wgut.shape), _const(wdown.shape)],
        out_specs=(_row(ts, d), _row(ts, d), _row(ts, hid), _row(ts, 2 * hid), _row(ts, d), _acc((1, d))),
        out_shape=(SDS((s, d), F32), SDS((s, d), BF16), SDS((s, hid), BF16), SDS((s, 2 * hid), BF16),
                   SDS((s, d), BF16), SDS((1, d), F32)),
        compiler_params=_cparams("arbitrary"))(h2, dh3, g_ffn, wgut, wdown)


def _bwd_xa(h1, dh2, g_xa, wq, wo, kv, ts):
    s, d = h1.shape
    scale = XA_DH ** -0.5

    def body(h_ref, dh2_ref, g_ref, wq_ref, wo_ref, kv_ref, dh1_ref, dh1b_ref, dq_ref, o_ref, hn_ref, dkv_ref,
             dg_ref, dq_scr):
        hn, xh, r = _rms_fwd(h_ref[...], g_ref[...])
        hb = _bf(hn)
        hn_ref[...] = hb
        q = _dot(hb, wq_ref[...])
        do = _dot(_bf(dh2_ref[...]), wo_ref[...], NT)

        @pl.when(pl.program_id(0) == 0)
        def _():
            dkv_ref[...] = jnp.zeros_like(dkv_ref)

        for h in range(XA_HEADS):
            lo, hi = XA_DH * h, XA_DH * (h + 1)
            qh = _bf(q[:, lo:hi])
            kh = _bf(kv_ref[:, lo:hi])
            vh = _bf(kv_ref[:, d + lo:d + hi])
            p = _softmax_rows(_dot(qh, kh, NT) * scale)
            pb = _bf(p)
            o_ref[:, lo:hi] = _bf(_dot(pb, vh))
            doh = _bf(do[:, lo:hi])
            dp = _dot(doh, vh, NT)
            ds = p * (dp - jnp.sum(p * dp, axis=-1, keepdims=True)) * scale
            dsb = _bf(ds)
            dq_scr[:, lo:hi] = _dot(dsb, kh)
            dkv_ref[:, lo:hi] += _dot(dsb, qh, TN)
            dkv_ref[:, d + lo:d + hi] += _dot(pb, doh, TN)
        dqb = _bf(dq_scr[...])
        dq_ref[...] = dqb
        dx, dg = _rms_bwd(_dot(dqb, wq_ref[...], NT), xh, r, g_ref[...])
        dh1 = dh2_ref[...] + dx
        dh1_ref[...] = dh1
        dh1b_ref[...] = _bf(dh1)
        _accumulate(dg_ref, dg)

    return pl.pallas_call(
        body, name="bwd_xa", grid=(s // ts,),
        in_specs=[_row(ts, d), _row(ts, d), _const(g_xa.shape), _const(wq.shape), _const(wo.shape), _const(kv.shape)],
        out_specs=(_row(ts, d), _row(ts, d), _row(ts, d), _row(ts, d), _row(ts, d), _acc(kv.shape), _acc((1, d))),
        out_shape=(SDS((s, d), F32), SDS((s, d), BF16), SDS((s, d), BF16), SDS((s, d), BF16), SDS((s, d), BF16),
                   SDS(kv.shape, F32), SDS((1, d), F32)),
        scratch_shapes=[pltpu.VMEM((ts, d), F32)],
        compiler_params=_cparams("arbitrary"))(h1, dh2, g_xa, wq, wo, kv)


def _bwd_kv(dkv, mn, mem, g_mem, wkv3):
    d = mem.shape[1]

    def body(dkv_ref, mn_ref, mem_ref, g_ref, w_ref, dw_ref, dg_ref):
        dkvb = _bf(dkv_ref[...])
        dmn = jnp.zeros(mem_ref.shape, F32)
        for b in range(N_DEV):
            blk = dkvb[:, 256 * b:256 * (b + 1)]
            dmn = dmn + _dot(blk, w_ref[b], NT)
            dw_ref[b] = _bf(_dot(mn_ref[...], blk, TN))
        _, xh, r = _rms_fwd(mem_ref[...], g_ref[...])
        _, dg = _rms_bwd(dmn, xh, r, g_ref[...])
        dg_ref[...] = dg

    return pl.pallas_call(body, name="bwd_kv", out_shape=(SDS(wkv3.shape, BF16), SDS((1, d), F32)),
                          compiler_params=_cparams())(dkv, mn, mem, g_mem, wkv3)


def _bwd_out(dh1, c, z, cln_g, cln_b, gln_g, gln_b, ws, wst, bst, w_out, ts):
    s, d = dh1.shape
    nh = GM_HEADS

    def body(dh1_ref, c_ref, zuv_ref, clg, clb, glg, glb, ws_ref, wst_ref, bst_ref, wo_ref,
             cat_ref, dc_ref, dzuv_ref, dws_ref, dbst_ref, dclg_ref, dclb_ref, dglg_ref, dglb_ref, dbin_ref,
             mixed_scr, dv_scr):
        cl, chat, crs = _ln_fwd(c_ref[...], clg[...], clb[...])
        sg = jax.nn.sigmoid(cl)
        zu = zuv_ref[:, 0:512]
        zv = zuv_ref[:, 512:1024]
        u, tu = _gelu(zu)
        vg, tv = _gelu(zv)
        vln, vhat, vrs = _ln_fwd(vg, glg[...], glb[...])
        vb = _bf(vln)
        _mix_fwd(vb, ws_ref, bst_ref, mixed_scr, ts)
        mixed = mixed_scr[...]
        cat_ref[:, 0:512] = _bf(cl * sg)
        cat_ref[:, 512:1024] = _bf(u * mixed)
        dcat = _dot(_bf(dh1_ref[...]), wo_ref[...], NT)
        dgm = dcat[:, 512:1024]
        dc, dclg, dclb = _ln_bwd(dcat[:, 0:512] * _silu_grad(cl, sg), chat, crs, clg[...])
        dc_ref[...] = dc
        dzu = dgm * mixed * _gelu_grad(zu, tu)
        dm = dgm * u

        @pl.when(pl.program_id(0) == 0)
        def _():
            dws_ref[...] = jnp.zeros_like(dws_ref)
            dbst_ref[...] = jnp.zeros_like(dbst_ref)

        tril, triu, lo = _mix_masks()
        head = lax.broadcasted_iota(jnp.int32, (1, nh), 1)
        for j in range(nh // 2):
            w0t = _bf(jnp.where(triu, wst_ref[2 * j], 0.0))
            w1t = _bf(jnp.where(triu, wst_ref[2 * j + 1], 0.0))
            for n in range(ts // CHUNK):
                rows = slice(n * CHUNK, (n + 1) * CHUNK)
                lanes = slice(j * 128, (j + 1) * 128)
                dmc = dm[rows, lanes]
                dmb = _bf(dmc)
                dv_scr[rows, lanes] = jnp.where(lo, _dot(w0t, dmb), _dot(w1t, dmb))
                vc = vb[rows, lanes]
                d0 = jnp.where(lo, dmc, 0.0)
                d1 = dmc - d0
                dws_ref[2 * j] += jnp.where(tril, _dot(_bf(d0), vc, NT), 0.0)
                dws_ref[2 * j + 1] += jnp.where(tril, _dot(_bf(d1), vc, NT), 0.0)
                dbst_ref[...] += (jnp.sum(d0, axis=1, keepdims=True) * (head == 2 * j).astype(F32)
                                  + jnp.sum(d1, axis=1, keepdims=True) * (head == 2 * j + 1).astype(F32))
        dvg, dglg, dglb = _ln_bwd(dv_scr[...], vhat, vrs, glg[...])
        dzv = dvg * _gelu_grad(zv, tv)
        dzuv_ref[:, 0:512] = _bf(dzu)
        dzuv_ref[:, 512:1024] = _bf(dzv)
        _accumulate(dclg_ref, dclg)
        _accumulate(dclb_ref, dclb)
        _accumulate(dglg_ref, dglg)
        _accumulate(dglb_ref, dglb)
        _accumulate(dbin_ref, jnp.concatenate([jnp.sum(dzu, axis=0, keepdims=True),
                                               jnp.sum(dzv, axis=0, keepdims=True)], axis=1))

    vec = (1, 512)
    return pl.pallas_call(
        body, name="bwd_out", grid=(s // ts,),
        in_specs=[_row(ts, d), _row(ts, 512), _row(ts, 1024, 1), _const(cln_g.shape), _const(cln_b.shape),
                  _const(gln_g.shape), _const(gln_b.shape), _const(ws.shape), _const(wst.shape), _const(bst.shape),
                  _const(w_out.shape)],
        out_specs=(_row(ts, d), _row(ts, 512), _row(ts, 1024), _acc(ws.shape), _acc(bst.shape), _acc(vec), _acc(vec),
                   _acc(vec), _acc(vec), _acc((1, 1024))),
        out_shape=(SDS((s, d), BF16), SDS((s, 512), F32), SDS((s, 1024), BF16), SDS(ws.shape, F32),
                   SDS(bst.shape, F32), SDS(vec, F32), SDS(vec, F32), SDS(vec, F32), SDS(vec, F32), SDS((1, 1024), F32)),
        scratch_shapes=[pltpu.VMEM((ts, 512), F32), pltpu.VMEM((ts, 512), F32)],
        compiler_params=_cparams("arbitrary"))(dh1, c, z, cln_g, cln_b, gln_g, gln_b, ws, wst, bst, w_out)


def _conv_bwd(dc, a, w):
    s, cw = a.shape
    rc = 256 if s % 256 == 0 else 128

    def body(dc_ref, a_ref, w_ref, da_ref, dw_ref, db_ref, pad_a, pad_d, part):
        pad_a[0:CONV_PAD, :] = jnp.zeros((CONV_PAD, 128), F32)
        pad_a[CONV_PAD:, :] = a_ref[...]
        pad_d[0:s, :] = dc_ref[...]
        pad_d[s:, :] = jnp.zeros((CONV_PAD, 128), F32)
        part[...] = jnp.zeros_like(part)

        def rows8(v):
            return jnp.sum(v.reshape(rc // 8, 8, 128), axis=0)

        def chunk(i, carry):
            r0 = pl.multiple_of(i * rc, rc)
            dcc = pad_d[pl.ds(r0, rc), :]
            acc = jnp.zeros((rc, 128), F32)
            for res, offs in _by_residue(range(0, CONV_K)):
                shifted = pad_d[pl.ds(r0 + res, rc + offs[-1] - res), :]
                for off in offs:
                    k = CONV_K - 1 - off
                    acc = acc + w_ref[k:k + 1, :] * shifted[off - res:off - res + rc, :]
            da_ref[pl.ds(r0, rc), :] = acc
            for res, offs in _by_residue(range(CONV_PAD - CONV_K + 1, CONV_PAD + 1)):
                shifted = pad_a[pl.ds(r0 + res, rc + offs[-1] - res), :]
                for off in offs:
                    k = off - (CONV_PAD - CONV_K + 1)
                    part[8 * k:8 * k + 8, :] += rows8(dcc * shifted[off - res:off - res + rc, :])
            part[8 * CONV_PAD:, :] += rows8(dcc)
            return carry

        lax.fori_loop(0, s // rc, chunk, 0)
        sums = jnp.sum(part[...].reshape(CONV_PAD + 1, 8, 128), axis=1)
        dw_ref[...] = sums[0:CONV_PAD, :]
        db_ref[...] = sums[CONV_PAD:, :]

    blk = lambda r: pl.BlockSpec((r, 128), lambda j: (0, j))
    return pl.pallas_call(
        body, name="conv_bwd", grid=(cw // 128,),
        in_specs=[blk(s), blk(s), blk(CONV_K)], out_specs=(blk(s), blk(CONV_PAD), blk(1)),
        out_shape=(SDS((s, cw), F32), SDS((CONV_PAD, cw), F32), SDS((1, cw), F32)),
        scratch_shapes=[pltpu.VMEM((s + CONV_PAD, 128), F32), pltpu.VMEM((s + CONV_PAD, 128), F32),
                        pltpu.VMEM((8 * (CONV_PAD + 1), 128), F32)],
        compiler_params=_cparams("arbitrary"))(dc, a, w)


def _bwd_in(x, dh1, da, z, dzuv, g_mix, w_in3, ts):
    s, d = x.shape

    def body(x_ref, dh1_ref, da_ref, zag_ref, dzuv_ref, g_ref, w_ref, dx_ref, dz_ref, hn_ref, dbin_ref, dg_ref):
        za = zag_ref[:, 0:512]
        sg = jax.nn.sigmoid(zag_ref[:, 512:1024])
        da_ = da_ref[...]
        dza = da_ * sg
        dzg = da_ * za * sg * (1.0 - sg)
        dz_ref[:, 0:512] = _bf(dza)
        dz_ref[:, 512:1024] = _bf(dzg)
        dz_ref[:, 1024:2048] = dzuv_ref[...]
        dhn = jnp.zeros((ts, d), F32)
        for b in range(N_DEV):
            dhn = dhn + _dot(dz_ref[:, 256 * b:256 * (b + 1)], w_ref[b], NT)
        hn, xh, r = _rms_fwd(x_ref[...], g_ref[...])
        hn_ref[...] = _bf(hn)
        dxn, dg = _rms_bwd(dhn, xh, r, g_ref[...])
        dx_ref[...] = dh1_ref[...] + dxn
        _accumulate(dg_ref, dg)
        _accumulate(dbin_ref, jnp.concatenate([jnp.sum(dza, axis=0, keepdims=True),
                                               jnp.sum(dzg, axis=0, keepdims=True)], axis=1))

    return pl.pallas_call(
        body, name="bwd_in", grid=(s // ts,),
        in_specs=[_row(ts, d), _row(ts, d), _row(ts, 512), _row(ts, 1024, 0), _row(ts, 1024), _const(g_mix.shape),
                  _const(w_in3.shape)],
        out_specs=(_row(ts, d), _row(ts, 2048), _row(ts, d), _acc((1, 1024)), _acc((1, d))),
        out_shape=(SDS((s, d), F32), SDS((s, 2048), BF16), SDS((s, d), BF16), SDS((1, 1024), F32), SDS((1, d), F32)),
        compiler_params=_cparams("arbitrary"))(x, dh1, da, z, dzuv, g_mix, w_in3)


def _tn_matmul(a, b, tm, name, col_blocks=False, behind=None):
    s, m = a.shape
    n = b.shape[1]
    cb = 256
    tn = 2 * cb if col_blocks else n
    if col_blocks:
        tm = m
    ts = min(s, 512)
    n_s = s // ts

    def body(a_ref, b_ref, *rest):
        o_ref, acc = rest[-2:]
        k = pl.program_id(2)

        @pl.when(k == 0)
        def _():
            acc[...] = jnp.zeros_like(acc)

        acc[...] += _dot(_bf(a_ref[...]), _bf(b_ref[...]), TN)

        @pl.when(k == n_s - 1)
        def _():
            if col_blocks:
                for blk in range(tn // cb):
                    o_ref[blk] = _bf(acc[:, cb * blk:cb * (blk + 1)])
            else:
                o_ref[...] = _bf(acc[...])

    if col_blocks:
        out_shape = SDS((n // cb, m, cb), BF16)
        out_spec = pl.BlockSpec((tn // cb, m, cb), lambda i, j, k: (j, 0, 0))
    else:
        out_shape = SDS((m, n), BF16)
        out_spec = pl.BlockSpec((tm, tn), lambda i, j, k: (i, j))
    extra = [] if behind is None else [behind]
    return pl.pallas_call(
        body, name=name, grid=(m // tm, n // tn, n_s),
        in_specs=[pl.BlockSpec((ts, tm), lambda i, j, k: (k, i)), pl.BlockSpec((ts, tn), lambda i, j, k: (k, j))]
        + [pl.BlockSpec(memory_space=pl.ANY)] * len(extra),
        out_specs=out_spec, out_shape=out_shape,
        scratch_shapes=[pltpu.VMEM((tm, tn), F32)],
        compiler_params=_cparams("parallel", "parallel", "arbitrary"))(a, b, *extra)


def _sum8(parts, name):
    _, r, c = parts.shape
    tr = r
    for cand in (512, 256, 352, 128, 8):
        if r % cand == 0 and cand <= r:
            tr = cand
            break

    def body(p_ref, o_ref):
        acc = p_ref[0].astype(F32)
        for dev in range(1, N_DEV):
            acc = acc + p_ref[dev].astype(F32)
        o_ref[...] = acc

    return pl.pallas_call(
        body, name=name, grid=(r // tr,),
        in_specs=[pl.BlockSpec((N_DEV, tr, c), lambda i: (0, i, 0))],
        out_specs=pl.BlockSpec((tr, c), lambda i: (i, 0)), out_shape=SDS((r, c), F32),
        compiler_params=_cparams("parallel"))(parts)


def _adamw(w, g, m, v, name):
    r, c = w.shape
    tr = r
    for cand in (256, 128, 176, 8):
        if r % cand == 0 and cand <= r:
            tr = cand
            break

    def body(w_ref, g_ref, m_ref, v_ref, d_ref, m2_ref, v2_ref):
        g_ = g_ref[...]
        m2 = ADAM_B1 * m_ref[...] + (1.0 - ADAM_B1) * g_
        v2 = ADAM_B2 * v_ref[...] + (1.0 - ADAM_B2) * (g_ * g_)
        m_hat = m2 / (1.0 - ADAM_B1 ** ADAM_STEP)
        v_hat = v2 / (1.0 - ADAM_B2 ** ADAM_STEP)
        d_ref[...] = -ADAM_LR * (m_hat / (jnp.sqrt(v_hat) + ADAM_EPS) + ADAM_WD * w_ref[...])
        m2_ref[...] = m2
        v2_ref[...] = v2

    blk = pl.BlockSpec((tr, c), lambda i: (i, 0))
    return pl.pallas_call(
        body, name=name, grid=(r // tr,), in_specs=[blk] * 4, out_specs=(blk,) * 3,
        out_shape=(SDS((r, c), F32),) * 3, compiler_params=_cparams("parallel"))(w, g, m, v)


def _pack(arrs):
    flat = jnp.concatenate([a.reshape(-1) for a in arrs])
    pad = (-flat.shape[0]) % (128 * 128)
    return jnp.pad(flat, (0, pad)).reshape(-1, 128)


def _unpack(packed, shapes):
    flat = packed.reshape(-1)
    out, off = [], 0
    for shp in shapes:
        size = 1
        for dim in shp:
            size *= dim
        out.append(flat[off:off + size].reshape(shp))
        off += size
    return out


SMALL = ("norm_mix_g", "b_in", "conv_w", "conv_b", "conv_ln_g", "conv_ln_b", "gm_ln_g", "gm_ln_b", "gm_w_s", "gm_b_s",
         "norm_xa_g", "mem_norm_g", "norm_ffn_g", "final_norm_g")
BIG = ("w_in", "w_out", "xa_wq", "xa_wkv", "xa_wo", "ffn_w_gate_up", "ffn_w_down")
WEIGHTS = ("norm_mix_g", "w_in", "b_in", "conv_w", "conv_b", "conv_ln_g", "conv_ln_b", "gm_ln_g", "gm_ln_b", "gm_w_s",
           "gm_b_s", "w_out", "norm_xa_g", "mem_norm_g", "xa_wq", "xa_wkv", "xa_wo", "norm_ffn_g", "ffn_w_gate_up",
           "ffn_w_down", "final_norm_g")


def kernel(x, mem, norm_mix_g, w_in, b_in, conv_w, conv_b, conv_ln_g, conv_ln_b, gm_ln_g, gm_ln_b, gm_w_s, gm_b_s, w_out, norm_xa_g, mem_norm_g, xa_wq, xa_wkv, xa_wo, norm_ffn_g, ffn_w_gate_up, ffn_w_down, final_norm_g, loss_target, m_norm_mix_g, m_w_in, m_b_in, m_conv_w, m_conv_b, m_conv_ln_g, m_conv_ln_b, m_gm_ln_g, m_gm_ln_b, m_gm_w_s, m_gm_b_s, m_w_out, m_norm_xa_g, m_mem_norm_g, m_xa_wq, m_xa_wkv, m_xa_wo, m_norm_ffn_g, m_ffn_w_gate_up, m_ffn_w_down, m_final_norm_g, v_norm_mix_g, v_w_in, v_b_in, v_conv_w, v_conv_b, v_conv_ln_g, v_conv_ln_b, v_gm_ln_g, v_gm_ln_b, v_gm_w_s, v_gm_b_s, v_w_out, v_norm_xa_g, v_mem_norm_g, v_xa_wq, v_xa_wkv, v_xa_wo, v_norm_ffn_g, v_ffn_w_gate_up, v_ffn_w_down, v_final_norm_g):
    w = dict(norm_mix_g=norm_mix_g, w_in=w_in, b_in=b_in, conv_w=conv_w, conv_b=conv_b, conv_ln_g=conv_ln_g,
             conv_ln_b=conv_ln_b, gm_ln_g=gm_ln_g, gm_ln_b=gm_ln_b, gm_w_s=gm_w_s, gm_b_s=gm_b_s, w_out=w_out,
             norm_xa_g=norm_xa_g, mem_norm_g=mem_norm_g, xa_wq=xa_wq, xa_wkv=xa_wkv, xa_wo=xa_wo,
             norm_ffn_g=norm_ffn_g, ffn_w_gate_up=ffn_w_gate_up, ffn_w_down=ffn_w_down, final_norm_g=final_norm_g)
    mom = dict(norm_mix_g=m_norm_mix_g, w_in=m_w_in, b_in=m_b_in, conv_w=m_conv_w, conv_b=m_conv_b,
               conv_ln_g=m_conv_ln_g, conv_ln_b=m_conv_ln_b, gm_ln_g=m_gm_ln_g, gm_ln_b=m_gm_ln_b, gm_w_s=m_gm_w_s,
               gm_b_s=m_gm_b_s, w_out=m_w_out, norm_xa_g=m_norm_xa_g, mem_norm_g=m_mem_norm_g, xa_wq=m_xa_wq,
               xa_wkv=m_xa_wkv, xa_wo=m_xa_wo, norm_ffn_g=m_norm_ffn_g, ffn_w_gate_up=m_ffn_w_gate_up,
               ffn_w_down=m_ffn_w_down, final_norm_g=m_final_norm_g)
    var = dict(norm_mix_g=v_norm_mix_g, w_in=v_w_in, b_in=v_b_in, conv_w=v_conv_w, conv_b=v_conv_b,
               conv_ln_g=v_conv_ln_g, conv_ln_b=v_conv_ln_b, gm_ln_g=v_gm_ln_g, gm_ln_b=v_gm_ln_b, gm_w_s=v_gm_w_s,
               gm_b_s=v_gm_b_s, w_out=v_w_out, norm_xa_g=v_norm_xa_g, mem_norm_g=v_mem_norm_g, xa_wq=v_xa_wq,
               xa_wkv=v_xa_wkv, xa_wo=v_xa_wo, norm_ffn_g=v_norm_ffn_g, ffn_w_gate_up=v_ffn_w_gate_up,
               ffn_w_down=v_ffn_w_down, final_norm_g=v_final_norm_g)

    me = 4 * lax.axis_index("x") + 2 * lax.axis_index("y") + lax.axis_index("c")
    s, d = x.shape[1], x.shape[2]
    xs = x.reshape(s, d)
    mems = mem.reshape(mem.shape[1], d)
    tgt = loss_target.reshape(s, d)
    ts = min(512, s)
    ts_ffn = min(256, s)
    row = lambda a: a.reshape(1, -1)

    conv_w_pad = jnp.pad(conv_w, ((0, CONV_PAD - CONV_K), (0, 128 - conv_w.shape[1])))
    shards = [_bf(w_in), conv_w_pad, _bf(xa_wkv), _bf(w_out), _bf(xa_wq), _bf(xa_wo), _bf(ffn_w_gate_up.T),
              _bf(ffn_w_down)]
    first_level = (SIBLING,) + OTHER_CHIPS
    g_send, g_recv, g_src, g_land, g_tok = _exchange_start(shards, [False] * len(shards), "gather_start", first_level)

    def arrived(idx, after, name):
        pick = lambda seq: [seq[i] for i in idx]
        srcs, lands = _exchange_wait(pick(g_send), pick(g_recv), pick(g_src), pick(g_land), [False] * len(idx), after,
                                     "gather_wait_" + name, first_level)
        return _forward_start([_with_own(land, src, me) for src, land in zip(srcs, lands)], "forward_start_" + name)

    def complete(handle, after, name):
        return _forward_wait(handle[0], handle[1], handle[2], after, "forward_wait_" + name)

    bst = gm_b_s.T
    wst = jnp.swapaxes(gm_w_s, 1, 2)

    h_in = arrived((0, 1), g_tok, "in")
    w_in3, conv_w8 = complete(h_in, h_in[3], "in")
    conv_w_f = conv_w8[:, :CONV_K, :conv_w.shape[1]].transpose(1, 0, 2).reshape(CONV_K, -1)
    cw = conv_w_f.shape[1]
    z, a = _fwd_in(xs, row(norm_mix_g), w_in3, row(b_in), ts)
    h_out = arrived((2, 3), z, "out")
    h_xa = arrived((4, 5), h_out[3], "xa")
    c = _conv_fwd(a, conv_w_f + h_xa[3][0:1, 0:1], row(conv_b))
    wkv3, w_out3 = complete(h_out, c, "out")
    w_out_f = w_out3.reshape(-1, d)
    kv, mn = _kv_proj(mems, row(mem_norm_g), wkv3)
    h1 = _fwd_out(xs, c, z, row(conv_ln_g), row(conv_ln_b), row(gm_ln_g), row(gm_ln_b), gm_w_s, bst, w_out_f, ts)
    h_gut = arrived((6,), h1, "gate_up")
    wq3, wo3 = complete(h_xa, h_gut[3], "xa")
    wq_f = wq3.reshape(-1, d)
    wo_f = wo3.reshape(-1, d)
    h2 = _fwd_xa(h1, row(norm_xa_g), wq_f, kv, wo_f, ts)
    h_down = arrived((7,), h2, "down")
    (wgut3,) = complete(h_gut, h_down[3], "gate_up")
    (wdown3,) = complete(h_down, wgut3, "down")
    wgut_f = wgut3.reshape(-1, d)
    wdown_f = wdown3.reshape(-1, d)
    dh3, dh3b, loss_p, d_final_g = _fwd_ffn(h2, row(norm_ffn_g), wgut_f, wdown_f, row(final_norm_g), tgt, ts_ffn)

    blocks = lambda m: m.reshape(N_DEV, -1, d)
    tok = lambda ex: ex[4][0:1, 0:1]
    dh2, dh2b, act, dgu, hn2, d_ffn_g = _bwd_ffn(h2, dh3, row(norm_ffn_g), wgut_f, wdown_f, ts_ffn)
    dwgut = _tn_matmul(dgu, hn2, 1408, "dw_gate_up")
    dwdown = _tn_matmul(act, dh3b, 1408, "dw_down")
    ex_ffn = _exchange_start([blocks(dwgut), blocks(dwdown)], [True, True], "scatter_start_ffn")
    dh1, dh1b, dq, o, hn1, dkv, d_xa_g = _bwd_xa(h1, dh2, row(norm_xa_g) + tok(ex_ffn), wq_f, wo_f, kv, ts)
    dwq = _tn_matmul(hn1, dq, 1024, "dw_q")
    dwo = _tn_matmul(o, dh2b, 1024, "dw_o")
    dwkv3, d_mem_g = _bwd_kv(dkv, mn, mems, row(mem_norm_g), wkv3)
    ex_xa = _exchange_start([blocks(dwq), blocks(dwo), dwkv3], [True] * 3, "scatter_start_xa")
    (cat, dc, dzuv, dws, dbst, d_cln_g, d_cln_b, d_gln_g, d_gln_b, dbin_uv) = _bwd_out(
        dh1, c, z, row(conv_ln_g) + tok(ex_xa), row(conv_ln_b), row(gm_ln_g), row(gm_ln_b), gm_w_s, wst, bst,
        w_out_f, ts)
    dwout = _tn_matmul(cat, dh1b, 1024, "dw_out")
    early = dict(b_in_uv=dbin_uv, conv_ln_g=d_cln_g, conv_ln_b=d_cln_b, gm_ln_g=d_gln_g, gm_ln_b=d_gln_b, gm_w_s=dws,
                 gm_b_s=dbst.T, norm_xa_g=d_xa_g, mem_norm_g=d_mem_g, norm_ffn_g=d_ffn_g, final_norm_g=d_final_g,
                 loss=loss_p)
    ex_out = _exchange_start([blocks(dwout), _pack(list(early.values()))], [True, False], "scatter_start_out")
    da, dconv_w, dconv_b = _conv_bwd(dc, a, conv_w_f + tok(ex_out))
    dx, dz, hn0, dbin_ag, d_mix_g = _bwd_in(xs, dh1, da, z, dzuv, row(norm_mix_g), w_in3, ts)
    late = dict(norm_mix_g=d_mix_g, b_in_ag=dbin_ag, conv_w=dconv_w[:CONV_K], conv_b=dconv_b)
    ex_late = _exchange_start([_pack(list(late.values()))], [False], "gather_start_late")
    dw_in3 = _tn_matmul(hn0, dz, 0, "dw_in", col_blocks=True, behind=ex_late[4])
    ex_in = _exchange_start([dw_in3], [True], "scatter_start_in")

    def small_totals(ex, k, after, name):
        srcs, lands = _exchange_wait([ex[0][k]], [ex[1][k]], [ex[2][k]], [ex[3][k]], [False], after, name)
        return _sum8(_with_own(lands[0], srcs[0], me), "sum_" + name)

    early_tot = small_totals(ex_out, 1, ex_in[4], "small_early")
    late_tot = small_totals(ex_late, 0, early_tot, "small_late")
    early_g = dict(zip(early, _unpack(early_tot, [v.shape for v in early.values()])))
    late_g = dict(zip(late, _unpack(late_tot, [v.shape for v in late.values()])))
    loss = early_g["loss"][0, 0]
    grads = {}
    for nm in SMALL:
        if nm == "b_in":
            g = jnp.concatenate([late_g["b_in_ag"], early_g["b_in_uv"]], axis=1)
        elif nm == "conv_w":
            g = lax.dynamic_slice_in_dim(late_g[nm], me * conv_w.shape[1], conv_w.shape[1], axis=1)
        else:
            g = late_g[nm] if nm in late_g else early_g[nm]
        grads[nm] = g.reshape(w[nm].shape)

    delta, new_m, new_v = {}, {}, {}
    small_shapes = [w[nm].shape for nm in SMALL]
    sd, sm, sv = _adamw(_pack([w[nm] for nm in SMALL]), _pack([grads[nm] for nm in SMALL]),
                        _pack([mom[nm] for nm in SMALL]), _pack([var[nm] for nm in SMALL]), "adamw_small")
    for packed_out, dst in ((sd, delta), (sm, new_m), (sv, new_v)):
        for nm, val in zip(SMALL, _unpack(packed_out, small_shapes)):
            dst[nm] = val

    def reduced(srcs, lands, names):
        for nm, src, land in zip(names, srcs, lands):
            own = lax.dynamic_index_in_dim(src, me, axis=0, keepdims=False)
            g = _sum8(_with_own(land, own, me), "sum_" + nm)
            grads[nm] = g.T if nm == "ffn_w_gate_up" else g
            delta[nm], new_m[nm], new_v[nm] = _adamw(w[nm], grads[nm], mom[nm], var[nm], "adamw_" + nm)
        return new_v[names[-1]]

    after = sd
    for ex, names, tag in ((ex_ffn, ("ffn_w_gate_up", "ffn_w_down"), "ffn"), (ex_xa, ("xa_wq", "xa_wo", "xa_wkv"), "xa")):
        srcs, lands = _exchange_wait(ex[0], ex[1], ex[2], ex[3], [True] * len(names), after, "scatter_wait_" + tag)
        after = reduced(srcs, lands, names)
    srcs, lands = _exchange_wait([ex_out[0][0]], [ex_out[1][0]], [ex_out[2][0]], [ex_out[3][0]], [True], after,
                                 "scatter_wait_out")
    after = reduced(srcs, lands, ("w_out",))
    srcs, lands = _exchange_wait(ex_in[0], ex_in[1], ex_in[2], ex_in[3], [True], after, "scatter_wait_in")
    reduced(srcs, lands, ("w_in",))

    return (loss, dx.reshape(x.shape), *[grads[nm] for nm in WEIGHTS], *[delta[nm] for nm in WEIGHTS],
            *[new_m[nm] for nm in WEIGHTS], *[new_v[nm] for nm in WEIGHTS])
```

```python
import functools

import jax
import jax.numpy as jnp
from jax import lax
from jax.experimental import pallas as pl
from jax.experimental.pallas import tpu as pltpu

F32 = jnp.float32
BF16 = jnp.bfloat16
SDS = jax.ShapeDtypeStruct

N_DEV = 8
RMS_EPS = 1e-6
LN_EPS = 1e-5
CONV_K = 31
CONV_PAD = 32
CHUNK = 128
GM_HEADS = 8
XA_HEADS = 4
XA_DH = 256
GELU_K0 = 0.7978845608028654
GELU_K1 = 0.044715
ADAM_LR = 0.001
ADAM_B1 = 0.9
ADAM_B2 = 0.999
ADAM_EPS = 1e-08
ADAM_WD = 0.01
ADAM_STEP = 10
VMEM_LIMIT = 60 * 1024 * 1024

NN = (((1,), (0,)), ((), ()))
NT = (((1,), (1,)), ((), ()))
TN = (((0,), (0,)), ((), ()))


def _dot(a, b, dims=NN):
    return lax.dot_general(a, b, dims, preferred_element_type=F32)


def _bf(x):
    return x.astype(BF16)


def _cparams(*sem):
    return pltpu.CompilerParams(dimension_semantics=tuple(sem) if sem else None, vmem_limit_bytes=VMEM_LIMIT)


def _row(ts, w, col=0):
    return pl.BlockSpec((ts, w), lambda i: (i, col))


def _const(shape):
    nd = len(shape)
    return pl.BlockSpec(shape, lambda i: (0,) * nd, pipeline_mode=pl.Buffered(1))


def _acc(shape):
    nd = len(shape)
    return pl.BlockSpec(shape, lambda i: (0,) * nd)


def _rms_fwd(x, g):
    r = lax.rsqrt(jnp.mean(x * x, axis=-1, keepdims=True) + RMS_EPS)
    xh = x * r
    return xh * g, xh, r


def _rms_bwd(dy, xh, r, g):
    gdy = dy * g
    dx = r * (gdy - xh * jnp.mean(gdy * xh, axis=-1, keepdims=True))
    dg = jnp.sum(dy * xh, axis=0, keepdims=True)
    return dx, dg


def _ln_fwd(x, g, b):
    mu = jnp.mean(x, axis=-1, keepdims=True)
    xc = x - mu
    rs = lax.rsqrt(jnp.mean(xc * xc, axis=-1, keepdims=True) + LN_EPS)
    xh = xc * rs
    return xh * g + b, xh, rs


def _ln_bwd(dy, xh, rs, g):
    dxh = dy * g
    dx = rs * (dxh - jnp.mean(dxh, axis=-1, keepdims=True) - xh * jnp.mean(dxh * xh, axis=-1, keepdims=True))
    return dx, jnp.sum(dy * xh, axis=0, keepdims=True), jnp.sum(dy, axis=0, keepdims=True)


def _gelu(x):
    t = jnp.tanh(GELU_K0 * (x + GELU_K1 * (x * x * x)))
    return 0.5 * x * (1.0 + t), t


def _gelu_grad(x, t):
    return 0.5 * (1.0 + t) + 0.5 * x * (1.0 - t * t) * (GELU_K0 * (1.0 + 3.0 * GELU_K1 * x * x))


def _silu_grad(x, sg):
    return sg * (1.0 + x * (1.0 - sg))


def _by_residue(offsets):
    groups = {}
    for off in offsets:
        groups.setdefault(off % 8, []).append(off)
    return [(res, sorted(offs)) for res, offs in sorted(groups.items())]


def _accumulate(ref, val):
    @pl.when(pl.program_id(0) == 0)
    def _():
        ref[...] = jnp.zeros_like(ref)
    ref[...] += val


def _mix_masks():
    row = lax.broadcasted_iota(jnp.int32, (CHUNK, CHUNK), 0)
    col = lax.broadcasted_iota(jnp.int32, (CHUNK, CHUNK), 1)
    return row >= col, row <= col, col < (CHUNK // 2)


def _mix_fwd(vb, ws_ref, bst_ref, mixed_scr, ts):
    tril, _, lo = _mix_masks()
    for j in range(GM_HEADS // 2):
        w0 = _bf(jnp.where(tril, ws_ref[2 * j], 0.0))
        w1 = _bf(jnp.where(tril, ws_ref[2 * j + 1], 0.0))
        bias = jnp.where(lo, bst_ref[:, 2 * j:2 * j + 1], bst_ref[:, 2 * j + 1:2 * j + 2])
        for n in range(ts // CHUNK):
            v = vb[n * CHUNK:(n + 1) * CHUNK, j * 128:(j + 1) * 128]
            mixed_scr[n * CHUNK:(n + 1) * CHUNK, j * 128:(j + 1) * 128] = jnp.where(lo, _dot(w0, v), _dot(w1, v)) + bias


def _exchange(srcs, scatter, name):
    n = len(srcs)

    def body(*refs):
        src_refs, out_refs = refs[:n], refs[n:2 * n]
        send_sems, recv_sems, local_sems = refs[2 * n:]
        x, y, c = lax.axis_index("x"), lax.axis_index("y"), lax.axis_index("c")
        me = 4 * x + 2 * y + c

        def peer_of(mask):
            px = x if not (mask >> 2) & 1 else 1 - x
            py = y if not (mask >> 1) & 1 else 1 - y
            pc = c if not mask & 1 else 1 - c
            return (px, py, pc), 4 * px + 2 * py + pc

        def remote(k, mask):
            peer, pidx = peer_of(mask)
            return pltpu.make_async_remote_copy(
                src_ref=src_refs[k].at[pidx] if scatter else src_refs[k],
                dst_ref=out_refs[k].at[me],
                send_sem=send_sems.at[k, mask - 1], recv_sem=recv_sems.at[k, mask - 1],
                device_id=peer, device_id_type=pl.DeviceIdType.MESH)

        def arrival(k, mask):
            peer, pidx = peer_of(mask)
            return pltpu.make_async_remote_copy(
                src_ref=src_refs[k].at[pidx] if scatter else src_refs[k],
                dst_ref=out_refs[k].at[pidx],
                send_sem=send_sems.at[k, mask - 1], recv_sem=recv_sems.at[k, mask - 1],
                device_id=peer, device_id_type=pl.DeviceIdType.MESH)

        sends, locals_ = [], []
        for k in range(n):
            for mask in range(1, N_DEV):
                cp = remote(k, mask)
                cp.start()
                sends.append(cp)
            lc = pltpu.make_async_copy(src_refs[k].at[me] if scatter else src_refs[k], out_refs[k].at[me],
                                       local_sems.at[k])
            lc.start()
            locals_.append(lc)
        for k in range(n):
            for mask in range(1, N_DEV):
                arrival(k, mask).wait_recv()
        for cp in sends:
            cp.wait_send()
        for lc in locals_:
            lc.wait()

    outs = [SDS((N_DEV,) + tuple(s.shape[1:] if scatter else s.shape), s.dtype) for s in srcs]
    hbm = pl.BlockSpec(memory_space=pl.ANY)
    return pl.pallas_call(
        body, name=name, out_shape=outs, in_specs=[hbm] * n, out_specs=[hbm] * n,
        scratch_shapes=[pltpu.SemaphoreType.DMA((n, N_DEV - 1)), pltpu.SemaphoreType.DMA((n, N_DEV - 1)),
                        pltpu.SemaphoreType.DMA((n,))],
    )(*srcs)


def _peer_of(mask):
    x, y, c = lax.axis_index("x"), lax.axis_index("y"), lax.axis_index("c")
    px = 1 - x if (mask >> 2) & 1 else x
    py = 1 - y if (mask >> 1) & 1 else y
    pc = 1 - c if mask & 1 else c
    return (px, py, pc), 4 * px + 2 * py + pc


ALL_PEERS = tuple(range(1, N_DEV))
OTHER_CHIPS = (2, 4, 6)
SIBLING = 1


def _split_copy(src_ref, land_ref, send_sem, recv_sem, mask, slot, scatter, outgoing):
    x, y, c = lax.axis_index("x"), lax.axis_index("y"), lax.axis_index("c")
    me = 4 * x + 2 * y + c
    peer, pidx = _peer_of(mask)
    return pltpu.make_async_remote_copy(
        src_ref=src_ref.at[pidx] if scatter else src_ref,
        dst_ref=land_ref.at[me if outgoing else pidx],
        send_sem=send_sem.at[slot], recv_sem=recv_sem.at[slot],
        device_id=peer, device_id_type=pl.DeviceIdType.MESH)


_HBM = pl.BlockSpec(memory_space=pltpu.HBM)
_SEM = pl.BlockSpec(memory_space=pltpu.SEMAPHORE)
_EFFECT = pltpu.SideEffectType.DATAFLOW_SIDE_EFFECTING


def _exchange_start(srcs, scatter, name, masks=ALL_PEERS, own_slot=None):
    n = len(srcs)
    lands = [lax.empty((N_DEV,) + tuple(s.shape[1:] if sc else s.shape), s.dtype) for s, sc in zip(srcs, scatter)]
    if own_slot is not None:
        lands = [land if sc else _with_own(land, s, own_slot) for land, s, sc in zip(lands, srcs, scatter)]
    lands = [pltpu.with_memory_space_constraint(land, pltpu.HBM) for land in lands]
    srcs = [pltpu.with_memory_space_constraint(s, pltpu.HBM) for s in srcs]

    def body(*refs):
        src_refs, land_refs = refs[:n], refs[n:2 * n]
        send_sems, recv_sems = refs[2 * n:3 * n], refs[3 * n:4 * n]
        token = refs[-1]
        for k in range(n):
            for slot, mask in enumerate(masks):
                _split_copy(src_refs[k], land_refs[k], send_sems[k], recv_sems[k], mask, slot, scatter[k], True).start()
        token[...] = jnp.zeros_like(token)

    sem = pltpu.SemaphoreType.DMA((len(masks),))
    out = pl.pallas_call(
        body, name=name,
        out_shape=tuple([sem] * (2 * n) + [pltpu.HBM(s.shape, s.dtype) for s in srcs]
                        + [pltpu.HBM(l.shape, l.dtype) for l in lands] + [SDS((8, 128), F32)]),
        in_specs=[_HBM] * (2 * n),
        out_specs=tuple([_SEM] * (2 * n) + [_HBM] * (2 * n) + [pl.BlockSpec(memory_space=pltpu.VMEM)]),
        input_output_aliases={i: 2 * n + i for i in range(2 * n)},
        compiler_params=pltpu.CompilerParams(has_side_effects=_EFFECT),
    )(*srcs, *lands)
    return out[:n], out[n:2 * n], out[2 * n:3 * n], out[3 * n:4 * n], out[-1]


def _exchange_wait(send_sems, recv_sems, srcs_thru, lands_thru, scatter, after, name, masks=ALL_PEERS):
    n = len(srcs_thru)

    def body(*refs):
        src_refs, land_refs = refs[:n], refs[n:2 * n]
        send_refs, recv_refs = refs[2 * n:3 * n], refs[3 * n:4 * n]
        for k in range(n):
            for slot, mask in enumerate(masks):
                args = (src_refs[k], land_refs[k], send_refs[k], recv_refs[k], mask, slot, scatter[k])
                _split_copy(*args, True).wait_send()
                _split_copy(*args, False).wait_recv()

    out = pl.pallas_call(
        body, name=name,
        out_shape=tuple([pltpu.HBM(s.shape, s.dtype) for s in srcs_thru]
                        + [pltpu.HBM(l.shape, l.dtype) for l in lands_thru]),
        in_specs=[_HBM] * (2 * n) + [_SEM] * (2 * n) + [pl.BlockSpec(memory_space=pl.ANY)],
        out_specs=tuple([_HBM] * (2 * n)),
        input_output_aliases={i: i for i in range(2 * n)},
        compiler_params=pltpu.CompilerParams(has_side_effects=_EFFECT),
    )(*srcs_thru, *lands_thru, *send_sems, *recv_sems, after)
    return out[:n], out[n:]


def _wait_and_forward(send_sems, recv_sems, srcs_thru, lands_thru, after, name, masks):
    n = len(lands_thru)

    def body(*refs):
        src_refs, land_refs = refs[:n], refs[n:2 * n]
        send_refs, recv_refs = refs[2 * n:3 * n], refs[3 * n:4 * n]
        outs = refs[4 * n + 1:]
        fsend, frecv, token = outs[2 * n:3 * n], outs[3 * n:4 * n], outs[-1]
        for k in range(n):
            for slot, mask in enumerate(masks):
                args = (src_refs[k], land_refs[k], send_refs[k], recv_refs[k], mask, slot, False)
                _split_copy(*args, True).wait_send()
                _split_copy(*args, False).wait_recv()
        sibling, _ = _peer_of(SIBLING)
        for k in range(n):
            for slot, mask in enumerate(OTHER_CHIPS):
                _, mine = _peer_of(mask)
                pltpu.make_async_remote_copy(
                    src_ref=land_refs[k].at[mine], dst_ref=land_refs[k].at[mine], send_sem=fsend[k].at[slot],
                    recv_sem=frecv[k].at[slot], device_id=sibling, device_id_type=pl.DeviceIdType.MESH).start()
        token[...] = jnp.zeros_like(token)

    sem = pltpu.SemaphoreType.DMA((len(OTHER_CHIPS),))
    out = pl.pallas_call(
        body, name=name,
        out_shape=tuple([pltpu.HBM(s.shape, s.dtype) for s in srcs_thru] + [pltpu.HBM(l.shape, l.dtype) for l in lands_thru]
                        + [sem] * (2 * n) + [SDS((8, 128), F32)]),
        in_specs=[_HBM] * (2 * n) + [_SEM] * (2 * n) + [pl.BlockSpec(memory_space=pl.ANY)],
        out_specs=tuple([_HBM] * (2 * n) + [_SEM] * (2 * n) + [pl.BlockSpec(memory_space=pltpu.VMEM)]),
        input_output_aliases={i: i for i in range(2 * n)},
        compiler_params=pltpu.CompilerParams(has_side_effects=_EFFECT),
    )(*srcs_thru, *lands_thru, *send_sems, *recv_sems, after)
    return out[2 * n:3 * n], out[3 * n:4 * n], out[n:2 * n], out[-1]


def _forward_wait(send_sems, recv_sems, lands, after, name):
    n = len(lands)

    def body(*refs):
        land_refs, send_refs, recv_refs = refs[:n], refs[n:2 * n], refs[2 * n:3 * n]
        sibling, _ = _peer_of(SIBLING)
        for k in range(n):
            for slot, mask in enumerate(OTHER_CHIPS):
                _, mine = _peer_of(mask)
                _, theirs = _peer_of(mask | SIBLING)
                for block, wait in ((mine, "wait_send"), (theirs, "wait_recv")):
                    getattr(pltpu.make_async_remote_copy(
                        src_ref=land_refs[k].at[block], dst_ref=land_refs[k].at[block],
                        send_sem=send_refs[k].at[slot], recv_sem=recv_refs[k].at[slot], device_id=sibling,
                        device_id_type=pl.DeviceIdType.MESH), wait)()

    return pl.pallas_call(
        body, name=name, out_shape=tuple(pltpu.HBM(l.shape, l.dtype) for l in lands),
        in_specs=[_HBM] * n + [_SEM] * (2 * n) + [pl.BlockSpec(memory_space=pl.ANY)],
        out_specs=tuple([_HBM] * n), input_output_aliases={i: i for i in range(n)},
        compiler_params=pltpu.CompilerParams(has_side_effects=_EFFECT),
    )(*lands, *send_sems, *recv_sems, after)


def _with_own(landed, own, me):
    return lax.dynamic_update_slice_in_dim(landed, own[None], me, axis=0)


def _kv_proj(mem, g_mem, wkv3):
    m = mem.shape[0]

    def body(mem_ref, g_ref, w_ref, kv_ref, mn_ref):
        y, _, _ = _rms_fwd(mem_ref[...], g_ref[...])
        yb = _bf(y)
        mn_ref[...] = yb
        for b in range(N_DEV):
            kv_ref[:, 256 * b:256 * (b + 1)] = _dot(yb, w_ref[b])

    return pl.pallas_call(body, name="kv_proj", out_shape=(SDS((m, 2048), F32), SDS(mem.shape, BF16)),
                          compiler_params=_cparams())(mem, g_mem, wkv3)


def _fwd_in(x, g_mix, w_in3, b_in, ts):
    s, d = x.shape

    def body(x_ref, g_ref, w_ref, b_ref, z_ref, a_ref):
        hn, _, _ = _rms_fwd(x_ref[...], g_ref[...])
        hb = _bf(hn)
        for b in range(N_DEV):
            z_ref[:, 256 * b:256 * (b + 1)] = _dot(hb, w_ref[b]) + b_ref[:, 256 * b:256 * (b + 1)]
        a_ref[...] = z_ref[:, 0:512] * jax.nn.sigmoid(z_ref[:, 512:1024])

    return pl.pallas_call(
        body, name="fwd_in", grid=(s // ts,),
        in_specs=[_row(ts, d), _const(g_mix.shape), _const(w_in3.shape), _const(b_in.shape)],
        out_specs=(_row(ts, 2048), _row(ts, 512)),
        out_shape=(SDS((s, 2048), F32), SDS((s, 512), F32)),
        compiler_params=_cparams("arbitrary"))(x, g_mix, w_in3, b_in)


def _conv_fwd(a, w, b):
    s, cw = a.shape
    rc = 256 if s % 256 == 0 else 128

    def body(a_ref, w_ref, b_ref, c_ref, pad):
        pad[0:CONV_PAD, :] = jnp.zeros((CONV_PAD, 128), F32)
        pad[CONV_PAD:, :] = a_ref[...]

        def chunk(i, carry):
            r0 = pl.multiple_of(i * rc, rc)
            acc = jnp.zeros((rc, 128), F32) + b_ref[...]
            for res, offs in _by_residue(range(CONV_PAD - CONV_K + 1, CONV_PAD + 1)):
                shifted = pad[pl.ds(r0 + res, rc + offs[-1] - res), :]
                for off in offs:
                    k = off - (CONV_PAD - CONV_K + 1)
                    acc = acc + w_ref[k:k + 1, :] * shifted[off - res:off - res + rc, :]
            c_ref[pl.ds(r0, rc), :] = acc
            return carry

        lax.fori_loop(0, s // rc, chunk, 0)

    blk = lambda r: pl.BlockSpec((r, 128), lambda j: (0, j))
    return pl.pallas_call(
        body, name="conv_fwd", grid=(cw // 128,),
        in_specs=[blk(s), blk(CONV_K), blk(1)], out_specs=blk(s), out_shape=SDS((s, cw), F32),
        scratch_shapes=[pltpu.VMEM((s + CONV_PAD, 128), F32)],
        compiler_params=_cparams("arbitrary"))(a, w, b)


def _fwd_out(x, c, z, cln_g, cln_b, gln_g, gln_b, ws, bst, w_out, ts):
    s, d = x.shape

    def body(x_ref, c_ref, zuv_ref, clg, clb, glg, glb, ws_ref, bst_ref, wo_ref, h1_ref, mixed_scr):
        cl, _, _ = _ln_fwd(c_ref[...], clg[...], clb[...])
        co = cl * jax.nn.sigmoid(cl)
        u, _ = _gelu(zuv_ref[:, 0:512])
        vg, _ = _gelu(zuv_ref[:, 512:1024])
        vln, _, _ = _ln_fwd(vg, glg[...], glb[...])
        _mix_fwd(_bf(vln), ws_ref, bst_ref, mixed_scr, ts)
        gm = u * mixed_scr[...]
        h1_ref[...] = x_ref[...] + _dot(_bf(co), wo_ref[0:512, :]) + _dot(_bf(gm), wo_ref[512:1024, :])

    return pl.pallas_call(
        body, name="fwd_out", grid=(s // ts,),
        in_specs=[_row(ts, d), _row(ts, 512), _row(ts, 1024, 1), _const(cln_g.shape), _const(cln_b.shape),
                  _const(gln_g.shape), _const(gln_b.shape), _const(ws.shape), _const(bst.shape), _const(w_out.shape)],
        out_specs=_row(ts, d), out_shape=SDS((s, d), F32),
        scratch_shapes=[pltpu.VMEM((ts, 512), F32)],
        compiler_params=_cparams("arbitrary"))(x, c, z, cln_g, cln_b, gln_g, gln_b, ws, bst, w_out)


def _softmax_rows(sc):
    m = jnp.max(sc, axis=-1, keepdims=True)
    e = jnp.exp(sc - m)
    return e / jnp.sum(e, axis=-1, keepdims=True)


def _fwd_xa(h1, g_xa, wq, kv, wo, ts):
    s, d = h1.shape
    scale = XA_DH ** -0.5

    def body(h_ref, g_ref, wq_ref, kv_ref, wo_ref, h2_ref, o_scr):
        hn, _, _ = _rms_fwd(h_ref[...], g_ref[...])
        q = _dot(_bf(hn), wq_ref[...])
        for h in range(XA_HEADS):
            qh = _bf(q[:, XA_DH * h:XA_DH * (h + 1)])
            kh = _bf(kv_ref[:, XA_DH * h:XA_DH * (h + 1)])
            vh = _bf(kv_ref[:, d + XA_DH * h:d + XA_DH * (h + 1)])
            p = _softmax_rows(_dot(qh, kh, NT) * scale)
            o_scr[:, XA_DH * h:XA_DH * (h + 1)] = _dot(_bf(p), vh)
        h2_ref[...] = h_ref[...] + _dot(_bf(o_scr[...]), wo_ref[...])

    return pl.pallas_call(
        body, name="fwd_xa", grid=(s // ts,),
        in_specs=[_row(ts, d), _const(g_xa.shape), _const(wq.shape), _const(kv.shape), _const(wo.shape)],
        out_specs=_row(ts, d), out_shape=SDS((s, d), F32),
        scratch_shapes=[pltpu.VMEM((ts, d), F32)],
        compiler_params=_cparams("arbitrary"))(h1, g_xa, wq, kv, wo)


def _fwd_ffn(h2, g_ffn, wgut, wdown, g_final, target, ts):
    s, d = h2.shape
    hid = wdown.shape[0]
    hc = hid // 2

    def body(h_ref, g_ref, wgu_ref, wd_ref, gf_ref, t_ref, dh3_ref, dh3b_ref, loss_ref, dgf_ref):
        hn, _, _ = _rms_fwd(h_ref[...], g_ref[...])
        hb = _bf(hn)
        h3 = h_ref[...]
        for n in range(2):
            g = _dot(hb, wgu_ref[hc * n:hc * (n + 1), :], NT)
            u = _dot(hb, wgu_ref[hid + hc * n:hid + hc * (n + 1), :], NT)
            act = g * jax.nn.sigmoid(g) * u
            h3 = h3 + _dot(_bf(act), wd_ref[hc * n:hc * (n + 1), :])
        y, xh, r = _rms_fwd(h3, gf_ref[...])
        diff = y - t_ref[...]
        part = 0.5 * jnp.sum(jnp.mean(diff * diff, axis=-1, keepdims=True), axis=0, keepdims=True)
        _accumulate(loss_ref, jnp.zeros(loss_ref.shape, F32) + part)
        dh3, dgf = _rms_bwd(diff * (1.0 / d), xh, r, gf_ref[...])
        dh3_ref[...] = dh3
        dh3b_ref[...] = _bf(dh3)
        _accumulate(dgf_ref, dgf)

    return pl.pallas_call(
        body, name="fwd_ffn", grid=(s // ts,),
        in_specs=[_row(ts, d), _const(g_ffn.shape), _const(wgut.shape), _const(wdown.shape), _const(g_final.shape),
                  _row(ts, d)],
        out_specs=(_row(ts, d), _row(ts, d), _acc((1, 128)), _acc((1, d))),
        out_shape=(SDS((s, d), F32), SDS((s, d), BF16), SDS((1, 128), F32), SDS((1, d), F32)),
        compiler_params=_cparams("arbitrary"))(h2, g_ffn, wgut, wdown, g_final, target)


def _bwd_ffn(h2, dh3, g_ffn, wgut, wdown, ts):
    s, d = h2.shape
    hid = wdown.shape[0]
    hc = hid // 2

    def body(h_ref, dh3_ref, g_ref, wgu_ref, wd_ref, dh2_ref, dh2b_ref, act_ref, dgu_ref, hn_ref, dg_ref):
        hn, xh, r = _rms_fwd(h_ref[...], g_ref[...])
        hb = _bf(hn)
        hn_ref[...] = hb
        db = _bf(dh3_ref[...])
        dhn = jnp.zeros((ts, d), F32)
        for n in range(2):
            wg = wgu_ref[hc * n:hc * (n + 1), :]
            wu = wgu_ref[hid + hc * n:hid + hc * (n + 1), :]
            g = _dot(hb, wg, NT)
            u = _dot(hb, wu, NT)
            sg = jax.nn.sigmoid(g)
            sl = g * sg
            act_ref[:, hc * n:hc * (n + 1)] = _bf(sl * u)
            dact = _dot(db, wd_ref[hc * n:hc * (n + 1), :], NT)
            dgb = _bf(dact * u * _silu_grad(g, sg))
            dub = _bf(dact * sl)
            dgu_ref[:, hc * n:hc * (n + 1)] = dgb
            dgu_ref[:, hid + hc * n:hid + hc * (n + 1)] = dub
            dhn = dhn + _dot(dgb, wg) + _dot(dub, wu)
        dx, dg = _rms_bwd(dhn, xh, r, g_ref[...])
        dh2 = dh3_ref[...] + dx
        dh2_ref[...] = dh2
        dh2b_ref[...] = _bf(dh2)
        _accumulate(dg_ref, dg)

    return pl.pallas_call(
        body, name="bwd_ffn", grid=(s // ts,),
        in_specs=[_row(ts, d), _row(ts, d), _const(g_ffn.shape), _const(wgut.shape), _const(wdown.shape)],
        out_specs=(_row(ts, d), _row(ts, d), _row(ts, hid), _row(ts, 2 * hid), _row(ts, d), _acc((1, d))),
        out_shape=(SDS((s, d), F32), SDS((s, d), BF16), SDS((s, hid), BF16), SDS((s, 2 * hid), BF16),
                   SDS((s, d), BF16), SDS((1, d), F32)),
        compiler_params=_cparams("arbitrary"))(h2, dh3, g_ffn, wgut, wdown)


def _bwd_xa(h1, dh2, g_xa, wq, wo, kv, ts):
    s, d = h1.shape
    scale = XA_DH ** -0.5

    def body(h_ref, dh2_ref, g_ref, wq_ref, wo_ref, kv_ref, dh1_ref, dh1b_ref, dq_ref, o_ref, hn_ref, dkv_ref,
             dg_ref, dq_scr):
        hn, xh, r = _rms_fwd(h_ref[...], g_ref[...])
        hb = _bf(hn)
        hn_ref[...] = hb
        q = _dot(hb, wq_ref[...])
        do = _dot(_bf(dh2_ref[...]), wo_ref[...], NT)

        @pl.when(pl.program_id(0) == 0)
        def _():
            dkv_ref[...] = jnp.zeros_like(dkv_ref)

        for h in range(XA_HEADS):
            lo, hi = XA_DH * h, XA_DH * (h + 1)
            qh = _bf(q[:, lo:hi])
            kh = _bf(kv_ref[:, lo:hi])
            vh = _bf(kv_ref[:, d + lo:d + hi])
            p = _softmax_rows(_dot(qh, kh, NT) * scale)
            pb = _bf(p)
            o_ref[:, lo:hi] = _bf(_dot(pb, vh))
            doh = _bf(do[:, lo:hi])
            dp = _dot(doh, vh, NT)
            ds = p * (dp - jnp.sum(p * dp, axis=-1, keepdims=True)) * scale
            dsb = _bf(ds)
            dq_scr[:, lo:hi] = _dot(dsb, kh)
            dkv_ref[:, lo:hi] += _dot(dsb, qh, TN)
            dkv_ref[:, d + lo:d + hi] += _dot(pb, doh, TN)
        dqb = _bf(dq_scr[...])
        dq_ref[...] = dqb
        dx, dg = _rms_bwd(_dot(dqb, wq_ref[...], NT), xh, r, g_ref[...])
        dh1 = dh2_ref[...] + dx
        dh1_ref[...] = dh1
        dh1b_ref[...] = _bf(dh1)
        _accumulate(dg_ref, dg)

    return pl.pallas_call(
        body, name="bwd_xa", grid=(s // ts,),
        in_specs=[_row(ts, d), _row(ts, d), _const(g_xa.shape), _const(wq.shape), _const(wo.shape), _const(kv.shape)],
        out_specs=(_row(ts, d), _row(ts, d), _row(ts, d), _row(ts, d), _row(ts, d), _acc(kv.shape), _acc((1, d))),
        out_shape=(SDS((s, d), F32), SDS((s, d), BF16), SDS((s, d), BF16), SDS((s, d), BF16), SDS((s, d), BF16),
                   SDS(kv.shape, F32), SDS((1, d), F32)),
        scratch_shapes=[pltpu.VMEM((ts, d), F32)],
        compiler_params=_cparams("arbitrary"))(h1, dh2, g_xa, wq, wo, kv)


def _bwd_kv(dkv, mn, mem, g_mem, wkv3):
    d = mem.shape[1]

    def body(dkv_ref, mn_ref, mem_ref, g_ref, w_ref, dw_ref, dg_ref):
        dkvb = _bf(dkv_ref[...])
        dmn = jnp.zeros(mem_ref.shape, F32)
        for b in range(N_DEV):
            blk = dkvb[:, 256 * b:256 * (b + 1)]
            dmn = dmn + _dot(blk, w_ref[b], NT)
            dw_ref[b] = _bf(_dot(mn_ref[...], blk, TN))
        _, xh, r = _rms_fwd(mem_ref[...], g_ref[...])
        _, dg = _rms_bwd(dmn, xh, r, g_ref[...])
        dg_ref[...] = dg

    return pl.pallas_call(body, name="bwd_kv", out_shape=(SDS(wkv3.shape, BF16), SDS((1, d), F32)),
                          compiler_params=_cparams())(dkv, mn, mem, g_mem, wkv3)


def _bwd_out(dh1, c, z, cln_g, cln_b, gln_g, gln_b, ws, wst, bst, w_out, ts):
    s, d = dh1.shape
    nh = GM_HEADS

    def body(dh1_ref, c_ref, zuv_ref, clg, clb, glg, glb, ws_ref, wst_ref, bst_ref, wo_ref,
             cat_ref, dc_ref, dzuv_ref, dws_ref, dbst_ref, dclg_ref, dclb_ref, dglg_ref, dglb_ref, dbin_ref,
             mixed_scr, dv_scr):
        cl, chat, crs = _ln_fwd(c_ref[...], clg[...], clb[...])
        sg = jax.nn.sigmoid(cl)
        zu = zuv_ref[:, 0:512]
        zv = zuv_ref[:, 512:1024]
        u, tu = _gelu(zu)
        vg, tv = _gelu(zv)
        vln, vhat, vrs = _ln_fwd(vg, glg[...], glb[...])
        vb = _bf(vln)
        _mix_fwd(vb, ws_ref, bst_ref, mixed_scr, ts)
        mixed = mixed_scr[...]
        cat_ref[:, 0:512] = _bf(cl * sg)
        cat_ref[:, 512:1024] = _bf(u * mixed)
        dcat = _dot(_bf(dh1_ref[...]), wo_ref[...], NT)
        dgm = dcat[:, 512:1024]
        dc, dclg, dclb = _ln_bwd(dcat[:, 0:512] * _silu_grad(cl, sg), chat, crs, clg[...])
        dc_ref[...] = dc
        dzu = dgm * mixed * _gelu_grad(zu, tu)
        dm = dgm * u

        @pl.when(pl.program_id(0) == 0)
        def _():
            dws_ref[...] = jnp.zeros_like(dws_ref)
            dbst_ref[...] = jnp.zeros_like(dbst_ref)

        tril, triu, lo = _mix_masks()
        head = lax.broadcasted_iota(jnp.int32, (1, nh), 1)
        for j in range(nh // 2):
            w0t = _bf(jnp.where(triu, wst_ref[2 * j], 0.0))
            w1t = _bf(jnp.where(triu, wst_ref[2 * j + 1], 0.0))
            for n in range(ts // CHUNK):
                rows = slice(n * CHUNK, (n + 1) * CHUNK)
                lanes = slice(j * 128, (j + 1) * 128)
                dmc = dm[rows, lanes]
                dmb = _bf(dmc)
                dv_scr[rows, lanes] = jnp.where(lo, _dot(w0t, dmb), _dot(w1t, dmb))
                vc = vb[rows, lanes]
                d0 = jnp.where(lo, dmc, 0.0)
                d1 = dmc - d0
                dws_ref[2 * j] += jnp.where(tril, _dot(_bf(d0), vc, NT), 0.0)
                dws_ref[2 * j + 1] += jnp.where(tril, _dot(_bf(d1), vc, NT), 0.0)
                dbst_ref[...] += (jnp.sum(d0, axis=1, keepdims=True) * (head == 2 * j).astype(F32)
                                  + jnp.sum(d1, axis=1, keepdims=True) * (head == 2 * j + 1).astype(F32))
        dvg, dglg, dglb = _ln_bwd(dv_scr[...], vhat, vrs, glg[...])
        dzv = dvg * _gelu_grad(zv, tv)
        dzuv_ref[:, 0:512] = _bf(dzu)
        dzuv_ref[:, 512:1024] = _bf(dzv)
        _accumulate(dclg_ref, dclg)
        _accumulate(dclb_ref, dclb)
        _accumulate(dglg_ref, dglg)
        _accumulate(dglb_ref, dglb)
        _accumulate(dbin_ref, jnp.concatenate([jnp.sum(dzu, axis=0, keepdims=True),
                                               jnp.sum(dzv, axis=0, keepdims=True)], axis=1))

    vec = (1, 512)
    return pl.pallas_call(
        body, name="bwd_out", grid=(s // ts,),
        in_specs=[_row(ts, d), _row(ts, 512), _row(ts, 1024, 1), _const(cln_g.shape), _const(cln_b.shape),
                  _const(gln_g.shape), _const(gln_b.shape), _const(ws.shape), _const(wst.shape), _const(bst.shape),
                  _const(w_out.shape)],
        out_specs=(_row(ts, d), _row(ts, 512), _row(ts, 1024), _acc(ws.shape), _acc(bst.shape), _acc(vec), _acc(vec),
                   _acc(vec), _acc(vec), _acc((1, 1024))),
        out_shape=(SDS((s, d), BF16), SDS((s, 512), F32), SDS((s, 1024), BF16), SDS(ws.shape, F32),
                   SDS(bst.shape, F32), SDS(vec, F32), SDS(vec, F32), SDS(vec, F32), SDS(vec, F32), SDS((1, 1024), F32)),
        scratch_shapes=[pltpu.VMEM((ts, 512), F32), pltpu.VMEM((ts, 512), F32)],
        compiler_params=_cparams("arbitrary"))(dh1, c, z, cln_g, cln_b, gln_g, gln_b, ws, wst, bst, w_out)


def _conv_bwd(dc, a, w):
    s, cw = a.shape
    rc = 256 if s % 256 == 0 else 128

    def body(dc_ref, a_ref, w_ref, da_ref, dw_ref, db_ref, pad_a, pad_d, part):
        pad_a[0:CONV_PAD, :] = jnp.zeros((CONV_PAD, 128), F32)
        pad_a[CONV_PAD:, :] = a_ref[...]
        pad_d[0:s, :] = dc_ref[...]
        pad_d[s:, :] = jnp.zeros((CONV_PAD, 128), F32)
        part[...] = jnp.zeros_like(part)

        def rows8(v):
            return jnp.sum(v.reshape(rc // 8, 8, 128), axis=0)

        def chunk(i, carry):
            r0 = pl.multiple_of(i * rc, rc)
            dcc = pad_d[pl.ds(r0, rc), :]
            acc = jnp.zeros((rc, 128), F32)
            for res, offs in _by_residue(range(0, CONV_K)):
                shifted = pad_d[pl.ds(r0 + res, rc + offs[-1] - res), :]
                for off in offs:
                    k = CONV_K - 1 - off
                    acc = acc + w_ref[k:k + 1, :] * shifted[off - res:off - res + rc, :]
            da_ref[pl.ds(r0, rc), :] = acc
            for res, offs in _by_residue(range(CONV_PAD - CONV_K + 1, CONV_PAD + 1)):
                shifted = pad_a[pl.ds(r0 + res, rc + offs[-1] - res), :]
                for off in offs:
                    k = off - (CONV_PAD - CONV_K + 1)
                    part[8 * k:8 * k + 8, :] += rows8(dcc * shifted[off - res:off - res + rc, :])
            part[8 * CONV_PAD:, :] += rows8(dcc)
            return carry

        lax.fori_loop(0, s // rc, chunk, 0)
        sums = jnp.sum(part[...].reshape(CONV_PAD + 1, 8, 128), axis=1)
        dw_ref[...] = sums[0:CONV_PAD, :]
        db_ref[...] = sums[CONV_PAD:, :]

    blk = lambda r: pl.BlockSpec((r, 128), lambda j: (0, j))
    return pl.pallas_call(
        body, name="conv_bwd", grid=(cw // 128,),
        in_specs=[blk(s), blk(s), blk(CONV_K)], out_specs=(blk(s), blk(CONV_PAD), blk(1)),
        out_shape=(SDS((s, cw), F32), SDS((CONV_PAD, cw), F32), SDS((1, cw), F32)),
        scratch_shapes=[pltpu.VMEM((s + CONV_PAD, 128), F32), pltpu.VMEM((s + CONV_PAD, 128), F32),
                        pltpu.VMEM((8 * (CONV_PAD + 1), 128), F32)],
        compiler_params=_cparams("arbitrary"))(dc, a, w)


def _bwd_in(x, dh1, da, z, dzuv, g_mix, w_in3, ts):
    s, d = x.shape

    def body(x_ref, dh1_ref, da_ref, zag_ref, dzuv_ref, g_ref, w_ref, dx_ref, dz_ref, hn_ref, dbin_ref, dg_ref):
        za = zag_ref[:, 0:512]
        sg = jax.nn.sigmoid(zag_ref[:, 512:1024])
        da_ = da_ref[...]
        dza = da_ * sg
        dzg = da_ * za * sg * (1.0 - sg)
        dz_ref[:, 0:512] = _bf(dza)
        dz_ref[:, 512:1024] = _bf(dzg)
        dz_ref[:, 1024:2048] = dzuv_ref[...]
        dhn = jnp.zeros((ts, d), F32)
        for b in range(N_DEV):
            dhn = dhn + _dot(dz_ref[:, 256 * b:256 * (b + 1)], w_ref[b], NT)
        hn, xh, r = _rms_fwd(x_ref[...], g_ref[...])
        hn_ref[...] = _bf(hn)
        dxn, dg = _rms_bwd(dhn, xh, r, g_ref[...])
        dx_ref[...] = dh1_ref[...] + dxn
        _accumulate(dg_ref, dg)
        _accumulate(dbin_ref, jnp.concatenate([jnp.sum(dza, axis=0, keepdims=True),
                                               jnp.sum(dzg, axis=0, keepdims=True)], axis=1))

    return pl.pallas_call(
        body, name="bwd_in", grid=(s // ts,),
        in_specs=[_row(ts, d), _row(ts, d), _row(ts, 512), _row(ts, 1024, 0), _row(ts, 1024), _const(g_mix.shape),
                  _const(w_in3.shape)],
        out_specs=(_row(ts, d), _row(ts, 2048), _row(ts, d), _acc((1, 1024)), _acc((1, d))),
        out_shape=(SDS((s, d), F32), SDS((s, 2048), BF16), SDS((s, d), BF16), SDS((1, 1024), F32), SDS((1, d), F32)),
        compiler_params=_cparams("arbitrary"))(x, dh1, da, z, dzuv, g_mix, w_in3)


def _tn_matmul(a, b, tm, name, col_blocks=False, behind=None):
    s, m = a.shape
    n = b.shape[1]
    cb = 256
    tn = 2 * cb if col_blocks else n
    if col_blocks:
        tm = m
    ts = min(s, 512)
    n_s = s // ts

    def body(a_ref, b_ref, *rest):
        o_ref, acc = rest[-2:]
        k = pl.program_id(2)

        @pl.when(k == 0)
        def _():
            acc[...] = jnp.zeros_like(acc)

        acc[...] += _dot(_bf(a_ref[...]), _bf(b_ref[...]), TN)

        @pl.when(k == n_s - 1)
        def _():
            if col_blocks:
                for blk in range(tn // cb):
                    o_ref[blk] = _bf(acc[:, cb * blk:cb * (blk + 1)])
            else:
                o_ref[...] = _bf(acc[...])

    if col_blocks:
        out_shape = SDS((n // cb, m, cb), BF16)
        out_spec = pl.BlockSpec((tn // cb, m, cb), lambda i, j, k: (j, 0, 0))
    else:
        out_shape = SDS((m, n), BF16)
        out_spec = pl.BlockSpec((tm, tn), lambda i, j, k: (i, j))
    extra = [] if behind is None else [behind]
    return pl.pallas_call(
        body, name=name, grid=(m // tm, n // tn, n_s),
        in_specs=[pl.BlockSpec((ts, tm), lambda i, j, k: (k, i)), pl.BlockSpec((ts, tn), lambda i, j, k: (k, j))]
        + [pl.BlockSpec(memory_space=pl.ANY)] * len(extra),
        out_specs=out_spec, out_shape=out_shape,
        scratch_shapes=[pltpu.VMEM((tm, tn), F32)],
        compiler_params=_cparams("parallel", "parallel", "arbitrary"))(a, b, *extra)


def _sum8(parts, name):
    _, r, c = parts.shape
    tr = r
    for cand in (512, 256, 352, 128, 8):
        if r % cand == 0 and cand <= r:
            tr = cand
            break

    def body(p_ref, o_ref):
        acc = p_ref[0].astype(F32)
        for dev in range(1, N_DEV):
            acc = acc + p_ref[dev].astype(F32)
        o_ref[...] = acc

    return pl.pallas_call(
        body, name=name, grid=(r // tr,),
        in_specs=[pl.BlockSpec((N_DEV, tr, c), lambda i: (0, i, 0))],
        out_specs=pl.BlockSpec((tr, c), lambda i: (i, 0)), out_shape=SDS((r, c), F32),
        compiler_params=_cparams("parallel"))(parts)


def _adam_update(w, g, m, v):
    m2 = ADAM_B1 * m + (1.0 - ADAM_B1) * g
    v2 = ADAM_B2 * v + (1.0 - ADAM_B2) * (g * g)
    m_hat = m2 / (1.0 - ADAM_B1 ** ADAM_STEP)
    v_hat = v2 / (1.0 - ADAM_B2 ** ADAM_STEP)
    return -ADAM_LR * (m_hat / (jnp.sqrt(v_hat) + ADAM_EPS) + ADAM_WD * w), m2, v2


def _adamw(w, g, m, v, name):
    r, c = w.shape
    tr = r
    for cand in (256, 128, 176, 8):
        if r % cand == 0 and cand <= r:
            tr = cand
            break

    def body(w_ref, g_ref, m_ref, v_ref, d_ref, m2_ref, v2_ref):
        d_ref[...], m2_ref[...], v2_ref[...] = _adam_update(w_ref[...], g_ref[...], m_ref[...], v_ref[...])

    blk = pl.BlockSpec((tr, c), lambda i: (i, 0))
    return pl.pallas_call(
        body, name=name, grid=(r // tr,), in_specs=[blk] * 4, out_specs=(blk,) * 3,
        out_shape=(SDS((r, c), F32),) * 3, compiler_params=_cparams("parallel"))(w, g, m, v)


def _adamw_many(ws, gs, ms, vs, name):
    n = len(ws)

    def body(*refs):
        ins, outs = refs[:4 * n], refs[4 * n:]
        for i in range(n):
            w, g, m, v = (ins[j * n + i][...] for j in range(4))
            outs[i][...], outs[n + i][...], outs[2 * n + i][...] = _adam_update(w, g, m, v)

    shapes = [SDS(w.shape, F32) for w in ws]
    out = pl.pallas_call(body, name=name, out_shape=shapes * 3, compiler_params=_cparams())(*ws, *gs, *ms, *vs)
    return out[:n], out[n:2 * n], out[2 * n:]


def _pack(arrs):
    flat = jnp.concatenate([a.reshape(-1) for a in arrs])
    pad = (-flat.shape[0]) % (128 * 128)
    return jnp.pad(flat, (0, pad)).reshape(-1, 128)


def _unpack(packed, shapes):
    flat = packed.reshape(-1)
    out, off = [], 0
    for shp in shapes:
        size = 1
        for dim in shp:
            size *= dim
        out.append(flat[off:off + size].reshape(shp))
        off += size
    return out


SMALL = ("norm_mix_g", "b_in", "conv_w", "conv_b", "conv_ln_g", "conv_ln_b", "gm_ln_g", "gm_ln_b", "gm_w_s", "gm_b_s",
         "norm_xa_g", "mem_norm_g", "norm_ffn_g", "final_norm_g")
BIG = ("w_in", "w_out", "xa_wq", "xa_wkv", "xa_wo", "ffn_w_gate_up", "ffn_w_down")
WEIGHTS = ("norm_mix_g", "w_in", "b_in", "conv_w", "conv_b", "conv_ln_g", "conv_ln_b", "gm_ln_g", "gm_ln_b", "gm_w_s",
           "gm_b_s", "w_out", "norm_xa_g", "mem_norm_g", "xa_wq", "xa_wkv", "xa_wo", "norm_ffn_g", "ffn_w_gate_up",
           "ffn_w_down", "final_norm_g")


def kernel(x, mem, norm_mix_g, w_in, b_in, conv_w, conv_b, conv_ln_g, conv_ln_b, gm_ln_g, gm_ln_b, gm_w_s, gm_b_s, w_out, norm_xa_g, mem_norm_g, xa_wq, xa_wkv, xa_wo, norm_ffn_g, ffn_w_gate_up, ffn_w_down, final_norm_g, loss_target, m_norm_mix_g, m_w_in, m_b_in, m_conv_w, m_conv_b, m_conv_ln_g, m_conv_ln_b, m_gm_ln_g, m_gm_ln_b, m_gm_w_s, m_gm_b_s, m_w_out, m_norm_xa_g, m_mem_norm_g, m_xa_wq, m_xa_wkv, m_xa_wo, m_norm_ffn_g, m_ffn_w_gate_up, m_ffn_w_down, m_final_norm_g, v_norm_mix_g, v_w_in, v_b_in, v_conv_w, v_conv_b, v_conv_ln_g, v_conv_ln_b, v_gm_ln_g, v_gm_ln_b, v_gm_w_s, v_gm_b_s, v_w_out, v_norm_xa_g, v_mem_norm_g, v_xa_wq, v_xa_wkv, v_xa_wo, v_norm_ffn_g, v_ffn_w_gate_up, v_ffn_w_down, v_final_norm_g):
    w = dict(norm_mix_g=norm_mix_g, w_in=w_in, b_in=b_in, conv_w=conv_w, conv_b=conv_b, conv_ln_g=conv_ln_g,
             conv_ln_b=conv_ln_b, gm_ln_g=gm_ln_g, gm_ln_b=gm_ln_b, gm_w_s=gm_w_s, gm_b_s=gm_b_s, w_out=w_out,
             norm_xa_g=norm_xa_g, mem_norm_g=mem_norm_g, xa_wq=xa_wq, xa_wkv=xa_wkv, xa_wo=xa_wo,
             norm_ffn_g=norm_ffn_g, ffn_w_gate_up=ffn_w_gate_up, ffn_w_down=ffn_w_down, final_norm_g=final_norm_g)
    mom = dict(norm_mix_g=m_norm_mix_g, w_in=m_w_in, b_in=m_b_in, conv_w=m_conv_w, conv_b=m_conv_b,
               conv_ln_g=m_conv_ln_g, conv_ln_b=m_conv_ln_b, gm_ln_g=m_gm_ln_g, gm_ln_b=m_gm_ln_b, gm_w_s=m_gm_w_s,
               gm_b_s=m_gm_b_s, w_out=m_w_out, norm_xa_g=m_norm_xa_g, mem_norm_g=m_mem_norm_g, xa_wq=m_xa_wq,
               xa_wkv=m_xa_wkv, xa_wo=m_xa_wo, norm_ffn_g=m_norm_ffn_g, ffn_w_gate_up=m_ffn_w_gate_up,
               ffn_w_down=m_ffn_w_down, final_norm_g=m_final_norm_g)
    var = dict(norm_mix_g=v_norm_mix_g, w_in=v_w_in, b_in=v_b_in, conv_w=v_conv_w, conv_b=v_conv_b,
               conv_ln_g=v_conv_ln_g, conv_ln_b=v_conv_ln_b, gm_ln_g=v_gm_ln_g, gm_ln_b=v_gm_ln_b, gm_w_s=v_gm_w_s,
               gm_b_s=v_gm_b_s, w_out=v_w_out, norm_xa_g=v_norm_xa_g, mem_norm_g=v_mem_norm_g, xa_wq=v_xa_wq,
               xa_wkv=v_xa_wkv, xa_wo=v_xa_wo, norm_ffn_g=v_norm_ffn_g, ffn_w_gate_up=v_ffn_w_gate_up,
               ffn_w_down=v_ffn_w_down, final_norm_g=v_final_norm_g)

    me = 4 * lax.axis_index("x") + 2 * lax.axis_index("y") + lax.axis_index("c")
    s, d = x.shape[1], x.shape[2]
    xs = x.reshape(s, d)
    mems = mem.reshape(mem.shape[1], d)
    tgt = loss_target.reshape(s, d)
    ts = min(512, s)
    ts_ffn = min(256, s)
    row = lambda a: a.reshape(1, -1)

    conv_w_pad = jnp.pad(conv_w, ((0, CONV_PAD - CONV_K), (0, 128 - conv_w.shape[1])))
    shards = [_bf(w_in), conv_w_pad, _bf(xa_wkv), _bf(w_out), _bf(xa_wq), _bf(xa_wo), _bf(ffn_w_gate_up.T),
              _bf(ffn_w_down)]
    first_level = (SIBLING,) + OTHER_CHIPS
    g_send, g_recv, g_src, g_land, g_tok = _exchange_start(shards, [False] * len(shards), "gather_start", first_level,
                                                           own_slot=me)

    def arrived(idx, after, name):
        pick = lambda seq: [seq[i] for i in idx]
        return _wait_and_forward(pick(g_send), pick(g_recv), pick(g_src), pick(g_land), after, "gather_pass_" + name,
                                 first_level)

    def complete(handle, after, name):
        return _forward_wait(handle[0], handle[1], handle[2], after, "forward_wait_" + name)

    bst = gm_b_s.T
    wst = jnp.swapaxes(gm_w_s, 1, 2)

    h_in = arrived((0, 1), g_tok, "in")
    w_in3, conv_w8 = complete(h_in, h_in[3], "in")
    conv_w_f = conv_w8[:, :CONV_K, :conv_w.shape[1]].transpose(1, 0, 2).reshape(CONV_K, -1)
    cw = conv_w_f.shape[1]
    z, a = _fwd_in(xs, row(norm_mix_g), w_in3, row(b_in), ts)
    h_out = arrived((2, 3), z, "out")
    h_xa = arrived((4, 5), h_out[3], "xa")
    c = _conv_fwd(a, conv_w_f + h_xa[3][0:1, 0:1], row(conv_b))
    wkv3, w_out3 = complete(h_out, c, "out")
    w_out_f = w_out3.reshape(-1, d)
    kv, mn = _kv_proj(mems, row(mem_norm_g), wkv3)
    h1 = _fwd_out(xs, c, z, row(conv_ln_g), row(conv_ln_b), row(gm_ln_g), row(gm_ln_b), gm_w_s, bst, w_out_f, ts)
    h_gut = arrived((6,), h1, "gate_up")
    wq3, wo3 = complete(h_xa, h_gut[3], "xa")
    wq_f = wq3.reshape(-1, d)
    wo_f = wo3.reshape(-1, d)
    h2 = _fwd_xa(h1, row(norm_xa_g), wq_f, kv, wo_f, ts)
    h_down = arrived((7,), h2, "down")
    (wgut3,) = complete(h_gut, h_down[3], "gate_up")
    (wdown3,) = complete(h_down, wgut3, "down")
    wgut_f = wgut3.reshape(-1, d)
    wdown_f = wdown3.reshape(-1, d)
    dh3, dh3b, loss_p, d_final_g = _fwd_ffn(h2, row(norm_ffn_g), wgut_f, wdown_f, row(final_norm_g), tgt, ts_ffn)

    blocks = lambda m: m.reshape(N_DEV, -1, d)
    tok = lambda ex: ex[4][0:1, 0:1]
    dh2, dh2b, act, dgu, hn2, d_ffn_g = _bwd_ffn(h2, dh3, row(norm_ffn_g), wgut_f, wdown_f, ts_ffn)
    dwgut = _tn_matmul(dgu, hn2, 1408, "dw_gate_up")
    dwdown = _tn_matmul(act, dh3b, 1408, "dw_down")
    ex_ffn = _exchange_start([blocks(dwgut), blocks(dwdown)], [True, True], "scatter_start_ffn")
    dh1, dh1b, dq, o, hn1, dkv, d_xa_g = _bwd_xa(h1, dh2, row(norm_xa_g) + tok(ex_ffn), wq_f, wo_f, kv, ts)
    dwq = _tn_matmul(hn1, dq, 1024, "dw_q")
    dwo = _tn_matmul(o, dh2b, 1024, "dw_o")
    dwkv3, d_mem_g = _bwd_kv(dkv, mn, mems, row(mem_norm_g), wkv3)
    ex_xa = _exchange_start([blocks(dwq), blocks(dwo), dwkv3], [True] * 3, "scatter_start_xa")
    (cat, dc, dzuv, dws, dbst, d_cln_g, d_cln_b, d_gln_g, d_gln_b, dbin_uv) = _bwd_out(
        dh1, c, z, row(conv_ln_g) + tok(ex_xa), row(conv_ln_b), row(gm_ln_g), row(gm_ln_b), gm_w_s, wst, bst,
        w_out_f, ts)
    dwout = _tn_matmul(cat, dh1b, 1024, "dw_out")
    early = dict(b_in_uv=dbin_uv, conv_ln_g=d_cln_g, conv_ln_b=d_cln_b, gm_ln_g=d_gln_g, gm_ln_b=d_gln_b, gm_w_s=dws,
                 gm_b_s=dbst.T, norm_xa_g=d_xa_g, mem_norm_g=d_mem_g, norm_ffn_g=d_ffn_g, final_norm_g=d_final_g,
                 loss=loss_p)
    ex_out = _exchange_start([blocks(dwout), _pack(list(early.values()))], [True, False], "scatter_start_out")
    da, dconv_w, dconv_b = _conv_bwd(dc, a, conv_w_f + tok(ex_out))
    dx, dz, hn0, dbin_ag, d_mix_g = _bwd_in(xs, dh1, da, z, dzuv, row(norm_mix_g), w_in3, ts)
    late = dict(norm_mix_g=d_mix_g, b_in_ag=dbin_ag, conv_w=dconv_w[:CONV_K], conv_b=dconv_b)
    ex_late = _exchange_start([_pack(list(late.values()))], [False], "gather_start_late")
    dw_in3 = _tn_matmul(hn0, dz, 0, "dw_in", col_blocks=True, behind=ex_late[4])
    ex_in = _exchange_start([dw_in3], [True], "scatter_start_in")

    def small_totals(ex, k, after, name):
        srcs, lands = _exchange_wait([ex[0][k]], [ex[1][k]], [ex[2][k]], [ex[3][k]], [False], after, name)
        return _sum8(_with_own(lands[0], srcs[0], me), "sum_" + name)

    early_tot = small_totals(ex_out, 1, ex_in[4], "small_early")
    late_tot = small_totals(ex_late, 0, early_tot, "small_late")
    early_g = dict(zip(early, _unpack(early_tot, [v.shape for v in early.values()])))
    late_g = dict(zip(late, _unpack(late_tot, [v.shape for v in late.values()])))
    loss = early_g["loss"][0, 0]
    grads = {}
    for nm in SMALL:
        if nm == "b_in":
            g = jnp.concatenate([late_g["b_in_ag"], early_g["b_in_uv"]], axis=1)
        elif nm == "conv_w":
            g = lax.dynamic_slice_in_dim(late_g[nm], me * conv_w.shape[1], conv_w.shape[1], axis=1)
        else:
            g = late_g[nm] if nm in late_g else early_g[nm]
        grads[nm] = g.reshape(w[nm].shape)

    delta, new_m, new_v = {}, {}, {}
    small = [[src[nm] for nm in SMALL] for src in (w, grads, mom, var)]
    for dst, vals in zip((delta, new_m, new_v), _adamw_many(*small, "adamw_small")):
        dst.update(zip(SMALL, vals))

    def reduced(srcs, lands, names):
        for nm, src, land in zip(names, srcs, lands):
            own = lax.dynamic_index_in_dim(src, me, axis=0, keepdims=False)
            g = _sum8(_with_own(land, own, me), "sum_" + nm)
            view = (lambda t: t.T) if nm == "ffn_w_gate_up" else (lambda t: t)
            upd = _adamw(view(w[nm]), g, view(mom[nm]), view(var[nm]), "adamw_" + nm)
            grads[nm], delta[nm], new_m[nm], new_v[nm] = (view(t) for t in (g, *upd))
        return new_v[names[-1]]

    after = delta[SMALL[0]]
    for ex, names, tag in ((ex_ffn, ("ffn_w_gate_up", "ffn_w_down"), "ffn"), (ex_xa, ("xa_wq", "xa_wo", "xa_wkv"), "xa")):
        srcs, lands = _exchange_wait(ex[0], ex[1], ex[2], ex[3], [True] * len(names), after, "scatter_wait_" + tag)
        after = reduced(srcs, lands, names)
    srcs, lands = _exchange_wait([ex_out[0][0]], [ex_out[1][0]], [ex_out[2][0]], [ex_out[3][0]], [True], after,
                                 "scatter_wait_out")
    after = reduced(srcs, lands, ("w_out",))
    srcs, lands = _exchange_wait(ex_in[0], ex_in[1], ex_in[2], ex_in[3], [True], after, "scatter_wait_in")
    reduced(srcs, lands, ("w_in",))

    return (loss, dx.reshape(x.shape), *[grads[nm] for nm in WEIGHTS], *[delta[nm] for nm in WEIGHTS],
            *[new_m[nm] for nm in WEIGHTS], *[new_v[nm] for nm in WEIGHTS])
```

```python
import functools

import jax
import jax.numpy as jnp
from jax import lax
from jax.experimental import pallas as pl
from jax.experimental.pallas import tpu as pltpu

F32 = jnp.float32
BF16 = jnp.bfloat16
SDS = jax.ShapeDtypeStruct

N_DEV = 8
RMS_EPS = 1e-6
LN_EPS = 1e-5
CONV_K = 31
CONV_PAD = 32
CHUNK = 128
GM_HEADS = 8
XA_HEADS = 4
XA_DH = 256
GELU_K0 = 0.7978845608028654
GELU_K1 = 0.044715
ADAM_LR = 0.001
ADAM_B1 = 0.9
ADAM_B2 = 0.999
ADAM_EPS = 1e-08
ADAM_WD = 0.01
ADAM_STEP = 10
VMEM_LIMIT = 60 * 1024 * 1024

NN = (((1,), (0,)), ((), ()))
NT = (((1,), (1,)), ((), ()))
TN = (((0,), (0,)), ((), ()))


def _dot(a, b, dims=NN):
    return lax.dot_general(a, b, dims, preferred_element_type=F32)


def _bf(x):
    return x.astype(BF16)


def _cparams(*sem):
    return pltpu.CompilerParams(dimension_semantics=tuple(sem) if sem else None, vmem_limit_bytes=VMEM_LIMIT)


def _row(ts, w, col=0):
    return pl.BlockSpec((ts, w), lambda i: (i, col))


def _const(shape):
    nd = len(shape)
    return pl.BlockSpec(shape, lambda i: (0,) * nd, pipeline_mode=pl.Buffered(1))


def _acc(shape):
    nd = len(shape)
    return pl.BlockSpec(shape, lambda i: (0,) * nd)


def _rms_fwd(x, g):
    r = lax.rsqrt(jnp.mean(x * x, axis=-1, keepdims=True) + RMS_EPS)
    xh = x * r
    return xh * g, xh, r


def _rms_bwd(dy, xh, r, g):
    gdy = dy * g
    dx = r * (gdy - xh * jnp.mean(gdy * xh, axis=-1, keepdims=True))
    dg = jnp.sum(dy * xh, axis=0, keepdims=True)
    return dx, dg


def _ln_fwd(x, g, b):
    mu = jnp.mean(x, axis=-1, keepdims=True)
    xc = x - mu
    rs = lax.rsqrt(jnp.mean(xc * xc, axis=-1, keepdims=True) + LN_EPS)
    xh = xc * rs
    return xh * g + b, xh, rs


def _ln_bwd(dy, xh, rs, g):
    dxh = dy * g
    dx = rs * (dxh - jnp.mean(dxh, axis=-1, keepdims=True) - xh * jnp.mean(dxh * xh, axis=-1, keepdims=True))
    return dx, jnp.sum(dy * xh, axis=0, keepdims=True), jnp.sum(dy, axis=0, keepdims=True)


def _gelu(x):
    t = jnp.tanh(GELU_K0 * (x + GELU_K1 * (x * x * x)))
    return 0.5 * x * (1.0 + t), t


def _gelu_grad(x, t):
    return 0.5 * (1.0 + t) + 0.5 * x * (1.0 - t * t) * (GELU_K0 * (1.0 + 3.0 * GELU_K1 * x * x))


def _silu_grad(x, sg):
    return sg * (1.0 + x * (1.0 - sg))


def _by_residue(offsets):
    groups = {}
    for off in offsets:
        groups.setdefault(off % 8, []).append(off)
    return [(res, sorted(offs)) for res, offs in sorted(groups.items())]


def _accumulate(ref, val):
    @pl.when(pl.program_id(0) == 0)
    def _():
        ref[...] = jnp.zeros_like(ref)
    ref[...] += val


def _mix_masks():
    row = lax.broadcasted_iota(jnp.int32, (CHUNK, CHUNK), 0)
    col = lax.broadcasted_iota(jnp.int32, (CHUNK, CHUNK), 1)
    return row >= col, row <= col, col < (CHUNK // 2)


def _mix_fwd(vb, ws_ref, bst_ref, mixed_scr, ts):
    tril, _, lo = _mix_masks()
    for j in range(GM_HEADS // 2):
        w0 = _bf(jnp.where(tril, ws_ref[2 * j], 0.0))
        w1 = _bf(jnp.where(tril, ws_ref[2 * j + 1], 0.0))
        bias = jnp.where(lo, bst_ref[:, 2 * j:2 * j + 1], bst_ref[:, 2 * j + 1:2 * j + 2])
        for n in range(ts // CHUNK):
            v = vb[n * CHUNK:(n + 1) * CHUNK, j * 128:(j + 1) * 128]
            mixed_scr[n * CHUNK:(n + 1) * CHUNK, j * 128:(j + 1) * 128] = jnp.where(lo, _dot(w0, v), _dot(w1, v)) + bias


def _exchange(srcs, scatter, name):
    n = len(srcs)

    def body(*refs):
        src_refs, out_refs = refs[:n], refs[n:2 * n]
        send_sems, recv_sems, local_sems = refs[2 * n:]
        x, y, c = lax.axis_index("x"), lax.axis_index("y"), lax.axis_index("c")
        me = 4 * x + 2 * y + c

        def peer_of(mask):
            px = x if not (mask >> 2) & 1 else 1 - x
            py = y if not (mask >> 1) & 1 else 1 - y
            pc = c if not mask & 1 else 1 - c
            return (px, py, pc), 4 * px + 2 * py + pc

        def remote(k, mask):
            peer, pidx = peer_of(mask)
            return pltpu.make_async_remote_copy(
                src_ref=src_refs[k].at[pidx] if scatter else src_refs[k],
                dst_ref=out_refs[k].at[me],
                send_sem=send_sems.at[k, mask - 1], recv_sem=recv_sems.at[k, mask - 1],
                device_id=peer, device_id_type=pl.DeviceIdType.MESH)

        def arrival(k, mask):
            peer, pidx = peer_of(mask)
            return pltpu.make_async_remote_copy(
                src_ref=src_refs[k].at[pidx] if scatter else src_refs[k],
                dst_ref=out_refs[k].at[pidx],
                send_sem=send_sems.at[k, mask - 1], recv_sem=recv_sems.at[k, mask - 1],
                device_id=peer, device_id_type=pl.DeviceIdType.MESH)

        sends, locals_ = [], []
        for k in range(n):
            for mask in range(1, N_DEV):
                cp = remote(k, mask)
                cp.start()
                sends.append(cp)
            lc = pltpu.make_async_copy(src_refs[k].at[me] if scatter else src_refs[k], out_refs[k].at[me],
                                       local_sems.at[k])
            lc.start()
            locals_.append(lc)
        for k in range(n):
            for mask in range(1, N_DEV):
                arrival(k, mask).wait_recv()
        for cp in sends:
            cp.wait_send()
        for lc in locals_:
            lc.wait()

    outs = [SDS((N_DEV,) + tuple(s.shape[1:] if scatter else s.shape), s.dtype) for s in srcs]
    hbm = pl.BlockSpec(memory_space=pl.ANY)
    return pl.pallas_call(
        body, name=name, out_shape=outs, in_specs=[hbm] * n, out_specs=[hbm] * n,
        scratch_shapes=[pltpu.SemaphoreType.DMA((n, N_DEV - 1)), pltpu.SemaphoreType.DMA((n, N_DEV - 1)),
                        pltpu.SemaphoreType.DMA((n,))],
    )(*srcs)


def _peer_of(mask):
    x, y, c = lax.axis_index("x"), lax.axis_index("y"), lax.axis_index("c")
    px = 1 - x if (mask >> 2) & 1 else x
    py = 1 - y if (mask >> 1) & 1 else y
    pc = 1 - c if mask & 1 else c
    return (px, py, pc), 4 * px + 2 * py + pc


ALL_PEERS = tuple(range(1, N_DEV))
OTHER_CHIPS = (2, 4, 6)
SIBLING = 1


def _split_copy(src_ref, land_ref, send_sem, recv_sem, mask, slot, scatter, outgoing):
    x, y, c = lax.axis_index("x"), lax.axis_index("y"), lax.axis_index("c")
    me = 4 * x + 2 * y + c
    peer, pidx = _peer_of(mask)
    return pltpu.make_async_remote_copy(
        src_ref=src_ref.at[pidx] if scatter else src_ref,
        dst_ref=land_ref.at[me if outgoing else pidx],
        send_sem=send_sem.at[slot], recv_sem=recv_sem.at[slot],
        device_id=peer, device_id_type=pl.DeviceIdType.MESH)


_HBM = pl.BlockSpec(memory_space=pltpu.HBM)
_SEM = pl.BlockSpec(memory_space=pltpu.SEMAPHORE)
_EFFECT = pltpu.SideEffectType.DATAFLOW_SIDE_EFFECTING


def _exchange_start(srcs, scatter, name, masks=ALL_PEERS, own_slot=None):
    n = len(srcs)
    lands = [lax.empty((N_DEV,) + tuple(s.shape[1:] if sc else s.shape), s.dtype) for s, sc in zip(srcs, scatter)]
    if own_slot is not None:
        lands = [land if sc else _with_own(land, s, own_slot) for land, s, sc in zip(lands, srcs, scatter)]
    lands = [pltpu.with_memory_space_constraint(land, pltpu.HBM) for land in lands]
    srcs = [pltpu.with_memory_space_constraint(s, pltpu.HBM) for s in srcs]

    def body(*refs):
        src_refs, land_refs = refs[:n], refs[n:2 * n]
        send_sems, recv_sems = refs[2 * n:3 * n], refs[3 * n:4 * n]
        token = refs[-1]
        for k in range(n):
            for slot, mask in enumerate(masks):
                _split_copy(src_refs[k], land_refs[k], send_sems[k], recv_sems[k], mask, slot, scatter[k], True).start()
        token[...] = jnp.zeros_like(token)

    sem = pltpu.SemaphoreType.DMA((len(masks),))
    out = pl.pallas_call(
        body, name=name,
        out_shape=tuple([sem] * (2 * n) + [pltpu.HBM(s.shape, s.dtype) for s in srcs]
                        + [pltpu.HBM(l.shape, l.dtype) for l in lands] + [SDS((8, 128), F32)]),
        in_specs=[_HBM] * (2 * n),
        out_specs=tuple([_SEM] * (2 * n) + [_HBM] * (2 * n) + [pl.BlockSpec(memory_space=pltpu.VMEM)]),
        input_output_aliases={i: 2 * n + i for i in range(2 * n)},
        compiler_params=pltpu.CompilerParams(has_side_effects=_EFFECT),
    )(*srcs, *lands)
    return out[:n], out[n:2 * n], out[2 * n:3 * n], out[3 * n:4 * n], out[-1]


def _exchange_wait(send_sems, recv_sems, srcs_thru, lands_thru, scatter, after, name, masks=ALL_PEERS):
    n = len(srcs_thru)

    def body(*refs):
        src_refs, land_refs = refs[:n], refs[n:2 * n]
        send_refs, recv_refs = refs[2 * n:3 * n], refs[3 * n:4 * n]
        for k in range(n):
            for slot, mask in enumerate(masks):
                args = (src_refs[k], land_refs[k], send_refs[k], recv_refs[k], mask, slot, scatter[k])
                _split_copy(*args, True).wait_send()
                _split_copy(*args, False).wait_recv()

    out = pl.pallas_call(
        body, name=name,
        out_shape=tuple([pltpu.HBM(s.shape, s.dtype) for s in srcs_thru]
                        + [pltpu.HBM(l.shape, l.dtype) for l in lands_thru]),
        in_specs=[_HBM] * (2 * n) + [_SEM] * (2 * n) + [pl.BlockSpec(memory_space=pl.ANY)],
        out_specs=tuple([_HBM] * (2 * n)),
        input_output_aliases={i: i for i in range(2 * n)},
        compiler_params=pltpu.CompilerParams(has_side_effects=_EFFECT),
    )(*srcs_thru, *lands_thru, *send_sems, *recv_sems, after)
    return out[:n], out[n:]


def _wait_and_forward(send_sems, recv_sems, srcs_thru, lands_thru, after, name, masks):
    n = len(lands_thru)

    def body(*refs):
        src_refs, land_refs = refs[:n], refs[n:2 * n]
        send_refs, recv_refs = refs[2 * n:3 * n], refs[3 * n:4 * n]
        outs = refs[4 * n + 1:]
        fsend, frecv, token = outs[2 * n:3 * n], outs[3 * n:4 * n], outs[-1]
        for k in range(n):
            for slot, mask in enumerate(masks):
                args = (src_refs[k], land_refs[k], send_refs[k], recv_refs[k], mask, slot, False)
                _split_copy(*args, True).wait_send()
                _split_copy(*args, False).wait_recv()
        sibling, _ = _peer_of(SIBLING)
        for k in range(n):
            for slot, mask in enumerate(OTHER_CHIPS):
                _, mine = _peer_of(mask)
                pltpu.make_async_remote_copy(
                    src_ref=land_refs[k].at[mine], dst_ref=land_refs[k].at[mine], send_sem=fsend[k].at[slot],
                    recv_sem=frecv[k].at[slot], device_id=sibling, device_id_type=pl.DeviceIdType.MESH).start()
        token[...] = jnp.zeros_like(token)

    sem = pltpu.SemaphoreType.DMA((len(OTHER_CHIPS),))
    out = pl.pallas_call(
        body, name=name,
        out_shape=tuple([pltpu.HBM(s.shape, s.dtype) for s in srcs_thru] + [pltpu.HBM(l.shape, l.dtype) for l in lands_thru]
                        + [sem] * (2 * n) + [SDS((8, 128), F32)]),
        in_specs=[_HBM] * (2 * n) + [_SEM] * (2 * n) + [pl.BlockSpec(memory_space=pl.ANY)],
        out_specs=tuple([_HBM] * (2 * n) + [_SEM] * (2 * n) + [pl.BlockSpec(memory_space=pltpu.VMEM)]),
        input_output_aliases={i: i for i in range(2 * n)},
        compiler_params=pltpu.CompilerParams(has_side_effects=_EFFECT),
    )(*srcs_thru, *lands_thru, *send_sems, *recv_sems, after)
    return out[2 * n:3 * n], out[3 * n:4 * n], out[n:2 * n], out[-1]


def _forward_wait(send_sems, recv_sems, lands, after, name):
    n = len(lands)

    def body(*refs):
        land_refs, send_refs, recv_refs = refs[:n], refs[n:2 * n], refs[2 * n:3 * n]
        sibling, _ = _peer_of(SIBLING)
        for k in range(n):
            for slot, mask in enumerate(OTHER_CHIPS):
                _, mine = _peer_of(mask)
                _, theirs = _peer_of(mask | SIBLING)
                for block, wait in ((mine, "wait_send"), (theirs, "wait_recv")):
                    getattr(pltpu.make_async_remote_copy(
                        src_ref=land_refs[k].at[block], dst_ref=land_refs[k].at[block],
                        send_sem=send_refs[k].at[slot], recv_sem=recv_refs[k].at[slot], device_id=sibling,
                        device_id_type=pl.DeviceIdType.MESH), wait)()

    return pl.pallas_call(
        body, name=name, out_shape=tuple(pltpu.HBM(l.shape, l.dtype) for l in lands),
        in_specs=[_HBM] * n + [_SEM] * (2 * n) + [pl.BlockSpec(memory_space=pl.ANY)],
        out_specs=tuple([_HBM] * n), input_output_aliases={i: i for i in range(n)},
        compiler_params=pltpu.CompilerParams(has_side_effects=_EFFECT),
    )(*lands, *send_sems, *recv_sems, after)


def _with_own(landed, own, me):
    return lax.dynamic_update_slice_in_dim(landed, own[None], me, axis=0)


def _kv_proj(mem, g_mem, wkv3):
    m = mem.shape[0]

    def body(mem_ref, g_ref, w_ref, kv_ref, mn_ref):
        y, _, _ = _rms_fwd(mem_ref[...], g_ref[...])
        yb = _bf(y)
        mn_ref[...] = yb
        for b in range(N_DEV):
            kv_ref[:, 256 * b:256 * (b + 1)] = _dot(yb, w_ref[b])

    return pl.pallas_call(body, name="kv_proj", out_shape=(SDS((m, 2048), F32), SDS(mem.shape, BF16)),
                          compiler_params=_cparams())(mem, g_mem, wkv3)


def _fwd_in(x, g_mix, w_in3, b_in, ts):
    s, d = x.shape

    def body(x_ref, g_ref, w_ref, b_ref, z_ref, a_ref):
        hn, _, _ = _rms_fwd(x_ref[...], g_ref[...])
        hb = _bf(hn)
        for b in range(N_DEV):
            z_ref[:, 256 * b:256 * (b + 1)] = _dot(hb, w_ref[b]) + b_ref[:, 256 * b:256 * (b + 1)]
        a_ref[...] = z_ref[:, 0:512] * jax.nn.sigmoid(z_ref[:, 512:1024])

    return pl.pallas_call(
        body, name="fwd_in", grid=(s // ts,),
        in_specs=[_row(ts, d), _const(g_mix.shape), _const(w_in3.shape), _const(b_in.shape)],
        out_specs=(_row(ts, 2048), _row(ts, 512)),
        out_shape=(SDS((s, 2048), F32), SDS((s, 512), F32)),
        compiler_params=_cparams("arbitrary"))(x, g_mix, w_in3, b_in)


def _conv_fwd(a, w, b):
    s, cw = a.shape
    rc = 256 if s % 256 == 0 else 128

    def body(a_ref, w_ref, b_ref, c_ref, pad):
        pad[0:CONV_PAD, :] = jnp.zeros((CONV_PAD, 128), F32)
        pad[CONV_PAD:, :] = a_ref[...]

        def chunk(i, carry):
            r0 = pl.multiple_of(i * rc, rc)
            acc = jnp.zeros((rc, 128), F32) + b_ref[...]
            for res, offs in _by_residue(range(CONV_PAD - CONV_K + 1, CONV_PAD + 1)):
                shifted = pad[pl.ds(r0 + res, rc + offs[-1] - res), :]
                for off in offs:
                    k = off - (CONV_PAD - CONV_K + 1)
                    acc = acc + w_ref[k:k + 1, :] * shifted[off - res:off - res + rc, :]
            c_ref[pl.ds(r0, rc), :] = acc
            return carry

        lax.fori_loop(0, s // rc, chunk, 0)

    blk = lambda r: pl.BlockSpec((r, 128), lambda j: (0, j))
    return pl.pallas_call(
        body, name="conv_fwd", grid=(cw // 128,),
        in_specs=[blk(s), blk(CONV_K), blk(1)], out_specs=blk(s), out_shape=SDS((s, cw), F32),
        scratch_shapes=[pltpu.VMEM((s + CONV_PAD, 128), F32)],
        compiler_params=_cparams("arbitrary"))(a, w, b)


def _fwd_out(x, c, z, cln_g, cln_b, gln_g, gln_b, ws, bst, w_out, ts):
    s, d = x.shape

    def body(x_ref, c_ref, zuv_ref, clg, clb, glg, glb, ws_ref, bst_ref, wo_ref, h1_ref, mixed_scr):
        cl, _, _ = _ln_fwd(c_ref[...], clg[...], clb[...])
        co = cl * jax.nn.sigmoid(cl)
        u, _ = _gelu(zuv_ref[:, 0:512])
        vg, _ = _gelu(zuv_ref[:, 512:1024])
        vln, _, _ = _ln_fwd(vg, glg[...], glb[...])
        _mix_fwd(_bf(vln), ws_ref, bst_ref, mixed_scr, ts)
        gm = u * mixed_scr[...]
        h1_ref[...] = x_ref[...] + _dot(_bf(co), wo_ref[0:512, :]) + _dot(_bf(gm), wo_ref[512:1024, :])

    return pl.pallas_call(
        body, name="fwd_out", grid=(s // ts,),
        in_specs=[_row(ts, d), _row(ts, 512), _row(ts, 1024, 1), _const(cln_g.shape), _const(cln_b.shape),
                  _const(gln_g.shape), _const(gln_b.shape), _const(ws.shape), _const(bst.shape), _const(w_out.shape)],
        out_specs=_row(ts, d), out_shape=SDS((s, d), F32),
        scratch_shapes=[pltpu.VMEM((ts, 512), F32)],
        compiler_params=_cparams("arbitrary"))(x, c, z, cln_g, cln_b, gln_g, gln_b, ws, bst, w_out)


def _softmax_rows(sc):
    m = jnp.max(sc, axis=-1, keepdims=True)
    e = jnp.exp(sc - m)
    return e / jnp.sum(e, axis=-1, keepdims=True)


def _fwd_xa(h1, g_xa, wq, kv, wo, ts):
    s, d = h1.shape
    scale = XA_DH ** -0.5

    def body(h_ref, g_ref, wq_ref, kv_ref, wo_ref, h2_ref, o_scr):
        hn, _, _ = _rms_fwd(h_ref[...], g_ref[...])
        q = _dot(_bf(hn), wq_ref[...])
        for h in range(XA_HEADS):
            qh = _bf(q[:, XA_DH * h:XA_DH * (h + 1)])
            kh = _bf(kv_ref[:, XA_DH * h:XA_DH * (h + 1)])
            vh = _bf(kv_ref[:, d + XA_DH * h:d + XA_DH * (h + 1)])
            p = _softmax_rows(_dot(qh, kh, NT) * scale)
            o_scr[:, XA_DH * h:XA_DH * (h + 1)] = _dot(_bf(p), vh)
        h2_ref[...] = h_ref[...] + _dot(_bf(o_scr[...]), wo_ref[...])

    return pl.pallas_call(
        body, name="fwd_xa", grid=(s // ts,),
        in_specs=[_row(ts, d), _const(g_xa.shape), _const(wq.shape), _const(kv.shape), _const(wo.shape)],
        out_specs=_row(ts, d), out_shape=SDS((s, d), F32),
        scratch_shapes=[pltpu.VMEM((ts, d), F32)],
        compiler_params=_cparams("arbitrary"))(h1, g_xa, wq, kv, wo)


def _fwd_ffn(h2, g_ffn, wgut, wdown, g_final, target, ts):
    s, d = h2.shape
    hid = wdown.shape[0]
    hc = hid // 2

    def body(h_ref, g_ref, wgu_ref, wd_ref, gf_ref, t_ref, dh3_ref, dh3b_ref, gu_ref, hn_ref, loss_ref, dgf_ref):
        hn, _, _ = _rms_fwd(h_ref[...], g_ref[...])
        hb = _bf(hn)
        hn_ref[...] = hb
        h3 = h_ref[...]
        for n in range(2):
            g = _dot(hb, wgu_ref[hc * n:hc * (n + 1), :], NT)
            u = _dot(hb, wgu_ref[hid + hc * n:hid + hc * (n + 1), :], NT)
            gu_ref[:, hc * n:hc * (n + 1)] = g
            gu_ref[:, hid + hc * n:hid + hc * (n + 1)] = u
            act = g * jax.nn.sigmoid(g) * u
            h3 = h3 + _dot(_bf(act), wd_ref[hc * n:hc * (n + 1), :])
        y, xh, r = _rms_fwd(h3, gf_ref[...])
        diff = y - t_ref[...]
        part = 0.5 * jnp.sum(jnp.mean(diff * diff, axis=-1, keepdims=True), axis=0, keepdims=True)
        _accumulate(loss_ref, jnp.zeros(loss_ref.shape, F32) + part)
        dh3, dgf = _rms_bwd(diff * (1.0 / d), xh, r, gf_ref[...])
        dh3_ref[...] = dh3
        dh3b_ref[...] = _bf(dh3)
        _accumulate(dgf_ref, dgf)

    return pl.pallas_call(
        body, name="fwd_ffn", grid=(s // ts,),
        in_specs=[_row(ts, d), _const(g_ffn.shape), _const(wgut.shape), _const(wdown.shape), _const(g_final.shape),
                  _row(ts, d)],
        out_specs=(_row(ts, d), _row(ts, d), _row(ts, 2 * hid), _row(ts, d), _acc((1, 128)), _acc((1, d))),
        out_shape=(SDS((s, d), F32), SDS((s, d), BF16), SDS((s, 2 * hid), F32), SDS((s, d), BF16), SDS((1, 128), F32),
                   SDS((1, d), F32)),
        compiler_params=_cparams("arbitrary"))(h2, g_ffn, wgut, wdown, g_final, target)


def _bwd_ffn(h2, dh3, gu, g_ffn, wgut, wdown, ts):
    s, d = h2.shape
    hid = wdown.shape[0]
    hc = hid // 2

    def body(h_ref, dh3_ref, gu_ref, g_ref, wgu_ref, wd_ref, dh2_ref, dh2b_ref, act_ref, dgu_ref, dg_ref):
        _, xh, r = _rms_fwd(h_ref[...], g_ref[...])
        db = _bf(dh3_ref[...])
        dhn = jnp.zeros((ts, d), F32)
        for n in range(2):
            wg = wgu_ref[hc * n:hc * (n + 1), :]
            wu = wgu_ref[hid + hc * n:hid + hc * (n + 1), :]
            g = gu_ref[:, hc * n:hc * (n + 1)]
            u = gu_ref[:, hid + hc * n:hid + hc * (n + 1)]
            sg = jax.nn.sigmoid(g)
            sl = g * sg
            act_ref[:, hc * n:hc * (n + 1)] = _bf(sl * u)
            dact = _dot(db, wd_ref[hc * n:hc * (n + 1), :], NT)
            dgb = _bf(dact * u * _silu_grad(g, sg))
            dub = _bf(dact * sl)
            dgu_ref[:, hc * n:hc * (n + 1)] = dgb
            dgu_ref[:, hid + hc * n:hid + hc * (n + 1)] = dub
            dhn = dhn + _dot(dgb, wg) + _dot(dub, wu)
        dx, dg = _rms_bwd(dhn, xh, r, g_ref[...])
        dh2 = dh3_ref[...] + dx
        dh2_ref[...] = dh2
        dh2b_ref[...] = _bf(dh2)
        _accumulate(dg_ref, dg)

    return pl.pallas_call(
        body, name="bwd_ffn", grid=(s // ts,),
        in_specs=[_row(ts, d), _row(ts, d), _row(ts, 2 * hid), _const(g_ffn.shape), _const(wgut.shape),
                  _const(wdown.shape)],
        out_specs=(_row(ts, d), _row(ts, d), _row(ts, hid), _row(ts, 2 * hid), _acc((1, d))),
        out_shape=(SDS((s, d), F32), SDS((s, d), BF16), SDS((s, hid), BF16), SDS((s, 2 * hid), BF16), SDS((1, d), F32)),
        compiler_params=_cparams("arbitrary"))(h2, dh3, gu, g_ffn, wgut, wdown)


def _bwd_xa(h1, dh2, g_xa, wq, wo, kv, ts):
    s, d = h1.shape
    scale = XA_DH ** -0.5

    def body(h_ref, dh2_ref, g_ref, wq_ref, wo_ref, kv_ref, dh1_ref, dh1b_ref, dq_ref, o_ref, hn_ref, dkv_ref,
             dg_ref, dq_scr):
        hn, xh, r = _rms_fwd(h_ref[...], g_ref[...])
        hb = _bf(hn)
        hn_ref[...] = hb
        q = _dot(hb, wq_ref[...])
        do = _dot(_bf(dh2_ref[...]), wo_ref[...], NT)

        @pl.when(pl.program_id(0) == 0)
        def _():
            dkv_ref[...] = jnp.zeros_like(dkv_ref)

        for h in range(XA_HEADS):
            lo, hi = XA_DH * h, XA_DH * (h + 1)
            qh = _bf(q[:, lo:hi])
            kh = _bf(kv_ref[:, lo:hi])
            vh = _bf(kv_ref[:, d + lo:d + hi])
            p = _softmax_rows(_dot(qh, kh, NT) * scale)
            pb = _bf(p)
            o_ref[:, lo:hi] = _bf(_dot(pb, vh))
            doh = _bf(do[:, lo:hi])
            dp = _dot(doh, vh, NT)
            ds = p * (dp - jnp.sum(p * dp, axis=-1, keepdims=True)) * scale
            dsb = _bf(ds)
            dq_scr[:, lo:hi] = _dot(dsb, kh)
            dkv_ref[:, lo:hi] += _dot(dsb, qh, TN)
            dkv_ref[:, d + lo:d + hi] += _dot(pb, doh, TN)
        dqb = _bf(dq_scr[...])
        dq_ref[...] = dqb
        dx, dg = _rms_bwd(_dot(dqb, wq_ref[...], NT), xh, r, g_ref[...])
        dh1 = dh2_ref[...] + dx
        dh1_ref[...] = dh1
        dh1b_ref[...] = _bf(dh1)
        _accumulate(dg_ref, dg)

    return pl.pallas_call(
        body, name="bwd_xa", grid=(s // ts,),
        in_specs=[_row(ts, d), _row(ts, d), _const(g_xa.shape), _const(wq.shape), _const(wo.shape), _const(kv.shape)],
        out_specs=(_row(ts, d), _row(ts, d), _row(ts, d), _row(ts, d), _row(ts, d), _acc(kv.shape), _acc((1, d))),
        out_shape=(SDS((s, d), F32), SDS((s, d), BF16), SDS((s, d), BF16), SDS((s, d), BF16), SDS((s, d), BF16),
                   SDS(kv.shape, F32), SDS((1, d), F32)),
        scratch_shapes=[pltpu.VMEM((ts, d), F32)],
        compiler_params=_cparams("arbitrary"))(h1, dh2, g_xa, wq, wo, kv)


def _bwd_kv(dkv, mn, mem, g_mem, wkv3):
    d = mem.shape[1]

    def body(dkv_ref, mn_ref, mem_ref, g_ref, w_ref, dw_ref, dg_ref):
        dkvb = _bf(dkv_ref[...])
        dmn = jnp.zeros(mem_ref.shape, F32)
        for b in range(N_DEV):
            blk = dkvb[:, 256 * b:256 * (b + 1)]
            dmn = dmn + _dot(blk, w_ref[b], NT)
            dw_ref[b] = _bf(_dot(mn_ref[...], blk, TN))
        _, xh, r = _rms_fwd(mem_ref[...], g_ref[...])
        _, dg = _rms_bwd(dmn, xh, r, g_ref[...])
        dg_ref[...] = dg

    return pl.pallas_call(body, name="bwd_kv", out_shape=(SDS(wkv3.shape, BF16), SDS((1, d), F32)),
                          compiler_params=_cparams())(dkv, mn, mem, g_mem, wkv3)


def _bwd_out(dh1, c, z, cln_g, cln_b, gln_g, gln_b, ws, wst, bst, w_out, ts):
    s, d = dh1.shape
    nh = GM_HEADS

    def body(dh1_ref, c_ref, zuv_ref, clg, clb, glg, glb, ws_ref, wst_ref, bst_ref, wo_ref,
             cat_ref, dc_ref, dzuv_ref, dws_ref, dbst_ref, dclg_ref, dclb_ref, dglg_ref, dglb_ref, dbin_ref,
             mixed_scr, dv_scr):
        cl, chat, crs = _ln_fwd(c_ref[...], clg[...], clb[...])
        sg = jax.nn.sigmoid(cl)
        zu = zuv_ref[:, 0:512]
        zv = zuv_ref[:, 512:1024]
        u, tu = _gelu(zu)
        vg, tv = _gelu(zv)
        vln, vhat, vrs = _ln_fwd(vg, glg[...], glb[...])
        vb = _bf(vln)
        _mix_fwd(vb, ws_ref, bst_ref, mixed_scr, ts)
        mixed = mixed_scr[...]
        cat_ref[:, 0:512] = _bf(cl * sg)
        cat_ref[:, 512:1024] = _bf(u * mixed)
        dcat = _dot(_bf(dh1_ref[...]), wo_ref[...], NT)
        dgm = dcat[:, 512:1024]
        dc, dclg, dclb = _ln_bwd(dcat[:, 0:512] * _silu_grad(cl, sg), chat, crs, clg[...])
        dc_ref[...] = dc
        dzu = dgm * mixed * _gelu_grad(zu, tu)
        dm = dgm * u

        @pl.when(pl.program_id(0) == 0)
        def _():
            dws_ref[...] = jnp.zeros_like(dws_ref)
            dbst_ref[...] = jnp.zeros_like(dbst_ref)

        tril, triu, lo = _mix_masks()
        head = lax.broadcasted_iota(jnp.int32, (1, nh), 1)
        for j in range(nh // 2):
            w0t = _bf(jnp.where(triu, wst_ref[2 * j], 0.0))
            w1t = _bf(jnp.where(triu, wst_ref[2 * j + 1], 0.0))
            for n in range(ts // CHUNK):
                rows = slice(n * CHUNK, (n + 1) * CHUNK)
                lanes = slice(j * 128, (j + 1) * 128)
                dmc = dm[rows, lanes]
                dmb = _bf(dmc)
                dv_scr[rows, lanes] = jnp.where(lo, _dot(w0t, dmb), _dot(w1t, dmb))
                vc = vb[rows, lanes]
                d0 = jnp.where(lo, dmc, 0.0)
                d1 = dmc - d0
                dws_ref[2 * j] += jnp.where(tril, _dot(_bf(d0), vc, NT), 0.0)
                dws_ref[2 * j + 1] += jnp.where(tril, _dot(_bf(d1), vc, NT), 0.0)
                dbst_ref[...] += (jnp.sum(d0, axis=1, keepdims=True) * (head == 2 * j).astype(F32)
                                  + jnp.sum(d1, axis=1, keepdims=True) * (head == 2 * j + 1).astype(F32))
        dvg, dglg, dglb = _ln_bwd(dv_scr[...], vhat, vrs, glg[...])
        dzv = dvg * _gelu_grad(zv, tv)
        dzuv_ref[:, 0:512] = _bf(dzu)
        dzuv_ref[:, 512:1024] = _bf(dzv)
        _accumulate(dclg_ref, dclg)
        _accumulate(dclb_ref, dclb)
        _accumulate(dglg_ref, dglg)
        _accumulate(dglb_ref, dglb)
        _accumulate(dbin_ref, jnp.concatenate([jnp.sum(dzu, axis=0, keepdims=True),
                                               jnp.sum(dzv, axis=0, keepdims=True)], axis=1))

    vec = (1, 512)
    return pl.pallas_call(
        body, name="bwd_out", grid=(s // ts,),
        in_specs=[_row(ts, d), _row(ts, 512), _row(ts, 1024, 1), _const(cln_g.shape), _const(cln_b.shape),
                  _const(gln_g.shape), _const(gln_b.shape), _const(ws.shape), _const(wst.shape), _const(bst.shape),
                  _const(w_out.shape)],
        out_specs=(_row(ts, d), _row(ts, 512), _row(ts, 1024), _acc(ws.shape), _acc(bst.shape), _acc(vec), _acc(vec),
                   _acc(vec), _acc(vec), _acc((1, 1024))),
        out_shape=(SDS((s, d), BF16), SDS((s, 512), F32), SDS((s, 1024), BF16), SDS(ws.shape, F32),
                   SDS(bst.shape, F32), SDS(vec, F32), SDS(vec, F32), SDS(vec, F32), SDS(vec, F32), SDS((1, 1024), F32)),
        scratch_shapes=[pltpu.VMEM((ts, 512), F32), pltpu.VMEM((ts, 512), F32)],
        compiler_params=_cparams("arbitrary"))(dh1, c, z, cln_g, cln_b, gln_g, gln_b, ws, wst, bst, w_out)


def _conv_bwd(dc, a, w):
    s, cw = a.shape
    rc = 256 if s % 256 == 0 else 128

    def body(dc_ref, a_ref, w_ref, da_ref, dw_ref, db_ref, pad_a, pad_d, part):
        pad_a[0:CONV_PAD, :] = jnp.zeros((CONV_PAD, 128), F32)
        pad_a[CONV_PAD:, :] = a_ref[...]
        pad_d[0:s, :] = dc_ref[...]
        pad_d[s:, :] = jnp.zeros((CONV_PAD, 128), F32)
        part[...] = jnp.zeros_like(part)

        def rows8(v):
            return jnp.sum(v.reshape(rc // 8, 8, 128), axis=0)

        def chunk(i, carry):
            r0 = pl.multiple_of(i * rc, rc)
            dcc = pad_d[pl.ds(r0, rc), :]
            acc = jnp.zeros((rc, 128), F32)
            for res, offs in _by_residue(range(0, CONV_K)):
                shifted = pad_d[pl.ds(r0 + res, rc + offs[-1] - res), :]
                for off in offs:
                    k = CONV_K - 1 - off
                    acc = acc + w_ref[k:k + 1, :] * shifted[off - res:off - res + rc, :]
            da_ref[pl.ds(r0, rc), :] = acc
            for res, offs in _by_residue(range(CONV_PAD - CONV_K + 1, CONV_PAD + 1)):
                shifted = pad_a[pl.ds(r0 + res, rc + offs[-1] - res), :]
                for off in offs:
                    k = off - (CONV_PAD - CONV_K + 1)
                    part[8 * k:8 * k + 8, :] += rows8(dcc * shifted[off - res:off - res + rc, :])
            part[8 * CONV_PAD:, :] += rows8(dcc)
            return carry

        lax.fori_loop(0, s // rc, chunk, 0)
        sums = jnp.sum(part[...].reshape(CONV_PAD + 1, 8, 128), axis=1)
        dw_ref[...] = sums[0:CONV_PAD, :]
        db_ref[...] = sums[CONV_PAD:, :]

    blk = lambda r: pl.BlockSpec((r, 128), lambda j: (0, j))
    return pl.pallas_call(
        body, name="conv_bwd", grid=(cw // 128,),
        in_specs=[blk(s), blk(s), blk(CONV_K)], out_specs=(blk(s), blk(CONV_PAD), blk(1)),
        out_shape=(SDS((s, cw), F32), SDS((CONV_PAD, cw), F32), SDS((1, cw), F32)),
        scratch_shapes=[pltpu.VMEM((s + CONV_PAD, 128), F32), pltpu.VMEM((s + CONV_PAD, 128), F32),
                        pltpu.VMEM((8 * (CONV_PAD + 1), 128), F32)],
        compiler_params=_cparams("arbitrary"))(dc, a, w)


def _bwd_in(x, dh1, da, z, dzuv, g_mix, w_in3, ts):
    s, d = x.shape

    def body(x_ref, dh1_ref, da_ref, zag_ref, dzuv_ref, g_ref, w_ref, dx_ref, dz_ref, hn_ref, dbin_ref, dg_ref):
        za = zag_ref[:, 0:512]
        sg = jax.nn.sigmoid(zag_ref[:, 512:1024])
        da_ = da_ref[...]
        dza = da_ * sg
        dzg = da_ * za * sg * (1.0 - sg)
        dz_ref[:, 0:512] = _bf(dza)
        dz_ref[:, 512:1024] = _bf(dzg)
        dz_ref[:, 1024:2048] = dzuv_ref[...]
        dhn = jnp.zeros((ts, d), F32)
        for b in range(N_DEV):
            dhn = dhn + _dot(dz_ref[:, 256 * b:256 * (b + 1)], w_ref[b], NT)
        hn, xh, r = _rms_fwd(x_ref[...], g_ref[...])
        hn_ref[...] = _bf(hn)
        dxn, dg = _rms_bwd(dhn, xh, r, g_ref[...])
        dx_ref[...] = dh1_ref[...] + dxn
        _accumulate(dg_ref, dg)
        _accumulate(dbin_ref, jnp.concatenate([jnp.sum(dza, axis=0, keepdims=True),
                                               jnp.sum(dzg, axis=0, keepdims=True)], axis=1))

    return pl.pallas_call(
        body, name="bwd_in", grid=(s // ts,),
        in_specs=[_row(ts, d), _row(ts, d), _row(ts, 512), _row(ts, 1024, 0), _row(ts, 1024), _const(g_mix.shape),
                  _const(w_in3.shape)],
        out_specs=(_row(ts, d), _row(ts, 2048), _row(ts, d), _acc((1, 1024)), _acc((1, d))),
        out_shape=(SDS((s, d), F32), SDS((s, 2048), BF16), SDS((s, d), BF16), SDS((1, 1024), F32), SDS((1, d), F32)),
        compiler_params=_cparams("arbitrary"))(x, dh1, da, z, dzuv, g_mix, w_in3)


def _tn_matmul(a, b, tm, name, col_blocks=False, behind=None):
    s, m = a.shape
    n = b.shape[1]
    cb = 256
    tn = 2 * cb if col_blocks else n
    if col_blocks:
        tm = m
    ts = min(s, 512)
    n_s = s // ts

    def body(a_ref, b_ref, *rest):
        o_ref, acc = rest[-2:]
        k = pl.program_id(2)

        @pl.when(k == 0)
        def _():
            acc[...] = jnp.zeros_like(acc)

        acc[...] += _dot(_bf(a_ref[...]), _bf(b_ref[...]), TN)

        @pl.when(k == n_s - 1)
        def _():
            if col_blocks:
                for blk in range(tn // cb):
                    o_ref[blk] = _bf(acc[:, cb * blk:cb * (blk + 1)])
            else:
                o_ref[...] = _bf(acc[...])

    if col_blocks:
        out_shape = SDS((n // cb, m, cb), BF16)
        out_spec = pl.BlockSpec((tn // cb, m, cb), lambda i, j, k: (j, 0, 0))
    else:
        out_shape = SDS((m, n), BF16)
        out_spec = pl.BlockSpec((tm, tn), lambda i, j, k: (i, j))
    extra = [] if behind is None else [behind]
    return pl.pallas_call(
        body, name=name, grid=(m // tm, n // tn, n_s),
        in_specs=[pl.BlockSpec((ts, tm), lambda i, j, k: (k, i)), pl.BlockSpec((ts, tn), lambda i, j, k: (k, j))]
        + [pl.BlockSpec(memory_space=pl.ANY)] * len(extra),
        out_specs=out_spec, out_shape=out_shape,
        scratch_shapes=[pltpu.VMEM((tm, tn), F32)],
        compiler_params=_cparams("parallel", "parallel", "arbitrary"))(a, b, *extra)


def _row_tile(r, cands):
    for cand in cands:
        if r % cand == 0:
            return cand
    return r


def _sum_in_device_order(me_ref, land_ref, own_ref):
    acc = None
    for dev in range(N_DEV):
        part = jnp.where(me_ref[0] == dev, own_ref[0], land_ref[dev]).astype(F32)
        acc = part if acc is None else acc + part
    return acc


def _sum8(land, own, me, name):
    _, r, c = land.shape
    tr = _row_tile(r, (512, 256, 352, 128, 8))
    own3 = own if own.ndim == 3 else own[None]
    own_map = (lambda i, me_ref: (me_ref[0], i, 0)) if own.ndim == 3 else (lambda i, me_ref: (0, i, 0))

    def body(me_ref, land_ref, own_ref, o_ref):
        o_ref[...] = _sum_in_device_order(me_ref, land_ref, own_ref)

    return pl.pallas_call(
        body, name=name, out_shape=SDS((r, c), F32),
        grid_spec=pltpu.PrefetchScalarGridSpec(
            num_scalar_prefetch=1, grid=(r // tr,),
            in_specs=[pl.BlockSpec((N_DEV, tr, c), lambda i, me_ref: (0, i, 0)), pl.BlockSpec((1, tr, c), own_map)],
            out_specs=pl.BlockSpec((tr, c), lambda i, me_ref: (i, 0))),
        compiler_params=_cparams("parallel"))(me, land, own3)


def _adam_update(w, g, m, v):
    m2 = ADAM_B1 * m + (1.0 - ADAM_B1) * g
    v2 = ADAM_B2 * v + (1.0 - ADAM_B2) * (g * g)
    m_hat = m2 / (1.0 - ADAM_B1 ** ADAM_STEP)
    v_hat = v2 / (1.0 - ADAM_B2 ** ADAM_STEP)
    return -ADAM_LR * (m_hat / (jnp.sqrt(v_hat) + ADAM_EPS) + ADAM_WD * w), m2, v2


def _sum_adamw(land, parts, w, m, v, me, name):
    r, c = w.shape
    tr = _row_tile(r, (256, 128, 176, 8))

    def body(me_ref, land_ref, own_ref, w_ref, m_ref, v_ref, g_ref, d_ref, m2_ref, v2_ref):
        g = _sum_in_device_order(me_ref, land_ref, own_ref)
        g_ref[...] = g
        d_ref[...], m2_ref[...], v2_ref[...] = _adam_update(w_ref[...], g, m_ref[...], v_ref[...])

    blk = pl.BlockSpec((tr, c), lambda i, me_ref: (i, 0))
    return pl.pallas_call(
        body, name=name, out_shape=(SDS((r, c), F32),) * 4,
        grid_spec=pltpu.PrefetchScalarGridSpec(
            num_scalar_prefetch=1, grid=(r // tr,),
            in_specs=[pl.BlockSpec((N_DEV, tr, c), lambda i, me_ref: (0, i, 0)),
                      pl.BlockSpec((1, tr, c), lambda i, me_ref: (me_ref[0], i, 0)), blk, blk, blk],
            out_specs=(blk,) * 4),
        compiler_params=_cparams("parallel"))(me, land, parts, w, m, v)


def _adamw_many(ws, gs, ms, vs, name):
    n = len(ws)

    def body(*refs):
        ins, outs = refs[:4 * n], refs[4 * n:]
        for i in range(n):
            w, g, m, v = (ins[j * n + i][...] for j in range(4))
            outs[i][...], outs[n + i][...], outs[2 * n + i][...] = _adam_update(w, g, m, v)

    shapes = [SDS(w.shape, F32) for w in ws]
    out = pl.pallas_call(body, name=name, out_shape=shapes * 3, compiler_params=_cparams())(*ws, *gs, *ms, *vs)
    return out[:n], out[n:2 * n], out[2 * n:]


def _pack(arrs):
    flat = jnp.concatenate([a.reshape(-1) for a in arrs])
    pad = (-flat.shape[0]) % (128 * 128)
    return jnp.pad(flat, (0, pad)).reshape(-1, 128)


def _unpack(packed, shapes):
    flat = packed.reshape(-1)
    out, off = [], 0
    for shp in shapes:
        size = 1
        for dim in shp:
            size *= dim
        out.append(flat[off:off + size].reshape(shp))
        off += size
    return out


SMALL = ("norm_mix_g", "b_in", "conv_w", "conv_b", "conv_ln_g", "conv_ln_b", "gm_ln_g", "gm_ln_b", "gm_w_s", "gm_b_s",
         "norm_xa_g", "mem_norm_g", "norm_ffn_g", "final_norm_g")
BIG = ("w_in", "w_out", "xa_wq", "xa_wkv", "xa_wo", "ffn_w_gate_up", "ffn_w_down")
WEIGHTS = ("norm_mix_g", "w_in", "b_in", "conv_w", "conv_b", "conv_ln_g", "conv_ln_b", "gm_ln_g", "gm_ln_b", "gm_w_s",
           "gm_b_s", "w_out", "norm_xa_g", "mem_norm_g", "xa_wq", "xa_wkv", "xa_wo", "norm_ffn_g", "ffn_w_gate_up",
           "ffn_w_down", "final_norm_g")


def kernel(x, mem, norm_mix_g, w_in, b_in, conv_w, conv_b, conv_ln_g, conv_ln_b, gm_ln_g, gm_ln_b, gm_w_s, gm_b_s, w_out, norm_xa_g, mem_norm_g, xa_wq, xa_wkv, xa_wo, norm_ffn_g, ffn_w_gate_up, ffn_w_down, final_norm_g, loss_target, m_norm_mix_g, m_w_in, m_b_in, m_conv_w, m_conv_b, m_conv_ln_g, m_conv_ln_b, m_gm_ln_g, m_gm_ln_b, m_gm_w_s, m_gm_b_s, m_w_out, m_norm_xa_g, m_mem_norm_g, m_xa_wq, m_xa_wkv, m_xa_wo, m_norm_ffn_g, m_ffn_w_gate_up, m_ffn_w_down, m_final_norm_g, v_norm_mix_g, v_w_in, v_b_in, v_conv_w, v_conv_b, v_conv_ln_g, v_conv_ln_b, v_gm_ln_g, v_gm_ln_b, v_gm_w_s, v_gm_b_s, v_w_out, v_norm_xa_g, v_mem_norm_g, v_xa_wq, v_xa_wkv, v_xa_wo, v_norm_ffn_g, v_ffn_w_gate_up, v_ffn_w_down, v_final_norm_g):
    w = dict(norm_mix_g=norm_mix_g, w_in=w_in, b_in=b_in, conv_w=conv_w, conv_b=conv_b, conv_ln_g=conv_ln_g,
             conv_ln_b=conv_ln_b, gm_ln_g=gm_ln_g, gm_ln_b=gm_ln_b, gm_w_s=gm_w_s, gm_b_s=gm_b_s, w_out=w_out,
             norm_xa_g=norm_xa_g, mem_norm_g=mem_norm_g, xa_wq=xa_wq, xa_wkv=xa_wkv, xa_wo=xa_wo,
             norm_ffn_g=norm_ffn_g, ffn_w_gate_up=ffn_w_gate_up, ffn_w_down=ffn_w_down, final_norm_g=final_norm_g)
    mom = dict(norm_mix_g=m_norm_mix_g, w_in=m_w_in, b_in=m_b_in, conv_w=m_conv_w, conv_b=m_conv_b,
               conv_ln_g=m_conv_ln_g, conv_ln_b=m_conv_ln_b, gm_ln_g=m_gm_ln_g, gm_ln_b=m_gm_ln_b, gm_w_s=m_gm_w_s,
               gm_b_s=m_gm_b_s, w_out=m_w_out, norm_xa_g=m_norm_xa_g, mem_norm_g=m_mem_norm_g, xa_wq=m_xa_wq,
               xa_wkv=m_xa_wkv, xa_wo=m_xa_wo, norm_ffn_g=m_norm_ffn_g, ffn_w_gate_up=m_ffn_w_gate_up,
               ffn_w_down=m_ffn_w_down, final_norm_g=m_final_norm_g)
    var = dict(norm_mix_g=v_norm_mix_g, w_in=v_w_in, b_in=v_b_in, conv_w=v_conv_w, conv_b=v_conv_b,
               conv_ln_g=v_conv_ln_g, conv_ln_b=v_conv_ln_b, gm_ln_g=v_gm_ln_g, gm_ln_b=v_gm_ln_b, gm_w_s=v_gm_w_s,
               gm_b_s=v_gm_b_s, w_out=v_w_out, norm_xa_g=v_norm_xa_g, mem_norm_g=v_mem_norm_g, xa_wq=v_xa_wq,
               xa_wkv=v_xa_wkv, xa_wo=v_xa_wo, norm_ffn_g=v_norm_ffn_g, ffn_w_gate_up=v_ffn_w_gate_up,
               ffn_w_down=v_ffn_w_down, final_norm_g=v_final_norm_g)

    me = 4 * lax.axis_index("x") + 2 * lax.axis_index("y") + lax.axis_index("c")
    s, d = x.shape[1], x.shape[2]
    xs = x.reshape(s, d)
    mems = mem.reshape(mem.shape[1], d)
    tgt = loss_target.reshape(s, d)
    ts = min(512, s)
    ts_ffn = min(256, s)
    row = lambda a: a.reshape(1, -1)

    conv_w_pad = jnp.pad(conv_w, ((0, CONV_PAD - CONV_K), (0, 128 - conv_w.shape[1])))
    first_level = (SIBLING,) + OTHER_CHIPS
    ex_first = _exchange_start([_bf(w_in), conv_w_pad], [False] * 2, "gather_start_in", first_level, own_slot=me)
    behind = lambda t: _bf(t + ex_first[4][0:1, 0:1])
    shards = [behind(xa_wkv), behind(w_out), behind(xa_wq), behind(xa_wo), behind(ffn_w_gate_up.T), behind(ffn_w_down)]
    ex_rest = _exchange_start(shards, [False] * len(shards), "gather_start_rest", first_level, own_slot=me)
    g_send, g_recv, g_src, g_land = (list(ex_first[i]) + list(ex_rest[i]) for i in range(4))
    g_tok = ex_rest[4]

    def arrived(idx, after, name):
        pick = lambda seq: [seq[i] for i in idx]
        return _wait_and_forward(pick(g_send), pick(g_recv), pick(g_src), pick(g_land), after, "gather_pass_" + name,
                                 first_level)

    def complete(handle, after, name):
        return _forward_wait(handle[0], handle[1], handle[2], after, "forward_wait_" + name)

    bst = gm_b_s.T
    wst = jnp.swapaxes(gm_w_s, 1, 2)

    h_in = arrived((0, 1), g_tok, "in")
    w_in3, conv_w8 = complete(h_in, h_in[3], "in")
    conv_w_f = conv_w8[:, :CONV_K, :conv_w.shape[1]].transpose(1, 0, 2).reshape(CONV_K, -1)
    cw = conv_w_f.shape[1]
    z, a = _fwd_in(xs, row(norm_mix_g), w_in3, row(b_in), ts)
    h_out = arrived((2, 3), z, "out")
    c = _conv_fwd(a, conv_w_f + h_out[3][0:1, 0:1], row(conv_b))
    h_xa = arrived((4, 5), c, "xa")
    wkv3, w_out3 = complete(h_out, h_xa[3], "out")
    w_out_f = w_out3.reshape(-1, d)
    kv, mn = _kv_proj(mems, row(mem_norm_g), wkv3)
    h1 = _fwd_out(xs, c, z, row(conv_ln_g), row(conv_ln_b), row(gm_ln_g), row(gm_ln_b), gm_w_s, bst, w_out_f, ts)
    h_gut = arrived((6,), h1, "gate_up")
    wq3, wo3 = complete(h_xa, h_gut[3], "xa")
    wq_f = wq3.reshape(-1, d)
    wo_f = wo3.reshape(-1, d)
    h2 = _fwd_xa(h1, row(norm_xa_g), wq_f, kv, wo_f, ts)
    h_down = arrived((7,), h2, "down")
    (wgut3,) = complete(h_gut, h_down[3], "gate_up")
    (wdown3,) = complete(h_down, wgut3, "down")
    wgut_f = wgut3.reshape(-1, d)
    wdown_f = wdown3.reshape(-1, d)
    dh3, dh3b, gu, hn2, loss_p, d_final_g = _fwd_ffn(h2, row(norm_ffn_g), wgut_f, wdown_f, row(final_norm_g), tgt,
                                                     ts_ffn)

    blocks = lambda m: m.reshape(N_DEV, -1, d)
    tok = lambda ex: ex[4][0:1, 0:1]
    dh2, dh2b, act, dgu, d_ffn_g = _bwd_ffn(h2, dh3, gu, row(norm_ffn_g), wgut_f, wdown_f, ts_ffn)
    dwgut = _tn_matmul(dgu, hn2, 1408, "dw_gate_up")
    dwdown = _tn_matmul(act, dh3b, 1408, "dw_down")
    ex_ffn = _exchange_start([blocks(dwgut), blocks(dwdown)], [True, True], "scatter_start_ffn")
    dh1, dh1b, dq, o, hn1, dkv, d_xa_g = _bwd_xa(h1, dh2, row(norm_xa_g) + tok(ex_ffn), wq_f, wo_f, kv, ts)
    dwq = _tn_matmul(hn1, dq, 1024, "dw_q")
    dwo = _tn_matmul(o, dh2b, 1024, "dw_o")
    dwkv3, d_mem_g = _bwd_kv(dkv, mn, mems, row(mem_norm_g), wkv3)
    ex_xa = _exchange_start([blocks(dwq), blocks(dwo), dwkv3], [True] * 3, "scatter_start_xa")
    (cat, dc, dzuv, dws, dbst, d_cln_g, d_cln_b, d_gln_g, d_gln_b, dbin_uv) = _bwd_out(
        dh1, c, z, row(conv_ln_g) + tok(ex_xa), row(conv_ln_b), row(gm_ln_g), row(gm_ln_b), gm_w_s, wst, bst,
        w_out_f, ts)
    dwout = _tn_matmul(cat, dh1b, 1024, "dw_out")
    early = dict(b_in_uv=dbin_uv, conv_ln_g=d_cln_g, conv_ln_b=d_cln_b, gm_ln_g=d_gln_g, gm_ln_b=d_gln_b, gm_w_s=dws,
                 gm_b_s=dbst.T, norm_xa_g=d_xa_g, mem_norm_g=d_mem_g, norm_ffn_g=d_ffn_g, final_norm_g=d_final_g,
                 loss=loss_p)
    ex_out = _exchange_start([blocks(dwout), _pack(list(early.values()))], [True, False], "scatter_start_out")
    da, dconv_w, dconv_b = _conv_bwd(dc, a, conv_w_f + tok(ex_out))
    dx, dz, hn0, dbin_ag, d_mix_g = _bwd_in(xs, dh1, da, z, dzuv, row(norm_mix_g), w_in3, ts)
    late = dict(norm_mix_g=d_mix_g, b_in_ag=dbin_ag, conv_w=dconv_w[:CONV_K], conv_b=dconv_b)
    ex_late = _exchange_start([_pack(list(late.values()))], [False], "gather_start_late")
    dw_in3 = _tn_matmul(hn0, dz, 0, "dw_in", col_blocks=True, behind=ex_late[4])
    ex_in = _exchange_start([dw_in3], [True], "scatter_start_in")

    me1 = me.astype(jnp.int32).reshape(1)

    def small_totals(ex, k, after, name):
        srcs, lands = _exchange_wait([ex[0][k]], [ex[1][k]], [ex[2][k]], [ex[3][k]], [False], after, name)
        return _sum8(lands[0], srcs[0], me1, "sum_" + name)

    early_tot = small_totals(ex_out, 1, ex_in[4], "small_early")
    late_tot = small_totals(ex_late, 0, early_tot, "small_late")
    early_g = dict(zip(early, _unpack(early_tot, [v.shape for v in early.values()])))
    late_g = dict(zip(late, _unpack(late_tot, [v.shape for v in late.values()])))
    loss = early_g["loss"][0, 0]
    grads = {}
    for nm in SMALL:
        if nm == "b_in":
            g = jnp.concatenate([late_g["b_in_ag"], early_g["b_in_uv"]], axis=1)
        elif nm == "conv_w":
            g = lax.dynamic_slice_in_dim(late_g[nm], me * conv_w.shape[1], conv_w.shape[1], axis=1)
        else:
            g = late_g[nm] if nm in late_g else early_g[nm]
        grads[nm] = g.reshape(w[nm].shape)

    delta, new_m, new_v = {}, {}, {}
    small = [[src[nm] for nm in SMALL] for src in (w, grads, mom, var)]
    for dst, vals in zip((delta, new_m, new_v), _adamw_many(*small, "adamw_small")):
        dst.update(zip(SMALL, vals))

    def reduced(srcs, lands, names):
        for nm, src, land in zip(names, srcs, lands):
            view = (lambda t: t.T) if nm == "ffn_w_gate_up" else (lambda t: t)
            upd = _sum_adamw(land, src, view(w[nm]), view(mom[nm]), view(var[nm]), me1, "adamw_" + nm)
            grads[nm], delta[nm], new_m[nm], new_v[nm] = (view(t) for t in upd)
        return new_v[names[-1]]

    after = delta[SMALL[0]]
    for ex, names, tag in ((ex_ffn, ("ffn_w_gate_up", "ffn_w_down"), "ffn"), (ex_xa, ("xa_wq", "xa_wo", "xa_wkv"), "xa")):
        srcs, lands = _exchange_wait(ex[0], ex[1], ex[2], ex[3], [True] * len(names), after, "scatter_wait_" + tag)
        after = reduced(srcs, lands, names)
    srcs, lands = _exchange_wait([ex_out[0][0]], [ex_out[1][0]], [ex_out[2][0]], [ex_out[3][0]], [True], after,
                                 "scatter_wait_out")
    after = reduced(srcs, lands, ("w_out",))
    srcs, lands = _exchange_wait(ex_in[0], ex_in[1], ex_in[2], ex_in[3], [True], after, "scatter_wait_in")
    reduced(srcs, lands, ("w_in",))

    return (loss, dx.reshape(x.shape), *[grads[nm] for nm in WEIGHTS], *[delta[nm] for nm in WEIGHTS],
            *[new_m[nm] for nm in WEIGHTS], *[new_v[nm] for nm in WEIGHTS])
```

```python
import functools

import jax
import jax.numpy as jnp
from jax import lax
from jax.experimental import pallas as pl
from jax.experimental.pallas import tpu as pltpu

F32 = jnp.float32
BF16 = jnp.bfloat16
SDS = jax.ShapeDtypeStruct

N_DEV = 8
RMS_EPS = 1e-6
LN_EPS = 1e-5
CONV_K = 31
CONV_PAD = 32
CHUNK = 128
GM_HEADS = 8
XA_HEADS = 4
XA_DH = 256
GELU_K0 = 0.7978845608028654
GELU_K1 = 0.044715
ADAM_LR = 0.001
ADAM_B1 = 0.9
ADAM_B2 = 0.999
ADAM_EPS = 1e-08
ADAM_WD = 0.01
ADAM_STEP = 10
VMEM_LIMIT = 60 * 1024 * 1024

NN = (((1,), (0,)), ((), ()))
NT = (((1,), (1,)), ((), ()))
TN = (((0,), (0,)), ((), ()))


def _dot(a, b, dims=NN):
    return lax.dot_general(a, b, dims, preferred_element_type=F32)


def _bf(x):
    return x.astype(BF16)


def _cparams(*sem):
    return pltpu.CompilerParams(dimension_semantics=tuple(sem) if sem else None, vmem_limit_bytes=VMEM_LIMIT)


def _row(ts, w, col=0):
    return pl.BlockSpec((ts, w), lambda i: (i, col))


def _const(shape):
    nd = len(shape)
    return pl.BlockSpec(shape, lambda i: (0,) * nd, pipeline_mode=pl.Buffered(1))


def _acc(shape):
    nd = len(shape)
    return pl.BlockSpec(shape, lambda i: (0,) * nd)


def _rms_fwd(x, g):
    r = lax.rsqrt(jnp.mean(x * x, axis=-1, keepdims=True) + RMS_EPS)
    xh = x * r
    return xh * g, xh, r


def _rms_bwd(dy, xh, r, g):
    gdy = dy * g
    dx = r * (gdy - xh * jnp.mean(gdy * xh, axis=-1, keepdims=True))
    dg = jnp.sum(dy * xh, axis=0, keepdims=True)
    return dx, dg


def _ln_fwd(x, g, b):
    mu = jnp.mean(x, axis=-1, keepdims=True)
    xc = x - mu
    rs = lax.rsqrt(jnp.mean(xc * xc, axis=-1, keepdims=True) + LN_EPS)
    xh = xc * rs
    return xh * g + b, xh, rs


def _ln_bwd(dy, xh, rs, g):
    dxh = dy * g
    dx = rs * (dxh - jnp.mean(dxh, axis=-1, keepdims=True) - xh * jnp.mean(dxh * xh, axis=-1, keepdims=True))
    return dx, jnp.sum(dy * xh, axis=0, keepdims=True), jnp.sum(dy, axis=0, keepdims=True)


def _gelu(x):
    t = jnp.tanh(GELU_K0 * (x + GELU_K1 * (x * x * x)))
    return 0.5 * x * (1.0 + t), t


def _gelu_grad(x, t):
    return 0.5 * (1.0 + t) + 0.5 * x * (1.0 - t * t) * (GELU_K0 * (1.0 + 3.0 * GELU_K1 * x * x))


def _silu_grad(x, sg):
    return sg * (1.0 + x * (1.0 - sg))


def _by_residue(offsets):
    groups = {}
    for off in offsets:
        groups.setdefault(off % 8, []).append(off)
    return [(res, sorted(offs)) for res, offs in sorted(groups.items())]


def _accumulate(ref, val):
    @pl.when(pl.program_id(0) == 0)
    def _():
        ref[...] = jnp.zeros_like(ref)
    ref[...] += val


def _mix_masks():
    row = lax.broadcasted_iota(jnp.int32, (CHUNK, CHUNK), 0)
    col = lax.broadcasted_iota(jnp.int32, (CHUNK, CHUNK), 1)
    return row >= col, row <= col, col < (CHUNK // 2)


def _mix_fwd(vb, ws_ref, bst_ref, mixed_scr, ts):
    tril, _, lo = _mix_masks()
    for j in range(GM_HEADS // 2):
        w0 = _bf(jnp.where(tril, ws_ref[2 * j], 0.0))
        w1 = _bf(jnp.where(tril, ws_ref[2 * j + 1], 0.0))
        bias = jnp.where(lo, bst_ref[:, 2 * j:2 * j + 1], bst_ref[:, 2 * j + 1:2 * j + 2])
        for n in range(ts // CHUNK):
            v = vb[n * CHUNK:(n + 1) * CHUNK, j * 128:(j + 1) * 128]
            mixed_scr[n * CHUNK:(n + 1) * CHUNK, j * 128:(j + 1) * 128] = jnp.where(lo, _dot(w0, v), _dot(w1, v)) + bias


def _exchange(srcs, scatter, name):
    n = len(srcs)

    def body(*refs):
        src_refs, out_refs = refs[:n], refs[n:2 * n]
        send_sems, recv_sems, local_sems = refs[2 * n:]
        x, y, c = lax.axis_index("x"), lax.axis_index("y"), lax.axis_index("c")
        me = 4 * x + 2 * y + c

        def peer_of(mask):
            px = x if not (mask >> 2) & 1 else 1 - x
            py = y if not (mask >> 1) & 1 else 1 - y
            pc = c if not mask & 1 else 1 - c
            return (px, py, pc), 4 * px + 2 * py + pc

        def remote(k, mask):
            peer, pidx = peer_of(mask)
            return pltpu.make_async_remote_copy(
                src_ref=src_refs[k].at[pidx] if scatter else src_refs[k],
                dst_ref=out_refs[k].at[me],
                send_sem=send_sems.at[k, mask - 1], recv_sem=recv_sems.at[k, mask - 1],
                device_id=peer, device_id_type=pl.DeviceIdType.MESH)

        def arrival(k, mask):
            peer, pidx = peer_of(mask)
            return pltpu.make_async_remote_copy(
                src_ref=src_refs[k].at[pidx] if scatter else src_refs[k],
                dst_ref=out_refs[k].at[pidx],
                send_sem=send_sems.at[k, mask - 1], recv_sem=recv_sems.at[k, mask - 1],
                device_id=peer, device_id_type=pl.DeviceIdType.MESH)

        sends, locals_ = [], []
        for k in range(n):
            for mask in range(1, N_DEV):
                cp = remote(k, mask)
                cp.start()
                sends.append(cp)
            lc = pltpu.make_async_copy(src_refs[k].at[me] if scatter else src_refs[k], out_refs[k].at[me],
                                       local_sems.at[k])
            lc.start()
            locals_.append(lc)
        for k in range(n):
            for mask in range(1, N_DEV):
                arrival(k, mask).wait_recv()
        for cp in sends:
            cp.wait_send()
        for lc in locals_:
            lc.wait()

    outs = [SDS((N_DEV,) + tuple(s.shape[1:] if scatter else s.shape), s.dtype) for s in srcs]
    hbm = pl.BlockSpec(memory_space=pl.ANY)
    return pl.pallas_call(
        body, name=name, out_shape=outs, in_specs=[hbm] * n, out_specs=[hbm] * n,
        scratch_shapes=[pltpu.SemaphoreType.DMA((n, N_DEV - 1)), pltpu.SemaphoreType.DMA((n, N_DEV - 1)),
                        pltpu.SemaphoreType.DMA((n,))],
    )(*srcs)


def _peer_of(mask):
    x, y, c = lax.axis_index("x"), lax.axis_index("y"), lax.axis_index("c")
    px = 1 - x if (mask >> 2) & 1 else x
    py = 1 - y if (mask >> 1) & 1 else y
    pc = 1 - c if mask & 1 else c
    return (px, py, pc), 4 * px + 2 * py + pc


ALL_PEERS = tuple(range(1, N_DEV))
OTHER_CHIPS = (2, 4, 6)
SIBLING = 1


def _split_copy(src_ref, land_ref, send_sem, recv_sem, mask, slot, scatter, outgoing):
    x, y, c = lax.axis_index("x"), lax.axis_index("y"), lax.axis_index("c")
    me = 4 * x + 2 * y + c
    peer, pidx = _peer_of(mask)
    return pltpu.make_async_remote_copy(
        src_ref=src_ref.at[pidx] if scatter else src_ref,
        dst_ref=land_ref.at[me if outgoing else pidx],
        send_sem=send_sem.at[slot], recv_sem=recv_sem.at[slot],
        device_id=peer, device_id_type=pl.DeviceIdType.MESH)


_HBM = pl.BlockSpec(memory_space=pltpu.HBM)
_SEM = pl.BlockSpec(memory_space=pltpu.SEMAPHORE)
_EFFECT = pltpu.SideEffectType.DATAFLOW_SIDE_EFFECTING


def _exchange_start(srcs, scatter, name, masks=ALL_PEERS, own_slot=None):
    n = len(srcs)
    lands = [lax.empty((N_DEV,) + tuple(s.shape[1:] if sc else s.shape), s.dtype) for s, sc in zip(srcs, scatter)]
    if own_slot is not None:
        lands = [land if sc else _with_own(land, s, own_slot) for land, s, sc in zip(lands, srcs, scatter)]
    lands = [pltpu.with_memory_space_constraint(land, pltpu.HBM) for land in lands]
    srcs = [pltpu.with_memory_space_constraint(s, pltpu.HBM) for s in srcs]

    def body(*refs):
        src_refs, land_refs = refs[:n], refs[n:2 * n]
        send_sems, recv_sems = refs[2 * n:3 * n], refs[3 * n:4 * n]
        token = refs[-1]
        for k in range(n):
            for slot, mask in enumerate(masks):
                _split_copy(src_refs[k], land_refs[k], send_sems[k], recv_sems[k], mask, slot, scatter[k], True).start()
        token[...] = jnp.zeros_like(token)

    sem = pltpu.SemaphoreType.DMA((len(masks),))
    out = pl.pallas_call(
        body, name=name,
        out_shape=tuple([sem] * (2 * n) + [pltpu.HBM(s.shape, s.dtype) for s in srcs]
                        + [pltpu.HBM(l.shape, l.dtype) for l in lands] + [SDS((8, 128), F32)]),
        in_specs=[_HBM] * (2 * n),
        out_specs=tuple([_SEM] * (2 * n) + [_HBM] * (2 * n) + [pl.BlockSpec(memory_space=pltpu.VMEM)]),
        input_output_aliases={i: 2 * n + i for i in range(2 * n)},
        compiler_params=pltpu.CompilerParams(has_side_effects=_EFFECT),
    )(*srcs, *lands)
    return out[:n], out[n:2 * n], out[2 * n:3 * n], out[3 * n:4 * n], out[-1]


def _exchange_wait(send_sems, recv_sems, srcs_thru, lands_thru, scatter, after, name, masks=ALL_PEERS):
    n = len(srcs_thru)

    def body(*refs):
        src_refs, land_refs = refs[:n], refs[n:2 * n]
        send_refs, recv_refs = refs[2 * n:3 * n], refs[3 * n:4 * n]
        for k in range(n):
            for slot, mask in enumerate(masks):
                args = (src_refs[k], land_refs[k], send_refs[k], recv_refs[k], mask, slot, scatter[k])
                _split_copy(*args, True).wait_send()
                _split_copy(*args, False).wait_recv()

    out = pl.pallas_call(
        body, name=name,
        out_shape=tuple([pltpu.HBM(s.shape, s.dtype) for s in srcs_thru]
                        + [pltpu.HBM(l.shape, l.dtype) for l in lands_thru]),
        in_specs=[_HBM] * (2 * n) + [_SEM] * (2 * n) + [pl.BlockSpec(memory_space=pl.ANY)],
        out_specs=tuple([_HBM] * (2 * n)),
        input_output_aliases={i: i for i in range(2 * n)},
        compiler_params=pltpu.CompilerParams(has_side_effects=_EFFECT),
    )(*srcs_thru, *lands_thru, *send_sems, *recv_sems, after)
    return out[:n], out[n:]


def _wait_and_forward(send_sems, recv_sems, srcs_thru, lands_thru, after, name, masks):
    n = len(lands_thru)

    def body(*refs):
        src_refs, land_refs = refs[:n], refs[n:2 * n]
        send_refs, recv_refs = refs[2 * n:3 * n], refs[3 * n:4 * n]
        outs = refs[4 * n + 1:]
        fsend, frecv, token = outs[2 * n:3 * n], outs[3 * n:4 * n], outs[-1]
        for k in range(n):
            for slot, mask in enumerate(masks):
                args = (src_refs[k], land_refs[k], send_refs[k], recv_refs[k], mask, slot, False)
                _split_copy(*args, True).wait_send()
                _split_copy(*args, False).wait_recv()
        sibling, _ = _peer_of(SIBLING)
        for k in range(n):
            for slot, mask in enumerate(OTHER_CHIPS):
                _, mine = _peer_of(mask)
                pltpu.make_async_remote_copy(
                    src_ref=land_refs[k].at[mine], dst_ref=land_refs[k].at[mine], send_sem=fsend[k].at[slot],
                    recv_sem=frecv[k].at[slot], device_id=sibling, device_id_type=pl.DeviceIdType.MESH).start()
        token[...] = jnp.zeros_like(token)

    sem = pltpu.SemaphoreType.DMA((len(OTHER_CHIPS),))
    out = pl.pallas_call(
        body, name=name,
        out_shape=tuple([pltpu.HBM(s.shape, s.dtype) for s in srcs_thru] + [pltpu.HBM(l.shape, l.dtype) for l in lands_thru]
                        + [sem] * (2 * n) + [SDS((8, 128), F32)]),
        in_specs=[_HBM] * (2 * n) + [_SEM] * (2 * n) + [pl.BlockSpec(memory_space=pl.ANY)],
        out_specs=tuple([_HBM] * (2 * n) + [_SEM] * (2 * n) + [pl.BlockSpec(memory_space=pltpu.VMEM)]),
        input_output_aliases={i: i for i in range(2 * n)},
        compiler_params=pltpu.CompilerParams(has_side_effects=_EFFECT),
    )(*srcs_thru, *lands_thru, *send_sems, *recv_sems, after)
    return out[2 * n:3 * n], out[3 * n:4 * n], out[n:2 * n], out[-1]


def _forward_wait(send_sems, recv_sems, lands, after, name):
    n = len(lands)

    def body(*refs):
        land_refs, send_refs, recv_refs = refs[:n], refs[n:2 * n], refs[2 * n:3 * n]
        sibling, _ = _peer_of(SIBLING)
        for k in range(n):
            for slot, mask in enumerate(OTHER_CHIPS):
                _, mine = _peer_of(mask)
                _, theirs = _peer_of(mask | SIBLING)
                for block, wait in ((mine, "wait_send"), (theirs, "wait_recv")):
                    getattr(pltpu.make_async_remote_copy(
                        src_ref=land_refs[k].at[block], dst_ref=land_refs[k].at[block],
                        send_sem=send_refs[k].at[slot], recv_sem=recv_refs[k].at[slot], device_id=sibling,
                        device_id_type=pl.DeviceIdType.MESH), wait)()

    return pl.pallas_call(
        body, name=name, out_shape=tuple(pltpu.HBM(l.shape, l.dtype) for l in lands),
        in_specs=[_HBM] * n + [_SEM] * (2 * n) + [pl.BlockSpec(memory_space=pl.ANY)],
        out_specs=tuple([_HBM] * n), input_output_aliases={i: i for i in range(n)},
        compiler_params=pltpu.CompilerParams(has_side_effects=_EFFECT),
    )(*lands, *send_sems, *recv_sems, after)


def _with_own(landed, own, me):
    return lax.dynamic_update_slice_in_dim(landed, own[None], me, axis=0)


def _kv_proj(mem, g_mem, wkv3):
    m = mem.shape[0]

    def body(mem_ref, g_ref, w_ref, kv_ref, mn_ref):
        y, _, _ = _rms_fwd(mem_ref[...], g_ref[...])
        yb = _bf(y)
        mn_ref[...] = yb
        for b in range(N_DEV):
            kv_ref[:, 256 * b:256 * (b + 1)] = _dot(yb, w_ref[b])

    return pl.pallas_call(body, name="kv_proj", out_shape=(SDS((m, 2048), F32), SDS(mem.shape, BF16)),
                          compiler_params=_cparams())(mem, g_mem, wkv3)


def _fwd_in(x, g_mix, w_in3, b_in, ts):
    s, d = x.shape

    def body(x_ref, g_ref, w_ref, b_ref, z_ref, a_ref):
        hn, _, _ = _rms_fwd(x_ref[...], g_ref[...])
        hb = _bf(hn)
        for b in range(N_DEV):
            z_ref[:, 256 * b:256 * (b + 1)] = _dot(hb, w_ref[b]) + b_ref[:, 256 * b:256 * (b + 1)]
        a_ref[...] = z_ref[:, 0:512] * jax.nn.sigmoid(z_ref[:, 512:1024])

    return pl.pallas_call(
        body, name="fwd_in", grid=(s // ts,),
        in_specs=[_row(ts, d), _const(g_mix.shape), _const(w_in3.shape), _const(b_in.shape)],
        out_specs=(_row(ts, 2048), _row(ts, 512)),
        out_shape=(SDS((s, 2048), F32), SDS((s, 512), F32)),
        compiler_params=_cparams("arbitrary"))(x, g_mix, w_in3, b_in)


def _conv_fwd(a, w, b):
    s, cw = a.shape
    rc = 256 if s % 256 == 0 else 128

    def body(a_ref, w_ref, b_ref, c_ref, pad):
        pad[0:CONV_PAD, :] = jnp.zeros((CONV_PAD, 128), F32)
        pad[CONV_PAD:, :] = a_ref[...]

        def chunk(i, carry):
            r0 = pl.multiple_of(i * rc, rc)
            acc = jnp.zeros((rc, 128), F32) + b_ref[...]
            for res, offs in _by_residue(range(CONV_PAD - CONV_K + 1, CONV_PAD + 1)):
                shifted = pad[pl.ds(r0 + res, rc + offs[-1] - res), :]
                for off in offs:
                    k = off - (CONV_PAD - CONV_K + 1)
                    acc = acc + w_ref[k:k + 1, :] * shifted[off - res:off - res + rc, :]
            c_ref[pl.ds(r0, rc), :] = acc
            return carry

        lax.fori_loop(0, s // rc, chunk, 0)

    blk = lambda r: pl.BlockSpec((r, 128), lambda j: (0, j))
    return pl.pallas_call(
        body, name="conv_fwd", grid=(cw // 128,),
        in_specs=[blk(s), blk(CONV_K), blk(1)], out_specs=blk(s), out_shape=SDS((s, cw), F32),
        scratch_shapes=[pltpu.VMEM((s + CONV_PAD, 128), F32)],
        compiler_params=_cparams("arbitrary"))(a, w, b)


def _fwd_out(x, c, z, cln_g, cln_b, gln_g, gln_b, ws, bst, w_out, ts):
    s, d = x.shape

    def body(x_ref, c_ref, zuv_ref, clg, clb, glg, glb, ws_ref, bst_ref, wo_ref, h1_ref, mixed_scr):
        cl, _, _ = _ln_fwd(c_ref[...], clg[...], clb[...])
        co = cl * jax.nn.sigmoid(cl)
        u, _ = _gelu(zuv_ref[:, 0:512])
        vg, _ = _gelu(zuv_ref[:, 512:1024])
        vln, _, _ = _ln_fwd(vg, glg[...], glb[...])
        _mix_fwd(_bf(vln), ws_ref, bst_ref, mixed_scr, ts)
        gm = u * mixed_scr[...]
        h1_ref[...] = x_ref[...] + _dot(_bf(co), wo_ref[0:512, :]) + _dot(_bf(gm), wo_ref[512:1024, :])

    return pl.pallas_call(
        body, name="fwd_out", grid=(s // ts,),
        in_specs=[_row(ts, d), _row(ts, 512), _row(ts, 1024, 1), _const(cln_g.shape), _const(cln_b.shape),
                  _const(gln_g.shape), _const(gln_b.shape), _const(ws.shape), _const(bst.shape), _const(w_out.shape)],
        out_specs=_row(ts, d), out_shape=SDS((s, d), F32),
        scratch_shapes=[pltpu.VMEM((ts, 512), F32)],
        compiler_params=_cparams("arbitrary"))(x, c, z, cln_g, cln_b, gln_g, gln_b, ws, bst, w_out)


def _softmax_rows(sc):
    m = jnp.max(sc, axis=-1, keepdims=True)
    e = jnp.exp(sc - m)
    return e / jnp.sum(e, axis=-1, keepdims=True)


def _fwd_xa(h1, g_xa, wq, kv, wo, ts):
    s, d = h1.shape
    scale = XA_DH ** -0.5

    def body(h_ref, g_ref, wq_ref, kv_ref, wo_ref, h2_ref, o_scr):
        hn, _, _ = _rms_fwd(h_ref[...], g_ref[...])
        q = _dot(_bf(hn), wq_ref[...])
        for h in range(XA_HEADS):
            qh = _bf(q[:, XA_DH * h:XA_DH * (h + 1)])
            kh = _bf(kv_ref[:, XA_DH * h:XA_DH * (h + 1)])
            vh = _bf(kv_ref[:, d + XA_DH * h:d + XA_DH * (h + 1)])
            p = _softmax_rows(_dot(qh, kh, NT) * scale)
            o_scr[:, XA_DH * h:XA_DH * (h + 1)] = _dot(_bf(p), vh)
        h2_ref[...] = h_ref[...] + _dot(_bf(o_scr[...]), wo_ref[...])

    return pl.pallas_call(
        body, name="fwd_xa", grid=(s // ts,),
        in_specs=[_row(ts, d), _const(g_xa.shape), _const(wq.shape), _const(kv.shape), _const(wo.shape)],
        out_specs=_row(ts, d), out_shape=SDS((s, d), F32),
        scratch_shapes=[pltpu.VMEM((ts, d), F32)],
        compiler_params=_cparams("arbitrary"))(h1, g_xa, wq, kv, wo)


def _fwd_ffn(h2, g_ffn, wgut, wdown, g_final, target, ts):
    s, d = h2.shape
    hid = wdown.shape[0]
    hc = hid // 2

    def body(h_ref, g_ref, wgu_ref, wd_ref, gf_ref, t_ref, dh3_ref, dh3b_ref, gu_ref, hn_ref, loss_ref, dgf_ref):
        hn, _, _ = _rms_fwd(h_ref[...], g_ref[...])
        hb = _bf(hn)
        hn_ref[...] = hb
        h3 = h_ref[...]
        for n in range(2):
            g = _dot(hb, wgu_ref[hc * n:hc * (n + 1), :], NT)
            u = _dot(hb, wgu_ref[hid + hc * n:hid + hc * (n + 1), :], NT)
            gu_ref[:, hc * n:hc * (n + 1)] = g
            gu_ref[:, hid + hc * n:hid + hc * (n + 1)] = u
            act = g * jax.nn.sigmoid(g) * u
            h3 = h3 + _dot(_bf(act), wd_ref[hc * n:hc * (n + 1), :])
        y, xh, r = _rms_fwd(h3, gf_ref[...])
        diff = y - t_ref[...]
        part = 0.5 * jnp.sum(jnp.mean(diff * diff, axis=-1, keepdims=True), axis=0, keepdims=True)
        _accumulate(loss_ref, jnp.zeros(loss_ref.shape, F32) + part)
        dh3, dgf = _rms_bwd(diff * (1.0 / d), xh, r, gf_ref[...])
        dh3_ref[...] = dh3
        dh3b_ref[...] = _bf(dh3)
        _accumulate(dgf_ref, dgf)

    return pl.pallas_call(
        body, name="fwd_ffn", grid=(s // ts,),
        in_specs=[_row(ts, d), _const(g_ffn.shape), _const(wgut.shape), _const(wdown.shape), _const(g_final.shape),
                  _row(ts, d)],
        out_specs=(_row(ts, d), _row(ts, d), _row(ts, 2 * hid), _row(ts, d), _acc((1, 128)), _acc((1, d))),
        out_shape=(SDS((s, d), F32), SDS((s, d), BF16), SDS((s, 2 * hid), F32), SDS((s, d), BF16), SDS((1, 128), F32),
                   SDS((1, d), F32)),
        compiler_params=_cparams("arbitrary"))(h2, g_ffn, wgut, wdown, g_final, target)


def _bwd_ffn(h2, dh3, gu, g_ffn, wgut, wdown, ts):
    s, d = h2.shape
    hid = wdown.shape[0]
    hc = hid // 2

    def body(h_ref, dh3_ref, gu_ref, g_ref, wgu_ref, wd_ref, dh2_ref, act_ref, dgu_ref, dg_ref):
        _, xh, r = _rms_fwd(h_ref[...], g_ref[...])
        db = _bf(dh3_ref[...])
        dhn = jnp.zeros((ts, d), F32)
        for n in range(2):
            wg = wgu_ref[hc * n:hc * (n + 1), :]
            wu = wgu_ref[hid + hc * n:hid + hc * (n + 1), :]
            g = gu_ref[:, hc * n:hc * (n + 1)]
            u = gu_ref[:, hid + hc * n:hid + hc * (n + 1)]
            sg = jax.nn.sigmoid(g)
            sl = g * sg
            act_ref[:, hc * n:hc * (n + 1)] = _bf(sl * u)
            dact = _dot(db, wd_ref[hc * n:hc * (n + 1), :], NT)
            dgb = _bf(dact * u * _silu_grad(g, sg))
            dub = _bf(dact * sl)
            dgu_ref[:, hc * n:hc * (n + 1)] = dgb
            dgu_ref[:, hid + hc * n:hid + hc * (n + 1)] = dub
            dhn = dhn + _dot(dgb, wg) + _dot(dub, wu)
        dx, dg = _rms_bwd(dhn, xh, r, g_ref[...])
        dh2_ref[...] = dh3_ref[...] + dx
        _accumulate(dg_ref, dg)

    return pl.pallas_call(
        body, name="bwd_ffn", grid=(s // ts,),
        in_specs=[_row(ts, d), _row(ts, d), _row(ts, 2 * hid), _const(g_ffn.shape), _const(wgut.shape),
                  _const(wdown.shape)],
        out_specs=(_row(ts, d), _row(ts, hid), _row(ts, 2 * hid), _acc((1, d))),
        out_shape=(SDS((s, d), F32), SDS((s, hid), BF16), SDS((s, 2 * hid), BF16), SDS((1, d), F32)),
        compiler_params=_cparams("arbitrary"))(h2, dh3, gu, g_ffn, wgut, wdown)


def _bwd_xa(h1, dh2, g_xa, wq, wo, kv, ts):
    s, d = h1.shape
    scale = XA_DH ** -0.5

    def body(h_ref, dh2_ref, g_ref, wq_ref, wo_ref, kv_ref, dh1_ref, dwq_ref, dwo_ref, dkv_ref, dg_ref,
             dq_scr, o_scr, accq, acco):
        hn, xh, r = _rms_fwd(h_ref[...], g_ref[...])
        hb = _bf(hn)
        q = _dot(hb, wq_ref[...])
        dh2b = _bf(dh2_ref[...])
        do = _dot(dh2b, wo_ref[...], NT)

        @pl.when(pl.program_id(0) == 0)
        def _():
            dkv_ref[...] = jnp.zeros_like(dkv_ref)
            accq[...] = jnp.zeros_like(accq)
            acco[...] = jnp.zeros_like(acco)

        for h in range(XA_HEADS):
            lo, hi = XA_DH * h, XA_DH * (h + 1)
            qh = _bf(q[:, lo:hi])
            kh = _bf(kv_ref[:, lo:hi])
            vh = _bf(kv_ref[:, d + lo:d + hi])
            p = _softmax_rows(_dot(qh, kh, NT) * scale)
            pb = _bf(p)
            o_scr[:, lo:hi] = _bf(_dot(pb, vh))
            doh = _bf(do[:, lo:hi])
            dp = _dot(doh, vh, NT)
            ds = p * (dp - jnp.sum(p * dp, axis=-1, keepdims=True)) * scale
            dsb = _bf(ds)
            dq_scr[:, lo:hi] = _dot(dsb, kh)
            dkv_ref[:, lo:hi] += _dot(dsb, qh, TN)
            dkv_ref[:, d + lo:d + hi] += _dot(pb, doh, TN)
        dqb = _bf(dq_scr[...])
        accq[...] += _dot(hb, dqb, TN)
        acco[...] += _dot(o_scr[...], dh2b, TN)
        dx, dg = _rms_bwd(_dot(dqb, wq_ref[...], NT), xh, r, g_ref[...])
        dh1_ref[...] = dh2_ref[...] + dx
        _accumulate(dg_ref, dg)

        @pl.when(pl.program_id(0) == pl.num_programs(0) - 1)
        def _():
            dwq_ref[...] = _bf(accq[...])
            dwo_ref[...] = _bf(acco[...])

    return pl.pallas_call(
        body, name="bwd_xa", grid=(s // ts,),
        in_specs=[_row(ts, d), _row(ts, d), _const(g_xa.shape), _const(wq.shape), _const(wo.shape), _const(kv.shape)],
        out_specs=(_row(ts, d), _acc((d, d)), _acc((d, d)), _acc(kv.shape), _acc((1, d))),
        out_shape=(SDS((s, d), F32), SDS((d, d), BF16), SDS((d, d), BF16), SDS(kv.shape, F32), SDS((1, d), F32)),
        scratch_shapes=[pltpu.VMEM((ts, d), F32), pltpu.VMEM((ts, d), BF16), pltpu.VMEM((d, d), F32),
                        pltpu.VMEM((d, d), F32)],
        compiler_params=_cparams("arbitrary"))(h1, dh2, g_xa, wq, wo, kv)


def _bwd_kv(dkv, mn, mem, g_mem, wkv3):
    d = mem.shape[1]

    def body(dkv_ref, mn_ref, mem_ref, g_ref, w_ref, dw_ref, dg_ref):
        dkvb = _bf(dkv_ref[...])
        dmn = jnp.zeros(mem_ref.shape, F32)
        for b in range(N_DEV):
            blk = dkvb[:, 256 * b:256 * (b + 1)]
            dmn = dmn + _dot(blk, w_ref[b], NT)
            dw_ref[b] = _bf(_dot(mn_ref[...], blk, TN))
        _, xh, r = _rms_fwd(mem_ref[...], g_ref[...])
        _, dg = _rms_bwd(dmn, xh, r, g_ref[...])
        dg_ref[...] = dg

    return pl.pallas_call(body, name="bwd_kv", out_shape=(SDS(wkv3.shape, BF16), SDS((1, d), F32)),
                          compiler_params=_cparams())(dkv, mn, mem, g_mem, wkv3)


def _bwd_out(dh1, c, z, cln_g, cln_b, gln_g, gln_b, ws, wst, bst, w_out, ts):
    s, d = dh1.shape
    nh = GM_HEADS

    def body(dh1_ref, c_ref, zuv_ref, clg, clb, glg, glb, ws_ref, wst_ref, bst_ref, wo_ref,
             dwo_ref, dc_ref, dzuv_ref, dws_ref, dbst_ref, dclg_ref, dclb_ref, dglg_ref, dglb_ref, dbin_ref,
             mixed_scr, dv_scr, acc):
        cl, chat, crs = _ln_fwd(c_ref[...], clg[...], clb[...])
        sg = jax.nn.sigmoid(cl)
        zu = zuv_ref[:, 0:512]
        zv = zuv_ref[:, 512:1024]
        u, tu = _gelu(zu)
        vg, tv = _gelu(zv)
        vln, vhat, vrs = _ln_fwd(vg, glg[...], glb[...])
        vb = _bf(vln)
        _mix_fwd(vb, ws_ref, bst_ref, mixed_scr, ts)
        mixed = mixed_scr[...]
        dh1b = _bf(dh1_ref[...])
        dcat = _dot(dh1b, wo_ref[...], NT)
        dgm = dcat[:, 512:1024]
        dc, dclg, dclb = _ln_bwd(dcat[:, 0:512] * _silu_grad(cl, sg), chat, crs, clg[...])
        dc_ref[...] = dc
        dzu = dgm * mixed * _gelu_grad(zu, tu)
        dm = dgm * u

        @pl.when(pl.program_id(0) == 0)
        def _():
            dws_ref[...] = jnp.zeros_like(dws_ref)
            dbst_ref[...] = jnp.zeros_like(dbst_ref)
            acc[...] = jnp.zeros_like(acc)

        acc[0:512, :] += _dot(_bf(cl * sg), dh1b, TN)
        acc[512:1024, :] += _dot(_bf(u * mixed), dh1b, TN)

        @pl.when(pl.program_id(0) == pl.num_programs(0) - 1)
        def _():
            dwo_ref[...] = _bf(acc[...])

        tril, triu, lo = _mix_masks()
        head = lax.broadcasted_iota(jnp.int32, (1, nh), 1)
        for j in range(nh // 2):
            w0t = _bf(jnp.where(triu, wst_ref[2 * j], 0.0))
            w1t = _bf(jnp.where(triu, wst_ref[2 * j + 1], 0.0))
            for n in range(ts // CHUNK):
                rows = slice(n * CHUNK, (n + 1) * CHUNK)
                lanes = slice(j * 128, (j + 1) * 128)
                dmc = dm[rows, lanes]
                dmb = _bf(dmc)
                dv_scr[rows, lanes] = jnp.where(lo, _dot(w0t, dmb), _dot(w1t, dmb))
                vc = vb[rows, lanes]
                d0 = jnp.where(lo, dmc, 0.0)
                d1 = dmc - d0
                dws_ref[2 * j] += jnp.where(tril, _dot(_bf(d0), vc, NT), 0.0)
                dws_ref[2 * j + 1] += jnp.where(tril, _dot(_bf(d1), vc, NT), 0.0)
                dbst_ref[...] += (jnp.sum(d0, axis=1, keepdims=True) * (head == 2 * j).astype(F32)
                                  + jnp.sum(d1, axis=1, keepdims=True) * (head == 2 * j + 1).astype(F32))
        dvg, dglg, dglb = _ln_bwd(dv_scr[...], vhat, vrs, glg[...])
        dzv = dvg * _gelu_grad(zv, tv)
        dzuv_ref[:, 0:512] = _bf(dzu)
        dzuv_ref[:, 512:1024] = _bf(dzv)
        _accumulate(dclg_ref, dclg)
        _accumulate(dclb_ref, dclb)
        _accumulate(dglg_ref, dglg)
        _accumulate(dglb_ref, dglb)
        _accumulate(dbin_ref, jnp.concatenate([jnp.sum(dzu, axis=0, keepdims=True),
                                               jnp.sum(dzv, axis=0, keepdims=True)], axis=1))

    vec = (1, 512)
    return pl.pallas_call(
        body, name="bwd_out", grid=(s // ts,),
        in_specs=[_row(ts, d), _row(ts, 512), _row(ts, 1024, 1), _const(cln_g.shape), _const(cln_b.shape),
                  _const(gln_g.shape), _const(gln_b.shape), _const(ws.shape), _const(wst.shape), _const(bst.shape),
                  _const(w_out.shape)],
        out_specs=(_acc((d, d)), _row(ts, 512), _row(ts, 1024), _acc(ws.shape), _acc(bst.shape), _acc(vec), _acc(vec),
                   _acc(vec), _acc(vec), _acc((1, 1024))),
        out_shape=(SDS((d, d), BF16), SDS((s, 512), F32), SDS((s, 1024), BF16), SDS(ws.shape, F32),
                   SDS(bst.shape, F32), SDS(vec, F32), SDS(vec, F32), SDS(vec, F32), SDS(vec, F32), SDS((1, 1024), F32)),
        scratch_shapes=[pltpu.VMEM((ts, 512), F32), pltpu.VMEM((ts, 512), F32), pltpu.VMEM((d, d), F32)],
        compiler_params=_cparams("arbitrary"))(dh1, c, z, cln_g, cln_b, gln_g, gln_b, ws, wst, bst, w_out)


def _conv_bwd(dc, a, w):
    s, cw = a.shape
    rc = 256 if s % 256 == 0 else 128

    def body(dc_ref, a_ref, w_ref, da_ref, dw_ref, db_ref, pad_a, pad_d, part):
        pad_a[0:CONV_PAD, :] = jnp.zeros((CONV_PAD, 128), F32)
        pad_a[CONV_PAD:, :] = a_ref[...]
        pad_d[0:s, :] = dc_ref[...]
        pad_d[s:, :] = jnp.zeros((CONV_PAD, 128), F32)
        part[...] = jnp.zeros_like(part)

        def rows8(v):
            return jnp.sum(v.reshape(rc // 8, 8, 128), axis=0)

        def chunk(i, carry):
            r0 = pl.multiple_of(i * rc, rc)
            dcc = pad_d[pl.ds(r0, rc), :]
            acc = jnp.zeros((rc, 128), F32)
            for res, offs in _by_residue(range(0, CONV_K)):
                shifted = pad_d[pl.ds(r0 + res, rc + offs[-1] - res), :]
                for off in offs:
                    k = CONV_K - 1 - off
                    acc = acc + w_ref[k:k + 1, :] * shifted[off - res:off - res + rc, :]
            da_ref[pl.ds(r0, rc), :] = acc
            for res, offs in _by_residue(range(CONV_PAD - CONV_K + 1, CONV_PAD + 1)):
                shifted = pad_a[pl.ds(r0 + res, rc + offs[-1] - res), :]
                for off in offs:
                    k = off - (CONV_PAD - CONV_K + 1)
                    part[8 * k:8 * k + 8, :] += rows8(dcc * shifted[off - res:off - res + rc, :])
            part[8 * CONV_PAD:, :] += rows8(dcc)
            return carry

        lax.fori_loop(0, s // rc, chunk, 0)
        sums = jnp.sum(part[...].reshape(CONV_PAD + 1, 8, 128), axis=1)
        dw_ref[...] = sums[0:CONV_PAD, :]
        db_ref[...] = sums[CONV_PAD:, :]

    blk = lambda r: pl.BlockSpec((r, 128), lambda j: (0, j))
    return pl.pallas_call(
        body, name="conv_bwd", grid=(cw // 128,),
        in_specs=[blk(s), blk(s), blk(CONV_K)], out_specs=(blk(s), blk(CONV_PAD), blk(1)),
        out_shape=(SDS((s, cw), F32), SDS((CONV_PAD, cw), F32), SDS((1, cw), F32)),
        scratch_shapes=[pltpu.VMEM((s + CONV_PAD, 128), F32), pltpu.VMEM((s + CONV_PAD, 128), F32),
                        pltpu.VMEM((8 * (CONV_PAD + 1), 128), F32)],
        compiler_params=_cparams("arbitrary"))(dc, a, w)


def _bwd_in(x, dh1, da, z, dzuv, g_mix, w_in3, ts):
    s, d = x.shape

    def body(x_ref, dh1_ref, da_ref, zag_ref, dzuv_ref, g_ref, w_ref, dx_ref, dw_ref, dbin_ref, dg_ref, dz_ref, acc):
        za = zag_ref[:, 0:512]
        sg = jax.nn.sigmoid(zag_ref[:, 512:1024])
        da_ = da_ref[...]
        dza = da_ * sg
        dzg = da_ * za * sg * (1.0 - sg)
        dz_ref[:, 0:512] = _bf(dza)
        dz_ref[:, 512:1024] = _bf(dzg)
        dz_ref[:, 1024:2048] = dzuv_ref[...]
        dhn = jnp.zeros((ts, d), F32)
        for b in range(N_DEV):
            dhn = dhn + _dot(dz_ref[:, 256 * b:256 * (b + 1)], w_ref[b], NT)
        hn, xh, r = _rms_fwd(x_ref[...], g_ref[...])

        @pl.when(pl.program_id(0) == 0)
        def _():
            acc[...] = jnp.zeros_like(acc)

        acc[...] += _dot(_bf(hn), dz_ref[...], TN)

        @pl.when(pl.program_id(0) == pl.num_programs(0) - 1)
        def _():
            for b in range(N_DEV):
                dw_ref[b] = _bf(acc[:, 256 * b:256 * (b + 1)])

        dxn, dg = _rms_bwd(dhn, xh, r, g_ref[...])
        dx_ref[...] = dh1_ref[...] + dxn
        _accumulate(dg_ref, dg)
        _accumulate(dbin_ref, jnp.concatenate([jnp.sum(dza, axis=0, keepdims=True),
                                               jnp.sum(dzg, axis=0, keepdims=True)], axis=1))

    return pl.pallas_call(
        body, name="bwd_in", grid=(s // ts,),
        in_specs=[_row(ts, d), _row(ts, d), _row(ts, 512), _row(ts, 1024, 0), _row(ts, 1024), _const(g_mix.shape),
                  _const(w_in3.shape)],
        out_specs=(_row(ts, d), _acc(w_in3.shape), _acc((1, 1024)), _acc((1, d))),
        out_shape=(SDS((s, d), F32), SDS(w_in3.shape, BF16), SDS((1, 1024), F32), SDS((1, d), F32)),
        scratch_shapes=[pltpu.VMEM((ts, 2048), BF16), pltpu.VMEM((d, 2048), F32)],
        compiler_params=_cparams("arbitrary"))(x, dh1, da, z, dzuv, g_mix, w_in3)


def _tn_matmul(a, b, tm, name):
    s, m = a.shape
    n = b.shape[1]
    ts = min(s, 512)
    n_s = s // ts

    def body(a_ref, b_ref, o_ref, acc):
        k = pl.program_id(1)

        @pl.when(k == 0)
        def _():
            acc[...] = jnp.zeros_like(acc)

        acc[...] += _dot(a_ref[...], b_ref[...], TN)

        @pl.when(k == n_s - 1)
        def _():
            o_ref[...] = _bf(acc[...])

    return pl.pallas_call(
        body, name=name, grid=(m // tm, n_s),
        in_specs=[pl.BlockSpec((ts, tm), lambda i, k: (k, i)), pl.BlockSpec((ts, n), lambda i, k: (k, 0))],
        out_specs=pl.BlockSpec((tm, n), lambda i, k: (i, 0)), out_shape=SDS((m, n), BF16),
        scratch_shapes=[pltpu.VMEM((tm, n), F32)],
        compiler_params=_cparams("parallel", "arbitrary"))(a, b)


def _row_tile(r, cands):
    for cand in cands:
        if r % cand == 0:
            return cand
    return r


def _sum_in_device_order(me_ref, land_ref, own_ref):
    acc = None
    for dev in range(N_DEV):
        part = jnp.where(me_ref[0] == dev, own_ref[0], land_ref[dev]).astype(F32)
        acc = part if acc is None else acc + part
    return acc


def _sum8(land, own, me, name):
    _, r, c = land.shape
    tr = _row_tile(r, (512, 256, 352, 128, 8))
    own3 = own if own.ndim == 3 else own[None]
    own_map = (lambda i, me_ref: (me_ref[0], i, 0)) if own.ndim == 3 else (lambda i, me_ref: (0, i, 0))

    def body(me_ref, land_ref, own_ref, o_ref):
        o_ref[...] = _sum_in_device_order(me_ref, land_ref, own_ref)

    return pl.pallas_call(
        body, name=name, out_shape=SDS((r, c), F32),
        grid_spec=pltpu.PrefetchScalarGridSpec(
            num_scalar_prefetch=1, grid=(r // tr,),
            in_specs=[pl.BlockSpec((N_DEV, tr, c), lambda i, me_ref: (0, i, 0)), pl.BlockSpec((1, tr, c), own_map)],
            out_specs=pl.BlockSpec((tr, c), lambda i, me_ref: (i, 0))),
        compiler_params=_cparams("parallel"))(me, land, own3)


def _adam_update(w, g, m, v):
    m2 = ADAM_B1 * m + (1.0 - ADAM_B1) * g
    v2 = ADAM_B2 * v + (1.0 - ADAM_B2) * (g * g)
    m_hat = m2 / (1.0 - ADAM_B1 ** ADAM_STEP)
    v_hat = v2 / (1.0 - ADAM_B2 ** ADAM_STEP)
    return -ADAM_LR * (m_hat / (jnp.sqrt(v_hat) + ADAM_EPS) + ADAM_WD * w), m2, v2


def _sum_adamw(land, parts, w, m, v, me, name):
    r, c = w.shape
    tr = _row_tile(r, (256, 128, 176, 8))

    def body(me_ref, land_ref, own_ref, w_ref, m_ref, v_ref, g_ref, d_ref, m2_ref, v2_ref):
        g = _sum_in_device_order(me_ref, land_ref, own_ref)
        g_ref[...] = g
        d_ref[...], m2_ref[...], v2_ref[...] = _adam_update(w_ref[...], g, m_ref[...], v_ref[...])

    blk = pl.BlockSpec((tr, c), lambda i, me_ref: (i, 0))
    return pl.pallas_call(
        body, name=name, out_shape=(SDS((r, c), F32),) * 4,
        grid_spec=pltpu.PrefetchScalarGridSpec(
            num_scalar_prefetch=1, grid=(r // tr,),
            in_specs=[pl.BlockSpec((N_DEV, tr, c), lambda i, me_ref: (0, i, 0)),
                      pl.BlockSpec((1, tr, c), lambda i, me_ref: (me_ref[0], i, 0)), blk, blk, blk],
            out_specs=(blk,) * 4),
        compiler_params=_cparams("parallel"))(me, land, parts, w, m, v)


def _adamw_many(ws, gs, ms, vs, name):
    n = len(ws)

    def body(*refs):
        ins, outs = refs[:4 * n], refs[4 * n:]
        for i in range(n):
            w, g, m, v = (ins[j * n + i][...] for j in range(4))
            outs[i][...], outs[n + i][...], outs[2 * n + i][...] = _adam_update(w, g, m, v)

    shapes = [SDS(w.shape, F32) for w in ws]
    out = pl.pallas_call(body, name=name, out_shape=shapes * 3, compiler_params=_cparams())(*ws, *gs, *ms, *vs)
    return out[:n], out[n:2 * n], out[2 * n:]


def _pack(arrs):
    flat = jnp.concatenate([a.reshape(-1) for a in arrs])
    pad = (-flat.shape[0]) % (128 * 128)
    return jnp.pad(flat, (0, pad)).reshape(-1, 128)


def _unpack(packed, shapes):
    flat = packed.reshape(-1)
    out, off = [], 0
    for shp in shapes:
        size = 1
        for dim in shp:
            size *= dim
        out.append(flat[off:off + size].reshape(shp))
        off += size
    return out


SMALL = ("norm_mix_g", "b_in", "conv_w", "conv_b", "conv_ln_g", "conv_ln_b", "gm_ln_g", "gm_ln_b", "gm_w_s", "gm_b_s",
         "norm_xa_g", "mem_norm_g", "norm_ffn_g", "final_norm_g")
BIG = ("w_in", "w_out", "xa_wq", "xa_wkv", "xa_wo", "ffn_w_gate_up", "ffn_w_down")
WEIGHTS = ("norm_mix_g", "w_in", "b_in", "conv_w", "conv_b", "conv_ln_g", "conv_ln_b", "gm_ln_g", "gm_ln_b", "gm_w_s",
           "gm_b_s", "w_out", "norm_xa_g", "mem_norm_g", "xa_wq", "xa_wkv", "xa_wo", "norm_ffn_g", "ffn_w_gate_up",
           "ffn_w_down", "final_norm_g")


def kernel(x, mem, norm_mix_g, w_in, b_in, conv_w, conv_b, conv_ln_g, conv_ln_b, gm_ln_g, gm_ln_b, gm_w_s, gm_b_s, w_out, norm_xa_g, mem_norm_g, xa_wq, xa_wkv, xa_wo, norm_ffn_g, ffn_w_gate_up, ffn_w_down, final_norm_g, loss_target, m_norm_mix_g, m_w_in, m_b_in, m_conv_w, m_conv_b, m_conv_ln_g, m_conv_ln_b, m_gm_ln_g, m_gm_ln_b, m_gm_w_s, m_gm_b_s, m_w_out, m_norm_xa_g, m_mem_norm_g, m_xa_wq, m_xa_wkv, m_xa_wo, m_norm_ffn_g, m_ffn_w_gate_up, m_ffn_w_down, m_final_norm_g, v_norm_mix_g, v_w_in, v_b_in, v_conv_w, v_conv_b, v_conv_ln_g, v_conv_ln_b, v_gm_ln_g, v_gm_ln_b, v_gm_w_s, v_gm_b_s, v_w_out, v_norm_xa_g, v_mem_norm_g, v_xa_wq, v_xa_wkv, v_xa_wo, v_norm_ffn_g, v_ffn_w_gate_up, v_ffn_w_down, v_final_norm_g):
    w = dict(norm_mix_g=norm_mix_g, w_in=w_in, b_in=b_in, conv_w=conv_w, conv_b=conv_b, conv_ln_g=conv_ln_g,
             conv_ln_b=conv_ln_b, gm_ln_g=gm_ln_g, gm_ln_b=gm_ln_b, gm_w_s=gm_w_s, gm_b_s=gm_b_s, w_out=w_out,
             norm_xa_g=norm_xa_g, mem_norm_g=mem_norm_g, xa_wq=xa_wq, xa_wkv=xa_wkv, xa_wo=xa_wo,
             norm_ffn_g=norm_ffn_g, ffn_w_gate_up=ffn_w_gate_up, ffn_w_down=ffn_w_down, final_norm_g=final_norm_g)
    mom = dict(norm_mix_g=m_norm_mix_g, w_in=m_w_in, b_in=m_b_in, conv_w=m_conv_w, conv_b=m_conv_b,
               conv_ln_g=m_conv_ln_g, conv_ln_b=m_conv_ln_b, gm_ln_g=m_gm_ln_g, gm_ln_b=m_gm_ln_b, gm_w_s=m_gm_w_s,
               gm_b_s=m_gm_b_s, w_out=m_w_out, norm_xa_g=m_norm_xa_g, mem_norm_g=m_mem_norm_g, xa_wq=m_xa_wq,
               xa_wkv=m_xa_wkv, xa_wo=m_xa_wo, norm_ffn_g=m_norm_ffn_g, ffn_w_gate_up=m_ffn_w_gate_up,
               ffn_w_down=m_ffn_w_down, final_norm_g=m_final_norm_g)
    var = dict(norm_mix_g=v_norm_mix_g, w_in=v_w_in, b_in=v_b_in, conv_w=v_conv_w, conv_b=v_conv_b,
               conv_ln_g=v_conv_ln_g, conv_ln_b=v_conv_ln_b, gm_ln_g=v_gm_ln_g, gm_ln_b=v_gm_ln_b, gm_w_s=v_gm_w_s,
               gm_b_s=v_gm_b_s, w_out=v_w_out, norm_xa_g=v_norm_xa_g, mem_norm_g=v_mem_norm_g, xa_wq=v_xa_wq,
               xa_wkv=v_xa_wkv, xa_wo=v_xa_wo, norm_ffn_g=v_norm_ffn_g, ffn_w_gate_up=v_ffn_w_gate_up,
               ffn_w_down=v_ffn_w_down, final_norm_g=v_final_norm_g)

    me = 4 * lax.axis_index("x") + 2 * lax.axis_index("y") + lax.axis_index("c")
    s, d = x.shape[1], x.shape[2]
    xs = x.reshape(s, d)
    mems = mem.reshape(mem.shape[1], d)
    tgt = loss_target.reshape(s, d)
    ts = min(512, s)
    ts_ffn = min(256, s)
    row = lambda a: a.reshape(1, -1)

    conv_w_pad = jnp.pad(conv_w, ((0, CONV_PAD - CONV_K), (0, 128 - conv_w.shape[1])))
    first_level = (SIBLING,) + OTHER_CHIPS
    ex_first = _exchange_start([_bf(w_in), conv_w_pad], [False] * 2, "gather_start_in", first_level, own_slot=me)
    behind = lambda t: _bf(t + ex_first[4][0:1, 0:1])
    shards = [behind(xa_wkv), behind(w_out), behind(xa_wq), behind(xa_wo), behind(ffn_w_gate_up.T), behind(ffn_w_down)]
    ex_rest = _exchange_start(shards, [False] * len(shards), "gather_start_rest", first_level, own_slot=me)
    g_send, g_recv, g_src, g_land = (list(ex_first[i]) + list(ex_rest[i]) for i in range(4))
    g_tok = ex_rest[4]

    def arrived(idx, after, name):
        pick = lambda seq: [seq[i] for i in idx]
        return _wait_and_forward(pick(g_send), pick(g_recv), pick(g_src), pick(g_land), after, "gather_pass_" + name,
                                 first_level)

    def complete(handle, after, name):
        return _forward_wait(handle[0], handle[1], handle[2], after, "forward_wait_" + name)

    bst = gm_b_s.T
    wst = jnp.swapaxes(gm_w_s, 1, 2)

    h_in = arrived((0, 1), g_tok, "in")
    w_in3, conv_w8 = complete(h_in, h_in[3], "in")
    conv_w_f = conv_w8[:, :CONV_K, :conv_w.shape[1]].transpose(1, 0, 2).reshape(CONV_K, -1)
    cw = conv_w_f.shape[1]
    z, a = _fwd_in(xs, row(norm_mix_g), w_in3, row(b_in), ts)
    h_out = arrived((2, 3), z, "out")
    c = _conv_fwd(a, conv_w_f + h_out[3][0:1, 0:1], row(conv_b))
    h_xa = arrived((4, 5), c, "xa")
    wkv3, w_out3 = complete(h_out, h_xa[3], "out")
    w_out_f = w_out3.reshape(-1, d)
    kv, mn = _kv_proj(mems, row(mem_norm_g), wkv3)
    h1 = _fwd_out(xs, c, z, row(conv_ln_g), row(conv_ln_b), row(gm_ln_g), row(gm_ln_b), gm_w_s, bst, w_out_f, ts)
    h_gut = arrived((6,), h1, "gate_up")
    wq3, wo3 = complete(h_xa, h_gut[3], "xa")
    wq_f = wq3.reshape(-1, d)
    wo_f = wo3.reshape(-1, d)
    h2 = _fwd_xa(h1, row(norm_xa_g), wq_f, kv, wo_f, ts)
    h_down = arrived((7,), h2, "down")
    (wgut3,) = complete(h_gut, h_down[3], "gate_up")
    (wdown3,) = complete(h_down, wgut3, "down")
    wgut_f = wgut3.reshape(-1, d)
    wdown_f = wdown3.reshape(-1, d)
    dh3, dh3b, gu, hn2, loss_p, d_final_g = _fwd_ffn(h2, row(norm_ffn_g), wgut_f, wdown_f, row(final_norm_g), tgt,
                                                     ts_ffn)

    blocks = lambda m: m.reshape(N_DEV, -1, d)
    tok = lambda ex: ex[4][0:1, 0:1]
    dh2, act, dgu, d_ffn_g = _bwd_ffn(h2, dh3, gu, row(norm_ffn_g), wgut_f, wdown_f, ts_ffn)
    dwgut = _tn_matmul(dgu, hn2, 1408, "dw_gate_up")
    dwdown = _tn_matmul(act, dh3b, 1408, "dw_down")
    ex_ffn = _exchange_start([blocks(dwgut), blocks(dwdown)], [True, True], "scatter_start_ffn")
    dh1, dwq, dwo, dkv, d_xa_g = _bwd_xa(h1, dh2, row(norm_xa_g) + tok(ex_ffn), wq_f, wo_f, kv, ts)
    dwkv3, d_mem_g = _bwd_kv(dkv, mn, mems, row(mem_norm_g), wkv3)
    ex_xa = _exchange_start([blocks(dwq), blocks(dwo), dwkv3], [True] * 3, "scatter_start_xa")
    (dwout, dc, dzuv, dws, dbst, d_cln_g, d_cln_b, d_gln_g, d_gln_b, dbin_uv) = _bwd_out(
        dh1, c, z, row(conv_ln_g) + tok(ex_xa), row(conv_ln_b), row(gm_ln_g), row(gm_ln_b), gm_w_s, wst, bst,
        w_out_f, ts)
    early = dict(b_in_uv=dbin_uv, conv_ln_g=d_cln_g, conv_ln_b=d_cln_b, gm_ln_g=d_gln_g, gm_ln_b=d_gln_b, gm_w_s=dws,
                 gm_b_s=dbst.T, norm_xa_g=d_xa_g, mem_norm_g=d_mem_g, norm_ffn_g=d_ffn_g, final_norm_g=d_final_g,
                 loss=loss_p)
    ex_out = _exchange_start([blocks(dwout), _pack(list(early.values()))], [True, False], "scatter_start_out")
    da, dconv_w, dconv_b = _conv_bwd(dc, a, conv_w_f + tok(ex_out))
    dx, dw_in3, dbin_ag, d_mix_g = _bwd_in(xs, dh1, da, z, dzuv, row(norm_mix_g), w_in3, ts)
    late = dict(norm_mix_g=d_mix_g, b_in_ag=dbin_ag, conv_w=dconv_w[:CONV_K], conv_b=dconv_b)
    ex_last = _exchange_start([_pack(list(late.values())), dw_in3], [False, True], "scatter_start_in")

    me1 = me.astype(jnp.int32).reshape(1)

    def small_totals(ex, k, after, name):
        srcs, lands = _exchange_wait([ex[0][k]], [ex[1][k]], [ex[2][k]], [ex[3][k]], [False], after, name)
        return _sum8(lands[0], srcs[0], me1, "sum_" + name)

    early_tot = small_totals(ex_out, 1, ex_last[4], "small_early")
    late_tot = small_totals(ex_last, 0, early_tot, "small_late")
    early_g = dict(zip(early, _unpack(early_tot, [v.shape for v in early.values()])))
    late_g = dict(zip(late, _unpack(late_tot, [v.shape for v in late.values()])))
    loss = early_g["loss"][0, 0]
    grads = {}
    for nm in SMALL:
        if nm == "b_in":
            g = jnp.concatenate([late_g["b_in_ag"], early_g["b_in_uv"]], axis=1)
        elif nm == "conv_w":
            g = lax.dynamic_slice_in_dim(late_g[nm], me * conv_w.shape[1], conv_w.shape[1], axis=1)
        else:
            g = late_g[nm] if nm in late_g else early_g[nm]
        grads[nm] = g.reshape(w[nm].shape)

    delta, new_m, new_v = {}, {}, {}
    small = [[src[nm] for nm in SMALL] for src in (w, grads, mom, var)]
    for dst, vals in zip((delta, new_m, new_v), _adamw_many(*small, "adamw_small")):
        dst.update(zip(SMALL, vals))

    def reduced(srcs, lands, names):
        for nm, src, land in zip(names, srcs, lands):
            view = (lambda t: t.T) if nm == "ffn_w_gate_up" else (lambda t: t)
            upd = _sum_adamw(land, src, view(w[nm]), view(mom[nm]), view(var[nm]), me1, "adamw_" + nm)
            grads[nm], delta[nm], new_m[nm], new_v[nm] = (view(t) for t in upd)
        return new_v[names[-1]]

    after = delta[SMALL[0]]
    for ex, names, tag in ((ex_ffn, ("ffn_w_gate_up", "ffn_w_down"), "ffn"), (ex_xa, ("xa_wq", "xa_wo", "xa_wkv"), "xa")):
        srcs, lands = _exchange_wait(ex[0], ex[1], ex[2], ex[3], [True] * len(names), after, "scatter_wait_" + tag)
        after = reduced(srcs, lands, names)
    srcs, lands = _exchange_wait([ex_out[0][0]], [ex_out[1][0]], [ex_out[2][0]], [ex_out[3][0]], [True], after,
                                 "scatter_wait_out")
    after = reduced(srcs, lands, ("w_out",))
    srcs, lands = _exchange_wait([ex_last[0][1]], [ex_last[1][1]], [ex_last[2][1]], [ex_last[3][1]], [True], after,
                                 "scatter_wait_in")
    reduced(srcs, lands, ("w_in",))

    return (loss, dx.reshape(x.shape), *[grads[nm] for nm in WEIGHTS], *[delta[nm] for nm in WEIGHTS],
            *[new_m[nm] for nm in WEIGHTS], *[new_v[nm] for nm in WEIGHTS])
```

```python
import functools

import jax
import jax.numpy as jnp
from jax import lax
from jax.experimental import pallas as pl
from jax.experimental.pallas import tpu as pltpu

F32 = jnp.float32
BF16 = jnp.bfloat16
SDS = jax.ShapeDtypeStruct

N_DEV = 8
RMS_EPS = 1e-6
LN_EPS = 1e-5
CONV_K = 31
CONV_PAD = 32
CHUNK = 128
GM_HEADS = 8
XA_HEADS = 4
XA_DH = 256
GELU_K0 = 0.7978845608028654
GELU_K1 = 0.044715
ADAM_LR = 0.001
ADAM_B1 = 0.9
ADAM_B2 = 0.999
ADAM_EPS = 1e-08
ADAM_WD = 0.01
ADAM_STEP = 10
VMEM_LIMIT = 60 * 1024 * 1024

NN = (((1,), (0,)), ((), ()))
NT = (((1,), (1,)), ((), ()))
TN = (((0,), (0,)), ((), ()))


def _dot(a, b, dims=NN):
    return lax.dot_general(a, b, dims, preferred_element_type=F32)


def _bf(x):
    return x.astype(BF16)


def _cparams(*sem):
    return pltpu.CompilerParams(dimension_semantics=tuple(sem) if sem else None, vmem_limit_bytes=VMEM_LIMIT)


def _row(ts, w, col=0):
    return pl.BlockSpec((ts, w), lambda i: (i, col))


def _const(shape):
    nd = len(shape)
    return pl.BlockSpec(shape, lambda i: (0,) * nd, pipeline_mode=pl.Buffered(1))


def _acc(shape):
    nd = len(shape)
    return pl.BlockSpec(shape, lambda i: (0,) * nd)


def _rms_fwd(x, g):
    r = lax.rsqrt(jnp.mean(x * x, axis=-1, keepdims=True) + RMS_EPS)
    xh = x * r
    return xh * g, xh, r


def _rms_bwd(dy, xh, r, g):
    gdy = dy * g
    dx = r * (gdy - xh * jnp.mean(gdy * xh, axis=-1, keepdims=True))
    dg = jnp.sum(dy * xh, axis=0, keepdims=True)
    return dx, dg


def _ln_fwd(x, g, b):
    mu = jnp.mean(x, axis=-1, keepdims=True)
    xc = x - mu
    rs = lax.rsqrt(jnp.mean(xc * xc, axis=-1, keepdims=True) + LN_EPS)
    xh = xc * rs
    return xh * g + b, xh, rs


def _ln_bwd(dy, xh, rs, g):
    dxh = dy * g
    dx = rs * (dxh - jnp.mean(dxh, axis=-1, keepdims=True) - xh * jnp.mean(dxh * xh, axis=-1, keepdims=True))
    return dx, jnp.sum(dy * xh, axis=0, keepdims=True), jnp.sum(dy, axis=0, keepdims=True)


def _gelu(x):
    t = jnp.tanh(GELU_K0 * (x + GELU_K1 * (x * x * x)))
    return 0.5 * x * (1.0 + t), t


def _gelu_grad(x, t):
    return 0.5 * (1.0 + t) + 0.5 * x * (1.0 - t * t) * (GELU_K0 * (1.0 + 3.0 * GELU_K1 * x * x))


def _silu_grad(x, sg):
    return sg * (1.0 + x * (1.0 - sg))


def _by_residue(offsets):
    groups = {}
    for off in offsets:
        groups.setdefault(off % 8, []).append(off)
    return [(res, sorted(offs)) for res, offs in sorted(groups.items())]


def _accumulate(ref, val):
    @pl.when(pl.program_id(0) == 0)
    def _():
        ref[...] = jnp.zeros_like(ref)
    ref[...] += val


def _mix_masks():
    row = lax.broadcasted_iota(jnp.int32, (CHUNK, CHUNK), 0)
    col = lax.broadcasted_iota(jnp.int32, (CHUNK, CHUNK), 1)
    return row >= col, row <= col, col < (CHUNK // 2)


def _mix_fwd(vb, ws_ref, bst_ref, mixed_scr, ts):
    tril, _, lo = _mix_masks()
    for j in range(GM_HEADS // 2):
        w0 = _bf(jnp.where(tril, ws_ref[2 * j], 0.0))
        w1 = _bf(jnp.where(tril, ws_ref[2 * j + 1], 0.0))
        bias = jnp.where(lo, bst_ref[:, 2 * j:2 * j + 1], bst_ref[:, 2 * j + 1:2 * j + 2])
        for n in range(ts // CHUNK):
            v = vb[n * CHUNK:(n + 1) * CHUNK, j * 128:(j + 1) * 128]
            mixed_scr[n * CHUNK:(n + 1) * CHUNK, j * 128:(j + 1) * 128] = jnp.where(lo, _dot(w0, v), _dot(w1, v)) + bias


def _exchange(srcs, scatter, name):
    n = len(srcs)

    def body(*refs):
        src_refs, out_refs = refs[:n], refs[n:2 * n]
        send_sems, recv_sems, local_sems = refs[2 * n:]
        x, y, c = lax.axis_index("x"), lax.axis_index("y"), lax.axis_index("c")
        me = 4 * x + 2 * y + c

        def peer_of(mask):
            px = x if not (mask >> 2) & 1 else 1 - x
            py = y if not (mask >> 1) & 1 else 1 - y
            pc = c if not mask & 1 else 1 - c
            return (px, py, pc), 4 * px + 2 * py + pc

        def remote(k, mask):
            peer, pidx = peer_of(mask)
            return pltpu.make_async_remote_copy(
                src_ref=src_refs[k].at[pidx] if scatter else src_refs[k],
                dst_ref=out_refs[k].at[me],
                send_sem=send_sems.at[k, mask - 1], recv_sem=recv_sems.at[k, mask - 1],
                device_id=peer, device_id_type=pl.DeviceIdType.MESH)

        def arrival(k, mask):
            peer, pidx = peer_of(mask)
            return pltpu.make_async_remote_copy(
                src_ref=src_refs[k].at[pidx] if scatter else src_refs[k],
                dst_ref=out_refs[k].at[pidx],
                send_sem=send_sems.at[k, mask - 1], recv_sem=recv_sems.at[k, mask - 1],
                device_id=peer, device_id_type=pl.DeviceIdType.MESH)

        sends, locals_ = [], []
        for k in range(n):
            for mask in range(1, N_DEV):
                cp = remote(k, mask)
                cp.start()
                sends.append(cp)
            lc = pltpu.make_async_copy(src_refs[k].at[me] if scatter else src_refs[k], out_refs[k].at[me],
                                       local_sems.at[k])
            lc.start()
            locals_.append(lc)
        for k in range(n):
            for mask in range(1, N_DEV):
                arrival(k, mask).wait_recv()
        for cp in sends:
            cp.wait_send()
        for lc in locals_:
            lc.wait()

    outs = [SDS((N_DEV,) + tuple(s.shape[1:] if scatter else s.shape), s.dtype) for s in srcs]
    hbm = pl.BlockSpec(memory_space=pl.ANY)
    return pl.pallas_call(
        body, name=name, out_shape=outs, in_specs=[hbm] * n, out_specs=[hbm] * n,
        scratch_shapes=[pltpu.SemaphoreType.DMA((n, N_DEV - 1)), pltpu.SemaphoreType.DMA((n, N_DEV - 1)),
                        pltpu.SemaphoreType.DMA((n,))],
    )(*srcs)


def _peer_of(mask):
    x, y, c = lax.axis_index("x"), lax.axis_index("y"), lax.axis_index("c")
    px = 1 - x if (mask >> 2) & 1 else x
    py = 1 - y if (mask >> 1) & 1 else y
    pc = 1 - c if mask & 1 else c
    return (px, py, pc), 4 * px + 2 * py + pc


ALL_PEERS = tuple(range(1, N_DEV))
OTHER_CHIPS = (2, 4, 6)
CHIPS = (0,) + OTHER_CHIPS
SIBLING = 1


def _split_copy(src_ref, land_ref, send_sem, recv_sem, mask, slot, scatter, outgoing):
    x, y, c = lax.axis_index("x"), lax.axis_index("y"), lax.axis_index("c")
    me = 4 * x + 2 * y + c
    peer, pidx = _peer_of(mask)
    return pltpu.make_async_remote_copy(
        src_ref=src_ref.at[pidx] if scatter else src_ref,
        dst_ref=land_ref.at[me if outgoing else pidx],
        send_sem=send_sem.at[slot], recv_sem=recv_sem.at[slot],
        device_id=peer, device_id_type=pl.DeviceIdType.MESH)


_HBM = pl.BlockSpec(memory_space=pltpu.HBM)
_SEM = pl.BlockSpec(memory_space=pltpu.SEMAPHORE)
_EFFECT = pltpu.SideEffectType.DATAFLOW_SIDE_EFFECTING


def _exchange_start(srcs, scatter, name, masks=ALL_PEERS, own_slot=None):
    n = len(srcs)
    lands = [lax.empty((N_DEV,) + tuple(s.shape[1:] if sc else s.shape), s.dtype) for s, sc in zip(srcs, scatter)]
    if own_slot is not None:
        lands = [land if sc else _with_own(land, s, own_slot) for land, s, sc in zip(lands, srcs, scatter)]
    lands = [pltpu.with_memory_space_constraint(land, pltpu.HBM) for land in lands]
    srcs = [pltpu.with_memory_space_constraint(s, pltpu.HBM) for s in srcs]

    def body(*refs):
        src_refs, land_refs = refs[:n], refs[n:2 * n]
        send_sems, recv_sems = refs[2 * n:3 * n], refs[3 * n:4 * n]
        token = refs[-1]
        for k in range(n):
            for slot, mask in enumerate(masks):
                _split_copy(src_refs[k], land_refs[k], send_sems[k], recv_sems[k], mask, slot, scatter[k], True).start()
        token[...] = jnp.zeros_like(token)

    sem = pltpu.SemaphoreType.DMA((len(masks),))
    out = pl.pallas_call(
        body, name=name,
        out_shape=tuple([sem] * (2 * n) + [pltpu.HBM(s.shape, s.dtype) for s in srcs]
                        + [pltpu.HBM(l.shape, l.dtype) for l in lands] + [SDS((8, 128), F32)]),
        in_specs=[_HBM] * (2 * n),
        out_specs=tuple([_SEM] * (2 * n) + [_HBM] * (2 * n) + [pl.BlockSpec(memory_space=pltpu.VMEM)]),
        input_output_aliases={i: 2 * n + i for i in range(2 * n)},
        compiler_params=pltpu.CompilerParams(has_side_effects=_EFFECT),
    )(*srcs, *lands)
    return out[:n], out[n:2 * n], out[2 * n:3 * n], out[3 * n:4 * n], out[-1]


def _exchange_wait(send_sems, recv_sems, srcs_thru, lands_thru, scatter, after, name, masks=ALL_PEERS):
    n = len(srcs_thru)

    def body(*refs):
        src_refs, land_refs = refs[:n], refs[n:2 * n]
        send_refs, recv_refs = refs[2 * n:3 * n], refs[3 * n:4 * n]
        for k in range(n):
            for slot, mask in enumerate(masks):
                args = (src_refs[k], land_refs[k], send_refs[k], recv_refs[k], mask, slot, scatter[k])
                _split_copy(*args, True).wait_send()
                _split_copy(*args, False).wait_recv()

    out = pl.pallas_call(
        body, name=name,
        out_shape=tuple([pltpu.HBM(s.shape, s.dtype) for s in srcs_thru]
                        + [pltpu.HBM(l.shape, l.dtype) for l in lands_thru]),
        in_specs=[_HBM] * (2 * n) + [_SEM] * (2 * n) + [pl.BlockSpec(memory_space=pl.ANY)],
        out_specs=tuple([_HBM] * (2 * n)),
        input_output_aliases={i: i for i in range(2 * n)},
        compiler_params=pltpu.CompilerParams(has_side_effects=_EFFECT),
    )(*srcs_thru, *lands_thru, *send_sems, *recv_sems, after)
    return out[:n], out[n:]


def _wait_and_forward(send_sems, recv_sems, srcs_thru, lands_thru, after, name, masks):
    n = len(lands_thru)

    def body(*refs):
        src_refs, land_refs = refs[:n], refs[n:2 * n]
        send_refs, recv_refs = refs[2 * n:3 * n], refs[3 * n:4 * n]
        outs = refs[4 * n + 1:]
        fsend, frecv, token = outs[2 * n:3 * n], outs[3 * n:4 * n], outs[-1]
        for k in range(n):
            for slot, mask in enumerate(masks):
                args = (src_refs[k], land_refs[k], send_refs[k], recv_refs[k], mask, slot, False)
                _split_copy(*args, True).wait_send()
                _split_copy(*args, False).wait_recv()
        sibling, _ = _peer_of(SIBLING)
        for k in range(n):
            for slot, mask in enumerate(OTHER_CHIPS):
                _, mine = _peer_of(mask)
                pltpu.make_async_remote_copy(
                    src_ref=land_refs[k].at[mine], dst_ref=land_refs[k].at[mine], send_sem=fsend[k].at[slot],
                    recv_sem=frecv[k].at[slot], device_id=sibling, device_id_type=pl.DeviceIdType.MESH).start()
        token[...] = jnp.zeros_like(token)

    sem = pltpu.SemaphoreType.DMA((len(OTHER_CHIPS),))
    out = pl.pallas_call(
        body, name=name,
        out_shape=tuple([pltpu.HBM(s.shape, s.dtype) for s in srcs_thru] + [pltpu.HBM(l.shape, l.dtype) for l in lands_thru]
                        + [sem] * (2 * n) + [SDS((8, 128), F32)]),
        in_specs=[_HBM] * (2 * n) + [_SEM] * (2 * n) + [pl.BlockSpec(memory_space=pl.ANY)],
        out_specs=tuple([_HBM] * (2 * n) + [_SEM] * (2 * n) + [pl.BlockSpec(memory_space=pltpu.VMEM)]),
        input_output_aliases={i: i for i in range(2 * n)},
        compiler_params=pltpu.CompilerParams(has_side_effects=_EFFECT),
    )(*srcs_thru, *lands_thru, *send_sems, *recv_sems, after)
    return out[2 * n:3 * n], out[3 * n:4 * n], out[n:2 * n], out[-1]


def _forward_wait(send_sems, recv_sems, lands, after, name):
    n = len(lands)

    def body(*refs):
        land_refs, send_refs, recv_refs = refs[:n], refs[n:2 * n], refs[2 * n:3 * n]
        sibling, _ = _peer_of(SIBLING)
        for k in range(n):
            for slot, mask in enumerate(OTHER_CHIPS):
                _, mine = _peer_of(mask)
                _, theirs = _peer_of(mask | SIBLING)
                for block, wait in ((mine, "wait_send"), (theirs, "wait_recv")):
                    getattr(pltpu.make_async_remote_copy(
                        src_ref=land_refs[k].at[block], dst_ref=land_refs[k].at[block],
                        send_sem=send_refs[k].at[slot], recv_sem=recv_refs[k].at[slot], device_id=sibling,
                        device_id_type=pl.DeviceIdType.MESH), wait)()

    return pl.pallas_call(
        body, name=name, out_shape=tuple(pltpu.HBM(l.shape, l.dtype) for l in lands),
        in_specs=[_HBM] * n + [_SEM] * (2 * n) + [pl.BlockSpec(memory_space=pl.ANY)],
        out_specs=tuple([_HBM] * n), input_output_aliases={i: i for i in range(n)},
        compiler_params=pltpu.CompilerParams(has_side_effects=_EFFECT),
    )(*lands, *send_sems, *recv_sems, after)


def _pair_copy(parts_ref, land_ref, send_sem, recv_sem, slot, chip, outgoing):
    sibling, _ = _peer_of(SIBLING)
    _, block = _peer_of(chip | SIBLING if outgoing else chip)
    return pltpu.make_async_remote_copy(
        src_ref=parts_ref.at[block], dst_ref=land_ref.at[block], send_sem=send_sem.at[slot], recv_sem=recv_sem.at[slot],
        device_id=sibling, device_id_type=pl.DeviceIdType.MESH)


def _pair_start(parts, name):
    n = len(parts)
    lands = [pltpu.with_memory_space_constraint(lax.empty(p.shape, p.dtype), pltpu.HBM) for p in parts]
    parts = [pltpu.with_memory_space_constraint(p, pltpu.HBM) for p in parts]

    def body(*refs):
        part_refs, land_refs = refs[:n], refs[n:2 * n]
        send_sems, recv_sems, token = refs[2 * n:3 * n], refs[3 * n:4 * n], refs[-1]
        for k in range(n):
            for slot, chip in enumerate(CHIPS):
                _pair_copy(part_refs[k], land_refs[k], send_sems[k], recv_sems[k], slot, chip, True).start()
        token[...] = jnp.zeros_like(token)

    sem = pltpu.SemaphoreType.DMA((len(CHIPS),))
    out = pl.pallas_call(
        body, name=name,
        out_shape=tuple([sem] * (2 * n) + [pltpu.HBM(p.shape, p.dtype) for p in parts + lands] + [SDS((8, 128), F32)]),
        in_specs=[_HBM] * (2 * n),
        out_specs=tuple([_SEM] * (2 * n) + [_HBM] * (2 * n) + [pl.BlockSpec(memory_space=pltpu.VMEM)]),
        input_output_aliases={i: 2 * n + i for i in range(2 * n)},
        compiler_params=pltpu.CompilerParams(has_side_effects=_EFFECT),
    )(*parts, *lands)
    return out[:n], out[n:2 * n], out[2 * n:3 * n], out[3 * n:4 * n], out[-1]


def _pair_wait(send_sems, recv_sems, parts_thru, lands_thru, after, name):
    n = len(parts_thru)

    def body(*refs):
        part_refs, land_refs = refs[:n], refs[n:2 * n]
        send_refs, recv_refs = refs[2 * n:3 * n], refs[3 * n:4 * n]
        for k in range(n):
            for slot, chip in enumerate(CHIPS):
                args = (part_refs[k], land_refs[k], send_refs[k], recv_refs[k], slot, chip)
                _pair_copy(*args, True).wait_send()
                _pair_copy(*args, False).wait_recv()

    out = pl.pallas_call(
        body, name=name,
        out_shape=tuple(pltpu.HBM(p.shape, p.dtype) for p in list(parts_thru) + list(lands_thru)),
        in_specs=[_HBM] * (2 * n) + [_SEM] * (2 * n) + [pl.BlockSpec(memory_space=pl.ANY)],
        out_specs=tuple([_HBM] * (2 * n)), input_output_aliases={i: i for i in range(2 * n)},
        compiler_params=pltpu.CompilerParams(has_side_effects=_EFFECT),
    )(*parts_thru, *lands_thru, *send_sems, *recv_sems, after)
    return out[:n], out[n:]


def _with_own(landed, own, me):
    return lax.dynamic_update_slice_in_dim(landed, own[None], me, axis=0)


def _kv_proj(mem, g_mem, wkv3):
    m = mem.shape[0]

    def body(mem_ref, g_ref, w_ref, kv_ref, mn_ref):
        y, _, _ = _rms_fwd(mem_ref[...], g_ref[...])
        yb = _bf(y)
        mn_ref[...] = yb
        for b in range(N_DEV):
            kv_ref[:, 256 * b:256 * (b + 1)] = _dot(yb, w_ref[b])

    return pl.pallas_call(body, name="kv_proj", out_shape=(SDS((m, 2048), F32), SDS(mem.shape, BF16)),
                          compiler_params=_cparams())(mem, g_mem, wkv3)


def _fwd_in(x, g_mix, w_in3, b_in, ts):
    s, d = x.shape

    def body(x_ref, g_ref, w_ref, b_ref, z_ref, a_ref):
        hn, _, _ = _rms_fwd(x_ref[...], g_ref[...])
        hb = _bf(hn)
        for b in range(N_DEV):
            z_ref[:, 256 * b:256 * (b + 1)] = _dot(hb, w_ref[b]) + b_ref[:, 256 * b:256 * (b + 1)]
        a_ref[...] = z_ref[:, 0:512] * jax.nn.sigmoid(z_ref[:, 512:1024])

    return pl.pallas_call(
        body, name="fwd_in", grid=(s // ts,),
        in_specs=[_row(ts, d), _const(g_mix.shape), _const(w_in3.shape), _const(b_in.shape)],
        out_specs=(_row(ts, 2048), _row(ts, 512)),
        out_shape=(SDS((s, 2048), F32), SDS((s, 512), F32)),
        compiler_params=_cparams("arbitrary"))(x, g_mix, w_in3, b_in)


def _conv_fwd(a, w, b):
    s, cw = a.shape
    rc = 256 if s % 256 == 0 else 128

    def body(a_ref, w_ref, b_ref, c_ref, pad):
        pad[0:CONV_PAD, :] = jnp.zeros((CONV_PAD, 128), F32)
        pad[CONV_PAD:, :] = a_ref[...]

        def chunk(i, carry):
            r0 = pl.multiple_of(i * rc, rc)
            acc = jnp.zeros((rc, 128), F32) + b_ref[...]
            for res, offs in _by_residue(range(CONV_PAD - CONV_K + 1, CONV_PAD + 1)):
                shifted = pad[pl.ds(r0 + res, rc + offs[-1] - res), :]
                for off in offs:
                    k = off - (CONV_PAD - CONV_K + 1)
                    acc = acc + w_ref[k:k + 1, :] * shifted[off - res:off - res + rc, :]
            c_ref[pl.ds(r0, rc), :] = acc
            return carry

        lax.fori_loop(0, s // rc, chunk, 0)

    blk = lambda r: pl.BlockSpec((r, 128), lambda j: (0, j))
    return pl.pallas_call(
        body, name="conv_fwd", grid=(cw // 128,),
        in_specs=[blk(s), blk(CONV_K), blk(1)], out_specs=blk(s), out_shape=SDS((s, cw), F32),
        scratch_shapes=[pltpu.VMEM((s + CONV_PAD, 128), F32)],
        compiler_params=_cparams("arbitrary"))(a, w, b)


def _fwd_out(x, c, z, cln_g, cln_b, gln_g, gln_b, ws, bst, w_out, ts):
    s, d = x.shape

    def body(x_ref, c_ref, zuv_ref, clg, clb, glg, glb, ws_ref, bst_ref, wo_ref, h1_ref, mixed_scr):
        cl, _, _ = _ln_fwd(c_ref[...], clg[...], clb[...])
        co = cl * jax.nn.sigmoid(cl)
        u, _ = _gelu(zuv_ref[:, 0:512])
        vg, _ = _gelu(zuv_ref[:, 512:1024])
        vln, _, _ = _ln_fwd(vg, glg[...], glb[...])
        _mix_fwd(_bf(vln), ws_ref, bst_ref, mixed_scr, ts)
        gm = u * mixed_scr[...]
        h1_ref[...] = x_ref[...] + _dot(_bf(co), wo_ref[0:512, :]) + _dot(_bf(gm), wo_ref[512:1024, :])

    return pl.pallas_call(
        body, name="fwd_out", grid=(s // ts,),
        in_specs=[_row(ts, d), _row(ts, 512), _row(ts, 1024, 1), _const(cln_g.shape), _const(cln_b.shape),
                  _const(gln_g.shape), _const(gln_b.shape), _const(ws.shape), _const(bst.shape), _const(w_out.shape)],
        out_specs=_row(ts, d), out_shape=SDS((s, d), F32),
        scratch_shapes=[pltpu.VMEM((ts, 512), F32)],
        compiler_params=_cparams("arbitrary"))(x, c, z, cln_g, cln_b, gln_g, gln_b, ws, bst, w_out)


def _softmax_rows(sc):
    m = jnp.max(sc, axis=-1, keepdims=True)
    e = jnp.exp(sc - m)
    return e / jnp.sum(e, axis=-1, keepdims=True)


def _fwd_xa(h1, g_xa, wq, kv, wo, ts):
    s, d = h1.shape
    scale = XA_DH ** -0.5

    def body(h_ref, g_ref, wq_ref, kv_ref, wo_ref, h2_ref, o_scr):
        hn, _, _ = _rms_fwd(h_ref[...], g_ref[...])
        q = _dot(_bf(hn), wq_ref[...])
        for h in range(XA_HEADS):
            qh = _bf(q[:, XA_DH * h:XA_DH * (h + 1)])
            kh = _bf(kv_ref[:, XA_DH * h:XA_DH * (h + 1)])
            vh = _bf(kv_ref[:, d + XA_DH * h:d + XA_DH * (h + 1)])
            p = _softmax_rows(_dot(qh, kh, NT) * scale)
            o_scr[:, XA_DH * h:XA_DH * (h + 1)] = _dot(_bf(p), vh)
        h2_ref[...] = h_ref[...] + _dot(_bf(o_scr[...]), wo_ref[...])

    return pl.pallas_call(
        body, name="fwd_xa", grid=(s // ts,),
        in_specs=[_row(ts, d), _const(g_xa.shape), _const(wq.shape), _const(kv.shape), _const(wo.shape)],
        out_specs=_row(ts, d), out_shape=SDS((s, d), F32),
        scratch_shapes=[pltpu.VMEM((ts, d), F32)],
        compiler_params=_cparams("arbitrary"))(h1, g_xa, wq, kv, wo)


def _fwd_ffn(h2, g_ffn, wgut, wdown, g_final, target, ts):
    s, d = h2.shape
    hid = wdown.shape[0]
    hc = hid // 2

    def body(h_ref, g_ref, wgu_ref, wd_ref, gf_ref, t_ref, dh3_ref, dh3b_ref, gu_ref, hn_ref, loss_ref, dgf_ref):
        hn, _, _ = _rms_fwd(h_ref[...], g_ref[...])
        hb = _bf(hn)
        hn_ref[...] = hb
        h3 = h_ref[...]
        for n in range(2):
            g = _dot(hb, wgu_ref[hc * n:hc * (n + 1), :], NT)
            u = _dot(hb, wgu_ref[hid + hc * n:hid + hc * (n + 1), :], NT)
            gu_ref[:, hc * n:hc * (n + 1)] = g
            gu_ref[:, hid + hc * n:hid + hc * (n + 1)] = u
            act = g * jax.nn.sigmoid(g) * u
            h3 = h3 + _dot(_bf(act), wd_ref[hc * n:hc * (n + 1), :])
        y, xh, r = _rms_fwd(h3, gf_ref[...])
        diff = y - t_ref[...]
        part = 0.5 * jnp.sum(jnp.mean(diff * diff, axis=-1, keepdims=True), axis=0, keepdims=True)
        _accumulate(loss_ref, jnp.zeros(loss_ref.shape, F32) + part)
        dh3, dgf = _rms_bwd(diff * (1.0 / d), xh, r, gf_ref[...])
        dh3_ref[...] = dh3
        dh3b_ref[...] = _bf(dh3)
        _accumulate(dgf_ref, dgf)

    return pl.pallas_call(
        body, name="fwd_ffn", grid=(s // ts,),
        in_specs=[_row(ts, d), _const(g_ffn.shape), _const(wgut.shape), _const(wdown.shape), _const(g_final.shape),
                  _row(ts, d)],
        out_specs=(_row(ts, d), _row(ts, d), _row(ts, 2 * hid), _row(ts, d), _acc((1, 128)), _acc((1, d))),
        out_shape=(SDS((s, d), F32), SDS((s, d), BF16), SDS((s, 2 * hid), F32), SDS((s, d), BF16), SDS((1, 128), F32),
                   SDS((1, d), F32)),
        compiler_params=_cparams("arbitrary"))(h2, g_ffn, wgut, wdown, g_final, target)


def _bwd_ffn(h2, dh3, gu, g_ffn, wgut, wdown, ts):
    s, d = h2.shape
    hid = wdown.shape[0]
    hc = hid // 2

    def body(h_ref, dh3_ref, gu_ref, g_ref, wgu_ref, wd_ref, dh2_ref, act_ref, dgu_ref, dg_ref):
        _, xh, r = _rms_fwd(h_ref[...], g_ref[...])
        db = _bf(dh3_ref[...])
        dhn = jnp.zeros((ts, d), F32)
        for n in range(2):
            wg = wgu_ref[hc * n:hc * (n + 1), :]
            wu = wgu_ref[hid + hc * n:hid + hc * (n + 1), :]
            g = gu_ref[:, hc * n:hc * (n + 1)]
            u = gu_ref[:, hid + hc * n:hid + hc * (n + 1)]
            sg = jax.nn.sigmoid(g)
            sl = g * sg
            act_ref[:, hc * n:hc * (n + 1)] = _bf(sl * u)
            dact = _dot(db, wd_ref[hc * n:hc * (n + 1), :], NT)
            dgb = _bf(dact * u * _silu_grad(g, sg))
            dub = _bf(dact * sl)
            dgu_ref[:, hc * n:hc * (n + 1)] = dgb
            dgu_ref[:, hid + hc * n:hid + hc * (n + 1)] = dub
            dhn = dhn + _dot(dgb, wg) + _dot(dub, wu)
        dx, dg = _rms_bwd(dhn, xh, r, g_ref[...])
        dh2_ref[...] = dh3_ref[...] + dx
        _accumulate(dg_ref, dg)

    return pl.pallas_call(
        body, name="bwd_ffn", grid=(s // ts,),
        in_specs=[_row(ts, d), _row(ts, d), _row(ts, 2 * hid), _const(g_ffn.shape), _const(wgut.shape),
                  _const(wdown.shape)],
        out_specs=(_row(ts, d), _row(ts, hid), _row(ts, 2 * hid), _acc((1, d))),
        out_shape=(SDS((s, d), F32), SDS((s, hid), BF16), SDS((s, 2 * hid), BF16), SDS((1, d), F32)),
        compiler_params=_cparams("arbitrary"))(h2, dh3, gu, g_ffn, wgut, wdown)


def _bwd_xa(h1, dh2, g_xa, wq, wo, kv, ts):
    s, d = h1.shape
    scale = XA_DH ** -0.5

    def body(h_ref, dh2_ref, g_ref, wq_ref, wo_ref, kv_ref, dh1_ref, dwq_ref, dwo_ref, dkv_ref, dg_ref,
             dq_scr, o_scr, accq, acco):
        hn, xh, r = _rms_fwd(h_ref[...], g_ref[...])
        hb = _bf(hn)
        q = _dot(hb, wq_ref[...])
        dh2b = _bf(dh2_ref[...])
        do = _dot(dh2b, wo_ref[...], NT)

        @pl.when(pl.program_id(0) == 0)
        def _():
            dkv_ref[...] = jnp.zeros_like(dkv_ref)
            accq[...] = jnp.zeros_like(accq)
            acco[...] = jnp.zeros_like(acco)

        for h in range(XA_HEADS):
            lo, hi = XA_DH * h, XA_DH * (h + 1)
            qh = _bf(q[:, lo:hi])
            kh = _bf(kv_ref[:, lo:hi])
            vh = _bf(kv_ref[:, d + lo:d + hi])
            p = _softmax_rows(_dot(qh, kh, NT) * scale)
            pb = _bf(p)
            o_scr[:, lo:hi] = _bf(_dot(pb, vh))
            doh = _bf(do[:, lo:hi])
            dp = _dot(doh, vh, NT)
            ds = p * (dp - jnp.sum(p * dp, axis=-1, keepdims=True)) * scale
            dsb = _bf(ds)
            dq_scr[:, lo:hi] = _dot(dsb, kh)
            dkv_ref[:, lo:hi] += _dot(dsb, qh, TN)
            dkv_ref[:, d + lo:d + hi] += _dot(pb, doh, TN)
        dqb = _bf(dq_scr[...])
        accq[...] += _dot(hb, dqb, TN)
        acco[...] += _dot(o_scr[...], dh2b, TN)
        dx, dg = _rms_bwd(_dot(dqb, wq_ref[...], NT), xh, r, g_ref[...])
        dh1_ref[...] = dh2_ref[...] + dx
        _accumulate(dg_ref, dg)

        @pl.when(pl.program_id(0) == pl.num_programs(0) - 1)
        def _():
            dwq_ref[...] = _bf(accq[...])
            dwo_ref[...] = _bf(acco[...])

    return pl.pallas_call(
        body, name="bwd_xa", grid=(s // ts,),
        in_specs=[_row(ts, d), _row(ts, d), _const(g_xa.shape), _const(wq.shape), _const(wo.shape), _const(kv.shape)],
        out_specs=(_row(ts, d), _acc((d, d)), _acc((d, d)), _acc(kv.shape), _acc((1, d))),
        out_shape=(SDS((s, d), F32), SDS((d, d), BF16), SDS((d, d), BF16), SDS(kv.shape, F32), SDS((1, d), F32)),
        scratch_shapes=[pltpu.VMEM((ts, d), F32), pltpu.VMEM((ts, d), BF16), pltpu.VMEM((d, d), F32),
                        pltpu.VMEM((d, d), F32)],
        compiler_params=_cparams("arbitrary"))(h1, dh2, g_xa, wq, wo, kv)


def _bwd_kv(dkv, mn, mem, g_mem, wkv3):
    d = mem.shape[1]

    def body(dkv_ref, mn_ref, mem_ref, g_ref, w_ref, dw_ref, dg_ref):
        dkvb = _bf(dkv_ref[...])
        dmn = jnp.zeros(mem_ref.shape, F32)
        for b in range(N_DEV):
            blk = dkvb[:, 256 * b:256 * (b + 1)]
            dmn = dmn + _dot(blk, w_ref[b], NT)
            dw_ref[b] = _bf(_dot(mn_ref[...], blk, TN))
        _, xh, r = _rms_fwd(mem_ref[...], g_ref[...])
        _, dg = _rms_bwd(dmn, xh, r, g_ref[...])
        dg_ref[...] = dg

    return pl.pallas_call(body, name="bwd_kv", out_shape=(SDS(wkv3.shape, BF16), SDS((1, d), F32)),
                          compiler_params=_cparams())(dkv, mn, mem, g_mem, wkv3)


def _bwd_out(dh1, c, z, cln_g, cln_b, gln_g, gln_b, ws, wst, bst, w_out, ts):
    s, d = dh1.shape
    nh = GM_HEADS

    def body(dh1_ref, c_ref, zuv_ref, clg, clb, glg, glb, ws_ref, wst_ref, bst_ref, wo_ref,
             dwo_ref, dc_ref, dzuv_ref, dws_ref, dbst_ref, dclg_ref, dclb_ref, dglg_ref, dglb_ref, dbin_ref,
             mixed_scr, dv_scr, acc):
        cl, chat, crs = _ln_fwd(c_ref[...], clg[...], clb[...])
        sg = jax.nn.sigmoid(cl)
        zu = zuv_ref[:, 0:512]
        zv = zuv_ref[:, 512:1024]
        u, tu = _gelu(zu)
        vg, tv = _gelu(zv)
        vln, vhat, vrs = _ln_fwd(vg, glg[...], glb[...])
        vb = _bf(vln)
        _mix_fwd(vb, ws_ref, bst_ref, mixed_scr, ts)
        mixed = mixed_scr[...]
        dh1b = _bf(dh1_ref[...])
        dcat = _dot(dh1b, wo_ref[...], NT)
        dgm = dcat[:, 512:1024]
        dc, dclg, dclb = _ln_bwd(dcat[:, 0:512] * _silu_grad(cl, sg), chat, crs, clg[...])
        dc_ref[...] = dc
        dzu = dgm * mixed * _gelu_grad(zu, tu)
        dm = dgm * u

        @pl.when(pl.program_id(0) == 0)
        def _():
            dws_ref[...] = jnp.zeros_like(dws_ref)
            dbst_ref[...] = jnp.zeros_like(dbst_ref)
            acc[...] = jnp.zeros_like(acc)

        acc[0:512, :] += _dot(_bf(cl * sg), dh1b, TN)
        acc[512:1024, :] += _dot(_bf(u * mixed), dh1b, TN)

        @pl.when(pl.program_id(0) == pl.num_programs(0) - 1)
        def _():
            dwo_ref[...] = _bf(acc[...])

        tril, triu, lo = _mix_masks()
        head = lax.broadcasted_iota(jnp.int32, (1, nh), 1)
        for j in range(nh // 2):
            w0t = _bf(jnp.where(triu, wst_ref[2 * j], 0.0))
            w1t = _bf(jnp.where(triu, wst_ref[2 * j + 1], 0.0))
            for n in range(ts // CHUNK):
                rows = slice(n * CHUNK, (n + 1) * CHUNK)
                lanes = slice(j * 128, (j + 1) * 128)
                dmc = dm[rows, lanes]
                dmb = _bf(dmc)
                dv_scr[rows, lanes] = jnp.where(lo, _dot(w0t, dmb), _dot(w1t, dmb))
                vc = vb[rows, lanes]
                d0 = jnp.where(lo, dmc, 0.0)
                d1 = dmc - d0
                dws_ref[2 * j] += jnp.where(tril, _dot(_bf(d0), vc, NT), 0.0)
                dws_ref[2 * j + 1] += jnp.where(tril, _dot(_bf(d1), vc, NT), 0.0)
                dbst_ref[...] += (jnp.sum(d0, axis=1, keepdims=True) * (head == 2 * j).astype(F32)
                                  + jnp.sum(d1, axis=1, keepdims=True) * (head == 2 * j + 1).astype(F32))
        dvg, dglg, dglb = _ln_bwd(dv_scr[...], vhat, vrs, glg[...])
        dzv = dvg * _gelu_grad(zv, tv)
        dzuv_ref[:, 0:512] = _bf(dzu)
        dzuv_ref[:, 512:1024] = _bf(dzv)
        _accumulate(dclg_ref, dclg)
        _accumulate(dclb_ref, dclb)
        _accumulate(dglg_ref, dglg)
        _accumulate(dglb_ref, dglb)
        _accumulate(dbin_ref, jnp.concatenate([jnp.sum(dzu, axis=0, keepdims=True),
                                               jnp.sum(dzv, axis=0, keepdims=True)], axis=1))

    vec = (1, 512)
    return pl.pallas_call(
        body, name="bwd_out", grid=(s // ts,),
        in_specs=[_row(ts, d), _row(ts, 512), _row(ts, 1024, 1), _const(cln_g.shape), _const(cln_b.shape),
                  _const(gln_g.shape), _const(gln_b.shape), _const(ws.shape), _const(wst.shape), _const(bst.shape),
                  _const(w_out.shape)],
        out_specs=(_acc((d, d)), _row(ts, 512), _row(ts, 1024), _acc(ws.shape), _acc(bst.shape), _acc(vec), _acc(vec),
                   _acc(vec), _acc(vec), _acc((1, 1024))),
        out_shape=(SDS((d, d), BF16), SDS((s, 512), F32), SDS((s, 1024), BF16), SDS(ws.shape, F32),
                   SDS(bst.shape, F32), SDS(vec, F32), SDS(vec, F32), SDS(vec, F32), SDS(vec, F32), SDS((1, 1024), F32)),
        scratch_shapes=[pltpu.VMEM((ts, 512), F32), pltpu.VMEM((ts, 512), F32), pltpu.VMEM((d, d), F32)],
        compiler_params=_cparams("arbitrary"))(dh1, c, z, cln_g, cln_b, gln_g, gln_b, ws, wst, bst, w_out)


def _conv_bwd(dc, a, w):
    s, cw = a.shape
    rc = 256 if s % 256 == 0 else 128

    def body(dc_ref, a_ref, w_ref, da_ref, dw_ref, db_ref, pad_a, pad_d, part):
        pad_a[0:CONV_PAD, :] = jnp.zeros((CONV_PAD, 128), F32)
        pad_a[CONV_PAD:, :] = a_ref[...]
        pad_d[0:s, :] = dc_ref[...]
        pad_d[s:, :] = jnp.zeros((CONV_PAD, 128), F32)
        part[...] = jnp.zeros_like(part)

        def rows8(v):
            return jnp.sum(v.reshape(rc // 8, 8, 128), axis=0)

        def chunk(i, carry):
            r0 = pl.multiple_of(i * rc, rc)
            dcc = pad_d[pl.ds(r0, rc), :]
            acc = jnp.zeros((rc, 128), F32)
            for res, offs in _by_residue(range(0, CONV_K)):
                shifted = pad_d[pl.ds(r0 + res, rc + offs[-1] - res), :]
                for off in offs:
                    k = CONV_K - 1 - off
                    acc = acc + w_ref[k:k + 1, :] * shifted[off - res:off - res + rc, :]
            da_ref[pl.ds(r0, rc), :] = acc
            for res, offs in _by_residue(range(CONV_PAD - CONV_K + 1, CONV_PAD + 1)):
                shifted = pad_a[pl.ds(r0 + res, rc + offs[-1] - res), :]
                for off in offs:
                    k = off - (CONV_PAD - CONV_K + 1)
                    part[8 * k:8 * k + 8, :] += rows8(dcc * shifted[off - res:off - res + rc, :])
            part[8 * CONV_PAD:, :] += rows8(dcc)
            return carry

        lax.fori_loop(0, s // rc, chunk, 0)
        sums = jnp.sum(part[...].reshape(CONV_PAD + 1, 8, 128), axis=1)
        dw_ref[...] = sums[0:CONV_PAD, :]
        db_ref[...] = sums[CONV_PAD:, :]

    blk = lambda r: pl.BlockSpec((r, 128), lambda j: (0, j))
    return pl.pallas_call(
        body, name="conv_bwd", grid=(cw // 128,),
        in_specs=[blk(s), blk(s), blk(CONV_K)], out_specs=(blk(s), blk(CONV_PAD), blk(1)),
        out_shape=(SDS((s, cw), F32), SDS((CONV_PAD, cw), F32), SDS((1, cw), F32)),
        scratch_shapes=[pltpu.VMEM((s + CONV_PAD, 128), F32), pltpu.VMEM((s + CONV_PAD, 128), F32),
                        pltpu.VMEM((8 * (CONV_PAD + 1), 128), F32)],
        compiler_params=_cparams("arbitrary"))(dc, a, w)


def _bwd_in(x, dh1, da, z, dzuv, g_mix, w_in3, ts):
    s, d = x.shape

    def body(x_ref, dh1_ref, da_ref, zag_ref, dzuv_ref, g_ref, w_ref, dx_ref, dw_ref, dbin_ref, dg_ref, dz_ref, acc):
        za = zag_ref[:, 0:512]
        sg = jax.nn.sigmoid(zag_ref[:, 512:1024])
        da_ = da_ref[...]
        dza = da_ * sg
        dzg = da_ * za * sg * (1.0 - sg)
        dz_ref[:, 0:512] = _bf(dza)
        dz_ref[:, 512:1024] = _bf(dzg)
        dz_ref[:, 1024:2048] = dzuv_ref[...]
        dhn = jnp.zeros((ts, d), F32)
        for b in range(N_DEV):
            dhn = dhn + _dot(dz_ref[:, 256 * b:256 * (b + 1)], w_ref[b], NT)
        hn, xh, r = _rms_fwd(x_ref[...], g_ref[...])

        @pl.when(pl.program_id(0) == 0)
        def _():
            acc[...] = jnp.zeros_like(acc)

        acc[...] += _dot(_bf(hn), dz_ref[...], TN)

        @pl.when(pl.program_id(0) == pl.num_programs(0) - 1)
        def _():
            for b in range(N_DEV):
                dw_ref[b] = _bf(acc[:, 256 * b:256 * (b + 1)])

        dxn, dg = _rms_bwd(dhn, xh, r, g_ref[...])
        dx_ref[...] = dh1_ref[...] + dxn
        _accumulate(dg_ref, dg)
        _accumulate(dbin_ref, jnp.concatenate([jnp.sum(dza, axis=0, keepdims=True),
                                               jnp.sum(dzg, axis=0, keepdims=True)], axis=1))

    return pl.pallas_call(
        body, name="bwd_in", grid=(s // ts,),
        in_specs=[_row(ts, d), _row(ts, d), _row(ts, 512), _row(ts, 1024, 0), _row(ts, 1024), _const(g_mix.shape),
                  _const(w_in3.shape)],
        out_specs=(_row(ts, d), _acc(w_in3.shape), _acc((1, 1024)), _acc((1, d))),
        out_shape=(SDS((s, d), F32), SDS(w_in3.shape, BF16), SDS((1, 1024), F32), SDS((1, d), F32)),
        scratch_shapes=[pltpu.VMEM((ts, 2048), BF16), pltpu.VMEM((d, 2048), F32)],
        compiler_params=_cparams("arbitrary"))(x, dh1, da, z, dzuv, g_mix, w_in3)


def _tn_matmul(a, b, tm, name):
    s, m = a.shape
    n = b.shape[1]
    ts = min(s, 512)
    n_s = s // ts

    def body(a_ref, b_ref, o_ref, acc):
        k = pl.program_id(1)

        @pl.when(k == 0)
        def _():
            acc[...] = jnp.zeros_like(acc)

        acc[...] += _dot(a_ref[...], b_ref[...], TN)

        @pl.when(k == n_s - 1)
        def _():
            o_ref[...] = _bf(acc[...])

    return pl.pallas_call(
        body, name=name, grid=(m // tm, n_s),
        in_specs=[pl.BlockSpec((ts, tm), lambda i, k: (k, i)), pl.BlockSpec((ts, n), lambda i, k: (k, 0))],
        out_specs=pl.BlockSpec((tm, n), lambda i, k: (i, 0)), out_shape=SDS((m, n), BF16),
        scratch_shapes=[pltpu.VMEM((tm, n), F32)],
        compiler_params=_cparams("parallel", "arbitrary"))(a, b)


def _row_tile(r, cands):
    for cand in cands:
        if r % cand == 0:
            return cand
    return r


def _sum_in_device_order(me_ref, land_ref, own_ref):
    acc = None
    for dev in range(N_DEV):
        part = jnp.where(me_ref[0] == dev, own_ref[0], land_ref[dev]).astype(F32)
        acc = part if acc is None else acc + part
    return acc


def _sum8(land, own, me, name):
    _, r, c = land.shape
    tr = _row_tile(r, (512, 256, 352, 128, 8))
    own3 = own if own.ndim == 3 else own[None]
    own_map = (lambda i, me_ref: (me_ref[0], i, 0)) if own.ndim == 3 else (lambda i, me_ref: (0, i, 0))

    def body(me_ref, land_ref, own_ref, o_ref):
        o_ref[...] = _sum_in_device_order(me_ref, land_ref, own_ref)

    return pl.pallas_call(
        body, name=name, out_shape=SDS((r, c), F32),
        grid_spec=pltpu.PrefetchScalarGridSpec(
            num_scalar_prefetch=1, grid=(r // tr,),
            in_specs=[pl.BlockSpec((N_DEV, tr, c), lambda i, me_ref: (0, i, 0)), pl.BlockSpec((1, tr, c), own_map)],
            out_specs=pl.BlockSpec((tr, c), lambda i, me_ref: (i, 0))),
        compiler_params=_cparams("parallel"))(me, land, own3)


def _adam_update(w, g, m, v):
    m2 = ADAM_B1 * m + (1.0 - ADAM_B1) * g
    v2 = ADAM_B2 * v + (1.0 - ADAM_B2) * (g * g)
    m_hat = m2 / (1.0 - ADAM_B1 ** ADAM_STEP)
    v_hat = v2 / (1.0 - ADAM_B2 ** ADAM_STEP)
    return -ADAM_LR * (m_hat / (jnp.sqrt(v_hat) + ADAM_EPS) + ADAM_WD * w), m2, v2


def _slot_spec(tr, c, mask):
    return pl.BlockSpec((1, tr, c), lambda i, me_ref: (me_ref[0] ^ mask, i, 0))


def _sum_adamw(land, parts, w, m, v, me, name, masks=ALL_PEERS):
    r, c = w.shape
    tr = _row_tile(r, (256, 128, 176, 8))
    n = len(masks)

    def body(me_ref, own_ref, *refs):
        w_ref, m_ref, v_ref, g_ref, d_ref, m2_ref, v2_ref = refs[n:]
        g = own_ref[0].astype(F32)
        for peer_ref in refs[:n]:
            g = g + peer_ref[0].astype(F32)
        g_ref[...] = g
        d_ref[...], m2_ref[...], v2_ref[...] = _adam_update(w_ref[...], g, m_ref[...], v_ref[...])

    blk = pl.BlockSpec((tr, c), lambda i, me_ref: (i, 0))
    return pl.pallas_call(
        body, name=name, out_shape=(SDS((r, c), F32),) * 4,
        grid_spec=pltpu.PrefetchScalarGridSpec(
            num_scalar_prefetch=1, grid=(r // tr,),
            in_specs=[_slot_spec(tr, c, 0)] + [_slot_spec(tr, c, mask) for mask in masks] + [blk, blk, blk],
            out_specs=(blk,) * 4),
        compiler_params=_cparams("parallel"))(me, parts, *([land] * n), w, m, v)


def _chip_sum(parts, from_sibling, me, name):
    _, r, c = parts.shape
    tr = _row_tile(r, (512, 256, 352, 128, 8))
    spec = pl.BlockSpec((1, tr, c), lambda j, i, me_ref: (me_ref[0] ^ (2 * j), i, 0))

    def body(me_ref, p_ref, q_ref, o_ref):
        o_ref[...] = _bf(p_ref[...].astype(F32) + q_ref[...].astype(F32))

    return pl.pallas_call(
        body, name=name, out_shape=SDS(parts.shape, BF16),
        grid_spec=pltpu.PrefetchScalarGridSpec(num_scalar_prefetch=1, grid=(len(CHIPS), r // tr), in_specs=[spec, spec],
                                               out_specs=spec),
        compiler_params=_cparams("parallel", "parallel"))(me, parts, from_sibling)


def _adamw_many(ws, gs, ms, vs, name):
    n = len(ws)

    def body(*refs):
        ins, outs = refs[:4 * n], refs[4 * n:]
        for i in range(n):
            w, g, m, v = (ins[j * n + i][...] for j in range(4))
            outs[i][...], outs[n + i][...], outs[2 * n + i][...] = _adam_update(w, g, m, v)

    shapes = [SDS(w.shape, F32) for w in ws]
    out = pl.pallas_call(body, name=name, out_shape=shapes * 3, compiler_params=_cparams())(*ws, *gs, *ms, *vs)
    return out[:n], out[n:2 * n], out[2 * n:]


def _pack(arrs):
    flat = jnp.concatenate([a.reshape(-1) for a in arrs])
    pad = (-flat.shape[0]) % (128 * 128)
    return jnp.pad(flat, (0, pad)).reshape(-1, 128)


def _unpack(packed, shapes):
    flat = packed.reshape(-1)
    out, off = [], 0
    for shp in shapes:
        size = 1
        for dim in shp:
            size *= dim
        out.append(flat[off:off + size].reshape(shp))
        off += size
    return out


SMALL = ("norm_mix_g", "b_in", "conv_w", "conv_b", "conv_ln_g", "conv_ln_b", "gm_ln_g", "gm_ln_b", "gm_w_s", "gm_b_s",
         "norm_xa_g", "mem_norm_g", "norm_ffn_g", "final_norm_g")
BIG = ("w_in", "w_out", "xa_wq", "xa_wkv", "xa_wo", "ffn_w_gate_up", "ffn_w_down")
WEIGHTS = ("norm_mix_g", "w_in", "b_in", "conv_w", "conv_b", "conv_ln_g", "conv_ln_b", "gm_ln_g", "gm_ln_b", "gm_w_s",
           "gm_b_s", "w_out", "norm_xa_g", "mem_norm_g", "xa_wq", "xa_wkv", "xa_wo", "norm_ffn_g", "ffn_w_gate_up",
           "ffn_w_down", "final_norm_g")


def kernel(x, mem, norm_mix_g, w_in, b_in, conv_w, conv_b, conv_ln_g, conv_ln_b, gm_ln_g, gm_ln_b, gm_w_s, gm_b_s, w_out, norm_xa_g, mem_norm_g, xa_wq, xa_wkv, xa_wo, norm_ffn_g, ffn_w_gate_up, ffn_w_down, final_norm_g, loss_target, m_norm_mix_g, m_w_in, m_b_in, m_conv_w, m_conv_b, m_conv_ln_g, m_conv_ln_b, m_gm_ln_g, m_gm_ln_b, m_gm_w_s, m_gm_b_s, m_w_out, m_norm_xa_g, m_mem_norm_g, m_xa_wq, m_xa_wkv, m_xa_wo, m_norm_ffn_g, m_ffn_w_gate_up, m_ffn_w_down, m_final_norm_g, v_norm_mix_g, v_w_in, v_b_in, v_conv_w, v_conv_b, v_conv_ln_g, v_conv_ln_b, v_gm_ln_g, v_gm_ln_b, v_gm_w_s, v_gm_b_s, v_w_out, v_norm_xa_g, v_mem_norm_g, v_xa_wq, v_xa_wkv, v_xa_wo, v_norm_ffn_g, v_ffn_w_gate_up, v_ffn_w_down, v_final_norm_g):
    w = dict(norm_mix_g=norm_mix_g, w_in=w_in, b_in=b_in, conv_w=conv_w, conv_b=conv_b, conv_ln_g=conv_ln_g,
             conv_ln_b=conv_ln_b, gm_ln_g=gm_ln_g, gm_ln_b=gm_ln_b, gm_w_s=gm_w_s, gm_b_s=gm_b_s, w_out=w_out,
             norm_xa_g=norm_xa_g, mem_norm_g=mem_norm_g, xa_wq=xa_wq, xa_wkv=xa_wkv, xa_wo=xa_wo,
             norm_ffn_g=norm_ffn_g, ffn_w_gate_up=ffn_w_gate_up, ffn_w_down=ffn_w_down, final_norm_g=final_norm_g)
    mom = dict(norm_mix_g=m_norm_mix_g, w_in=m_w_in, b_in=m_b_in, conv_w=m_conv_w, conv_b=m_conv_b,
               conv_ln_g=m_conv_ln_g, conv_ln_b=m_conv_ln_b, gm_ln_g=m_gm_ln_g, gm_ln_b=m_gm_ln_b, gm_w_s=m_gm_w_s,
               gm_b_s=m_gm_b_s, w_out=m_w_out, norm_xa_g=m_norm_xa_g, mem_norm_g=m_mem_norm_g, xa_wq=m_xa_wq,
               xa_wkv=m_xa_wkv, xa_wo=m_xa_wo, norm_ffn_g=m_norm_ffn_g, ffn_w_gate_up=m_ffn_w_gate_up,
               ffn_w_down=m_ffn_w_down, final_norm_g=m_final_norm_g)
    var = dict(norm_mix_g=v_norm_mix_g, w_in=v_w_in, b_in=v_b_in, conv_w=v_conv_w, conv_b=v_conv_b,
               conv_ln_g=v_conv_ln_g, conv_ln_b=v_conv_ln_b, gm_ln_g=v_gm_ln_g, gm_ln_b=v_gm_ln_b, gm_w_s=v_gm_w_s,
               gm_b_s=v_gm_b_s, w_out=v_w_out, norm_xa_g=v_norm_xa_g, mem_norm_g=v_mem_norm_g, xa_wq=v_xa_wq,
               xa_wkv=v_xa_wkv, xa_wo=v_xa_wo, norm_ffn_g=v_norm_ffn_g, ffn_w_gate_up=v_ffn_w_gate_up,
               ffn_w_down=v_ffn_w_down, final_norm_g=v_final_norm_g)

    me = 4 * lax.axis_index("x") + 2 * lax.axis_index("y") + lax.axis_index("c")
    s, d = x.shape[1], x.shape[2]
    xs = x.reshape(s, d)
    mems = mem.reshape(mem.shape[1], d)
    tgt = loss_target.reshape(s, d)
    ts = min(512, s)
    ts_ffn = min(256, s)
    row = lambda a: a.reshape(1, -1)

    conv_w_pad = jnp.pad(conv_w, ((0, CONV_PAD - CONV_K), (0, 128 - conv_w.shape[1])))
    first_level = (SIBLING,) + OTHER_CHIPS
    ex_first = _exchange_start([_bf(w_in), conv_w_pad], [False] * 2, "gather_start_in", first_level, own_slot=me)
    behind = lambda t: _bf(t + ex_first[4][0:1, 0:1])
    shards = [behind(xa_wkv), behind(w_out), behind(xa_wq), behind(xa_wo), behind(ffn_w_gate_up.T), behind(ffn_w_down)]
    ex_rest = _exchange_start(shards, [False] * len(shards), "gather_start_rest", first_level, own_slot=me)
    g_send, g_recv, g_src, g_land = (list(ex_first[i]) + list(ex_rest[i]) for i in range(4))
    g_tok = ex_rest[4]

    def arrived(idx, after, name):
        pick = lambda seq: [seq[i] for i in idx]
        return _wait_and_forward(pick(g_send), pick(g_recv), pick(g_src), pick(g_land), after, "gather_pass_" + name,
                                 first_level)

    def complete(handle, after, name):
        return _forward_wait(handle[0], handle[1], handle[2], after, "forward_wait_" + name)

    bst = gm_b_s.T
    wst = jnp.swapaxes(gm_w_s, 1, 2)

    h_in = arrived((0, 1), g_tok, "in")
    w_in3, conv_w8 = complete(h_in, h_in[3], "in")
    conv_w_f = conv_w8[:, :CONV_K, :conv_w.shape[1]].transpose(1, 0, 2).reshape(CONV_K, -1)
    cw = conv_w_f.shape[1]
    z, a = _fwd_in(xs, row(norm_mix_g), w_in3, row(b_in), ts)
    h_out = arrived((2, 3), z, "out")
    c = _conv_fwd(a, conv_w_f + h_out[3][0:1, 0:1], row(conv_b))
    h_xa = arrived((4, 5), c, "xa")
    wkv3, w_out3 = complete(h_out, h_xa[3], "out")
    w_out_f = w_out3.reshape(-1, d)
    kv, mn = _kv_proj(mems, row(mem_norm_g), wkv3)
    h1 = _fwd_out(xs, c, z, row(conv_ln_g), row(conv_ln_b), row(gm_ln_g), row(gm_ln_b), gm_w_s, bst, w_out_f, ts)
    h_gut = arrived((6,), h1, "gate_up")
    wq3, wo3 = complete(h_xa, h_gut[3], "xa")
    wq_f = wq3.reshape(-1, d)
    wo_f = wo3.reshape(-1, d)
    h2 = _fwd_xa(h1, row(norm_xa_g), wq_f, kv, wo_f, ts)
    h_down = arrived((7,), h2, "down")
    (wgut3,) = complete(h_gut, h_down[3], "gate_up")
    (wdown3,) = complete(h_down, wgut3, "down")
    wgut_f = wgut3.reshape(-1, d)
    wdown_f = wdown3.reshape(-1, d)
    dh3, dh3b, gu, hn2, loss_p, d_final_g = _fwd_ffn(h2, row(norm_ffn_g), wgut_f, wdown_f, row(final_norm_g), tgt,
                                                     ts_ffn)

    blocks = lambda m: m.reshape(N_DEV, -1, d)
    tok = lambda ex: ex[4][0:1, 0:1]
    me1 = me.astype(jnp.int32).reshape(1)

    def chip_sums(pair, after, tag):
        parts, handed = _pair_wait(pair[0], pair[1], pair[2], pair[3], after, "pair_wait_" + tag)
        sums = [_chip_sum(p, h, me1, "chip_sum_%s_%d" % (tag, i)) for i, (p, h) in enumerate(zip(parts, handed))]
        return _exchange_start(sums, [True] * len(sums), "scatter_start_" + tag, OTHER_CHIPS)

    dh2, act, dgu, d_ffn_g = _bwd_ffn(h2, dh3, gu, row(norm_ffn_g), wgut_f, wdown_f, ts_ffn)
    dwgut = _tn_matmul(dgu, hn2, 1408, "dw_gate_up")
    dwdown = _tn_matmul(act, dh3b, 1408, "dw_down")
    pair_ffn = _pair_start([blocks(dwgut), blocks(dwdown)], "pair_start_ffn")
    dh1, dwq, dwo, dkv, d_xa_g = _bwd_xa(h1, dh2, row(norm_xa_g) + tok(pair_ffn), wq_f, wo_f, kv, ts)
    ex_ffn = chip_sums(pair_ffn, dh1, "ffn")
    dwkv3, d_mem_g = _bwd_kv(dkv, mn, mems, row(mem_norm_g) + tok(ex_ffn), wkv3)
    pair_xa = _pair_start([blocks(dwq), blocks(dwo), dwkv3], "pair_start_xa")
    (dwout, dc, dzuv, dws, dbst, d_cln_g, d_cln_b, d_gln_g, d_gln_b, dbin_uv) = _bwd_out(
        dh1, c, z, row(conv_ln_g) + tok(pair_xa), row(conv_ln_b), row(gm_ln_g), row(gm_ln_b), gm_w_s, wst, bst,
        w_out_f, ts)
    ex_xa = chip_sums(pair_xa, dc, "xa")
    early = dict(b_in_uv=dbin_uv, conv_ln_g=d_cln_g, conv_ln_b=d_cln_b, gm_ln_g=d_gln_g, gm_ln_b=d_gln_b, gm_w_s=dws,
                 gm_b_s=dbst.T, norm_xa_g=d_xa_g, mem_norm_g=d_mem_g, norm_ffn_g=d_ffn_g, final_norm_g=d_final_g,
                 loss=loss_p)
    ex_out = _exchange_start([blocks(dwout), _pack(list(early.values())) + tok(ex_xa)], [True, False],
                             "scatter_start_out")
    da, dconv_w, dconv_b = _conv_bwd(dc, a, conv_w_f + tok(ex_out))
    dx, dw_in3, dbin_ag, d_mix_g = _bwd_in(xs, dh1, da, z, dzuv, row(norm_mix_g), w_in3, ts)
    late = dict(norm_mix_g=d_mix_g, b_in_ag=dbin_ag, conv_w=dconv_w[:CONV_K], conv_b=dconv_b)
    ex_last = _exchange_start([_pack(list(late.values())), dw_in3], [False, True], "scatter_start_in")

    def small_totals(ex, k, after, name):
        srcs, lands = _exchange_wait([ex[0][k]], [ex[1][k]], [ex[2][k]], [ex[3][k]], [False], after, name)
        return _sum8(lands[0], srcs[0], me1, "sum_" + name)

    early_tot = small_totals(ex_out, 1, ex_last[4], "small_early")
    late_tot = small_totals(ex_last, 0, early_tot, "small_late")
    early_g = dict(zip(early, _unpack(early_tot, [v.shape for v in early.values()])))
    late_g = dict(zip(late, _unpack(late_tot, [v.shape for v in late.values()])))
    loss = early_g["loss"][0, 0]
    grads = {}
    for nm in SMALL:
        if nm == "b_in":
            g = jnp.concatenate([late_g["b_in_ag"], early_g["b_in_uv"]], axis=1)
        elif nm == "conv_w":
            g = lax.dynamic_slice_in_dim(late_g[nm], me * conv_w.shape[1], conv_w.shape[1], axis=1)
        else:
            g = late_g[nm] if nm in late_g else early_g[nm]
        grads[nm] = g.reshape(w[nm].shape)

    delta, new_m, new_v = {}, {}, {}
    small = [[src[nm] for nm in SMALL] for src in (w, grads, mom, var)]
    for dst, vals in zip((delta, new_m, new_v), _adamw_many(*small, "adamw_small")):
        dst.update(zip(SMALL, vals))

    def reduced(srcs, lands, names, masks):
        for nm, src, land in zip(names, srcs, lands):
            view = (lambda t: t.T) if nm == "ffn_w_gate_up" else (lambda t: t)
            upd = _sum_adamw(land, src, view(w[nm]), view(mom[nm]), view(var[nm]), me1, "adamw_" + nm, masks)
            grads[nm], delta[nm], new_m[nm], new_v[nm] = (view(t) for t in upd)
        return new_v[names[-1]]

    after = delta[SMALL[0]]
    for ex, names, tag in ((ex_ffn, ("ffn_w_gate_up", "ffn_w_down"), "ffn"), (ex_xa, ("xa_wq", "xa_wo", "xa_wkv"), "xa")):
        srcs, lands = _exchange_wait(ex[0], ex[1], ex[2], ex[3], [True] * len(names), after, "scatter_wait_" + tag,
                                     OTHER_CHIPS)
        after = reduced(srcs, lands, names, OTHER_CHIPS)
    srcs, lands = _exchange_wait([ex_out[0][0]], [ex_out[1][0]], [ex_out[2][0]], [ex_out[3][0]], [True], after,
                                 "scatter_wait_out")
    after = reduced(srcs, lands, ("w_out",), ALL_PEERS)
    srcs, lands = _exchange_wait([ex_last[0][1]], [ex_last[1][1]], [ex_last[2][1]], [ex_last[3][1]], [True], after,
                                 "scatter_wait_in")
    reduced(srcs, lands, ("w_in",), ALL_PEERS)

    return (loss, dx.reshape(x.shape), *[grads[nm] for nm in WEIGHTS], *[delta[nm] for nm in WEIGHTS],
            *[new_m[nm] for nm in WEIGHTS], *[new_v[nm] for nm in WEIGHTS])
```

```python
import functools

import jax
import jax.numpy as jnp
from jax import lax
from jax.experimental import pallas as pl
from jax.experimental.pallas import tpu as pltpu

F32 = jnp.float32
BF16 = jnp.bfloat16
SDS = jax.ShapeDtypeStruct

N_DEV = 8
RMS_EPS = 1e-6
LN_EPS = 1e-5
CONV_K = 31
CONV_PAD = 32
CHUNK = 128
GM_HEADS = 8
XA_HEADS = 4
XA_DH = 256
GELU_K0 = 0.7978845608028654
GELU_K1 = 0.044715
ADAM_LR = 0.001
ADAM_B1 = 0.9
ADAM_B2 = 0.999
ADAM_EPS = 1e-08
ADAM_WD = 0.01
ADAM_STEP = 10
VMEM_LIMIT = 60 * 1024 * 1024

NN = (((1,), (0,)), ((), ()))
NT = (((1,), (1,)), ((), ()))
TN = (((0,), (0,)), ((), ()))


def _dot(a, b, dims=NN):
    return lax.dot_general(a, b, dims, preferred_element_type=F32)


def _bf(x):
    return x.astype(BF16)


def _cparams(*sem):
    return pltpu.CompilerParams(dimension_semantics=tuple(sem) if sem else None, vmem_limit_bytes=VMEM_LIMIT)


def _row(ts, w, col=0):
    return pl.BlockSpec((ts, w), lambda i: (i, col))


def _const(shape):
    nd = len(shape)
    return pl.BlockSpec(shape, lambda i: (0,) * nd, pipeline_mode=pl.Buffered(1))


def _acc(shape):
    nd = len(shape)
    return pl.BlockSpec(shape, lambda i: (0,) * nd)


def _rms_fwd(x, g):
    r = lax.rsqrt(jnp.mean(x * x, axis=-1, keepdims=True) + RMS_EPS)
    xh = x * r
    return xh * g, xh, r


def _rms_bwd(dy, xh, r, g):
    gdy = dy * g
    dx = r * (gdy - xh * jnp.mean(gdy * xh, axis=-1, keepdims=True))
    dg = jnp.sum(dy * xh, axis=0, keepdims=True)
    return dx, dg


def _ln_fwd(x, g, b):
    mu = jnp.mean(x, axis=-1, keepdims=True)
    xc = x - mu
    rs = lax.rsqrt(jnp.mean(xc * xc, axis=-1, keepdims=True) + LN_EPS)
    xh = xc * rs
    return xh * g + b, xh, rs


def _ln_bwd(dy, xh, rs, g):
    dxh = dy * g
    dx = rs * (dxh - jnp.mean(dxh, axis=-1, keepdims=True) - xh * jnp.mean(dxh * xh, axis=-1, keepdims=True))
    return dx, jnp.sum(dy * xh, axis=0, keepdims=True), jnp.sum(dy, axis=0, keepdims=True)


def _gelu(x):
    t = jnp.tanh(GELU_K0 * (x + GELU_K1 * (x * x * x)))
    return 0.5 * x * (1.0 + t), t


def _gelu_grad(x, t):
    return 0.5 * (1.0 + t) + 0.5 * x * (1.0 - t * t) * (GELU_K0 * (1.0 + 3.0 * GELU_K1 * x * x))


def _silu_grad(x, sg):
    return sg * (1.0 + x * (1.0 - sg))


def _by_residue(offsets):
    groups = {}
    for off in offsets:
        groups.setdefault(off % 8, []).append(off)
    return [(res, sorted(offs)) for res, offs in sorted(groups.items())]


def _accumulate(ref, val):
    @pl.when(pl.program_id(0) == 0)
    def _():
        ref[...] = jnp.zeros_like(ref)
    ref[...] += val


def _mix_masks():
    row = lax.broadcasted_iota(jnp.int32, (CHUNK, CHUNK), 0)
    col = lax.broadcasted_iota(jnp.int32, (CHUNK, CHUNK), 1)
    return row >= col, row <= col, col < (CHUNK // 2)


def _mix_fwd(vb, ws_ref, bst_ref, mixed_scr, ts):
    tril, _, lo = _mix_masks()
    for j in range(GM_HEADS // 2):
        w0 = _bf(jnp.where(tril, ws_ref[2 * j], 0.0))
        w1 = _bf(jnp.where(tril, ws_ref[2 * j + 1], 0.0))
        bias = jnp.where(lo, bst_ref[:, 2 * j:2 * j + 1], bst_ref[:, 2 * j + 1:2 * j + 2])
        for n in range(ts // CHUNK):
            v = vb[n * CHUNK:(n + 1) * CHUNK, j * 128:(j + 1) * 128]
            mixed_scr[n * CHUNK:(n + 1) * CHUNK, j * 128:(j + 1) * 128] = jnp.where(lo, _dot(w0, v), _dot(w1, v)) + bias


def _exchange(srcs, scatter, name):
    n = len(srcs)

    def body(*refs):
        src_refs, out_refs = refs[:n], refs[n:2 * n]
        send_sems, recv_sems, local_sems = refs[2 * n:]
        x, y, c = lax.axis_index("x"), lax.axis_index("y"), lax.axis_index("c")
        me = 4 * x + 2 * y + c

        def peer_of(mask):
            px = x if not (mask >> 2) & 1 else 1 - x
            py = y if not (mask >> 1) & 1 else 1 - y
            pc = c if not mask & 1 else 1 - c
            return (px, py, pc), 4 * px + 2 * py + pc

        def remote(k, mask):
            peer, pidx = peer_of(mask)
            return pltpu.make_async_remote_copy(
                src_ref=src_refs[k].at[pidx] if scatter else src_refs[k],
                dst_ref=out_refs[k].at[me],
                send_sem=send_sems.at[k, mask - 1], recv_sem=recv_sems.at[k, mask - 1],
                device_id=peer, device_id_type=pl.DeviceIdType.MESH)

        def arrival(k, mask):
            peer, pidx = peer_of(mask)
            return pltpu.make_async_remote_copy(
                src_ref=src_refs[k].at[pidx] if scatter else src_refs[k],
                dst_ref=out_refs[k].at[pidx],
                send_sem=send_sems.at[k, mask - 1], recv_sem=recv_sems.at[k, mask - 1],
                device_id=peer, device_id_type=pl.DeviceIdType.MESH)

        sends, locals_ = [], []
        for k in range(n):
            for mask in range(1, N_DEV):
                cp = remote(k, mask)
                cp.start()
                sends.append(cp)
            lc = pltpu.make_async_copy(src_refs[k].at[me] if scatter else src_refs[k], out_refs[k].at[me],
                                       local_sems.at[k])
            lc.start()
            locals_.append(lc)
        for k in range(n):
            for mask in range(1, N_DEV):
                arrival(k, mask).wait_recv()
        for cp in sends:
            cp.wait_send()
        for lc in locals_:
            lc.wait()

    outs = [SDS((N_DEV,) + tuple(s.shape[1:] if scatter else s.shape), s.dtype) for s in srcs]
    hbm = pl.BlockSpec(memory_space=pl.ANY)
    return pl.pallas_call(
        body, name=name, out_shape=outs, in_specs=[hbm] * n, out_specs=[hbm] * n,
        scratch_shapes=[pltpu.SemaphoreType.DMA((n, N_DEV - 1)), pltpu.SemaphoreType.DMA((n, N_DEV - 1)),
                        pltpu.SemaphoreType.DMA((n,))],
    )(*srcs)


def _peer_of(mask):
    x, y, c = lax.axis_index("x"), lax.axis_index("y"), lax.axis_index("c")
    px = 1 - x if (mask >> 2) & 1 else x
    py = 1 - y if (mask >> 1) & 1 else y
    pc = 1 - c if mask & 1 else c
    return (px, py, pc), 4 * px + 2 * py + pc


ALL_PEERS = tuple(range(1, N_DEV))
OTHER_CHIPS = (2, 4, 6)
CHIPS = (0,) + OTHER_CHIPS
SIBLING = 1


def _split_copy(src_ref, land_ref, send_sem, recv_sem, mask, slot, scatter, outgoing):
    x, y, c = lax.axis_index("x"), lax.axis_index("y"), lax.axis_index("c")
    me = 4 * x + 2 * y + c
    peer, pidx = _peer_of(mask)
    return pltpu.make_async_remote_copy(
        src_ref=src_ref.at[pidx] if scatter else src_ref,
        dst_ref=land_ref.at[me if outgoing else pidx],
        send_sem=send_sem.at[slot], recv_sem=recv_sem.at[slot],
        device_id=peer, device_id_type=pl.DeviceIdType.MESH)


_HBM = pl.BlockSpec(memory_space=pltpu.HBM)
_SEM = pl.BlockSpec(memory_space=pltpu.SEMAPHORE)
_EFFECT = pltpu.SideEffectType.DATAFLOW_SIDE_EFFECTING


def _per_array(masks, n):
    return [tuple(masks)] * n if isinstance(masks[0], int) else [tuple(m) for m in masks]


def _exchange_start(srcs, scatter, name, masks=ALL_PEERS, own_slot=None):
    n = len(srcs)
    masks = _per_array(masks, n)
    lands = [lax.empty((N_DEV,) + tuple(s.shape[1:] if sc else s.shape), s.dtype) for s, sc in zip(srcs, scatter)]
    if own_slot is not None:
        lands = [land if sc else _with_own(land, s, own_slot) for land, s, sc in zip(lands, srcs, scatter)]
    lands = [pltpu.with_memory_space_constraint(land, pltpu.HBM) for land in lands]
    srcs = [pltpu.with_memory_space_constraint(s, pltpu.HBM) for s in srcs]

    def body(*refs):
        src_refs, land_refs = refs[:n], refs[n:2 * n]
        send_sems, recv_sems = refs[2 * n:3 * n], refs[3 * n:4 * n]
        token = refs[-1]
        for k in range(n):
            for slot, mask in enumerate(masks[k]):
                _split_copy(src_refs[k], land_refs[k], send_sems[k], recv_sems[k], mask, slot, scatter[k], True).start()
        token[...] = jnp.zeros_like(token)

    sems = [pltpu.SemaphoreType.DMA((len(m),)) for m in masks]
    out = pl.pallas_call(
        body, name=name,
        out_shape=tuple(sems + sems + [pltpu.HBM(s.shape, s.dtype) for s in srcs]
                        + [pltpu.HBM(l.shape, l.dtype) for l in lands] + [SDS((8, 128), F32)]),
        in_specs=[_HBM] * (2 * n),
        out_specs=tuple([_SEM] * (2 * n) + [_HBM] * (2 * n) + [pl.BlockSpec(memory_space=pltpu.VMEM)]),
        input_output_aliases={i: 2 * n + i for i in range(2 * n)},
        compiler_params=pltpu.CompilerParams(has_side_effects=_EFFECT),
    )(*srcs, *lands)
    return out[:n], out[n:2 * n], out[2 * n:3 * n], out[3 * n:4 * n], out[-1]


def _exchange_wait(send_sems, recv_sems, srcs_thru, lands_thru, scatter, after, name, masks=ALL_PEERS):
    n = len(srcs_thru)
    masks = _per_array(masks, n)

    def body(*refs):
        src_refs, land_refs = refs[:n], refs[n:2 * n]
        send_refs, recv_refs = refs[2 * n:3 * n], refs[3 * n:4 * n]
        for k in range(n):
            for slot, mask in enumerate(masks[k]):
                args = (src_refs[k], land_refs[k], send_refs[k], recv_refs[k], mask, slot, scatter[k])
                _split_copy(*args, True).wait_send()
                _split_copy(*args, False).wait_recv()

    out = pl.pallas_call(
        body, name=name,
        out_shape=tuple([pltpu.HBM(s.shape, s.dtype) for s in srcs_thru]
                        + [pltpu.HBM(l.shape, l.dtype) for l in lands_thru]),
        in_specs=[_HBM] * (2 * n) + [_SEM] * (2 * n) + [pl.BlockSpec(memory_space=pl.ANY)],
        out_specs=tuple([_HBM] * (2 * n)),
        input_output_aliases={i: i for i in range(2 * n)},
        compiler_params=pltpu.CompilerParams(has_side_effects=_EFFECT),
    )(*srcs_thru, *lands_thru, *send_sems, *recv_sems, after)
    return out[:n], out[n:]


def _wait_and_forward(send_sems, recv_sems, srcs_thru, lands_thru, after, name, masks):
    n = len(lands_thru)

    def body(*refs):
        src_refs, land_refs = refs[:n], refs[n:2 * n]
        send_refs, recv_refs = refs[2 * n:3 * n], refs[3 * n:4 * n]
        outs = refs[4 * n + 1:]
        fsend, frecv, token = outs[2 * n:3 * n], outs[3 * n:4 * n], outs[-1]
        for k in range(n):
            for slot, mask in enumerate(masks):
                args = (src_refs[k], land_refs[k], send_refs[k], recv_refs[k], mask, slot, False)
                _split_copy(*args, True).wait_send()
                _split_copy(*args, False).wait_recv()
        sibling, _ = _peer_of(SIBLING)
        for k in range(n):
            for slot, mask in enumerate(OTHER_CHIPS):
                _, mine = _peer_of(mask)
                pltpu.make_async_remote_copy(
                    src_ref=land_refs[k].at[mine], dst_ref=land_refs[k].at[mine], send_sem=fsend[k].at[slot],
                    recv_sem=frecv[k].at[slot], device_id=sibling, device_id_type=pl.DeviceIdType.MESH).start()
        token[...] = jnp.zeros_like(token)

    sem = pltpu.SemaphoreType.DMA((len(OTHER_CHIPS),))
    out = pl.pallas_call(
        body, name=name,
        out_shape=tuple([pltpu.HBM(s.shape, s.dtype) for s in srcs_thru] + [pltpu.HBM(l.shape, l.dtype) for l in lands_thru]
                        + [sem] * (2 * n) + [SDS((8, 128), F32)]),
        in_specs=[_HBM] * (2 * n) + [_SEM] * (2 * n) + [pl.BlockSpec(memory_space=pl.ANY)],
        out_specs=tuple([_HBM] * (2 * n) + [_SEM] * (2 * n) + [pl.BlockSpec(memory_space=pltpu.VMEM)]),
        input_output_aliases={i: i for i in range(2 * n)},
        compiler_params=pltpu.CompilerParams(has_side_effects=_EFFECT),
    )(*srcs_thru, *lands_thru, *send_sems, *recv_sems, after)
    return out[2 * n:3 * n], out[3 * n:4 * n], out[n:2 * n], out[-1]


def _forward_wait(send_sems, recv_sems, lands, after, name):
    n = len(lands)

    def body(*refs):
        land_refs, send_refs, recv_refs = refs[:n], refs[n:2 * n], refs[2 * n:3 * n]
        sibling, _ = _peer_of(SIBLING)
        for k in range(n):
            for slot, mask in enumerate(OTHER_CHIPS):
                _, mine = _peer_of(mask)
                _, theirs = _peer_of(mask | SIBLING)
                for block, wait in ((mine, "wait_send"), (theirs, "wait_recv")):
                    getattr(pltpu.make_async_remote_copy(
                        src_ref=land_refs[k].at[block], dst_ref=land_refs[k].at[block],
                        send_sem=send_refs[k].at[slot], recv_sem=recv_refs[k].at[slot], device_id=sibling,
                        device_id_type=pl.DeviceIdType.MESH), wait)()

    return pl.pallas_call(
        body, name=name, out_shape=tuple(pltpu.HBM(l.shape, l.dtype) for l in lands),
        in_specs=[_HBM] * n + [_SEM] * (2 * n) + [pl.BlockSpec(memory_space=pl.ANY)],
        out_specs=tuple([_HBM] * n), input_output_aliases={i: i for i in range(n)},
        compiler_params=pltpu.CompilerParams(has_side_effects=_EFFECT),
    )(*lands, *send_sems, *recv_sems, after)


def _pair_copy(parts_ref, land_ref, send_sem, recv_sem, slot, chip, outgoing):
    sibling, _ = _peer_of(SIBLING)
    _, block = _peer_of(chip | SIBLING if outgoing else chip)
    return pltpu.make_async_remote_copy(
        src_ref=parts_ref.at[block], dst_ref=land_ref.at[block], send_sem=send_sem.at[slot], recv_sem=recv_sem.at[slot],
        device_id=sibling, device_id_type=pl.DeviceIdType.MESH)


def _pair_start(parts, name):
    n = len(parts)
    lands = [pltpu.with_memory_space_constraint(lax.empty(p.shape, p.dtype), pltpu.HBM) for p in parts]
    parts = [pltpu.with_memory_space_constraint(p, pltpu.HBM) for p in parts]

    def body(*refs):
        part_refs, land_refs = refs[:n], refs[n:2 * n]
        send_sems, recv_sems, token = refs[2 * n:3 * n], refs[3 * n:4 * n], refs[-1]
        for k in range(n):
            for slot, chip in enumerate(CHIPS):
                _pair_copy(part_refs[k], land_refs[k], send_sems[k], recv_sems[k], slot, chip, True).start()
        token[...] = jnp.zeros_like(token)

    sem = pltpu.SemaphoreType.DMA((len(CHIPS),))
    out = pl.pallas_call(
        body, name=name,
        out_shape=tuple([sem] * (2 * n) + [pltpu.HBM(p.shape, p.dtype) for p in parts + lands] + [SDS((8, 128), F32)]),
        in_specs=[_HBM] * (2 * n),
        out_specs=tuple([_SEM] * (2 * n) + [_HBM] * (2 * n) + [pl.BlockSpec(memory_space=pltpu.VMEM)]),
        input_output_aliases={i: 2 * n + i for i in range(2 * n)},
        compiler_params=pltpu.CompilerParams(has_side_effects=_EFFECT),
    )(*parts, *lands)
    return out[:n], out[n:2 * n], out[2 * n:3 * n], out[3 * n:4 * n], out[-1]


def _pair_wait(send_sems, recv_sems, parts_thru, lands_thru, after, name):
    n = len(parts_thru)

    def body(*refs):
        part_refs, land_refs = refs[:n], refs[n:2 * n]
        send_refs, recv_refs = refs[2 * n:3 * n], refs[3 * n:4 * n]
        for k in range(n):
            for slot, chip in enumerate(CHIPS):
                args = (part_refs[k], land_refs[k], send_refs[k], recv_refs[k], slot, chip)
                _pair_copy(*args, True).wait_send()
                _pair_copy(*args, False).wait_recv()

    out = pl.pallas_call(
        body, name=name,
        out_shape=tuple(pltpu.HBM(p.shape, p.dtype) for p in list(parts_thru) + list(lands_thru)),
        in_specs=[_HBM] * (2 * n) + [_SEM] * (2 * n) + [pl.BlockSpec(memory_space=pl.ANY)],
        out_specs=tuple([_HBM] * (2 * n)), input_output_aliases={i: i for i in range(2 * n)},
        compiler_params=pltpu.CompilerParams(has_side_effects=_EFFECT),
    )(*parts_thru, *lands_thru, *send_sems, *recv_sems, after)
    return out[:n], out[n:]


def _with_own(landed, own, me):
    return lax.dynamic_update_slice_in_dim(landed, own[None], me, axis=0)


def _kv_proj(mem, g_mem, wkv3):
    m = mem.shape[0]

    def body(mem_ref, g_ref, w_ref, kv_ref, mn_ref):
        y, _, _ = _rms_fwd(mem_ref[...], g_ref[...])
        yb = _bf(y)
        mn_ref[...] = yb
        for b in range(N_DEV):
            kv_ref[:, 256 * b:256 * (b + 1)] = _dot(yb, w_ref[b])

    return pl.pallas_call(body, name="kv_proj", out_shape=(SDS((m, 2048), F32), SDS(mem.shape, BF16)),
                          compiler_params=_cparams())(mem, g_mem, wkv3)


def _fwd_in(x, g_mix, w_in3, b_in, ts):
    s, d = x.shape

    def body(x_ref, g_ref, w_ref, b_ref, z_ref, a_ref):
        hn, _, _ = _rms_fwd(x_ref[...], g_ref[...])
        hb = _bf(hn)
        for b in range(N_DEV):
            z_ref[:, 256 * b:256 * (b + 1)] = _dot(hb, w_ref[b]) + b_ref[:, 256 * b:256 * (b + 1)]
        a_ref[...] = z_ref[:, 0:512] * jax.nn.sigmoid(z_ref[:, 512:1024])

    return pl.pallas_call(
        body, name="fwd_in", grid=(s // ts,),
        in_specs=[_row(ts, d), _const(g_mix.shape), _const(w_in3.shape), _const(b_in.shape)],
        out_specs=(_row(ts, 2048), _row(ts, 512)),
        out_shape=(SDS((s, 2048), F32), SDS((s, 512), F32)),
        compiler_params=_cparams("arbitrary"))(x, g_mix, w_in3, b_in)


def _conv_fwd(a, w, b):
    s, cw = a.shape
    rc = 256 if s % 256 == 0 else 128

    def body(a_ref, w_ref, b_ref, c_ref, pad):
        pad[0:CONV_PAD, :] = jnp.zeros((CONV_PAD, 128), F32)
        pad[CONV_PAD:, :] = a_ref[...]

        def chunk(i, carry):
            r0 = pl.multiple_of(i * rc, rc)
            acc = jnp.zeros((rc, 128), F32) + b_ref[...]
            for res, offs in _by_residue(range(CONV_PAD - CONV_K + 1, CONV_PAD + 1)):
                shifted = pad[pl.ds(r0 + res, rc + offs[-1] - res), :]
                for off in offs:
                    k = off - (CONV_PAD - CONV_K + 1)
                    acc = acc + w_ref[k:k + 1, :] * shifted[off - res:off - res + rc, :]
            c_ref[pl.ds(r0, rc), :] = acc
            return carry

        lax.fori_loop(0, s // rc, chunk, 0)

    blk = lambda r: pl.BlockSpec((r, 128), lambda j: (0, j))
    return pl.pallas_call(
        body, name="conv_fwd", grid=(cw // 128,),
        in_specs=[blk(s), blk(CONV_K), blk(1)], out_specs=blk(s), out_shape=SDS((s, cw), F32),
        scratch_shapes=[pltpu.VMEM((s + CONV_PAD, 128), F32)],
        compiler_params=_cparams("arbitrary"))(a, w, b)


def _fwd_out(x, c, z, cln_g, cln_b, gln_g, gln_b, ws, bst, w_out, ts):
    s, d = x.shape

    def body(x_ref, c_ref, zuv_ref, clg, clb, glg, glb, ws_ref, bst_ref, wo_ref, h1_ref, mixed_scr):
        cl, _, _ = _ln_fwd(c_ref[...], clg[...], clb[...])
        co = cl * jax.nn.sigmoid(cl)
        u, _ = _gelu(zuv_ref[:, 0:512])
        vg, _ = _gelu(zuv_ref[:, 512:1024])
        vln, _, _ = _ln_fwd(vg, glg[...], glb[...])
        _mix_fwd(_bf(vln), ws_ref, bst_ref, mixed_scr, ts)
        gm = u * mixed_scr[...]
        h1_ref[...] = x_ref[...] + _dot(_bf(co), wo_ref[0:512, :]) + _dot(_bf(gm), wo_ref[512:1024, :])

    return pl.pallas_call(
        body, name="fwd_out", grid=(s // ts,),
        in_specs=[_row(ts, d), _row(ts, 512), _row(ts, 1024, 1), _const(cln_g.shape), _const(cln_b.shape),
                  _const(gln_g.shape), _const(gln_b.shape), _const(ws.shape), _const(bst.shape), _const(w_out.shape)],
        out_specs=_row(ts, d), out_shape=SDS((s, d), F32),
        scratch_shapes=[pltpu.VMEM((ts, 512), F32)],
        compiler_params=_cparams("arbitrary"))(x, c, z, cln_g, cln_b, gln_g, gln_b, ws, bst, w_out)


def _softmax_rows(sc):
    m = jnp.max(sc, axis=-1, keepdims=True)
    e = jnp.exp(sc - m)
    return e / jnp.sum(e, axis=-1, keepdims=True)


def _fwd_xa(h1, g_xa, wq, kv, wo, ts):
    s, d = h1.shape
    scale = XA_DH ** -0.5

    def body(h_ref, g_ref, wq_ref, kv_ref, wo_ref, h2_ref, q_ref, o_ref, o_scr):
        hn, _, _ = _rms_fwd(h_ref[...], g_ref[...])
        q_ref[...] = _bf(_dot(_bf(hn), wq_ref[...]))
        for h in range(XA_HEADS):
            qh = q_ref[:, XA_DH * h:XA_DH * (h + 1)]
            kh = _bf(kv_ref[:, XA_DH * h:XA_DH * (h + 1)])
            vh = _bf(kv_ref[:, d + XA_DH * h:d + XA_DH * (h + 1)])
            p = _softmax_rows(_dot(qh, kh, NT) * scale)
            o_scr[:, XA_DH * h:XA_DH * (h + 1)] = _dot(_bf(p), vh)
        o_ref[...] = _bf(o_scr[...])
        h2_ref[...] = h_ref[...] + _dot(o_ref[...], wo_ref[...])

    return pl.pallas_call(
        body, name="fwd_xa", grid=(s // ts,),
        in_specs=[_row(ts, d), _const(g_xa.shape), _const(wq.shape), _const(kv.shape), _const(wo.shape)],
        out_specs=(_row(ts, d), _row(ts, d), _row(ts, d)),
        out_shape=(SDS((s, d), F32), SDS((s, d), BF16), SDS((s, d), BF16)),
        scratch_shapes=[pltpu.VMEM((ts, d), F32)],
        compiler_params=_cparams("arbitrary"))(h1, g_xa, wq, kv, wo)


def _fwd_ffn(h2, g_ffn, wgut, wdown, g_final, target, ts):
    s, d = h2.shape
    hid = wdown.shape[0]
    hc = hid // 2

    def body(h_ref, g_ref, wgu_ref, wd_ref, gf_ref, t_ref, dh3_ref, dh3b_ref, gu_ref, hn_ref, loss_ref, dgf_ref):
        hn, _, _ = _rms_fwd(h_ref[...], g_ref[...])
        hb = _bf(hn)
        hn_ref[...] = hb
        h3 = h_ref[...]
        for n in range(2):
            g = _dot(hb, wgu_ref[hc * n:hc * (n + 1), :], NT)
            u = _dot(hb, wgu_ref[hid + hc * n:hid + hc * (n + 1), :], NT)
            gu_ref[:, hc * n:hc * (n + 1)] = g
            gu_ref[:, hid + hc * n:hid + hc * (n + 1)] = u
            act = g * jax.nn.sigmoid(g) * u
            h3 = h3 + _dot(_bf(act), wd_ref[hc * n:hc * (n + 1), :])
        y, xh, r = _rms_fwd(h3, gf_ref[...])
        diff = y - t_ref[...]
        part = 0.5 * jnp.sum(jnp.mean(diff * diff, axis=-1, keepdims=True), axis=0, keepdims=True)
        _accumulate(loss_ref, jnp.zeros(loss_ref.shape, F32) + part)
        dh3, dgf = _rms_bwd(diff * (1.0 / d), xh, r, gf_ref[...])
        dh3_ref[...] = dh3
        dh3b_ref[...] = _bf(dh3)
        _accumulate(dgf_ref, dgf)

    return pl.pallas_call(
        body, name="fwd_ffn", grid=(s // ts,),
        in_specs=[_row(ts, d), _const(g_ffn.shape), _const(wgut.shape), _const(wdown.shape), _const(g_final.shape),
                  _row(ts, d)],
        out_specs=(_row(ts, d), _row(ts, d), _row(ts, 2 * hid), _row(ts, d), _acc((1, 128)), _acc((1, d))),
        out_shape=(SDS((s, d), F32), SDS((s, d), BF16), SDS((s, 2 * hid), F32), SDS((s, d), BF16), SDS((1, 128), F32),
                   SDS((1, d), F32)),
        compiler_params=_cparams("arbitrary"))(h2, g_ffn, wgut, wdown, g_final, target)


def _bwd_ffn(h2, dh3, gu, g_ffn, wgut, wdown, ts):
    s, d = h2.shape
    hid = wdown.shape[0]
    hc = hid // 2

    def body(h_ref, dh3_ref, gu_ref, g_ref, wgu_ref, wd_ref, dh2_ref, act_ref, dgu_ref, dg_ref):
        _, xh, r = _rms_fwd(h_ref[...], g_ref[...])
        db = _bf(dh3_ref[...])
        dhn = jnp.zeros((ts, d), F32)
        for n in range(2):
            wg = wgu_ref[hc * n:hc * (n + 1), :]
            wu = wgu_ref[hid + hc * n:hid + hc * (n + 1), :]
            g = gu_ref[:, hc * n:hc * (n + 1)]
            u = gu_ref[:, hid + hc * n:hid + hc * (n + 1)]
            sg = jax.nn.sigmoid(g)
            sl = g * sg
            act_ref[:, hc * n:hc * (n + 1)] = _bf(sl * u)
            dact = _dot(db, wd_ref[hc * n:hc * (n + 1), :], NT)
            dgb = _bf(dact * u * _silu_grad(g, sg))
            dub = _bf(dact * sl)
            dgu_ref[:, hc * n:hc * (n + 1)] = dgb
            dgu_ref[:, hid + hc * n:hid + hc * (n + 1)] = dub
            dhn = dhn + _dot(dgb, wg) + _dot(dub, wu)
        dx, dg = _rms_bwd(dhn, xh, r, g_ref[...])
        dh2_ref[...] = dh3_ref[...] + dx
        _accumulate(dg_ref, dg)

    return pl.pallas_call(
        body, name="bwd_ffn", grid=(s // ts,),
        in_specs=[_row(ts, d), _row(ts, d), _row(ts, 2 * hid), _const(g_ffn.shape), _const(wgut.shape),
                  _const(wdown.shape)],
        out_specs=(_row(ts, d), _row(ts, hid), _row(ts, 2 * hid), _acc((1, d))),
        out_shape=(SDS((s, d), F32), SDS((s, hid), BF16), SDS((s, 2 * hid), BF16), SDS((1, d), F32)),
        compiler_params=_cparams("arbitrary"))(h2, dh3, gu, g_ffn, wgut, wdown)


def _bwd_xa(h1, dh2, qb, ob, g_xa, wq, wo, kv, ts):
    s, d = h1.shape
    scale = XA_DH ** -0.5

    def body(h_ref, dh2_ref, q_ref, o_ref, g_ref, wq_ref, wo_ref, kv_ref, dh1_ref, dwq_ref, dwo_ref, dkv_ref, dg_ref,
             dq_scr, accq, acco):
        hn, xh, r = _rms_fwd(h_ref[...], g_ref[...])
        hb = _bf(hn)
        dh2b = _bf(dh2_ref[...])
        do = _dot(dh2b, wo_ref[...], NT)

        @pl.when(pl.program_id(0) == 0)
        def _():
            dkv_ref[...] = jnp.zeros_like(dkv_ref)
            accq[...] = jnp.zeros_like(accq)
            acco[...] = jnp.zeros_like(acco)

        for h in range(XA_HEADS):
            lo, hi = XA_DH * h, XA_DH * (h + 1)
            qh = q_ref[:, lo:hi]
            kh = _bf(kv_ref[:, lo:hi])
            vh = _bf(kv_ref[:, d + lo:d + hi])
            p = _softmax_rows(_dot(qh, kh, NT) * scale)
            pb = _bf(p)
            doh = _bf(do[:, lo:hi])
            dp = _dot(doh, vh, NT)
            ds = p * (dp - jnp.sum(p * dp, axis=-1, keepdims=True)) * scale
            dsb = _bf(ds)
            dq_scr[:, lo:hi] = _dot(dsb, kh)
            dkv_ref[:, lo:hi] += _dot(dsb, qh, TN)
            dkv_ref[:, d + lo:d + hi] += _dot(pb, doh, TN)
        dqb = _bf(dq_scr[...])
        accq[...] += _dot(hb, dqb, TN)
        acco[...] += _dot(o_ref[...], dh2b, TN)
        dx, dg = _rms_bwd(_dot(dqb, wq_ref[...], NT), xh, r, g_ref[...])
        dh1_ref[...] = dh2_ref[...] + dx
        _accumulate(dg_ref, dg)

        @pl.when(pl.program_id(0) == pl.num_programs(0) - 1)
        def _():
            dwq_ref[...] = _bf(accq[...])
            dwo_ref[...] = _bf(acco[...])

    return pl.pallas_call(
        body, name="bwd_xa", grid=(s // ts,),
        in_specs=[_row(ts, d), _row(ts, d), _row(ts, d), _row(ts, d), _const(g_xa.shape), _const(wq.shape),
                  _const(wo.shape), _const(kv.shape)],
        out_specs=(_row(ts, d), _acc((d, d)), _acc((d, d)), _acc(kv.shape), _acc((1, d))),
        out_shape=(SDS((s, d), F32), SDS((d, d), BF16), SDS((d, d), BF16), SDS(kv.shape, F32), SDS((1, d), F32)),
        scratch_shapes=[pltpu.VMEM((ts, d), F32), pltpu.VMEM((d, d), F32), pltpu.VMEM((d, d), F32)],
        compiler_params=_cparams("arbitrary"))(h1, dh2, qb, ob, g_xa, wq, wo, kv)


def _bwd_kv(dkv, mn, mem, g_mem, wkv3):
    d = mem.shape[1]

    def body(dkv_ref, mn_ref, mem_ref, g_ref, w_ref, dw_ref, dg_ref):
        dkvb = _bf(dkv_ref[...])
        dmn = jnp.zeros(mem_ref.shape, F32)
        for b in range(N_DEV):
            blk = dkvb[:, 256 * b:256 * (b + 1)]
            dmn = dmn + _dot(blk, w_ref[b], NT)
            dw_ref[b] = _bf(_dot(mn_ref[...], blk, TN))
        _, xh, r = _rms_fwd(mem_ref[...], g_ref[...])
        _, dg = _rms_bwd(dmn, xh, r, g_ref[...])
        dg_ref[...] = dg

    return pl.pallas_call(body, name="bwd_kv", out_shape=(SDS(wkv3.shape, BF16), SDS((1, d), F32)),
                          compiler_params=_cparams())(dkv, mn, mem, g_mem, wkv3)


def _bwd_out(dh1, c, z, cln_g, cln_b, gln_g, gln_b, ws, wst, bst, w_out, ts):
    s, d = dh1.shape
    nh = GM_HEADS

    def body(dh1_ref, c_ref, zuv_ref, clg, clb, glg, glb, ws_ref, wst_ref, bst_ref, wo_ref,
             dwo_ref, dc_ref, dzuv_ref, dws_ref, dbst_ref, dclg_ref, dclb_ref, dglg_ref, dglb_ref, dbin_ref,
             mixed_scr, dv_scr, acc):
        cl, chat, crs = _ln_fwd(c_ref[...], clg[...], clb[...])
        sg = jax.nn.sigmoid(cl)
        zu = zuv_ref[:, 0:512]
        zv = zuv_ref[:, 512:1024]
        u, tu = _gelu(zu)
        vg, tv = _gelu(zv)
        vln, vhat, vrs = _ln_fwd(vg, glg[...], glb[...])
        vb = _bf(vln)
        _mix_fwd(vb, ws_ref, bst_ref, mixed_scr, ts)
        mixed = mixed_scr[...]
        dh1b = _bf(dh1_ref[...])
        dcat = _dot(dh1b, wo_ref[...], NT)
        dgm = dcat[:, 512:1024]
        dc, dclg, dclb = _ln_bwd(dcat[:, 0:512] * _silu_grad(cl, sg), chat, crs, clg[...])
        dc_ref[...] = dc
        dzu = dgm * mixed * _gelu_grad(zu, tu)
        dm = dgm * u

        @pl.when(pl.program_id(0) == 0)
        def _():
            dws_ref[...] = jnp.zeros_like(dws_ref)
            dbst_ref[...] = jnp.zeros_like(dbst_ref)
            acc[...] = jnp.zeros_like(acc)

        acc[0:512, :] += _dot(_bf(cl * sg), dh1b, TN)
        acc[512:1024, :] += _dot(_bf(u * mixed), dh1b, TN)

        @pl.when(pl.program_id(0) == pl.num_programs(0) - 1)
        def _():
            dwo_ref[...] = _bf(acc[...])

        tril, triu, lo = _mix_masks()
        head = lax.broadcasted_iota(jnp.int32, (1, nh), 1)
        for j in range(nh // 2):
            w0t = _bf(jnp.where(triu, wst_ref[2 * j], 0.0))
            w1t = _bf(jnp.where(triu, wst_ref[2 * j + 1], 0.0))
            for n in range(ts // CHUNK):
                rows = slice(n * CHUNK, (n + 1) * CHUNK)
                lanes = slice(j * 128, (j + 1) * 128)
                dmc = dm[rows, lanes]
                dmb = _bf(dmc)
                dv_scr[rows, lanes] = jnp.where(lo, _dot(w0t, dmb), _dot(w1t, dmb))
                vc = vb[rows, lanes]
                d0 = jnp.where(lo, dmc, 0.0)
                d1 = dmc - d0
                dws_ref[2 * j] += jnp.where(tril, _dot(_bf(d0), vc, NT), 0.0)
                dws_ref[2 * j + 1] += jnp.where(tril, _dot(_bf(d1), vc, NT), 0.0)
                dbst_ref[...] += (jnp.sum(d0, axis=1, keepdims=True) * (head == 2 * j).astype(F32)
                                  + jnp.sum(d1, axis=1, keepdims=True) * (head == 2 * j + 1).astype(F32))
        dvg, dglg, dglb = _ln_bwd(dv_scr[...], vhat, vrs, glg[...])
        dzv = dvg * _gelu_grad(zv, tv)
        dzuv_ref[:, 0:512] = _bf(dzu)
        dzuv_ref[:, 512:1024] = _bf(dzv)
        _accumulate(dclg_ref, dclg)
        _accumulate(dclb_ref, dclb)
        _accumulate(dglg_ref, dglg)
        _accumulate(dglb_ref, dglb)
        _accumulate(dbin_ref, jnp.concatenate([jnp.sum(dzu, axis=0, keepdims=True),
                                               jnp.sum(dzv, axis=0, keepdims=True)], axis=1))

    vec = (1, 512)
    return pl.pallas_call(
        body, name="bwd_out", grid=(s // ts,),
        in_specs=[_row(ts, d), _row(ts, 512), _row(ts, 1024, 1), _const(cln_g.shape), _const(cln_b.shape),
                  _const(gln_g.shape), _const(gln_b.shape), _const(ws.shape), _const(wst.shape), _const(bst.shape),
                  _const(w_out.shape)],
        out_specs=(_acc((d, d)), _row(ts, 512), _row(ts, 1024), _acc(ws.shape), _acc(bst.shape), _acc(vec), _acc(vec),
                   _acc(vec), _acc(vec), _acc((1, 1024))),
        out_shape=(SDS((d, d), BF16), SDS((s, 512), F32), SDS((s, 1024), BF16), SDS(ws.shape, F32),
                   SDS(bst.shape, F32), SDS(vec, F32), SDS(vec, F32), SDS(vec, F32), SDS(vec, F32), SDS((1, 1024), F32)),
        scratch_shapes=[pltpu.VMEM((ts, 512), F32), pltpu.VMEM((ts, 512), F32), pltpu.VMEM((d, d), F32)],
        compiler_params=_cparams("arbitrary"))(dh1, c, z, cln_g, cln_b, gln_g, gln_b, ws, wst, bst, w_out)


def _conv_bwd(dc, a, w):
    s, cw = a.shape
    rc = 256 if s % 256 == 0 else 128

    def body(dc_ref, a_ref, w_ref, da_ref, dw_ref, db_ref, pad_a, pad_d, part):
        pad_a[0:CONV_PAD, :] = jnp.zeros((CONV_PAD, 128), F32)
        pad_a[CONV_PAD:, :] = a_ref[...]
        pad_d[0:s, :] = dc_ref[...]
        pad_d[s:, :] = jnp.zeros((CONV_PAD, 128), F32)
        part[...] = jnp.zeros_like(part)

        def rows8(v):
            return jnp.sum(v.reshape(rc // 8, 8, 128), axis=0)

        def chunk(i, carry):
            r0 = pl.multiple_of(i * rc, rc)
            dcc = pad_d[pl.ds(r0, rc), :]
            acc = jnp.zeros((rc, 128), F32)
            for res, offs in _by_residue(range(0, CONV_K)):
                shifted = pad_d[pl.ds(r0 + res, rc + offs[-1] - res), :]
                for off in offs:
                    k = CONV_K - 1 - off
                    acc = acc + w_ref[k:k + 1, :] * shifted[off - res:off - res + rc, :]
            da_ref[pl.ds(r0, rc), :] = acc
            for res, offs in _by_residue(range(CONV_PAD - CONV_K + 1, CONV_PAD + 1)):
                shifted = pad_a[pl.ds(r0 + res, rc + offs[-1] - res), :]
                for off in offs:
                    k = off - (CONV_PAD - CONV_K + 1)
                    part[8 * k:8 * k + 8, :] += rows8(dcc * shifted[off - res:off - res + rc, :])
            part[8 * CONV_PAD:, :] += rows8(dcc)
            return carry

        lax.fori_loop(0, s // rc, chunk, 0)
        sums = jnp.sum(part[...].reshape(CONV_PAD + 1, 8, 128), axis=1)
        dw_ref[...] = sums[0:CONV_PAD, :]
        db_ref[...] = sums[CONV_PAD:, :]

    blk = lambda r: pl.BlockSpec((r, 128), lambda j: (0, j))
    return pl.pallas_call(
        body, name="conv_bwd", grid=(cw // 128,),
        in_specs=[blk(s), blk(s), blk(CONV_K)], out_specs=(blk(s), blk(CONV_PAD), blk(1)),
        out_shape=(SDS((s, cw), F32), SDS((CONV_PAD, cw), F32), SDS((1, cw), F32)),
        scratch_shapes=[pltpu.VMEM((s + CONV_PAD, 128), F32), pltpu.VMEM((s + CONV_PAD, 128), F32),
                        pltpu.VMEM((8 * (CONV_PAD + 1), 128), F32)],
        compiler_params=_cparams("arbitrary"))(dc, a, w)


def _bwd_in(x, dh1, da, z, dzuv, g_mix, w_in3, ts):
    s, d = x.shape

    def body(x_ref, dh1_ref, da_ref, zag_ref, dzuv_ref, g_ref, w_ref, dx_ref, dw_ref, dbin_ref, dg_ref, dz_ref, acc):
        za = zag_ref[:, 0:512]
        sg = jax.nn.sigmoid(zag_ref[:, 512:1024])
        da_ = da_ref[...]
        dza = da_ * sg
        dzg = da_ * za * sg * (1.0 - sg)
        dz_ref[:, 0:512] = _bf(dza)
        dz_ref[:, 512:1024] = _bf(dzg)
        dz_ref[:, 1024:2048] = dzuv_ref[...]
        dhn = jnp.zeros((ts, d), F32)
        for b in range(N_DEV):
            dhn = dhn + _dot(dz_ref[:, 256 * b:256 * (b + 1)], w_ref[b], NT)
        hn, xh, r = _rms_fwd(x_ref[...], g_ref[...])

        @pl.when(pl.program_id(0) == 0)
        def _():
            acc[...] = jnp.zeros_like(acc)

        acc[...] += _dot(_bf(hn), dz_ref[...], TN)

        @pl.when(pl.program_id(0) == pl.num_programs(0) - 1)
        def _():
            for b in range(N_DEV):
                dw_ref[b] = _bf(acc[:, 256 * b:256 * (b + 1)])

        dxn, dg = _rms_bwd(dhn, xh, r, g_ref[...])
        dx_ref[...] = dh1_ref[...] + dxn
        _accumulate(dg_ref, dg)
        _accumulate(dbin_ref, jnp.concatenate([jnp.sum(dza, axis=0, keepdims=True),
                                               jnp.sum(dzg, axis=0, keepdims=True)], axis=1))

    return pl.pallas_call(
        body, name="bwd_in", grid=(s // ts,),
        in_specs=[_row(ts, d), _row(ts, d), _row(ts, 512), _row(ts, 1024, 0), _row(ts, 1024), _const(g_mix.shape),
                  _const(w_in3.shape)],
        out_specs=(_row(ts, d), _acc(w_in3.shape), _acc((1, 1024)), _acc((1, d))),
        out_shape=(SDS((s, d), F32), SDS(w_in3.shape, BF16), SDS((1, 1024), F32), SDS((1, d), F32)),
        scratch_shapes=[pltpu.VMEM((ts, 2048), BF16), pltpu.VMEM((d, 2048), F32)],
        compiler_params=_cparams("arbitrary"))(x, dh1, da, z, dzuv, g_mix, w_in3)


def _tn_matmul(a, b, tm, name):
    s, m = a.shape
    n = b.shape[1]
    ts = min(s, 1024)
    n_s = s // ts

    def body(a_ref, b_ref, o_ref, acc):
        k = pl.program_id(1)

        @pl.when(k == 0)
        def _():
            acc[...] = jnp.zeros_like(acc)

        acc[...] += _dot(a_ref[...], b_ref[...], TN)

        @pl.when(k == n_s - 1)
        def _():
            o_ref[...] = _bf(acc[...])

    return pl.pallas_call(
        body, name=name, grid=(m // tm, n_s),
        in_specs=[pl.BlockSpec((ts, tm), lambda i, k: (k, i)), pl.BlockSpec((ts, n), lambda i, k: (k, 0))],
        out_specs=pl.BlockSpec((tm, n), lambda i, k: (i, 0)), out_shape=SDS((m, n), BF16),
        scratch_shapes=[pltpu.VMEM((tm, n), F32)],
        compiler_params=_cparams("parallel", "arbitrary"))(a, b)


def _row_tile(r, cands):
    for cand in cands:
        if r % cand == 0:
            return cand
    return r


def _sum_in_device_order(me_ref, land_ref, own_ref):
    acc = None
    for dev in range(N_DEV):
        part = jnp.where(me_ref[0] == dev, own_ref[0], land_ref[dev]).astype(F32)
        acc = part if acc is None else acc + part
    return acc


def _sum8(land, own, me, name):
    _, r, c = land.shape
    tr = _row_tile(r, (512, 256, 352, 128, 8))
    own3 = own if own.ndim == 3 else own[None]
    own_map = (lambda i, me_ref: (me_ref[0], i, 0)) if own.ndim == 3 else (lambda i, me_ref: (0, i, 0))

    def body(me_ref, land_ref, own_ref, o_ref):
        o_ref[...] = _sum_in_device_order(me_ref, land_ref, own_ref)

    return pl.pallas_call(
        body, name=name, out_shape=SDS((r, c), F32),
        grid_spec=pltpu.PrefetchScalarGridSpec(
            num_scalar_prefetch=1, grid=(r // tr,),
            in_specs=[pl.BlockSpec((N_DEV, tr, c), lambda i, me_ref: (0, i, 0)), pl.BlockSpec((1, tr, c), own_map)],
            out_specs=pl.BlockSpec((tr, c), lambda i, me_ref: (i, 0))),
        compiler_params=_cparams("parallel"))(me, land, own3)


def _adam_update(w, g, m, v):
    m2 = ADAM_B1 * m + (1.0 - ADAM_B1) * g
    v2 = ADAM_B2 * v + (1.0 - ADAM_B2) * (g * g)
    m_hat = m2 / (1.0 - ADAM_B1 ** ADAM_STEP)
    v_hat = v2 / (1.0 - ADAM_B2 ** ADAM_STEP)
    return -ADAM_LR * (m_hat / (jnp.sqrt(v_hat) + ADAM_EPS) + ADAM_WD * w), m2, v2


def _slot_spec(tr, c, mask):
    return pl.BlockSpec((1, tr, c), lambda i, me_ref: (me_ref[0] ^ mask, i, 0))


def _sum_adamw(land, parts, w, m, v, me, name, masks=ALL_PEERS):
    r, c = w.shape
    tr = _row_tile(r, (256, 128, 176, 8))
    n = len(masks)

    def body(me_ref, own_ref, *refs):
        w_ref, m_ref, v_ref, g_ref, d_ref, m2_ref, v2_ref = refs[n:]
        g = own_ref[0].astype(F32)
        for peer_ref in refs[:n]:
            g = g + peer_ref[0].astype(F32)
        g_ref[...] = g
        d_ref[...], m2_ref[...], v2_ref[...] = _adam_update(w_ref[...], g, m_ref[...], v_ref[...])

    blk = pl.BlockSpec((tr, c), lambda i, me_ref: (i, 0))
    return pl.pallas_call(
        body, name=name, out_shape=(SDS((r, c), F32),) * 4,
        grid_spec=pltpu.PrefetchScalarGridSpec(
            num_scalar_prefetch=1, grid=(r // tr,),
            in_specs=[_slot_spec(tr, c, 0)] + [_slot_spec(tr, c, mask) for mask in masks] + [blk, blk, blk],
            out_specs=(blk,) * 4),
        compiler_params=_cparams("parallel"))(me, parts, *([land] * n), w, m, v)


def _chip_sum(parts, from_sibling, me, name):
    _, r, c = parts.shape
    tr = _row_tile(r, (512, 256, 352, 128, 8))
    spec = pl.BlockSpec((1, tr, c), lambda j, i, me_ref: (me_ref[0] ^ (2 * j), i, 0))

    def body(me_ref, p_ref, q_ref, o_ref):
        o_ref[...] = _bf(p_ref[...].astype(F32) + q_ref[...].astype(F32))

    return pl.pallas_call(
        body, name=name, out_shape=SDS(parts.shape, BF16),
        grid_spec=pltpu.PrefetchScalarGridSpec(num_scalar_prefetch=1, grid=(len(CHIPS), r // tr), in_specs=[spec, spec],
                                               out_specs=spec),
        compiler_params=_cparams("parallel", "parallel"))(me, parts, from_sibling)


def _adamw_many(ws, gs, ms, vs, name):
    n = len(ws)

    def body(*refs):
        ins, outs = refs[:4 * n], refs[4 * n:]
        for i in range(n):
            w, g, m, v = (ins[j * n + i][...] for j in range(4))
            outs[i][...], outs[n + i][...], outs[2 * n + i][...] = _adam_update(w, g, m, v)

    shapes = [SDS(w.shape, F32) for w in ws]
    out = pl.pallas_call(body, name=name, out_shape=shapes * 3, compiler_params=_cparams())(*ws, *gs, *ms, *vs)
    return out[:n], out[n:2 * n], out[2 * n:]


def _pack(arrs):
    flat = jnp.concatenate([a.reshape(-1) for a in arrs])
    pad = (-flat.shape[0]) % (128 * 128)
    return jnp.pad(flat, (0, pad)).reshape(-1, 128)


def _unpack(packed, shapes):
    flat = packed.reshape(-1)
    out, off = [], 0
    for shp in shapes:
        size = 1
        for dim in shp:
            size *= dim
        out.append(flat[off:off + size].reshape(shp))
        off += size
    return out


SMALL = ("norm_mix_g", "b_in", "conv_w", "conv_b", "conv_ln_g", "conv_ln_b", "gm_ln_g", "gm_ln_b", "gm_w_s", "gm_b_s",
         "norm_xa_g", "mem_norm_g", "norm_ffn_g", "final_norm_g")
BIG = ("w_in", "w_out", "xa_wq", "xa_wkv", "xa_wo", "ffn_w_gate_up", "ffn_w_down")
WEIGHTS = ("norm_mix_g", "w_in", "b_in", "conv_w", "conv_b", "conv_ln_g", "conv_ln_b", "gm_ln_g", "gm_ln_b", "gm_w_s",
           "gm_b_s", "w_out", "norm_xa_g", "mem_norm_g", "xa_wq", "xa_wkv", "xa_wo", "norm_ffn_g", "ffn_w_gate_up",
           "ffn_w_down", "final_norm_g")


def kernel(x, mem, norm_mix_g, w_in, b_in, conv_w, conv_b, conv_ln_g, conv_ln_b, gm_ln_g, gm_ln_b, gm_w_s, gm_b_s, w_out, norm_xa_g, mem_norm_g, xa_wq, xa_wkv, xa_wo, norm_ffn_g, ffn_w_gate_up, ffn_w_down, final_norm_g, loss_target, m_norm_mix_g, m_w_in, m_b_in, m_conv_w, m_conv_b, m_conv_ln_g, m_conv_ln_b, m_gm_ln_g, m_gm_ln_b, m_gm_w_s, m_gm_b_s, m_w_out, m_norm_xa_g, m_mem_norm_g, m_xa_wq, m_xa_wkv, m_xa_wo, m_norm_ffn_g, m_ffn_w_gate_up, m_ffn_w_down, m_final_norm_g, v_norm_mix_g, v_w_in, v_b_in, v_conv_w, v_conv_b, v_conv_ln_g, v_conv_ln_b, v_gm_ln_g, v_gm_ln_b, v_gm_w_s, v_gm_b_s, v_w_out, v_norm_xa_g, v_mem_norm_g, v_xa_wq, v_xa_wkv, v_xa_wo, v_norm_ffn_g, v_ffn_w_gate_up, v_ffn_w_down, v_final_norm_g):
    w = dict(norm_mix_g=norm_mix_g, w_in=w_in, b_in=b_in, conv_w=conv_w, conv_b=conv_b, conv_ln_g=conv_ln_g,
             conv_ln_b=conv_ln_b, gm_ln_g=gm_ln_g, gm_ln_b=gm_ln_b, gm_w_s=gm_w_s, gm_b_s=gm_b_s, w_out=w_out,
             norm_xa_g=norm_xa_g, mem_norm_g=mem_norm_g, xa_wq=xa_wq, xa_wkv=xa_wkv, xa_wo=xa_wo,
             norm_ffn_g=norm_ffn_g, ffn_w_gate_up=ffn_w_gate_up, ffn_w_down=ffn_w_down, final_norm_g=final_norm_g)
    mom = dict(norm_mix_g=m_norm_mix_g, w_in=m_w_in, b_in=m_b_in, conv_w=m_conv_w, conv_b=m_conv_b,
               conv_ln_g=m_conv_ln_g, conv_ln_b=m_conv_ln_b, gm_ln_g=m_gm_ln_g, gm_ln_b=m_gm_ln_b, gm_w_s=m_gm_w_s,
               gm_b_s=m_gm_b_s, w_out=m_w_out, norm_xa_g=m_norm_xa_g, mem_norm_g=m_mem_norm_g, xa_wq=m_xa_wq,
               xa_wkv=m_xa_wkv, xa_wo=m_xa_wo, norm_ffn_g=m_norm_ffn_g, ffn_w_gate_up=m_ffn_w_gate_up,
               ffn_w_down=m_ffn_w_down, final_norm_g=m_final_norm_g)
    var = dict(norm_mix_g=v_norm_mix_g, w_in=v_w_in, b_in=v_b_in, conv_w=v_conv_w, conv_b=v_conv_b,
               conv_ln_g=v_conv_ln_g, conv_ln_b=v_conv_ln_b, gm_ln_g=v_gm_ln_g, gm_ln_b=v_gm_ln_b, gm_w_s=v_gm_w_s,
               gm_b_s=v_gm_b_s, w_out=v_w_out, norm_xa_g=v_norm_xa_g, mem_norm_g=v_mem_norm_g, xa_wq=v_xa_wq,
               xa_wkv=v_xa_wkv, xa_wo=v_xa_wo, norm_ffn_g=v_norm_ffn_g, ffn_w_gate_up=v_ffn_w_gate_up,
               ffn_w_down=v_ffn_w_down, final_norm_g=v_final_norm_g)

    me = 4 * lax.axis_index("x") + 2 * lax.axis_index("y") + lax.axis_index("c")
    s, d = x.shape[1], x.shape[2]
    xs = x.reshape(s, d)
    mems = mem.reshape(mem.shape[1], d)
    tgt = loss_target.reshape(s, d)
    ts = min(512, s)
    ts_ffn = min(256, s)
    row = lambda a: a.reshape(1, -1)

    conv_w_pad = jnp.pad(conv_w, ((0, CONV_PAD - CONV_K), (0, 128 - conv_w.shape[1])))
    first_level = (SIBLING,) + OTHER_CHIPS
    ex_first = _exchange_start([_bf(w_in), conv_w_pad], [False] * 2, "gather_start_in", first_level, own_slot=me)
    behind = lambda t: _bf(t + ex_first[4][0:1, 0:1])
    shards = [behind(xa_wkv), behind(w_out), behind(xa_wq), behind(xa_wo), behind(ffn_w_gate_up.T), behind(ffn_w_down)]
    ex_rest = _exchange_start(shards, [False] * len(shards), "gather_start_rest", first_level, own_slot=me)
    g_send, g_recv, g_src, g_land = (list(ex_first[i]) + list(ex_rest[i]) for i in range(4))
    g_tok = ex_rest[4]

    def arrived(idx, after, name):
        pick = lambda seq: [seq[i] for i in idx]
        return _wait_and_forward(pick(g_send), pick(g_recv), pick(g_src), pick(g_land), after, "gather_pass_" + name,
                                 first_level)

    def complete(handle, after, name):
        return _forward_wait(handle[0], handle[1], handle[2], after, "forward_wait_" + name)

    bst = gm_b_s.T
    wst = jnp.swapaxes(gm_w_s, 1, 2)

    h_in = arrived((0, 1), g_tok, "in")
    w_in3, conv_w8 = complete(h_in, h_in[3], "in")
    conv_w_f = conv_w8[:, :CONV_K, :conv_w.shape[1]].transpose(1, 0, 2).reshape(CONV_K, -1)
    cw = conv_w_f.shape[1]
    z, a = _fwd_in(xs, row(norm_mix_g), w_in3, row(b_in), ts)
    h_out = arrived((2, 3), z, "out")
    c = _conv_fwd(a, conv_w_f + h_out[3][0:1, 0:1], row(conv_b))
    h_xa = arrived((4, 5), c, "xa")
    wkv3, w_out3 = complete(h_out, h_xa[3], "out")
    w_out_f = w_out3.reshape(-1, d)
    kv, mn = _kv_proj(mems, row(mem_norm_g), wkv3)
    h1 = _fwd_out(xs, c, z, row(conv_ln_g), row(conv_ln_b), row(gm_ln_g), row(gm_ln_b), gm_w_s, bst, w_out_f, ts)
    h_gut = arrived((6,), h1, "gate_up")
    wq3, wo3 = complete(h_xa, h_gut[3], "xa")
    wq_f = wq3.reshape(-1, d)
    wo_f = wo3.reshape(-1, d)
    h2, qb, ob = _fwd_xa(h1, row(norm_xa_g), wq_f, kv, wo_f, ts)
    h_down = arrived((7,), h2, "down")
    (wgut3,) = complete(h_gut, h_down[3], "gate_up")
    (wdown3,) = complete(h_down, wgut3, "down")
    wgut_f = wgut3.reshape(-1, d)
    wdown_f = wdown3.reshape(-1, d)
    dh3, dh3b, gu, hn2, loss_p, d_final_g = _fwd_ffn(h2, row(norm_ffn_g), wgut_f, wdown_f, row(final_norm_g), tgt,
                                                     ts_ffn)

    blocks = lambda m: m.reshape(N_DEV, -1, d)
    tok = lambda ex: ex[4][0:1, 0:1]
    me1 = me.astype(jnp.int32).reshape(1)

    def chip_sums(pair, after, tag):
        parts, handed = _pair_wait(pair[0], pair[1], pair[2], pair[3], after, "pair_wait_" + tag)
        return [_chip_sum(p, h, me1, "chip_sum_%s_%d" % (tag, i)) for i, (p, h) in enumerate(zip(parts, handed))]

    dh2, act, dgu, d_ffn_g = _bwd_ffn(h2, dh3, gu, row(norm_ffn_g), wgut_f, wdown_f, ts_ffn)
    dwgut = _tn_matmul(dgu, hn2, 1408, "dw_gate_up")
    dwdown = _tn_matmul(act, dh3b, 1408, "dw_down")
    pair_ffn = _pair_start([blocks(dwgut), blocks(dwdown)], "pair_start_ffn")
    dh1, dwq, dwo, dkv, d_xa_g = _bwd_xa(h1, dh2, qb, ob, row(norm_xa_g) + tok(pair_ffn), wq_f, wo_f, kv, ts)
    ex_ffn = _exchange_start(chip_sums(pair_ffn, dh1, "ffn"), [True] * 2, "scatter_start_ffn", OTHER_CHIPS)
    dwkv3, d_mem_g = _bwd_kv(dkv, mn, mems, row(mem_norm_g) + tok(ex_ffn), wkv3)
    pair_xa = _pair_start([blocks(dwq), blocks(dwo), dwkv3], "pair_start_xa")
    (dwout, dc, dzuv, dws, dbst, d_cln_g, d_cln_b, d_gln_g, d_gln_b, dbin_uv) = _bwd_out(
        dh1, c, z, row(conv_ln_g) + tok(pair_xa), row(conv_ln_b), row(gm_ln_g), row(gm_ln_b), gm_w_s, wst, bst,
        w_out_f, ts)
    early = dict(b_in_uv=dbin_uv, conv_ln_g=d_cln_g, conv_ln_b=d_cln_b, gm_ln_g=d_gln_g, gm_ln_b=d_gln_b, gm_w_s=dws,
                 gm_b_s=dbst.T, norm_xa_g=d_xa_g, mem_norm_g=d_mem_g, norm_ffn_g=d_ffn_g, final_norm_g=d_final_g,
                 loss=loss_p)
    ex_xa = _exchange_start(chip_sums(pair_xa, dc, "xa") + [blocks(dwout), _pack(list(early.values()))],
                            [True] * 4 + [False], "scatter_start_xa", [OTHER_CHIPS] * 3 + [ALL_PEERS] * 2)
    da, dconv_w, dconv_b = _conv_bwd(dc, a, conv_w_f + tok(ex_xa))
    dx, dw_in3, dbin_ag, d_mix_g = _bwd_in(xs, dh1, da, z, dzuv, row(norm_mix_g), w_in3, ts)
    late = dict(norm_mix_g=d_mix_g, b_in_ag=dbin_ag, conv_w=dconv_w[:CONV_K], conv_b=dconv_b)
    ex_last = _exchange_start([_pack(list(late.values())), dw_in3], [False, True], "scatter_start_in")

    picked = [(ex_ffn, 0), (ex_ffn, 1), (ex_xa, 0), (ex_xa, 1), (ex_xa, 2), (ex_xa, 3), (ex_xa, 4), (ex_last, 0)]
    names = ("ffn_w_gate_up", "ffn_w_down", "xa_wq", "xa_wo", "xa_wkv", "w_out")
    srcs, lands = _exchange_wait(*[[ex[i][k] for ex, k in picked] for i in range(4)], [True] * 6 + [False] * 2,
                                 ex_last[4], "scatter_wait", [OTHER_CHIPS] * 5 + [ALL_PEERS] * 3)
    early_tot = _sum8(lands[6], srcs[6], me1, "sum_small_early")
    late_tot = _sum8(lands[7], srcs[7], me1, "sum_small_late")
    early_g = dict(zip(early, _unpack(early_tot, [v.shape for v in early.values()])))
    late_g = dict(zip(late, _unpack(late_tot, [v.shape for v in late.values()])))
    loss = early_g["loss"][0, 0]
    grads = {}
    for nm in SMALL:
        if nm == "b_in":
            g = jnp.concatenate([late_g["b_in_ag"], early_g["b_in_uv"]], axis=1)
        elif nm == "conv_w":
            g = lax.dynamic_slice_in_dim(late_g[nm], me * conv_w.shape[1], conv_w.shape[1], axis=1)
        else:
            g = late_g[nm] if nm in late_g else early_g[nm]
        grads[nm] = g.reshape(w[nm].shape)

    delta, new_m, new_v = {}, {}, {}
    small = [[src[nm] for nm in SMALL] for src in (w, grads, mom, var)]
    for dst, vals in zip((delta, new_m, new_v), _adamw_many(*small, "adamw_small")):
        dst.update(zip(SMALL, vals))

    def reduced(nm, src, land, masks):
        view = (lambda t: t.T) if nm == "ffn_w_gate_up" else (lambda t: t)
        upd = _sum_adamw(land, src, view(w[nm]), view(mom[nm]), view(var[nm]), me1, "adamw_" + nm, masks)
        grads[nm], delta[nm], new_m[nm], new_v[nm] = (view(t) for t in upd)

    for k, nm in enumerate(names):
        reduced(nm, srcs[k], lands[k], ALL_PEERS if nm == "w_out" else OTHER_CHIPS)
    srcs, lands = _exchange_wait(*[[ex_last[i][1]] for i in range(4)], [True], new_v["w_out"], "scatter_wait_in")
    reduced("w_in", srcs[0], lands[0], ALL_PEERS)

    return (loss, dx.reshape(x.shape), *[grads[nm] for nm in WEIGHTS], *[delta[nm] for nm in WEIGHTS],
            *[new_m[nm] for nm in WEIGHTS], *[new_v[nm] for nm in WEIGHTS])
```

```python
import functools

import jax
import jax.numpy as jnp
from jax import lax
from jax.experimental import pallas as pl
from jax.experimental.pallas import tpu as pltpu

F32 = jnp.float32
BF16 = jnp.bfloat16
SDS = jax.ShapeDtypeStruct

N_DEV = 8
RMS_EPS = 1e-6
LN_EPS = 1e-5
CONV_K = 31
CONV_PAD = 32
CHUNK = 128
GM_HEADS = 8
XA_HEADS = 4
XA_DH = 256
GELU_K0 = 0.7978845608028654
GELU_K1 = 0.044715
ADAM_LR = 0.001
ADAM_B1 = 0.9
ADAM_B2 = 0.999
ADAM_EPS = 1e-08
ADAM_WD = 0.01
ADAM_STEP = 10
VMEM_LIMIT = 60 * 1024 * 1024

NN = (((1,), (0,)), ((), ()))
NT = (((1,), (1,)), ((), ()))
TN = (((0,), (0,)), ((), ()))


def _dot(a, b, dims=NN):
    return lax.dot_general(a, b, dims, preferred_element_type=F32)


def _bf(x):
    return x.astype(BF16)


def _cparams(*sem):
    return pltpu.CompilerParams(dimension_semantics=tuple(sem) if sem else None, vmem_limit_bytes=VMEM_LIMIT)


def _row(ts, w, col=0):
    return pl.BlockSpec((ts, w), lambda i: (i, col))


def _const(shape):
    nd = len(shape)
    return pl.BlockSpec(shape, lambda i: (0,) * nd, pipeline_mode=pl.Buffered(1))


def _acc(shape):
    nd = len(shape)
    return pl.BlockSpec(shape, lambda i: (0,) * nd)


def _rms_fwd(x, g):
    r = lax.rsqrt(jnp.mean(x * x, axis=-1, keepdims=True) + RMS_EPS)
    xh = x * r
    return xh * g, xh, r


def _rms_bwd(dy, xh, r, g):
    gdy = dy * g
    dx = r * (gdy - xh * jnp.mean(gdy * xh, axis=-1, keepdims=True))
    dg = jnp.sum(dy * xh, axis=0, keepdims=True)
    return dx, dg


def _ln_fwd(x, g, b):
    mu = jnp.mean(x, axis=-1, keepdims=True)
    xc = x - mu
    rs = lax.rsqrt(jnp.mean(xc * xc, axis=-1, keepdims=True) + LN_EPS)
    xh = xc * rs
    return xh * g + b, xh, rs


def _ln_bwd(dy, xh, rs, g):
    dxh = dy * g
    dx = rs * (dxh - jnp.mean(dxh, axis=-1, keepdims=True) - xh * jnp.mean(dxh * xh, axis=-1, keepdims=True))
    return dx, jnp.sum(dy * xh, axis=0, keepdims=True), jnp.sum(dy, axis=0, keepdims=True)


def _gelu(x):
    t = jnp.tanh(GELU_K0 * (x + GELU_K1 * (x * x * x)))
    return 0.5 * x * (1.0 + t), t


def _gelu_grad(x, t):
    return 0.5 * (1.0 + t) + 0.5 * x * (1.0 - t * t) * (GELU_K0 * (1.0 + 3.0 * GELU_K1 * x * x))


def _silu_grad(x, sg):
    return sg * (1.0 + x * (1.0 - sg))


def _by_residue(offsets):
    groups = {}
    for off in offsets:
        groups.setdefault(off % 8, []).append(off)
    return [(res, sorted(offs)) for res, offs in sorted(groups.items())]


def _accumulate(ref, val):
    @pl.when(pl.program_id(0) == 0)
    def _():
        ref[...] = jnp.zeros_like(ref)
    ref[...] += val


def _mix_masks():
    row = lax.broadcasted_iota(jnp.int32, (CHUNK, CHUNK), 0)
    col = lax.broadcasted_iota(jnp.int32, (CHUNK, CHUNK), 1)
    return row >= col, row <= col, col < (CHUNK // 2)


def _mix_fwd(vb, ws_ref, bst_ref, mixed_scr, ts):
    tril, _, lo = _mix_masks()
    for j in range(GM_HEADS // 2):
        w0 = _bf(jnp.where(tril, ws_ref[2 * j], 0.0))
        w1 = _bf(jnp.where(tril, ws_ref[2 * j + 1], 0.0))
        bias = jnp.where(lo, bst_ref[:, 2 * j:2 * j + 1], bst_ref[:, 2 * j + 1:2 * j + 2])
        for n in range(ts // CHUNK):
            v = vb[n * CHUNK:(n + 1) * CHUNK, j * 128:(j + 1) * 128]
            mixed_scr[n * CHUNK:(n + 1) * CHUNK, j * 128:(j + 1) * 128] = jnp.where(lo, _dot(w0, v), _dot(w1, v)) + bias


def _exchange(srcs, scatter, name):
    n = len(srcs)

    def body(*refs):
        src_refs, out_refs = refs[:n], refs[n:2 * n]
        send_sems, recv_sems, local_sems = refs[2 * n:]
        x, y, c = lax.axis_index("x"), lax.axis_index("y"), lax.axis_index("c")
        me = 4 * x + 2 * y + c

        def peer_of(mask):
            px = x if not (mask >> 2) & 1 else 1 - x
            py = y if not (mask >> 1) & 1 else 1 - y
            pc = c if not mask & 1 else 1 - c
            return (px, py, pc), 4 * px + 2 * py + pc

        def remote(k, mask):
            peer, pidx = peer_of(mask)
            return pltpu.make_async_remote_copy(
                src_ref=src_refs[k].at[pidx] if scatter else src_refs[k],
                dst_ref=out_refs[k].at[me],
                send_sem=send_sems.at[k, mask - 1], recv_sem=recv_sems.at[k, mask - 1],
                device_id=peer, device_id_type=pl.DeviceIdType.MESH)

        def arrival(k, mask):
            peer, pidx = peer_of(mask)
            return pltpu.make_async_remote_copy(
                src_ref=src_refs[k].at[pidx] if scatter else src_refs[k],
                dst_ref=out_refs[k].at[pidx],
                send_sem=send_sems.at[k, mask - 1], recv_sem=recv_sems.at[k, mask - 1],
                device_id=peer, device_id_type=pl.DeviceIdType.MESH)

        sends, locals_ = [], []
        for k in range(n):
            for mask in range(1, N_DEV):
                cp = remote(k, mask)
                cp.start()
                sends.append(cp)
            lc = pltpu.make_async_copy(src_refs[k].at[me] if scatter else src_refs[k], out_refs[k].at[me],
                                       local_sems.at[k])
            lc.start()
            locals_.append(lc)
        for k in range(n):
            for mask in range(1, N_DEV):
                arrival(k, mask).wait_recv()
        for cp in sends:
            cp.wait_send()
        for lc in locals_:
            lc.wait()

    outs = [SDS((N_DEV,) + tuple(s.shape[1:] if scatter else s.shape), s.dtype) for s in srcs]
    hbm = pl.BlockSpec(memory_space=pl.ANY)
    return pl.pallas_call(
        body, name=name, out_shape=outs, in_specs=[hbm] * n, out_specs=[hbm] * n,
        scratch_shapes=[pltpu.SemaphoreType.DMA((n, N_DEV - 1)), pltpu.SemaphoreType.DMA((n, N_DEV - 1)),
                        pltpu.SemaphoreType.DMA((n,))],
    )(*srcs)


def _peer_of(mask):
    x, y, c = lax.axis_index("x"), lax.axis_index("y"), lax.axis_index("c")
    px = 1 - x if (mask >> 2) & 1 else x
    py = 1 - y if (mask >> 1) & 1 else y
    pc = 1 - c if mask & 1 else c
    return (px, py, pc), 4 * px + 2 * py + pc


ALL_PEERS = tuple(range(1, N_DEV))
OTHER_CHIPS = (2, 4, 6)
CHIPS = (0,) + OTHER_CHIPS
SIBLING = 1


def _split_copy(src_ref, land_ref, send_sem, recv_sem, mask, slot, scatter, outgoing):
    x, y, c = lax.axis_index("x"), lax.axis_index("y"), lax.axis_index("c")
    me = 4 * x + 2 * y + c
    peer, pidx = _peer_of(mask)
    return pltpu.make_async_remote_copy(
        src_ref=src_ref.at[pidx] if scatter else src_ref,
        dst_ref=land_ref.at[me if outgoing else pidx],
        send_sem=send_sem.at[slot], recv_sem=recv_sem.at[slot],
        device_id=peer, device_id_type=pl.DeviceIdType.MESH)


_HBM = pl.BlockSpec(memory_space=pltpu.HBM)
_SEM = pl.BlockSpec(memory_space=pltpu.SEMAPHORE)
_EFFECT = pltpu.SideEffectType.DATAFLOW_SIDE_EFFECTING


def _per_array(masks, n):
    return [tuple(masks)] * n if isinstance(masks[0], int) else [tuple(m) for m in masks]


def _exchange_start(srcs, scatter, name, masks=ALL_PEERS, own_slot=None):
    n = len(srcs)
    masks = _per_array(masks, n)
    lands = [lax.empty((N_DEV,) + tuple(s.shape[1:] if sc else s.shape), s.dtype) for s, sc in zip(srcs, scatter)]
    if own_slot is not None:
        lands = [land if sc else _with_own(land, s, own_slot) for land, s, sc in zip(lands, srcs, scatter)]
    lands = [pltpu.with_memory_space_constraint(land, pltpu.HBM) for land in lands]
    srcs = [pltpu.with_memory_space_constraint(s, pltpu.HBM) for s in srcs]

    def body(*refs):
        src_refs, land_refs = refs[:n], refs[n:2 * n]
        send_sems, recv_sems = refs[2 * n:3 * n], refs[3 * n:4 * n]
        token = refs[-1]
        for k in range(n):
            for slot, mask in enumerate(masks[k]):
                _split_copy(src_refs[k], land_refs[k], send_sems[k], recv_sems[k], mask, slot, scatter[k], True).start()
        token[...] = jnp.zeros_like(token)

    sems = [pltpu.SemaphoreType.DMA((len(m),)) for m in masks]
    out = pl.pallas_call(
        body, name=name,
        out_shape=tuple(sems + sems + [pltpu.HBM(s.shape, s.dtype) for s in srcs]
                        + [pltpu.HBM(l.shape, l.dtype) for l in lands] + [SDS((8, 128), F32)]),
        in_specs=[_HBM] * (2 * n),
        out_specs=tuple([_SEM] * (2 * n) + [_HBM] * (2 * n) + [pl.BlockSpec(memory_space=pltpu.VMEM)]),
        input_output_aliases={i: 2 * n + i for i in range(2 * n)},
        compiler_params=pltpu.CompilerParams(has_side_effects=_EFFECT),
    )(*srcs, *lands)
    return out[:n], out[n:2 * n], out[2 * n:3 * n], out[3 * n:4 * n], out[-1]


def _exchange_wait(send_sems, recv_sems, srcs_thru, lands_thru, scatter, after, name, masks=ALL_PEERS):
    n = len(srcs_thru)
    masks = _per_array(masks, n)

    def body(*refs):
        src_refs, land_refs = refs[:n], refs[n:2 * n]
        send_refs, recv_refs = refs[2 * n:3 * n], refs[3 * n:4 * n]
        for k in range(n):
            for slot, mask in enumerate(masks[k]):
                args = (src_refs[k], land_refs[k], send_refs[k], recv_refs[k], mask, slot, scatter[k])
                _split_copy(*args, True).wait_send()
                _split_copy(*args, False).wait_recv()

    out = pl.pallas_call(
        body, name=name,
        out_shape=tuple([pltpu.HBM(s.shape, s.dtype) for s in srcs_thru]
                        + [pltpu.HBM(l.shape, l.dtype) for l in lands_thru]),
        in_specs=[_HBM] * (2 * n) + [_SEM] * (2 * n) + [pl.BlockSpec(memory_space=pl.ANY)],
        out_specs=tuple([_HBM] * (2 * n)),
        input_output_aliases={i: i for i in range(2 * n)},
        compiler_params=pltpu.CompilerParams(has_side_effects=_EFFECT),
    )(*srcs_thru, *lands_thru, *send_sems, *recv_sems, after)
    return out[:n], out[n:]


def _wait_and_forward(send_sems, recv_sems, srcs_thru, lands_thru, after, name, masks):
    n = len(lands_thru)

    def body(*refs):
        src_refs, land_refs = refs[:n], refs[n:2 * n]
        send_refs, recv_refs = refs[2 * n:3 * n], refs[3 * n:4 * n]
        outs = refs[4 * n + 1:]
        fsend, frecv, token = outs[2 * n:3 * n], outs[3 * n:4 * n], outs[-1]
        for k in range(n):
            for slot, mask in enumerate(masks):
                args = (src_refs[k], land_refs[k], send_refs[k], recv_refs[k], mask, slot, False)
                _split_copy(*args, True).wait_send()
                _split_copy(*args, False).wait_recv()
        sibling, _ = _peer_of(SIBLING)
        for k in range(n):
            for slot, mask in enumerate(OTHER_CHIPS):
                _, mine = _peer_of(mask)
                pltpu.make_async_remote_copy(
                    src_ref=land_refs[k].at[mine], dst_ref=land_refs[k].at[mine], send_sem=fsend[k].at[slot],
                    recv_sem=frecv[k].at[slot], device_id=sibling, device_id_type=pl.DeviceIdType.MESH).start()
        token[...] = jnp.zeros_like(token)

    sem = pltpu.SemaphoreType.DMA((len(OTHER_CHIPS),))
    out = pl.pallas_call(
        body, name=name,
        out_shape=tuple([pltpu.HBM(s.shape, s.dtype) for s in srcs_thru] + [pltpu.HBM(l.shape, l.dtype) for l in lands_thru]
                        + [sem] * (2 * n) + [SDS((8, 128), F32)]),
        in_specs=[_HBM] * (2 * n) + [_SEM] * (2 * n) + [pl.BlockSpec(memory_space=pl.ANY)],
        out_specs=tuple([_HBM] * (2 * n) + [_SEM] * (2 * n) + [pl.BlockSpec(memory_space=pltpu.VMEM)]),
        input_output_aliases={i: i for i in range(2 * n)},
        compiler_params=pltpu.CompilerParams(has_side_effects=_EFFECT),
    )(*srcs_thru, *lands_thru, *send_sems, *recv_sems, after)
    return out[2 * n:3 * n], out[3 * n:4 * n], out[n:2 * n], out[-1]


def _forward_wait(send_sems, recv_sems, lands, after, name):
    n = len(lands)

    def body(*refs):
        land_refs, send_refs, recv_refs = refs[:n], refs[n:2 * n], refs[2 * n:3 * n]
        sibling, _ = _peer_of(SIBLING)
        for k in range(n):
            for slot, mask in enumerate(OTHER_CHIPS):
                _, mine = _peer_of(mask)
                _, theirs = _peer_of(mask | SIBLING)
                for block, wait in ((mine, "wait_send"), (theirs, "wait_recv")):
                    getattr(pltpu.make_async_remote_copy(
                        src_ref=land_refs[k].at[block], dst_ref=land_refs[k].at[block],
                        send_sem=send_refs[k].at[slot], recv_sem=recv_refs[k].at[slot], device_id=sibling,
                        device_id_type=pl.DeviceIdType.MESH), wait)()

    return pl.pallas_call(
        body, name=name, out_shape=tuple(pltpu.HBM(l.shape, l.dtype) for l in lands),
        in_specs=[_HBM] * n + [_SEM] * (2 * n) + [pl.BlockSpec(memory_space=pl.ANY)],
        out_specs=tuple([_HBM] * n), input_output_aliases={i: i for i in range(n)},
        compiler_params=pltpu.CompilerParams(has_side_effects=_EFFECT),
    )(*lands, *send_sems, *recv_sems, after)


def _pair_copy(parts_ref, land_ref, send_sem, recv_sem, slot, chip, outgoing):
    sibling, _ = _peer_of(SIBLING)
    _, block = _peer_of(chip | SIBLING if outgoing else chip)
    return pltpu.make_async_remote_copy(
        src_ref=parts_ref.at[block], dst_ref=land_ref.at[block], send_sem=send_sem.at[slot], recv_sem=recv_sem.at[slot],
        device_id=sibling, device_id_type=pl.DeviceIdType.MESH)


def _pair_start(parts, name):
    n = len(parts)
    lands = [pltpu.with_memory_space_constraint(lax.empty(p.shape, p.dtype), pltpu.HBM) for p in parts]
    parts = [pltpu.with_memory_space_constraint(p, pltpu.HBM) for p in parts]

    def body(*refs):
        part_refs, land_refs = refs[:n], refs[n:2 * n]
        send_sems, recv_sems, token = refs[2 * n:3 * n], refs[3 * n:4 * n], refs[-1]
        for k in range(n):
            for slot, chip in enumerate(CHIPS):
                _pair_copy(part_refs[k], land_refs[k], send_sems[k], recv_sems[k], slot, chip, True).start()
        token[...] = jnp.zeros_like(token)

    sem = pltpu.SemaphoreType.DMA((len(CHIPS),))
    out = pl.pallas_call(
        body, name=name,
        out_shape=tuple([sem] * (2 * n) + [pltpu.HBM(p.shape, p.dtype) for p in parts + lands] + [SDS((8, 128), F32)]),
        in_specs=[_HBM] * (2 * n),
        out_specs=tuple([_SEM] * (2 * n) + [_HBM] * (2 * n) + [pl.BlockSpec(memory_space=pltpu.VMEM)]),
        input_output_aliases={i: 2 * n + i for i in range(2 * n)},
        compiler_params=pltpu.CompilerParams(has_side_effects=_EFFECT),
    )(*parts, *lands)
    return out[:n], out[n:2 * n], out[2 * n:3 * n], out[3 * n:4 * n], out[-1]


def _pair_wait(send_sems, recv_sems, parts_thru, lands_thru, after, name):
    n = len(parts_thru)

    def body(*refs):
        part_refs, land_refs = refs[:n], refs[n:2 * n]
        send_refs, recv_refs = refs[2 * n:3 * n], refs[3 * n:4 * n]
        for k in range(n):
            for slot, chip in enumerate(CHIPS):
                args = (part_refs[k], land_refs[k], send_refs[k], recv_refs[k], slot, chip)
                _pair_copy(*args, True).wait_send()
                _pair_copy(*args, False).wait_recv()

    out = pl.pallas_call(
        body, name=name,
        out_shape=tuple(pltpu.HBM(p.shape, p.dtype) for p in list(parts_thru) + list(lands_thru)),
        in_specs=[_HBM] * (2 * n) + [_SEM] * (2 * n) + [pl.BlockSpec(memory_space=pl.ANY)],
        out_specs=tuple([_HBM] * (2 * n)), input_output_aliases={i: i for i in range(2 * n)},
        compiler_params=pltpu.CompilerParams(has_side_effects=_EFFECT),
    )(*parts_thru, *lands_thru, *send_sems, *recv_sems, after)
    return out[:n], out[n:]


def _with_own(landed, own, me):
    return lax.dynamic_update_slice_in_dim(landed, own[None], me, axis=0)


def _kv_proj(mem, g_mem, wkv3):
    m = mem.shape[0]

    def body(mem_ref, g_ref, w_ref, kv_ref, mn_ref):
        y, _, _ = _rms_fwd(mem_ref[...], g_ref[...])
        yb = _bf(y)
        mn_ref[...] = yb
        for b in range(N_DEV):
            kv_ref[:, 256 * b:256 * (b + 1)] = _dot(yb, w_ref[b])

    return pl.pallas_call(body, name="kv_proj", out_shape=(SDS((m, 2048), F32), SDS(mem.shape, BF16)),
                          compiler_params=_cparams())(mem, g_mem, wkv3)


def _fwd_in(x, g_mix, w_in3, b_in, ts):
    s, d = x.shape

    def body(x_ref, g_ref, w_ref, b_ref, z_ref, a_ref):
        hn, _, _ = _rms_fwd(x_ref[...], g_ref[...])
        hb = _bf(hn)
        for b in range(N_DEV):
            z_ref[:, 256 * b:256 * (b + 1)] = _dot(hb, w_ref[b]) + b_ref[:, 256 * b:256 * (b + 1)]
        a_ref[...] = z_ref[:, 0:512] * jax.nn.sigmoid(z_ref[:, 512:1024])

    return pl.pallas_call(
        body, name="fwd_in", grid=(s // ts,),
        in_specs=[_row(ts, d), _const(g_mix.shape), _const(w_in3.shape), _const(b_in.shape)],
        out_specs=(_row(ts, 2048), _row(ts, 512)),
        out_shape=(SDS((s, 2048), F32), SDS((s, 512), F32)),
        compiler_params=_cparams("arbitrary"))(x, g_mix, w_in3, b_in)


def _conv_fwd(a, w, b):
    s, cw = a.shape
    rc = 256 if s % 256 == 0 else 128

    def body(a_ref, w_ref, b_ref, c_ref, pad):
        pad[0:CONV_PAD, :] = jnp.zeros((CONV_PAD, 128), F32)
        pad[CONV_PAD:, :] = a_ref[...]

        def chunk(i, carry):
            r0 = pl.multiple_of(i * rc, rc)
            acc = jnp.zeros((rc, 128), F32) + b_ref[...]
            for res, offs in _by_residue(range(CONV_PAD - CONV_K + 1, CONV_PAD + 1)):
                shifted = pad[pl.ds(r0 + res, rc + offs[-1] - res), :]
                for off in offs:
                    k = off - (CONV_PAD - CONV_K + 1)
                    acc = acc + w_ref[k:k + 1, :] * shifted[off - res:off - res + rc, :]
            c_ref[pl.ds(r0, rc), :] = acc
            return carry

        lax.fori_loop(0, s // rc, chunk, 0)

    blk = lambda r: pl.BlockSpec((r, 128), lambda j: (0, j))
    return pl.pallas_call(
        body, name="conv_fwd", grid=(cw // 128,),
        in_specs=[blk(s), blk(CONV_K), blk(1)], out_specs=blk(s), out_shape=SDS((s, cw), F32),
        scratch_shapes=[pltpu.VMEM((s + CONV_PAD, 128), F32)],
        compiler_params=_cparams("arbitrary"))(a, w, b)


def _fwd_out(x, c, z, cln_g, cln_b, gln_g, gln_b, ws, bst, w_out, ts):
    s, d = x.shape

    def body(x_ref, c_ref, zuv_ref, clg, clb, glg, glb, ws_ref, bst_ref, wo_ref, h1_ref, mixed_scr):
        cl, _, _ = _ln_fwd(c_ref[...], clg[...], clb[...])
        co = cl * jax.nn.sigmoid(cl)
        u, _ = _gelu(zuv_ref[:, 0:512])
        vg, _ = _gelu(zuv_ref[:, 512:1024])
        vln, _, _ = _ln_fwd(vg, glg[...], glb[...])
        _mix_fwd(_bf(vln), ws_ref, bst_ref, mixed_scr, ts)
        gm = u * mixed_scr[...]
        h1_ref[...] = x_ref[...] + _dot(_bf(co), wo_ref[0:512, :]) + _dot(_bf(gm), wo_ref[512:1024, :])

    return pl.pallas_call(
        body, name="fwd_out", grid=(s // ts,),
        in_specs=[_row(ts, d), _row(ts, 512), _row(ts, 1024, 1), _const(cln_g.shape), _const(cln_b.shape),
                  _const(gln_g.shape), _const(gln_b.shape), _const(ws.shape), _const(bst.shape), _const(w_out.shape)],
        out_specs=_row(ts, d), out_shape=SDS((s, d), F32),
        scratch_shapes=[pltpu.VMEM((ts, 512), F32)],
        compiler_params=_cparams("arbitrary"))(x, c, z, cln_g, cln_b, gln_g, gln_b, ws, bst, w_out)


def _softmax_rows(sc):
    m = jnp.max(sc, axis=-1, keepdims=True)
    e = jnp.exp(sc - m)
    return e / jnp.sum(e, axis=-1, keepdims=True)


def _fwd_xa(h1, g_xa, wq, kv, wo, ts):
    s, d = h1.shape
    scale = XA_DH ** -0.5

    def body(h_ref, g_ref, wq_ref, kv_ref, wo_ref, h2_ref, q_ref, o_ref, o_scr):
        hn, _, _ = _rms_fwd(h_ref[...], g_ref[...])
        q_ref[...] = _bf(_dot(_bf(hn), wq_ref[...]))
        for h in range(XA_HEADS):
            qh = q_ref[:, XA_DH * h:XA_DH * (h + 1)]
            kh = _bf(kv_ref[:, XA_DH * h:XA_DH * (h + 1)])
            vh = _bf(kv_ref[:, d + XA_DH * h:d + XA_DH * (h + 1)])
            p = _softmax_rows(_dot(qh, kh, NT) * scale)
            o_scr[:, XA_DH * h:XA_DH * (h + 1)] = _dot(_bf(p), vh)
        o_ref[...] = _bf(o_scr[...])
        h2_ref[...] = h_ref[...] + _dot(o_ref[...], wo_ref[...])

    return pl.pallas_call(
        body, name="fwd_xa", grid=(s // ts,),
        in_specs=[_row(ts, d), _const(g_xa.shape), _const(wq.shape), _const(kv.shape), _const(wo.shape)],
        out_specs=(_row(ts, d), _row(ts, d), _row(ts, d)),
        out_shape=(SDS((s, d), F32), SDS((s, d), BF16), SDS((s, d), BF16)),
        scratch_shapes=[pltpu.VMEM((ts, d), F32)],
        compiler_params=_cparams("arbitrary"))(h1, g_xa, wq, kv, wo)


def _fwd_ffn(h2, g_ffn, wgut, wdown, g_final, target, ts):
    s, d = h2.shape
    hid = wdown.shape[0]
    hc = hid // 2

    def body(h_ref, g_ref, wgu_ref, wd_ref, gf_ref, t_ref, dh3_ref, dh3b_ref, gu_ref, hn_ref, loss_ref, dgf_ref):
        hn, _, _ = _rms_fwd(h_ref[...], g_ref[...])
        hb = _bf(hn)
        hn_ref[...] = hb
        h3 = h_ref[...]
        for n in range(2):
            g = _dot(hb, wgu_ref[hc * n:hc * (n + 1), :], NT)
            u = _dot(hb, wgu_ref[hid + hc * n:hid + hc * (n + 1), :], NT)
            gu_ref[:, hc * n:hc * (n + 1)] = g
            gu_ref[:, hid + hc * n:hid + hc * (n + 1)] = u
            act = g * jax.nn.sigmoid(g) * u
            h3 = h3 + _dot(_bf(act), wd_ref[hc * n:hc * (n + 1), :])
        y, xh, r = _rms_fwd(h3, gf_ref[...])
        diff = y - t_ref[...]
        part = 0.5 * jnp.sum(jnp.mean(diff * diff, axis=-1, keepdims=True), axis=0, keepdims=True)
        _accumulate(loss_ref, jnp.zeros(loss_ref.shape, F32) + part)
        dh3, dgf = _rms_bwd(diff * (1.0 / d), xh, r, gf_ref[...])
        dh3_ref[...] = dh3
        dh3b_ref[...] = _bf(dh3)
        _accumulate(dgf_ref, dgf)

    return pl.pallas_call(
        body, name="fwd_ffn", grid=(s // ts,),
        in_specs=[_row(ts, d), _const(g_ffn.shape), _const(wgut.shape), _const(wdown.shape), _const(g_final.shape),
                  _row(ts, d)],
        out_specs=(_row(ts, d), _row(ts, d), _row(ts, 2 * hid), _row(ts, d), _acc((1, 128)), _acc((1, d))),
        out_shape=(SDS((s, d), F32), SDS((s, d), BF16), SDS((s, 2 * hid), F32), SDS((s, d), BF16), SDS((1, 128), F32),
                   SDS((1, d), F32)),
        compiler_params=_cparams("arbitrary"))(h2, g_ffn, wgut, wdown, g_final, target)


def _bwd_ffn(h2, dh3, gu, g_ffn, wgut, wdown, ts):
    s, d = h2.shape
    hid = wdown.shape[0]
    hc = hid // 2

    def body(h_ref, dh3_ref, gu_ref, g_ref, wgu_ref, wd_ref, dh2_ref, act_ref, dgu_ref, dg_ref):
        _, xh, r = _rms_fwd(h_ref[...], g_ref[...])
        db = _bf(dh3_ref[...])
        dhn = jnp.zeros((ts, d), F32)
        for n in range(2):
            wg = wgu_ref[hc * n:hc * (n + 1), :]
            wu = wgu_ref[hid + hc * n:hid + hc * (n + 1), :]
            g = gu_ref[:, hc * n:hc * (n + 1)]
            u = gu_ref[:, hid + hc * n:hid + hc * (n + 1)]
            sg = jax.nn.sigmoid(g)
            sl = g * sg
            act_ref[:, hc * n:hc * (n + 1)] = _bf(sl * u)
            dact = _dot(db, wd_ref[hc * n:hc * (n + 1), :], NT)
            dgb = _bf(dact * u * _silu_grad(g, sg))
            dub = _bf(dact * sl)
            dgu_ref[:, hc * n:hc * (n + 1)] = dgb
            dgu_ref[:, hid + hc * n:hid + hc * (n + 1)] = dub
            dhn = dhn + _dot(dgb, wg) + _dot(dub, wu)
        dx, dg = _rms_bwd(dhn, xh, r, g_ref[...])
        dh2_ref[...] = dh3_ref[...] + dx
        _accumulate(dg_ref, dg)

    return pl.pallas_call(
        body, name="bwd_ffn", grid=(s // ts,),
        in_specs=[_row(ts, d), _row(ts, d), _row(ts, 2 * hid), _const(g_ffn.shape), _const(wgut.shape),
                  _const(wdown.shape)],
        out_specs=(_row(ts, d), _row(ts, hid), _row(ts, 2 * hid), _acc((1, d))),
        out_shape=(SDS((s, d), F32), SDS((s, hid), BF16), SDS((s, 2 * hid), BF16), SDS((1, d), F32)),
        compiler_params=_cparams("arbitrary"))(h2, dh3, gu, g_ffn, wgut, wdown)


def _bwd_xa(h1, dh2, qb, ob, g_xa, wq, wo, kv, ts):
    s, d = h1.shape
    scale = XA_DH ** -0.5

    def body(h_ref, dh2_ref, q_ref, o_ref, g_ref, wq_ref, wo_ref, kv_ref, dh1_ref, dwq_ref, dwo_ref, dkv_ref, dg_ref,
             dq_scr, accq, acco):
        hn, xh, r = _rms_fwd(h_ref[...], g_ref[...])
        hb = _bf(hn)
        dh2b = _bf(dh2_ref[...])
        do = _dot(dh2b, wo_ref[...], NT)

        @pl.when(pl.program_id(0) == 0)
        def _():
            dkv_ref[...] = jnp.zeros_like(dkv_ref)
            accq[...] = jnp.zeros_like(accq)
            acco[...] = jnp.zeros_like(acco)

        for h in range(XA_HEADS):
            lo, hi = XA_DH * h, XA_DH * (h + 1)
            qh = q_ref[:, lo:hi]
            kh = _bf(kv_ref[:, lo:hi])
            vh = _bf(kv_ref[:, d + lo:d + hi])
            p = _softmax_rows(_dot(qh, kh, NT) * scale)
            pb = _bf(p)
            doh = _bf(do[:, lo:hi])
            dp = _dot(doh, vh, NT)
            ds = p * (dp - jnp.sum(p * dp, axis=-1, keepdims=True)) * scale
            dsb = _bf(ds)
            dq_scr[:, lo:hi] = _dot(dsb, kh)
            dkv_ref[:, lo:hi] += _dot(dsb, qh, TN)
            dkv_ref[:, d + lo:d + hi] += _dot(pb, doh, TN)
        dqb = _bf(dq_scr[...])
        accq[...] += _dot(hb, dqb, TN)
        acco[...] += _dot(o_ref[...], dh2b, TN)
        dx, dg = _rms_bwd(_dot(dqb, wq_ref[...], NT), xh, r, g_ref[...])
        dh1_ref[...] = dh2_ref[...] + dx
        _accumulate(dg_ref, dg)

        @pl.when(pl.program_id(0) == pl.num_programs(0) - 1)
        def _():
            dwq_ref[...] = _bf(accq[...])
            dwo_ref[...] = _bf(acco[...])

    return pl.pallas_call(
        body, name="bwd_xa", grid=(s // ts,),
        in_specs=[_row(ts, d), _row(ts, d), _row(ts, d), _row(ts, d), _const(g_xa.shape), _const(wq.shape),
                  _const(wo.shape), _const(kv.shape)],
        out_specs=(_row(ts, d), _acc((d, d)), _acc((d, d)), _acc(kv.shape), _acc((1, d))),
        out_shape=(SDS((s, d), F32), SDS((d, d), BF16), SDS((d, d), BF16), SDS(kv.shape, F32), SDS((1, d), F32)),
        scratch_shapes=[pltpu.VMEM((ts, d), F32), pltpu.VMEM((d, d), F32), pltpu.VMEM((d, d), F32)],
        compiler_params=_cparams("arbitrary"))(h1, dh2, qb, ob, g_xa, wq, wo, kv)


def _bwd_kv(dkv, mn, mem, g_mem, wkv3):
    d = mem.shape[1]

    def body(dkv_ref, mn_ref, mem_ref, g_ref, w_ref, dw_ref, dg_ref):
        dkvb = _bf(dkv_ref[...])
        dmn = jnp.zeros(mem_ref.shape, F32)
        for b in range(N_DEV):
            blk = dkvb[:, 256 * b:256 * (b + 1)]
            dmn = dmn + _dot(blk, w_ref[b], NT)
            dw_ref[b] = _bf(_dot(mn_ref[...], blk, TN))
        _, xh, r = _rms_fwd(mem_ref[...], g_ref[...])
        _, dg = _rms_bwd(dmn, xh, r, g_ref[...])
        dg_ref[...] = dg

    return pl.pallas_call(body, name="bwd_kv", out_shape=(SDS(wkv3.shape, BF16), SDS((1, d), F32)),
                          compiler_params=_cparams())(dkv, mn, mem, g_mem, wkv3)


def _bwd_out(dh1, c, z, cln_g, cln_b, gln_g, gln_b, ws, wst, bst, w_out, ts):
    s, d = dh1.shape
    nh = GM_HEADS

    def body(dh1_ref, c_ref, zuv_ref, clg, clb, glg, glb, ws_ref, wst_ref, bst_ref, wo_ref,
             dwo_ref, dc_ref, dzuv_ref, dws_ref, dbst_ref, dclg_ref, dclb_ref, dglg_ref, dglb_ref, dbin_ref,
             mixed_scr, dv_scr, acc):
        cl, chat, crs = _ln_fwd(c_ref[...], clg[...], clb[...])
        sg = jax.nn.sigmoid(cl)
        zu = zuv_ref[:, 0:512]
        zv = zuv_ref[:, 512:1024]
        u, tu = _gelu(zu)
        vg, tv = _gelu(zv)
        vln, vhat, vrs = _ln_fwd(vg, glg[...], glb[...])
        vb = _bf(vln)
        _mix_fwd(vb, ws_ref, bst_ref, mixed_scr, ts)
        mixed = mixed_scr[...]
        dh1b = _bf(dh1_ref[...])
        dcat = _dot(dh1b, wo_ref[...], NT)
        dgm = dcat[:, 512:1024]
        dc, dclg, dclb = _ln_bwd(dcat[:, 0:512] * _silu_grad(cl, sg), chat, crs, clg[...])
        dc_ref[...] = dc
        dzu = dgm * mixed * _gelu_grad(zu, tu)
        dm = dgm * u

        @pl.when(pl.program_id(0) == 0)
        def _():
            dws_ref[...] = jnp.zeros_like(dws_ref)
            dbst_ref[...] = jnp.zeros_like(dbst_ref)
            acc[...] = jnp.zeros_like(acc)

        acc[0:512, :] += _dot(_bf(cl * sg), dh1b, TN)
        acc[512:1024, :] += _dot(_bf(u * mixed), dh1b, TN)

        @pl.when(pl.program_id(0) == pl.num_programs(0) - 1)
        def _():
            dwo_ref[...] = _bf(acc[...])

        tril, triu, lo = _mix_masks()
        head = lax.broadcasted_iota(jnp.int32, (1, nh), 1)
        for j in range(nh // 2):
            w0t = _bf(jnp.where(triu, wst_ref[2 * j], 0.0))
            w1t = _bf(jnp.where(triu, wst_ref[2 * j + 1], 0.0))
            for n in range(ts // CHUNK):
                rows = slice(n * CHUNK, (n + 1) * CHUNK)
                lanes = slice(j * 128, (j + 1) * 128)
                dmc = dm[rows, lanes]
                dmb = _bf(dmc)
                dv_scr[rows, lanes] = jnp.where(lo, _dot(w0t, dmb), _dot(w1t, dmb))
                vc = vb[rows, lanes]
                d0 = jnp.where(lo, dmc, 0.0)
                d1 = dmc - d0
                dws_ref[2 * j] += jnp.where(tril, _dot(_bf(d0), vc, NT), 0.0)
                dws_ref[2 * j + 1] += jnp.where(tril, _dot(_bf(d1), vc, NT), 0.0)
                dbst_ref[...] += (jnp.sum(d0, axis=1, keepdims=True) * (head == 2 * j).astype(F32)
                                  + jnp.sum(d1, axis=1, keepdims=True) * (head == 2 * j + 1).astype(F32))
        dvg, dglg, dglb = _ln_bwd(dv_scr[...], vhat, vrs, glg[...])
        dzv = dvg * _gelu_grad(zv, tv)
        dzuv_ref[:, 0:512] = _bf(dzu)
        dzuv_ref[:, 512:1024] = _bf(dzv)
        _accumulate(dclg_ref, dclg)
        _accumulate(dclb_ref, dclb)
        _accumulate(dglg_ref, dglg)
        _accumulate(dglb_ref, dglb)
        _accumulate(dbin_ref, jnp.concatenate([jnp.sum(dzu, axis=0, keepdims=True),
                                               jnp.sum(dzv, axis=0, keepdims=True)], axis=1))

    vec = (1, 512)
    return pl.pallas_call(
        body, name="bwd_out", grid=(s // ts,),
        in_specs=[_row(ts, d), _row(ts, 512), _row(ts, 1024, 1), _const(cln_g.shape), _const(cln_b.shape),
                  _const(gln_g.shape), _const(gln_b.shape), _const(ws.shape), _const(wst.shape), _const(bst.shape),
                  _const(w_out.shape)],
        out_specs=(_acc((d, d)), _row(ts, 512), _row(ts, 1024), _acc(ws.shape), _acc(bst.shape), _acc(vec), _acc(vec),
                   _acc(vec), _acc(vec), _acc((1, 1024))),
        out_shape=(SDS((d, d), BF16), SDS((s, 512), F32), SDS((s, 1024), BF16), SDS(ws.shape, F32),
                   SDS(bst.shape, F32), SDS(vec, F32), SDS(vec, F32), SDS(vec, F32), SDS(vec, F32), SDS((1, 1024), F32)),
        scratch_shapes=[pltpu.VMEM((ts, 512), F32), pltpu.VMEM((ts, 512), F32), pltpu.VMEM((d, d), F32)],
        compiler_params=_cparams("arbitrary"))(dh1, c, z, cln_g, cln_b, gln_g, gln_b, ws, wst, bst, w_out)


def _conv_bwd(dc, a, w):
    s, cw = a.shape
    rc = 256 if s % 256 == 0 else 128

    def body(dc_ref, a_ref, w_ref, da_ref, dw_ref, db_ref, pad_a, pad_d, part):
        pad_a[0:CONV_PAD, :] = jnp.zeros((CONV_PAD, 128), F32)
        pad_a[CONV_PAD:, :] = a_ref[...]
        pad_d[0:s, :] = dc_ref[...]
        pad_d[s:, :] = jnp.zeros((CONV_PAD, 128), F32)
        part[...] = jnp.zeros_like(part)

        def rows8(v):
            return jnp.sum(v.reshape(rc // 8, 8, 128), axis=0)

        def chunk(i, carry):
            r0 = pl.multiple_of(i * rc, rc)
            dcc = pad_d[pl.ds(r0, rc), :]
            acc = jnp.zeros((rc, 128), F32)
            for res, offs in _by_residue(range(0, CONV_K)):
                shifted = pad_d[pl.ds(r0 + res, rc + offs[-1] - res), :]
                for off in offs:
                    k = CONV_K - 1 - off
                    acc = acc + w_ref[k:k + 1, :] * shifted[off - res:off - res + rc, :]
            da_ref[pl.ds(r0, rc), :] = acc
            for res, offs in _by_residue(range(CONV_PAD - CONV_K + 1, CONV_PAD + 1)):
                shifted = pad_a[pl.ds(r0 + res, rc + offs[-1] - res), :]
                for off in offs:
                    k = off - (CONV_PAD - CONV_K + 1)
                    part[8 * k:8 * k + 8, :] += rows8(dcc * shifted[off - res:off - res + rc, :])
            part[8 * CONV_PAD:, :] += rows8(dcc)
            return carry

        lax.fori_loop(0, s // rc, chunk, 0)
        sums = jnp.sum(part[...].reshape(CONV_PAD + 1, 8, 128), axis=1)
        dw_ref[...] = sums[0:CONV_PAD, :]
        db_ref[...] = sums[CONV_PAD:, :]

    blk = lambda r: pl.BlockSpec((r, 128), lambda j: (0, j))
    return pl.pallas_call(
        body, name="conv_bwd", grid=(cw // 128,),
        in_specs=[blk(s), blk(s), blk(CONV_K)], out_specs=(blk(s), blk(CONV_PAD), blk(1)),
        out_shape=(SDS((s, cw), F32), SDS((CONV_PAD, cw), F32), SDS((1, cw), F32)),
        scratch_shapes=[pltpu.VMEM((s + CONV_PAD, 128), F32), pltpu.VMEM((s + CONV_PAD, 128), F32),
                        pltpu.VMEM((8 * (CONV_PAD + 1), 128), F32)],
        compiler_params=_cparams("arbitrary"))(dc, a, w)


def _bwd_in(x, dh1, da, z, dzuv, g_mix, w_in3, ts):
    s, d = x.shape

    def body(x_ref, dh1_ref, da_ref, zag_ref, dzuv_ref, g_ref, w_ref, dx_ref, dw_ref, dbin_ref, dg_ref, dz_ref, acc):
        za = zag_ref[:, 0:512]
        sg = jax.nn.sigmoid(zag_ref[:, 512:1024])
        da_ = da_ref[...]
        dza = da_ * sg
        dzg = da_ * za * sg * (1.0 - sg)
        dz_ref[:, 0:512] = _bf(dza)
        dz_ref[:, 512:1024] = _bf(dzg)
        dz_ref[:, 1024:2048] = dzuv_ref[...]
        dhn = jnp.zeros((ts, d), F32)
        for b in range(N_DEV):
            dhn = dhn + _dot(dz_ref[:, 256 * b:256 * (b + 1)], w_ref[b], NT)
        hn, xh, r = _rms_fwd(x_ref[...], g_ref[...])

        @pl.when(pl.program_id(0) == 0)
        def _():
            acc[...] = jnp.zeros_like(acc)

        acc[...] += _dot(_bf(hn), dz_ref[...], TN)

        @pl.when(pl.program_id(0) == pl.num_programs(0) - 1)
        def _():
            for b in range(N_DEV):
                dw_ref[b] = _bf(acc[:, 256 * b:256 * (b + 1)])

        dxn, dg = _rms_bwd(dhn, xh, r, g_ref[...])
        dx_ref[...] = dh1_ref[...] + dxn
        _accumulate(dg_ref, dg)
        _accumulate(dbin_ref, jnp.concatenate([jnp.sum(dza, axis=0, keepdims=True),
                                               jnp.sum(dzg, axis=0, keepdims=True)], axis=1))

    return pl.pallas_call(
        body, name="bwd_in", grid=(s // ts,),
        in_specs=[_row(ts, d), _row(ts, d), _row(ts, 512), _row(ts, 1024, 0), _row(ts, 1024), _const(g_mix.shape),
                  _const(w_in3.shape)],
        out_specs=(_row(ts, d), _acc(w_in3.shape), _acc((1, 1024)), _acc((1, d))),
        out_shape=(SDS((s, d), F32), SDS(w_in3.shape, BF16), SDS((1, 1024), F32), SDS((1, d), F32)),
        scratch_shapes=[pltpu.VMEM((ts, 2048), BF16), pltpu.VMEM((d, 2048), F32)],
        compiler_params=_cparams("arbitrary"))(x, dh1, da, z, dzuv, g_mix, w_in3)


def _tn_matmul(a, b, tm, name):
    s, m = a.shape
    n = b.shape[1]
    ts = min(s, 1024)
    n_s = s // ts

    def body(a_ref, b_ref, o_ref, acc):
        k = pl.program_id(1)

        @pl.when(k == 0)
        def _():
            acc[...] = jnp.zeros_like(acc)

        acc[...] += _dot(a_ref[...], b_ref[...], TN)

        @pl.when(k == n_s - 1)
        def _():
            o_ref[...] = _bf(acc[...])

    return pl.pallas_call(
        body, name=name, grid=(m // tm, n_s),
        in_specs=[pl.BlockSpec((ts, tm), lambda i, k: (k, i)), pl.BlockSpec((ts, n), lambda i, k: (k, 0))],
        out_specs=pl.BlockSpec((tm, n), lambda i, k: (i, 0)), out_shape=SDS((m, n), BF16),
        scratch_shapes=[pltpu.VMEM((tm, n), F32)],
        compiler_params=_cparams("parallel", "arbitrary"))(a, b)


def _row_tile(r, cands):
    for cand in cands:
        if r % cand == 0:
            return cand
    return r


def _sum_in_device_order(me_ref, land_ref, own_ref):
    acc = None
    for dev in range(N_DEV):
        part = jnp.where(me_ref[0] == dev, own_ref[0], land_ref[dev]).astype(F32)
        acc = part if acc is None else acc + part
    return acc


def _sum8(land, own, me, name):
    _, r, c = land.shape
    tr = _row_tile(r, (512, 256, 352, 128, 8))
    own3 = own if own.ndim == 3 else own[None]
    own_map = (lambda i, me_ref: (me_ref[0], i, 0)) if own.ndim == 3 else (lambda i, me_ref: (0, i, 0))

    def body(me_ref, land_ref, own_ref, o_ref):
        o_ref[...] = _sum_in_device_order(me_ref, land_ref, own_ref)

    return pl.pallas_call(
        body, name=name, out_shape=SDS((r, c), F32),
        grid_spec=pltpu.PrefetchScalarGridSpec(
            num_scalar_prefetch=1, grid=(r // tr,),
            in_specs=[pl.BlockSpec((N_DEV, tr, c), lambda i, me_ref: (0, i, 0)), pl.BlockSpec((1, tr, c), own_map)],
            out_specs=pl.BlockSpec((tr, c), lambda i, me_ref: (i, 0))),
        compiler_params=_cparams("parallel"))(me, land, own3)


def _adam_update(w, g, m, v):
    m2 = ADAM_B1 * m + (1.0 - ADAM_B1) * g
    v2 = ADAM_B2 * v + (1.0 - ADAM_B2) * (g * g)
    m_hat = m2 / (1.0 - ADAM_B1 ** ADAM_STEP)
    v_hat = v2 / (1.0 - ADAM_B2 ** ADAM_STEP)
    return -ADAM_LR * (m_hat / (jnp.sqrt(v_hat) + ADAM_EPS) + ADAM_WD * w), m2, v2


def _slot_spec(tr, c, mask):
    return pl.BlockSpec((1, tr, c), lambda i, me_ref: (me_ref[0] ^ mask, i, 0))


def _sum_adamw(land, parts, w, m, v, me, name, masks=ALL_PEERS):
    r, c = w.shape
    tr = _row_tile(r, (256, 128, 176, 8))
    n = len(masks)

    def body(me_ref, own_ref, *refs):
        w_ref, m_ref, v_ref, g_ref, d_ref, m2_ref, v2_ref = refs[n:]
        g = own_ref[0].astype(F32)
        for peer_ref in refs[:n]:
            g = g + peer_ref[0].astype(F32)
        g_ref[...] = g
        d_ref[...], m2_ref[...], v2_ref[...] = _adam_update(w_ref[...], g, m_ref[...], v_ref[...])

    blk = pl.BlockSpec((tr, c), lambda i, me_ref: (i, 0))
    return pl.pallas_call(
        body, name=name, out_shape=(SDS((r, c), F32),) * 4,
        grid_spec=pltpu.PrefetchScalarGridSpec(
            num_scalar_prefetch=1, grid=(r // tr,),
            in_specs=[_slot_spec(tr, c, 0)] + [_slot_spec(tr, c, mask) for mask in masks] + [blk, blk, blk],
            out_specs=(blk,) * 4),
        compiler_params=_cparams("parallel"))(me, parts, *([land] * n), w, m, v)


def _chip_sum(parts, from_sibling, me, name):
    _, r, c = parts.shape
    tr = _row_tile(r, (512, 256, 352, 128, 8))
    spec = pl.BlockSpec((1, tr, c), lambda j, i, me_ref: (me_ref[0] ^ (2 * j), i, 0))

    def body(me_ref, p_ref, q_ref, o_ref):
        o_ref[...] = _bf(p_ref[...].astype(F32) + q_ref[...].astype(F32))

    return pl.pallas_call(
        body, name=name, out_shape=SDS(parts.shape, BF16),
        grid_spec=pltpu.PrefetchScalarGridSpec(num_scalar_prefetch=1, grid=(len(CHIPS), r // tr), in_specs=[spec, spec],
                                               out_specs=spec),
        compiler_params=_cparams("parallel", "parallel"))(me, parts, from_sibling)


def _adamw_many(ws, gs, ms, vs, name):
    n = len(ws)

    def body(*refs):
        ins, outs = refs[:4 * n], refs[4 * n:]
        for i in range(n):
            w, g, m, v = (ins[j * n + i][...] for j in range(4))
            outs[i][...], outs[n + i][...], outs[2 * n + i][...] = _adam_update(w, g, m, v)

    shapes = [SDS(w.shape, F32) for w in ws]
    out = pl.pallas_call(body, name=name, out_shape=shapes * 3, compiler_params=_cparams())(*ws, *gs, *ms, *vs)
    return out[:n], out[n:2 * n], out[2 * n:]


def _pack(arrs):
    flat = jnp.concatenate([a.reshape(-1) for a in arrs])
    pad = (-flat.shape[0]) % (128 * 128)
    return jnp.pad(flat, (0, pad)).reshape(-1, 128)


def _unpack(packed, shapes):
    flat = packed.reshape(-1)
    out, off = [], 0
    for shp in shapes:
        size = 1
        for dim in shp:
            size *= dim
        out.append(flat[off:off + size].reshape(shp))
        off += size
    return out


SMALL = ("norm_mix_g", "b_in", "conv_w", "conv_b", "conv_ln_g", "conv_ln_b", "gm_ln_g", "gm_ln_b", "gm_w_s", "gm_b_s",
         "norm_xa_g", "mem_norm_g", "norm_ffn_g", "final_norm_g")
BIG = ("w_in", "w_out", "xa_wq", "xa_wkv", "xa_wo", "ffn_w_gate_up", "ffn_w_down")
WEIGHTS = ("norm_mix_g", "w_in", "b_in", "conv_w", "conv_b", "conv_ln_g", "conv_ln_b", "gm_ln_g", "gm_ln_b", "gm_w_s",
           "gm_b_s", "w_out", "norm_xa_g", "mem_norm_g", "xa_wq", "xa_wkv", "xa_wo", "norm_ffn_g", "ffn_w_gate_up",
           "ffn_w_down", "final_norm_g")


def kernel(x, mem, norm_mix_g, w_in, b_in, conv_w, conv_b, conv_ln_g, conv_ln_b, gm_ln_g, gm_ln_b, gm_w_s, gm_b_s, w_out, norm_xa_g, mem_norm_g, xa_wq, xa_wkv, xa_wo, norm_ffn_g, ffn_w_gate_up, ffn_w_down, final_norm_g, loss_target, m_norm_mix_g, m_w_in, m_b_in, m_conv_w, m_conv_b, m_conv_ln_g, m_conv_ln_b, m_gm_ln_g, m_gm_ln_b, m_gm_w_s, m_gm_b_s, m_w_out, m_norm_xa_g, m_mem_norm_g, m_xa_wq, m_xa_wkv, m_xa_wo, m_norm_ffn_g, m_ffn_w_gate_up, m_ffn_w_down, m_final_norm_g, v_norm_mix_g, v_w_in, v_b_in, v_conv_w, v_conv_b, v_conv_ln_g, v_conv_ln_b, v_gm_ln_g, v_gm_ln_b, v_gm_w_s, v_gm_b_s, v_w_out, v_norm_xa_g, v_mem_norm_g, v_xa_wq, v_xa_wkv, v_xa_wo, v_norm_ffn_g, v_ffn_w_gate_up, v_ffn_w_down, v_final_norm_g):
    w = dict(norm_mix_g=norm_mix_g, w_in=w_in, b_in=b_in, conv_w=conv_w, conv_b=conv_b, conv_ln_g=conv_ln_g,
             conv_ln_b=conv_ln_b, gm_ln_g=gm_ln_g, gm_ln_b=gm_ln_b, gm_w_s=gm_w_s, gm_b_s=gm_b_s, w_out=w_out,
             norm_xa_g=norm_xa_g, mem_norm_g=mem_norm_g, xa_wq=xa_wq, xa_wkv=xa_wkv, xa_wo=xa_wo,
             norm_ffn_g=norm_ffn_g, ffn_w_gate_up=ffn_w_gate_up, ffn_w_down=ffn_w_down, final_norm_g=final_norm_g)
    mom = dict(norm_mix_g=m_norm_mix_g, w_in=m_w_in, b_in=m_b_in, conv_w=m_conv_w, conv_b=m_conv_b,
               conv_ln_g=m_conv_ln_g, conv_ln_b=m_conv_ln_b, gm_ln_g=m_gm_ln_g, gm_ln_b=m_gm_ln_b, gm_w_s=m_gm_w_s,
               gm_b_s=m_gm_b_s, w_out=m_w_out, norm_xa_g=m_norm_xa_g, mem_norm_g=m_mem_norm_g, xa_wq=m_xa_wq,
               xa_wkv=m_xa_wkv, xa_wo=m_xa_wo, norm_ffn_g=m_norm_ffn_g, ffn_w_gate_up=m_ffn_w_gate_up,
               ffn_w_down=m_ffn_w_down, final_norm_g=m_final_norm_g)
    var = dict(norm_mix_g=v_norm_mix_g, w_in=v_w_in, b_in=v_b_in, conv_w=v_conv_w, conv_b=v_conv_b,
               conv_ln_g=v_conv_ln_g, conv_ln_b=v_conv_ln_b, gm_ln_g=v_gm_ln_g, gm_ln_b=v_gm_ln_b, gm_w_s=v_gm_w_s,
               gm_b_s=v_gm_b_s, w_out=v_w_out, norm_xa_g=v_norm_xa_g, mem_norm_g=v_mem_norm_g, xa_wq=v_xa_wq,
               xa_wkv=v_xa_wkv, xa_wo=v_xa_wo, norm_ffn_g=v_norm_ffn_g, ffn_w_gate_up=v_ffn_w_gate_up,
               ffn_w_down=v_ffn_w_down, final_norm_g=v_final_norm_g)

    me = 4 * lax.axis_index("x") + 2 * lax.axis_index("y") + lax.axis_index("c")
    s, d = x.shape[1], x.shape[2]
    xs = x.reshape(s, d)
    mems = mem.reshape(mem.shape[1], d)
    tgt = loss_target.reshape(s, d)
    ts = min(512, s)
    ts_ffn = min(256, s)
    row = lambda a: a.reshape(1, -1)

    conv_w_pad = jnp.pad(conv_w, ((0, CONV_PAD - CONV_K), (0, 128 - conv_w.shape[1])))
    first_level = (SIBLING,) + OTHER_CHIPS
    ex_first = _exchange_start([_bf(w_in), conv_w_pad], [False] * 2, "gather_start_in", first_level, own_slot=me)
    behind = lambda t: _bf(t + ex_first[4][0:1, 0:1])
    shards = [behind(xa_wkv), behind(w_out), behind(xa_wq), behind(xa_wo), behind(ffn_w_gate_up.T), behind(ffn_w_down)]
    ex_rest = _exchange_start(shards, [False] * len(shards), "gather_start_rest", first_level, own_slot=me)
    g_send, g_recv, g_src, g_land = (list(ex_first[i]) + list(ex_rest[i]) for i in range(4))
    g_tok = ex_rest[4]

    def arrived(idx, after, name):
        pick = lambda seq: [seq[i] for i in idx]
        return _wait_and_forward(pick(g_send), pick(g_recv), pick(g_src), pick(g_land), after, "gather_pass_" + name,
                                 first_level)

    def complete(handle, after, name):
        return _forward_wait(handle[0], handle[1], handle[2], after, "forward_wait_" + name)

    bst = gm_b_s.T
    wst = jnp.swapaxes(gm_w_s, 1, 2)

    h_in = arrived((0, 1), g_tok, "in")
    w_in3, conv_w8 = complete(h_in, h_in[3], "in")
    conv_w_f = conv_w8[:, :CONV_K, :conv_w.shape[1]].transpose(1, 0, 2).reshape(CONV_K, -1)
    cw = conv_w_f.shape[1]
    z, a = _fwd_in(xs, row(norm_mix_g), w_in3, row(b_in), ts)
    h_out = arrived((2, 3), z, "out")
    c = _conv_fwd(a, conv_w_f + h_out[3][0:1, 0:1], row(conv_b))
    h_xa = arrived((4, 5), c, "xa")
    wkv3, w_out3 = complete(h_out, h_xa[3], "out")
    w_out_f = w_out3.reshape(-1, d)
    kv, mn = _kv_proj(mems, row(mem_norm_g), wkv3)
    h1 = _fwd_out(xs, c, z, row(conv_ln_g), row(conv_ln_b), row(gm_ln_g), row(gm_ln_b), gm_w_s, bst, w_out_f, ts)
    h_gut = arrived((6,), h1, "gate_up")
    wq3, wo3 = complete(h_xa, h_gut[3], "xa")
    wq_f = wq3.reshape(-1, d)
    wo_f = wo3.reshape(-1, d)
    h2, qb, ob = _fwd_xa(h1, row(norm_xa_g), wq_f, kv, wo_f, ts)
    h_down = arrived((7,), h2, "down")
    (wgut3,) = complete(h_gut, h_down[3], "gate_up")
    (wdown3,) = complete(h_down, wgut3, "down")
    wgut_f = wgut3.reshape(-1, d)
    wdown_f = wdown3.reshape(-1, d)
    dh3, dh3b, gu, hn2, loss_p, d_final_g = _fwd_ffn(h2, row(norm_ffn_g), wgut_f, wdown_f, row(final_norm_g), tgt,
                                                     ts_ffn)

    blocks = lambda m: m.reshape(N_DEV, -1, d)
    tok = lambda ex: ex[4][0:1, 0:1]
    me1 = me.astype(jnp.int32).reshape(1)

    def chip_sums(pair, after, tag):
        parts, handed = _pair_wait(pair[0], pair[1], pair[2], pair[3], after, "pair_wait_" + tag)
        return [_chip_sum(p, h, me1, "chip_sum_%s_%d" % (tag, i)) for i, (p, h) in enumerate(zip(parts, handed))]

    dh2, act, dgu, d_ffn_g = _bwd_ffn(h2, dh3, gu, row(norm_ffn_g), wgut_f, wdown_f, ts_ffn)
    dwgut = _tn_matmul(dgu, hn2, 1408, "dw_gate_up")
    dwdown = _tn_matmul(act, dh3b, 1408, "dw_down")
    pair_ffn = _pair_start([blocks(dwgut), blocks(dwdown)], "pair_start_ffn")
    dh1, dwq, dwo, dkv, d_xa_g = _bwd_xa(h1, dh2, qb, ob, row(norm_xa_g) + tok(pair_ffn), wq_f, wo_f, kv, ts)
    ex_ffn = _exchange_start(chip_sums(pair_ffn, dh1, "ffn"), [True] * 2, "scatter_start_ffn", OTHER_CHIPS)
    dwkv3, d_mem_g = _bwd_kv(dkv, mn, mems, row(mem_norm_g) + tok(ex_ffn), wkv3)
    pair_xa = _pair_start([blocks(dwq), blocks(dwo), dwkv3], "pair_start_xa")
    (dwout, dc, dzuv, dws, dbst, d_cln_g, d_cln_b, d_gln_g, d_gln_b, dbin_uv) = _bwd_out(
        dh1, c, z, row(conv_ln_g) + tok(pair_xa), row(conv_ln_b), row(gm_ln_g), row(gm_ln_b), gm_w_s, wst, bst,
        w_out_f, ts)
    early = dict(b_in_uv=dbin_uv, conv_ln_g=d_cln_g, conv_ln_b=d_cln_b, gm_ln_g=d_gln_g, gm_ln_b=d_gln_b,
                 gm_b_s=dbst.T, norm_xa_g=d_xa_g, mem_norm_g=d_mem_g, norm_ffn_g=d_ffn_g, final_norm_g=d_final_g,
                 loss=loss_p)
    dws_b = _bf(dws).reshape(GM_HEADS * CHUNK, CHUNK)
    ex_xa = _exchange_start(chip_sums(pair_xa, dc, "xa") + [blocks(dwout), dws_b, _pack(list(early.values()))],
                            [True] * 4 + [False] * 2, "scatter_start_xa", [OTHER_CHIPS] * 3 + [ALL_PEERS] * 3)
    da, dconv_w, dconv_b = _conv_bwd(dc, a, conv_w_f + tok(ex_xa))
    dx, dw_in3, dbin_ag, d_mix_g = _bwd_in(xs, dh1, da, z, dzuv, row(norm_mix_g), w_in3, ts)
    late = dict(norm_mix_g=d_mix_g, b_in_ag=dbin_ag, conv_w=dconv_w[:CONV_K], conv_b=dconv_b)
    ex_last = _exchange_start([_pack(list(late.values())), dw_in3], [False, True], "scatter_start_in")

    grads, delta, new_m, new_v = {}, {}, {}, {}

    def waited(ex, idx, scatter, after, name, masks=ALL_PEERS):
        return _exchange_wait(*[[ex[i][k] for k in idx] for i in range(4)], scatter, after, name, masks)

    def reduced(names, srcs, lands, masks):
        for nm, src, land, mk in zip(names, srcs, lands, masks):
            view = (lambda t: t.T) if nm == "ffn_w_gate_up" else (lambda t: t)
            upd = _sum_adamw(land, src, view(w[nm]), view(mom[nm]), view(var[nm]), me1, "adamw_" + nm, mk)
            grads[nm], delta[nm], new_m[nm], new_v[nm] = (view(t) for t in upd)
        return new_v[names[-1]]

    after = reduced(("ffn_w_gate_up", "ffn_w_down"),
                    *waited(ex_ffn, (0, 1), [True] * 2, ex_last[4], "scatter_wait_ffn", OTHER_CHIPS), [OTHER_CHIPS] * 2)
    xa_masks = [OTHER_CHIPS] * 3 + [ALL_PEERS]
    after = reduced(("xa_wq", "xa_wo", "xa_wkv", "w_out"),
                    *waited(ex_xa, (0, 1, 2, 3), [True] * 4, after, "scatter_wait_xa", xa_masks), xa_masks)

    (dws_src, early_src), (dws_land, early_land) = waited(ex_xa, (4, 5), [False] * 2, after, "gather_wait_early")
    (late_src,), (late_land,) = waited(ex_last, (0,), [False], early_land, "gather_wait_late")
    dws_tot = _sum8(dws_land, dws_src, me1, "sum_gm_w_s")
    early_tot = _sum8(early_land, early_src, me1, "sum_small_early")
    late_tot = _sum8(late_land, late_src, me1, "sum_small_late")
    early_g = dict(zip(early, _unpack(early_tot, [v.shape for v in early.values()])), gm_w_s=dws_tot)
    late_g = dict(zip(late, _unpack(late_tot, [v.shape for v in late.values()])))
    loss = early_g["loss"][0, 0]
    for nm in SMALL:
        if nm == "b_in":
            g = jnp.concatenate([late_g["b_in_ag"], early_g["b_in_uv"]], axis=1)
        elif nm == "conv_w":
            g = lax.dynamic_slice_in_dim(late_g[nm], me * conv_w.shape[1], conv_w.shape[1], axis=1)
        else:
            g = late_g[nm] if nm in late_g else early_g[nm]
        grads[nm] = g.reshape(w[nm].shape)

    small = [[src[nm] for nm in SMALL] for src in (w, grads, mom, var)]
    for dst, vals in zip((delta, new_m, new_v), _adamw_many(*small, "adamw_small")):
        dst.update(zip(SMALL, vals))

    reduced(("w_in",), *waited(ex_last, (1,), [True], delta[SMALL[0]], "scatter_wait_in"), [ALL_PEERS])

    return (loss, dx.reshape(x.shape), *[grads[nm] for nm in WEIGHTS], *[delta[nm] for nm in WEIGHTS],
            *[new_m[nm] for nm in WEIGHTS], *[new_v[nm] for nm in WEIGHTS])
```

```python
import functools

import jax
import jax.numpy as jnp
from jax import lax
from jax.experimental import pallas as pl
from jax.experimental.pallas import tpu as pltpu

F32 = jnp.float32
BF16 = jnp.bfloat16
SDS = jax.ShapeDtypeStruct

N_DEV = 8
RMS_EPS = 1e-6
LN_EPS = 1e-5
CONV_K = 31
CONV_PAD = 32
CHUNK = 128
GM_HEADS = 8
XA_HEADS = 4
XA_DH = 256
GELU_K0 = 0.7978845608028654
GELU_K1 = 0.044715
ADAM_LR = 0.001
ADAM_B1 = 0.9
ADAM_B2 = 0.999
ADAM_EPS = 1e-08
ADAM_WD = 0.01
ADAM_STEP = 10
VMEM_LIMIT = 60 * 1024 * 1024

NN = (((1,), (0,)), ((), ()))
NT = (((1,), (1,)), ((), ()))
TN = (((0,), (0,)), ((), ()))


def _dot(a, b, dims=NN):
    return lax.dot_general(a, b, dims, preferred_element_type=F32)


def _bf(x):
    return x.astype(BF16)


def _cparams(*sem):
    return pltpu.CompilerParams(dimension_semantics=tuple(sem) if sem else None, vmem_limit_bytes=VMEM_LIMIT)


def _row(ts, w, col=0):
    return pl.BlockSpec((ts, w), lambda i: (i, col))


def _const(shape):
    nd = len(shape)
    return pl.BlockSpec(shape, lambda i: (0,) * nd, pipeline_mode=pl.Buffered(1))


def _acc(shape):
    nd = len(shape)
    return pl.BlockSpec(shape, lambda i: (0,) * nd)


def _rms_fwd(x, g):
    r = lax.rsqrt(jnp.mean(x * x, axis=-1, keepdims=True) + RMS_EPS)
    xh = x * r
    return xh * g, xh, r


def _rms_bwd(dy, xh, r, g):
    gdy = dy * g
    dx = r * (gdy - xh * jnp.mean(gdy * xh, axis=-1, keepdims=True))
    dg = jnp.sum(dy * xh, axis=0, keepdims=True)
    return dx, dg


def _ln_fwd(x, g, b):
    mu = jnp.mean(x, axis=-1, keepdims=True)
    xc = x - mu
    rs = lax.rsqrt(jnp.mean(xc * xc, axis=-1, keepdims=True) + LN_EPS)
    xh = xc * rs
    return xh * g + b, xh, rs


def _ln_bwd(dy, xh, rs, g):
    dxh = dy * g
    dx = rs * (dxh - jnp.mean(dxh, axis=-1, keepdims=True) - xh * jnp.mean(dxh * xh, axis=-1, keepdims=True))
    return dx, jnp.sum(dy * xh, axis=0, keepdims=True), jnp.sum(dy, axis=0, keepdims=True)


def _gelu(x):
    t = jnp.tanh(GELU_K0 * (x + GELU_K1 * (x * x * x)))
    return 0.5 * x * (1.0 + t), t


def _gelu_grad(x, t):
    return 0.5 * (1.0 + t) + 0.5 * x * (1.0 - t * t) * (GELU_K0 * (1.0 + 3.0 * GELU_K1 * x * x))


def _silu_grad(x, sg):
    return sg * (1.0 + x * (1.0 - sg))


def _by_residue(offsets):
    groups = {}
    for off in offsets:
        groups.setdefault(off % 8, []).append(off)
    return [(res, sorted(offs)) for res, offs in sorted(groups.items())]


def _accumulate(ref, val):
    @pl.when(pl.program_id(0) == 0)
    def _():
        ref[...] = jnp.zeros_like(ref)
    ref[...] += val


def _mix_masks():
    row = lax.broadcasted_iota(jnp.int32, (CHUNK, CHUNK), 0)
    col = lax.broadcasted_iota(jnp.int32, (CHUNK, CHUNK), 1)
    return row >= col, row <= col, col < (CHUNK // 2)


def _mix_fwd(vb, ws_ref, bst_ref, mixed_scr, ts):
    tril, _, lo = _mix_masks()
    for j in range(GM_HEADS // 2):
        w0 = _bf(jnp.where(tril, ws_ref[2 * j], 0.0))
        w1 = _bf(jnp.where(tril, ws_ref[2 * j + 1], 0.0))
        bias = jnp.where(lo, bst_ref[:, 2 * j:2 * j + 1], bst_ref[:, 2 * j + 1:2 * j + 2])
        for n in range(ts // CHUNK):
            v = vb[n * CHUNK:(n + 1) * CHUNK, j * 128:(j + 1) * 128]
            mixed_scr[n * CHUNK:(n + 1) * CHUNK, j * 128:(j + 1) * 128] = jnp.where(lo, _dot(w0, v), _dot(w1, v)) + bias


def _exchange(srcs, scatter, name):
    n = len(srcs)

    def body(*refs):
        src_refs, out_refs = refs[:n], refs[n:2 * n]
        send_sems, recv_sems, local_sems = refs[2 * n:]
        x, y, c = lax.axis_index("x"), lax.axis_index("y"), lax.axis_index("c")
        me = 4 * x + 2 * y + c

        def peer_of(mask):
            px = x if not (mask >> 2) & 1 else 1 - x
            py = y if not (mask >> 1) & 1 else 1 - y
            pc = c if not mask & 1 else 1 - c
            return (px, py, pc), 4 * px + 2 * py + pc

        def remote(k, mask):
            peer, pidx = peer_of(mask)
            return pltpu.make_async_remote_copy(
                src_ref=src_refs[k].at[pidx] if scatter else src_refs[k],
                dst_ref=out_refs[k].at[me],
                send_sem=send_sems.at[k, mask - 1], recv_sem=recv_sems.at[k, mask - 1],
                device_id=peer, device_id_type=pl.DeviceIdType.MESH)

        def arrival(k, mask):
            peer, pidx = peer_of(mask)
            return pltpu.make_async_remote_copy(
                src_ref=src_refs[k].at[pidx] if scatter else src_refs[k],
                dst_ref=out_refs[k].at[pidx],
                send_sem=send_sems.at[k, mask - 1], recv_sem=recv_sems.at[k, mask - 1],
                device_id=peer, device_id_type=pl.DeviceIdType.MESH)

        sends, locals_ = [], []
        for k in range(n):
            for mask in range(1, N_DEV):
                cp = remote(k, mask)
                cp.start()
                sends.append(cp)
            lc = pltpu.make_async_copy(src_refs[k].at[me] if scatter else src_refs[k], out_refs[k].at[me],
                                       local_sems.at[k])
            lc.start()
            locals_.append(lc)
        for k in range(n):
            for mask in range(1, N_DEV):
                arrival(k, mask).wait_recv()
        for cp in sends:
            cp.wait_send()
        for lc in locals_:
            lc.wait()

    outs = [SDS((N_DEV,) + tuple(s.shape[1:] if scatter else s.shape), s.dtype) for s in srcs]
    hbm = pl.BlockSpec(memory_space=pl.ANY)
    return pl.pallas_call(
        body, name=name, out_shape=outs, in_specs=[hbm] * n, out_specs=[hbm] * n,
        scratch_shapes=[pltpu.SemaphoreType.DMA((n, N_DEV - 1)), pltpu.SemaphoreType.DMA((n, N_DEV - 1)),
                        pltpu.SemaphoreType.DMA((n,))],
    )(*srcs)


def _peer_of(mask):
    x, y, c = lax.axis_index("x"), lax.axis_index("y"), lax.axis_index("c")
    px = 1 - x if (mask >> 2) & 1 else x
    py = 1 - y if (mask >> 1) & 1 else y
    pc = 1 - c if mask & 1 else c
    return (px, py, pc), 4 * px + 2 * py + pc


ALL_PEERS = tuple(range(1, N_DEV))
OTHER_CHIPS = (2, 4, 6)
CHIPS = (0,) + OTHER_CHIPS
SIBLING = 1


def _split_copy(src_ref, land_ref, send_sem, recv_sem, mask, slot, scatter, outgoing):
    x, y, c = lax.axis_index("x"), lax.axis_index("y"), lax.axis_index("c")
    me = 4 * x + 2 * y + c
    peer, pidx = _peer_of(mask)
    return pltpu.make_async_remote_copy(
        src_ref=src_ref.at[pidx] if scatter else src_ref,
        dst_ref=land_ref.at[me if outgoing else pidx],
        send_sem=send_sem.at[slot], recv_sem=recv_sem.at[slot],
        device_id=peer, device_id_type=pl.DeviceIdType.MESH)


_HBM = pl.BlockSpec(memory_space=pltpu.HBM)
_SEM = pl.BlockSpec(memory_space=pltpu.SEMAPHORE)
_EFFECT = pltpu.SideEffectType.DATAFLOW_SIDE_EFFECTING


def _per_array(masks, n):
    return [tuple(masks)] * n if isinstance(masks[0], int) else [tuple(m) for m in masks]


def _exchange_start(srcs, scatter, name, masks=ALL_PEERS, own_slot=None):
    n = len(srcs)
    masks = _per_array(masks, n)
    lands = [lax.empty((N_DEV,) + tuple(s.shape[1:] if sc else s.shape), s.dtype) for s, sc in zip(srcs, scatter)]
    if own_slot is not None:
        lands = [land if sc else _with_own(land, s, own_slot) for land, s, sc in zip(lands, srcs, scatter)]
    lands = [pltpu.with_memory_space_constraint(land, pltpu.HBM) for land in lands]
    srcs = [pltpu.with_memory_space_constraint(s, pltpu.HBM) for s in srcs]

    def body(*refs):
        src_refs, land_refs = refs[:n], refs[n:2 * n]
        send_sems, recv_sems = refs[2 * n:3 * n], refs[3 * n:4 * n]
        token = refs[-1]
        for k in range(n):
            for slot, mask in enumerate(masks[k]):
                _split_copy(src_refs[k], land_refs[k], send_sems[k], recv_sems[k], mask, slot, scatter[k], True).start()
        token[...] = jnp.zeros_like(token)

    sems = [pltpu.SemaphoreType.DMA((len(m),)) for m in masks]
    out = pl.pallas_call(
        body, name=name,
        out_shape=tuple(sems + sems + [pltpu.HBM(s.shape, s.dtype) for s in srcs]
                        + [pltpu.HBM(l.shape, l.dtype) for l in lands] + [SDS((8, 128), F32)]),
        in_specs=[_HBM] * (2 * n),
        out_specs=tuple([_SEM] * (2 * n) + [_HBM] * (2 * n) + [pl.BlockSpec(memory_space=pltpu.VMEM)]),
        input_output_aliases={i: 2 * n + i for i in range(2 * n)},
        compiler_params=pltpu.CompilerParams(has_side_effects=_EFFECT),
    )(*srcs, *lands)
    return out[:n], out[n:2 * n], out[2 * n:3 * n], out[3 * n:4 * n], out[-1]


def _exchange_wait(send_sems, recv_sems, srcs_thru, lands_thru, scatter, after, name, masks=ALL_PEERS):
    n = len(srcs_thru)
    masks = _per_array(masks, n)

    def body(*refs):
        src_refs, land_refs = refs[:n], refs[n:2 * n]
        send_refs, recv_refs = refs[2 * n:3 * n], refs[3 * n:4 * n]
        for k in range(n):
            for slot, mask in enumerate(masks[k]):
                args = (src_refs[k], land_refs[k], send_refs[k], recv_refs[k], mask, slot, scatter[k])
                _split_copy(*args, True).wait_send()
                _split_copy(*args, False).wait_recv()

    out = pl.pallas_call(
        body, name=name,
        out_shape=tuple([pltpu.HBM(s.shape, s.dtype) for s in srcs_thru]
                        + [pltpu.HBM(l.shape, l.dtype) for l in lands_thru]),
        in_specs=[_HBM] * (2 * n) + [_SEM] * (2 * n) + [pl.BlockSpec(memory_space=pl.ANY)],
        out_specs=tuple([_HBM] * (2 * n)),
        input_output_aliases={i: i for i in range(2 * n)},
        compiler_params=pltpu.CompilerParams(has_side_effects=_EFFECT),
    )(*srcs_thru, *lands_thru, *send_sems, *recv_sems, after)
    return out[:n], out[n:]


def _wait_and_forward(send_sems, recv_sems, srcs_thru, lands_thru, after, name, masks):
    n = len(lands_thru)

    def body(*refs):
        src_refs, land_refs = refs[:n], refs[n:2 * n]
        send_refs, recv_refs = refs[2 * n:3 * n], refs[3 * n:4 * n]
        outs = refs[4 * n + 1:]
        fsend, frecv, token = outs[2 * n:3 * n], outs[3 * n:4 * n], outs[-1]
        for k in range(n):
            for slot, mask in enumerate(masks):
                args = (src_refs[k], land_refs[k], send_refs[k], recv_refs[k], mask, slot, False)
                _split_copy(*args, True).wait_send()
                _split_copy(*args, False).wait_recv()
        sibling, _ = _peer_of(SIBLING)
        for k in range(n):
            for slot, mask in enumerate(OTHER_CHIPS):
                _, mine = _peer_of(mask)
                pltpu.make_async_remote_copy(
                    src_ref=land_refs[k].at[mine], dst_ref=land_refs[k].at[mine], send_sem=fsend[k].at[slot],
                    recv_sem=frecv[k].at[slot], device_id=sibling, device_id_type=pl.DeviceIdType.MESH).start()
        token[...] = jnp.zeros_like(token)

    sem = pltpu.SemaphoreType.DMA((len(OTHER_CHIPS),))
    out = pl.pallas_call(
        body, name=name,
        out_shape=tuple([pltpu.HBM(s.shape, s.dtype) for s in srcs_thru] + [pltpu.HBM(l.shape, l.dtype) for l in lands_thru]
                        + [sem] * (2 * n) + [SDS((8, 128), F32)]),
        in_specs=[_HBM] * (2 * n) + [_SEM] * (2 * n) + [pl.BlockSpec(memory_space=pl.ANY)],
        out_specs=tuple([_HBM] * (2 * n) + [_SEM] * (2 * n) + [pl.BlockSpec(memory_space=pltpu.VMEM)]),
        input_output_aliases={i: i for i in range(2 * n)},
        compiler_params=pltpu.CompilerParams(has_side_effects=_EFFECT),
    )(*srcs_thru, *lands_thru, *send_sems, *recv_sems, after)
    return out[2 * n:3 * n], out[3 * n:4 * n], out[n:2 * n], out[-1]


def _forward_wait(send_sems, recv_sems, lands, after, name):
    n = len(lands)

    def body(*refs):
        land_refs, send_refs, recv_refs = refs[:n], refs[n:2 * n], refs[2 * n:3 * n]
        sibling, _ = _peer_of(SIBLING)
        for k in range(n):
            for slot, mask in enumerate(OTHER_CHIPS):
                _, mine = _peer_of(mask)
                _, theirs = _peer_of(mask | SIBLING)
                for block, wait in ((mine, "wait_send"), (theirs, "wait_recv")):
                    getattr(pltpu.make_async_remote_copy(
                        src_ref=land_refs[k].at[block], dst_ref=land_refs[k].at[block],
                        send_sem=send_refs[k].at[slot], recv_sem=recv_refs[k].at[slot], device_id=sibling,
                        device_id_type=pl.DeviceIdType.MESH), wait)()

    return pl.pallas_call(
        body, name=name, out_shape=tuple(pltpu.HBM(l.shape, l.dtype) for l in lands),
        in_specs=[_HBM] * n + [_SEM] * (2 * n) + [pl.BlockSpec(memory_space=pl.ANY)],
        out_specs=tuple([_HBM] * n), input_output_aliases={i: i for i in range(n)},
        compiler_params=pltpu.CompilerParams(has_side_effects=_EFFECT),
    )(*lands, *send_sems, *recv_sems, after)


def _pair_copy(parts_ref, land_ref, send_sem, recv_sem, slot, chip, outgoing):
    sibling, _ = _peer_of(SIBLING)
    _, block = _peer_of(chip | SIBLING if outgoing else chip)
    return pltpu.make_async_remote_copy(
        src_ref=parts_ref.at[block], dst_ref=land_ref.at[block], send_sem=send_sem.at[slot], recv_sem=recv_sem.at[slot],
        device_id=sibling, device_id_type=pl.DeviceIdType.MESH)


def _pair_start(parts, name):
    n = len(parts)
    lands = [pltpu.with_memory_space_constraint(lax.empty(p.shape, p.dtype), pltpu.HBM) for p in parts]
    parts = [pltpu.with_memory_space_constraint(p, pltpu.HBM) for p in parts]

    def body(*refs):
        part_refs, land_refs = refs[:n], refs[n:2 * n]
        send_sems, recv_sems, token = refs[2 * n:3 * n], refs[3 * n:4 * n], refs[-1]
        for k in range(n):
            for slot, chip in enumerate(CHIPS):
                _pair_copy(part_refs[k], land_refs[k], send_sems[k], recv_sems[k], slot, chip, True).start()
        token[...] = jnp.zeros_like(token)

    sem = pltpu.SemaphoreType.DMA((len(CHIPS),))
    out = pl.pallas_call(
        body, name=name,
        out_shape=tuple([sem] * (2 * n) + [pltpu.HBM(p.shape, p.dtype) for p in parts + lands] + [SDS((8, 128), F32)]),
        in_specs=[_HBM] * (2 * n),
        out_specs=tuple([_SEM] * (2 * n) + [_HBM] * (2 * n) + [pl.BlockSpec(memory_space=pltpu.VMEM)]),
        input_output_aliases={i: 2 * n + i for i in range(2 * n)},
        compiler_params=pltpu.CompilerParams(has_side_effects=_EFFECT),
    )(*parts, *lands)
    return out[:n], out[n:2 * n], out[2 * n:3 * n], out[3 * n:4 * n], out[-1]


def _pair_wait(send_sems, recv_sems, parts_thru, lands_thru, after, name):
    n = len(parts_thru)

    def body(*refs):
        part_refs, land_refs = refs[:n], refs[n:2 * n]
        send_refs, recv_refs = refs[2 * n:3 * n], refs[3 * n:4 * n]
        for k in range(n):
            for slot, chip in enumerate(CHIPS):
                args = (part_refs[k], land_refs[k], send_refs[k], recv_refs[k], slot, chip)
                _pair_copy(*args, True).wait_send()
                _pair_copy(*args, False).wait_recv()

    out = pl.pallas_call(
        body, name=name,
        out_shape=tuple(pltpu.HBM(p.shape, p.dtype) for p in list(parts_thru) + list(lands_thru)),
        in_specs=[_HBM] * (2 * n) + [_SEM] * (2 * n) + [pl.BlockSpec(memory_space=pl.ANY)],
        out_specs=tuple([_HBM] * (2 * n)), input_output_aliases={i: i for i in range(2 * n)},
        compiler_params=pltpu.CompilerParams(has_side_effects=_EFFECT),
    )(*parts_thru, *lands_thru, *send_sems, *recv_sems, after)
    return out[:n], out[n:]


def _with_own(landed, own, me):
    return lax.dynamic_update_slice_in_dim(landed, own[None], me, axis=0)


def _kv_proj(mem, g_mem, wkv3):
    m = mem.shape[0]

    def body(mem_ref, g_ref, w_ref, kv_ref, mn_ref):
        y, _, _ = _rms_fwd(mem_ref[...], g_ref[...])
        yb = _bf(y)
        mn_ref[...] = yb
        for b in range(N_DEV):
            kv_ref[:, 256 * b:256 * (b + 1)] = _dot(yb, w_ref[b])

    return pl.pallas_call(body, name="kv_proj", out_shape=(SDS((m, 2048), F32), SDS(mem.shape, BF16)),
                          compiler_params=_cparams())(mem, g_mem, wkv3)


def _fwd_in(x, g_mix, w_in3, b_in, ts):
    s, d = x.shape

    def body(x_ref, g_ref, w_ref, b_ref, z_ref, a_ref):
        hn, _, _ = _rms_fwd(x_ref[...], g_ref[...])
        hb = _bf(hn)
        for b in range(N_DEV):
            z_ref[:, 256 * b:256 * (b + 1)] = _dot(hb, w_ref[b]) + b_ref[:, 256 * b:256 * (b + 1)]
        a_ref[...] = z_ref[:, 0:512] * jax.nn.sigmoid(z_ref[:, 512:1024])

    return pl.pallas_call(
        body, name="fwd_in", grid=(s // ts,),
        in_specs=[_row(ts, d), _const(g_mix.shape), _const(w_in3.shape), _const(b_in.shape)],
        out_specs=(_row(ts, 2048), _row(ts, 512)),
        out_shape=(SDS((s, 2048), F32), SDS((s, 512), F32)),
        compiler_params=_cparams("arbitrary"))(x, g_mix, w_in3, b_in)


def _conv_fwd(a, w, b):
    s, cw = a.shape
    rc = 256 if s % 256 == 0 else 128

    def body(a_ref, w_ref, b_ref, c_ref, pad):
        pad[0:CONV_PAD, :] = jnp.zeros((CONV_PAD, 128), F32)
        pad[CONV_PAD:, :] = a_ref[...]

        def chunk(i, carry):
            r0 = pl.multiple_of(i * rc, rc)
            acc = jnp.zeros((rc, 128), F32) + b_ref[...]
            for res, offs in _by_residue(range(CONV_PAD - CONV_K + 1, CONV_PAD + 1)):
                shifted = pad[pl.ds(r0 + res, rc + offs[-1] - res), :]
                for off in offs:
                    k = off - (CONV_PAD - CONV_K + 1)
                    acc = acc + w_ref[k:k + 1, :] * shifted[off - res:off - res + rc, :]
            c_ref[pl.ds(r0, rc), :] = acc
            return carry

        lax.fori_loop(0, s // rc, chunk, 0)

    blk = lambda r: pl.BlockSpec((r, 128), lambda j: (0, j))
    return pl.pallas_call(
        body, name="conv_fwd", grid=(cw // 128,),
        in_specs=[blk(s), blk(CONV_K), blk(1)], out_specs=blk(s), out_shape=SDS((s, cw), F32),
        scratch_shapes=[pltpu.VMEM((s + CONV_PAD, 128), F32)],
        compiler_params=_cparams("arbitrary"))(a, w, b)


def _fwd_out(x, c, z, cln_g, cln_b, gln_g, gln_b, ws, bst, w_out, ts):
    s, d = x.shape

    def body(x_ref, c_ref, zuv_ref, clg, clb, glg, glb, ws_ref, bst_ref, wo_ref, h1_ref, mixed_scr):
        cl, _, _ = _ln_fwd(c_ref[...], clg[...], clb[...])
        co = cl * jax.nn.sigmoid(cl)
        u, _ = _gelu(zuv_ref[:, 0:512])
        vg, _ = _gelu(zuv_ref[:, 512:1024])
        vln, _, _ = _ln_fwd(vg, glg[...], glb[...])
        _mix_fwd(_bf(vln), ws_ref, bst_ref, mixed_scr, ts)
        gm = u * mixed_scr[...]
        h1_ref[...] = x_ref[...] + _dot(_bf(co), wo_ref[0:512, :]) + _dot(_bf(gm), wo_ref[512:1024, :])

    return pl.pallas_call(
        body, name="fwd_out", grid=(s // ts,),
        in_specs=[_row(ts, d), _row(ts, 512), _row(ts, 1024, 1), _const(cln_g.shape), _const(cln_b.shape),
                  _const(gln_g.shape), _const(gln_b.shape), _const(ws.shape), _const(bst.shape), _const(w_out.shape)],
        out_specs=_row(ts, d), out_shape=SDS((s, d), F32),
        scratch_shapes=[pltpu.VMEM((ts, 512), F32)],
        compiler_params=_cparams("arbitrary"))(x, c, z, cln_g, cln_b, gln_g, gln_b, ws, bst, w_out)


def _softmax_rows(sc):
    m = jnp.max(sc, axis=-1, keepdims=True)
    e = jnp.exp(sc - m)
    return e / jnp.sum(e, axis=-1, keepdims=True)


def _fwd_xa(h1, g_xa, wq, kv, wo, ts):
    s, d = h1.shape
    scale = XA_DH ** -0.5

    def body(h_ref, g_ref, wq_ref, kv_ref, wo_ref, h2_ref, q_ref, o_ref, o_scr):
        hn, _, _ = _rms_fwd(h_ref[...], g_ref[...])
        q_ref[...] = _bf(_dot(_bf(hn), wq_ref[...]))
        for h in range(XA_HEADS):
            qh = q_ref[:, XA_DH * h:XA_DH * (h + 1)]
            kh = _bf(kv_ref[:, XA_DH * h:XA_DH * (h + 1)])
            vh = _bf(kv_ref[:, d + XA_DH * h:d + XA_DH * (h + 1)])
            p = _softmax_rows(_dot(qh, kh, NT) * scale)
            o_scr[:, XA_DH * h:XA_DH * (h + 1)] = _dot(_bf(p), vh)
        o_ref[...] = _bf(o_scr[...])
        h2_ref[...] = h_ref[...] + _dot(o_ref[...], wo_ref[...])

    return pl.pallas_call(
        body, name="fwd_xa", grid=(s // ts,),
        in_specs=[_row(ts, d), _const(g_xa.shape), _const(wq.shape), _const(kv.shape), _const(wo.shape)],
        out_specs=(_row(ts, d), _row(ts, d), _row(ts, d)),
        out_shape=(SDS((s, d), F32), SDS((s, d), BF16), SDS((s, d), BF16)),
        scratch_shapes=[pltpu.VMEM((ts, d), F32)],
        compiler_params=_cparams("arbitrary"))(h1, g_xa, wq, kv, wo)


def _fwd_ffn(h2, g_ffn, wgut, wdown, g_final, target, ts):
    s, d = h2.shape
    hid = wdown.shape[0]
    hc = hid // 2

    def body(h_ref, g_ref, wgu_ref, wd_ref, gf_ref, t_ref, dh3_ref, dh3b_ref, gu_ref, hn_ref, loss_ref, dgf_ref):
        hn, _, _ = _rms_fwd(h_ref[...], g_ref[...])
        hb = _bf(hn)
        hn_ref[...] = hb
        h3 = h_ref[...]
        for n in range(2):
            g = _dot(hb, wgu_ref[hc * n:hc * (n + 1), :], NT)
            u = _dot(hb, wgu_ref[hid + hc * n:hid + hc * (n + 1), :], NT)
            gu_ref[:, hc * n:hc * (n + 1)] = g
            gu_ref[:, hid + hc * n:hid + hc * (n + 1)] = u
            act = g * jax.nn.sigmoid(g) * u
            h3 = h3 + _dot(_bf(act), wd_ref[hc * n:hc * (n + 1), :])
        y, xh, r = _rms_fwd(h3, gf_ref[...])
        diff = y - t_ref[...]
        part = 0.5 * jnp.sum(jnp.mean(diff * diff, axis=-1, keepdims=True), axis=0, keepdims=True)
        _accumulate(loss_ref, jnp.zeros(loss_ref.shape, F32) + part)
        dh3, dgf = _rms_bwd(diff * (1.0 / d), xh, r, gf_ref[...])
        dh3_ref[...] = dh3
        dh3b_ref[...] = _bf(dh3)
        _accumulate(dgf_ref, dgf)

    return pl.pallas_call(
        body, name="fwd_ffn", grid=(s // ts,),
        in_specs=[_row(ts, d), _const(g_ffn.shape), _const(wgut.shape), _const(wdown.shape), _const(g_final.shape),
                  _row(ts, d)],
        out_specs=(_row(ts, d), _row(ts, d), _row(ts, 2 * hid), _row(ts, d), _acc((1, 128)), _acc((1, d))),
        out_shape=(SDS((s, d), F32), SDS((s, d), BF16), SDS((s, 2 * hid), F32), SDS((s, d), BF16), SDS((1, 128), F32),
                   SDS((1, d), F32)),
        compiler_params=_cparams("arbitrary"))(h2, g_ffn, wgut, wdown, g_final, target)


def _bwd_ffn(h2, dh3, gu, g_ffn, wgut, wdown, ts):
    s, d = h2.shape
    hid = wdown.shape[0]
    hc = hid // 2

    def body(h_ref, dh3_ref, gu_ref, g_ref, wgu_ref, wd_ref, dh2_ref, act_ref, dgu_ref, dg_ref):
        _, xh, r = _rms_fwd(h_ref[...], g_ref[...])
        db = _bf(dh3_ref[...])
        dhn = jnp.zeros((ts, d), F32)
        for n in range(2):
            wg = wgu_ref[hc * n:hc * (n + 1), :]
            wu = wgu_ref[hid + hc * n:hid + hc * (n + 1), :]
            g = gu_ref[:, hc * n:hc * (n + 1)]
            u = gu_ref[:, hid + hc * n:hid + hc * (n + 1)]
            sg = jax.nn.sigmoid(g)
            sl = g * sg
            act_ref[:, hc * n:hc * (n + 1)] = _bf(sl * u)
            dact = _dot(db, wd_ref[hc * n:hc * (n + 1), :], NT)
            dgb = _bf(dact * u * _silu_grad(g, sg))
            dub = _bf(dact * sl)
            dgu_ref[:, hc * n:hc * (n + 1)] = dgb
            dgu_ref[:, hid + hc * n:hid + hc * (n + 1)] = dub
            dhn = dhn + _dot(dgb, wg) + _dot(dub, wu)
        dx, dg = _rms_bwd(dhn, xh, r, g_ref[...])
        dh2_ref[...] = dh3_ref[...] + dx
        _accumulate(dg_ref, dg)

    return pl.pallas_call(
        body, name="bwd_ffn", grid=(s // ts,),
        in_specs=[_row(ts, d), _row(ts, d), _row(ts, 2 * hid), _const(g_ffn.shape), _const(wgut.shape),
                  _const(wdown.shape)],
        out_specs=(_row(ts, d), _row(ts, hid), _row(ts, 2 * hid), _acc((1, d))),
        out_shape=(SDS((s, d), F32), SDS((s, hid), BF16), SDS((s, 2 * hid), BF16), SDS((1, d), F32)),
        compiler_params=_cparams("arbitrary"))(h2, dh3, gu, g_ffn, wgut, wdown)


def _bwd_xa(h1, dh2, qb, ob, g_xa, wq, wo, kv, ts):
    s, d = h1.shape
    scale = XA_DH ** -0.5

    def body(h_ref, dh2_ref, q_ref, o_ref, g_ref, wq_ref, wo_ref, kv_ref, dh1_ref, dwq_ref, dwo_ref, dkv_ref, dg_ref,
             dq_scr, accq, acco):
        hn, xh, r = _rms_fwd(h_ref[...], g_ref[...])
        hb = _bf(hn)
        dh2b = _bf(dh2_ref[...])
        do = _dot(dh2b, wo_ref[...], NT)

        @pl.when(pl.program_id(0) == 0)
        def _():
            dkv_ref[...] = jnp.zeros_like(dkv_ref)
            accq[...] = jnp.zeros_like(accq)
            acco[...] = jnp.zeros_like(acco)

        for h in range(XA_HEADS):
            lo, hi = XA_DH * h, XA_DH * (h + 1)
            qh = q_ref[:, lo:hi]
            kh = _bf(kv_ref[:, lo:hi])
            vh = _bf(kv_ref[:, d + lo:d + hi])
            p = _softmax_rows(_dot(qh, kh, NT) * scale)
            pb = _bf(p)
            doh = _bf(do[:, lo:hi])
            dp = _dot(doh, vh, NT)
            ds = p * (dp - jnp.sum(p * dp, axis=-1, keepdims=True)) * scale
            dsb = _bf(ds)
            dq_scr[:, lo:hi] = _dot(dsb, kh)
            dkv_ref[:, lo:hi] += _dot(dsb, qh, TN)
            dkv_ref[:, d + lo:d + hi] += _dot(pb, doh, TN)
        dqb = _bf(dq_scr[...])
        accq[...] += _dot(hb, dqb, TN)
        acco[...] += _dot(o_ref[...], dh2b, TN)
        dx, dg = _rms_bwd(_dot(dqb, wq_ref[...], NT), xh, r, g_ref[...])
        dh1_ref[...] = dh2_ref[...] + dx
        _accumulate(dg_ref, dg)

        @pl.when(pl.program_id(0) == pl.num_programs(0) - 1)
        def _():
            dwq_ref[...] = _bf(accq[...])
            dwo_ref[...] = _bf(acco[...])

    return pl.pallas_call(
        body, name="bwd_xa", grid=(s // ts,),
        in_specs=[_row(ts, d), _row(ts, d), _row(ts, d), _row(ts, d), _const(g_xa.shape), _const(wq.shape),
                  _const(wo.shape), _const(kv.shape)],
        out_specs=(_row(ts, d), _acc((d, d)), _acc((d, d)), _acc(kv.shape), _acc((1, d))),
        out_shape=(SDS((s, d), F32), SDS((d, d), BF16), SDS((d, d), BF16), SDS(kv.shape, F32), SDS((1, d), F32)),
        scratch_shapes=[pltpu.VMEM((ts, d), F32), pltpu.VMEM((d, d), F32), pltpu.VMEM((d, d), F32)],
        compiler_params=_cparams("arbitrary"))(h1, dh2, qb, ob, g_xa, wq, wo, kv)


def _bwd_kv(dkv, mn, mem, g_mem, wkv3):
    d = mem.shape[1]

    def body(dkv_ref, mn_ref, mem_ref, g_ref, w_ref, dw_ref, dg_ref):
        dkvb = _bf(dkv_ref[...])
        dmn = jnp.zeros(mem_ref.shape, F32)
        for b in range(N_DEV):
            blk = dkvb[:, 256 * b:256 * (b + 1)]
            dmn = dmn + _dot(blk, w_ref[b], NT)
            dw_ref[b] = _bf(_dot(mn_ref[...], blk, TN))
        _, xh, r = _rms_fwd(mem_ref[...], g_ref[...])
        _, dg = _rms_bwd(dmn, xh, r, g_ref[...])
        dg_ref[...] = dg

    return pl.pallas_call(body, name="bwd_kv", out_shape=(SDS(wkv3.shape, BF16), SDS((1, d), F32)),
                          compiler_params=_cparams())(dkv, mn, mem, g_mem, wkv3)


def _bwd_out(dh1, c, z, cln_g, cln_b, gln_g, gln_b, ws, wst, bst, w_out, ts):
    s, d = dh1.shape
    nh = GM_HEADS

    def body(dh1_ref, c_ref, zuv_ref, clg, clb, glg, glb, ws_ref, wst_ref, bst_ref, wo_ref,
             dwo_ref, dc_ref, dzuv_ref, dws_ref, dbst_ref, dclg_ref, dclb_ref, dglg_ref, dglb_ref, dbin_ref,
             mixed_scr, dv_scr, acc):
        cl, chat, crs = _ln_fwd(c_ref[...], clg[...], clb[...])
        sg = jax.nn.sigmoid(cl)
        zu = zuv_ref[:, 0:512]
        zv = zuv_ref[:, 512:1024]
        u, tu = _gelu(zu)
        vg, tv = _gelu(zv)
        vln, vhat, vrs = _ln_fwd(vg, glg[...], glb[...])
        vb = _bf(vln)
        _mix_fwd(vb, ws_ref, bst_ref, mixed_scr, ts)
        mixed = mixed_scr[...]
        dh1b = _bf(dh1_ref[...])
        dcat = _dot(dh1b, wo_ref[...], NT)
        dgm = dcat[:, 512:1024]
        dc, dclg, dclb = _ln_bwd(dcat[:, 0:512] * _silu_grad(cl, sg), chat, crs, clg[...])
        dc_ref[...] = dc
        dzu = dgm * mixed * _gelu_grad(zu, tu)
        dm = dgm * u

        @pl.when(pl.program_id(0) == 0)
        def _():
            dws_ref[...] = jnp.zeros_like(dws_ref)
            dbst_ref[...] = jnp.zeros_like(dbst_ref)
            acc[...] = jnp.zeros_like(acc)

        acc[0:512, :] += _dot(_bf(cl * sg), dh1b, TN)
        acc[512:1024, :] += _dot(_bf(u * mixed), dh1b, TN)

        @pl.when(pl.program_id(0) == pl.num_programs(0) - 1)
        def _():
            dwo_ref[...] = _bf(acc[...])

        tril, triu, lo = _mix_masks()
        head = lax.broadcasted_iota(jnp.int32, (1, nh), 1)
        for j in range(nh // 2):
            w0t = _bf(jnp.where(triu, wst_ref[2 * j], 0.0))
            w1t = _bf(jnp.where(triu, wst_ref[2 * j + 1], 0.0))
            for n in range(ts // CHUNK):
                rows = slice(n * CHUNK, (n + 1) * CHUNK)
                lanes = slice(j * 128, (j + 1) * 128)
                dmc = dm[rows, lanes]
                dmb = _bf(dmc)
                dv_scr[rows, lanes] = jnp.where(lo, _dot(w0t, dmb), _dot(w1t, dmb))
                vc = vb[rows, lanes]
                d0 = jnp.where(lo, dmc, 0.0)
                d1 = dmc - d0
                dws_ref[2 * j] += jnp.where(tril, _dot(_bf(d0), vc, NT), 0.0)
                dws_ref[2 * j + 1] += jnp.where(tril, _dot(_bf(d1), vc, NT), 0.0)
                dbst_ref[...] += (jnp.sum(d0, axis=1, keepdims=True) * (head == 2 * j).astype(F32)
                                  + jnp.sum(d1, axis=1, keepdims=True) * (head == 2 * j + 1).astype(F32))
        dvg, dglg, dglb = _ln_bwd(dv_scr[...], vhat, vrs, glg[...])
        dzv = dvg * _gelu_grad(zv, tv)
        dzuv_ref[:, 0:512] = _bf(dzu)
        dzuv_ref[:, 512:1024] = _bf(dzv)
        _accumulate(dclg_ref, dclg)
        _accumulate(dclb_ref, dclb)
        _accumulate(dglg_ref, dglg)
        _accumulate(dglb_ref, dglb)
        _accumulate(dbin_ref, jnp.concatenate([jnp.sum(dzu, axis=0, keepdims=True),
                                               jnp.sum(dzv, axis=0, keepdims=True)], axis=1))

    vec = (1, 512)
    return pl.pallas_call(
        body, name="bwd_out", grid=(s // ts,),
        in_specs=[_row(ts, d), _row(ts, 512), _row(ts, 1024, 1), _const(cln_g.shape), _const(cln_b.shape),
                  _const(gln_g.shape), _const(gln_b.shape), _const(ws.shape), _const(wst.shape), _const(bst.shape),
                  _const(w_out.shape)],
        out_specs=(_acc((d, d)), _row(ts, 512), _row(ts, 1024), _acc(ws.shape), _acc(bst.shape), _acc(vec), _acc(vec),
                   _acc(vec), _acc(vec), _acc((1, 1024))),
        out_shape=(SDS((d, d), BF16), SDS((s, 512), F32), SDS((s, 1024), BF16), SDS(ws.shape, F32),
                   SDS(bst.shape, F32), SDS(vec, F32), SDS(vec, F32), SDS(vec, F32), SDS(vec, F32), SDS((1, 1024), F32)),
        scratch_shapes=[pltpu.VMEM((ts, 512), F32), pltpu.VMEM((ts, 512), F32), pltpu.VMEM((d, d), F32)],
        compiler_params=_cparams("arbitrary"))(dh1, c, z, cln_g, cln_b, gln_g, gln_b, ws, wst, bst, w_out)


CONV_RC = 256
FWD_TAPS = range(CONV_PAD - CONV_K + 1, CONV_PAD + 1)


def _bwd_in(x, dh1, dc, a, z, dzuv, g_mix, w_in3, conv_w, ts):
    s, d = x.shape
    cw = conv_w.shape[1]
    nb = cw // 128
    n = s // ts
    rc = min(CONV_RC, ts)
    per_halo = ts // CONV_PAD
    w4 = conv_w.reshape(CONV_K, nb, 128).transpose(1, 0, 2)

    def body(x_ref, dh1_ref, dc_ref, a_ref, ahalo_ref, zag_ref, dzuv_ref, g_ref, w_ref, cw_ref,
             dx_ref, dw_ref, dcw_ref, dcb_ref, dbin_ref, dg_ref, dz_ref, acc, pad_d, pad_a, part, da_scr):
        i = pl.program_id(0)

        @pl.when(i == 0)
        def _():
            acc[...] = jnp.zeros_like(acc)
            part[...] = jnp.zeros_like(part)
            pad_d[:, ts:, :] = jnp.zeros((nb, CONV_PAD, 128), F32)

        before = jnp.where(i == n - 1, 0.0, ahalo_ref[...])
        for blk in range(nb):
            lanes = slice(128 * blk, 128 * (blk + 1))
            pad_d[blk, 0:ts, :] = dc_ref[:, lanes]
            pad_a[blk, 0:CONV_PAD, :] = before[:, lanes]
            pad_a[blk, CONV_PAD:, :] = a_ref[:, lanes]

        def rows8(v):
            return jnp.sum(v.reshape(rc // 8, 8, 128), axis=0)

        def unit(u, carry):
            blk = u // (ts // rc)
            r0 = pl.multiple_of((u % (ts // rc)) * rc, rc)
            acc_da = jnp.zeros((rc, 128), F32)
            for res, offs in _by_residue(range(0, CONV_K)):
                shifted = pad_d[blk, pl.ds(r0 + res, rc + offs[-1] - res), :]
                for off in offs:
                    k = CONV_K - 1 - off
                    acc_da = acc_da + cw_ref[blk, k:k + 1, :] * shifted[off - res:off - res + rc, :]
            da_scr[blk, pl.ds(r0, rc), :] = acc_da
            dcc = pad_d[blk, pl.ds(r0, rc), :]
            for res, offs in _by_residue(FWD_TAPS):
                shifted = pad_a[blk, pl.ds(r0 + res, rc + offs[-1] - res), :]
                for off in offs:
                    k = off - FWD_TAPS[0]
                    part[blk, 8 * k:8 * k + 8, :] += rows8(dcc * shifted[off - res:off - res + rc, :])
            part[blk, 8 * CONV_PAD:, :] += rows8(dcc)
            return carry

        lax.fori_loop(0, nb * (ts // rc), unit, 0)
        pad_d[:, ts:, :] = pad_d[:, 0:CONV_PAD, :]

        za = zag_ref[:, 0:512]
        sg = jax.nn.sigmoid(zag_ref[:, 512:1024])
        da_ = jnp.concatenate([da_scr[blk] for blk in range(nb)], axis=1)
        dza = da_ * sg
        dzg = da_ * za * sg * (1.0 - sg)
        dz_ref[:, 0:512] = _bf(dza)
        dz_ref[:, 512:1024] = _bf(dzg)
        dz_ref[:, 1024:2048] = dzuv_ref[...]
        dhn = jnp.zeros((ts, d), F32)
        for b in range(N_DEV):
            dhn = dhn + _dot(dz_ref[:, 256 * b:256 * (b + 1)], w_ref[b], NT)
        hn, xh, r = _rms_fwd(x_ref[...], g_ref[...])
        acc[...] += _dot(_bf(hn), dz_ref[...], TN)

        @pl.when(i == n - 1)
        def _():
            for b in range(N_DEV):
                dw_ref[b] = _bf(acc[:, 256 * b:256 * (b + 1)])
            for blk in range(nb):
                sums = jnp.sum(part[blk].reshape(CONV_PAD + 1, 8, 128), axis=1)
                dcw_ref[:, 128 * blk:128 * (blk + 1)] = sums[0:CONV_PAD, :]
                dcb_ref[:, 128 * blk:128 * (blk + 1)] = sums[CONV_PAD:, :]

        dxn, dg = _rms_bwd(dhn, xh, r, g_ref[...])
        dx_ref[...] = dh1_ref[...] + dxn
        _accumulate(dg_ref, dg)
        _accumulate(dbin_ref, jnp.concatenate([jnp.sum(dza, axis=0, keepdims=True),
                                               jnp.sum(dzg, axis=0, keepdims=True)], axis=1))

    back = lambda width, col=0: pl.BlockSpec((ts, width), lambda i: (n - 1 - i, col))
    halo = pl.BlockSpec((CONV_PAD, cw), lambda i: (jnp.maximum((n - 1 - i) * per_halo - 1, 0), 0))
    return pl.pallas_call(
        body, name="bwd_in", grid=(n,),
        in_specs=[back(d), back(d), back(cw), back(cw), halo, back(1024, 0), back(1024), _const(g_mix.shape),
                  _const(w_in3.shape), _const(w4.shape)],
        out_specs=(back(d), _acc(w_in3.shape), _acc((CONV_PAD, cw)), _acc((1, cw)), _acc((1, 1024)), _acc((1, d))),
        out_shape=(SDS((s, d), F32), SDS(w_in3.shape, BF16), SDS((CONV_PAD, cw), F32), SDS((1, cw), F32),
                   SDS((1, 1024), F32), SDS((1, d), F32)),
        scratch_shapes=[pltpu.VMEM((ts, 2048), BF16), pltpu.VMEM((d, 2048), F32),
                        pltpu.VMEM((nb, ts + CONV_PAD, 128), F32), pltpu.VMEM((nb, ts + CONV_PAD, 128), F32),
                        pltpu.VMEM((nb, 8 * (CONV_PAD + 1), 128), F32), pltpu.VMEM((nb, ts, 128), F32)],
        compiler_params=_cparams("arbitrary"))(x, dh1, dc, a, a, z, dzuv, g_mix, w_in3, w4)


def _tn_matmul(a, b, tm, name):
    s, m = a.shape
    n = b.shape[1]
    ts = _row_tile(s, (1024, 512, 256, 128))
    n_s = s // ts

    def body(a_ref, b_ref, o_ref, acc):
        k = pl.program_id(1)

        @pl.when(k == 0)
        def _():
            acc[...] = jnp.zeros_like(acc)

        acc[...] += _dot(a_ref[...], b_ref[...], TN)

        @pl.when(k == n_s - 1)
        def _():
            o_ref[...] = _bf(acc[...])

    return pl.pallas_call(
        body, name=name, grid=(m // tm, n_s),
        in_specs=[pl.BlockSpec((ts, tm), lambda i, k: (k, i)), pl.BlockSpec((ts, n), lambda i, k: (k, 0))],
        out_specs=pl.BlockSpec((tm, n), lambda i, k: (i, 0)), out_shape=SDS((m, n), BF16),
        scratch_shapes=[pltpu.VMEM((tm, n), F32)],
        compiler_params=_cparams("parallel", "arbitrary"))(a, b)


def _row_tile(r, cands):
    for cand in cands:
        if r % cand == 0:
            return cand
    return r


def _sum_in_device_order(me_ref, land_ref, own_ref):
    acc = None
    for dev in range(N_DEV):
        part = jnp.where(me_ref[0] == dev, own_ref[0], land_ref[dev]).astype(F32)
        acc = part if acc is None else acc + part
    return acc


def _sum8(land, own, me, name):
    _, r, c = land.shape
    tr = _row_tile(r, (512, 256, 352, 128, 8))
    own3 = own if own.ndim == 3 else own[None]
    own_map = (lambda i, me_ref: (me_ref[0], i, 0)) if own.ndim == 3 else (lambda i, me_ref: (0, i, 0))

    def body(me_ref, land_ref, own_ref, o_ref):
        o_ref[...] = _sum_in_device_order(me_ref, land_ref, own_ref)

    return pl.pallas_call(
        body, name=name, out_shape=SDS((r, c), F32),
        grid_spec=pltpu.PrefetchScalarGridSpec(
            num_scalar_prefetch=1, grid=(r // tr,),
            in_specs=[pl.BlockSpec((N_DEV, tr, c), lambda i, me_ref: (0, i, 0)), pl.BlockSpec((1, tr, c), own_map)],
            out_specs=pl.BlockSpec((tr, c), lambda i, me_ref: (i, 0))),
        compiler_params=_cparams("parallel"))(me, land, own3)


def _adam_update(w, g, m, v):
    m2 = ADAM_B1 * m + (1.0 - ADAM_B1) * g
    v2 = ADAM_B2 * v + (1.0 - ADAM_B2) * (g * g)
    m_hat = m2 / (1.0 - ADAM_B1 ** ADAM_STEP)
    v_hat = v2 / (1.0 - ADAM_B2 ** ADAM_STEP)
    return -ADAM_LR * (m_hat / (jnp.sqrt(v_hat) + ADAM_EPS) + ADAM_WD * w), m2, v2


def _slot_spec(tr, c, mask):
    return pl.BlockSpec((1, tr, c), lambda i, me_ref: (me_ref[0] ^ mask, i, 0))


def _sum_adamw(land, parts, w, m, v, me, name, masks=ALL_PEERS):
    r, c = w.shape
    tr = _row_tile(r, (256, 128, 176, 8))
    n = len(masks)

    def body(me_ref, own_ref, *refs):
        w_ref, m_ref, v_ref, g_ref, d_ref, m2_ref, v2_ref = refs[n:]
        g = own_ref[0].astype(F32)
        for peer_ref in refs[:n]:
            g = g + peer_ref[0].astype(F32)
        g_ref[...] = g
        d_ref[...], m2_ref[...], v2_ref[...] = _adam_update(w_ref[...], g, m_ref[...], v_ref[...])

    blk = pl.BlockSpec((tr, c), lambda i, me_ref: (i, 0))
    return pl.pallas_call(
        body, name=name, out_shape=(SDS((r, c), F32),) * 4,
        grid_spec=pltpu.PrefetchScalarGridSpec(
            num_scalar_prefetch=1, grid=(r // tr,),
            in_specs=[_slot_spec(tr, c, 0)] + [_slot_spec(tr, c, mask) for mask in masks] + [blk, blk, blk],
            out_specs=(blk,) * 4),
        compiler_params=_cparams("parallel"))(me, parts, *([land] * n), w, m, v)


def _chip_sum(parts, from_sibling, me, name):
    _, r, c = parts.shape
    tr = _row_tile(r, (512, 256, 352, 128, 8))
    spec = pl.BlockSpec((1, tr, c), lambda j, i, me_ref: (me_ref[0] ^ (2 * j), i, 0))

    def body(me_ref, p_ref, q_ref, o_ref):
        o_ref[...] = _bf(p_ref[...].astype(F32) + q_ref[...].astype(F32))

    return pl.pallas_call(
        body, name=name, out_shape=SDS(parts.shape, BF16),
        grid_spec=pltpu.PrefetchScalarGridSpec(num_scalar_prefetch=1, grid=(len(CHIPS), r // tr), in_specs=[spec, spec],
                                               out_specs=spec),
        compiler_params=_cparams("parallel", "parallel"))(me, parts, from_sibling)


def _adamw_many(ws, gs, ms, vs, name):
    n = len(ws)

    def body(*refs):
        ins, outs = refs[:4 * n], refs[4 * n:]
        for i in range(n):
            w, g, m, v = (ins[j * n + i][...] for j in range(4))
            outs[i][...], outs[n + i][...], outs[2 * n + i][...] = _adam_update(w, g, m, v)

    shapes = [SDS(w.shape, F32) for w in ws]
    out = pl.pallas_call(body, name=name, out_shape=shapes * 3, compiler_params=_cparams())(*ws, *gs, *ms, *vs)
    return out[:n], out[n:2 * n], out[2 * n:]


def _pack(arrs):
    flat = jnp.concatenate([a.reshape(-1) for a in arrs])
    pad = (-flat.shape[0]) % (128 * 128)
    return jnp.pad(flat, (0, pad)).reshape(-1, 128)


def _unpack(packed, shapes):
    flat = packed.reshape(-1)
    out, off = [], 0
    for shp in shapes:
        size = 1
        for dim in shp:
            size *= dim
        out.append(flat[off:off + size].reshape(shp))
        off += size
    return out


SMALL = ("norm_mix_g", "b_in", "conv_w", "conv_b", "conv_ln_g", "conv_ln_b", "gm_ln_g", "gm_ln_b", "gm_w_s", "gm_b_s",
         "norm_xa_g", "mem_norm_g", "norm_ffn_g", "final_norm_g")
BIG = ("w_in", "w_out", "xa_wq", "xa_wkv", "xa_wo", "ffn_w_gate_up", "ffn_w_down")
WEIGHTS = ("norm_mix_g", "w_in", "b_in", "conv_w", "conv_b", "conv_ln_g", "conv_ln_b", "gm_ln_g", "gm_ln_b", "gm_w_s",
           "gm_b_s", "w_out", "norm_xa_g", "mem_norm_g", "xa_wq", "xa_wkv", "xa_wo", "norm_ffn_g", "ffn_w_gate_up",
           "ffn_w_down", "final_norm_g")


def kernel(x, mem, norm_mix_g, w_in, b_in, conv_w, conv_b, conv_ln_g, conv_ln_b, gm_ln_g, gm_ln_b, gm_w_s, gm_b_s, w_out, norm_xa_g, mem_norm_g, xa_wq, xa_wkv, xa_wo, norm_ffn_g, ffn_w_gate_up, ffn_w_down, final_norm_g, loss_target, m_norm_mix_g, m_w_in, m_b_in, m_conv_w, m_conv_b, m_conv_ln_g, m_conv_ln_b, m_gm_ln_g, m_gm_ln_b, m_gm_w_s, m_gm_b_s, m_w_out, m_norm_xa_g, m_mem_norm_g, m_xa_wq, m_xa_wkv, m_xa_wo, m_norm_ffn_g, m_ffn_w_gate_up, m_ffn_w_down, m_final_norm_g, v_norm_mix_g, v_w_in, v_b_in, v_conv_w, v_conv_b, v_conv_ln_g, v_conv_ln_b, v_gm_ln_g, v_gm_ln_b, v_gm_w_s, v_gm_b_s, v_w_out, v_norm_xa_g, v_mem_norm_g, v_xa_wq, v_xa_wkv, v_xa_wo, v_norm_ffn_g, v_ffn_w_gate_up, v_ffn_w_down, v_final_norm_g):
    w = dict(norm_mix_g=norm_mix_g, w_in=w_in, b_in=b_in, conv_w=conv_w, conv_b=conv_b, conv_ln_g=conv_ln_g,
             conv_ln_b=conv_ln_b, gm_ln_g=gm_ln_g, gm_ln_b=gm_ln_b, gm_w_s=gm_w_s, gm_b_s=gm_b_s, w_out=w_out,
             norm_xa_g=norm_xa_g, mem_norm_g=mem_norm_g, xa_wq=xa_wq, xa_wkv=xa_wkv, xa_wo=xa_wo,
             norm_ffn_g=norm_ffn_g, ffn_w_gate_up=ffn_w_gate_up, ffn_w_down=ffn_w_down, final_norm_g=final_norm_g)
    mom = dict(norm_mix_g=m_norm_mix_g, w_in=m_w_in, b_in=m_b_in, conv_w=m_conv_w, conv_b=m_conv_b,
               conv_ln_g=m_conv_ln_g, conv_ln_b=m_conv_ln_b, gm_ln_g=m_gm_ln_g, gm_ln_b=m_gm_ln_b, gm_w_s=m_gm_w_s,
               gm_b_s=m_gm_b_s, w_out=m_w_out, norm_xa_g=m_norm_xa_g, mem_norm_g=m_mem_norm_g, xa_wq=m_xa_wq,
               xa_wkv=m_xa_wkv, xa_wo=m_xa_wo, norm_ffn_g=m_norm_ffn_g, ffn_w_gate_up=m_ffn_w_gate_up,
               ffn_w_down=m_ffn_w_down, final_norm_g=m_final_norm_g)
    var = dict(norm_mix_g=v_norm_mix_g, w_in=v_w_in, b_in=v_b_in, conv_w=v_conv_w, conv_b=v_conv_b,
               conv_ln_g=v_conv_ln_g, conv_ln_b=v_conv_ln_b, gm_ln_g=v_gm_ln_g, gm_ln_b=v_gm_ln_b, gm_w_s=v_gm_w_s,
               gm_b_s=v_gm_b_s, w_out=v_w_out, norm_xa_g=v_norm_xa_g, mem_norm_g=v_mem_norm_g, xa_wq=v_xa_wq,
               xa_wkv=v_xa_wkv, xa_wo=v_xa_wo, norm_ffn_g=v_norm_ffn_g, ffn_w_gate_up=v_ffn_w_gate_up,
               ffn_w_down=v_ffn_w_down, final_norm_g=v_final_norm_g)

    me = 4 * lax.axis_index("x") + 2 * lax.axis_index("y") + lax.axis_index("c")
    s, d = x.shape[1], x.shape[2]
    xs = x.reshape(s, d)
    mems = mem.reshape(mem.shape[1], d)
    tgt = loss_target.reshape(s, d)
    ts = min(512, s)
    ts_ffn = min(256, s)
    row = lambda a: a.reshape(1, -1)

    conv_w_pad = jnp.pad(conv_w, ((0, CONV_PAD - CONV_K), (0, 128 - conv_w.shape[1])))
    first_level = (SIBLING,) + OTHER_CHIPS
    ex_first = _exchange_start([_bf(w_in), conv_w_pad], [False] * 2, "gather_start_in", first_level, own_slot=me)
    behind = lambda t: _bf(t + ex_first[4][0:1, 0:1])
    shards = [behind(xa_wkv), behind(w_out), behind(xa_wq), behind(xa_wo), behind(ffn_w_gate_up.T), behind(ffn_w_down)]
    ex_rest = _exchange_start(shards, [False] * len(shards), "gather_start_rest", first_level, own_slot=me)
    g_send, g_recv, g_src, g_land = (list(ex_first[i]) + list(ex_rest[i]) for i in range(4))
    g_tok = ex_rest[4]

    def arrived(idx, after, name):
        pick = lambda seq: [seq[i] for i in idx]
        return _wait_and_forward(pick(g_send), pick(g_recv), pick(g_src), pick(g_land), after, "gather_pass_" + name,
                                 first_level)

    def complete(handle, after, name):
        return _forward_wait(handle[0], handle[1], handle[2], after, "forward_wait_" + name)

    bst = gm_b_s.T
    wst = jnp.swapaxes(gm_w_s, 1, 2)

    h_in = arrived((0, 1), g_tok, "in")
    w_in3, conv_w8 = complete(h_in, h_in[3], "in")
    conv_w_f = conv_w8[:, :CONV_K, :conv_w.shape[1]].transpose(1, 0, 2).reshape(CONV_K, -1)
    cw = conv_w_f.shape[1]
    z, a = _fwd_in(xs, row(norm_mix_g), w_in3, row(b_in), ts)
    h_out = arrived((2, 3), z, "out")
    c = _conv_fwd(a, conv_w_f + h_out[3][0:1, 0:1], row(conv_b))
    h_xa = arrived((4, 5), c, "xa")
    wkv3, w_out3 = complete(h_out, h_xa[3], "out")
    w_out_f = w_out3.reshape(-1, d)
    kv, mn = _kv_proj(mems, row(mem_norm_g), wkv3)
    h1 = _fwd_out(xs, c, z, row(conv_ln_g), row(conv_ln_b), row(gm_ln_g), row(gm_ln_b), gm_w_s, bst, w_out_f, ts)
    h_gut = arrived((6,), h1, "gate_up")
    wq3, wo3 = complete(h_xa, h_gut[3], "xa")
    wq_f = wq3.reshape(-1, d)
    wo_f = wo3.reshape(-1, d)
    h2, qb, ob = _fwd_xa(h1, row(norm_xa_g), wq_f, kv, wo_f, ts)
    h_down = arrived((7,), h2, "down")
    (wgut3,) = complete(h_gut, h_down[3], "gate_up")
    (wdown3,) = complete(h_down, wgut3, "down")
    wgut_f = wgut3.reshape(-1, d)
    wdown_f = wdown3.reshape(-1, d)
    dh3, dh3b, gu, hn2, loss_p, d_final_g = _fwd_ffn(h2, row(norm_ffn_g), wgut_f, wdown_f, row(final_norm_g), tgt,
                                                     ts_ffn)

    blocks = lambda m: m.reshape(N_DEV, -1, d)
    tok = lambda ex: ex[4][0:1, 0:1]
    me1 = me.astype(jnp.int32).reshape(1)

    def chip_sums(pair, after, tag):
        parts, handed = _pair_wait(pair[0], pair[1], pair[2], pair[3], after, "pair_wait_" + tag)
        return [_chip_sum(p, h, me1, "chip_sum_%s_%d" % (tag, i)) for i, (p, h) in enumerate(zip(parts, handed))]

    dh2, act, dgu, d_ffn_g = _bwd_ffn(h2, dh3, gu, row(norm_ffn_g), wgut_f, wdown_f, ts_ffn)
    dwgut = _tn_matmul(dgu, hn2, 1408, "dw_gate_up")
    dwdown = _tn_matmul(act, dh3b, 1408, "dw_down")
    pair_ffn = _pair_start([blocks(dwgut), blocks(dwdown)], "pair_start_ffn")
    dh1, dwq, dwo, dkv, d_xa_g = _bwd_xa(h1, dh2, qb, ob, row(norm_xa_g) + tok(pair_ffn), wq_f, wo_f, kv, ts)
    ex_ffn = _exchange_start(chip_sums(pair_ffn, dh1, "ffn"), [True] * 2, "scatter_start_ffn", OTHER_CHIPS)
    dwkv3, d_mem_g = _bwd_kv(dkv, mn, mems, row(mem_norm_g) + tok(ex_ffn), wkv3)
    pair_xa = _pair_start([blocks(dwq), blocks(dwo), dwkv3], "pair_start_xa")
    (dwout, dc, dzuv, dws, dbst, d_cln_g, d_cln_b, d_gln_g, d_gln_b, dbin_uv) = _bwd_out(
        dh1, c, z, row(conv_ln_g) + tok(pair_xa), row(conv_ln_b), row(gm_ln_g), row(gm_ln_b), gm_w_s, wst, bst,
        w_out_f, ts)
    early = dict(b_in_uv=dbin_uv, conv_ln_g=d_cln_g, conv_ln_b=d_cln_b, gm_ln_g=d_gln_g, gm_ln_b=d_gln_b,
                 gm_b_s=dbst.T, norm_xa_g=d_xa_g, mem_norm_g=d_mem_g, norm_ffn_g=d_ffn_g, final_norm_g=d_final_g,
                 loss=loss_p)
    dws_b = _bf(dws).reshape(GM_HEADS * CHUNK, CHUNK)
    ex_xa = _exchange_start(chip_sums(pair_xa, dc, "xa") + [blocks(dwout), dws_b, _pack(list(early.values()))],
                            [True] * 4 + [False] * 2, "scatter_start_xa", [OTHER_CHIPS] * 3 + [ALL_PEERS] * 3)
    dx, dw_in3, dconv_w, dconv_b, dbin_ag, d_mix_g = _bwd_in(xs, dh1, dc, a, z, dzuv, row(norm_mix_g) + tok(ex_xa),
                                                             w_in3, conv_w_f, ts)
    late = dict(norm_mix_g=d_mix_g, b_in_ag=dbin_ag, conv_w=dconv_w[:CONV_K], conv_b=dconv_b)
    ex_last = _exchange_start([_pack(list(late.values())), dw_in3], [False, True], "scatter_start_in")

    grads, delta, new_m, new_v = {}, {}, {}, {}

    def waited(ex, idx, scatter, after, name, masks=ALL_PEERS):
        return _exchange_wait(*[[ex[i][k] for k in idx] for i in range(4)], scatter, after, name, masks)

    def reduced(names, srcs, lands, masks):
        for nm, src, land, mk in zip(names, srcs, lands, masks):
            view = (lambda t: t.T) if nm == "ffn_w_gate_up" else (lambda t: t)
            upd = _sum_adamw(land, src, view(w[nm]), view(mom[nm]), view(var[nm]), me1, "adamw_" + nm, mk)
            grads[nm], delta[nm], new_m[nm], new_v[nm] = (view(t) for t in upd)
        return new_v[names[-1]]

    after = reduced(("ffn_w_gate_up", "ffn_w_down"),
                    *waited(ex_ffn, (0, 1), [True] * 2, ex_last[4], "scatter_wait_ffn", OTHER_CHIPS), [OTHER_CHIPS] * 2)
    xa_masks = [OTHER_CHIPS] * 3 + [ALL_PEERS]
    after = reduced(("xa_wq", "xa_wo", "xa_wkv", "w_out"),
                    *waited(ex_xa, (0, 1, 2, 3), [True] * 4, after, "scatter_wait_xa", xa_masks), xa_masks)

    (dws_src, early_src), (dws_land, early_land) = waited(ex_xa, (4, 5), [False] * 2, after, "gather_wait_early")
    (late_src,), (late_land,) = waited(ex_last, (0,), [False], early_land, "gather_wait_late")
    dws_tot = _sum8(dws_land, dws_src, me1, "sum_gm_w_s")
    early_tot = _sum8(early_land, early_src, me1, "sum_small_early")
    late_tot = _sum8(late_land, late_src, me1, "sum_small_late")
    early_g = dict(zip(early, _unpack(early_tot, [v.shape for v in early.values()])), gm_w_s=dws_tot)
    late_g = dict(zip(late, _unpack(late_tot, [v.shape for v in late.values()])))
    loss = early_g["loss"][0, 0]
    for nm in SMALL:
        if nm == "b_in":
            g = jnp.concatenate([late_g["b_in_ag"], early_g["b_in_uv"]], axis=1)
        elif nm == "conv_w":
            g = lax.dynamic_slice_in_dim(late_g[nm], me * conv_w.shape[1], conv_w.shape[1], axis=1)
        else:
            g = late_g[nm] if nm in late_g else early_g[nm]
        grads[nm] = g.reshape(w[nm].shape)

    small = [[src[nm] for nm in SMALL] for src in (w, grads, mom, var)]
    for dst, vals in zip((delta, new_m, new_v), _adamw_many(*small, "adamw_small")):
        dst.update(zip(SMALL, vals))

    reduced(("w_in",), *waited(ex_last, (1,), [True], delta[SMALL[0]], "scatter_wait_in"), [ALL_PEERS])

    return (loss, dx.reshape(x.shape), *[grads[nm] for nm in WEIGHTS], *[delta[nm] for nm in WEIGHTS],
            *[new_m[nm] for nm in WEIGHTS], *[new_v[nm] for nm in WEIGHTS])
```

```python
import functools

import jax
import jax.numpy as jnp
from jax import lax
from jax.experimental import pallas as pl
from jax.experimental.pallas import tpu as pltpu

F32 = jnp.float32
BF16 = jnp.bfloat16
SDS = jax.ShapeDtypeStruct

N_DEV = 8
RMS_EPS = 1e-6
LN_EPS = 1e-5
CONV_K = 31
CONV_PAD = 32
CHUNK = 128
GM_HEADS = 8
XA_HEADS = 4
XA_DH = 256
GELU_K0 = 0.7978845608028654
GELU_K1 = 0.044715
ADAM_LR = 0.001
ADAM_B1 = 0.9
ADAM_B2 = 0.999
ADAM_EPS = 1e-08
ADAM_WD = 0.01
ADAM_STEP = 10
VMEM_LIMIT = 60 * 1024 * 1024

NN = (((1,), (0,)), ((), ()))
NT = (((1,), (1,)), ((), ()))
TN = (((0,), (0,)), ((), ()))


def _dot(a, b, dims=NN):
    return lax.dot_general(a, b, dims, preferred_element_type=F32)


def _bf(x):
    return x.astype(BF16)


def _cparams(*sem):
    return pltpu.CompilerParams(dimension_semantics=tuple(sem) if sem else None, vmem_limit_bytes=VMEM_LIMIT)


def _row(ts, w, col=0):
    return pl.BlockSpec((ts, w), lambda i: (i, col))


def _const(shape):
    nd = len(shape)
    return pl.BlockSpec(shape, lambda i: (0,) * nd, pipeline_mode=pl.Buffered(1))


def _acc(shape):
    nd = len(shape)
    return pl.BlockSpec(shape, lambda i: (0,) * nd)


def _rms_fwd(x, g):
    r = lax.rsqrt(jnp.mean(x * x, axis=-1, keepdims=True) + RMS_EPS)
    xh = x * r
    return xh * g, xh, r


def _rms_bwd(dy, xh, r, g):
    gdy = dy * g
    dx = r * (gdy - xh * jnp.mean(gdy * xh, axis=-1, keepdims=True))
    dg = jnp.sum(dy * xh, axis=0, keepdims=True)
    return dx, dg


def _ln_fwd(x, g, b):
    mu = jnp.mean(x, axis=-1, keepdims=True)
    xc = x - mu
    rs = lax.rsqrt(jnp.mean(xc * xc, axis=-1, keepdims=True) + LN_EPS)
    xh = xc * rs
    return xh * g + b, xh, rs


def _ln_bwd(dy, xh, rs, g):
    dxh = dy * g
    dx = rs * (dxh - jnp.mean(dxh, axis=-1, keepdims=True) - xh * jnp.mean(dxh * xh, axis=-1, keepdims=True))
    return dx, jnp.sum(dy * xh, axis=0, keepdims=True), jnp.sum(dy, axis=0, keepdims=True)


def _gelu(x):
    t = jnp.tanh(GELU_K0 * (x + GELU_K1 * (x * x * x)))
    return 0.5 * x * (1.0 + t), t


def _gelu_grad(x, t):
    return 0.5 * (1.0 + t) + 0.5 * x * (1.0 - t * t) * (GELU_K0 * (1.0 + 3.0 * GELU_K1 * x * x))


def _silu_grad(x, sg):
    return sg * (1.0 + x * (1.0 - sg))


def _by_residue(offsets):
    groups = {}
    for off in offsets:
        groups.setdefault(off % 8, []).append(off)
    return [(res, sorted(offs)) for res, offs in sorted(groups.items())]


def _accumulate(ref, val):
    @pl.when(pl.program_id(0) == 0)
    def _():
        ref[...] = jnp.zeros_like(ref)
    ref[...] += val


def _mix_masks():
    row = lax.broadcasted_iota(jnp.int32, (CHUNK, CHUNK), 0)
    col = lax.broadcasted_iota(jnp.int32, (CHUNK, CHUNK), 1)
    return row >= col, row <= col, col < (CHUNK // 2)


def _mix_fwd(vb, ws_ref, bst_ref, mixed_scr, ts):
    tril, _, lo = _mix_masks()
    for j in range(GM_HEADS // 2):
        w0 = _bf(jnp.where(tril, ws_ref[2 * j], 0.0))
        w1 = _bf(jnp.where(tril, ws_ref[2 * j + 1], 0.0))
        bias = jnp.where(lo, bst_ref[:, 2 * j:2 * j + 1], bst_ref[:, 2 * j + 1:2 * j + 2])
        for n in range(ts // CHUNK):
            v = vb[n * CHUNK:(n + 1) * CHUNK, j * 128:(j + 1) * 128]
            mixed_scr[n * CHUNK:(n + 1) * CHUNK, j * 128:(j + 1) * 128] = jnp.where(lo, _dot(w0, v), _dot(w1, v)) + bias


def _exchange(srcs, scatter, name):
    n = len(srcs)

    def body(*refs):
        src_refs, out_refs = refs[:n], refs[n:2 * n]
        send_sems, recv_sems, local_sems = refs[2 * n:]
        x, y, c = lax.axis_index("x"), lax.axis_index("y"), lax.axis_index("c")
        me = 4 * x + 2 * y + c

        def peer_of(mask):
            px = x if not (mask >> 2) & 1 else 1 - x
            py = y if not (mask >> 1) & 1 else 1 - y
            pc = c if not mask & 1 else 1 - c
            return (px, py, pc), 4 * px + 2 * py + pc

        def remote(k, mask):
            peer, pidx = peer_of(mask)
            return pltpu.make_async_remote_copy(
                src_ref=src_refs[k].at[pidx] if scatter else src_refs[k],
                dst_ref=out_refs[k].at[me],
                send_sem=send_sems.at[k, mask - 1], recv_sem=recv_sems.at[k, mask - 1],
                device_id=peer, device_id_type=pl.DeviceIdType.MESH)

        def arrival(k, mask):
            peer, pidx = peer_of(mask)
            return pltpu.make_async_remote_copy(
                src_ref=src_refs[k].at[pidx] if scatter else src_refs[k],
                dst_ref=out_refs[k].at[pidx],
                send_sem=send_sems.at[k, mask - 1], recv_sem=recv_sems.at[k, mask - 1],
                device_id=peer, device_id_type=pl.DeviceIdType.MESH)

        sends, locals_ = [], []
        for k in range(n):
            for mask in range(1, N_DEV):
                cp = remote(k, mask)
                cp.start()
                sends.append(cp)
            lc = pltpu.make_async_copy(src_refs[k].at[me] if scatter else src_refs[k], out_refs[k].at[me],
                                       local_sems.at[k])
            lc.start()
            locals_.append(lc)
        for k in range(n):
            for mask in range(1, N_DEV):
                arrival(k, mask).wait_recv()
        for cp in sends:
            cp.wait_send()
        for lc in locals_:
            lc.wait()

    outs = [SDS((N_DEV,) + tuple(s.shape[1:] if scatter else s.shape), s.dtype) for s in srcs]
    hbm = pl.BlockSpec(memory_space=pl.ANY)
    return pl.pallas_call(
        body, name=name, out_shape=outs, in_specs=[hbm] * n, out_specs=[hbm] * n,
        scratch_shapes=[pltpu.SemaphoreType.DMA((n, N_DEV - 1)), pltpu.SemaphoreType.DMA((n, N_DEV - 1)),
                        pltpu.SemaphoreType.DMA((n,))],
    )(*srcs)


def _peer_of(mask):
    x, y, c = lax.axis_index("x"), lax.axis_index("y"), lax.axis_index("c")
    px = 1 - x if (mask >> 2) & 1 else x
    py = 1 - y if (mask >> 1) & 1 else y
    pc = 1 - c if mask & 1 else c
    return (px, py, pc), 4 * px + 2 * py + pc


ALL_PEERS = tuple(range(1, N_DEV))
OTHER_CHIPS = (2, 4, 6)
CHIPS = (0,) + OTHER_CHIPS
SIBLING = 1


def _split_copy(src_ref, land_ref, send_sem, recv_sem, mask, slot, scatter, outgoing):
    x, y, c = lax.axis_index("x"), lax.axis_index("y"), lax.axis_index("c")
    me = 4 * x + 2 * y + c
    peer, pidx = _peer_of(mask)
    return pltpu.make_async_remote_copy(
        src_ref=src_ref.at[pidx] if scatter else src_ref,
        dst_ref=land_ref.at[me if outgoing else pidx],
        send_sem=send_sem.at[slot], recv_sem=recv_sem.at[slot],
        device_id=peer, device_id_type=pl.DeviceIdType.MESH)


_HBM = pl.BlockSpec(memory_space=pltpu.HBM)
_SEM = pl.BlockSpec(memory_space=pltpu.SEMAPHORE)
_EFFECT = pltpu.SideEffectType.DATAFLOW_SIDE_EFFECTING


def _per_array(masks, n):
    return [tuple(masks)] * n if isinstance(masks[0], int) else [tuple(m) for m in masks]


def _exchange_start(srcs, scatter, name, masks=ALL_PEERS, own_slot=None):
    n = len(srcs)
    masks = _per_array(masks, n)
    lands = [lax.empty((N_DEV,) + tuple(s.shape[1:] if sc else s.shape), s.dtype) for s, sc in zip(srcs, scatter)]
    if own_slot is not None:
        lands = [land if sc else _with_own(land, s, own_slot) for land, s, sc in zip(lands, srcs, scatter)]
    lands = [pltpu.with_memory_space_constraint(land, pltpu.HBM) for land in lands]
    srcs = [pltpu.with_memory_space_constraint(s, pltpu.HBM) for s in srcs]

    def body(*refs):
        src_refs, land_refs = refs[:n], refs[n:2 * n]
        send_sems, recv_sems = refs[2 * n:3 * n], refs[3 * n:4 * n]
        token = refs[-1]
        for k in range(n):
            for slot, mask in enumerate(masks[k]):
                _split_copy(src_refs[k], land_refs[k], send_sems[k], recv_sems[k], mask, slot, scatter[k], True).start()
        token[...] = jnp.zeros_like(token)

    sems = [pltpu.SemaphoreType.DMA((len(m),)) for m in masks]
    out = pl.pallas_call(
        body, name=name,
        out_shape=tuple(sems + sems + [pltpu.HBM(s.shape, s.dtype) for s in srcs]
                        + [pltpu.HBM(l.shape, l.dtype) for l in lands] + [SDS((8, 128), F32)]),
        in_specs=[_HBM] * (2 * n),
        out_specs=tuple([_SEM] * (2 * n) + [_HBM] * (2 * n) + [pl.BlockSpec(memory_space=pltpu.VMEM)]),
        input_output_aliases={i: 2 * n + i for i in range(2 * n)},
        compiler_params=pltpu.CompilerParams(has_side_effects=_EFFECT),
    )(*srcs, *lands)
    return out[:n], out[n:2 * n], out[2 * n:3 * n], out[3 * n:4 * n], out[-1]


def _exchange_wait(send_sems, recv_sems, srcs_thru, lands_thru, scatter, after, name, masks=ALL_PEERS):
    n = len(srcs_thru)
    masks = _per_array(masks, n)

    def body(*refs):
        src_refs, land_refs = refs[:n], refs[n:2 * n]
        send_refs, recv_refs = refs[2 * n:3 * n], refs[3 * n:4 * n]
        for k in range(n):
            for slot, mask in enumerate(masks[k]):
                args = (src_refs[k], land_refs[k], send_refs[k], recv_refs[k], mask, slot, scatter[k])
                _split_copy(*args, True).wait_send()
                _split_copy(*args, False).wait_recv()

    out = pl.pallas_call(
        body, name=name,
        out_shape=tuple([pltpu.HBM(s.shape, s.dtype) for s in srcs_thru]
                        + [pltpu.HBM(l.shape, l.dtype) for l in lands_thru]),
        in_specs=[_HBM] * (2 * n) + [_SEM] * (2 * n) + [pl.BlockSpec(memory_space=pl.ANY)],
        out_specs=tuple([_HBM] * (2 * n)),
        input_output_aliases={i: i for i in range(2 * n)},
        compiler_params=pltpu.CompilerParams(has_side_effects=_EFFECT),
    )(*srcs_thru, *lands_thru, *send_sems, *recv_sems, after)
    return out[:n], out[n:]


def _wait_and_forward(send_sems, recv_sems, srcs_thru, lands_thru, after, name, masks):
    n = len(lands_thru)

    def body(*refs):
        src_refs, land_refs = refs[:n], refs[n:2 * n]
        send_refs, recv_refs = refs[2 * n:3 * n], refs[3 * n:4 * n]
        outs = refs[4 * n + 1:]
        fsend, frecv, token = outs[2 * n:3 * n], outs[3 * n:4 * n], outs[-1]
        for k in range(n):
            for slot, mask in enumerate(masks):
                args = (src_refs[k], land_refs[k], send_refs[k], recv_refs[k], mask, slot, False)
                _split_copy(*args, True).wait_send()
                _split_copy(*args, False).wait_recv()
        sibling, _ = _peer_of(SIBLING)
        for k in range(n):
            for slot, mask in enumerate(OTHER_CHIPS):
                _, mine = _peer_of(mask)
                pltpu.make_async_remote_copy(
                    src_ref=land_refs[k].at[mine], dst_ref=land_refs[k].at[mine], send_sem=fsend[k].at[slot],
                    recv_sem=frecv[k].at[slot], device_id=sibling, device_id_type=pl.DeviceIdType.MESH).start()
        token[...] = jnp.zeros_like(token)

    sem = pltpu.SemaphoreType.DMA((len(OTHER_CHIPS),))
    out = pl.pallas_call(
        body, name=name,
        out_shape=tuple([pltpu.HBM(s.shape, s.dtype) for s in srcs_thru] + [pltpu.HBM(l.shape, l.dtype) for l in lands_thru]
                        + [sem] * (2 * n) + [SDS((8, 128), F32)]),
        in_specs=[_HBM] * (2 * n) + [_SEM] * (2 * n) + [pl.BlockSpec(memory_space=pl.ANY)],
        out_specs=tuple([_HBM] * (2 * n) + [_SEM] * (2 * n) + [pl.BlockSpec(memory_space=pltpu.VMEM)]),
        input_output_aliases={i: i for i in range(2 * n)},
        compiler_params=pltpu.CompilerParams(has_side_effects=_EFFECT),
    )(*srcs_thru, *lands_thru, *send_sems, *recv_sems, after)
    return out[2 * n:3 * n], out[3 * n:4 * n], out[n:2 * n], out[-1]


def _forward_wait(send_sems, recv_sems, lands, after, name):
    n = len(lands)

    def body(*refs):
        land_refs, send_refs, recv_refs = refs[:n], refs[n:2 * n], refs[2 * n:3 * n]
        sibling, _ = _peer_of(SIBLING)
        for k in range(n):
            for slot, mask in enumerate(OTHER_CHIPS):
                _, mine = _peer_of(mask)
                _, theirs = _peer_of(mask | SIBLING)
                for block, wait in ((mine, "wait_send"), (theirs, "wait_recv")):
                    getattr(pltpu.make_async_remote_copy(
                        src_ref=land_refs[k].at[block], dst_ref=land_refs[k].at[block],
                        send_sem=send_refs[k].at[slot], recv_sem=recv_refs[k].at[slot], device_id=sibling,
                        device_id_type=pl.DeviceIdType.MESH), wait)()

    return pl.pallas_call(
        body, name=name, out_shape=tuple(pltpu.HBM(l.shape, l.dtype) for l in lands),
        in_specs=[_HBM] * n + [_SEM] * (2 * n) + [pl.BlockSpec(memory_space=pl.ANY)],
        out_specs=tuple([_HBM] * n), input_output_aliases={i: i for i in range(n)},
        compiler_params=pltpu.CompilerParams(has_side_effects=_EFFECT),
    )(*lands, *send_sems, *recv_sems, after)


def _pair_copy(parts_ref, land_ref, send_sem, recv_sem, slot, chip, outgoing):
    sibling, _ = _peer_of(SIBLING)
    _, block = _peer_of(chip | SIBLING if outgoing else chip)
    return pltpu.make_async_remote_copy(
        src_ref=parts_ref.at[block], dst_ref=land_ref.at[block], send_sem=send_sem.at[slot], recv_sem=recv_sem.at[slot],
        device_id=sibling, device_id_type=pl.DeviceIdType.MESH)


def _pair_start(parts, name):
    n = len(parts)
    lands = [pltpu.with_memory_space_constraint(lax.empty(p.shape, p.dtype), pltpu.HBM) for p in parts]
    parts = [pltpu.with_memory_space_constraint(p, pltpu.HBM) for p in parts]

    def body(*refs):
        part_refs, land_refs = refs[:n], refs[n:2 * n]
        send_sems, recv_sems, token = refs[2 * n:3 * n], refs[3 * n:4 * n], refs[-1]
        for k in range(n):
            for slot, chip in enumerate(CHIPS):
                _pair_copy(part_refs[k], land_refs[k], send_sems[k], recv_sems[k], slot, chip, True).start()
        token[...] = jnp.zeros_like(token)

    sem = pltpu.SemaphoreType.DMA((len(CHIPS),))
    out = pl.pallas_call(
        body, name=name,
        out_shape=tuple([sem] * (2 * n) + [pltpu.HBM(p.shape, p.dtype) for p in parts + lands] + [SDS((8, 128), F32)]),
        in_specs=[_HBM] * (2 * n),
        out_specs=tuple([_SEM] * (2 * n) + [_HBM] * (2 * n) + [pl.BlockSpec(memory_space=pltpu.VMEM)]),
        input_output_aliases={i: 2 * n + i for i in range(2 * n)},
        compiler_params=pltpu.CompilerParams(has_side_effects=_EFFECT),
    )(*parts, *lands)
    return out[:n], out[n:2 * n], out[2 * n:3 * n], out[3 * n:4 * n], out[-1]


def _pair_wait(send_sems, recv_sems, parts_thru, lands_thru, after, name):
    n = len(parts_thru)

    def body(*refs):
        part_refs, land_refs = refs[:n], refs[n:2 * n]
        send_refs, recv_refs = refs[2 * n:3 * n], refs[3 * n:4 * n]
        for k in range(n):
            for slot, chip in enumerate(CHIPS):
                args = (part_refs[k], land_refs[k], send_refs[k], recv_refs[k], slot, chip)
                _pair_copy(*args, True).wait_send()
                _pair_copy(*args, False).wait_recv()

    out = pl.pallas_call(
        body, name=name,
        out_shape=tuple(pltpu.HBM(p.shape, p.dtype) for p in list(parts_thru) + list(lands_thru)),
        in_specs=[_HBM] * (2 * n) + [_SEM] * (2 * n) + [pl.BlockSpec(memory_space=pl.ANY)],
        out_specs=tuple([_HBM] * (2 * n)), input_output_aliases={i: i for i in range(2 * n)},
        compiler_params=pltpu.CompilerParams(has_side_effects=_EFFECT),
    )(*parts_thru, *lands_thru, *send_sems, *recv_sems, after)
    return out[:n], out[n:]


def _with_own(landed, own, me):
    return lax.dynamic_update_slice_in_dim(landed, own[None], me, axis=0)


def _kv_proj(mem, g_mem, wkv3):
    m = mem.shape[0]

    def body(mem_ref, g_ref, w_ref, kv_ref, mn_ref):
        y, _, _ = _rms_fwd(mem_ref[...], g_ref[...])
        yb = _bf(y)
        mn_ref[...] = yb
        for b in range(N_DEV):
            kv_ref[:, 256 * b:256 * (b + 1)] = _dot(yb, w_ref[b])

    return pl.pallas_call(body, name="kv_proj", out_shape=(SDS((m, 2048), F32), SDS(mem.shape, BF16)),
                          compiler_params=_cparams())(mem, g_mem, wkv3)


def _fwd_in(x, g_mix, w_in3, b_in, ts):
    s, d = x.shape

    def body(x_ref, g_ref, w_ref, b_ref, z_ref, a_ref):
        hn, _, _ = _rms_fwd(x_ref[...], g_ref[...])
        hb = _bf(hn)
        for b in range(N_DEV):
            z_ref[:, 256 * b:256 * (b + 1)] = _dot(hb, w_ref[b]) + b_ref[:, 256 * b:256 * (b + 1)]
        a_ref[...] = z_ref[:, 0:512] * jax.nn.sigmoid(z_ref[:, 512:1024])

    return pl.pallas_call(
        body, name="fwd_in", grid=(s // ts,),
        in_specs=[_row(ts, d), _const(g_mix.shape), _const(w_in3.shape), _const(b_in.shape)],
        out_specs=(_row(ts, 2048), _row(ts, 512)),
        out_shape=(SDS((s, 2048), F32), SDS((s, 512), F32)),
        compiler_params=_cparams("arbitrary"))(x, g_mix, w_in3, b_in)


def _conv_fwd(a, w, b):
    s, cw = a.shape
    rc = 256 if s % 256 == 0 else 128

    def body(a_ref, w_ref, b_ref, c_ref, pad):
        pad[0:CONV_PAD, :] = jnp.zeros((CONV_PAD, 128), F32)
        pad[CONV_PAD:, :] = a_ref[...]

        def chunk(i, carry):
            r0 = pl.multiple_of(i * rc, rc)
            acc = jnp.zeros((rc, 128), F32) + b_ref[...]
            for res, offs in _by_residue(range(CONV_PAD - CONV_K + 1, CONV_PAD + 1)):
                shifted = pad[pl.ds(r0 + res, rc + offs[-1] - res), :]
                for off in offs:
                    k = off - (CONV_PAD - CONV_K + 1)
                    acc = acc + w_ref[k:k + 1, :] * shifted[off - res:off - res + rc, :]
            c_ref[pl.ds(r0, rc), :] = acc
            return carry

        lax.fori_loop(0, s // rc, chunk, 0)

    blk = lambda r: pl.BlockSpec((r, 128), lambda j: (0, j))
    return pl.pallas_call(
        body, name="conv_fwd", grid=(cw // 128,),
        in_specs=[blk(s), blk(CONV_K), blk(1)], out_specs=blk(s), out_shape=SDS((s, cw), F32),
        scratch_shapes=[pltpu.VMEM((s + CONV_PAD, 128), F32)],
        compiler_params=_cparams("arbitrary"))(a, w, b)


def _fwd_out(x, c, z, cln_g, cln_b, gln_g, gln_b, ws, bst, w_out, ts):
    s, d = x.shape

    def body(x_ref, c_ref, zuv_ref, clg, clb, glg, glb, ws_ref, bst_ref, wo_ref, h1_ref, mixed_scr):
        cl, _, _ = _ln_fwd(c_ref[...], clg[...], clb[...])
        co = cl * jax.nn.sigmoid(cl)
        u, _ = _gelu(zuv_ref[:, 0:512])
        vg, _ = _gelu(zuv_ref[:, 512:1024])
        vln, _, _ = _ln_fwd(vg, glg[...], glb[...])
        _mix_fwd(_bf(vln), ws_ref, bst_ref, mixed_scr, ts)
        gm = u * mixed_scr[...]
        h1_ref[...] = x_ref[...] + _dot(_bf(co), wo_ref[0:512, :]) + _dot(_bf(gm), wo_ref[512:1024, :])

    return pl.pallas_call(
        body, name="fwd_out", grid=(s // ts,),
        in_specs=[_row(ts, d), _row(ts, 512), _row(ts, 1024, 1), _const(cln_g.shape), _const(cln_b.shape),
                  _const(gln_g.shape), _const(gln_b.shape), _const(ws.shape), _const(bst.shape), _const(w_out.shape)],
        out_specs=_row(ts, d), out_shape=SDS((s, d), F32),
        scratch_shapes=[pltpu.VMEM((ts, 512), F32)],
        compiler_params=_cparams("arbitrary"))(x, c, z, cln_g, cln_b, gln_g, gln_b, ws, bst, w_out)


def _softmax_rows(sc):
    m = jnp.max(sc, axis=-1, keepdims=True)
    e = jnp.exp(sc - m)
    return e / jnp.sum(e, axis=-1, keepdims=True)


def _fwd_xa(h1, g_xa, wq, kv, wo, ts):
    s, d = h1.shape
    scale = XA_DH ** -0.5

    def body(h_ref, g_ref, wq_ref, kv_ref, wo_ref, h2_ref, q_ref, o_ref, o_scr):
        hn, _, _ = _rms_fwd(h_ref[...], g_ref[...])
        q_ref[...] = _bf(_dot(_bf(hn), wq_ref[...]))
        for h in range(XA_HEADS):
            qh = q_ref[:, XA_DH * h:XA_DH * (h + 1)]
            kh = _bf(kv_ref[:, XA_DH * h:XA_DH * (h + 1)])
            vh = _bf(kv_ref[:, d + XA_DH * h:d + XA_DH * (h + 1)])
            p = _softmax_rows(_dot(qh, kh, NT) * scale)
            o_scr[:, XA_DH * h:XA_DH * (h + 1)] = _dot(_bf(p), vh)
        o_ref[...] = _bf(o_scr[...])
        h2_ref[...] = h_ref[...] + _dot(o_ref[...], wo_ref[...])

    return pl.pallas_call(
        body, name="fwd_xa", grid=(s // ts,),
        in_specs=[_row(ts, d), _const(g_xa.shape), _const(wq.shape), _const(kv.shape), _const(wo.shape)],
        out_specs=(_row(ts, d), _row(ts, d), _row(ts, d)),
        out_shape=(SDS((s, d), F32), SDS((s, d), BF16), SDS((s, d), BF16)),
        scratch_shapes=[pltpu.VMEM((ts, d), F32)],
        compiler_params=_cparams("arbitrary"))(h1, g_xa, wq, kv, wo)


def _fwd_ffn(h2, g_ffn, wgut, wdown, g_final, target, ts):
    s, d = h2.shape
    hid = wdown.shape[0]
    hc = hid // 2

    def body(h_ref, g_ref, wgu_ref, wd_ref, gf_ref, t_ref, dh3_ref, dh3b_ref, gu_ref, hn_ref, loss_ref, dgf_ref):
        hn, _, _ = _rms_fwd(h_ref[...], g_ref[...])
        hb = _bf(hn)
        hn_ref[...] = hb
        h3 = h_ref[...]
        for n in range(2):
            g = _dot(hb, wgu_ref[hc * n:hc * (n + 1), :], NT)
            u = _dot(hb, wgu_ref[hid + hc * n:hid + hc * (n + 1), :], NT)
            gu_ref[:, hc * n:hc * (n + 1)] = g
            gu_ref[:, hid + hc * n:hid + hc * (n + 1)] = u
            act = g * jax.nn.sigmoid(g) * u
            h3 = h3 + _dot(_bf(act), wd_ref[hc * n:hc * (n + 1), :])
        y, xh, r = _rms_fwd(h3, gf_ref[...])
        diff = y - t_ref[...]
        part = 0.5 * jnp.sum(jnp.mean(diff * diff, axis=-1, keepdims=True), axis=0, keepdims=True)
        _accumulate(loss_ref, jnp.zeros(loss_ref.shape, F32) + part)
        dh3, dgf = _rms_bwd(diff * (1.0 / d), xh, r, gf_ref[...])
        dh3_ref[...] = dh3
        dh3b_ref[...] = _bf(dh3)
        _accumulate(dgf_ref, dgf)

    return pl.pallas_call(
        body, name="fwd_ffn", grid=(s // ts,),
        in_specs=[_row(ts, d), _const(g_ffn.shape), _const(wgut.shape), _const(wdown.shape), _const(g_final.shape),
                  _row(ts, d)],
        out_specs=(_row(ts, d), _row(ts, d), _row(ts, 2 * hid), _row(ts, d), _acc((1, 128)), _acc((1, d))),
        out_shape=(SDS((s, d), F32), SDS((s, d), BF16), SDS((s, 2 * hid), F32), SDS((s, d), BF16), SDS((1, 128), F32),
                   SDS((1, d), F32)),
        compiler_params=_cparams("arbitrary"))(h2, g_ffn, wgut, wdown, g_final, target)


def _bwd_ffn(h2, dh3, gu, g_ffn, wgut, wdown, ts):
    s, d = h2.shape
    hid = wdown.shape[0]
    hc = hid // 2

    def body(h_ref, dh3_ref, gu_ref, g_ref, wgu_ref, wd_ref, dh2_ref, act_ref, dgu_ref, dg_ref):
        _, xh, r = _rms_fwd(h_ref[...], g_ref[...])
        db = _bf(dh3_ref[...])
        dhn = jnp.zeros((ts, d), F32)
        for n in range(2):
            wg = wgu_ref[hc * n:hc * (n + 1), :]
            wu = wgu_ref[hid + hc * n:hid + hc * (n + 1), :]
            g = gu_ref[:, hc * n:hc * (n + 1)]
            u = gu_ref[:, hid + hc * n:hid + hc * (n + 1)]
            sg = jax.nn.sigmoid(g)
            sl = g * sg
            act_ref[:, hc * n:hc * (n + 1)] = _bf(sl * u)
            dact = _dot(db, wd_ref[hc * n:hc * (n + 1), :], NT)
            dgb = _bf(dact * u * _silu_grad(g, sg))
            dub = _bf(dact * sl)
            dgu_ref[:, hc * n:hc * (n + 1)] = dgb
            dgu_ref[:, hid + hc * n:hid + hc * (n + 1)] = dub
            dhn = dhn + _dot(dgb, wg) + _dot(dub, wu)
        dx, dg = _rms_bwd(dhn, xh, r, g_ref[...])
        dh2_ref[...] = dh3_ref[...] + dx
        _accumulate(dg_ref, dg)

    return pl.pallas_call(
        body, name="bwd_ffn", grid=(s // ts,),
        in_specs=[_row(ts, d), _row(ts, d), _row(ts, 2 * hid), _const(g_ffn.shape), _const(wgut.shape),
                  _const(wdown.shape)],
        out_specs=(_row(ts, d), _row(ts, hid), _row(ts, 2 * hid), _acc((1, d))),
        out_shape=(SDS((s, d), F32), SDS((s, hid), BF16), SDS((s, 2 * hid), BF16), SDS((1, d), F32)),
        compiler_params=_cparams("arbitrary"))(h2, dh3, gu, g_ffn, wgut, wdown)


def _bwd_xa(h1, dh2, qb, ob, g_xa, wq, wo, kv, ts):
    s, d = h1.shape
    scale = XA_DH ** -0.5

    def body(h_ref, dh2_ref, q_ref, o_ref, g_ref, wq_ref, wo_ref, kv_ref, dh1_ref, dwq_ref, dwo_ref, dkv_ref, dg_ref,
             dq_scr, accq, acco):
        hn, xh, r = _rms_fwd(h_ref[...], g_ref[...])
        hb = _bf(hn)
        dh2b = _bf(dh2_ref[...])
        do = _dot(dh2b, wo_ref[...], NT)

        @pl.when(pl.program_id(0) == 0)
        def _():
            dkv_ref[...] = jnp.zeros_like(dkv_ref)
            accq[...] = jnp.zeros_like(accq)
            acco[...] = jnp.zeros_like(acco)

        for h in range(XA_HEADS):
            lo, hi = XA_DH * h, XA_DH * (h + 1)
            qh = q_ref[:, lo:hi]
            kh = _bf(kv_ref[:, lo:hi])
            vh = _bf(kv_ref[:, d + lo:d + hi])
            p = _softmax_rows(_dot(qh, kh, NT) * scale)
            pb = _bf(p)
            doh = _bf(do[:, lo:hi])
            dp = _dot(doh, vh, NT)
            ds = p * (dp - jnp.sum(p * dp, axis=-1, keepdims=True)) * scale
            dsb = _bf(ds)
            dq_scr[:, lo:hi] = _dot(dsb, kh)
            dkv_ref[:, lo:hi] += _dot(dsb, qh, TN)
            dkv_ref[:, d + lo:d + hi] += _dot(pb, doh, TN)
        dqb = _bf(dq_scr[...])
        accq[...] += _dot(hb, dqb, TN)
        acco[...] += _dot(o_ref[...], dh2b, TN)
        dx, dg = _rms_bwd(_dot(dqb, wq_ref[...], NT), xh, r, g_ref[...])
        dh1_ref[...] = dh2_ref[...] + dx
        _accumulate(dg_ref, dg)

        @pl.when(pl.program_id(0) == pl.num_programs(0) - 1)
        def _():
            dwq_ref[...] = _bf(accq[...])
            dwo_ref[...] = _bf(acco[...])

    return pl.pallas_call(
        body, name="bwd_xa", grid=(s // ts,),
        in_specs=[_row(ts, d), _row(ts, d), _row(ts, d), _row(ts, d), _const(g_xa.shape), _const(wq.shape),
                  _const(wo.shape), _const(kv.shape)],
        out_specs=(_row(ts, d), _acc((d, d)), _acc((d, d)), _acc(kv.shape), _acc((1, d))),
        out_shape=(SDS((s, d), F32), SDS((d, d), BF16), SDS((d, d), BF16), SDS(kv.shape, F32), SDS((1, d), F32)),
        scratch_shapes=[pltpu.VMEM((ts, d), F32), pltpu.VMEM((d, d), F32), pltpu.VMEM((d, d), F32)],
        compiler_params=_cparams("arbitrary"))(h1, dh2, qb, ob, g_xa, wq, wo, kv)


def _bwd_kv(dkv, mn, mem, g_mem, wkv3):
    d = mem.shape[1]

    def body(dkv_ref, mn_ref, mem_ref, g_ref, w_ref, dw_ref, dg_ref):
        dkvb = _bf(dkv_ref[...])
        dmn = jnp.zeros(mem_ref.shape, F32)
        for b in range(N_DEV):
            blk = dkvb[:, 256 * b:256 * (b + 1)]
            dmn = dmn + _dot(blk, w_ref[b], NT)
            dw_ref[b] = _bf(_dot(mn_ref[...], blk, TN))
        _, xh, r = _rms_fwd(mem_ref[...], g_ref[...])
        _, dg = _rms_bwd(dmn, xh, r, g_ref[...])
        dg_ref[...] = dg

    return pl.pallas_call(body, name="bwd_kv", out_shape=(SDS(wkv3.shape, BF16), SDS((1, d), F32)),
                          compiler_params=_cparams())(dkv, mn, mem, g_mem, wkv3)


def _bwd_out(dh1, c, z, cln_g, cln_b, gln_g, gln_b, ws, wst, bst, w_out, ts):
    s, d = dh1.shape
    nh = GM_HEADS

    def body(dh1_ref, c_ref, zuv_ref, clg, clb, glg, glb, ws_ref, wst_ref, bst_ref, wo_ref,
             dwo_ref, dc_ref, dzuv_ref, dws_ref, dbst_ref, dclg_ref, dclb_ref, dglg_ref, dglb_ref, dbin_ref,
             mixed_scr, dv_scr, acc):
        cl, chat, crs = _ln_fwd(c_ref[...], clg[...], clb[...])
        sg = jax.nn.sigmoid(cl)
        zu = zuv_ref[:, 0:512]
        zv = zuv_ref[:, 512:1024]
        u, tu = _gelu(zu)
        vg, tv = _gelu(zv)
        vln, vhat, vrs = _ln_fwd(vg, glg[...], glb[...])
        vb = _bf(vln)
        _mix_fwd(vb, ws_ref, bst_ref, mixed_scr, ts)
        mixed = mixed_scr[...]
        dh1b = _bf(dh1_ref[...])
        dcat = _dot(dh1b, wo_ref[...], NT)
        dgm = dcat[:, 512:1024]
        dc, dclg, dclb = _ln_bwd(dcat[:, 0:512] * _silu_grad(cl, sg), chat, crs, clg[...])
        dc_ref[...] = dc
        dzu = dgm * mixed * _gelu_grad(zu, tu)
        dm = dgm * u

        @pl.when(pl.program_id(0) == 0)
        def _():
            dws_ref[...] = jnp.zeros_like(dws_ref)
            dbst_ref[...] = jnp.zeros_like(dbst_ref)
            acc[...] = jnp.zeros_like(acc)

        acc[0:512, :] += _dot(_bf(cl * sg), dh1b, TN)
        acc[512:1024, :] += _dot(_bf(u * mixed), dh1b, TN)

        @pl.when(pl.program_id(0) == pl.num_programs(0) - 1)
        def _():
            dwo_ref[...] = _bf(acc[...])

        tril, triu, lo = _mix_masks()
        head = lax.broadcasted_iota(jnp.int32, (1, nh), 1)
        for j in range(nh // 2):
            w0t = _bf(jnp.where(triu, wst_ref[2 * j], 0.0))
            w1t = _bf(jnp.where(triu, wst_ref[2 * j + 1], 0.0))
            for n in range(ts // CHUNK):
                rows = slice(n * CHUNK, (n + 1) * CHUNK)
                lanes = slice(j * 128, (j + 1) * 128)
                dmc = dm[rows, lanes]
                dmb = _bf(dmc)
                dv_scr[rows, lanes] = jnp.where(lo, _dot(w0t, dmb), _dot(w1t, dmb))
                vc = vb[rows, lanes]
                d0 = jnp.where(lo, dmc, 0.0)
                d1 = dmc - d0
                dws_ref[2 * j] += jnp.where(tril, _dot(_bf(d0), vc, NT), 0.0)
                dws_ref[2 * j + 1] += jnp.where(tril, _dot(_bf(d1), vc, NT), 0.0)
                dbst_ref[...] += (jnp.sum(d0, axis=1, keepdims=True) * (head == 2 * j).astype(F32)
                                  + jnp.sum(d1, axis=1, keepdims=True) * (head == 2 * j + 1).astype(F32))
        dvg, dglg, dglb = _ln_bwd(dv_scr[...], vhat, vrs, glg[...])
        dzv = dvg * _gelu_grad(zv, tv)
        dzuv_ref[:, 0:512] = _bf(dzu)
        dzuv_ref[:, 512:1024] = _bf(dzv)
        _accumulate(dclg_ref, dclg)
        _accumulate(dclb_ref, dclb)
        _accumulate(dglg_ref, dglg)
        _accumulate(dglb_ref, dglb)
        _accumulate(dbin_ref, jnp.concatenate([jnp.sum(dzu, axis=0, keepdims=True),
                                               jnp.sum(dzv, axis=0, keepdims=True)], axis=1))

    vec = (1, 512)
    return pl.pallas_call(
        body, name="bwd_out", grid=(s // ts,),
        in_specs=[_row(ts, d), _row(ts, 512), _row(ts, 1024, 1), _const(cln_g.shape), _const(cln_b.shape),
                  _const(gln_g.shape), _const(gln_b.shape), _const(ws.shape), _const(wst.shape), _const(bst.shape),
                  _const(w_out.shape)],
        out_specs=(_acc((d, d)), _row(ts, 512), _row(ts, 1024), _acc(ws.shape), _acc(bst.shape), _acc(vec), _acc(vec),
                   _acc(vec), _acc(vec), _acc((1, 1024))),
        out_shape=(SDS((d, d), BF16), SDS((s, 512), F32), SDS((s, 1024), BF16), SDS(ws.shape, F32),
                   SDS(bst.shape, F32), SDS(vec, F32), SDS(vec, F32), SDS(vec, F32), SDS(vec, F32), SDS((1, 1024), F32)),
        scratch_shapes=[pltpu.VMEM((ts, 512), F32), pltpu.VMEM((ts, 512), F32), pltpu.VMEM((d, d), F32)],
        compiler_params=_cparams("arbitrary"))(dh1, c, z, cln_g, cln_b, gln_g, gln_b, ws, wst, bst, w_out)


CONV_RC = 256
FWD_TAPS = range(CONV_PAD - CONV_K + 1, CONV_PAD + 1)


def _bwd_in(x, dh1, dc, a, z, dzuv, g_mix, w_in3, conv_w, ts):
    s, d = x.shape
    cw = conv_w.shape[1]
    nb = cw // 128
    n = s // ts
    rc = min(CONV_RC, ts)
    per_halo = ts // CONV_PAD
    w4 = conv_w.reshape(CONV_K, nb, 128).transpose(1, 0, 2)

    def body(x_ref, dh1_ref, dc_ref, a_ref, ahalo_ref, zag_ref, dzuv_ref, g_ref, w_ref, cw_ref,
             dx_ref, dw_ref, dcw_ref, dcb_ref, dbin_ref, dg_ref, dz_ref, acc, pad_d, pad_a, part, da_scr):
        i = pl.program_id(0)

        @pl.when(i == 0)
        def _():
            acc[...] = jnp.zeros_like(acc)
            part[...] = jnp.zeros_like(part)
            pad_d[:, ts:, :] = jnp.zeros((nb, CONV_PAD, 128), F32)

        before = jnp.where(i == n - 1, 0.0, ahalo_ref[...])
        for blk in range(nb):
            lanes = slice(128 * blk, 128 * (blk + 1))
            pad_d[blk, 0:ts, :] = dc_ref[:, lanes]
            pad_a[blk, 0:CONV_PAD, :] = before[:, lanes]
            pad_a[blk, CONV_PAD:, :] = a_ref[:, lanes]

        def rows8(v):
            return jnp.sum(v.reshape(rc // 8, 8, 128), axis=0)

        def unit(blk, r0):
            acc_da = jnp.zeros((rc, 128), F32)
            for res, offs in _by_residue(range(0, CONV_K)):
                shifted = pad_d[blk, pl.ds(r0 + res, rc + offs[-1] - res), :]
                for off in offs:
                    k = CONV_K - 1 - off
                    acc_da = acc_da + cw_ref[blk, k:k + 1, :] * shifted[off - res:off - res + rc, :]
            da_scr[blk, pl.ds(r0, rc), :] = acc_da
            dcc = pad_d[blk, pl.ds(r0, rc), :]
            for res, offs in _by_residue(FWD_TAPS):
                shifted = pad_a[blk, pl.ds(r0 + res, rc + offs[-1] - res), :]
                for off in offs:
                    k = off - FWD_TAPS[0]
                    part[blk, 8 * k:8 * k + 8, :] += rows8(dcc * shifted[off - res:off - res + rc, :])
            part[blk, 8 * CONV_PAD:, :] += rows8(dcc)

        for blk in range(nb):
            for r0 in range(0, ts, rc):
                unit(blk, r0)
        pad_d[:, ts:, :] = pad_d[:, 0:CONV_PAD, :]

        za = zag_ref[:, 0:512]
        sg = jax.nn.sigmoid(zag_ref[:, 512:1024])
        da_ = jnp.concatenate([da_scr[blk] for blk in range(nb)], axis=1)
        dza = da_ * sg
        dzg = da_ * za * sg * (1.0 - sg)
        dz_ref[:, 0:512] = _bf(dza)
        dz_ref[:, 512:1024] = _bf(dzg)
        dz_ref[:, 1024:2048] = dzuv_ref[...]
        dhn = jnp.zeros((ts, d), F32)
        for b in range(N_DEV):
            dhn = dhn + _dot(dz_ref[:, 256 * b:256 * (b + 1)], w_ref[b], NT)
        hn, xh, r = _rms_fwd(x_ref[...], g_ref[...])
        acc[...] += _dot(_bf(hn), dz_ref[...], TN)

        @pl.when(i == n - 1)
        def _():
            for b in range(N_DEV):
                dw_ref[b] = _bf(acc[:, 256 * b:256 * (b + 1)])
            for blk in range(nb):
                sums = jnp.sum(part[blk].reshape(CONV_PAD + 1, 8, 128), axis=1)
                dcw_ref[:, 128 * blk:128 * (blk + 1)] = sums[0:CONV_PAD, :]
                dcb_ref[:, 128 * blk:128 * (blk + 1)] = sums[CONV_PAD:, :]

        dxn, dg = _rms_bwd(dhn, xh, r, g_ref[...])
        dx_ref[...] = dh1_ref[...] + dxn
        _accumulate(dg_ref, dg)
        _accumulate(dbin_ref, jnp.concatenate([jnp.sum(dza, axis=0, keepdims=True),
                                               jnp.sum(dzg, axis=0, keepdims=True)], axis=1))

    back = lambda width, col=0: pl.BlockSpec((ts, width), lambda i: (n - 1 - i, col))
    halo = pl.BlockSpec((CONV_PAD, cw), lambda i: (jnp.maximum((n - 1 - i) * per_halo - 1, 0), 0))
    return pl.pallas_call(
        body, name="bwd_in", grid=(n,),
        in_specs=[back(d), back(d), back(cw), back(cw), halo, back(1024, 0), back(1024), _const(g_mix.shape),
                  _const(w_in3.shape), _const(w4.shape)],
        out_specs=(back(d), _acc(w_in3.shape), _acc((CONV_PAD, cw)), _acc((1, cw)), _acc((1, 1024)), _acc((1, d))),
        out_shape=(SDS((s, d), F32), SDS(w_in3.shape, BF16), SDS((CONV_PAD, cw), F32), SDS((1, cw), F32),
                   SDS((1, 1024), F32), SDS((1, d), F32)),
        scratch_shapes=[pltpu.VMEM((ts, 2048), BF16), pltpu.VMEM((d, 2048), F32),
                        pltpu.VMEM((nb, ts + CONV_PAD, 128), F32), pltpu.VMEM((nb, ts + CONV_PAD, 128), F32),
                        pltpu.VMEM((nb, 8 * (CONV_PAD + 1), 128), F32), pltpu.VMEM((nb, ts, 128), F32)],
        compiler_params=_cparams("arbitrary"))(x, dh1, dc, a, a, z, dzuv, g_mix, w_in3, w4)


def _tn_matmul(a, b, tm, name):
    s, m = a.shape
    n = b.shape[1]
    ts = _row_tile(s, (1024, 512, 256, 128))
    n_s = s // ts

    def body(a_ref, b_ref, o_ref, acc):
        k = pl.program_id(1)

        @pl.when(k == 0)
        def _():
            acc[...] = jnp.zeros_like(acc)

        acc[...] += _dot(a_ref[...], b_ref[...], TN)

        @pl.when(k == n_s - 1)
        def _():
            o_ref[...] = _bf(acc[...])

    return pl.pallas_call(
        body, name=name, grid=(m // tm, n_s),
        in_specs=[pl.BlockSpec((ts, tm), lambda i, k: (k, i)), pl.BlockSpec((ts, n), lambda i, k: (k, 0))],
        out_specs=pl.BlockSpec((tm, n), lambda i, k: (i, 0)), out_shape=SDS((m, n), BF16),
        scratch_shapes=[pltpu.VMEM((tm, n), F32)],
        compiler_params=_cparams("parallel", "arbitrary"))(a, b)


def _row_tile(r, cands):
    for cand in cands:
        if r % cand == 0:
            return cand
    return r


def _sum_in_device_order(me_ref, land_ref, own_ref):
    acc = None
    for dev in range(N_DEV):
        part = jnp.where(me_ref[0] == dev, own_ref[0], land_ref[dev]).astype(F32)
        acc = part if acc is None else acc + part
    return acc


def _sum8(land, own, me, name):
    _, r, c = land.shape
    tr = _row_tile(r, (512, 256, 352, 128, 8))
    own3 = own if own.ndim == 3 else own[None]
    own_map = (lambda i, me_ref: (me_ref[0], i, 0)) if own.ndim == 3 else (lambda i, me_ref: (0, i, 0))

    def body(me_ref, land_ref, own_ref, o_ref):
        o_ref[...] = _sum_in_device_order(me_ref, land_ref, own_ref)

    return pl.pallas_call(
        body, name=name, out_shape=SDS((r, c), F32),
        grid_spec=pltpu.PrefetchScalarGridSpec(
            num_scalar_prefetch=1, grid=(r // tr,),
            in_specs=[pl.BlockSpec((N_DEV, tr, c), lambda i, me_ref: (0, i, 0)), pl.BlockSpec((1, tr, c), own_map)],
            out_specs=pl.BlockSpec((tr, c), lambda i, me_ref: (i, 0))),
        compiler_params=_cparams("parallel"))(me, land, own3)


def _adam_update(w, g, m, v):
    m2 = ADAM_B1 * m + (1.0 - ADAM_B1) * g
    v2 = ADAM_B2 * v + (1.0 - ADAM_B2) * (g * g)
    m_hat = m2 / (1.0 - ADAM_B1 ** ADAM_STEP)
    v_hat = v2 / (1.0 - ADAM_B2 ** ADAM_STEP)
    return -ADAM_LR * (m_hat / (jnp.sqrt(v_hat) + ADAM_EPS) + ADAM_WD * w), m2, v2


def _slot_spec(tr, c, mask):
    return pl.BlockSpec((1, tr, c), lambda i, me_ref: (me_ref[0] ^ mask, i, 0))


def _sum_adamw(land, parts, w, m, v, me, name, masks=ALL_PEERS):
    r, c = w.shape
    tr = _row_tile(r, (256, 128, 176, 8))
    n = len(masks)

    def body(me_ref, own_ref, *refs):
        w_ref, m_ref, v_ref, g_ref, d_ref, m2_ref, v2_ref = refs[n:]
        g = own_ref[0].astype(F32)
        for peer_ref in refs[:n]:
            g = g + peer_ref[0].astype(F32)
        g_ref[...] = g
        d_ref[...], m2_ref[...], v2_ref[...] = _adam_update(w_ref[...], g, m_ref[...], v_ref[...])

    blk = pl.BlockSpec((tr, c), lambda i, me_ref: (i, 0))
    return pl.pallas_call(
        body, name=name, out_shape=(SDS((r, c), F32),) * 4,
        grid_spec=pltpu.PrefetchScalarGridSpec(
            num_scalar_prefetch=1, grid=(r // tr,),
            in_specs=[_slot_spec(tr, c, 0)] + [_slot_spec(tr, c, mask) for mask in masks] + [blk, blk, blk],
            out_specs=(blk,) * 4),
        compiler_params=_cparams("parallel"))(me, parts, *([land] * n), w, m, v)


def _chip_sum(parts, from_sibling, me, name):
    _, r, c = parts.shape
    tr = _row_tile(r, (512, 256, 352, 128, 8))
    spec = pl.BlockSpec((1, tr, c), lambda j, i, me_ref: (me_ref[0] ^ (2 * j), i, 0))

    def body(me_ref, p_ref, q_ref, o_ref):
        o_ref[...] = _bf(p_ref[...].astype(F32) + q_ref[...].astype(F32))

    return pl.pallas_call(
        body, name=name, out_shape=SDS(parts.shape, BF16),
        grid_spec=pltpu.PrefetchScalarGridSpec(num_scalar_prefetch=1, grid=(len(CHIPS), r // tr), in_specs=[spec, spec],
                                               out_specs=spec),
        compiler_params=_cparams("parallel", "parallel"))(me, parts, from_sibling)


def _adamw_many(ws, gs, ms, vs, name):
    n = len(ws)

    def body(*refs):
        ins, outs = refs[:4 * n], refs[4 * n:]
        for i in range(n):
            w, g, m, v = (ins[j * n + i][...] for j in range(4))
            outs[i][...], outs[n + i][...], outs[2 * n + i][...] = _adam_update(w, g, m, v)

    shapes = [SDS(w.shape, F32) for w in ws]
    out = pl.pallas_call(body, name=name, out_shape=shapes * 3, compiler_params=_cparams())(*ws, *gs, *ms, *vs)
    return out[:n], out[n:2 * n], out[2 * n:]


def _pack(arrs):
    flat = jnp.concatenate([a.reshape(-1) for a in arrs])
    pad = (-flat.shape[0]) % (128 * 128)
    return jnp.pad(flat, (0, pad)).reshape(-1, 128)


def _unpack(packed, shapes):
    flat = packed.reshape(-1)
    out, off = [], 0
    for shp in shapes:
        size = 1
        for dim in shp:
            size *= dim
        out.append(flat[off:off + size].reshape(shp))
        off += size
    return out


SMALL = ("norm_mix_g", "b_in", "conv_w", "conv_b", "conv_ln_g", "conv_ln_b", "gm_ln_g", "gm_ln_b", "gm_w_s", "gm_b_s",
         "norm_xa_g", "mem_norm_g", "norm_ffn_g", "final_norm_g")
BIG = ("w_in", "w_out", "xa_wq", "xa_wkv", "xa_wo", "ffn_w_gate_up", "ffn_w_down")
WEIGHTS = ("norm_mix_g", "w_in", "b_in", "conv_w", "conv_b", "conv_ln_g", "conv_ln_b", "gm_ln_g", "gm_ln_b", "gm_w_s",
           "gm_b_s", "w_out", "norm_xa_g", "mem_norm_g", "xa_wq", "xa_wkv", "xa_wo", "norm_ffn_g", "ffn_w_gate_up",
           "ffn_w_down", "final_norm_g")


def kernel(x, mem, norm_mix_g, w_in, b_in, conv_w, conv_b, conv_ln_g, conv_ln_b, gm_ln_g, gm_ln_b, gm_w_s, gm_b_s, w_out, norm_xa_g, mem_norm_g, xa_wq, xa_wkv, xa_wo, norm_ffn_g, ffn_w_gate_up, ffn_w_down, final_norm_g, loss_target, m_norm_mix_g, m_w_in, m_b_in, m_conv_w, m_conv_b, m_conv_ln_g, m_conv_ln_b, m_gm_ln_g, m_gm_ln_b, m_gm_w_s, m_gm_b_s, m_w_out, m_norm_xa_g, m_mem_norm_g, m_xa_wq, m_xa_wkv, m_xa_wo, m_norm_ffn_g, m_ffn_w_gate_up, m_ffn_w_down, m_final_norm_g, v_norm_mix_g, v_w_in, v_b_in, v_conv_w, v_conv_b, v_conv_ln_g, v_conv_ln_b, v_gm_ln_g, v_gm_ln_b, v_gm_w_s, v_gm_b_s, v_w_out, v_norm_xa_g, v_mem_norm_g, v_xa_wq, v_xa_wkv, v_xa_wo, v_norm_ffn_g, v_ffn_w_gate_up, v_ffn_w_down, v_final_norm_g):
    w = dict(norm_mix_g=norm_mix_g, w_in=w_in, b_in=b_in, conv_w=conv_w, conv_b=conv_b, conv_ln_g=conv_ln_g,
             conv_ln_b=conv_ln_b, gm_ln_g=gm_ln_g, gm_ln_b=gm_ln_b, gm_w_s=gm_w_s, gm_b_s=gm_b_s, w_out=w_out,
             norm_xa_g=norm_xa_g, mem_norm_g=mem_norm_g, xa_wq=xa_wq, xa_wkv=xa_wkv, xa_wo=xa_wo,
             norm_ffn_g=norm_ffn_g, ffn_w_gate_up=ffn_w_gate_up, ffn_w_down=ffn_w_down, final_norm_g=final_norm_g)
    mom = dict(norm_mix_g=m_norm_mix_g, w_in=m_w_in, b_in=m_b_in, conv_w=m_conv_w, conv_b=m_conv_b,
               conv_ln_g=m_conv_ln_g, conv_ln_b=m_conv_ln_b, gm_ln_g=m_gm_ln_g, gm_ln_b=m_gm_ln_b, gm_w_s=m_gm_w_s,
               gm_b_s=m_gm_b_s, w_out=m_w_out, norm_xa_g=m_norm_xa_g, mem_norm_g=m_mem_norm_g, xa_wq=m_xa_wq,
               xa_wkv=m_xa_wkv, xa_wo=m_xa_wo, norm_ffn_g=m_norm_ffn_g, ffn_w_gate_up=m_ffn_w_gate_up,
               ffn_w_down=m_ffn_w_down, final_norm_g=m_final_norm_g)
    var = dict(norm_mix_g=v_norm_mix_g, w_in=v_w_in, b_in=v_b_in, conv_w=v_conv_w, conv_b=v_conv_b,
               conv_ln_g=v_conv_ln_g, conv_ln_b=v_conv_ln_b, gm_ln_g=v_gm_ln_g, gm_ln_b=v_gm_ln_b, gm_w_s=v_gm_w_s,
               gm_b_s=v_gm_b_s, w_out=v_w_out, norm_xa_g=v_norm_xa_g, mem_norm_g=v_mem_norm_g, xa_wq=v_xa_wq,
               xa_wkv=v_xa_wkv, xa_wo=v_xa_wo, norm_ffn_g=v_norm_ffn_g, ffn_w_gate_up=v_ffn_w_gate_up,
               ffn_w_down=v_ffn_w_down, final_norm_g=v_final_norm_g)

    me = 4 * lax.axis_index("x") + 2 * lax.axis_index("y") + lax.axis_index("c")
    s, d = x.shape[1], x.shape[2]
    xs = x.reshape(s, d)
    mems = mem.reshape(mem.shape[1], d)
    tgt = loss_target.reshape(s, d)
    ts = min(512, s)
    ts_ffn = min(256, s)
    row = lambda a: a.reshape(1, -1)

    conv_w_pad = jnp.pad(conv_w, ((0, CONV_PAD - CONV_K), (0, 128 - conv_w.shape[1])))
    first_level = (SIBLING,) + OTHER_CHIPS
    ex_first = _exchange_start([_bf(w_in), conv_w_pad], [False] * 2, "gather_start_in", first_level, own_slot=me)
    behind = lambda t: _bf(t + ex_first[4][0:1, 0:1])
    shards = [behind(xa_wkv), behind(w_out), behind(xa_wq), behind(xa_wo), behind(ffn_w_gate_up.T), behind(ffn_w_down)]
    ex_rest = _exchange_start(shards, [False] * len(shards), "gather_start_rest", first_level, own_slot=me)
    g_send, g_recv, g_src, g_land = (list(ex_first[i]) + list(ex_rest[i]) for i in range(4))
    g_tok = ex_rest[4]

    def arrived(idx, after, name):
        pick = lambda seq: [seq[i] for i in idx]
        return _wait_and_forward(pick(g_send), pick(g_recv), pick(g_src), pick(g_land), after, "gather_pass_" + name,
                                 first_level)

    def complete(handle, after, name):
        return _forward_wait(handle[0], handle[1], handle[2], after, "forward_wait_" + name)

    bst = gm_b_s.T
    wst = jnp.swapaxes(gm_w_s, 1, 2)

    h_in = arrived((0, 1), g_tok, "in")
    w_in3, conv_w8 = complete(h_in, h_in[3], "in")
    conv_w_f = conv_w8[:, :CONV_K, :conv_w.shape[1]].transpose(1, 0, 2).reshape(CONV_K, -1)
    cw = conv_w_f.shape[1]
    z, a = _fwd_in(xs, row(norm_mix_g), w_in3, row(b_in), ts)
    h_out = arrived((2, 3), z, "out")
    c = _conv_fwd(a, conv_w_f + h_out[3][0:1, 0:1], row(conv_b))
    h_xa = arrived((4, 5), c, "xa")
    wkv3, w_out3 = complete(h_out, h_xa[3], "out")
    w_out_f = w_out3.reshape(-1, d)
    kv, mn = _kv_proj(mems, row(mem_norm_g), wkv3)
    h1 = _fwd_out(xs, c, z, row(conv_ln_g), row(conv_ln_b), row(gm_ln_g), row(gm_ln_b), gm_w_s, bst, w_out_f, ts)
    h_gut = arrived((6,), h1, "gate_up")
    wq3, wo3 = complete(h_xa, h_gut[3], "xa")
    wq_f = wq3.reshape(-1, d)
    wo_f = wo3.reshape(-1, d)
    h2, qb, ob = _fwd_xa(h1, row(norm_xa_g), wq_f, kv, wo_f, ts)
    h_down = arrived((7,), h2, "down")
    (wgut3,) = complete(h_gut, h_down[3], "gate_up")
    (wdown3,) = complete(h_down, wgut3, "down")
    wgut_f = wgut3.reshape(-1, d)
    wdown_f = wdown3.reshape(-1, d)
    dh3, dh3b, gu, hn2, loss_p, d_final_g = _fwd_ffn(h2, row(norm_ffn_g), wgut_f, wdown_f, row(final_norm_g), tgt,
                                                     ts_ffn)

    blocks = lambda m: m.reshape(N_DEV, -1, d)
    tok = lambda ex: ex[4][0:1, 0:1]
    me1 = me.astype(jnp.int32).reshape(1)

    def chip_sums(pair, after, tag):
        parts, handed = _pair_wait(pair[0], pair[1], pair[2], pair[3], after, "pair_wait_" + tag)
        return [_chip_sum(p, h, me1, "chip_sum_%s_%d" % (tag, i)) for i, (p, h) in enumerate(zip(parts, handed))]

    dh2, act, dgu, d_ffn_g = _bwd_ffn(h2, dh3, gu, row(norm_ffn_g), wgut_f, wdown_f, ts_ffn)
    dwgut = _tn_matmul(dgu, hn2, 1408, "dw_gate_up")
    dwdown = _tn_matmul(act, dh3b, 1408, "dw_down")
    pair_ffn = _pair_start([blocks(dwgut), blocks(dwdown)], "pair_start_ffn")
    dh1, dwq, dwo, dkv, d_xa_g = _bwd_xa(h1, dh2, qb, ob, row(norm_xa_g) + tok(pair_ffn), wq_f, wo_f, kv, ts)
    ex_ffn = _exchange_start(chip_sums(pair_ffn, dh1, "ffn"), [True] * 2, "scatter_start_ffn", OTHER_CHIPS)
    dwkv3, d_mem_g = _bwd_kv(dkv, mn, mems, row(mem_norm_g) + tok(ex_ffn), wkv3)
    pair_xa = _pair_start([blocks(dwq), blocks(dwo), dwkv3], "pair_start_xa")
    (dwout, dc, dzuv, dws, dbst, d_cln_g, d_cln_b, d_gln_g, d_gln_b, dbin_uv) = _bwd_out(
        dh1, c, z, row(conv_ln_g) + tok(pair_xa), row(conv_ln_b), row(gm_ln_g), row(gm_ln_b), gm_w_s, wst, bst,
        w_out_f, ts)
    early = dict(b_in_uv=dbin_uv, conv_ln_g=d_cln_g, conv_ln_b=d_cln_b, gm_ln_g=d_gln_g, gm_ln_b=d_gln_b,
                 gm_b_s=dbst.T, norm_xa_g=d_xa_g, mem_norm_g=d_mem_g, norm_ffn_g=d_ffn_g, final_norm_g=d_final_g,
                 loss=loss_p)
    dws_b = _bf(dws).reshape(GM_HEADS * CHUNK, CHUNK)
    ex_xa = _exchange_start(chip_sums(pair_xa, dc, "xa") + [blocks(dwout), dws_b, _pack(list(early.values()))],
                            [True] * 4 + [False] * 2, "scatter_start_xa", [OTHER_CHIPS] * 3 + [ALL_PEERS] * 3)
    dx, dw_in3, dconv_w, dconv_b, dbin_ag, d_mix_g = _bwd_in(xs, dh1, dc, a, z, dzuv, row(norm_mix_g) + tok(ex_xa),
                                                             w_in3, conv_w_f, ts)
    late = dict(norm_mix_g=d_mix_g, b_in_ag=dbin_ag, conv_w=dconv_w[:CONV_K], conv_b=dconv_b)
    ex_last = _exchange_start([_pack(list(late.values())), dw_in3], [False, True], "scatter_start_in")

    grads, delta, new_m, new_v = {}, {}, {}, {}

    def waited(ex, idx, scatter, after, name, masks=ALL_PEERS):
        return _exchange_wait(*[[ex[i][k] for k in idx] for i in range(4)], scatter, after, name, masks)

    def reduced(names, srcs, lands, masks):
        for nm, src, land, mk in zip(names, srcs, lands, masks):
            view = (lambda t: t.T) if nm == "ffn_w_gate_up" else (lambda t: t)
            upd = _sum_adamw(land, src, view(w[nm]), view(mom[nm]), view(var[nm]), me1, "adamw_" + nm, mk)
            grads[nm], delta[nm], new_m[nm], new_v[nm] = (view(t) for t in upd)
        return new_v[names[-1]]

    after = reduced(("ffn_w_gate_up", "ffn_w_down"),
                    *waited(ex_ffn, (0, 1), [True] * 2, ex_last[4], "scatter_wait_ffn", OTHER_CHIPS), [OTHER_CHIPS] * 2)
    xa_masks = [OTHER_CHIPS] * 3 + [ALL_PEERS]
    after = reduced(("xa_wq", "xa_wo", "xa_wkv", "w_out"),
                    *waited(ex_xa, (0, 1, 2, 3), [True] * 4, after, "scatter_wait_xa", xa_masks), xa_masks)

    (dws_src, early_src), (dws_land, early_land) = waited(ex_xa, (4, 5), [False] * 2, after, "gather_wait_early")
    (late_src,), (late_land,) = waited(ex_last, (0,), [False], early_land, "gather_wait_late")
    dws_tot = _sum8(dws_land, dws_src, me1, "sum_gm_w_s")
    early_tot = _sum8(early_land, early_src, me1, "sum_small_early")
    late_tot = _sum8(late_land, late_src, me1, "sum_small_late")
    early_g = dict(zip(early, _unpack(early_tot, [v.shape for v in early.values()])), gm_w_s=dws_tot)
    late_g = dict(zip(late, _unpack(late_tot, [v.shape for v in late.values()])))
    loss = early_g["loss"][0, 0]
    for nm in SMALL:
        if nm == "b_in":
            g = jnp.concatenate([late_g["b_in_ag"], early_g["b_in_uv"]], axis=1)
        elif nm == "conv_w":
            g = lax.dynamic_slice_in_dim(late_g[nm], me * conv_w.shape[1], conv_w.shape[1], axis=1)
        else:
            g = late_g[nm] if nm in late_g else early_g[nm]
        grads[nm] = g.reshape(w[nm].shape)

    small = [[src[nm] for nm in SMALL] for src in (w, grads, mom, var)]
    for dst, vals in zip((delta, new_m, new_v), _adamw_many(*small, "adamw_small")):
        dst.update(zip(SMALL, vals))

    reduced(("w_in",), *waited(ex_last, (1,), [True], delta[SMALL[0]], "scatter_wait_in"), [ALL_PEERS])

    return (loss, dx.reshape(x.shape), *[grads[nm] for nm in WEIGHTS], *[delta[nm] for nm in WEIGHTS],
            *[new_m[nm] for nm in WEIGHTS], *[new_v[nm] for nm in WEIGHTS])
```

```python
import functools

import jax
import jax.numpy as jnp
from jax import lax
from jax.experimental import pallas as pl
from jax.experimental.pallas import tpu as pltpu

F32 = jnp.float32
BF16 = jnp.bfloat16
SDS = jax.ShapeDtypeStruct

N_DEV = 8
RMS_EPS = 1e-6
LN_EPS = 1e-5
CONV_K = 31
CONV_PAD = 32
CHUNK = 128
GM_HEADS = 8
XA_HEADS = 4
XA_DH = 256
GELU_K0 = 0.7978845608028654
GELU_K1 = 0.044715
ADAM_LR = 0.001
ADAM_B1 = 0.9
ADAM_B2 = 0.999
ADAM_EPS = 1e-08
ADAM_WD = 0.01
ADAM_STEP = 10
VMEM_LIMIT = 60 * 1024 * 1024

NN = (((1,), (0,)), ((), ()))
NT = (((1,), (1,)), ((), ()))
TN = (((0,), (0,)), ((), ()))


def _dot(a, b, dims=NN):
    return lax.dot_general(a, b, dims, preferred_element_type=F32)


def _bf(x):
    return x.astype(BF16)


def _cparams(*sem):
    return pltpu.CompilerParams(dimension_semantics=tuple(sem) if sem else None, vmem_limit_bytes=VMEM_LIMIT)


def _row(ts, w, col=0):
    return pl.BlockSpec((ts, w), lambda i: (i, col))


def _const(shape):
    nd = len(shape)
    return pl.BlockSpec(shape, lambda i: (0,) * nd, pipeline_mode=pl.Buffered(1))


def _acc(shape):
    nd = len(shape)
    return pl.BlockSpec(shape, lambda i: (0,) * nd)


def _rms_fwd(x, g):
    r = lax.rsqrt(jnp.mean(x * x, axis=-1, keepdims=True) + RMS_EPS)
    xh = x * r
    return xh * g, xh, r


def _rms_bwd(dy, xh, r, g):
    gdy = dy * g
    dx = r * (gdy - xh * jnp.mean(gdy * xh, axis=-1, keepdims=True))
    dg = jnp.sum(dy * xh, axis=0, keepdims=True)
    return dx, dg


def _ln_fwd(x, g, b):
    mu = jnp.mean(x, axis=-1, keepdims=True)
    xc = x - mu
    rs = lax.rsqrt(jnp.mean(xc * xc, axis=-1, keepdims=True) + LN_EPS)
    xh = xc * rs
    return xh * g + b, xh, rs


def _ln_bwd(dy, xh, rs, g):
    dxh = dy * g
    dx = rs * (dxh - jnp.mean(dxh, axis=-1, keepdims=True) - xh * jnp.mean(dxh * xh, axis=-1, keepdims=True))
    return dx, jnp.sum(dy * xh, axis=0, keepdims=True), jnp.sum(dy, axis=0, keepdims=True)


def _gelu(x):
    t = jnp.tanh(GELU_K0 * (x + GELU_K1 * (x * x * x)))
    return 0.5 * x * (1.0 + t), t


def _gelu_grad(x, t):
    return 0.5 * (1.0 + t) + 0.5 * x * (1.0 - t * t) * (GELU_K0 * (1.0 + 3.0 * GELU_K1 * x * x))


def _silu_grad(x, sg):
    return sg * (1.0 + x * (1.0 - sg))


def _by_residue(offsets):
    groups = {}
    for off in offsets:
        groups.setdefault(off % 8, []).append(off)
    return [(res, sorted(offs)) for res, offs in sorted(groups.items())]


def _accumulate(ref, val):
    @pl.when(pl.program_id(0) == 0)
    def _():
        ref[...] = jnp.zeros_like(ref)
    ref[...] += val


def _mix_masks():
    row = lax.broadcasted_iota(jnp.int32, (CHUNK, CHUNK), 0)
    col = lax.broadcasted_iota(jnp.int32, (CHUNK, CHUNK), 1)
    return row >= col, row <= col, col < (CHUNK // 2)


def _mix_fwd(vb, ws_ref, bst_ref, mixed_scr, ts):
    tril, _, lo = _mix_masks()
    for j in range(GM_HEADS // 2):
        w0 = _bf(jnp.where(tril, ws_ref[2 * j], 0.0))
        w1 = _bf(jnp.where(tril, ws_ref[2 * j + 1], 0.0))
        bias = jnp.where(lo, bst_ref[:, 2 * j:2 * j + 1], bst_ref[:, 2 * j + 1:2 * j + 2])
        for n in range(ts // CHUNK):
            v = vb[n * CHUNK:(n + 1) * CHUNK, j * 128:(j + 1) * 128]
            mixed_scr[n * CHUNK:(n + 1) * CHUNK, j * 128:(j + 1) * 128] = jnp.where(lo, _dot(w0, v), _dot(w1, v)) + bias


def _exchange(srcs, scatter, name):
    n = len(srcs)

    def body(*refs):
        src_refs, out_refs = refs[:n], refs[n:2 * n]
        send_sems, recv_sems, local_sems = refs[2 * n:]
        x, y, c = lax.axis_index("x"), lax.axis_index("y"), lax.axis_index("c")
        me = 4 * x + 2 * y + c

        def peer_of(mask):
            px = x if not (mask >> 2) & 1 else 1 - x
            py = y if not (mask >> 1) & 1 else 1 - y
            pc = c if not mask & 1 else 1 - c
            return (px, py, pc), 4 * px + 2 * py + pc

        def remote(k, mask):
            peer, pidx = peer_of(mask)
            return pltpu.make_async_remote_copy(
                src_ref=src_refs[k].at[pidx] if scatter else src_refs[k],
                dst_ref=out_refs[k].at[me],
                send_sem=send_sems.at[k, mask - 1], recv_sem=recv_sems.at[k, mask - 1],
                device_id=peer, device_id_type=pl.DeviceIdType.MESH)

        def arrival(k, mask):
            peer, pidx = peer_of(mask)
            return pltpu.make_async_remote_copy(
                src_ref=src_refs[k].at[pidx] if scatter else src_refs[k],
                dst_ref=out_refs[k].at[pidx],
                send_sem=send_sems.at[k, mask - 1], recv_sem=recv_sems.at[k, mask - 1],
                device_id=peer, device_id_type=pl.DeviceIdType.MESH)

        sends, locals_ = [], []
        for k in range(n):
            for mask in range(1, N_DEV):
                cp = remote(k, mask)
                cp.start()
                sends.append(cp)
            lc = pltpu.make_async_copy(src_refs[k].at[me] if scatter else src_refs[k], out_refs[k].at[me],
                                       local_sems.at[k])
            lc.start()
            locals_.append(lc)
        for k in range(n):
            for mask in range(1, N_DEV):
                arrival(k, mask).wait_recv()
        for cp in sends:
            cp.wait_send()
        for lc in locals_:
            lc.wait()

    outs = [SDS((N_DEV,) + tuple(s.shape[1:] if scatter else s.shape), s.dtype) for s in srcs]
    hbm = pl.BlockSpec(memory_space=pl.ANY)
    return pl.pallas_call(
        body, name=name, out_shape=outs, in_specs=[hbm] * n, out_specs=[hbm] * n,
        scratch_shapes=[pltpu.SemaphoreType.DMA((n, N_DEV - 1)), pltpu.SemaphoreType.DMA((n, N_DEV - 1)),
                        pltpu.SemaphoreType.DMA((n,))],
    )(*srcs)


def _peer_of(mask):
    x, y, c = lax.axis_index("x"), lax.axis_index("y"), lax.axis_index("c")
    px = 1 - x if (mask >> 2) & 1 else x
    py = 1 - y if (mask >> 1) & 1 else y
    pc = 1 - c if mask & 1 else c
    return (px, py, pc), 4 * px + 2 * py + pc


ALL_PEERS = tuple(range(1, N_DEV))
OTHER_CHIPS = (2, 4, 6)
CHIPS = (0,) + OTHER_CHIPS
SIBLING = 1


def _split_copy(src_ref, land_ref, send_sem, recv_sem, mask, slot, scatter, outgoing):
    x, y, c = lax.axis_index("x"), lax.axis_index("y"), lax.axis_index("c")
    me = 4 * x + 2 * y + c
    peer, pidx = _peer_of(mask)
    return pltpu.make_async_remote_copy(
        src_ref=src_ref.at[pidx] if scatter else src_ref,
        dst_ref=land_ref.at[me if outgoing else pidx],
        send_sem=send_sem.at[slot], recv_sem=recv_sem.at[slot],
        device_id=peer, device_id_type=pl.DeviceIdType.MESH)


_HBM = pl.BlockSpec(memory_space=pltpu.HBM)
_SEM = pl.BlockSpec(memory_space=pltpu.SEMAPHORE)
_EFFECT = pltpu.SideEffectType.DATAFLOW_SIDE_EFFECTING


def _per_array(masks, n):
    return [tuple(masks)] * n if isinstance(masks[0], int) else [tuple(m) for m in masks]


def _exchange_start(srcs, scatter, name, masks=ALL_PEERS, place_own=False):
    n = len(srcs)
    masks = _per_array(masks, n)
    lands = [lax.empty((N_DEV,) + tuple(s.shape[1:] if sc else s.shape), s.dtype) for s, sc in zip(srcs, scatter)]
    lands = [pltpu.with_memory_space_constraint(land, pltpu.HBM) for land in lands]
    srcs = [pltpu.with_memory_space_constraint(s, pltpu.HBM) for s in srcs]

    def body(*refs):
        src_refs, land_refs = refs[:n], refs[n:2 * n]
        send_sems, recv_sems = refs[2 * n:3 * n], refs[3 * n:4 * n]
        token = refs[4 * n + 2 * n]
        for k in range(n):
            for slot, mask in enumerate(masks[k]):
                _split_copy(src_refs[k], land_refs[k], send_sems[k], recv_sems[k], mask, slot, scatter[k], True).start()
        if place_own:
            x, y, c = lax.axis_index("x"), lax.axis_index("y"), lax.axis_index("c")
            me = 4 * x + 2 * y + c
            own = [pltpu.make_async_copy(src_refs[k], land_refs[k].at[me], refs[-1].at[k])
                   for k in range(n) if not scatter[k]]
            for cp in own:
                cp.start()
            for cp in own:
                cp.wait()
        token[...] = jnp.zeros_like(token)

    sems = [pltpu.SemaphoreType.DMA((len(m),)) for m in masks]
    out = pl.pallas_call(
        body, name=name,
        out_shape=tuple(sems + sems + [pltpu.HBM(s.shape, s.dtype) for s in srcs]
                        + [pltpu.HBM(l.shape, l.dtype) for l in lands] + [SDS((8, 128), F32)]),
        in_specs=[_HBM] * (2 * n),
        out_specs=tuple([_SEM] * (2 * n) + [_HBM] * (2 * n) + [pl.BlockSpec(memory_space=pltpu.VMEM)]),
        input_output_aliases={i: 2 * n + i for i in range(2 * n)},
        scratch_shapes=[pltpu.SemaphoreType.DMA((n,))] if place_own else [],
        compiler_params=pltpu.CompilerParams(has_side_effects=_EFFECT),
    )(*srcs, *lands)
    return out[:n], out[n:2 * n], out[2 * n:3 * n], out[3 * n:4 * n], out[-1]


def _exchange_wait(send_sems, recv_sems, srcs_thru, lands_thru, scatter, after, name, masks=ALL_PEERS):
    n = len(srcs_thru)
    masks = _per_array(masks, n)

    def body(*refs):
        src_refs, land_refs = refs[:n], refs[n:2 * n]
        send_refs, recv_refs = refs[2 * n:3 * n], refs[3 * n:4 * n]
        for k in range(n):
            for slot, mask in enumerate(masks[k]):
                args = (src_refs[k], land_refs[k], send_refs[k], recv_refs[k], mask, slot, scatter[k])
                _split_copy(*args, True).wait_send()
                _split_copy(*args, False).wait_recv()

    out = pl.pallas_call(
        body, name=name,
        out_shape=tuple([pltpu.HBM(s.shape, s.dtype) for s in srcs_thru]
                        + [pltpu.HBM(l.shape, l.dtype) for l in lands_thru]),
        in_specs=[_HBM] * (2 * n) + [_SEM] * (2 * n) + [pl.BlockSpec(memory_space=pl.ANY)],
        out_specs=tuple([_HBM] * (2 * n)),
        input_output_aliases={i: i for i in range(2 * n)},
        compiler_params=pltpu.CompilerParams(has_side_effects=_EFFECT),
    )(*srcs_thru, *lands_thru, *send_sems, *recv_sems, after)
    return out[:n], out[n:]


def _wait_and_forward(send_sems, recv_sems, srcs_thru, lands_thru, after, name, masks):
    n = len(lands_thru)

    def body(*refs):
        src_refs, land_refs = refs[:n], refs[n:2 * n]
        send_refs, recv_refs = refs[2 * n:3 * n], refs[3 * n:4 * n]
        outs = refs[4 * n + 1:]
        fsend, frecv, token = outs[2 * n:3 * n], outs[3 * n:4 * n], outs[-1]
        for k in range(n):
            for slot, mask in enumerate(masks):
                args = (src_refs[k], land_refs[k], send_refs[k], recv_refs[k], mask, slot, False)
                _split_copy(*args, True).wait_send()
                _split_copy(*args, False).wait_recv()
        sibling, _ = _peer_of(SIBLING)
        for k in range(n):
            for slot, mask in enumerate(OTHER_CHIPS):
                _, mine = _peer_of(mask)
                pltpu.make_async_remote_copy(
                    src_ref=land_refs[k].at[mine], dst_ref=land_refs[k].at[mine], send_sem=fsend[k].at[slot],
                    recv_sem=frecv[k].at[slot], device_id=sibling, device_id_type=pl.DeviceIdType.MESH).start()
        token[...] = jnp.zeros_like(token)

    sem = pltpu.SemaphoreType.DMA((len(OTHER_CHIPS),))
    out = pl.pallas_call(
        body, name=name,
        out_shape=tuple([pltpu.HBM(s.shape, s.dtype) for s in srcs_thru] + [pltpu.HBM(l.shape, l.dtype) for l in lands_thru]
                        + [sem] * (2 * n) + [SDS((8, 128), F32)]),
        in_specs=[_HBM] * (2 * n) + [_SEM] * (2 * n) + [pl.BlockSpec(memory_space=pl.ANY)],
        out_specs=tuple([_HBM] * (2 * n) + [_SEM] * (2 * n) + [pl.BlockSpec(memory_space=pltpu.VMEM)]),
        input_output_aliases={i: i for i in range(2 * n)},
        compiler_params=pltpu.CompilerParams(has_side_effects=_EFFECT),
    )(*srcs_thru, *lands_thru, *send_sems, *recv_sems, after)
    return out[2 * n:3 * n], out[3 * n:4 * n], out[n:2 * n], out[-1]


def _forward_wait(send_sems, recv_sems, lands, after, name):
    n = len(lands)

    def body(*refs):
        land_refs, send_refs, recv_refs = refs[:n], refs[n:2 * n], refs[2 * n:3 * n]
        sibling, _ = _peer_of(SIBLING)
        for k in range(n):
            for slot, mask in enumerate(OTHER_CHIPS):
                _, mine = _peer_of(mask)
                _, theirs = _peer_of(mask | SIBLING)
                for block, wait in ((mine, "wait_send"), (theirs, "wait_recv")):
                    getattr(pltpu.make_async_remote_copy(
                        src_ref=land_refs[k].at[block], dst_ref=land_refs[k].at[block],
                        send_sem=send_refs[k].at[slot], recv_sem=recv_refs[k].at[slot], device_id=sibling,
                        device_id_type=pl.DeviceIdType.MESH), wait)()

    return pl.pallas_call(
        body, name=name, out_shape=tuple(pltpu.HBM(l.shape, l.dtype) for l in lands),
        in_specs=[_HBM] * n + [_SEM] * (2 * n) + [pl.BlockSpec(memory_space=pl.ANY)],
        out_specs=tuple([_HBM] * n), input_output_aliases={i: i for i in range(n)},
        compiler_params=pltpu.CompilerParams(has_side_effects=_EFFECT),
    )(*lands, *send_sems, *recv_sems, after)


def _pair_copy(parts_ref, land_ref, send_sem, recv_sem, slot, chip, outgoing):
    sibling, _ = _peer_of(SIBLING)
    _, block = _peer_of(chip | SIBLING if outgoing else chip)
    return pltpu.make_async_remote_copy(
        src_ref=parts_ref.at[block], dst_ref=land_ref.at[block], send_sem=send_sem.at[slot], recv_sem=recv_sem.at[slot],
        device_id=sibling, device_id_type=pl.DeviceIdType.MESH)


def _pair_start(parts, name):
    n = len(parts)
    lands = [pltpu.with_memory_space_constraint(lax.empty(p.shape, p.dtype), pltpu.HBM) for p in parts]
    parts = [pltpu.with_memory_space_constraint(p, pltpu.HBM) for p in parts]

    def body(*refs):
        part_refs, land_refs = refs[:n], refs[n:2 * n]
        send_sems, recv_sems, token = refs[2 * n:3 * n], refs[3 * n:4 * n], refs[-1]
        for k in range(n):
            for slot, chip in enumerate(CHIPS):
                _pair_copy(part_refs[k], land_refs[k], send_sems[k], recv_sems[k], slot, chip, True).start()
        token[...] = jnp.zeros_like(token)

    sem = pltpu.SemaphoreType.DMA((len(CHIPS),))
    out = pl.pallas_call(
        body, name=name,
        out_shape=tuple([sem] * (2 * n) + [pltpu.HBM(p.shape, p.dtype) for p in parts + lands] + [SDS((8, 128), F32)]),
        in_specs=[_HBM] * (2 * n),
        out_specs=tuple([_SEM] * (2 * n) + [_HBM] * (2 * n) + [pl.BlockSpec(memory_space=pltpu.VMEM)]),
        input_output_aliases={i: 2 * n + i for i in range(2 * n)},
        compiler_params=pltpu.CompilerParams(has_side_effects=_EFFECT),
    )(*parts, *lands)
    return out[:n], out[n:2 * n], out[2 * n:3 * n], out[3 * n:4 * n], out[-1]


def _pair_wait(send_sems, recv_sems, parts_thru, lands_thru, after, name):
    n = len(parts_thru)

    def body(*refs):
        part_refs, land_refs = refs[:n], refs[n:2 * n]
        send_refs, recv_refs = refs[2 * n:3 * n], refs[3 * n:4 * n]
        for k in range(n):
            for slot, chip in enumerate(CHIPS):
                args = (part_refs[k], land_refs[k], send_refs[k], recv_refs[k], slot, chip)
                _pair_copy(*args, True).wait_send()
                _pair_copy(*args, False).wait_recv()

    out = pl.pallas_call(
        body, name=name,
        out_shape=tuple(pltpu.HBM(p.shape, p.dtype) for p in list(parts_thru) + list(lands_thru)),
        in_specs=[_HBM] * (2 * n) + [_SEM] * (2 * n) + [pl.BlockSpec(memory_space=pl.ANY)],
        out_specs=tuple([_HBM] * (2 * n)), input_output_aliases={i: i for i in range(2 * n)},
        compiler_params=pltpu.CompilerParams(has_side_effects=_EFFECT),
    )(*parts_thru, *lands_thru, *send_sems, *recv_sems, after)
    return out[:n], out[n:]


def _with_own(landed, own, me):
    return lax.dynamic_update_slice_in_dim(landed, own[None], me, axis=0)


def _kv_proj(mem, g_mem, wkv3):
    m = mem.shape[0]

    def body(mem_ref, g_ref, w_ref, kv_ref, mn_ref):
        y, _, _ = _rms_fwd(mem_ref[...], g_ref[...])
        yb = _bf(y)
        mn_ref[...] = yb
        for b in range(N_DEV):
            kv_ref[:, 256 * b:256 * (b + 1)] = _dot(yb, w_ref[b])

    return pl.pallas_call(body, name="kv_proj", out_shape=(SDS((m, 2048), F32), SDS(mem.shape, BF16)),
                          compiler_params=_cparams())(mem, g_mem, wkv3)


def _fwd_in(x, g_mix, w_in3, b_in, ts):
    s, d = x.shape

    def body(x_ref, g_ref, w_ref, b_ref, z_ref, a_ref):
        hn, _, _ = _rms_fwd(x_ref[...], g_ref[...])
        hb = _bf(hn)
        for b in range(N_DEV):
            z_ref[:, 256 * b:256 * (b + 1)] = _dot(hb, w_ref[b]) + b_ref[:, 256 * b:256 * (b + 1)]
        a_ref[...] = z_ref[:, 0:512] * jax.nn.sigmoid(z_ref[:, 512:1024])

    return pl.pallas_call(
        body, name="fwd_in", grid=(s // ts,),
        in_specs=[_row(ts, d), _const(g_mix.shape), _const(w_in3.shape), _const(b_in.shape)],
        out_specs=(_row(ts, 2048), _row(ts, 512)),
        out_shape=(SDS((s, 2048), F32), SDS((s, 512), F32)),
        compiler_params=_cparams("arbitrary"))(x, g_mix, w_in3, b_in)


def _conv_fwd(a, w, b):
    s, cw = a.shape
    rc = 256 if s % 256 == 0 else 128

    def body(a_ref, w_ref, b_ref, c_ref, pad):
        pad[0:CONV_PAD, :] = jnp.zeros((CONV_PAD, 128), F32)
        pad[CONV_PAD:, :] = a_ref[...]

        def chunk(i, carry):
            r0 = pl.multiple_of(i * rc, rc)
            acc = jnp.zeros((rc, 128), F32) + b_ref[...]
            for res, offs in _by_residue(range(CONV_PAD - CONV_K + 1, CONV_PAD + 1)):
                shifted = pad[pl.ds(r0 + res, rc + offs[-1] - res), :]
                for off in offs:
                    k = off - (CONV_PAD - CONV_K + 1)
                    acc = acc + w_ref[k:k + 1, :] * shifted[off - res:off - res + rc, :]
            c_ref[pl.ds(r0, rc), :] = acc
            return carry

        lax.fori_loop(0, s // rc, chunk, 0)

    blk = lambda r: pl.BlockSpec((r, 128), lambda j: (0, j))
    return pl.pallas_call(
        body, name="conv_fwd", grid=(cw // 128,),
        in_specs=[blk(s), blk(CONV_K), blk(1)], out_specs=blk(s), out_shape=SDS((s, cw), F32),
        scratch_shapes=[pltpu.VMEM((s + CONV_PAD, 128), F32)],
        compiler_params=_cparams("arbitrary"))(a, w, b)


def _fwd_out(x, c, z, cln_g, cln_b, gln_g, gln_b, ws, bst, w_out, ts):
    s, d = x.shape

    def body(x_ref, c_ref, zuv_ref, clg, clb, glg, glb, ws_ref, bst_ref, wo_ref, h1_ref, mixed_scr):
        cl, _, _ = _ln_fwd(c_ref[...], clg[...], clb[...])
        co = cl * jax.nn.sigmoid(cl)
        u, _ = _gelu(zuv_ref[:, 0:512])
        vg, _ = _gelu(zuv_ref[:, 512:1024])
        vln, _, _ = _ln_fwd(vg, glg[...], glb[...])
        _mix_fwd(_bf(vln), ws_ref, bst_ref, mixed_scr, ts)
        gm = u * mixed_scr[...]
        h1_ref[...] = x_ref[...] + _dot(_bf(co), wo_ref[0:512, :]) + _dot(_bf(gm), wo_ref[512:1024, :])

    return pl.pallas_call(
        body, name="fwd_out", grid=(s // ts,),
        in_specs=[_row(ts, d), _row(ts, 512), _row(ts, 1024, 1), _const(cln_g.shape), _const(cln_b.shape),
                  _const(gln_g.shape), _const(gln_b.shape), _const(ws.shape), _const(bst.shape), _const(w_out.shape)],
        out_specs=_row(ts, d), out_shape=SDS((s, d), F32),
        scratch_shapes=[pltpu.VMEM((ts, 512), F32)],
        compiler_params=_cparams("arbitrary"))(x, c, z, cln_g, cln_b, gln_g, gln_b, ws, bst, w_out)


def _softmax_rows(sc):
    m = jnp.max(sc, axis=-1, keepdims=True)
    e = jnp.exp(sc - m)
    return e / jnp.sum(e, axis=-1, keepdims=True)


def _fwd_xa(h1, g_xa, wq, kv, wo, ts):
    s, d = h1.shape
    scale = XA_DH ** -0.5

    def body(h_ref, g_ref, wq_ref, kv_ref, wo_ref, h2_ref, q_ref, o_ref, o_scr):
        hn, _, _ = _rms_fwd(h_ref[...], g_ref[...])
        q_ref[...] = _bf(_dot(_bf(hn), wq_ref[...]))
        for h in range(XA_HEADS):
            qh = q_ref[:, XA_DH * h:XA_DH * (h + 1)]
            kh = _bf(kv_ref[:, XA_DH * h:XA_DH * (h + 1)])
            vh = _bf(kv_ref[:, d + XA_DH * h:d + XA_DH * (h + 1)])
            p = _softmax_rows(_dot(qh, kh, NT) * scale)
            o_scr[:, XA_DH * h:XA_DH * (h + 1)] = _dot(_bf(p), vh)
        o_ref[...] = _bf(o_scr[...])
        h2_ref[...] = h_ref[...] + _dot(o_ref[...], wo_ref[...])

    return pl.pallas_call(
        body, name="fwd_xa", grid=(s // ts,),
        in_specs=[_row(ts, d), _const(g_xa.shape), _const(wq.shape), _const(kv.shape), _const(wo.shape)],
        out_specs=(_row(ts, d), _row(ts, d), _row(ts, d)),
        out_shape=(SDS((s, d), F32), SDS((s, d), BF16), SDS((s, d), BF16)),
        scratch_shapes=[pltpu.VMEM((ts, d), F32)],
        compiler_params=_cparams("arbitrary"))(h1, g_xa, wq, kv, wo)


def _fwd_ffn(h2, g_ffn, wgut, wdown, g_final, target, ts):
    s, d = h2.shape
    hid = wdown.shape[0]
    hc = hid // 2

    def body(h_ref, g_ref, wgu_ref, wd_ref, gf_ref, t_ref, dh3_ref, dh3b_ref, gu_ref, hn_ref, loss_ref, dgf_ref):
        hn, _, _ = _rms_fwd(h_ref[...], g_ref[...])
        hb = _bf(hn)
        hn_ref[...] = hb
        h3 = h_ref[...]
        for n in range(2):
            g = _dot(hb, wgu_ref[hc * n:hc * (n + 1), :], NT)
            u = _dot(hb, wgu_ref[hid + hc * n:hid + hc * (n + 1), :], NT)
            gu_ref[:, hc * n:hc * (n + 1)] = g
            gu_ref[:, hid + hc * n:hid + hc * (n + 1)] = u
            act = g * jax.nn.sigmoid(g) * u
            h3 = h3 + _dot(_bf(act), wd_ref[hc * n:hc * (n + 1), :])
        y, xh, r = _rms_fwd(h3, gf_ref[...])
        diff = y - t_ref[...]
        part = 0.5 * jnp.sum(jnp.mean(diff * diff, axis=-1, keepdims=True), axis=0, keepdims=True)
        _accumulate(loss_ref, jnp.zeros(loss_ref.shape, F32) + part)
        dh3, dgf = _rms_bwd(diff * (1.0 / d), xh, r, gf_ref[...])
        dh3_ref[...] = dh3
        dh3b_ref[...] = _bf(dh3)
        _accumulate(dgf_ref, dgf)

    return pl.pallas_call(
        body, name="fwd_ffn", grid=(s // ts,),
        in_specs=[_row(ts, d), _const(g_ffn.shape), _const(wgut.shape), _const(wdown.shape), _const(g_final.shape),
                  _row(ts, d)],
        out_specs=(_row(ts, d), _row(ts, d), _row(ts, 2 * hid), _row(ts, d), _acc((1, 128)), _acc((1, d))),
        out_shape=(SDS((s, d), F32), SDS((s, d), BF16), SDS((s, 2 * hid), F32), SDS((s, d), BF16), SDS((1, 128), F32),
                   SDS((1, d), F32)),
        compiler_params=_cparams("arbitrary"))(h2, g_ffn, wgut, wdown, g_final, target)


def _bwd_ffn(h2, dh3, gu, g_ffn, wgut, wdown, ts):
    s, d = h2.shape
    hid = wdown.shape[0]
    hc = hid // 2

    def body(h_ref, dh3_ref, gu_ref, g_ref, wgu_ref, wd_ref, dh2_ref, act_ref, dgu_ref, dg_ref):
        _, xh, r = _rms_fwd(h_ref[...], g_ref[...])
        db = _bf(dh3_ref[...])
        dhn = jnp.zeros((ts, d), F32)
        for n in range(2):
            wg = wgu_ref[hc * n:hc * (n + 1), :]
            wu = wgu_ref[hid + hc * n:hid + hc * (n + 1), :]
            g = gu_ref[:, hc * n:hc * (n + 1)]
            u = gu_ref[:, hid + hc * n:hid + hc * (n + 1)]
            sg = jax.nn.sigmoid(g)
            sl = g * sg
            act_ref[:, hc * n:hc * (n + 1)] = _bf(sl * u)
            dact = _dot(db, wd_ref[hc * n:hc * (n + 1), :], NT)
            dgb = _bf(dact * u * _silu_grad(g, sg))
            dub = _bf(dact * sl)
            dgu_ref[:, hc * n:hc * (n + 1)] = dgb
            dgu_ref[:, hid + hc * n:hid + hc * (n + 1)] = dub
            dhn = dhn + _dot(dgb, wg) + _dot(dub, wu)
        dx, dg = _rms_bwd(dhn, xh, r, g_ref[...])
        dh2_ref[...] = dh3_ref[...] + dx
        _accumulate(dg_ref, dg)

    return pl.pallas_call(
        body, name="bwd_ffn", grid=(s // ts,),
        in_specs=[_row(ts, d), _row(ts, d), _row(ts, 2 * hid), _const(g_ffn.shape), _const(wgut.shape),
                  _const(wdown.shape)],
        out_specs=(_row(ts, d), _row(ts, hid), _row(ts, 2 * hid), _acc((1, d))),
        out_shape=(SDS((s, d), F32), SDS((s, hid), BF16), SDS((s, 2 * hid), BF16), SDS((1, d), F32)),
        compiler_params=_cparams("arbitrary"))(h2, dh3, gu, g_ffn, wgut, wdown)


def _bwd_xa(h1, dh2, qb, ob, g_xa, wq, wo, kv, ts):
    s, d = h1.shape
    scale = XA_DH ** -0.5

    def body(h_ref, dh2_ref, q_ref, o_ref, g_ref, wq_ref, wo_ref, kv_ref, dh1_ref, dwq_ref, dwo_ref, dkv_ref, dg_ref,
             dq_scr, accq, acco):
        hn, xh, r = _rms_fwd(h_ref[...], g_ref[...])
        hb = _bf(hn)
        dh2b = _bf(dh2_ref[...])
        do = _dot(dh2b, wo_ref[...], NT)

        @pl.when(pl.program_id(0) == 0)
        def _():
            dkv_ref[...] = jnp.zeros_like(dkv_ref)
            accq[...] = jnp.zeros_like(accq)
            acco[...] = jnp.zeros_like(acco)

        for h in range(XA_HEADS):
            lo, hi = XA_DH * h, XA_DH * (h + 1)
            qh = q_ref[:, lo:hi]
            kh = _bf(kv_ref[:, lo:hi])
            vh = _bf(kv_ref[:, d + lo:d + hi])
            p = _softmax_rows(_dot(qh, kh, NT) * scale)
            pb = _bf(p)
            doh = _bf(do[:, lo:hi])
            dp = _dot(doh, vh, NT)
            ds = p * (dp - jnp.sum(p * dp, axis=-1, keepdims=True)) * scale
            dsb = _bf(ds)
            dq_scr[:, lo:hi] = _dot(dsb, kh)
            dkv_ref[:, lo:hi] += _dot(dsb, qh, TN)
            dkv_ref[:, d + lo:d + hi] += _dot(pb, doh, TN)
        dqb = _bf(dq_scr[...])
        accq[...] += _dot(hb, dqb, TN)
        acco[...] += _dot(o_ref[...], dh2b, TN)
        dx, dg = _rms_bwd(_dot(dqb, wq_ref[...], NT), xh, r, g_ref[...])
        dh1_ref[...] = dh2_ref[...] + dx
        _accumulate(dg_ref, dg)

        @pl.when(pl.program_id(0) == pl.num_programs(0) - 1)
        def _():
            dwq_ref[...] = _bf(accq[...])
            dwo_ref[...] = _bf(acco[...])

    return pl.pallas_call(
        body, name="bwd_xa", grid=(s // ts,),
        in_specs=[_row(ts, d), _row(ts, d), _row(ts, d), _row(ts, d), _const(g_xa.shape), _const(wq.shape),
                  _const(wo.shape), _const(kv.shape)],
        out_specs=(_row(ts, d), _acc((d, d)), _acc((d, d)), _acc(kv.shape), _acc((1, d))),
        out_shape=(SDS((s, d), F32), SDS((d, d), BF16), SDS((d, d), BF16), SDS(kv.shape, F32), SDS((1, d), F32)),
        scratch_shapes=[pltpu.VMEM((ts, d), F32), pltpu.VMEM((d, d), F32), pltpu.VMEM((d, d), F32)],
        compiler_params=_cparams("arbitrary"))(h1, dh2, qb, ob, g_xa, wq, wo, kv)


def _bwd_kv(dkv, mn, mem, g_mem, wkv3):
    d = mem.shape[1]

    def body(dkv_ref, mn_ref, mem_ref, g_ref, w_ref, dw_ref, dg_ref):
        dkvb = _bf(dkv_ref[...])
        dmn = jnp.zeros(mem_ref.shape, F32)
        for b in range(N_DEV):
            blk = dkvb[:, 256 * b:256 * (b + 1)]
            dmn = dmn + _dot(blk, w_ref[b], NT)
            dw_ref[b] = _bf(_dot(mn_ref[...], blk, TN))
        _, xh, r = _rms_fwd(mem_ref[...], g_ref[...])
        _, dg = _rms_bwd(dmn, xh, r, g_ref[...])
        dg_ref[...] = dg

    return pl.pallas_call(body, name="bwd_kv", out_shape=(SDS(wkv3.shape, BF16), SDS((1, d), F32)),
                          compiler_params=_cparams())(dkv, mn, mem, g_mem, wkv3)


def _bwd_out(dh1, c, z, cln_g, cln_b, gln_g, gln_b, ws, wst, bst, w_out, ts):
    s, d = dh1.shape
    nh = GM_HEADS

    def body(dh1_ref, c_ref, zuv_ref, clg, clb, glg, glb, ws_ref, wst_ref, bst_ref, wo_ref,
             dwo_ref, dc_ref, dzuv_ref, dws_ref, dbst_ref, dclg_ref, dclb_ref, dglg_ref, dglb_ref, dbin_ref,
             mixed_scr, dv_scr, acc):
        cl, chat, crs = _ln_fwd(c_ref[...], clg[...], clb[...])
        sg = jax.nn.sigmoid(cl)
        zu = zuv_ref[:, 0:512]
        zv = zuv_ref[:, 512:1024]
        u, tu = _gelu(zu)
        vg, tv = _gelu(zv)
        vln, vhat, vrs = _ln_fwd(vg, glg[...], glb[...])
        vb = _bf(vln)
        _mix_fwd(vb, ws_ref, bst_ref, mixed_scr, ts)
        mixed = mixed_scr[...]
        dh1b = _bf(dh1_ref[...])
        dcat = _dot(dh1b, wo_ref[...], NT)
        dgm = dcat[:, 512:1024]
        dc, dclg, dclb = _ln_bwd(dcat[:, 0:512] * _silu_grad(cl, sg), chat, crs, clg[...])
        dc_ref[...] = dc
        dzu = dgm * mixed * _gelu_grad(zu, tu)
        dm = dgm * u

        @pl.when(pl.program_id(0) == 0)
        def _():
            dws_ref[...] = jnp.zeros_like(dws_ref)
            dbst_ref[...] = jnp.zeros_like(dbst_ref)
            acc[...] = jnp.zeros_like(acc)

        acc[0:512, :] += _dot(_bf(cl * sg), dh1b, TN)
        acc[512:1024, :] += _dot(_bf(u * mixed), dh1b, TN)

        @pl.when(pl.program_id(0) == pl.num_programs(0) - 1)
        def _():
            dwo_ref[...] = _bf(acc[...])

        tril, triu, lo = _mix_masks()
        head = lax.broadcasted_iota(jnp.int32, (1, nh), 1)
        for j in range(nh // 2):
            w0t = _bf(jnp.where(triu, wst_ref[2 * j], 0.0))
            w1t = _bf(jnp.where(triu, wst_ref[2 * j + 1], 0.0))
            for n in range(ts // CHUNK):
                rows = slice(n * CHUNK, (n + 1) * CHUNK)
                lanes = slice(j * 128, (j + 1) * 128)
                dmc = dm[rows, lanes]
                dmb = _bf(dmc)
                dv_scr[rows, lanes] = jnp.where(lo, _dot(w0t, dmb), _dot(w1t, dmb))
                vc = vb[rows, lanes]
                d0 = jnp.where(lo, dmc, 0.0)
                d1 = dmc - d0
                dws_ref[2 * j] += jnp.where(tril, _dot(_bf(d0), vc, NT), 0.0)
                dws_ref[2 * j + 1] += jnp.where(tril, _dot(_bf(d1), vc, NT), 0.0)
                dbst_ref[...] += (jnp.sum(d0, axis=1, keepdims=True) * (head == 2 * j).astype(F32)
                                  + jnp.sum(d1, axis=1, keepdims=True) * (head == 2 * j + 1).astype(F32))
        dvg, dglg, dglb = _ln_bwd(dv_scr[...], vhat, vrs, glg[...])
        dzv = dvg * _gelu_grad(zv, tv)
        dzuv_ref[:, 0:512] = _bf(dzu)
        dzuv_ref[:, 512:1024] = _bf(dzv)
        _accumulate(dclg_ref, dclg)
        _accumulate(dclb_ref, dclb)
        _accumulate(dglg_ref, dglg)
        _accumulate(dglb_ref, dglb)
        _accumulate(dbin_ref, jnp.concatenate([jnp.sum(dzu, axis=0, keepdims=True),
                                               jnp.sum(dzv, axis=0, keepdims=True)], axis=1))

    vec = (1, 512)
    return pl.pallas_call(
        body, name="bwd_out", grid=(s // ts,),
        in_specs=[_row(ts, d), _row(ts, 512), _row(ts, 1024, 1), _const(cln_g.shape), _const(cln_b.shape),
                  _const(gln_g.shape), _const(gln_b.shape), _const(ws.shape), _const(wst.shape), _const(bst.shape),
                  _const(w_out.shape)],
        out_specs=(_acc((d, d)), _row(ts, 512), _row(ts, 1024), _acc(ws.shape), _acc(bst.shape), _acc(vec), _acc(vec),
                   _acc(vec), _acc(vec), _acc((1, 1024))),
        out_shape=(SDS((d, d), BF16), SDS((s, 512), F32), SDS((s, 1024), BF16), SDS(ws.shape, F32),
                   SDS(bst.shape, F32), SDS(vec, F32), SDS(vec, F32), SDS(vec, F32), SDS(vec, F32), SDS((1, 1024), F32)),
        scratch_shapes=[pltpu.VMEM((ts, 512), F32), pltpu.VMEM((ts, 512), F32), pltpu.VMEM((d, d), F32)],
        compiler_params=_cparams("arbitrary"))(dh1, c, z, cln_g, cln_b, gln_g, gln_b, ws, wst, bst, w_out)


CONV_RC = 256
FWD_TAPS = range(CONV_PAD - CONV_K + 1, CONV_PAD + 1)


def _bwd_in(x, dh1, dc, a, z, dzuv, g_mix, w_in3, conv_w, ts):
    s, d = x.shape
    cw = conv_w.shape[1]
    nb = cw // 128
    n = s // ts
    rc = min(CONV_RC, ts)
    per_halo = ts // CONV_PAD
    w4 = conv_w.reshape(CONV_K, nb, 128).transpose(1, 0, 2)

    def body(x_ref, dh1_ref, dc_ref, a_ref, ahalo_ref, zag_ref, dzuv_ref, g_ref, w_ref, cw_ref,
             dx_ref, dw_ref, dcw_ref, dcb_ref, dbin_ref, dg_ref, dz_ref, acc, pad_d, pad_a, part, da_scr):
        i = pl.program_id(0)

        @pl.when(i == 0)
        def _():
            acc[...] = jnp.zeros_like(acc)
            part[...] = jnp.zeros_like(part)
            pad_d[:, ts:, :] = jnp.zeros((nb, CONV_PAD, 128), F32)

        before = jnp.where(i == n - 1, 0.0, ahalo_ref[...])
        for blk in range(nb):
            lanes = slice(128 * blk, 128 * (blk + 1))
            pad_d[blk, 0:ts, :] = dc_ref[:, lanes]
            pad_a[blk, 0:CONV_PAD, :] = before[:, lanes]
            pad_a[blk, CONV_PAD:, :] = a_ref[:, lanes]

        def rows8(v):
            return jnp.sum(v.reshape(rc // 8, 8, 128), axis=0)

        def unit(blk, r0):
            acc_da = jnp.zeros((rc, 128), F32)
            for res, offs in _by_residue(range(0, CONV_K)):
                shifted = pad_d[blk, pl.ds(r0 + res, rc + offs[-1] - res), :]
                for off in offs:
                    k = CONV_K - 1 - off
                    acc_da = acc_da + cw_ref[blk, k:k + 1, :] * shifted[off - res:off - res + rc, :]
            da_scr[blk, pl.ds(r0, rc), :] = acc_da
            dcc = pad_d[blk, pl.ds(r0, rc), :]
            for res, offs in _by_residue(FWD_TAPS):
                shifted = pad_a[blk, pl.ds(r0 + res, rc + offs[-1] - res), :]
                for off in offs:
                    k = off - FWD_TAPS[0]
                    part[blk, 8 * k:8 * k + 8, :] += rows8(dcc * shifted[off - res:off - res + rc, :])
            part[blk, 8 * CONV_PAD:, :] += rows8(dcc)

        for blk in range(nb):
            for r0 in range(0, ts, rc):
                unit(blk, r0)
        pad_d[:, ts:, :] = pad_d[:, 0:CONV_PAD, :]

        za = zag_ref[:, 0:512]
        sg = jax.nn.sigmoid(zag_ref[:, 512:1024])
        da_ = jnp.concatenate([da_scr[blk] for blk in range(nb)], axis=1)
        dza = da_ * sg
        dzg = da_ * za * sg * (1.0 - sg)
        dz_ref[:, 0:512] = _bf(dza)
        dz_ref[:, 512:1024] = _bf(dzg)
        dz_ref[:, 1024:2048] = dzuv_ref[...]
        dhn = jnp.zeros((ts, d), F32)
        for b in range(N_DEV):
            dhn = dhn + _dot(dz_ref[:, 256 * b:256 * (b + 1)], w_ref[b], NT)
        hn, xh, r = _rms_fwd(x_ref[...], g_ref[...])
        acc[...] += _dot(_bf(hn), dz_ref[...], TN)

        @pl.when(i == n - 1)
        def _():
            for b in range(N_DEV):
                dw_ref[b] = _bf(acc[:, 256 * b:256 * (b + 1)])
            for blk in range(nb):
                sums = jnp.sum(part[blk].reshape(CONV_PAD + 1, 8, 128), axis=1)
                dcw_ref[:, 128 * blk:128 * (blk + 1)] = sums[0:CONV_PAD, :]
                dcb_ref[:, 128 * blk:128 * (blk + 1)] = sums[CONV_PAD:, :]

        dxn, dg = _rms_bwd(dhn, xh, r, g_ref[...])
        dx_ref[...] = dh1_ref[...] + dxn
        _accumulate(dg_ref, dg)
        _accumulate(dbin_ref, jnp.concatenate([jnp.sum(dza, axis=0, keepdims=True),
                                               jnp.sum(dzg, axis=0, keepdims=True)], axis=1))

    back = lambda width, col=0: pl.BlockSpec((ts, width), lambda i: (n - 1 - i, col))
    halo = pl.BlockSpec((CONV_PAD, cw), lambda i: (jnp.maximum((n - 1 - i) * per_halo - 1, 0), 0))
    return pl.pallas_call(
        body, name="bwd_in", grid=(n,),
        in_specs=[back(d), back(d), back(cw), back(cw), halo, back(1024, 0), back(1024), _const(g_mix.shape),
                  _const(w_in3.shape), _const(w4.shape)],
        out_specs=(back(d), _acc(w_in3.shape), _acc((CONV_PAD, cw)), _acc((1, cw)), _acc((1, 1024)), _acc((1, d))),
        out_shape=(SDS((s, d), F32), SDS(w_in3.shape, BF16), SDS((CONV_PAD, cw), F32), SDS((1, cw), F32),
                   SDS((1, 1024), F32), SDS((1, d), F32)),
        scratch_shapes=[pltpu.VMEM((ts, 2048), BF16), pltpu.VMEM((d, 2048), F32),
                        pltpu.VMEM((nb, ts + CONV_PAD, 128), F32), pltpu.VMEM((nb, ts + CONV_PAD, 128), F32),
                        pltpu.VMEM((nb, 8 * (CONV_PAD + 1), 128), F32), pltpu.VMEM((nb, ts, 128), F32)],
        compiler_params=_cparams("arbitrary"))(x, dh1, dc, a, a, z, dzuv, g_mix, w_in3, w4)


def _tn_matmul(a, b, tm, name):
    s, m = a.shape
    n = b.shape[1]
    ts = _row_tile(s, (1024, 512, 256, 128))
    n_s = s // ts

    def body(a_ref, b_ref, o_ref, acc):
        k = pl.program_id(1)

        @pl.when(k == 0)
        def _():
            acc[...] = jnp.zeros_like(acc)

        acc[...] += _dot(a_ref[...], b_ref[...], TN)

        @pl.when(k == n_s - 1)
        def _():
            o_ref[...] = _bf(acc[...])

    return pl.pallas_call(
        body, name=name, grid=(m // tm, n_s),
        in_specs=[pl.BlockSpec((ts, tm), lambda i, k: (k, i)), pl.BlockSpec((ts, n), lambda i, k: (k, 0))],
        out_specs=pl.BlockSpec((tm, n), lambda i, k: (i, 0)), out_shape=SDS((m, n), BF16),
        scratch_shapes=[pltpu.VMEM((tm, n), F32)],
        compiler_params=_cparams("parallel", "arbitrary"))(a, b)


def _row_tile(r, cands):
    for cand in cands:
        if r % cand == 0:
            return cand
    return r


def _sum_in_device_order(me_ref, land_ref, own_ref):
    acc = None
    for dev in range(N_DEV):
        part = jnp.where(me_ref[0] == dev, own_ref[0], land_ref[dev]).astype(F32)
        acc = part if acc is None else acc + part
    return acc


def _sum8(land, own, me, name):
    _, r, c = land.shape
    tr = _row_tile(r, (512, 256, 352, 128, 8))
    own3 = own if own.ndim == 3 else own[None]
    own_map = (lambda i, me_ref: (me_ref[0], i, 0)) if own.ndim == 3 else (lambda i, me_ref: (0, i, 0))

    def body(me_ref, land_ref, own_ref, o_ref):
        o_ref[...] = _sum_in_device_order(me_ref, land_ref, own_ref)

    return pl.pallas_call(
        body, name=name, out_shape=SDS((r, c), F32),
        grid_spec=pltpu.PrefetchScalarGridSpec(
            num_scalar_prefetch=1, grid=(r // tr,),
            in_specs=[pl.BlockSpec((N_DEV, tr, c), lambda i, me_ref: (0, i, 0)), pl.BlockSpec((1, tr, c), own_map)],
            out_specs=pl.BlockSpec((tr, c), lambda i, me_ref: (i, 0))),
        compiler_params=_cparams("parallel"))(me, land, own3)


def _adam_update(w, g, m, v):
    m2 = ADAM_B1 * m + (1.0 - ADAM_B1) * g
    v2 = ADAM_B2 * v + (1.0 - ADAM_B2) * (g * g)
    m_hat = m2 / (1.0 - ADAM_B1 ** ADAM_STEP)
    v_hat = v2 / (1.0 - ADAM_B2 ** ADAM_STEP)
    return -ADAM_LR * (m_hat / (jnp.sqrt(v_hat) + ADAM_EPS) + ADAM_WD * w), m2, v2


def _slot_spec(tr, c, mask):
    return pl.BlockSpec((1, tr, c), lambda i, me_ref: (me_ref[0] ^ mask, i, 0))


def _sum_adamw(land, parts, w, m, v, me, name, masks=ALL_PEERS):
    r, c = w.shape
    tr = _row_tile(r, (256, 128, 176, 8))
    n = len(masks)

    def body(me_ref, own_ref, *refs):
        w_ref, m_ref, v_ref, g_ref, d_ref, m2_ref, v2_ref = refs[n:]
        g = own_ref[0].astype(F32)
        for peer_ref in refs[:n]:
            g = g + peer_ref[0].astype(F32)
        g_ref[...] = g
        d_ref[...], m2_ref[...], v2_ref[...] = _adam_update(w_ref[...], g, m_ref[...], v_ref[...])

    blk = pl.BlockSpec((tr, c), lambda i, me_ref: (i, 0))
    return pl.pallas_call(
        body, name=name, out_shape=(SDS((r, c), F32),) * 4,
        grid_spec=pltpu.PrefetchScalarGridSpec(
            num_scalar_prefetch=1, grid=(r // tr,),
            in_specs=[_slot_spec(tr, c, 0)] + [_slot_spec(tr, c, mask) for mask in masks] + [blk, blk, blk],
            out_specs=(blk,) * 4),
        compiler_params=_cparams("parallel"))(me, parts, *([land] * n), w, m, v)


def _chip_sum(parts, from_sibling, me, name):
    _, r, c = parts.shape
    tr = _row_tile(r, (512, 256, 352, 128, 8))
    spec = pl.BlockSpec((1, tr, c), lambda j, i, me_ref: (me_ref[0] ^ (2 * j), i, 0))

    def body(me_ref, p_ref, q_ref, o_ref):
        o_ref[...] = _bf(p_ref[...].astype(F32) + q_ref[...].astype(F32))

    return pl.pallas_call(
        body, name=name, out_shape=SDS(parts.shape, BF16),
        grid_spec=pltpu.PrefetchScalarGridSpec(num_scalar_prefetch=1, grid=(len(CHIPS), r // tr), in_specs=[spec, spec],
                                               out_specs=spec),
        compiler_params=_cparams("parallel", "parallel"))(me, parts, from_sibling)


def _adamw_many(ws, gs, ms, vs, name):
    n = len(ws)

    def body(*refs):
        ins, outs = refs[:4 * n], refs[4 * n:]
        for i in range(n):
            w, g, m, v = (ins[j * n + i][...] for j in range(4))
            outs[i][...], outs[n + i][...], outs[2 * n + i][...] = _adam_update(w, g, m, v)

    shapes = [SDS(w.shape, F32) for w in ws]
    out = pl.pallas_call(body, name=name, out_shape=shapes * 3, compiler_params=_cparams())(*ws, *gs, *ms, *vs)
    return out[:n], out[n:2 * n], out[2 * n:]


def _pack(arrs):
    flat = jnp.concatenate([a.reshape(-1) for a in arrs])
    pad = (-flat.shape[0]) % (128 * 128)
    return jnp.pad(flat, (0, pad)).reshape(-1, 128)


def _unpack(packed, shapes):
    flat = packed.reshape(-1)
    out, off = [], 0
    for shp in shapes:
        size = 1
        for dim in shp:
            size *= dim
        out.append(flat[off:off + size].reshape(shp))
        off += size
    return out


SMALL = ("norm_mix_g", "b_in", "conv_w", "conv_b", "conv_ln_g", "conv_ln_b", "gm_ln_g", "gm_ln_b", "gm_w_s", "gm_b_s",
         "norm_xa_g", "mem_norm_g", "norm_ffn_g", "final_norm_g")
BIG = ("w_in", "w_out", "xa_wq", "xa_wkv", "xa_wo", "ffn_w_gate_up", "ffn_w_down")
WEIGHTS = ("norm_mix_g", "w_in", "b_in", "conv_w", "conv_b", "conv_ln_g", "conv_ln_b", "gm_ln_g", "gm_ln_b", "gm_w_s",
           "gm_b_s", "w_out", "norm_xa_g", "mem_norm_g", "xa_wq", "xa_wkv", "xa_wo", "norm_ffn_g", "ffn_w_gate_up",
           "ffn_w_down", "final_norm_g")


def kernel(x, mem, norm_mix_g, w_in, b_in, conv_w, conv_b, conv_ln_g, conv_ln_b, gm_ln_g, gm_ln_b, gm_w_s, gm_b_s, w_out, norm_xa_g, mem_norm_g, xa_wq, xa_wkv, xa_wo, norm_ffn_g, ffn_w_gate_up, ffn_w_down, final_norm_g, loss_target, m_norm_mix_g, m_w_in, m_b_in, m_conv_w, m_conv_b, m_conv_ln_g, m_conv_ln_b, m_gm_ln_g, m_gm_ln_b, m_gm_w_s, m_gm_b_s, m_w_out, m_norm_xa_g, m_mem_norm_g, m_xa_wq, m_xa_wkv, m_xa_wo, m_norm_ffn_g, m_ffn_w_gate_up, m_ffn_w_down, m_final_norm_g, v_norm_mix_g, v_w_in, v_b_in, v_conv_w, v_conv_b, v_conv_ln_g, v_conv_ln_b, v_gm_ln_g, v_gm_ln_b, v_gm_w_s, v_gm_b_s, v_w_out, v_norm_xa_g, v_mem_norm_g, v_xa_wq, v_xa_wkv, v_xa_wo, v_norm_ffn_g, v_ffn_w_gate_up, v_ffn_w_down, v_final_norm_g):
    w = dict(norm_mix_g=norm_mix_g, w_in=w_in, b_in=b_in, conv_w=conv_w, conv_b=conv_b, conv_ln_g=conv_ln_g,
             conv_ln_b=conv_ln_b, gm_ln_g=gm_ln_g, gm_ln_b=gm_ln_b, gm_w_s=gm_w_s, gm_b_s=gm_b_s, w_out=w_out,
             norm_xa_g=norm_xa_g, mem_norm_g=mem_norm_g, xa_wq=xa_wq, xa_wkv=xa_wkv, xa_wo=xa_wo,
             norm_ffn_g=norm_ffn_g, ffn_w_gate_up=ffn_w_gate_up, ffn_w_down=ffn_w_down, final_norm_g=final_norm_g)
    mom = dict(norm_mix_g=m_norm_mix_g, w_in=m_w_in, b_in=m_b_in, conv_w=m_conv_w, conv_b=m_conv_b,
               conv_ln_g=m_conv_ln_g, conv_ln_b=m_conv_ln_b, gm_ln_g=m_gm_ln_g, gm_ln_b=m_gm_ln_b, gm_w_s=m_gm_w_s,
               gm_b_s=m_gm_b_s, w_out=m_w_out, norm_xa_g=m_norm_xa_g, mem_norm_g=m_mem_norm_g, xa_wq=m_xa_wq,
               xa_wkv=m_xa_wkv, xa_wo=m_xa_wo, norm_ffn_g=m_norm_ffn_g, ffn_w_gate_up=m_ffn_w_gate_up,
               ffn_w_down=m_ffn_w_down, final_norm_g=m_final_norm_g)
    var = dict(norm_mix_g=v_norm_mix_g, w_in=v_w_in, b_in=v_b_in, conv_w=v_conv_w, conv_b=v_conv_b,
               conv_ln_g=v_conv_ln_g, conv_ln_b=v_conv_ln_b, gm_ln_g=v_gm_ln_g, gm_ln_b=v_gm_ln_b, gm_w_s=v_gm_w_s,
               gm_b_s=v_gm_b_s, w_out=v_w_out, norm_xa_g=v_norm_xa_g, mem_norm_g=v_mem_norm_g, xa_wq=v_xa_wq,
               xa_wkv=v_xa_wkv, xa_wo=v_xa_wo, norm_ffn_g=v_norm_ffn_g, ffn_w_gate_up=v_ffn_w_gate_up,
               ffn_w_down=v_ffn_w_down, final_norm_g=v_final_norm_g)

    me = 4 * lax.axis_index("x") + 2 * lax.axis_index("y") + lax.axis_index("c")
    s, d = x.shape[1], x.shape[2]
    xs = x.reshape(s, d)
    mems = mem.reshape(mem.shape[1], d)
    tgt = loss_target.reshape(s, d)
    ts = min(512, s)
    ts_ffn = min(256, s)
    row = lambda a: a.reshape(1, -1)

    conv_w_pad = jnp.pad(conv_w, ((0, CONV_PAD - CONV_K), (0, 128 - conv_w.shape[1])))
    first_level = (SIBLING,) + OTHER_CHIPS
    ex_first = _exchange_start([_bf(w_in), conv_w_pad], [False] * 2, "gather_start_in", first_level, place_own=True)
    behind = lambda t: _bf(t + ex_first[4][0:1, 0:1])
    shards = [behind(xa_wkv), behind(w_out), behind(xa_wq), behind(xa_wo), behind(ffn_w_gate_up.T), behind(ffn_w_down)]
    ex_rest = _exchange_start(shards, [False] * len(shards), "gather_start_rest", first_level, place_own=True)
    g_send, g_recv, g_src, g_land = (list(ex_first[i]) + list(ex_rest[i]) for i in range(4))
    g_tok = ex_rest[4]

    def arrived(idx, after, name):
        pick = lambda seq: [seq[i] for i in idx]
        return _wait_and_forward(pick(g_send), pick(g_recv), pick(g_src), pick(g_land), after, "gather_pass_" + name,
                                 first_level)

    def complete(handle, after, name):
        return _forward_wait(handle[0], handle[1], handle[2], after, "forward_wait_" + name)

    bst = gm_b_s.T
    wst = jnp.swapaxes(gm_w_s, 1, 2)

    h_in = arrived((0, 1), g_tok, "in")
    w_in3, conv_w8 = complete(h_in, h_in[3], "in")
    conv_w_f = conv_w8[:, :CONV_K, :conv_w.shape[1]].transpose(1, 0, 2).reshape(CONV_K, -1)
    cw = conv_w_f.shape[1]
    z, a = _fwd_in(xs, row(norm_mix_g), w_in3, row(b_in), ts)
    h_out = arrived((2, 3), z, "out")
    c = _conv_fwd(a, conv_w_f + h_out[3][0:1, 0:1], row(conv_b))
    h_xa = arrived((4, 5), c, "xa")
    wkv3, w_out3 = complete(h_out, h_xa[3], "out")
    w_out_f = w_out3.reshape(-1, d)
    kv, mn = _kv_proj(mems, row(mem_norm_g), wkv3)
    h1 = _fwd_out(xs, c, z, row(conv_ln_g), row(conv_ln_b), row(gm_ln_g), row(gm_ln_b), gm_w_s, bst, w_out_f, ts)
    h_gut = arrived((6,), h1, "gate_up")
    wq3, wo3 = complete(h_xa, h_gut[3], "xa")
    wq_f = wq3.reshape(-1, d)
    wo_f = wo3.reshape(-1, d)
    h2, qb, ob = _fwd_xa(h1, row(norm_xa_g), wq_f, kv, wo_f, ts)
    h_down = arrived((7,), h2, "down")
    (wgut3,) = complete(h_gut, h_down[3], "gate_up")
    (wdown3,) = complete(h_down, wgut3, "down")
    wgut_f = wgut3.reshape(-1, d)
    wdown_f = wdown3.reshape(-1, d)
    dh3, dh3b, gu, hn2, loss_p, d_final_g = _fwd_ffn(h2, row(norm_ffn_g), wgut_f, wdown_f, row(final_norm_g), tgt,
                                                     ts_ffn)

    blocks = lambda m: m.reshape(N_DEV, -1, d)
    tok = lambda ex: ex[4][0:1, 0:1]
    me1 = me.astype(jnp.int32).reshape(1)

    def chip_sums(pair, after, tag):
        parts, handed = _pair_wait(pair[0], pair[1], pair[2], pair[3], after, "pair_wait_" + tag)
        return [_chip_sum(p, h, me1, "chip_sum_%s_%d" % (tag, i)) for i, (p, h) in enumerate(zip(parts, handed))]

    dh2, act, dgu, d_ffn_g = _bwd_ffn(h2, dh3, gu, row(norm_ffn_g), wgut_f, wdown_f, ts_ffn)
    dwgut = _tn_matmul(dgu, hn2, 1408, "dw_gate_up")
    dwdown = _tn_matmul(act, dh3b, 1408, "dw_down")
    pair_ffn = _pair_start([blocks(dwgut), blocks(dwdown)], "pair_start_ffn")
    dh1, dwq, dwo, dkv, d_xa_g = _bwd_xa(h1, dh2, qb, ob, row(norm_xa_g) + tok(pair_ffn), wq_f, wo_f, kv, ts)
    ex_ffn = _exchange_start(chip_sums(pair_ffn, dh1, "ffn"), [True] * 2, "scatter_start_ffn", OTHER_CHIPS)
    dwkv3, d_mem_g = _bwd_kv(dkv, mn, mems, row(mem_norm_g) + tok(ex_ffn), wkv3)
    pair_xa = _pair_start([blocks(dwq), blocks(dwo), dwkv3], "pair_start_xa")
    (dwout, dc, dzuv, dws, dbst, d_cln_g, d_cln_b, d_gln_g, d_gln_b, dbin_uv) = _bwd_out(
        dh1, c, z, row(conv_ln_g) + tok(pair_xa), row(conv_ln_b), row(gm_ln_g), row(gm_ln_b), gm_w_s, wst, bst,
        w_out_f, ts)
    early = dict(b_in_uv=dbin_uv, conv_ln_g=d_cln_g, conv_ln_b=d_cln_b, gm_ln_g=d_gln_g, gm_ln_b=d_gln_b,
                 gm_b_s=dbst.T, norm_xa_g=d_xa_g, mem_norm_g=d_mem_g, norm_ffn_g=d_ffn_g, final_norm_g=d_final_g,
                 loss=loss_p)
    dws_b = _bf(dws).reshape(GM_HEADS * CHUNK, CHUNK)
    ex_xa = _exchange_start(chip_sums(pair_xa, dc, "xa") + [blocks(dwout), dws_b, _pack(list(early.values()))],
                            [True] * 4 + [False] * 2, "scatter_start_xa", [OTHER_CHIPS] * 3 + [ALL_PEERS] * 3)
    dx, dw_in3, dconv_w, dconv_b, dbin_ag, d_mix_g = _bwd_in(xs, dh1, dc, a, z, dzuv, row(norm_mix_g) + tok(ex_xa),
                                                             w_in3, conv_w_f, ts)
    late = dict(norm_mix_g=d_mix_g, b_in_ag=dbin_ag, conv_w=dconv_w[:CONV_K], conv_b=dconv_b)
    ex_last = _exchange_start([_pack(list(late.values())), dw_in3], [False, True], "scatter_start_in")

    grads, delta, new_m, new_v = {}, {}, {}, {}

    def waited(ex, idx, scatter, after, name, masks=ALL_PEERS):
        return _exchange_wait(*[[ex[i][k] for k in idx] for i in range(4)], scatter, after, name, masks)

    def reduced(names, srcs, lands, masks):
        for nm, src, land, mk in zip(names, srcs, lands, masks):
            view = (lambda t: t.T) if nm == "ffn_w_gate_up" else (lambda t: t)
            upd = _sum_adamw(land, src, view(w[nm]), view(mom[nm]), view(var[nm]), me1, "adamw_" + nm, mk)
            grads[nm], delta[nm], new_m[nm], new_v[nm] = (view(t) for t in upd)
        return new_v[names[-1]]

    after = reduced(("ffn_w_gate_up", "ffn_w_down"),
                    *waited(ex_ffn, (0, 1), [True] * 2, ex_last[4], "scatter_wait_ffn", OTHER_CHIPS), [OTHER_CHIPS] * 2)
    xa_masks = [OTHER_CHIPS] * 3 + [ALL_PEERS]
    after = reduced(("xa_wq", "xa_wo", "xa_wkv", "w_out"),
                    *waited(ex_xa, (0, 1, 2, 3), [True] * 4, after, "scatter_wait_xa", xa_masks), xa_masks)

    (dws_src, early_src), (dws_land, early_land) = waited(ex_xa, (4, 5), [False] * 2, after, "gather_wait_early")
    (late_src,), (late_land,) = waited(ex_last, (0,), [False], early_land, "gather_wait_late")
    dws_tot = _sum8(dws_land, dws_src, me1, "sum_gm_w_s")
    early_tot = _sum8(early_land, early_src, me1, "sum_small_early")
    late_tot = _sum8(late_land, late_src, me1, "sum_small_late")
    early_g = dict(zip(early, _unpack(early_tot, [v.shape for v in early.values()])), gm_w_s=dws_tot)
    late_g = dict(zip(late, _unpack(late_tot, [v.shape for v in late.values()])))
    loss = early_g["loss"][0, 0]
    for nm in SMALL:
        if nm == "b_in":
            g = jnp.concatenate([late_g["b_in_ag"], early_g["b_in_uv"]], axis=1)
        elif nm == "conv_w":
            g = lax.dynamic_slice_in_dim(late_g[nm], me * conv_w.shape[1], conv_w.shape[1], axis=1)
        else:
            g = late_g[nm] if nm in late_g else early_g[nm]
        grads[nm] = g.reshape(w[nm].shape)

    small = [[src[nm] for nm in SMALL] for src in (w, grads, mom, var)]
    for dst, vals in zip((delta, new_m, new_v), _adamw_many(*small, "adamw_small")):
        dst.update(zip(SMALL, vals))

    reduced(("w_in",), *waited(ex_last, (1,), [True], delta[SMALL[0]], "scatter_wait_in"), [ALL_PEERS])

    return (loss, dx.reshape(x.shape), *[grads[nm] for nm in WEIGHTS], *[delta[nm] for nm in WEIGHTS],
            *[new_m[nm] for nm in WEIGHTS], *[new_v[nm] for nm in WEIGHTS])
```

```python
import jax
import jax.numpy as jnp
from jax import lax
from jax.experimental import pallas as pl
from jax.experimental.pallas import tpu as pltpu

F32 = jnp.float32
BF16 = jnp.bfloat16
SDS = jax.ShapeDtypeStruct

N_DEV = 8
RMS_EPS = 1e-6
LN_EPS = 1e-5
CONV_K = 31
CONV_PAD = 32
CHUNK = 128
GM_HEADS = 8
XA_HEADS = 4
XA_DH = 256
GELU_K0 = 0.7978845608028654
GELU_K1 = 0.044715
ADAM_LR = 0.001
ADAM_B1 = 0.9
ADAM_B2 = 0.999
ADAM_EPS = 1e-08
ADAM_WD = 0.01
ADAM_STEP = 10
VMEM_LIMIT = 60 * 1024 * 1024

NN = (((1,), (0,)), ((), ()))
NT = (((1,), (1,)), ((), ()))
TN = (((0,), (0,)), ((), ()))


def _dot(a, b, dims=NN):
    return lax.dot_general(a, b, dims, preferred_element_type=F32)


def _bf(x):
    return x.astype(BF16)


def _cparams(*sem):
    return pltpu.CompilerParams(dimension_semantics=tuple(sem) if sem else None, vmem_limit_bytes=VMEM_LIMIT)


def _row(ts, w, col=0):
    return pl.BlockSpec((ts, w), lambda i: (i, col))


def _const(shape):
    nd = len(shape)
    return pl.BlockSpec(shape, lambda i: (0,) * nd, pipeline_mode=pl.Buffered(1))


def _acc(shape):
    nd = len(shape)
    return pl.BlockSpec(shape, lambda i: (0,) * nd)


def _rms_fwd(x, g):
    r = lax.rsqrt(jnp.mean(x * x, axis=-1, keepdims=True) + RMS_EPS)
    xh = x * r
    return xh * g, xh, r


def _rms_bwd(dy, xh, r, g):
    gdy = dy * g
    dx = r * (gdy - xh * jnp.mean(gdy * xh, axis=-1, keepdims=True))
    dg = jnp.sum(dy * xh, axis=0, keepdims=True)
    return dx, dg


def _ln_fwd(x, g, b):
    mu = jnp.mean(x, axis=-1, keepdims=True)
    xc = x - mu
    rs = lax.rsqrt(jnp.mean(xc * xc, axis=-1, keepdims=True) + LN_EPS)
    xh = xc * rs
    return xh * g + b, xh, rs


def _ln_bwd(dy, xh, rs, g):
    dxh = dy * g
    dx = rs * (dxh - jnp.mean(dxh, axis=-1, keepdims=True) - xh * jnp.mean(dxh * xh, axis=-1, keepdims=True))
    return dx, jnp.sum(dy * xh, axis=0, keepdims=True), jnp.sum(dy, axis=0, keepdims=True)


def _gelu(x):
    t = jnp.tanh(GELU_K0 * (x + GELU_K1 * (x * x * x)))
    return 0.5 * x * (1.0 + t), t


def _gelu_grad(x, t):
    return 0.5 * (1.0 + t) + 0.5 * x * (1.0 - t * t) * (GELU_K0 * (1.0 + 3.0 * GELU_K1 * x * x))


def _silu_grad(x, sg):
    return sg * (1.0 + x * (1.0 - sg))


def _by_residue(offsets):
    groups = {}
    for off in offsets:
        groups.setdefault(off % 8, []).append(off)
    return [(res, sorted(offs)) for res, offs in sorted(groups.items())]


def _accumulate(ref, val):
    @pl.when(pl.program_id(0) == 0)
    def _():
        ref[...] = jnp.zeros_like(ref)
    ref[...] += val


def _mix_masks():
    row = lax.broadcasted_iota(jnp.int32, (CHUNK, CHUNK), 0)
    col = lax.broadcasted_iota(jnp.int32, (CHUNK, CHUNK), 1)
    return row >= col, row <= col, col < (CHUNK // 2)


def _mix_fwd(vb, ws_ref, bst_ref, mixed_scr, ts):
    tril, _, lo = _mix_masks()
    for j in range(GM_HEADS // 2):
        w0 = _bf(jnp.where(tril, ws_ref[2 * j], 0.0))
        w1 = _bf(jnp.where(tril, ws_ref[2 * j + 1], 0.0))
        bias = jnp.where(lo, bst_ref[:, 2 * j:2 * j + 1], bst_ref[:, 2 * j + 1:2 * j + 2])
        for n in range(ts // CHUNK):
            v = vb[n * CHUNK:(n + 1) * CHUNK, j * 128:(j + 1) * 128]
            mixed_scr[n * CHUNK:(n + 1) * CHUNK, j * 128:(j + 1) * 128] = jnp.where(lo, _dot(w0, v), _dot(w1, v)) + bias


def _peer_of(mask):
    x, y, c = lax.axis_index("x"), lax.axis_index("y"), lax.axis_index("c")
    px = 1 - x if (mask >> 2) & 1 else x
    py = 1 - y if (mask >> 1) & 1 else y
    pc = 1 - c if mask & 1 else c
    return (px, py, pc), 4 * px + 2 * py + pc


ALL_PEERS = tuple(range(1, N_DEV))
OTHER_CHIPS = (2, 4, 6)
CHIPS = (0,) + OTHER_CHIPS
SIBLING = 1


def _split_copy(src_ref, land_ref, send_sem, recv_sem, mask, slot, scatter, outgoing):
    x, y, c = lax.axis_index("x"), lax.axis_index("y"), lax.axis_index("c")
    me = 4 * x + 2 * y + c
    peer, pidx = _peer_of(mask)
    return pltpu.make_async_remote_copy(
        src_ref=src_ref.at[pidx] if scatter else src_ref,
        dst_ref=land_ref.at[me if outgoing else pidx],
        send_sem=send_sem.at[slot], recv_sem=recv_sem.at[slot],
        device_id=peer, device_id_type=pl.DeviceIdType.MESH)


_HBM = pl.BlockSpec(memory_space=pltpu.HBM)
_SEM = pl.BlockSpec(memory_space=pltpu.SEMAPHORE)
_EFFECT = pltpu.SideEffectType.DATAFLOW_SIDE_EFFECTING


def _per_array(masks, n):
    return [tuple(masks)] * n if isinstance(masks[0], int) else [tuple(m) for m in masks]


def _exchange_start(srcs, scatter, name, masks=ALL_PEERS, own_slot=None):
    n = len(srcs)
    masks = _per_array(masks, n)
    lands = [lax.empty((N_DEV,) + tuple(s.shape[1:] if sc else s.shape), s.dtype) for s, sc in zip(srcs, scatter)]
    if own_slot is not None:
        lands = [land if sc else _with_own(land, s, own_slot) for land, s, sc in zip(lands, srcs, scatter)]
    lands = [pltpu.with_memory_space_constraint(land, pltpu.HBM) for land in lands]
    srcs = [pltpu.with_memory_space_constraint(s, pltpu.HBM) for s in srcs]

    def body(*refs):
        src_refs, land_refs = refs[:n], refs[n:2 * n]
        send_sems, recv_sems = refs[2 * n:3 * n], refs[3 * n:4 * n]
        token = refs[-1]
        for k in range(n):
            for slot, mask in enumerate(masks[k]):
                _split_copy(src_refs[k], land_refs[k], send_sems[k], recv_sems[k], mask, slot, scatter[k], True).start()
        token[...] = jnp.zeros_like(token)

    sems = [pltpu.SemaphoreType.DMA((len(m),)) for m in masks]
    out = pl.pallas_call(
        body, name=name,
        out_shape=tuple(sems + sems + [pltpu.HBM(s.shape, s.dtype) for s in srcs]
                        + [pltpu.HBM(l.shape, l.dtype) for l in lands] + [SDS((8, 128), F32)]),
        in_specs=[_HBM] * (2 * n),
        out_specs=tuple([_SEM] * (2 * n) + [_HBM] * (2 * n) + [pl.BlockSpec(memory_space=pltpu.VMEM)]),
        input_output_aliases={i: 2 * n + i for i in range(2 * n)},
        compiler_params=pltpu.CompilerParams(has_side_effects=_EFFECT),
    )(*srcs, *lands)
    return out[:n], out[n:2 * n], out[2 * n:3 * n], out[3 * n:4 * n], out[-1]


def _exchange_wait(send_sems, recv_sems, srcs_thru, lands_thru, scatter, after, name, masks=ALL_PEERS):
    n = len(srcs_thru)
    masks = _per_array(masks, n)

    def body(*refs):
        src_refs, land_refs = refs[:n], refs[n:2 * n]
        send_refs, recv_refs = refs[2 * n:3 * n], refs[3 * n:4 * n]
        for k in range(n):
            for slot, mask in enumerate(masks[k]):
                args = (src_refs[k], land_refs[k], send_refs[k], recv_refs[k], mask, slot, scatter[k])
                _split_copy(*args, True).wait_send()
                _split_copy(*args, False).wait_recv()

    out = pl.pallas_call(
        body, name=name,
        out_shape=tuple([pltpu.HBM(s.shape, s.dtype) for s in srcs_thru]
                        + [pltpu.HBM(l.shape, l.dtype) for l in lands_thru]),
        in_specs=[_HBM] * (2 * n) + [_SEM] * (2 * n) + [pl.BlockSpec(memory_space=pl.ANY)],
        out_specs=tuple([_HBM] * (2 * n)),
        input_output_aliases={i: i for i in range(2 * n)},
        compiler_params=pltpu.CompilerParams(has_side_effects=_EFFECT),
    )(*srcs_thru, *lands_thru, *send_sems, *recv_sems, after)
    return out[:n], out[n:]


def _wait_and_forward(send_sems, recv_sems, srcs_thru, lands_thru, after, name, masks):
    n = len(lands_thru)

    def body(*refs):
        src_refs, land_refs = refs[:n], refs[n:2 * n]
        send_refs, recv_refs = refs[2 * n:3 * n], refs[3 * n:4 * n]
        outs = refs[4 * n + 1:]
        fsend, frecv, token = outs[2 * n:3 * n], outs[3 * n:4 * n], outs[-1]
        for k in range(n):
            for slot, mask in enumerate(masks):
                args = (src_refs[k], land_refs[k], send_refs[k], recv_refs[k], mask, slot, False)
                _split_copy(*args, True).wait_send()
                _split_copy(*args, False).wait_recv()
        sibling, _ = _peer_of(SIBLING)
        for k in range(n):
            for slot, mask in enumerate(OTHER_CHIPS):
                _, mine = _peer_of(mask)
                pltpu.make_async_remote_copy(
                    src_ref=land_refs[k].at[mine], dst_ref=land_refs[k].at[mine], send_sem=fsend[k].at[slot],
                    recv_sem=frecv[k].at[slot], device_id=sibling, device_id_type=pl.DeviceIdType.MESH).start()
        token[...] = jnp.zeros_like(token)

    sem = pltpu.SemaphoreType.DMA((len(OTHER_CHIPS),))
    out = pl.pallas_call(
        body, name=name,
        out_shape=tuple([pltpu.HBM(s.shape, s.dtype) for s in srcs_thru] + [pltpu.HBM(l.shape, l.dtype) for l in lands_thru]
                        + [sem] * (2 * n) + [SDS((8, 128), F32)]),
        in_specs=[_HBM] * (2 * n) + [_SEM] * (2 * n) + [pl.BlockSpec(memory_space=pl.ANY)],
        out_specs=tuple([_HBM] * (2 * n) + [_SEM] * (2 * n) + [pl.BlockSpec(memory_space=pltpu.VMEM)]),
        input_output_aliases={i: i for i in range(2 * n)},
        compiler_params=pltpu.CompilerParams(has_side_effects=_EFFECT),
    )(*srcs_thru, *lands_thru, *send_sems, *recv_sems, after)
    return out[2 * n:3 * n], out[3 * n:4 * n], out[n:2 * n], out[-1]


def _forward_wait(send_sems, recv_sems, lands, after, name):
    n = len(lands)

    def body(*refs):
        land_refs, send_refs, recv_refs = refs[:n], refs[n:2 * n], refs[2 * n:3 * n]
        sibling, _ = _peer_of(SIBLING)
        for k in range(n):
            for slot, mask in enumerate(OTHER_CHIPS):
                _, mine = _peer_of(mask)
                _, theirs = _peer_of(mask | SIBLING)
                for block, wait in ((mine, "wait_send"), (theirs, "wait_recv")):
                    getattr(pltpu.make_async_remote_copy(
                        src_ref=land_refs[k].at[block], dst_ref=land_refs[k].at[block],
                        send_sem=send_refs[k].at[slot], recv_sem=recv_refs[k].at[slot], device_id=sibling,
                        device_id_type=pl.DeviceIdType.MESH), wait)()

    return pl.pallas_call(
        body, name=name, out_shape=tuple(pltpu.HBM(l.shape, l.dtype) for l in lands),
        in_specs=[_HBM] * n + [_SEM] * (2 * n) + [pl.BlockSpec(memory_space=pl.ANY)],
        out_specs=tuple([_HBM] * n), input_output_aliases={i: i for i in range(n)},
        compiler_params=pltpu.CompilerParams(has_side_effects=_EFFECT),
    )(*lands, *send_sems, *recv_sems, after)


def _pair_copy(parts_ref, land_ref, send_sem, recv_sem, slot, chip, outgoing):
    sibling, _ = _peer_of(SIBLING)
    _, block = _peer_of(chip | SIBLING if outgoing else chip)
    return pltpu.make_async_remote_copy(
        src_ref=parts_ref.at[block], dst_ref=land_ref.at[block], send_sem=send_sem.at[slot], recv_sem=recv_sem.at[slot],
        device_id=sibling, device_id_type=pl.DeviceIdType.MESH)


def _pair_start(parts, name):
    n = len(parts)
    lands = [pltpu.with_memory_space_constraint(lax.empty(p.shape, p.dtype), pltpu.HBM) for p in parts]
    parts = [pltpu.with_memory_space_constraint(p, pltpu.HBM) for p in parts]

    def body(*refs):
        part_refs, land_refs = refs[:n], refs[n:2 * n]
        send_sems, recv_sems, token = refs[2 * n:3 * n], refs[3 * n:4 * n], refs[-1]
        for k in range(n):
            for slot, chip in enumerate(CHIPS):
                _pair_copy(part_refs[k], land_refs[k], send_sems[k], recv_sems[k], slot, chip, True).start()
        token[...] = jnp.zeros_like(token)

    sem = pltpu.SemaphoreType.DMA((len(CHIPS),))
    out = pl.pallas_call(
        body, name=name,
        out_shape=tuple([sem] * (2 * n) + [pltpu.HBM(p.shape, p.dtype) for p in parts + lands] + [SDS((8, 128), F32)]),
        in_specs=[_HBM] * (2 * n),
        out_specs=tuple([_SEM] * (2 * n) + [_HBM] * (2 * n) + [pl.BlockSpec(memory_space=pltpu.VMEM)]),
        input_output_aliases={i: 2 * n + i for i in range(2 * n)},
        compiler_params=pltpu.CompilerParams(has_side_effects=_EFFECT),
    )(*parts, *lands)
    return out[:n], out[n:2 * n], out[2 * n:3 * n], out[3 * n:4 * n], out[-1]


def _pair_wait(send_sems, recv_sems, parts_thru, lands_thru, after, name):
    n = len(parts_thru)

    def body(*refs):
        part_refs, land_refs = refs[:n], refs[n:2 * n]
        send_refs, recv_refs = refs[2 * n:3 * n], refs[3 * n:4 * n]
        for k in range(n):
            for slot, chip in enumerate(CHIPS):
                args = (part_refs[k], land_refs[k], send_refs[k], recv_refs[k], slot, chip)
                _pair_copy(*args, True).wait_send()
                _pair_copy(*args, False).wait_recv()

    out = pl.pallas_call(
        body, name=name,
        out_shape=tuple(pltpu.HBM(p.shape, p.dtype) for p in list(parts_thru) + list(lands_thru)),
        in_specs=[_HBM] * (2 * n) + [_SEM] * (2 * n) + [pl.BlockSpec(memory_space=pl.ANY)],
        out_specs=tuple([_HBM] * (2 * n)), input_output_aliases={i: i for i in range(2 * n)},
        compiler_params=pltpu.CompilerParams(has_side_effects=_EFFECT),
    )(*parts_thru, *lands_thru, *send_sems, *recv_sems, after)
    return out[:n], out[n:]


def _with_own(landed, own, me):
    return lax.dynamic_update_slice_in_dim(landed, own[None], me, axis=0)


def _kv_proj(mem, g_mem, wkv3):
    m = mem.shape[0]

    def body(mem_ref, g_ref, w_ref, kv_ref, mn_ref):
        y, _, _ = _rms_fwd(mem_ref[...], g_ref[...])
        yb = _bf(y)
        mn_ref[...] = yb
        for b in range(N_DEV):
            kv_ref[:, 256 * b:256 * (b + 1)] = _dot(yb, w_ref[b])

    return pl.pallas_call(body, name="kv_proj", out_shape=(SDS((m, 2048), F32), SDS(mem.shape, BF16)),
                          compiler_params=_cparams())(mem, g_mem, wkv3)


def _fwd_in(x, g_mix, w_in3, b_in, ts):
    s, d = x.shape

    def body(x_ref, g_ref, w_ref, b_ref, z_ref, a_ref):
        hn, _, _ = _rms_fwd(x_ref[...], g_ref[...])
        hb = _bf(hn)
        for b in range(N_DEV):
            z_ref[:, 256 * b:256 * (b + 1)] = _dot(hb, w_ref[b]) + b_ref[:, 256 * b:256 * (b + 1)]
        a_ref[...] = z_ref[:, 0:512] * jax.nn.sigmoid(z_ref[:, 512:1024])

    return pl.pallas_call(
        body, name="fwd_in", grid=(s // ts,),
        in_specs=[_row(ts, d), _const(g_mix.shape), _const(w_in3.shape), _const(b_in.shape)],
        out_specs=(_row(ts, 2048), _row(ts, 512)),
        out_shape=(SDS((s, 2048), F32), SDS((s, 512), F32)),
        compiler_params=_cparams("arbitrary"))(x, g_mix, w_in3, b_in)


def _conv_fwd(a, w, b):
    s, cw = a.shape
    rc = 256 if s % 256 == 0 else 128

    def body(a_ref, w_ref, b_ref, c_ref, pad):
        pad[0:CONV_PAD, :] = jnp.zeros((CONV_PAD, 128), F32)
        pad[CONV_PAD:, :] = a_ref[...]

        def chunk(i, carry):
            r0 = pl.multiple_of(i * rc, rc)
            acc = jnp.zeros((rc, 128), F32) + b_ref[...]
            for res, offs in _by_residue(range(CONV_PAD - CONV_K + 1, CONV_PAD + 1)):
                shifted = pad[pl.ds(r0 + res, rc + offs[-1] - res), :]
                for off in offs:
                    k = off - (CONV_PAD - CONV_K + 1)
                    acc = acc + w_ref[k:k + 1, :] * shifted[off - res:off - res + rc, :]
            c_ref[pl.ds(r0, rc), :] = acc
            return carry

        lax.fori_loop(0, s // rc, chunk, 0)

    blk = lambda r: pl.BlockSpec((r, 128), lambda j: (0, j))
    return pl.pallas_call(
        body, name="conv_fwd", grid=(cw // 128,),
        in_specs=[blk(s), blk(CONV_K), blk(1)], out_specs=blk(s), out_shape=SDS((s, cw), F32),
        scratch_shapes=[pltpu.VMEM((s + CONV_PAD, 128), F32)],
        compiler_params=_cparams("arbitrary"))(a, w, b)


def _fwd_out(x, c, z, cln_g, cln_b, gln_g, gln_b, ws, bst, w_out, ts):
    s, d = x.shape

    def body(x_ref, c_ref, zuv_ref, clg, clb, glg, glb, ws_ref, bst_ref, wo_ref, h1_ref, mixed_scr):
        cl, _, _ = _ln_fwd(c_ref[...], clg[...], clb[...])
        co = cl * jax.nn.sigmoid(cl)
        u, _ = _gelu(zuv_ref[:, 0:512])
        vg, _ = _gelu(zuv_ref[:, 512:1024])
        vln, _, _ = _ln_fwd(vg, glg[...], glb[...])
        _mix_fwd(_bf(vln), ws_ref, bst_ref, mixed_scr, ts)
        gm = u * mixed_scr[...]
        h1_ref[...] = x_ref[...] + _dot(_bf(co), wo_ref[0:512, :]) + _dot(_bf(gm), wo_ref[512:1024, :])

    return pl.pallas_call(
        body, name="fwd_out", grid=(s // ts,),
        in_specs=[_row(ts, d), _row(ts, 512), _row(ts, 1024, 1), _const(cln_g.shape), _const(cln_b.shape),
                  _const(gln_g.shape), _const(gln_b.shape), _const(ws.shape), _const(bst.shape), _const(w_out.shape)],
        out_specs=_row(ts, d), out_shape=SDS((s, d), F32),
        scratch_shapes=[pltpu.VMEM((ts, 512), F32)],
        compiler_params=_cparams("arbitrary"))(x, c, z, cln_g, cln_b, gln_g, gln_b, ws, bst, w_out)


def _softmax_rows(sc):
    m = jnp.max(sc, axis=-1, keepdims=True)
    e = jnp.exp(sc - m)
    return e / jnp.sum(e, axis=-1, keepdims=True)


def _fwd_xa(h1, g_xa, wq, kv, wo, ts):
    s, d = h1.shape
    scale = XA_DH ** -0.5

    def body(h_ref, g_ref, wq_ref, kv_ref, wo_ref, h2_ref, q_ref, o_ref, o_scr):
        hn, _, _ = _rms_fwd(h_ref[...], g_ref[...])
        q_ref[...] = _bf(_dot(_bf(hn), wq_ref[...]))
        for h in range(XA_HEADS):
            qh = q_ref[:, XA_DH * h:XA_DH * (h + 1)]
            kh = _bf(kv_ref[:, XA_DH * h:XA_DH * (h + 1)])
            vh = _bf(kv_ref[:, d + XA_DH * h:d + XA_DH * (h + 1)])
            p = _softmax_rows(_dot(qh, kh, NT) * scale)
            o_scr[:, XA_DH * h:XA_DH * (h + 1)] = _dot(_bf(p), vh)
        o_ref[...] = _bf(o_scr[...])
        h2_ref[...] = h_ref[...] + _dot(o_ref[...], wo_ref[...])

    return pl.pallas_call(
        body, name="fwd_xa", grid=(s // ts,),
        in_specs=[_row(ts, d), _const(g_xa.shape), _const(wq.shape), _const(kv.shape), _const(wo.shape)],
        out_specs=(_row(ts, d), _row(ts, d), _row(ts, d)),
        out_shape=(SDS((s, d), F32), SDS((s, d), BF16), SDS((s, d), BF16)),
        scratch_shapes=[pltpu.VMEM((ts, d), F32)],
        compiler_params=_cparams("arbitrary"))(h1, g_xa, wq, kv, wo)


def _fwd_ffn(h2, g_ffn, wgut, wdown, g_final, target, ts):
    s, d = h2.shape
    hid = wdown.shape[0]
    hc = hid // 2

    def body(h_ref, g_ref, wgu_ref, wd_ref, gf_ref, t_ref, dh3_ref, dh3b_ref, gu_ref, hn_ref, loss_ref, dgf_ref):
        hn, _, _ = _rms_fwd(h_ref[...], g_ref[...])
        hb = _bf(hn)
        hn_ref[...] = hb
        h3 = h_ref[...]
        for n in range(2):
            g = _dot(hb, wgu_ref[hc * n:hc * (n + 1), :], NT)
            u = _dot(hb, wgu_ref[hid + hc * n:hid + hc * (n + 1), :], NT)
            gu_ref[:, hc * n:hc * (n + 1)] = g
            gu_ref[:, hid + hc * n:hid + hc * (n + 1)] = u
            act = g * jax.nn.sigmoid(g) * u
            h3 = h3 + _dot(_bf(act), wd_ref[hc * n:hc * (n + 1), :])
        y, xh, r = _rms_fwd(h3, gf_ref[...])
        diff = y - t_ref[...]
        part = 0.5 * jnp.sum(jnp.mean(diff * diff, axis=-1, keepdims=True), axis=0, keepdims=True)
        _accumulate(loss_ref, jnp.zeros(loss_ref.shape, F32) + part)
        dh3, dgf = _rms_bwd(diff * (1.0 / d), xh, r, gf_ref[...])
        dh3_ref[...] = dh3
        dh3b_ref[...] = _bf(dh3)
        _accumulate(dgf_ref, dgf)

    return pl.pallas_call(
        body, name="fwd_ffn", grid=(s // ts,),
        in_specs=[_row(ts, d), _const(g_ffn.shape), _const(wgut.shape), _const(wdown.shape), _const(g_final.shape),
                  _row(ts, d)],
        out_specs=(_row(ts, d), _row(ts, d), _row(ts, 2 * hid), _row(ts, d), _acc((1, 128)), _acc((1, d))),
        out_shape=(SDS((s, d), F32), SDS((s, d), BF16), SDS((s, 2 * hid), F32), SDS((s, d), BF16), SDS((1, 128), F32),
                   SDS((1, d), F32)),
        compiler_params=_cparams("arbitrary"))(h2, g_ffn, wgut, wdown, g_final, target)


def _bwd_ffn(h2, dh3, gu, g_ffn, wgut, wdown, ts):
    s, d = h2.shape
    hid = wdown.shape[0]
    hc = hid // 2

    def body(h_ref, dh3_ref, gu_ref, g_ref, wgu_ref, wd_ref, dh2_ref, act_ref, dgu_ref, dg_ref):
        _, xh, r = _rms_fwd(h_ref[...], g_ref[...])
        db = _bf(dh3_ref[...])
        dhn = jnp.zeros((ts, d), F32)
        for n in range(2):
            wg = wgu_ref[hc * n:hc * (n + 1), :]
            wu = wgu_ref[hid + hc * n:hid + hc * (n + 1), :]
            g = gu_ref[:, hc * n:hc * (n + 1)]
            u = gu_ref[:, hid + hc * n:hid + hc * (n + 1)]
            sg = jax.nn.sigmoid(g)
            sl = g * sg
            act_ref[:, hc * n:hc * (n + 1)] = _bf(sl * u)
            dact = _dot(db, wd_ref[hc * n:hc * (n + 1), :], NT)
            dgb = _bf(dact * u * _silu_grad(g, sg))
            dub = _bf(dact * sl)
            dgu_ref[:, hc * n:hc * (n + 1)] = dgb
            dgu_ref[:, hid + hc * n:hid + hc * (n + 1)] = dub
            dhn = dhn + _dot(dgb, wg) + _dot(dub, wu)
        dx, dg = _rms_bwd(dhn, xh, r, g_ref[...])
        dh2_ref[...] = dh3_ref[...] + dx
        _accumulate(dg_ref, dg)

    return pl.pallas_call(
        body, name="bwd_ffn", grid=(s // ts,),
        in_specs=[_row(ts, d), _row(ts, d), _row(ts, 2 * hid), _const(g_ffn.shape), _const(wgut.shape),
                  _const(wdown.shape)],
        out_specs=(_row(ts, d), _row(ts, hid), _row(ts, 2 * hid), _acc((1, d))),
        out_shape=(SDS((s, d), F32), SDS((s, hid), BF16), SDS((s, 2 * hid), BF16), SDS((1, d), F32)),
        compiler_params=_cparams("arbitrary"))(h2, dh3, gu, g_ffn, wgut, wdown)


def _bwd_xa(h1, dh2, qb, ob, g_xa, wq, wo, kv, ts):
    s, d = h1.shape
    scale = XA_DH ** -0.5

    def body(h_ref, dh2_ref, q_ref, o_ref, g_ref, wq_ref, wo_ref, kv_ref, dh1_ref, dwq_ref, dwo_ref, dkv_ref, dg_ref,
             dq_scr, accq, acco):
        hn, xh, r = _rms_fwd(h_ref[...], g_ref[...])
        hb = _bf(hn)
        dh2b = _bf(dh2_ref[...])
        do = _dot(dh2b, wo_ref[...], NT)

        @pl.when(pl.program_id(0) == 0)
        def _():
            dkv_ref[...] = jnp.zeros_like(dkv_ref)
            accq[...] = jnp.zeros_like(accq)
            acco[...] = jnp.zeros_like(acco)

        for h in range(XA_HEADS):
            lo, hi = XA_DH * h, XA_DH * (h + 1)
            qh = q_ref[:, lo:hi]
            kh = _bf(kv_ref[:, lo:hi])
            vh = _bf(kv_ref[:, d + lo:d + hi])
            p = _softmax_rows(_dot(qh, kh, NT) * scale)
            pb = _bf(p)
            doh = _bf(do[:, lo:hi])
            dp = _dot(doh, vh, NT)
            ds = p * (dp - jnp.sum(p * dp, axis=-1, keepdims=True)) * scale
            dsb = _bf(ds)
            dq_scr[:, lo:hi] = _dot(dsb, kh)
            dkv_ref[:, lo:hi] += _dot(dsb, qh, TN)
            dkv_ref[:, d + lo:d + hi] += _dot(pb, doh, TN)
        dqb = _bf(dq_scr[...])
        accq[...] += _dot(hb, dqb, TN)
        acco[...] += _dot(o_ref[...], dh2b, TN)
        dx, dg = _rms_bwd(_dot(dqb, wq_ref[...], NT), xh, r, g_ref[...])
        dh1_ref[...] = dh2_ref[...] + dx
        _accumulate(dg_ref, dg)

        @pl.when(pl.program_id(0) == pl.num_programs(0) - 1)
        def _():
            dwq_ref[...] = _bf(accq[...])
            dwo_ref[...] = _bf(acco[...])

    return pl.pallas_call(
        body, name="bwd_xa", grid=(s // ts,),
        in_specs=[_row(ts, d), _row(ts, d), _row(ts, d), _row(ts, d), _const(g_xa.shape), _const(wq.shape),
                  _const(wo.shape), _const(kv.shape)],
        out_specs=(_row(ts, d), _acc((d, d)), _acc((d, d)), _acc(kv.shape), _acc((1, d))),
        out_shape=(SDS((s, d), F32), SDS((d, d), BF16), SDS((d, d), BF16), SDS(kv.shape, F32), SDS((1, d), F32)),
        scratch_shapes=[pltpu.VMEM((ts, d), F32), pltpu.VMEM((d, d), F32), pltpu.VMEM((d, d), F32)],
        compiler_params=_cparams("arbitrary"))(h1, dh2, qb, ob, g_xa, wq, wo, kv)


def _bwd_kv(dkv, mn, mem, g_mem, wkv3):
    d = mem.shape[1]

    def body(dkv_ref, mn_ref, mem_ref, g_ref, w_ref, dw_ref, dg_ref):
        dkvb = _bf(dkv_ref[...])
        dmn = jnp.zeros(mem_ref.shape, F32)
        for b in range(N_DEV):
            blk = dkvb[:, 256 * b:256 * (b + 1)]
            dmn = dmn + _dot(blk, w_ref[b], NT)
            dw_ref[b] = _bf(_dot(mn_ref[...], blk, TN))
        _, xh, r = _rms_fwd(mem_ref[...], g_ref[...])
        _, dg = _rms_bwd(dmn, xh, r, g_ref[...])
        dg_ref[...] = dg

    return pl.pallas_call(body, name="bwd_kv", out_shape=(SDS(wkv3.shape, BF16), SDS((1, d), F32)),
                          compiler_params=_cparams())(dkv, mn, mem, g_mem, wkv3)


def _bwd_out(dh1, c, z, cln_g, cln_b, gln_g, gln_b, ws, wst, bst, w_out, ts):
    s, d = dh1.shape
    nh = GM_HEADS

    def body(dh1_ref, c_ref, zuv_ref, clg, clb, glg, glb, ws_ref, wst_ref, bst_ref, wo_ref,
             dwo_ref, dc_ref, dzuv_ref, dws_ref, dbst_ref, dclg_ref, dclb_ref, dglg_ref, dglb_ref, dbin_ref,
             mixed_scr, dv_scr, acc):
        cl, chat, crs = _ln_fwd(c_ref[...], clg[...], clb[...])
        sg = jax.nn.sigmoid(cl)
        zu = zuv_ref[:, 0:512]
        zv = zuv_ref[:, 512:1024]
        u, tu = _gelu(zu)
        vg, tv = _gelu(zv)
        vln, vhat, vrs = _ln_fwd(vg, glg[...], glb[...])
        vb = _bf(vln)
        _mix_fwd(vb, ws_ref, bst_ref, mixed_scr, ts)
        mixed = mixed_scr[...]
        dh1b = _bf(dh1_ref[...])
        dcat = _dot(dh1b, wo_ref[...], NT)
        dgm = dcat[:, 512:1024]
        dc, dclg, dclb = _ln_bwd(dcat[:, 0:512] * _silu_grad(cl, sg), chat, crs, clg[...])
        dc_ref[...] = dc
        dzu = dgm * mixed * _gelu_grad(zu, tu)
        dm = dgm * u

        @pl.when(pl.program_id(0) == 0)
        def _():
            dws_ref[...] = jnp.zeros_like(dws_ref)
            dbst_ref[...] = jnp.zeros_like(dbst_ref)
            acc[...] = jnp.zeros_like(acc)

        acc[0:512, :] += _dot(_bf(cl * sg), dh1b, TN)
        acc[512:1024, :] += _dot(_bf(u * mixed), dh1b, TN)

        @pl.when(pl.program_id(0) == pl.num_programs(0) - 1)
        def _():
            dwo_ref[...] = _bf(acc[...])

        tril, triu, lo = _mix_masks()
        head = lax.broadcasted_iota(jnp.int32, (1, nh), 1)
        for j in range(nh // 2):
            w0t = _bf(jnp.where(triu, wst_ref[2 * j], 0.0))
            w1t = _bf(jnp.where(triu, wst_ref[2 * j + 1], 0.0))
            for n in range(ts // CHUNK):
                rows = slice(n * CHUNK, (n + 1) * CHUNK)
                lanes = slice(j * 128, (j + 1) * 128)
                dmc = dm[rows, lanes]
                dmb = _bf(dmc)
                dv_scr[rows, lanes] = jnp.where(lo, _dot(w0t, dmb), _dot(w1t, dmb))
                vc = vb[rows, lanes]
                d0 = jnp.where(lo, dmc, 0.0)
                d1 = dmc - d0
                dws_ref[2 * j] += jnp.where(tril, _dot(_bf(d0), vc, NT), 0.0)
                dws_ref[2 * j + 1] += jnp.where(tril, _dot(_bf(d1), vc, NT), 0.0)
                dbst_ref[...] += (jnp.sum(d0, axis=1, keepdims=True) * (head == 2 * j).astype(F32)
                                  + jnp.sum(d1, axis=1, keepdims=True) * (head == 2 * j + 1).astype(F32))
        dvg, dglg, dglb = _ln_bwd(dv_scr[...], vhat, vrs, glg[...])
        dzv = dvg * _gelu_grad(zv, tv)
        dzuv_ref[:, 0:512] = _bf(dzu)
        dzuv_ref[:, 512:1024] = _bf(dzv)
        _accumulate(dclg_ref, dclg)
        _accumulate(dclb_ref, dclb)
        _accumulate(dglg_ref, dglg)
        _accumulate(dglb_ref, dglb)
        _accumulate(dbin_ref, jnp.concatenate([jnp.sum(dzu, axis=0, keepdims=True),
                                               jnp.sum(dzv, axis=0, keepdims=True)], axis=1))

    vec = (1, 512)
    return pl.pallas_call(
        body, name="bwd_out", grid=(s // ts,),
        in_specs=[_row(ts, d), _row(ts, 512), _row(ts, 1024, 1), _const(cln_g.shape), _const(cln_b.shape),
                  _const(gln_g.shape), _const(gln_b.shape), _const(ws.shape), _const(wst.shape), _const(bst.shape),
                  _const(w_out.shape)],
        out_specs=(_acc((d, d)), _row(ts, 512), _row(ts, 1024), _acc(ws.shape), _acc(bst.shape), _acc(vec), _acc(vec),
                   _acc(vec), _acc(vec), _acc((1, 1024))),
        out_shape=(SDS((d, d), BF16), SDS((s, 512), F32), SDS((s, 1024), BF16), SDS(ws.shape, F32),
                   SDS(bst.shape, F32), SDS(vec, F32), SDS(vec, F32), SDS(vec, F32), SDS(vec, F32), SDS((1, 1024), F32)),
        scratch_shapes=[pltpu.VMEM((ts, 512), F32), pltpu.VMEM((ts, 512), F32), pltpu.VMEM((d, d), F32)],
        compiler_params=_cparams("arbitrary"))(dh1, c, z, cln_g, cln_b, gln_g, gln_b, ws, wst, bst, w_out)


CONV_RC = 256
FWD_TAPS = range(CONV_PAD - CONV_K + 1, CONV_PAD + 1)


def _bwd_in(x, dh1, dc, a, z, dzuv, g_mix, w_in3, conv_w, ts):
    s, d = x.shape
    cw = conv_w.shape[1]
    nb = cw // 128
    n = s // ts
    rc = min(CONV_RC, ts)
    per_halo = ts // CONV_PAD
    w4 = conv_w.reshape(CONV_K, nb, 128).transpose(1, 0, 2)

    def body(x_ref, dh1_ref, dc_ref, a_ref, ahalo_ref, zag_ref, dzuv_ref, g_ref, w_ref, cw_ref,
             dx_ref, dw_ref, dcw_ref, dcb_ref, dbin_ref, dg_ref, dz_ref, acc, pad_d, pad_a, part, da_scr):
        i = pl.program_id(0)

        @pl.when(i == 0)
        def _():
            acc[...] = jnp.zeros_like(acc)
            part[...] = jnp.zeros_like(part)
            pad_d[:, ts:, :] = jnp.zeros((nb, CONV_PAD, 128), F32)

        before = jnp.where(i == n - 1, 0.0, ahalo_ref[...])
        for blk in range(nb):
            lanes = slice(128 * blk, 128 * (blk + 1))
            pad_d[blk, 0:ts, :] = dc_ref[:, lanes]
            pad_a[blk, 0:CONV_PAD, :] = before[:, lanes]
            pad_a[blk, CONV_PAD:, :] = a_ref[:, lanes]

        def rows8(v):
            return jnp.sum(v.reshape(rc // 8, 8, 128), axis=0)

        def unit(blk, r0):
            acc_da = jnp.zeros((rc, 128), F32)
            for res, offs in _by_residue(range(0, CONV_K)):
                shifted = pad_d[blk, pl.ds(r0 + res, rc + offs[-1] - res), :]
                for off in offs:
                    k = CONV_K - 1 - off
                    acc_da = acc_da + cw_ref[blk, k:k + 1, :] * shifted[off - res:off - res + rc, :]
            da_scr[blk, pl.ds(r0, rc), :] = acc_da
            dcc = pad_d[blk, pl.ds(r0, rc), :]
            for res, offs in _by_residue(FWD_TAPS):
                shifted = pad_a[blk, pl.ds(r0 + res, rc + offs[-1] - res), :]
                for off in offs:
                    k = off - FWD_TAPS[0]
                    part[blk, 8 * k:8 * k + 8, :] += rows8(dcc * shifted[off - res:off - res + rc, :])
            part[blk, 8 * CONV_PAD:, :] += rows8(dcc)

        for blk in range(nb):
            for r0 in range(0, ts, rc):
                unit(blk, r0)
        pad_d[:, ts:, :] = pad_d[:, 0:CONV_PAD, :]

        za = zag_ref[:, 0:512]
        sg = jax.nn.sigmoid(zag_ref[:, 512:1024])
        da_ = jnp.concatenate([da_scr[blk] for blk in range(nb)], axis=1)
        dza = da_ * sg
        dzg = da_ * za * sg * (1.0 - sg)
        dz_ref[:, 0:512] = _bf(dza)
        dz_ref[:, 512:1024] = _bf(dzg)
        dz_ref[:, 1024:2048] = dzuv_ref[...]
        dhn = jnp.zeros((ts, d), F32)
        for b in range(N_DEV):
            dhn = dhn + _dot(dz_ref[:, 256 * b:256 * (b + 1)], w_ref[b], NT)
        hn, xh, r = _rms_fwd(x_ref[...], g_ref[...])
        acc[...] += _dot(_bf(hn), dz_ref[...], TN)

        @pl.when(i == n - 1)
        def _():
            for b in range(N_DEV):
                dw_ref[b] = _bf(acc[:, 256 * b:256 * (b + 1)])
            for blk in range(nb):
                sums = jnp.sum(part[blk].reshape(CONV_PAD + 1, 8, 128), axis=1)
                dcw_ref[:, 128 * blk:128 * (blk + 1)] = sums[0:CONV_PAD, :]
                dcb_ref[:, 128 * blk:128 * (blk + 1)] = sums[CONV_PAD:, :]

        dxn, dg = _rms_bwd(dhn, xh, r, g_ref[...])
        dx_ref[...] = dh1_ref[...] + dxn
        _accumulate(dg_ref, dg)
        _accumulate(dbin_ref, jnp.concatenate([jnp.sum(dza, axis=0, keepdims=True),
                                               jnp.sum(dzg, axis=0, keepdims=True)], axis=1))

    back = lambda width, col=0: pl.BlockSpec((ts, width), lambda i: (n - 1 - i, col))
    halo = pl.BlockSpec((CONV_PAD, cw), lambda i: (jnp.maximum((n - 1 - i) * per_halo - 1, 0), 0))
    return pl.pallas_call(
        body, name="bwd_in", grid=(n,),
        in_specs=[back(d), back(d), back(cw), back(cw), halo, back(1024, 0), back(1024), _const(g_mix.shape),
                  _const(w_in3.shape), _const(w4.shape)],
        out_specs=(back(d), _acc(w_in3.shape), _acc((CONV_PAD, cw)), _acc((1, cw)), _acc((1, 1024)), _acc((1, d))),
        out_shape=(SDS((s, d), F32), SDS(w_in3.shape, BF16), SDS((CONV_PAD, cw), F32), SDS((1, cw), F32),
                   SDS((1, 1024), F32), SDS((1, d), F32)),
        scratch_shapes=[pltpu.VMEM((ts, 2048), BF16), pltpu.VMEM((d, 2048), F32),
                        pltpu.VMEM((nb, ts + CONV_PAD, 128), F32), pltpu.VMEM((nb, ts + CONV_PAD, 128), F32),
                        pltpu.VMEM((nb, 8 * (CONV_PAD + 1), 128), F32), pltpu.VMEM((nb, ts, 128), F32)],
        compiler_params=_cparams("arbitrary"))(x, dh1, dc, a, a, z, dzuv, g_mix, w_in3, w4)


def _tn_matmul(a, b, tm, name):
    s, m = a.shape
    n = b.shape[1]
    ts = _row_tile(s, (1024, 512, 256, 128))
    n_s = s // ts

    def body(a_ref, b_ref, o_ref, acc):
        k = pl.program_id(1)

        @pl.when(k == 0)
        def _():
            acc[...] = jnp.zeros_like(acc)

        acc[...] += _dot(a_ref[...], b_ref[...], TN)

        @pl.when(k == n_s - 1)
        def _():
            o_ref[...] = _bf(acc[...])

    return pl.pallas_call(
        body, name=name, grid=(m // tm, n_s),
        in_specs=[pl.BlockSpec((ts, tm), lambda i, k: (k, i)), pl.BlockSpec((ts, n), lambda i, k: (k, 0))],
        out_specs=pl.BlockSpec((tm, n), lambda i, k: (i, 0)), out_shape=SDS((m, n), BF16),
        scratch_shapes=[pltpu.VMEM((tm, n), F32)],
        compiler_params=_cparams("parallel", "arbitrary"))(a, b)


def _row_tile(r, cands):
    for cand in cands:
        if r % cand == 0:
            return cand
    return r


def _sum_in_device_order(me_ref, land_ref, own_ref):
    acc = None
    for dev in range(N_DEV):
        part = jnp.where(me_ref[0] == dev, own_ref[0], land_ref[dev]).astype(F32)
        acc = part if acc is None else acc + part
    return acc


def _sum8(land, own, me, name):
    _, r, c = land.shape
    tr = _row_tile(r, (512, 256, 352, 128, 8))
    own3 = own if own.ndim == 3 else own[None]
    own_map = (lambda i, me_ref: (me_ref[0], i, 0)) if own.ndim == 3 else (lambda i, me_ref: (0, i, 0))

    def body(me_ref, land_ref, own_ref, o_ref):
        o_ref[...] = _sum_in_device_order(me_ref, land_ref, own_ref)

    return pl.pallas_call(
        body, name=name, out_shape=SDS((r, c), F32),
        grid_spec=pltpu.PrefetchScalarGridSpec(
            num_scalar_prefetch=1, grid=(r // tr,),
            in_specs=[pl.BlockSpec((N_DEV, tr, c), lambda i, me_ref: (0, i, 0)), pl.BlockSpec((1, tr, c), own_map)],
            out_specs=pl.BlockSpec((tr, c), lambda i, me_ref: (i, 0))),
        compiler_params=_cparams("parallel"))(me, land, own3)


def _adam_update(w, g, m, v):
    m2 = ADAM_B1 * m + (1.0 - ADAM_B1) * g
    v2 = ADAM_B2 * v + (1.0 - ADAM_B2) * (g * g)
    m_hat = m2 / (1.0 - ADAM_B1 ** ADAM_STEP)
    v_hat = v2 / (1.0 - ADAM_B2 ** ADAM_STEP)
    return -ADAM_LR * (m_hat / (jnp.sqrt(v_hat) + ADAM_EPS) + ADAM_WD * w), m2, v2


def _slot_spec(tr, c, mask):
    return pl.BlockSpec((1, tr, c), lambda i, me_ref: (me_ref[0] ^ mask, i, 0))


def _sum_adamw(land, parts, w, m, v, me, name, masks=ALL_PEERS):
    r, c = w.shape
    tr = _row_tile(r, (256, 128, 176, 8))
    n = len(masks)

    def body(me_ref, own_ref, *refs):
        w_ref, m_ref, v_ref, g_ref, d_ref, m2_ref, v2_ref = refs[n:]
        g = own_ref[0].astype(F32)
        for peer_ref in refs[:n]:
            g = g + peer_ref[0].astype(F32)
        g_ref[...] = g
        d_ref[...], m2_ref[...], v2_ref[...] = _adam_update(w_ref[...], g, m_ref[...], v_ref[...])

    blk = pl.BlockSpec((tr, c), lambda i, me_ref: (i, 0))
    return pl.pallas_call(
        body, name=name, out_shape=(SDS((r, c), F32),) * 4,
        grid_spec=pltpu.PrefetchScalarGridSpec(
            num_scalar_prefetch=1, grid=(r // tr,),
            in_specs=[_slot_spec(tr, c, 0)] + [_slot_spec(tr, c, mask) for mask in masks] + [blk, blk, blk],
            out_specs=(blk,) * 4),
        compiler_params=_cparams("parallel"))(me, parts, *([land] * n), w, m, v)


def _chip_sum(parts, from_sibling, me, name):
    _, r, c = parts.shape
    tr = _row_tile(r, (512, 256, 352, 128, 8))
    spec = pl.BlockSpec((1, tr, c), lambda j, i, me_ref: (me_ref[0] ^ (2 * j), i, 0))

    def body(me_ref, p_ref, q_ref, o_ref):
        o_ref[...] = _bf(p_ref[...].astype(F32) + q_ref[...].astype(F32))

    return pl.pallas_call(
        body, name=name, out_shape=SDS(parts.shape, BF16),
        grid_spec=pltpu.PrefetchScalarGridSpec(num_scalar_prefetch=1, grid=(len(CHIPS), r // tr), in_specs=[spec, spec],
                                               out_specs=spec),
        compiler_params=_cparams("parallel", "parallel"))(me, parts, from_sibling)


def _adamw_many(ws, gs, ms, vs, name):
    n = len(ws)

    def body(*refs):
        ins, outs = refs[:4 * n], refs[4 * n:]
        for i in range(n):
            w, g, m, v = (ins[j * n + i][...] for j in range(4))
            outs[i][...], outs[n + i][...], outs[2 * n + i][...] = _adam_update(w, g, m, v)

    shapes = [SDS(w.shape, F32) for w in ws]
    out = pl.pallas_call(body, name=name, out_shape=shapes * 3, compiler_params=_cparams())(*ws, *gs, *ms, *vs)
    return out[:n], out[n:2 * n], out[2 * n:]


def _pack(arrs):
    flat = jnp.concatenate([a.reshape(-1) for a in arrs])
    pad = (-flat.shape[0]) % (128 * 128)
    return jnp.pad(flat, (0, pad)).reshape(-1, 128)


def _unpack(packed, shapes):
    flat = packed.reshape(-1)
    out, off = [], 0
    for shp in shapes:
        size = 1
        for dim in shp:
            size *= dim
        out.append(flat[off:off + size].reshape(shp))
        off += size
    return out


SMALL = ("norm_mix_g", "b_in", "conv_w", "conv_b", "conv_ln_g", "conv_ln_b", "gm_ln_g", "gm_ln_b", "gm_w_s", "gm_b_s",
         "norm_xa_g", "mem_norm_g", "norm_ffn_g", "final_norm_g")
BIG = ("w_in", "w_out", "xa_wq", "xa_wkv", "xa_wo", "ffn_w_gate_up", "ffn_w_down")
WEIGHTS = ("norm_mix_g", "w_in", "b_in", "conv_w", "conv_b", "conv_ln_g", "conv_ln_b", "gm_ln_g", "gm_ln_b", "gm_w_s",
           "gm_b_s", "w_out", "norm_xa_g", "mem_norm_g", "xa_wq", "xa_wkv", "xa_wo", "norm_ffn_g", "ffn_w_gate_up",
           "ffn_w_down", "final_norm_g")


def kernel(x, mem, norm_mix_g, w_in, b_in, conv_w, conv_b, conv_ln_g, conv_ln_b, gm_ln_g, gm_ln_b, gm_w_s, gm_b_s, w_out, norm_xa_g, mem_norm_g, xa_wq, xa_wkv, xa_wo, norm_ffn_g, ffn_w_gate_up, ffn_w_down, final_norm_g, loss_target, m_norm_mix_g, m_w_in, m_b_in, m_conv_w, m_conv_b, m_conv_ln_g, m_conv_ln_b, m_gm_ln_g, m_gm_ln_b, m_gm_w_s, m_gm_b_s, m_w_out, m_norm_xa_g, m_mem_norm_g, m_xa_wq, m_xa_wkv, m_xa_wo, m_norm_ffn_g, m_ffn_w_gate_up, m_ffn_w_down, m_final_norm_g, v_norm_mix_g, v_w_in, v_b_in, v_conv_w, v_conv_b, v_conv_ln_g, v_conv_ln_b, v_gm_ln_g, v_gm_ln_b, v_gm_w_s, v_gm_b_s, v_w_out, v_norm_xa_g, v_mem_norm_g, v_xa_wq, v_xa_wkv, v_xa_wo, v_norm_ffn_g, v_ffn_w_gate_up, v_ffn_w_down, v_final_norm_g):
    w = dict(norm_mix_g=norm_mix_g, w_in=w_in, b_in=b_in, conv_w=conv_w, conv_b=conv_b, conv_ln_g=conv_ln_g,
             conv_ln_b=conv_ln_b, gm_ln_g=gm_ln_g, gm_ln_b=gm_ln_b, gm_w_s=gm_w_s, gm_b_s=gm_b_s, w_out=w_out,
             norm_xa_g=norm_xa_g, mem_norm_g=mem_norm_g, xa_wq=xa_wq, xa_wkv=xa_wkv, xa_wo=xa_wo,
             norm_ffn_g=norm_ffn_g, ffn_w_gate_up=ffn_w_gate_up, ffn_w_down=ffn_w_down, final_norm_g=final_norm_g)
    mom = dict(norm_mix_g=m_norm_mix_g, w_in=m_w_in, b_in=m_b_in, conv_w=m_conv_w, conv_b=m_conv_b,
               conv_ln_g=m_conv_ln_g, conv_ln_b=m_conv_ln_b, gm_ln_g=m_gm_ln_g, gm_ln_b=m_gm_ln_b, gm_w_s=m_gm_w_s,
               gm_b_s=m_gm_b_s, w_out=m_w_out, norm_xa_g=m_norm_xa_g, mem_norm_g=m_mem_norm_g, xa_wq=m_xa_wq,
               xa_wkv=m_xa_wkv, xa_wo=m_xa_wo, norm_ffn_g=m_norm_ffn_g, ffn_w_gate_up=m_ffn_w_gate_up,
               ffn_w_down=m_ffn_w_down, final_norm_g=m_final_norm_g)
    var = dict(norm_mix_g=v_norm_mix_g, w_in=v_w_in, b_in=v_b_in, conv_w=v_conv_w, conv_b=v_conv_b,
               conv_ln_g=v_conv_ln_g, conv_ln_b=v_conv_ln_b, gm_ln_g=v_gm_ln_g, gm_ln_b=v_gm_ln_b, gm_w_s=v_gm_w_s,
               gm_b_s=v_gm_b_s, w_out=v_w_out, norm_xa_g=v_norm_xa_g, mem_norm_g=v_mem_norm_g, xa_wq=v_xa_wq,
               xa_wkv=v_xa_wkv, xa_wo=v_xa_wo, norm_ffn_g=v_norm_ffn_g, ffn_w_gate_up=v_ffn_w_gate_up,
               ffn_w_down=v_ffn_w_down, final_norm_g=v_final_norm_g)

    me = 4 * lax.axis_index("x") + 2 * lax.axis_index("y") + lax.axis_index("c")
    s, d = x.shape[1], x.shape[2]
    xs = x.reshape(s, d)
    mems = mem.reshape(mem.shape[1], d)
    tgt = loss_target.reshape(s, d)
    ts = min(512, s)
    ts_ffn = min(256, s)
    row = lambda a: a.reshape(1, -1)

    conv_w_pad = jnp.pad(conv_w, ((0, CONV_PAD - CONV_K), (0, 128 - conv_w.shape[1])))
    first_level = (SIBLING,) + OTHER_CHIPS
    ex_first = _exchange_start([_bf(w_in), conv_w_pad], [False] * 2, "gather_start_in", first_level, own_slot=me)
    behind = lambda t: _bf(t + ex_first[4][0:1, 0:1])
    shards = [behind(xa_wkv), behind(w_out), behind(xa_wq), behind(xa_wo), behind(ffn_w_gate_up.T), behind(ffn_w_down)]
    ex_rest = _exchange_start(shards, [False] * len(shards), "gather_start_rest", first_level, own_slot=me)
    g_send, g_recv, g_src, g_land = (list(ex_first[i]) + list(ex_rest[i]) for i in range(4))
    g_tok = ex_rest[4]

    def arrived(idx, after, name):
        pick = lambda seq: [seq[i] for i in idx]
        return _wait_and_forward(pick(g_send), pick(g_recv), pick(g_src), pick(g_land), after, "gather_pass_" + name,
                                 first_level)

    def complete(handle, after, name):
        return _forward_wait(handle[0], handle[1], handle[2], after, "forward_wait_" + name)

    bst = gm_b_s.T
    wst = jnp.swapaxes(gm_w_s, 1, 2)

    h_in = arrived((0, 1), g_tok, "in")
    w_in3, conv_w8 = complete(h_in, h_in[3], "in")
    conv_w_f = conv_w8[:, :CONV_K, :conv_w.shape[1]].transpose(1, 0, 2).reshape(CONV_K, -1)
    cw = conv_w_f.shape[1]
    z, a = _fwd_in(xs, row(norm_mix_g), w_in3, row(b_in), ts)
    h_out = arrived((2, 3), z, "out")
    c = _conv_fwd(a, conv_w_f + h_out[3][0:1, 0:1], row(conv_b))
    h_xa = arrived((4, 5), c, "xa")
    wkv3, w_out3 = complete(h_out, h_xa[3], "out")
    w_out_f = w_out3.reshape(-1, d)
    kv, mn = _kv_proj(mems, row(mem_norm_g), wkv3)
    h1 = _fwd_out(xs, c, z, row(conv_ln_g), row(conv_ln_b), row(gm_ln_g), row(gm_ln_b), gm_w_s, bst, w_out_f, ts)
    h_gut = arrived((6,), h1, "gate_up")
    wq3, wo3 = complete(h_xa, h_gut[3], "xa")
    wq_f = wq3.reshape(-1, d)
    wo_f = wo3.reshape(-1, d)
    h2, qb, ob = _fwd_xa(h1, row(norm_xa_g), wq_f, kv, wo_f, ts)
    h_down = arrived((7,), h2, "down")
    (wgut3,) = complete(h_gut, h_down[3], "gate_up")
    (wdown3,) = complete(h_down, wgut3, "down")
    wgut_f = wgut3.reshape(-1, d)
    wdown_f = wdown3.reshape(-1, d)
    dh3, dh3b, gu, hn2, loss_p, d_final_g = _fwd_ffn(h2, row(norm_ffn_g), wgut_f, wdown_f, row(final_norm_g), tgt,
                                                     ts_ffn)

    blocks = lambda m: m.reshape(N_DEV, -1, d)
    tok = lambda ex: ex[4][0:1, 0:1]
    me1 = me.astype(jnp.int32).reshape(1)

    def chip_sums(pair, after, tag):
        parts, handed = _pair_wait(pair[0], pair[1], pair[2], pair[3], after, "pair_wait_" + tag)
        return [_chip_sum(p, h, me1, "chip_sum_%s_%d" % (tag, i)) for i, (p, h) in enumerate(zip(parts, handed))]

    dh2, act, dgu, d_ffn_g = _bwd_ffn(h2, dh3, gu, row(norm_ffn_g), wgut_f, wdown_f, ts_ffn)
    dwgut = _tn_matmul(dgu, hn2, 2816, "dw_gate_up")
    dwdown = _tn_matmul(act, dh3b, 2816, "dw_down")
    pair_ffn = _pair_start([blocks(dwgut), blocks(dwdown)], "pair_start_ffn")
    dh1, dwq, dwo, dkv, d_xa_g = _bwd_xa(h1, dh2, qb, ob, row(norm_xa_g) + tok(pair_ffn), wq_f, wo_f, kv, ts)
    ex_ffn = _exchange_start(chip_sums(pair_ffn, dh1, "ffn"), [True] * 2, "scatter_start_ffn", OTHER_CHIPS)
    dwkv3, d_mem_g = _bwd_kv(dkv, mn, mems, row(mem_norm_g) + tok(ex_ffn), wkv3)
    pair_xa = _pair_start([blocks(dwq), blocks(dwo), dwkv3], "pair_start_xa")
    (dwout, dc, dzuv, dws, dbst, d_cln_g, d_cln_b, d_gln_g, d_gln_b, dbin_uv) = _bwd_out(
        dh1, c, z, row(conv_ln_g) + tok(pair_xa), row(conv_ln_b), row(gm_ln_g), row(gm_ln_b), gm_w_s, wst, bst,
        w_out_f, ts)
    early = dict(b_in_uv=dbin_uv, conv_ln_g=d_cln_g, conv_ln_b=d_cln_b, gm_ln_g=d_gln_g, gm_ln_b=d_gln_b,
                 gm_b_s=dbst.T, norm_xa_g=d_xa_g, mem_norm_g=d_mem_g, norm_ffn_g=d_ffn_g, final_norm_g=d_final_g,
                 loss=loss_p)
    dws_b = _bf(dws).reshape(GM_HEADS * CHUNK, CHUNK)
    ex_xa = _exchange_start(chip_sums(pair_xa, dc, "xa") + [blocks(dwout), dws_b, _pack(list(early.values()))],
                            [True] * 4 + [False] * 2, "scatter_start_xa", [OTHER_CHIPS] * 3 + [ALL_PEERS] * 3)
    dx, dw_in3, dconv_w, dconv_b, dbin_ag, d_mix_g = _bwd_in(xs, dh1, dc, a, z, dzuv, row(norm_mix_g) + tok(ex_xa),
                                                             w_in3, conv_w_f, ts)
    late = dict(norm_mix_g=d_mix_g, b_in_ag=dbin_ag, conv_w=dconv_w[:CONV_K], conv_b=dconv_b)
    ex_last = _exchange_start([_pack(list(late.values())), dw_in3], [False, True], "scatter_start_in")

    grads, delta, new_m, new_v = {}, {}, {}, {}

    def waited(ex, idx, scatter, after, name, masks=ALL_PEERS):
        return _exchange_wait(*[[ex[i][k] for k in idx] for i in range(4)], scatter, after, name, masks)

    def reduced(names, srcs, lands, masks):
        for nm, src, land, mk in zip(names, srcs, lands, masks):
            view = (lambda t: t.T) if nm == "ffn_w_gate_up" else (lambda t: t)
            upd = _sum_adamw(land, src, view(w[nm]), view(mom[nm]), view(var[nm]), me1, "adamw_" + nm, mk)
            grads[nm], delta[nm], new_m[nm], new_v[nm] = (view(t) for t in upd)
        return new_v[names[-1]]

    after = reduced(("ffn_w_gate_up", "ffn_w_down"),
                    *waited(ex_ffn, (0, 1), [True] * 2, ex_last[4], "scatter_wait_ffn", OTHER_CHIPS), [OTHER_CHIPS] * 2)
    xa_masks = [OTHER_CHIPS] * 3 + [ALL_PEERS]
    after = reduced(("xa_wq", "xa_wo", "xa_wkv", "w_out"),
                    *waited(ex_xa, (0, 1, 2, 3), [True] * 4, after, "scatter_wait_xa", xa_masks), xa_masks)

    (dws_src, early_src), (dws_land, early_land) = waited(ex_xa, (4, 5), [False] * 2, after, "gather_wait_early")
    (late_src,), (late_land,) = waited(ex_last, (0,), [False], early_land, "gather_wait_late")
    dws_tot = _sum8(dws_land, dws_src, me1, "sum_gm_w_s")
    early_tot = _sum8(early_land, early_src, me1, "sum_small_early")
    late_tot = _sum8(late_land, late_src, me1, "sum_small_late")
    early_g = dict(zip(early, _unpack(early_tot, [v.shape for v in early.values()])), gm_w_s=dws_tot)
    late_g = dict(zip(late, _unpack(late_tot, [v.shape for v in late.values()])))
    loss = early_g["loss"][0, 0]
    for nm in SMALL:
        if nm == "b_in":
            g = jnp.concatenate([late_g["b_in_ag"], early_g["b_in_uv"]], axis=1)
        elif nm == "conv_w":
            g = lax.dynamic_slice_in_dim(late_g[nm], me * conv_w.shape[1], conv_w.shape[1], axis=1)
        else:
            g = late_g[nm] if nm in late_g else early_g[nm]
        grads[nm] = g.reshape(w[nm].shape)

    small = [[src[nm] for nm in SMALL] for src in (w, grads, mom, var)]
    for dst, vals in zip((delta, new_m, new_v), _adamw_many(*small, "adamw_small")):
        dst.update(zip(SMALL, vals))

    reduced(("w_in",), *waited(ex_last, (1,), [True], delta[SMALL[0]], "scatter_wait_in"), [ALL_PEERS])

    return (loss, dx.reshape(x.shape), *[grads[nm] for nm in WEIGHTS], *[delta[nm] for nm in WEIGHTS],
            *[new_m[nm] for nm in WEIGHTS], *[new_v[nm] for nm in WEIGHTS])
```

```python
import jax
import jax.numpy as jnp
from jax import lax
from jax.experimental import pallas as pl
from jax.experimental.pallas import tpu as pltpu

F32 = jnp.float32
BF16 = jnp.bfloat16
SDS = jax.ShapeDtypeStruct

N_DEV = 8
RMS_EPS = 1e-6
LN_EPS = 1e-5
CONV_K = 31
CONV_PAD = 32
CHUNK = 128
GM_HEADS = 8
XA_HEADS = 4
XA_DH = 256
GELU_K0 = 0.7978845608028654
GELU_K1 = 0.044715
ADAM_LR = 0.001
ADAM_B1 = 0.9
ADAM_B2 = 0.999
ADAM_EPS = 1e-08
ADAM_WD = 0.01
ADAM_STEP = 10
VMEM_LIMIT = 60 * 1024 * 1024

NN = (((1,), (0,)), ((), ()))
NT = (((1,), (1,)), ((), ()))
TN = (((0,), (0,)), ((), ()))


def _dot(a, b, dims=NN):
    return lax.dot_general(a, b, dims, preferred_element_type=F32)


def _bf(x):
    return x.astype(BF16)


def _cparams(*sem):
    return pltpu.CompilerParams(dimension_semantics=tuple(sem) if sem else None, vmem_limit_bytes=VMEM_LIMIT)


def _row(ts, w, col=0):
    return pl.BlockSpec((ts, w), lambda i: (i, col))


def _const(shape):
    nd = len(shape)
    return pl.BlockSpec(shape, lambda i: (0,) * nd, pipeline_mode=pl.Buffered(1))


def _acc(shape):
    nd = len(shape)
    return pl.BlockSpec(shape, lambda i: (0,) * nd)


def _rms_fwd(x, g):
    r = lax.rsqrt(jnp.mean(x * x, axis=-1, keepdims=True) + RMS_EPS)
    xh = x * r
    return xh * g, xh, r


def _rms_bwd(dy, xh, r, g):
    gdy = dy * g
    dx = r * (gdy - xh * jnp.mean(gdy * xh, axis=-1, keepdims=True))
    dg = jnp.sum(dy * xh, axis=0, keepdims=True)
    return dx, dg


def _ln_fwd(x, g, b):
    mu = jnp.mean(x, axis=-1, keepdims=True)
    xc = x - mu
    rs = lax.rsqrt(jnp.mean(xc * xc, axis=-1, keepdims=True) + LN_EPS)
    xh = xc * rs
    return xh * g + b, xh, rs


def _ln_bwd(dy, xh, rs, g):
    dxh = dy * g
    dx = rs * (dxh - jnp.mean(dxh, axis=-1, keepdims=True) - xh * jnp.mean(dxh * xh, axis=-1, keepdims=True))
    return dx, jnp.sum(dy * xh, axis=0, keepdims=True), jnp.sum(dy, axis=0, keepdims=True)


def _gelu_tanh(x):
    return jnp.tanh(GELU_K0 * (x + GELU_K1 * (x * x * x)))


def _gelu(x, t):
    return 0.5 * x * (1.0 + t)


def _gelu_grad(x, t):
    return 0.5 * (1.0 + t) + 0.5 * x * (1.0 - t * t) * (GELU_K0 * (1.0 + 3.0 * GELU_K1 * x * x))


def _silu_grad(x, sg):
    return sg * (1.0 + x * (1.0 - sg))


def _by_residue(offsets):
    groups = {}
    for off in offsets:
        groups.setdefault(off % 8, []).append(off)
    return [(res, sorted(offs)) for res, offs in sorted(groups.items())]


def _accumulate(ref, val):
    @pl.when(pl.program_id(0) == 0)
    def _():
        ref[...] = jnp.zeros_like(ref)
    ref[...] += val


def _mix_masks():
    row = lax.broadcasted_iota(jnp.int32, (CHUNK, CHUNK), 0)
    col = lax.broadcasted_iota(jnp.int32, (CHUNK, CHUNK), 1)
    return row >= col, row <= col, col < (CHUNK // 2)


def _mix_fwd(vb, ws_ref, bst_ref, mixed_scr, ts):
    tril, _, lo = _mix_masks()
    for j in range(GM_HEADS // 2):
        w0 = _bf(jnp.where(tril, ws_ref[2 * j], 0.0))
        w1 = _bf(jnp.where(tril, ws_ref[2 * j + 1], 0.0))
        bias = jnp.where(lo, bst_ref[:, 2 * j:2 * j + 1], bst_ref[:, 2 * j + 1:2 * j + 2])
        for n in range(ts // CHUNK):
            v = vb[n * CHUNK:(n + 1) * CHUNK, j * 128:(j + 1) * 128]
            mixed_scr[n * CHUNK:(n + 1) * CHUNK, j * 128:(j + 1) * 128] = jnp.where(lo, _dot(w0, v), _dot(w1, v)) + bias


def _peer_of(mask):
    x, y, c = lax.axis_index("x"), lax.axis_index("y"), lax.axis_index("c")
    px = 1 - x if (mask >> 2) & 1 else x
    py = 1 - y if (mask >> 1) & 1 else y
    pc = 1 - c if mask & 1 else c
    return (px, py, pc), 4 * px + 2 * py + pc


ALL_PEERS = tuple(range(1, N_DEV))
OTHER_CHIPS = (2, 4, 6)
CHIPS = (0,) + OTHER_CHIPS
SIBLING = 1


def _split_copy(src_ref, land_ref, send_sem, recv_sem, mask, slot, scatter, outgoing):
    x, y, c = lax.axis_index("x"), lax.axis_index("y"), lax.axis_index("c")
    me = 4 * x + 2 * y + c
    peer, pidx = _peer_of(mask)
    return pltpu.make_async_remote_copy(
        src_ref=src_ref.at[pidx] if scatter else src_ref,
        dst_ref=land_ref.at[me if outgoing else pidx],
        send_sem=send_sem.at[slot], recv_sem=recv_sem.at[slot],
        device_id=peer, device_id_type=pl.DeviceIdType.MESH)


_HBM = pl.BlockSpec(memory_space=pltpu.HBM)
_SEM = pl.BlockSpec(memory_space=pltpu.SEMAPHORE)
_EFFECT = pltpu.SideEffectType.DATAFLOW_SIDE_EFFECTING


def _per_array(masks, n):
    return [tuple(masks)] * n if isinstance(masks[0], int) else [tuple(m) for m in masks]


def _exchange_start(srcs, scatter, name, masks=ALL_PEERS, own_slot=None):
    n = len(srcs)
    masks = _per_array(masks, n)
    lands = [lax.empty((N_DEV,) + tuple(s.shape[1:] if sc else s.shape), s.dtype) for s, sc in zip(srcs, scatter)]
    if own_slot is not None:
        lands = [land if sc else _with_own(land, s, own_slot) for land, s, sc in zip(lands, srcs, scatter)]
    lands = [pltpu.with_memory_space_constraint(land, pltpu.HBM) for land in lands]
    srcs = [pltpu.with_memory_space_constraint(s, pltpu.HBM) for s in srcs]

    def body(*refs):
        src_refs, land_refs = refs[:n], refs[n:2 * n]
        send_sems, recv_sems = refs[2 * n:3 * n], refs[3 * n:4 * n]
        token = refs[-1]
        for k in range(n):
            for slot, mask in enumerate(masks[k]):
                _split_copy(src_refs[k], land_refs[k], send_sems[k], recv_sems[k], mask, slot, scatter[k], True).start()
        token[...] = jnp.zeros_like(token)

    sems = [pltpu.SemaphoreType.DMA((len(m),)) for m in masks]
    out = pl.pallas_call(
        body, name=name,
        out_shape=tuple(sems + sems + [pltpu.HBM(s.shape, s.dtype) for s in srcs]
                        + [pltpu.HBM(l.shape, l.dtype) for l in lands] + [SDS((8, 128), F32)]),
        in_specs=[_HBM] * (2 * n),
        out_specs=tuple([_SEM] * (2 * n) + [_HBM] * (2 * n) + [pl.BlockSpec(memory_space=pltpu.VMEM)]),
        input_output_aliases={i: 2 * n + i for i in range(2 * n)},
        compiler_params=pltpu.CompilerParams(has_side_effects=_EFFECT),
    )(*srcs, *lands)
    return out[:n], out[n:2 * n], out[2 * n:3 * n], out[3 * n:4 * n], out[-1]


def _exchange_wait(send_sems, recv_sems, srcs_thru, lands_thru, scatter, after, name, masks=ALL_PEERS):
    n = len(srcs_thru)
    masks = _per_array(masks, n)

    def body(*refs):
        src_refs, land_refs = refs[:n], refs[n:2 * n]
        send_refs, recv_refs = refs[2 * n:3 * n], refs[3 * n:4 * n]
        for k in range(n):
            for slot, mask in enumerate(masks[k]):
                args = (src_refs[k], land_refs[k], send_refs[k], recv_refs[k], mask, slot, scatter[k])
                _split_copy(*args, True).wait_send()
                _split_copy(*args, False).wait_recv()

    out = pl.pallas_call(
        body, name=name,
        out_shape=tuple([pltpu.HBM(s.shape, s.dtype) for s in srcs_thru]
                        + [pltpu.HBM(l.shape, l.dtype) for l in lands_thru]),
        in_specs=[_HBM] * (2 * n) + [_SEM] * (2 * n) + [pl.BlockSpec(memory_space=pl.ANY)],
        out_specs=tuple([_HBM] * (2 * n)),
        input_output_aliases={i: i for i in range(2 * n)},
        compiler_params=pltpu.CompilerParams(has_side_effects=_EFFECT),
    )(*srcs_thru, *lands_thru, *send_sems, *recv_sems, after)
    return out[:n], out[n:]


def _wait_and_forward(send_sems, recv_sems, srcs_thru, lands_thru, after, name, masks):
    n = len(lands_thru)

    def body(*refs):
        src_refs, land_refs = refs[:n], refs[n:2 * n]
        send_refs, recv_refs = refs[2 * n:3 * n], refs[3 * n:4 * n]
        outs = refs[4 * n + 1:]
        fsend, frecv, token = outs[2 * n:3 * n], outs[3 * n:4 * n], outs[-1]
        for k in range(n):
            for slot, mask in enumerate(masks):
                args = (src_refs[k], land_refs[k], send_refs[k], recv_refs[k], mask, slot, False)
                _split_copy(*args, True).wait_send()
                _split_copy(*args, False).wait_recv()
        sibling, _ = _peer_of(SIBLING)
        for k in range(n):
            for slot, mask in enumerate(OTHER_CHIPS):
                _, mine = _peer_of(mask)
                pltpu.make_async_remote_copy(
                    src_ref=land_refs[k].at[mine], dst_ref=land_refs[k].at[mine], send_sem=fsend[k].at[slot],
                    recv_sem=frecv[k].at[slot], device_id=sibling, device_id_type=pl.DeviceIdType.MESH).start()
        token[...] = jnp.zeros_like(token)

    sem = pltpu.SemaphoreType.DMA((len(OTHER_CHIPS),))
    out = pl.pallas_call(
        body, name=name,
        out_shape=tuple([pltpu.HBM(s.shape, s.dtype) for s in srcs_thru] + [pltpu.HBM(l.shape, l.dtype) for l in lands_thru]
                        + [sem] * (2 * n) + [SDS((8, 128), F32)]),
        in_specs=[_HBM] * (2 * n) + [_SEM] * (2 * n) + [pl.BlockSpec(memory_space=pl.ANY)],
        out_specs=tuple([_HBM] * (2 * n) + [_SEM] * (2 * n) + [pl.BlockSpec(memory_space=pltpu.VMEM)]),
        input_output_aliases={i: i for i in range(2 * n)},
        compiler_params=pltpu.CompilerParams(has_side_effects=_EFFECT),
    )(*srcs_thru, *lands_thru, *send_sems, *recv_sems, after)
    return out[2 * n:3 * n], out[3 * n:4 * n], out[n:2 * n], out[-1]


def _forward_wait(send_sems, recv_sems, lands, after, name):
    n = len(lands)

    def body(*refs):
        land_refs, send_refs, recv_refs = refs[:n], refs[n:2 * n], refs[2 * n:3 * n]
        sibling, _ = _peer_of(SIBLING)
        for k in range(n):
            for slot, mask in enumerate(OTHER_CHIPS):
                _, mine = _peer_of(mask)
                _, theirs = _peer_of(mask | SIBLING)
                for block, wait in ((mine, "wait_send"), (theirs, "wait_recv")):
                    getattr(pltpu.make_async_remote_copy(
                        src_ref=land_refs[k].at[block], dst_ref=land_refs[k].at[block],
                        send_sem=send_refs[k].at[slot], recv_sem=recv_refs[k].at[slot], device_id=sibling,
                        device_id_type=pl.DeviceIdType.MESH), wait)()

    return pl.pallas_call(
        body, name=name, out_shape=tuple(pltpu.HBM(l.shape, l.dtype) for l in lands),
        in_specs=[_HBM] * n + [_SEM] * (2 * n) + [pl.BlockSpec(memory_space=pl.ANY)],
        out_specs=tuple([_HBM] * n), input_output_aliases={i: i for i in range(n)},
        compiler_params=pltpu.CompilerParams(has_side_effects=_EFFECT),
    )(*lands, *send_sems, *recv_sems, after)


def _pair_copy(parts_ref, land_ref, send_sem, recv_sem, slot, chip, outgoing):
    sibling, _ = _peer_of(SIBLING)
    _, block = _peer_of(chip | SIBLING if outgoing else chip)
    return pltpu.make_async_remote_copy(
        src_ref=parts_ref.at[block], dst_ref=land_ref.at[block], send_sem=send_sem.at[slot], recv_sem=recv_sem.at[slot],
        device_id=sibling, device_id_type=pl.DeviceIdType.MESH)


def _pair_start(parts, name):
    n = len(parts)
    lands = [pltpu.with_memory_space_constraint(lax.empty(p.shape, p.dtype), pltpu.HBM) for p in parts]
    parts = [pltpu.with_memory_space_constraint(p, pltpu.HBM) for p in parts]

    def body(*refs):
        part_refs, land_refs = refs[:n], refs[n:2 * n]
        send_sems, recv_sems, token = refs[2 * n:3 * n], refs[3 * n:4 * n], refs[-1]
        for k in range(n):
            for slot, chip in enumerate(CHIPS):
                _pair_copy(part_refs[k], land_refs[k], send_sems[k], recv_sems[k], slot, chip, True).start()
        token[...] = jnp.zeros_like(token)

    sem = pltpu.SemaphoreType.DMA((len(CHIPS),))
    out = pl.pallas_call(
        body, name=name,
        out_shape=tuple([sem] * (2 * n) + [pltpu.HBM(p.shape, p.dtype) for p in parts + lands] + [SDS((8, 128), F32)]),
        in_specs=[_HBM] * (2 * n),
        out_specs=tuple([_SEM] * (2 * n) + [_HBM] * (2 * n) + [pl.BlockSpec(memory_space=pltpu.VMEM)]),
        input_output_aliases={i: 2 * n + i for i in range(2 * n)},
        compiler_params=pltpu.CompilerParams(has_side_effects=_EFFECT),
    )(*parts, *lands)
    return out[:n], out[n:2 * n], out[2 * n:3 * n], out[3 * n:4 * n], out[-1]


def _pair_wait(send_sems, recv_sems, parts_thru, lands_thru, after, name):
    n = len(parts_thru)

    def body(*refs):
        part_refs, land_refs = refs[:n], refs[n:2 * n]
        send_refs, recv_refs = refs[2 * n:3 * n], refs[3 * n:4 * n]
        for k in range(n):
            for slot, chip in enumerate(CHIPS):
                args = (part_refs[k], land_refs[k], send_refs[k], recv_refs[k], slot, chip)
                _pair_copy(*args, True).wait_send()
                _pair_copy(*args, False).wait_recv()

    out = pl.pallas_call(
        body, name=name,
        out_shape=tuple(pltpu.HBM(p.shape, p.dtype) for p in list(parts_thru) + list(lands_thru)),
        in_specs=[_HBM] * (2 * n) + [_SEM] * (2 * n) + [pl.BlockSpec(memory_space=pl.ANY)],
        out_specs=tuple([_HBM] * (2 * n)), input_output_aliases={i: i for i in range(2 * n)},
        compiler_params=pltpu.CompilerParams(has_side_effects=_EFFECT),
    )(*parts_thru, *lands_thru, *send_sems, *recv_sems, after)
    return out[:n], out[n:]


def _with_own(landed, own, me):
    return lax.dynamic_update_slice_in_dim(landed, own[None], me, axis=0)


def _kv_proj(mem, g_mem, wkv3):
    m = mem.shape[0]

    def body(mem_ref, g_ref, w_ref, kv_ref, mn_ref):
        y, _, _ = _rms_fwd(mem_ref[...], g_ref[...])
        yb = _bf(y)
        mn_ref[...] = yb
        for b in range(N_DEV):
            kv_ref[:, 256 * b:256 * (b + 1)] = _dot(yb, w_ref[b])

    return pl.pallas_call(body, name="kv_proj", out_shape=(SDS((m, 2048), F32), SDS(mem.shape, BF16)),
                          compiler_params=_cparams())(mem, g_mem, wkv3)


def _fwd_in(x, g_mix, w_in3, b_in, ts):
    s, d = x.shape

    def body(x_ref, g_ref, w_ref, b_ref, z_ref, a_ref, t_ref):
        hn, _, _ = _rms_fwd(x_ref[...], g_ref[...])
        hb = _bf(hn)
        for b in range(N_DEV):
            z_ref[:, 256 * b:256 * (b + 1)] = _dot(hb, w_ref[b]) + b_ref[:, 256 * b:256 * (b + 1)]
        a_ref[...] = z_ref[:, 0:512] * jax.nn.sigmoid(z_ref[:, 512:1024])
        t_ref[...] = _gelu_tanh(z_ref[:, 1024:2048])

    return pl.pallas_call(
        body, name="fwd_in", grid=(s // ts,),
        in_specs=[_row(ts, d), _const(g_mix.shape), _const(w_in3.shape), _const(b_in.shape)],
        out_specs=(_row(ts, 2048), _row(ts, 512), _row(ts, 1024)),
        out_shape=(SDS((s, 2048), F32), SDS((s, 512), F32), SDS((s, 1024), F32)),
        compiler_params=_cparams("arbitrary"))(x, g_mix, w_in3, b_in)


def _conv_fwd(a, w, b):
    s, cw = a.shape
    rc = 256 if s % 256 == 0 else 128

    def body(a_ref, w_ref, b_ref, c_ref, pad):
        pad[0:CONV_PAD, :] = jnp.zeros((CONV_PAD, 128), F32)
        pad[CONV_PAD:, :] = a_ref[...]

        def chunk(i, carry):
            r0 = pl.multiple_of(i * rc, rc)
            acc = jnp.zeros((rc, 128), F32) + b_ref[...]
            for res, offs in _by_residue(range(CONV_PAD - CONV_K + 1, CONV_PAD + 1)):
                shifted = pad[pl.ds(r0 + res, rc + offs[-1] - res), :]
                for off in offs:
                    k = off - (CONV_PAD - CONV_K + 1)
                    acc = acc + w_ref[k:k + 1, :] * shifted[off - res:off - res + rc, :]
            c_ref[pl.ds(r0, rc), :] = acc
            return carry

        lax.fori_loop(0, s // rc, chunk, 0)

    blk = lambda r: pl.BlockSpec((r, 128), lambda j: (0, j))
    return pl.pallas_call(
        body, name="conv_fwd", grid=(cw // 128,),
        in_specs=[blk(s), blk(CONV_K), blk(1)], out_specs=blk(s), out_shape=SDS((s, cw), F32),
        scratch_shapes=[pltpu.VMEM((s + CONV_PAD, 128), F32)],
        compiler_params=_cparams("arbitrary"))(a, w, b)


def _fwd_out(x, c, z, tuv, cln_g, cln_b, gln_g, gln_b, ws, bst, w_out, ts):
    s, d = x.shape

    def body(x_ref, c_ref, zuv_ref, tuv_ref, clg, clb, glg, glb, ws_ref, bst_ref, wo_ref, h1_ref, mixed_scr):
        cl, _, _ = _ln_fwd(c_ref[...], clg[...], clb[...])
        co = cl * jax.nn.sigmoid(cl)
        u = _gelu(zuv_ref[:, 0:512], tuv_ref[:, 0:512])
        vg = _gelu(zuv_ref[:, 512:1024], tuv_ref[:, 512:1024])
        vln, _, _ = _ln_fwd(vg, glg[...], glb[...])
        _mix_fwd(_bf(vln), ws_ref, bst_ref, mixed_scr, ts)
        gm = u * mixed_scr[...]
        h1_ref[...] = x_ref[...] + _dot(_bf(co), wo_ref[0:512, :]) + _dot(_bf(gm), wo_ref[512:1024, :])

    return pl.pallas_call(
        body, name="fwd_out", grid=(s // ts,),
        in_specs=[_row(ts, d), _row(ts, 512), _row(ts, 1024, 1), _row(ts, 1024), _const(cln_g.shape),
                  _const(cln_b.shape), _const(gln_g.shape), _const(gln_b.shape), _const(ws.shape), _const(bst.shape),
                  _const(w_out.shape)],
        out_specs=_row(ts, d), out_shape=SDS((s, d), F32),
        scratch_shapes=[pltpu.VMEM((ts, 512), F32)],
        compiler_params=_cparams("arbitrary"))(x, c, z, tuv, cln_g, cln_b, gln_g, gln_b, ws, bst, w_out)


def _softmax_rows(sc):
    m = jnp.max(sc, axis=-1, keepdims=True)
    e = jnp.exp(sc - m)
    return e / jnp.sum(e, axis=-1, keepdims=True)


def _fwd_xa(h1, g_xa, wq, kv, wo, ts):
    s, d = h1.shape
    scale = XA_DH ** -0.5

    def body(h_ref, g_ref, wq_ref, kv_ref, wo_ref, h2_ref, q_ref, o_ref, o_scr):
        hn, _, _ = _rms_fwd(h_ref[...], g_ref[...])
        q_ref[...] = _bf(_dot(_bf(hn), wq_ref[...]))
        for h in range(XA_HEADS):
            qh = q_ref[:, XA_DH * h:XA_DH * (h + 1)]
            kh = _bf(kv_ref[:, XA_DH * h:XA_DH * (h + 1)])
            vh = _bf(kv_ref[:, d + XA_DH * h:d + XA_DH * (h + 1)])
            p = _softmax_rows(_dot(qh, kh, NT) * scale)
            o_scr[:, XA_DH * h:XA_DH * (h + 1)] = _dot(_bf(p), vh)
        o_ref[...] = _bf(o_scr[...])
        h2_ref[...] = h_ref[...] + _dot(o_ref[...], wo_ref[...])

    return pl.pallas_call(
        body, name="fwd_xa", grid=(s // ts,),
        in_specs=[_row(ts, d), _const(g_xa.shape), _const(wq.shape), _const(kv.shape), _const(wo.shape)],
        out_specs=(_row(ts, d), _row(ts, d), _row(ts, d)),
        out_shape=(SDS((s, d), F32), SDS((s, d), BF16), SDS((s, d), BF16)),
        scratch_shapes=[pltpu.VMEM((ts, d), F32)],
        compiler_params=_cparams("arbitrary"))(h1, g_xa, wq, kv, wo)


def _fwd_ffn(h2, g_ffn, wgut, wdown, g_final, target, ts):
    s, d = h2.shape
    hid = wdown.shape[0]
    hc = hid // 2

    def body(h_ref, g_ref, wgu_ref, wd_ref, gf_ref, t_ref, dh3_ref, dh3b_ref, gu_ref, hn_ref, loss_ref, dgf_ref):
        hn, _, _ = _rms_fwd(h_ref[...], g_ref[...])
        hb = _bf(hn)
        hn_ref[...] = hb
        h3 = h_ref[...]
        for n in range(2):
            g = _dot(hb, wgu_ref[hc * n:hc * (n + 1), :], NT)
            u = _dot(hb, wgu_ref[hid + hc * n:hid + hc * (n + 1), :], NT)
            gu_ref[:, hc * n:hc * (n + 1)] = g
            gu_ref[:, hid + hc * n:hid + hc * (n + 1)] = u
            act = g * jax.nn.sigmoid(g) * u
            h3 = h3 + _dot(_bf(act), wd_ref[hc * n:hc * (n + 1), :])
        y, xh, r = _rms_fwd(h3, gf_ref[...])
        diff = y - t_ref[...]
        part = 0.5 * jnp.sum(jnp.mean(diff * diff, axis=-1, keepdims=True), axis=0, keepdims=True)
        _accumulate(loss_ref, jnp.zeros(loss_ref.shape, F32) + part)
        dh3, dgf = _rms_bwd(diff * (1.0 / d), xh, r, gf_ref[...])
        dh3_ref[...] = dh3
        dh3b_ref[...] = _bf(dh3)
        _accumulate(dgf_ref, dgf)

    return pl.pallas_call(
        body, name="fwd_ffn", grid=(s // ts,),
        in_specs=[_row(ts, d), _const(g_ffn.shape), _const(wgut.shape), _const(wdown.shape), _const(g_final.shape),
                  _row(ts, d)],
        out_specs=(_row(ts, d), _row(ts, d), _row(ts, 2 * hid), _row(ts, d), _acc((1, 128)), _acc((1, d))),
        out_shape=(SDS((s, d), F32), SDS((s, d), BF16), SDS((s, 2 * hid), F32), SDS((s, d), BF16), SDS((1, 128), F32),
                   SDS((1, d), F32)),
        compiler_params=_cparams("arbitrary"))(h2, g_ffn, wgut, wdown, g_final, target)


def _bwd_ffn(h2, dh3, gu, g_ffn, wgut, wdown, ts):
    s, d = h2.shape
    hid = wdown.shape[0]
    hc = hid // 2

    def body(h_ref, dh3_ref, gu_ref, g_ref, wgu_ref, wd_ref, dh2_ref, act_ref, dgu_ref, dg_ref):
        _, xh, r = _rms_fwd(h_ref[...], g_ref[...])
        db = _bf(dh3_ref[...])
        dhn = jnp.zeros((ts, d), F32)
        for n in range(2):
            wg = wgu_ref[hc * n:hc * (n + 1), :]
            wu = wgu_ref[hid + hc * n:hid + hc * (n + 1), :]
            g = gu_ref[:, hc * n:hc * (n + 1)]
            u = gu_ref[:, hid + hc * n:hid + hc * (n + 1)]
            sg = jax.nn.sigmoid(g)
            sl = g * sg
            act_ref[:, hc * n:hc * (n + 1)] = _bf(sl * u)
            dact = _dot(db, wd_ref[hc * n:hc * (n + 1), :], NT)
            dgb = _bf(dact * u * _silu_grad(g, sg))
            dub = _bf(dact * sl)
            dgu_ref[:, hc * n:hc * (n + 1)] = dgb
            dgu_ref[:, hid + hc * n:hid + hc * (n + 1)] = dub
            dhn = dhn + _dot(dgb, wg) + _dot(dub, wu)
        dx, dg = _rms_bwd(dhn, xh, r, g_ref[...])
        dh2_ref[...] = dh3_ref[...] + dx
        _accumulate(dg_ref, dg)

    return pl.pallas_call(
        body, name="bwd_ffn", grid=(s // ts,),
        in_specs=[_row(ts, d), _row(ts, d), _row(ts, 2 * hid), _const(g_ffn.shape), _const(wgut.shape),
                  _const(wdown.shape)],
        out_specs=(_row(ts, d), _row(ts, hid), _row(ts, 2 * hid), _acc((1, d))),
        out_shape=(SDS((s, d), F32), SDS((s, hid), BF16), SDS((s, 2 * hid), BF16), SDS((1, d), F32)),
        compiler_params=_cparams("arbitrary"))(h2, dh3, gu, g_ffn, wgut, wdown)


def _bwd_xa(h1, dh2, qb, ob, g_xa, wq, wo, kv, ts):
    s, d = h1.shape
    scale = XA_DH ** -0.5

    def body(h_ref, dh2_ref, q_ref, o_ref, g_ref, wq_ref, wo_ref, kv_ref, dh1_ref, dwq_ref, dwo_ref, dkv_ref, dg_ref,
             dq_scr, accq, acco):
        hn, xh, r = _rms_fwd(h_ref[...], g_ref[...])
        hb = _bf(hn)
        dh2b = _bf(dh2_ref[...])
        do = _dot(dh2b, wo_ref[...], NT)

        @pl.when(pl.program_id(0) == 0)
        def _():
            dkv_ref[...] = jnp.zeros_like(dkv_ref)
            accq[...] = jnp.zeros_like(accq)
            acco[...] = jnp.zeros_like(acco)

        for h in range(XA_HEADS):
            lo, hi = XA_DH * h, XA_DH * (h + 1)
            qh = q_ref[:, lo:hi]
            kh = _bf(kv_ref[:, lo:hi])
            vh = _bf(kv_ref[:, d + lo:d + hi])
            p = _softmax_rows(_dot(qh, kh, NT) * scale)
            pb = _bf(p)
            doh = _bf(do[:, lo:hi])
            dp = _dot(doh, vh, NT)
            ds = p * (dp - jnp.sum(p * dp, axis=-1, keepdims=True)) * scale
            dsb = _bf(ds)
            dq_scr[:, lo:hi] = _dot(dsb, kh)
            dkv_ref[:, lo:hi] += _dot(dsb, qh, TN)
            dkv_ref[:, d + lo:d + hi] += _dot(pb, doh, TN)
        dqb = _bf(dq_scr[...])
        accq[...] += _dot(hb, dqb, TN)
        acco[...] += _dot(o_ref[...], dh2b, TN)
        dx, dg = _rms_bwd(_dot(dqb, wq_ref[...], NT), xh, r, g_ref[...])
        dh1_ref[...] = dh2_ref[...] + dx
        _accumulate(dg_ref, dg)

        @pl.when(pl.program_id(0) == pl.num_programs(0) - 1)
        def _():
            dwq_ref[...] = _bf(accq[...])
            dwo_ref[...] = _bf(acco[...])

    return pl.pallas_call(
        body, name="bwd_xa", grid=(s // ts,),
        in_specs=[_row(ts, d), _row(ts, d), _row(ts, d), _row(ts, d), _const(g_xa.shape), _const(wq.shape),
                  _const(wo.shape), _const(kv.shape)],
        out_specs=(_row(ts, d), _acc((d, d)), _acc((d, d)), _acc(kv.shape), _acc((1, d))),
        out_shape=(SDS((s, d), F32), SDS((d, d), BF16), SDS((d, d), BF16), SDS(kv.shape, F32), SDS((1, d), F32)),
        scratch_shapes=[pltpu.VMEM((ts, d), F32), pltpu.VMEM((d, d), F32), pltpu.VMEM((d, d), F32)],
        compiler_params=_cparams("arbitrary"))(h1, dh2, qb, ob, g_xa, wq, wo, kv)


def _bwd_kv(dkv, mn, mem, g_mem, wkv3):
    d = mem.shape[1]

    def body(dkv_ref, mn_ref, mem_ref, g_ref, w_ref, dw_ref, dg_ref):
        dkvb = _bf(dkv_ref[...])
        dmn = jnp.zeros(mem_ref.shape, F32)
        for b in range(N_DEV):
            blk = dkvb[:, 256 * b:256 * (b + 1)]
            dmn = dmn + _dot(blk, w_ref[b], NT)
            dw_ref[b] = _bf(_dot(mn_ref[...], blk, TN))
        _, xh, r = _rms_fwd(mem_ref[...], g_ref[...])
        _, dg = _rms_bwd(dmn, xh, r, g_ref[...])
        dg_ref[...] = dg

    return pl.pallas_call(body, name="bwd_kv", out_shape=(SDS(wkv3.shape, BF16), SDS((1, d), F32)),
                          compiler_params=_cparams())(dkv, mn, mem, g_mem, wkv3)


def _bwd_out(dh1, c, z, tuv, cln_g, cln_b, gln_g, gln_b, ws, wst, bst, w_out, ts):
    s, d = dh1.shape
    nh = GM_HEADS

    def body(dh1_ref, c_ref, zuv_ref, tuv_ref, clg, clb, glg, glb, ws_ref, wst_ref, bst_ref, wo_ref,
             dwo_ref, dc_ref, dzuv_ref, dws_ref, dbst_ref, dclg_ref, dclb_ref, dglg_ref, dglb_ref, dbin_ref,
             mixed_scr, dv_scr, acc):
        cl, chat, crs = _ln_fwd(c_ref[...], clg[...], clb[...])
        sg = jax.nn.sigmoid(cl)
        zu = zuv_ref[:, 0:512]
        zv = zuv_ref[:, 512:1024]
        tu = tuv_ref[:, 0:512]
        tv = tuv_ref[:, 512:1024]
        u = _gelu(zu, tu)
        vg = _gelu(zv, tv)
        vln, vhat, vrs = _ln_fwd(vg, glg[...], glb[...])
        vb = _bf(vln)
        _mix_fwd(vb, ws_ref, bst_ref, mixed_scr, ts)
        mixed = mixed_scr[...]
        dh1b = _bf(dh1_ref[...])
        dcat = _dot(dh1b, wo_ref[...], NT)
        dgm = dcat[:, 512:1024]
        dc, dclg, dclb = _ln_bwd(dcat[:, 0:512] * _silu_grad(cl, sg), chat, crs, clg[...])
        dc_ref[...] = dc
        dzu = dgm * mixed * _gelu_grad(zu, tu)
        dm = dgm * u

        @pl.when(pl.program_id(0) == 0)
        def _():
            dws_ref[...] = jnp.zeros_like(dws_ref)
            dbst_ref[...] = jnp.zeros_like(dbst_ref)
            acc[...] = jnp.zeros_like(acc)

        acc[0:512, :] += _dot(_bf(cl * sg), dh1b, TN)
        acc[512:1024, :] += _dot(_bf(u * mixed), dh1b, TN)

        @pl.when(pl.program_id(0) == pl.num_programs(0) - 1)
        def _():
            dwo_ref[...] = _bf(acc[...])

        tril, triu, lo = _mix_masks()
        head = lax.broadcasted_iota(jnp.int32, (1, nh), 1)
        for j in range(nh // 2):
            w0t = _bf(jnp.where(triu, wst_ref[2 * j], 0.0))
            w1t = _bf(jnp.where(triu, wst_ref[2 * j + 1], 0.0))
            for n in range(ts // CHUNK):
                rows = slice(n * CHUNK, (n + 1) * CHUNK)
                lanes = slice(j * 128, (j + 1) * 128)
                dmc = dm[rows, lanes]
                dmb = _bf(dmc)
                dv_scr[rows, lanes] = jnp.where(lo, _dot(w0t, dmb), _dot(w1t, dmb))
                vc = vb[rows, lanes]
                d0 = jnp.where(lo, dmc, 0.0)
                d1 = dmc - d0
                dws_ref[2 * j] += jnp.where(tril, _dot(_bf(d0), vc, NT), 0.0)
                dws_ref[2 * j + 1] += jnp.where(tril, _dot(_bf(d1), vc, NT), 0.0)
                dbst_ref[...] += (jnp.sum(d0, axis=1, keepdims=True) * (head == 2 * j).astype(F32)
                                  + jnp.sum(d1, axis=1, keepdims=True) * (head == 2 * j + 1).astype(F32))
        dvg, dglg, dglb = _ln_bwd(dv_scr[...], vhat, vrs, glg[...])
        dzv = dvg * _gelu_grad(zv, tv)
        dzuv_ref[:, 0:512] = _bf(dzu)
        dzuv_ref[:, 512:1024] = _bf(dzv)
        _accumulate(dclg_ref, dclg)
        _accumulate(dclb_ref, dclb)
        _accumulate(dglg_ref, dglg)
        _accumulate(dglb_ref, dglb)
        _accumulate(dbin_ref, jnp.concatenate([jnp.sum(dzu, axis=0, keepdims=True),
                                               jnp.sum(dzv, axis=0, keepdims=True)], axis=1))

    vec = (1, 512)
    return pl.pallas_call(
        body, name="bwd_out", grid=(s // ts,),
        in_specs=[_row(ts, d), _row(ts, 512), _row(ts, 1024, 1), _row(ts, 1024), _const(cln_g.shape),
                  _const(cln_b.shape), _const(gln_g.shape), _const(gln_b.shape), _const(ws.shape), _const(wst.shape),
                  _const(bst.shape), _const(w_out.shape)],
        out_specs=(_acc((d, d)), _row(ts, 512), _row(ts, 1024), _acc(ws.shape), _acc(bst.shape), _acc(vec), _acc(vec),
                   _acc(vec), _acc(vec), _acc((1, 1024))),
        out_shape=(SDS((d, d), BF16), SDS((s, 512), F32), SDS((s, 1024), BF16), SDS(ws.shape, F32),
                   SDS(bst.shape, F32), SDS(vec, F32), SDS(vec, F32), SDS(vec, F32), SDS(vec, F32), SDS((1, 1024), F32)),
        scratch_shapes=[pltpu.VMEM((ts, 512), F32), pltpu.VMEM((ts, 512), F32), pltpu.VMEM((d, d), F32)],
        compiler_params=_cparams("arbitrary"))(dh1, c, z, tuv, cln_g, cln_b, gln_g, gln_b, ws, wst, bst, w_out)


CONV_RC = 256
FWD_TAPS = range(CONV_PAD - CONV_K + 1, CONV_PAD + 1)


def _bwd_in(x, dh1, dc, a, z, dzuv, g_mix, w_in3, conv_w, ts):
    s, d = x.shape
    cw = conv_w.shape[1]
    nb = cw // 128
    n = s // ts
    rc = min(CONV_RC, ts)
    per_halo = ts // CONV_PAD
    w4 = conv_w.reshape(CONV_K, nb, 128).transpose(1, 0, 2)

    def body(x_ref, dh1_ref, dc_ref, a_ref, ahalo_ref, zag_ref, dzuv_ref, g_ref, w_ref, cw_ref,
             dx_ref, dw_ref, dcw_ref, dcb_ref, dbin_ref, dg_ref, dz_ref, acc, pad_d, pad_a, part, da_scr):
        i = pl.program_id(0)

        @pl.when(i == 0)
        def _():
            acc[...] = jnp.zeros_like(acc)
            part[...] = jnp.zeros_like(part)
            pad_d[:, ts:, :] = jnp.zeros((nb, CONV_PAD, 128), F32)

        before = jnp.where(i == n - 1, 0.0, ahalo_ref[...])
        for blk in range(nb):
            lanes = slice(128 * blk, 128 * (blk + 1))
            pad_d[blk, 0:ts, :] = dc_ref[:, lanes]
            pad_a[blk, 0:CONV_PAD, :] = before[:, lanes]
            pad_a[blk, CONV_PAD:, :] = a_ref[:, lanes]

        def rows8(v):
            return jnp.sum(v.reshape(rc // 8, 8, 128), axis=0)

        def unit(blk, r0):
            acc_da = jnp.zeros((rc, 128), F32)
            for res, offs in _by_residue(range(0, CONV_K)):
                shifted = pad_d[blk, pl.ds(r0 + res, rc + offs[-1] - res), :]
                for off in offs:
                    k = CONV_K - 1 - off
                    acc_da = acc_da + cw_ref[blk, k:k + 1, :] * shifted[off - res:off - res + rc, :]
            da_scr[blk, pl.ds(r0, rc), :] = acc_da
            dcc = pad_d[blk, pl.ds(r0, rc), :]
            for res, offs in _by_residue(FWD_TAPS):
                shifted = pad_a[blk, pl.ds(r0 + res, rc + offs[-1] - res), :]
                for off in offs:
                    k = off - FWD_TAPS[0]
                    part[blk, 8 * k:8 * k + 8, :] += rows8(dcc * shifted[off - res:off - res + rc, :])
            part[blk, 8 * CONV_PAD:, :] += rows8(dcc)

        for blk in range(nb):
            for r0 in range(0, ts, rc):
                unit(blk, r0)
        pad_d[:, ts:, :] = pad_d[:, 0:CONV_PAD, :]

        za = zag_ref[:, 0:512]
        sg = jax.nn.sigmoid(zag_ref[:, 512:1024])
        da_ = jnp.concatenate([da_scr[blk] for blk in range(nb)], axis=1)
        dza = da_ * sg
        dzg = da_ * za * sg * (1.0 - sg)
        dz_ref[:, 0:512] = _bf(dza)
        dz_ref[:, 512:1024] = _bf(dzg)
        dz_ref[:, 1024:2048] = dzuv_ref[...]
        dhn = jnp.zeros((ts, d), F32)
        for b in range(N_DEV):
            dhn = dhn + _dot(dz_ref[:, 256 * b:256 * (b + 1)], w_ref[b], NT)
        hn, xh, r = _rms_fwd(x_ref[...], g_ref[...])
        acc[...] += _dot(_bf(hn), dz_ref[...], TN)

        @pl.when(i == n - 1)
        def _():
            for b in range(N_DEV):
                dw_ref[b] = _bf(acc[:, 256 * b:256 * (b + 1)])
            for blk in range(nb):
                sums = jnp.sum(part[blk].reshape(CONV_PAD + 1, 8, 128), axis=1)
                dcw_ref[:, 128 * blk:128 * (blk + 1)] = sums[0:CONV_PAD, :]
                dcb_ref[:, 128 * blk:128 * (blk + 1)] = sums[CONV_PAD:, :]

        dxn, dg = _rms_bwd(dhn, xh, r, g_ref[...])
        dx_ref[...] = dh1_ref[...] + dxn
        _accumulate(dg_ref, dg)
        _accumulate(dbin_ref, jnp.concatenate([jnp.sum(dza, axis=0, keepdims=True),
                                               jnp.sum(dzg, axis=0, keepdims=True)], axis=1))

    back = lambda width, col=0: pl.BlockSpec((ts, width), lambda i: (n - 1 - i, col))
    halo = pl.BlockSpec((CONV_PAD, cw), lambda i: (jnp.maximum((n - 1 - i) * per_halo - 1, 0), 0))
    return pl.pallas_call(
        body, name="bwd_in", grid=(n,),
        in_specs=[back(d), back(d), back(cw), back(cw), halo, back(1024, 0), back(1024), _const(g_mix.shape),
                  _const(w_in3.shape), _const(w4.shape)],
        out_specs=(back(d), _acc(w_in3.shape), _acc((CONV_PAD, cw)), _acc((1, cw)), _acc((1, 1024)), _acc((1, d))),
        out_shape=(SDS((s, d), F32), SDS(w_in3.shape, BF16), SDS((CONV_PAD, cw), F32), SDS((1, cw), F32),
                   SDS((1, 1024), F32), SDS((1, d), F32)),
        scratch_shapes=[pltpu.VMEM((ts, 2048), BF16), pltpu.VMEM((d, 2048), F32),
                        pltpu.VMEM((nb, ts + CONV_PAD, 128), F32), pltpu.VMEM((nb, ts + CONV_PAD, 128), F32),
                        pltpu.VMEM((nb, 8 * (CONV_PAD + 1), 128), F32), pltpu.VMEM((nb, ts, 128), F32)],
        compiler_params=_cparams("arbitrary"))(x, dh1, dc, a, a, z, dzuv, g_mix, w_in3, w4)


def _tn_matmul(a, b, tm, name):
    s, m = a.shape
    n = b.shape[1]
    ts = _row_tile(s, (1024, 512, 256, 128))
    n_s = s // ts

    def body(a_ref, b_ref, o_ref, acc):
        k = pl.program_id(1)

        @pl.when(k == 0)
        def _():
            acc[...] = jnp.zeros_like(acc)

        acc[...] += _dot(a_ref[...], b_ref[...], TN)

        @pl.when(k == n_s - 1)
        def _():
            o_ref[...] = _bf(acc[...])

    return pl.pallas_call(
        body, name=name, grid=(m // tm, n_s),
        in_specs=[pl.BlockSpec((ts, tm), lambda i, k: (k, i)), pl.BlockSpec((ts, n), lambda i, k: (k, 0))],
        out_specs=pl.BlockSpec((tm, n), lambda i, k: (i, 0)), out_shape=SDS((m, n), BF16),
        scratch_shapes=[pltpu.VMEM((tm, n), F32)],
        compiler_params=_cparams("parallel", "arbitrary"))(a, b)


def _row_tile(r, cands):
    for cand in cands:
        if r % cand == 0:
            return cand
    return r


def _sum_in_device_order(me_ref, land_ref, own_ref):
    acc = None
    for dev in range(N_DEV):
        part = jnp.where(me_ref[0] == dev, own_ref[0], land_ref[dev]).astype(F32)
        acc = part if acc is None else acc + part
    return acc


def _sum8(land, own, me, name):
    _, r, c = land.shape
    tr = _row_tile(r, (512, 256, 352, 128, 8))
    own3 = own if own.ndim == 3 else own[None]
    own_map = (lambda i, me_ref: (me_ref[0], i, 0)) if own.ndim == 3 else (lambda i, me_ref: (0, i, 0))

    def body(me_ref, land_ref, own_ref, o_ref):
        o_ref[...] = _sum_in_device_order(me_ref, land_ref, own_ref)

    return pl.pallas_call(
        body, name=name, out_shape=SDS((r, c), F32),
        grid_spec=pltpu.PrefetchScalarGridSpec(
            num_scalar_prefetch=1, grid=(r // tr,),
            in_specs=[pl.BlockSpec((N_DEV, tr, c), lambda i, me_ref: (0, i, 0)), pl.BlockSpec((1, tr, c), own_map)],
            out_specs=pl.BlockSpec((tr, c), lambda i, me_ref: (i, 0))),
        compiler_params=_cparams("parallel"))(me, land, own3)


def _adam_update(w, g, m, v):
    m2 = ADAM_B1 * m + (1.0 - ADAM_B1) * g
    v2 = ADAM_B2 * v + (1.0 - ADAM_B2) * (g * g)
    m_hat = m2 / (1.0 - ADAM_B1 ** ADAM_STEP)
    v_hat = v2 / (1.0 - ADAM_B2 ** ADAM_STEP)
    return -ADAM_LR * (m_hat / (jnp.sqrt(v_hat) + ADAM_EPS) + ADAM_WD * w), m2, v2


def _slot_spec(tr, c, mask):
    return pl.BlockSpec((1, tr, c), lambda i, me_ref: (me_ref[0] ^ mask, i, 0))


def _sum_adamw(land, parts, w, m, v, me, name, masks=ALL_PEERS):
    r, c = w.shape
    tr = _row_tile(r, (256, 128, 176, 8))
    n = len(masks)

    def body(me_ref, own_ref, *refs):
        w_ref, m_ref, v_ref, g_ref, d_ref, m2_ref, v2_ref = refs[n:]
        g = own_ref[0].astype(F32)
        for peer_ref in refs[:n]:
            g = g + peer_ref[0].astype(F32)
        g_ref[...] = g
        d_ref[...], m2_ref[...], v2_ref[...] = _adam_update(w_ref[...], g, m_ref[...], v_ref[...])

    blk = pl.BlockSpec((tr, c), lambda i, me_ref: (i, 0))
    return pl.pallas_call(
        body, name=name, out_shape=(SDS((r, c), F32),) * 4,
        grid_spec=pltpu.PrefetchScalarGridSpec(
            num_scalar_prefetch=1, grid=(r // tr,),
            in_specs=[_slot_spec(tr, c, 0)] + [_slot_spec(tr, c, mask) for mask in masks] + [blk, blk, blk],
            out_specs=(blk,) * 4),
        compiler_params=_cparams("parallel"))(me, parts, *([land] * n), w, m, v)


def _chip_sum(parts, from_sibling, me, name):
    _, r, c = parts.shape
    tr = _row_tile(r, (512, 256, 352, 128, 8))
    spec = pl.BlockSpec((1, tr, c), lambda j, i, me_ref: (me_ref[0] ^ (2 * j), i, 0))

    def body(me_ref, p_ref, q_ref, o_ref):
        o_ref[...] = _bf(p_ref[...].astype(F32) + q_ref[...].astype(F32))

    return pl.pallas_call(
        body, name=name, out_shape=SDS(parts.shape, BF16),
        grid_spec=pltpu.PrefetchScalarGridSpec(num_scalar_prefetch=1, grid=(len(CHIPS), r // tr), in_specs=[spec, spec],
                                               out_specs=spec),
        compiler_params=_cparams("parallel", "parallel"))(me, parts, from_sibling)


def _adamw_many(ws, gs, ms, vs, name):
    n = len(ws)

    def body(*refs):
        ins, outs = refs[:4 * n], refs[4 * n:]
        for i in range(n):
            w, g, m, v = (ins[j * n + i][...] for j in range(4))
            outs[i][...], outs[n + i][...], outs[2 * n + i][...] = _adam_update(w, g, m, v)

    shapes = [SDS(w.shape, F32) for w in ws]
    out = pl.pallas_call(body, name=name, out_shape=shapes * 3, compiler_params=_cparams())(*ws, *gs, *ms, *vs)
    return out[:n], out[n:2 * n], out[2 * n:]


def _pack(arrs):
    flat = jnp.concatenate([a.reshape(-1) for a in arrs])
    pad = (-flat.shape[0]) % (128 * 128)
    return jnp.pad(flat, (0, pad)).reshape(-1, 128)


def _unpack(packed, shapes):
    flat = packed.reshape(-1)
    out, off = [], 0
    for shp in shapes:
        size = 1
        for dim in shp:
            size *= dim
        out.append(flat[off:off + size].reshape(shp))
        off += size
    return out


SMALL = ("norm_mix_g", "b_in", "conv_w", "conv_b", "conv_ln_g", "conv_ln_b", "gm_ln_g", "gm_ln_b", "gm_w_s", "gm_b_s",
         "norm_xa_g", "mem_norm_g", "norm_ffn_g", "final_norm_g")
BIG = ("w_in", "w_out", "xa_wq", "xa_wkv", "xa_wo", "ffn_w_gate_up", "ffn_w_down")
WEIGHTS = ("norm_mix_g", "w_in", "b_in", "conv_w", "conv_b", "conv_ln_g", "conv_ln_b", "gm_ln_g", "gm_ln_b", "gm_w_s",
           "gm_b_s", "w_out", "norm_xa_g", "mem_norm_g", "xa_wq", "xa_wkv", "xa_wo", "norm_ffn_g", "ffn_w_gate_up",
           "ffn_w_down", "final_norm_g")


def kernel(x, mem, norm_mix_g, w_in, b_in, conv_w, conv_b, conv_ln_g, conv_ln_b, gm_ln_g, gm_ln_b, gm_w_s, gm_b_s, w_out, norm_xa_g, mem_norm_g, xa_wq, xa_wkv, xa_wo, norm_ffn_g, ffn_w_gate_up, ffn_w_down, final_norm_g, loss_target, m_norm_mix_g, m_w_in, m_b_in, m_conv_w, m_conv_b, m_conv_ln_g, m_conv_ln_b, m_gm_ln_g, m_gm_ln_b, m_gm_w_s, m_gm_b_s, m_w_out, m_norm_xa_g, m_mem_norm_g, m_xa_wq, m_xa_wkv, m_xa_wo, m_norm_ffn_g, m_ffn_w_gate_up, m_ffn_w_down, m_final_norm_g, v_norm_mix_g, v_w_in, v_b_in, v_conv_w, v_conv_b, v_conv_ln_g, v_conv_ln_b, v_gm_ln_g, v_gm_ln_b, v_gm_w_s, v_gm_b_s, v_w_out, v_norm_xa_g, v_mem_norm_g, v_xa_wq, v_xa_wkv, v_xa_wo, v_norm_ffn_g, v_ffn_w_gate_up, v_ffn_w_down, v_final_norm_g):
    w = dict(norm_mix_g=norm_mix_g, w_in=w_in, b_in=b_in, conv_w=conv_w, conv_b=conv_b, conv_ln_g=conv_ln_g,
             conv_ln_b=conv_ln_b, gm_ln_g=gm_ln_g, gm_ln_b=gm_ln_b, gm_w_s=gm_w_s, gm_b_s=gm_b_s, w_out=w_out,
             norm_xa_g=norm_xa_g, mem_norm_g=mem_norm_g, xa_wq=xa_wq, xa_wkv=xa_wkv, xa_wo=xa_wo,
             norm_ffn_g=norm_ffn_g, ffn_w_gate_up=ffn_w_gate_up, ffn_w_down=ffn_w_down, final_norm_g=final_norm_g)
    mom = dict(norm_mix_g=m_norm_mix_g, w_in=m_w_in, b_in=m_b_in, conv_w=m_conv_w, conv_b=m_conv_b,
               conv_ln_g=m_conv_ln_g, conv_ln_b=m_conv_ln_b, gm_ln_g=m_gm_ln_g, gm_ln_b=m_gm_ln_b, gm_w_s=m_gm_w_s,
               gm_b_s=m_gm_b_s, w_out=m_w_out, norm_xa_g=m_norm_xa_g, mem_norm_g=m_mem_norm_g, xa_wq=m_xa_wq,
               xa_wkv=m_xa_wkv, xa_wo=m_xa_wo, norm_ffn_g=m_norm_ffn_g, ffn_w_gate_up=m_ffn_w_gate_up,
               ffn_w_down=m_ffn_w_down, final_norm_g=m_final_norm_g)
    var = dict(norm_mix_g=v_norm_mix_g, w_in=v_w_in, b_in=v_b_in, conv_w=v_conv_w, conv_b=v_conv_b,
               conv_ln_g=v_conv_ln_g, conv_ln_b=v_conv_ln_b, gm_ln_g=v_gm_ln_g, gm_ln_b=v_gm_ln_b, gm_w_s=v_gm_w_s,
               gm_b_s=v_gm_b_s, w_out=v_w_out, norm_xa_g=v_norm_xa_g, mem_norm_g=v_mem_norm_g, xa_wq=v_xa_wq,
               xa_wkv=v_xa_wkv, xa_wo=v_xa_wo, norm_ffn_g=v_norm_ffn_g, ffn_w_gate_up=v_ffn_w_gate_up,
               ffn_w_down=v_ffn_w_down, final_norm_g=v_final_norm_g)

    me = 4 * lax.axis_index("x") + 2 * lax.axis_index("y") + lax.axis_index("c")
    s, d = x.shape[1], x.shape[2]
    xs = x.reshape(s, d)
    mems = mem.reshape(mem.shape[1], d)
    tgt = loss_target.reshape(s, d)
    ts = min(512, s)
    ts_ffn = min(256, s)
    row = lambda a: a.reshape(1, -1)

    conv_w_pad = jnp.pad(conv_w, ((0, CONV_PAD - CONV_K), (0, 128 - conv_w.shape[1])))
    first_level = (SIBLING,) + OTHER_CHIPS
    ex_first = _exchange_start([_bf(w_in), conv_w_pad], [False] * 2, "gather_start_in", first_level, own_slot=me)
    behind = lambda t: _bf(t + ex_first[4][0:1, 0:1])
    shards = [behind(xa_wkv), behind(w_out), behind(xa_wq), behind(xa_wo), behind(ffn_w_gate_up.T), behind(ffn_w_down)]
    ex_rest = _exchange_start(shards, [False] * len(shards), "gather_start_rest", first_level, own_slot=me)
    g_send, g_recv, g_src, g_land = (list(ex_first[i]) + list(ex_rest[i]) for i in range(4))
    g_tok = ex_rest[4]

    def arrived(idx, after, name):
        pick = lambda seq: [seq[i] for i in idx]
        return _wait_and_forward(pick(g_send), pick(g_recv), pick(g_src), pick(g_land), after, "gather_pass_" + name,
                                 first_level)

    def complete(handle, after, name):
        return _forward_wait(handle[0], handle[1], handle[2], after, "forward_wait_" + name)

    bst = gm_b_s.T
    wst = jnp.swapaxes(gm_w_s, 1, 2)

    h_in = arrived((0, 1), g_tok, "in")
    w_in3, conv_w8 = complete(h_in, h_in[3], "in")
    conv_w_f = conv_w8[:, :CONV_K, :conv_w.shape[1]].transpose(1, 0, 2).reshape(CONV_K, -1)
    cw = conv_w_f.shape[1]
    z, a, tuv = _fwd_in(xs, row(norm_mix_g), w_in3, row(b_in), ts)
    h_out = arrived((2, 3), z, "out")
    c = _conv_fwd(a, conv_w_f + h_out[3][0:1, 0:1], row(conv_b))
    h_xa = arrived((4, 5), c, "xa")
    wkv3, w_out3 = complete(h_out, h_xa[3], "out")
    w_out_f = w_out3.reshape(-1, d)
    kv, mn = _kv_proj(mems, row(mem_norm_g), wkv3)
    h1 = _fwd_out(xs, c, z, tuv, row(conv_ln_g), row(conv_ln_b), row(gm_ln_g), row(gm_ln_b), gm_w_s, bst, w_out_f, ts)
    h_gut = arrived((6,), h1, "gate_up")
    wq3, wo3 = complete(h_xa, h_gut[3], "xa")
    wq_f = wq3.reshape(-1, d)
    wo_f = wo3.reshape(-1, d)
    h2, qb, ob = _fwd_xa(h1, row(norm_xa_g), wq_f, kv, wo_f, ts)
    h_down = arrived((7,), h2, "down")
    (wgut3,) = complete(h_gut, h_down[3], "gate_up")
    (wdown3,) = complete(h_down, wgut3, "down")
    wgut_f = wgut3.reshape(-1, d)
    wdown_f = wdown3.reshape(-1, d)
    dh3, dh3b, gu, hn2, loss_p, d_final_g = _fwd_ffn(h2, row(norm_ffn_g), wgut_f, wdown_f, row(final_norm_g), tgt,
                                                     ts_ffn)

    blocks = lambda m: m.reshape(N_DEV, -1, d)
    tok = lambda ex: ex[4][0:1, 0:1]
    me1 = me.astype(jnp.int32).reshape(1)

    def chip_sums(pair, after, tag):
        parts, handed = _pair_wait(pair[0], pair[1], pair[2], pair[3], after, "pair_wait_" + tag)
        return [_chip_sum(p, h, me1, "chip_sum_%s_%d" % (tag, i)) for i, (p, h) in enumerate(zip(parts, handed))]

    dh2, act, dgu, d_ffn_g = _bwd_ffn(h2, dh3, gu, row(norm_ffn_g), wgut_f, wdown_f, ts_ffn)
    dwgut = _tn_matmul(dgu, hn2, 2816, "dw_gate_up")
    dwdown = _tn_matmul(act, dh3b, 2816, "dw_down")
    pair_ffn = _pair_start([blocks(dwgut), blocks(dwdown)], "pair_start_ffn")
    dh1, dwq, dwo, dkv, d_xa_g = _bwd_xa(h1, dh2, qb, ob, row(norm_xa_g) + tok(pair_ffn), wq_f, wo_f, kv, ts)
    ex_ffn = _exchange_start(chip_sums(pair_ffn, dh1, "ffn"), [True] * 2, "scatter_start_ffn", OTHER_CHIPS)
    dwkv3, d_mem_g = _bwd_kv(dkv, mn, mems, row(mem_norm_g) + tok(ex_ffn), wkv3)
    pair_xa = _pair_start([blocks(dwq), blocks(dwo), dwkv3], "pair_start_xa")
    (dwout, dc, dzuv, dws, dbst, d_cln_g, d_cln_b, d_gln_g, d_gln_b, dbin_uv) = _bwd_out(
        dh1, c, z, tuv, row(conv_ln_g) + tok(pair_xa), row(conv_ln_b), row(gm_ln_g), row(gm_ln_b), gm_w_s, wst, bst,
        w_out_f, ts)
    early = dict(b_in_uv=dbin_uv, conv_ln_g=d_cln_g, conv_ln_b=d_cln_b, gm_ln_g=d_gln_g, gm_ln_b=d_gln_b,
                 gm_b_s=dbst.T, norm_xa_g=d_xa_g, mem_norm_g=d_mem_g, norm_ffn_g=d_ffn_g, final_norm_g=d_final_g,
                 loss=loss_p)
    dws_b = _bf(dws).reshape(GM_HEADS * CHUNK, CHUNK)
    ex_xa = _exchange_start(chip_sums(pair_xa, dc, "xa") + [blocks(dwout), dws_b, _pack(list(early.values()))],
                            [True] * 4 + [False] * 2, "scatter_start_xa", [OTHER_CHIPS] * 3 + [ALL_PEERS] * 3)
    dx, dw_in3, dconv_w, dconv_b, dbin_ag, d_mix_g = _bwd_in(xs, dh1, dc, a, z, dzuv, row(norm_mix_g) + tok(ex_xa),
                                                             w_in3, conv_w_f, ts)
    late = dict(norm_mix_g=d_mix_g, b_in_ag=dbin_ag, conv_w=dconv_w[:CONV_K], conv_b=dconv_b)
    ex_last = _exchange_start([_pack(list(late.values())), dw_in3], [False, True], "scatter_start_in")

    grads, delta, new_m, new_v = {}, {}, {}, {}

    def waited(ex, idx, scatter, after, name, masks=ALL_PEERS):
        return _exchange_wait(*[[ex[i][k] for k in idx] for i in range(4)], scatter, after, name, masks)

    def reduced(names, srcs, lands, masks):
        for nm, src, land, mk in zip(names, srcs, lands, masks):
            view = (lambda t: t.T) if nm == "ffn_w_gate_up" else (lambda t: t)
            upd = _sum_adamw(land, src, view(w[nm]), view(mom[nm]), view(var[nm]), me1, "adamw_" + nm, mk)
            grads[nm], delta[nm], new_m[nm], new_v[nm] = (view(t) for t in upd)
        return new_v[names[-1]]

    after = reduced(("ffn_w_gate_up", "ffn_w_down"),
                    *waited(ex_ffn, (0, 1), [True] * 2, ex_last[4], "scatter_wait_ffn", OTHER_CHIPS), [OTHER_CHIPS] * 2)
    xa_masks = [OTHER_CHIPS] * 3 + [ALL_PEERS]
    after = reduced(("xa_wq", "xa_wo", "xa_wkv", "w_out"),
                    *waited(ex_xa, (0, 1, 2, 3), [True] * 4, after, "scatter_wait_xa", xa_masks), xa_masks)

    (dws_src, early_src), (dws_land, early_land) = waited(ex_xa, (4, 5), [False] * 2, after, "gather_wait_early")
    (late_src,), (late_land,) = waited(ex_last, (0,), [False], early_land, "gather_wait_late")
    dws_tot = _sum8(dws_land, dws_src, me1, "sum_gm_w_s")
    early_tot = _sum8(early_land, early_src, me1, "sum_small_early")
    late_tot = _sum8(late_land, late_src, me1, "sum_small_late")
    early_g = dict(zip(early, _unpack(early_tot, [v.shape for v in early.values()])), gm_w_s=dws_tot)
    late_g = dict(zip(late, _unpack(late_tot, [v.shape for v in late.values()])))
    loss = early_g["loss"][0, 0]
    for nm in SMALL:
        if nm == "b_in":
            g = jnp.concatenate([late_g["b_in_ag"], early_g["b_in_uv"]], axis=1)
        elif nm == "conv_w":
            g = lax.dynamic_slice_in_dim(late_g[nm], me * conv_w.shape[1], conv_w.shape[1], axis=1)
        else:
            g = late_g[nm] if nm in late_g else early_g[nm]
        grads[nm] = g.reshape(w[nm].shape)

    small = [[src[nm] for nm in SMALL] for src in (w, grads, mom, var)]
    for dst, vals in zip((delta, new_m, new_v), _adamw_many(*small, "adamw_small")):
        dst.update(zip(SMALL, vals))

    reduced(("w_in",), *waited(ex_last, (1,), [True], delta[SMALL[0]], "scatter_wait_in"), [ALL_PEERS])

    return (loss, dx.reshape(x.shape), *[grads[nm] for nm in WEIGHTS], *[delta[nm] for nm in WEIGHTS],
            *[new_m[nm] for nm in WEIGHTS], *[new_v[nm] for nm in WEIGHTS])
```

```python
import jax
import jax.numpy as jnp
from jax import lax
from jax.experimental import pallas as pl
from jax.experimental.pallas import tpu as pltpu

F32 = jnp.float32
BF16 = jnp.bfloat16
SDS = jax.ShapeDtypeStruct

N_DEV = 8
RMS_EPS = 1e-6
LN_EPS = 1e-5
CONV_K = 31
CONV_PAD = 32
CHUNK = 128
GM_HEADS = 8
XA_HEADS = 4
XA_DH = 256
GELU_K0 = 0.7978845608028654
GELU_K1 = 0.044715
ADAM_LR = 0.001
ADAM_B1 = 0.9
ADAM_B2 = 0.999
ADAM_EPS = 1e-08
ADAM_WD = 0.01
ADAM_STEP = 10
VMEM_LIMIT = 60 * 1024 * 1024

NN = (((1,), (0,)), ((), ()))
NT = (((1,), (1,)), ((), ()))
TN = (((0,), (0,)), ((), ()))


def _dot(a, b, dims=NN):
    return lax.dot_general(a, b, dims, preferred_element_type=F32)


def _bf(x):
    return x.astype(BF16)


def _cparams(*sem):
    return pltpu.CompilerParams(dimension_semantics=tuple(sem) if sem else None, vmem_limit_bytes=VMEM_LIMIT)


def _row(ts, w, col=0):
    return pl.BlockSpec((ts, w), lambda i: (i, col))


def _const(shape):
    nd = len(shape)
    return pl.BlockSpec(shape, lambda i: (0,) * nd, pipeline_mode=pl.Buffered(1))


def _acc(shape):
    nd = len(shape)
    return pl.BlockSpec(shape, lambda i: (0,) * nd)


def _rms_fwd(x, g):
    r = lax.rsqrt(jnp.mean(x * x, axis=-1, keepdims=True) + RMS_EPS)
    xh = x * r
    return xh * g, xh, r


def _rms_bwd(dy, xh, r, g):
    gdy = dy * g
    dx = r * (gdy - xh * jnp.mean(gdy * xh, axis=-1, keepdims=True))
    dg = jnp.sum(dy * xh, axis=0, keepdims=True)
    return dx, dg


def _ln_fwd(x, g, b):
    mu = jnp.mean(x, axis=-1, keepdims=True)
    xc = x - mu
    rs = lax.rsqrt(jnp.mean(xc * xc, axis=-1, keepdims=True) + LN_EPS)
    xh = xc * rs
    return xh * g + b, xh, rs


def _ln_bwd(dy, xh, rs, g):
    dxh = dy * g
    dx = rs * (dxh - jnp.mean(dxh, axis=-1, keepdims=True) - xh * jnp.mean(dxh * xh, axis=-1, keepdims=True))
    return dx, jnp.sum(dy * xh, axis=0, keepdims=True), jnp.sum(dy, axis=0, keepdims=True)


def _gelu(x):
    t = jnp.tanh(GELU_K0 * (x + GELU_K1 * (x * x * x)))
    return 0.5 * x * (1.0 + t), t


def _gelu_grad(x, t):
    return 0.5 * (1.0 + t) + 0.5 * x * (1.0 - t * t) * (GELU_K0 * (1.0 + 3.0 * GELU_K1 * x * x))


def _silu_grad(x, sg):
    return sg * (1.0 + x * (1.0 - sg))


def _by_residue(offsets):
    groups = {}
    for off in offsets:
        groups.setdefault(off % 8, []).append(off)
    return [(res, sorted(offs)) for res, offs in sorted(groups.items())]


def _accumulate(ref, val):
    @pl.when(pl.program_id(0) == 0)
    def _():
        ref[...] = jnp.zeros_like(ref)
    ref[...] += val


def _mix_masks():
    row = lax.broadcasted_iota(jnp.int32, (CHUNK, CHUNK), 0)
    col = lax.broadcasted_iota(jnp.int32, (CHUNK, CHUNK), 1)
    return row >= col, row <= col, col < (CHUNK // 2)


def _mix_fwd(vb, ws_ref, bst_ref, mixed_scr, ts):
    tril, _, lo = _mix_masks()
    for j in range(GM_HEADS // 2):
        w0 = _bf(jnp.where(tril, ws_ref[2 * j], 0.0))
        w1 = _bf(jnp.where(tril, ws_ref[2 * j + 1], 0.0))
        bias = jnp.where(lo, bst_ref[:, 2 * j:2 * j + 1], bst_ref[:, 2 * j + 1:2 * j + 2])
        for n in range(ts // CHUNK):
            v = vb[n * CHUNK:(n + 1) * CHUNK, j * 128:(j + 1) * 128]
            mixed_scr[n * CHUNK:(n + 1) * CHUNK, j * 128:(j + 1) * 128] = jnp.where(lo, _dot(w0, v), _dot(w1, v)) + bias


def _peer_of(mask):
    x, y, c = lax.axis_index("x"), lax.axis_index("y"), lax.axis_index("c")
    px = 1 - x if (mask >> 2) & 1 else x
    py = 1 - y if (mask >> 1) & 1 else y
    pc = 1 - c if mask & 1 else c
    return (px, py, pc), 4 * px + 2 * py + pc


ALL_PEERS = tuple(range(1, N_DEV))
OTHER_CHIPS = (2, 4, 6)
CHIPS = (0,) + OTHER_CHIPS
SIBLING = 1


def _split_copy(src_ref, land_ref, send_sem, recv_sem, mask, slot, scatter, outgoing):
    x, y, c = lax.axis_index("x"), lax.axis_index("y"), lax.axis_index("c")
    me = 4 * x + 2 * y + c
    peer, pidx = _peer_of(mask)
    return pltpu.make_async_remote_copy(
        src_ref=src_ref.at[pidx] if scatter else src_ref,
        dst_ref=land_ref.at[me if outgoing else pidx],
        send_sem=send_sem.at[slot], recv_sem=recv_sem.at[slot],
        device_id=peer, device_id_type=pl.DeviceIdType.MESH)


_HBM = pl.BlockSpec(memory_space=pltpu.HBM)
_SEM = pl.BlockSpec(memory_space=pltpu.SEMAPHORE)
_EFFECT = pltpu.SideEffectType.DATAFLOW_SIDE_EFFECTING


def _per_array(masks, n):
    return [tuple(masks)] * n if isinstance(masks[0], int) else [tuple(m) for m in masks]


def _exchange_start(srcs, scatter, name, masks=ALL_PEERS, own_slot=None):
    n = len(srcs)
    masks = _per_array(masks, n)
    lands = [lax.empty((N_DEV,) + tuple(s.shape[1:] if sc else s.shape), s.dtype) for s, sc in zip(srcs, scatter)]
    if own_slot is not None:
        lands = [land if sc else _with_own(land, s, own_slot) for land, s, sc in zip(lands, srcs, scatter)]
    lands = [pltpu.with_memory_space_constraint(land, pltpu.HBM) for land in lands]
    srcs = [pltpu.with_memory_space_constraint(s, pltpu.HBM) for s in srcs]

    def body(*refs):
        src_refs, land_refs = refs[:n], refs[n:2 * n]
        send_sems, recv_sems = refs[2 * n:3 * n], refs[3 * n:4 * n]
        token = refs[-1]
        for k in range(n):
            for slot, mask in enumerate(masks[k]):
                _split_copy(src_refs[k], land_refs[k], send_sems[k], recv_sems[k], mask, slot, scatter[k], True).start()
        token[...] = jnp.zeros_like(token)

    sems = [pltpu.SemaphoreType.DMA((len(m),)) for m in masks]
    out = pl.pallas_call(
        body, name=name,
        out_shape=tuple(sems + sems + [pltpu.HBM(s.shape, s.dtype) for s in srcs]
                        + [pltpu.HBM(l.shape, l.dtype) for l in lands] + [SDS((8, 128), F32)]),
        in_specs=[_HBM] * (2 * n),
        out_specs=tuple([_SEM] * (2 * n) + [_HBM] * (2 * n) + [pl.BlockSpec(memory_space=pltpu.VMEM)]),
        input_output_aliases={i: 2 * n + i for i in range(2 * n)},
        compiler_params=pltpu.CompilerParams(has_side_effects=_EFFECT),
    )(*srcs, *lands)
    return out[:n], out[n:2 * n], out[2 * n:3 * n], out[3 * n:4 * n], out[-1]


def _exchange_wait(send_sems, recv_sems, srcs_thru, lands_thru, scatter, after, name, masks=ALL_PEERS):
    n = len(srcs_thru)
    masks = _per_array(masks, n)

    def body(*refs):
        src_refs, land_refs = refs[:n], refs[n:2 * n]
        send_refs, recv_refs = refs[2 * n:3 * n], refs[3 * n:4 * n]
        for k in range(n):
            for slot, mask in enumerate(masks[k]):
                args = (src_refs[k], land_refs[k], send_refs[k], recv_refs[k], mask, slot, scatter[k])
                _split_copy(*args, True).wait_send()
                _split_copy(*args, False).wait_recv()

    out = pl.pallas_call(
        body, name=name,
        out_shape=tuple([pltpu.HBM(s.shape, s.dtype) for s in srcs_thru]
                        + [pltpu.HBM(l.shape, l.dtype) for l in lands_thru]),
        in_specs=[_HBM] * (2 * n) + [_SEM] * (2 * n) + [pl.BlockSpec(memory_space=pl.ANY)],
        out_specs=tuple([_HBM] * (2 * n)),
        input_output_aliases={i: i for i in range(2 * n)},
        compiler_params=pltpu.CompilerParams(has_side_effects=_EFFECT),
    )(*srcs_thru, *lands_thru, *send_sems, *recv_sems, after)
    return out[:n], out[n:]


def _wait_and_forward(send_sems, recv_sems, srcs_thru, lands_thru, after, name, masks):
    n = len(lands_thru)

    def body(*refs):
        src_refs, land_refs = refs[:n], refs[n:2 * n]
        send_refs, recv_refs = refs[2 * n:3 * n], refs[3 * n:4 * n]
        outs = refs[4 * n + 1:]
        fsend, frecv, token = outs[2 * n:3 * n], outs[3 * n:4 * n], outs[-1]
        for k in range(n):
            for slot, mask in enumerate(masks):
                args = (src_refs[k], land_refs[k], send_refs[k], recv_refs[k], mask, slot, False)
                _split_copy(*args, True).wait_send()
                _split_copy(*args, False).wait_recv()
        sibling, _ = _peer_of(SIBLING)
        for k in range(n):
            for slot, mask in enumerate(OTHER_CHIPS):
                _, mine = _peer_of(mask)
                pltpu.make_async_remote_copy(
                    src_ref=land_refs[k].at[mine], dst_ref=land_refs[k].at[mine], send_sem=fsend[k].at[slot],
                    recv_sem=frecv[k].at[slot], device_id=sibling, device_id_type=pl.DeviceIdType.MESH).start()
        token[...] = jnp.zeros_like(token)

    sem = pltpu.SemaphoreType.DMA((len(OTHER_CHIPS),))
    out = pl.pallas_call(
        body, name=name,
        out_shape=tuple([pltpu.HBM(s.shape, s.dtype) for s in srcs_thru] + [pltpu.HBM(l.shape, l.dtype) for l in lands_thru]
                        + [sem] * (2 * n) + [SDS((8, 128), F32)]),
        in_specs=[_HBM] * (2 * n) + [_SEM] * (2 * n) + [pl.BlockSpec(memory_space=pl.ANY)],
        out_specs=tuple([_HBM] * (2 * n) + [_SEM] * (2 * n) + [pl.BlockSpec(memory_space=pltpu.VMEM)]),
        input_output_aliases={i: i for i in range(2 * n)},
        compiler_params=pltpu.CompilerParams(has_side_effects=_EFFECT),
    )(*srcs_thru, *lands_thru, *send_sems, *recv_sems, after)
    return out[2 * n:3 * n], out[3 * n:4 * n], out[n:2 * n], out[-1]


def _forward_wait(send_sems, recv_sems, lands, after, name):
    n = len(lands)

    def body(*refs):
        land_refs, send_refs, recv_refs = refs[:n], refs[n:2 * n], refs[2 * n:3 * n]
        sibling, _ = _peer_of(SIBLING)
        for k in range(n):
            for slot, mask in enumerate(OTHER_CHIPS):
                _, mine = _peer_of(mask)
                _, theirs = _peer_of(mask | SIBLING)
                for block, wait in ((mine, "wait_send"), (theirs, "wait_recv")):
                    getattr(pltpu.make_async_remote_copy(
                        src_ref=land_refs[k].at[block], dst_ref=land_refs[k].at[block],
                        send_sem=send_refs[k].at[slot], recv_sem=recv_refs[k].at[slot], device_id=sibling,
                        device_id_type=pl.DeviceIdType.MESH), wait)()

    return pl.pallas_call(
        body, name=name, out_shape=tuple(pltpu.HBM(l.shape, l.dtype) for l in lands),
        in_specs=[_HBM] * n + [_SEM] * (2 * n) + [pl.BlockSpec(memory_space=pl.ANY)],
        out_specs=tuple([_HBM] * n), input_output_aliases={i: i for i in range(n)},
        compiler_params=pltpu.CompilerParams(has_side_effects=_EFFECT),
    )(*lands, *send_sems, *recv_sems, after)


def _pair_copy(parts_ref, land_ref, send_sem, recv_sem, slot, chip, outgoing):
    sibling, _ = _peer_of(SIBLING)
    _, block = _peer_of(chip | SIBLING if outgoing else chip)
    return pltpu.make_async_remote_copy(
        src_ref=parts_ref.at[block], dst_ref=land_ref.at[block], send_sem=send_sem.at[slot], recv_sem=recv_sem.at[slot],
        device_id=sibling, device_id_type=pl.DeviceIdType.MESH)


def _pair_start(parts, name):
    n = len(parts)
    lands = [pltpu.with_memory_space_constraint(lax.empty(p.shape, p.dtype), pltpu.HBM) for p in parts]
    parts = [pltpu.with_memory_space_constraint(p, pltpu.HBM) for p in parts]

    def body(*refs):
        part_refs, land_refs = refs[:n], refs[n:2 * n]
        send_sems, recv_sems, token = refs[2 * n:3 * n], refs[3 * n:4 * n], refs[-1]
        for k in range(n):
            for slot, chip in enumerate(CHIPS):
                _pair_copy(part_refs[k], land_refs[k], send_sems[k], recv_sems[k], slot, chip, True).start()
        token[...] = jnp.zeros_like(token)

    sem = pltpu.SemaphoreType.DMA((len(CHIPS),))
    out = pl.pallas_call(
        body, name=name,
        out_shape=tuple([sem] * (2 * n) + [pltpu.HBM(p.shape, p.dtype) for p in parts + lands] + [SDS((8, 128), F32)]),
        in_specs=[_HBM] * (2 * n),
        out_specs=tuple([_SEM] * (2 * n) + [_HBM] * (2 * n) + [pl.BlockSpec(memory_space=pltpu.VMEM)]),
        input_output_aliases={i: 2 * n + i for i in range(2 * n)},
        compiler_params=pltpu.CompilerParams(has_side_effects=_EFFECT),
    )(*parts, *lands)
    return out[:n], out[n:2 * n], out[2 * n:3 * n], out[3 * n:4 * n], out[-1]


def _pair_wait(send_sems, recv_sems, parts_thru, lands_thru, after, name):
    n = len(parts_thru)

    def body(*refs):
        part_refs, land_refs = refs[:n], refs[n:2 * n]
        send_refs, recv_refs = refs[2 * n:3 * n], refs[3 * n:4 * n]
        for k in range(n):
            for slot, chip in enumerate(CHIPS):
                args = (part_refs[k], land_refs[k], send_refs[k], recv_refs[k], slot, chip)
                _pair_copy(*args, True).wait_send()
                _pair_copy(*args, False).wait_recv()

    out = pl.pallas_call(
        body, name=name,
        out_shape=tuple(pltpu.HBM(p.shape, p.dtype) for p in list(parts_thru) + list(lands_thru)),
        in_specs=[_HBM] * (2 * n) + [_SEM] * (2 * n) + [pl.BlockSpec(memory_space=pl.ANY)],
        out_specs=tuple([_HBM] * (2 * n)), input_output_aliases={i: i for i in range(2 * n)},
        compiler_params=pltpu.CompilerParams(has_side_effects=_EFFECT),
    )(*parts_thru, *lands_thru, *send_sems, *recv_sems, after)
    return out[:n], out[n:]


def _with_own(landed, own, me):
    return lax.dynamic_update_slice_in_dim(landed, own[None], me, axis=0)


def _kv_proj(mem, g_mem, wkv3):
    m = mem.shape[0]

    def body(mem_ref, g_ref, w_ref, kv_ref, mn_ref):
        y, _, _ = _rms_fwd(mem_ref[...], g_ref[...])
        yb = _bf(y)
        mn_ref[...] = yb
        for b in range(N_DEV):
            kv_ref[:, 256 * b:256 * (b + 1)] = _dot(yb, w_ref[b])

    return pl.pallas_call(body, name="kv_proj", out_shape=(SDS((m, 2048), F32), SDS(mem.shape, BF16)),
                          compiler_params=_cparams())(mem, g_mem, wkv3)


def _fwd_in(x, g_mix, w_in3, b_in, ts):
    s, d = x.shape

    def body(x_ref, g_ref, w_ref, b_ref, zuv_ref, a_ref, zag):
        hn, _, _ = _rms_fwd(x_ref[...], g_ref[...])
        hb = _bf(hn)
        for b in range(N_DEV):
            dst = zag if b < N_DEV // 2 else zuv_ref
            col = 256 * (b % (N_DEV // 2))
            dst[:, col:col + 256] = _dot(hb, w_ref[b]) + b_ref[:, 256 * b:256 * (b + 1)]
        a_ref[...] = zag[:, 0:512] * jax.nn.sigmoid(zag[:, 512:1024])

    return pl.pallas_call(
        body, name="fwd_in", grid=(s // ts,),
        in_specs=[_row(ts, d), _const(g_mix.shape), _const(w_in3.shape), _const(b_in.shape)],
        out_specs=(_row(ts, 1024), _row(ts, 512)),
        out_shape=(SDS((s, 1024), F32), SDS((s, 512), F32)),
        scratch_shapes=[pltpu.VMEM((ts, 1024), F32)],
        compiler_params=_cparams("arbitrary"))(x, g_mix, w_in3, b_in)


def _conv_fwd(a, w, b):
    s, cw = a.shape
    rc = 256 if s % 256 == 0 else 128

    def body(a_ref, w_ref, b_ref, c_ref, pad):
        pad[0:CONV_PAD, :] = jnp.zeros((CONV_PAD, 128), F32)
        pad[CONV_PAD:, :] = a_ref[...]

        def chunk(i, carry):
            r0 = pl.multiple_of(i * rc, rc)
            acc = jnp.zeros((rc, 128), F32) + b_ref[...]
            for res, offs in _by_residue(range(CONV_PAD - CONV_K + 1, CONV_PAD + 1)):
                shifted = pad[pl.ds(r0 + res, rc + offs[-1] - res), :]
                for off in offs:
                    k = off - (CONV_PAD - CONV_K + 1)
                    acc = acc + w_ref[k:k + 1, :] * shifted[off - res:off - res + rc, :]
            c_ref[pl.ds(r0, rc), :] = acc
            return carry

        lax.fori_loop(0, s // rc, chunk, 0)

    blk = lambda r: pl.BlockSpec((r, 128), lambda j: (0, j))
    return pl.pallas_call(
        body, name="conv_fwd", grid=(cw // 128,),
        in_specs=[blk(s), blk(CONV_K), blk(1)], out_specs=blk(s), out_shape=SDS((s, cw), F32),
        scratch_shapes=[pltpu.VMEM((s + CONV_PAD, 128), F32)],
        compiler_params=_cparams("arbitrary"))(a, w, b)


def _fwd_out(x, c, zuv, cln_g, cln_b, gln_g, gln_b, ws, bst, w_out, ts):
    s, d = x.shape

    def body(x_ref, c_ref, zuv_ref, clg, clb, glg, glb, ws_ref, bst_ref, wo_ref, h1_ref, mixed_scr):
        cl, _, _ = _ln_fwd(c_ref[...], clg[...], clb[...])
        co = cl * jax.nn.sigmoid(cl)
        u, _ = _gelu(zuv_ref[:, 0:512])
        vg, _ = _gelu(zuv_ref[:, 512:1024])
        vln, _, _ = _ln_fwd(vg, glg[...], glb[...])
        _mix_fwd(_bf(vln), ws_ref, bst_ref, mixed_scr, ts)
        gm = u * mixed_scr[...]
        h1_ref[...] = x_ref[...] + _dot(_bf(co), wo_ref[0:512, :]) + _dot(_bf(gm), wo_ref[512:1024, :])

    return pl.pallas_call(
        body, name="fwd_out", grid=(s // ts,),
        in_specs=[_row(ts, d), _row(ts, 512), _row(ts, 1024), _const(cln_g.shape), _const(cln_b.shape),
                  _const(gln_g.shape), _const(gln_b.shape), _const(ws.shape), _const(bst.shape), _const(w_out.shape)],
        out_specs=_row(ts, d), out_shape=SDS((s, d), F32),
        scratch_shapes=[pltpu.VMEM((ts, 512), F32)],
        compiler_params=_cparams("arbitrary"))(x, c, zuv, cln_g, cln_b, gln_g, gln_b, ws, bst, w_out)


def _softmax_rows(sc):
    m = jnp.max(sc, axis=-1, keepdims=True)
    e = jnp.exp(sc - m)
    return e / jnp.sum(e, axis=-1, keepdims=True)


def _fwd_xa(h1, g_xa, wq, kv, wo, ts):
    s, d = h1.shape
    scale = XA_DH ** -0.5

    def body(h_ref, g_ref, wq_ref, kv_ref, wo_ref, h2_ref, o_scr):
        hn, _, _ = _rms_fwd(h_ref[...], g_ref[...])
        q = _dot(_bf(hn), wq_ref[...])
        for h in range(XA_HEADS):
            qh = _bf(q[:, XA_DH * h:XA_DH * (h + 1)])
            kh = _bf(kv_ref[:, XA_DH * h:XA_DH * (h + 1)])
            vh = _bf(kv_ref[:, d + XA_DH * h:d + XA_DH * (h + 1)])
            p = _softmax_rows(_dot(qh, kh, NT) * scale)
            o_scr[:, XA_DH * h:XA_DH * (h + 1)] = _dot(_bf(p), vh)
        h2_ref[...] = h_ref[...] + _dot(_bf(o_scr[...]), wo_ref[...])

    return pl.pallas_call(
        body, name="fwd_xa", grid=(s // ts,),
        in_specs=[_row(ts, d), _const(g_xa.shape), _const(wq.shape), _const(kv.shape), _const(wo.shape)],
        out_specs=_row(ts, d), out_shape=SDS((s, d), F32),
        scratch_shapes=[pltpu.VMEM((ts, d), F32)],
        compiler_params=_cparams("arbitrary"))(h1, g_xa, wq, kv, wo)


def _fwd_ffn(h2, g_ffn, wgut, wdown, g_final, target, ts):
    s, d = h2.shape
    hid = wdown.shape[0]
    hc = hid // 2

    def body(h_ref, g_ref, wgu_ref, wd_ref, gf_ref, t_ref, dh3_ref, dh3b_ref, gu_ref, hn_ref, loss_ref, dgf_ref):
        hn, _, _ = _rms_fwd(h_ref[...], g_ref[...])
        hb = _bf(hn)
        hn_ref[...] = hb
        h3 = h_ref[...]
        for n in range(2):
            g = _dot(hb, wgu_ref[hc * n:hc * (n + 1), :], NT)
            u = _dot(hb, wgu_ref[hid + hc * n:hid + hc * (n + 1), :], NT)
            gu_ref[:, hc * n:hc * (n + 1)] = g
            gu_ref[:, hid + hc * n:hid + hc * (n + 1)] = u
            act = g * jax.nn.sigmoid(g) * u
            h3 = h3 + _dot(_bf(act), wd_ref[hc * n:hc * (n + 1), :])
        y, xh, r = _rms_fwd(h3, gf_ref[...])
        diff = y - t_ref[...]
        part = 0.5 * jnp.sum(jnp.mean(diff * diff, axis=-1, keepdims=True), axis=0, keepdims=True)
        _accumulate(loss_ref, jnp.zeros(loss_ref.shape, F32) + part)
        dh3, dgf = _rms_bwd(diff * (1.0 / d), xh, r, gf_ref[...])
        dh3_ref[...] = dh3
        dh3b_ref[...] = _bf(dh3)
        _accumulate(dgf_ref, dgf)

    return pl.pallas_call(
        body, name="fwd_ffn", grid=(s // ts,),
        in_specs=[_row(ts, d), _const(g_ffn.shape), _const(wgut.shape), _const(wdown.shape), _const(g_final.shape),
                  _row(ts, d)],
        out_specs=(_row(ts, d), _row(ts, d), _row(ts, 2 * hid), _row(ts, d), _acc((1, 128)), _acc((1, d))),
        out_shape=(SDS((s, d), F32), SDS((s, d), BF16), SDS((s, 2 * hid), F32), SDS((s, d), BF16), SDS((1, 128), F32),
                   SDS((1, d), F32)),
        compiler_params=_cparams("arbitrary"))(h2, g_ffn, wgut, wdown, g_final, target)


def _bwd_ffn(h2, dh3, gu, g_ffn, wgut, wdown, ts):
    s, d = h2.shape
    hid = wdown.shape[0]
    hc = hid // 2

    def body(h_ref, dh3_ref, gu_ref, g_ref, wgu_ref, wd_ref, dh2_ref, act_ref, dgu_ref, dg_ref):
        _, xh, r = _rms_fwd(h_ref[...], g_ref[...])
        db = _bf(dh3_ref[...])
        dhn = jnp.zeros((ts, d), F32)
        for n in range(2):
            wg = wgu_ref[hc * n:hc * (n + 1), :]
            wu = wgu_ref[hid + hc * n:hid + hc * (n + 1), :]
            g = gu_ref[:, hc * n:hc * (n + 1)]
            u = gu_ref[:, hid + hc * n:hid + hc * (n + 1)]
            sg = jax.nn.sigmoid(g)
            sl = g * sg
            act_ref[:, hc * n:hc * (n + 1)] = _bf(sl * u)
            dact = _dot(db, wd_ref[hc * n:hc * (n + 1), :], NT)
            dgb = _bf(dact * u * _silu_grad(g, sg))
            dub = _bf(dact * sl)
            dgu_ref[:, hc * n:hc * (n + 1)] = dgb
            dgu_ref[:, hid + hc * n:hid + hc * (n + 1)] = dub
            dhn = dhn + _dot(dgb, wg) + _dot(dub, wu)
        dx, dg = _rms_bwd(dhn, xh, r, g_ref[...])
        dh2_ref[...] = dh3_ref[...] + dx
        _accumulate(dg_ref, dg)

    return pl.pallas_call(
        body, name="bwd_ffn", grid=(s // ts,),
        in_specs=[_row(ts, d), _row(ts, d), _row(ts, 2 * hid), _const(g_ffn.shape), _const(wgut.shape),
                  _const(wdown.shape)],
        out_specs=(_row(ts, d), _row(ts, hid), _row(ts, 2 * hid), _acc((1, d))),
        out_shape=(SDS((s, d), F32), SDS((s, hid), BF16), SDS((s, 2 * hid), BF16), SDS((1, d), F32)),
        compiler_params=_cparams("arbitrary"))(h2, dh3, gu, g_ffn, wgut, wdown)


def _bwd_xa(h1, dh2, g_xa, wq, wo, kv, ts):
    s, d = h1.shape
    scale = XA_DH ** -0.5

    def body(h_ref, dh2_ref, g_ref, wq_ref, wo_ref, kv_ref, dh1_ref, dwq_ref, dwo_ref, dkv_ref, dg_ref,
             dq_scr, o_scr, accq, acco):
        hn, xh, r = _rms_fwd(h_ref[...], g_ref[...])
        hb = _bf(hn)
        q = _dot(hb, wq_ref[...])
        dh2b = _bf(dh2_ref[...])
        do = _dot(dh2b, wo_ref[...], NT)

        @pl.when(pl.program_id(0) == 0)
        def _():
            dkv_ref[...] = jnp.zeros_like(dkv_ref)
            accq[...] = jnp.zeros_like(accq)
            acco[...] = jnp.zeros_like(acco)

        for h in range(XA_HEADS):
            lo, hi = XA_DH * h, XA_DH * (h + 1)
            qh = _bf(q[:, lo:hi])
            kh = _bf(kv_ref[:, lo:hi])
            vh = _bf(kv_ref[:, d + lo:d + hi])
            p = _softmax_rows(_dot(qh, kh, NT) * scale)
            pb = _bf(p)
            o_scr[:, lo:hi] = _bf(_dot(pb, vh))
            doh = _bf(do[:, lo:hi])
            dp = _dot(doh, vh, NT)
            ds = p * (dp - jnp.sum(p * dp, axis=-1, keepdims=True)) * scale
            dsb = _bf(ds)
            dq_scr[:, lo:hi] = _dot(dsb, kh)
            dkv_ref[:, lo:hi] += _dot(dsb, qh, TN)
            dkv_ref[:, d + lo:d + hi] += _dot(pb, doh, TN)
        dqb = _bf(dq_scr[...])
        accq[...] += _dot(hb, dqb, TN)
        acco[...] += _dot(o_scr[...], dh2b, TN)
        dx, dg = _rms_bwd(_dot(dqb, wq_ref[...], NT), xh, r, g_ref[...])
        dh1_ref[...] = dh2_ref[...] + dx
        _accumulate(dg_ref, dg)

        @pl.when(pl.program_id(0) == pl.num_programs(0) - 1)
        def _():
            dwq_ref[...] = _bf(accq[...])
            dwo_ref[...] = _bf(acco[...])

    return pl.pallas_call(
        body, name="bwd_xa", grid=(s // ts,),
        in_specs=[_row(ts, d), _row(ts, d), _const(g_xa.shape), _const(wq.shape), _const(wo.shape), _const(kv.shape)],
        out_specs=(_row(ts, d), _acc((d, d)), _acc((d, d)), _acc(kv.shape), _acc((1, d))),
        out_shape=(SDS((s, d), F32), SDS((d, d), BF16), SDS((d, d), BF16), SDS(kv.shape, F32), SDS((1, d), F32)),
        scratch_shapes=[pltpu.VMEM((ts, d), F32), pltpu.VMEM((ts, d), BF16), pltpu.VMEM((d, d), F32),
                        pltpu.VMEM((d, d), F32)],
        compiler_params=_cparams("arbitrary"))(h1, dh2, g_xa, wq, wo, kv)


def _bwd_kv(dkv, mn, mem, g_mem, wkv3):
    d = mem.shape[1]

    def body(dkv_ref, mn_ref, mem_ref, g_ref, w_ref, dw_ref, dg_ref):
        dkvb = _bf(dkv_ref[...])
        dmn = jnp.zeros(mem_ref.shape, F32)
        for b in range(N_DEV):
            blk = dkvb[:, 256 * b:256 * (b + 1)]
            dmn = dmn + _dot(blk, w_ref[b], NT)
            dw_ref[b] = _bf(_dot(mn_ref[...], blk, TN))
        _, xh, r = _rms_fwd(mem_ref[...], g_ref[...])
        _, dg = _rms_bwd(dmn, xh, r, g_ref[...])
        dg_ref[...] = dg

    return pl.pallas_call(body, name="bwd_kv", out_shape=(SDS(wkv3.shape, BF16), SDS((1, d), F32)),
                          compiler_params=_cparams())(dkv, mn, mem, g_mem, wkv3)


def _bwd_out(dh1, c, zuv, cln_g, cln_b, gln_g, gln_b, ws, wst, bst, w_out, ts):
    s, d = dh1.shape
    nh = GM_HEADS

    def body(dh1_ref, c_ref, zuv_ref, clg, clb, glg, glb, ws_ref, wst_ref, bst_ref, wo_ref,
             dwo_ref, dc_ref, dzuv_ref, dws_ref, dbst_ref, dclg_ref, dclb_ref, dglg_ref, dglb_ref, dbin_ref,
             mixed_scr, dv_scr, acc):
        cl, chat, crs = _ln_fwd(c_ref[...], clg[...], clb[...])
        sg = jax.nn.sigmoid(cl)
        zu = zuv_ref[:, 0:512]
        zv = zuv_ref[:, 512:1024]
        u, tu = _gelu(zu)
        vg, tv = _gelu(zv)
        vln, vhat, vrs = _ln_fwd(vg, glg[...], glb[...])
        vb = _bf(vln)
        _mix_fwd(vb, ws_ref, bst_ref, mixed_scr, ts)
        mixed = mixed_scr[...]
        dh1b = _bf(dh1_ref[...])
        dcat = _dot(dh1b, wo_ref[...], NT)
        dgm = dcat[:, 512:1024]
        dc, dclg, dclb = _ln_bwd(dcat[:, 0:512] * _silu_grad(cl, sg), chat, crs, clg[...])
        dc_ref[...] = dc
        dzu = dgm * mixed * _gelu_grad(zu, tu)
        dm = dgm * u

        @pl.when(pl.program_id(0) == 0)
        def _():
            dws_ref[...] = jnp.zeros_like(dws_ref)
            dbst_ref[...] = jnp.zeros_like(dbst_ref)
            acc[...] = jnp.zeros_like(acc)

        acc[0:512, :] += _dot(_bf(cl * sg), dh1b, TN)
        acc[512:1024, :] += _dot(_bf(u * mixed), dh1b, TN)

        @pl.when(pl.program_id(0) == pl.num_programs(0) - 1)
        def _():
            dwo_ref[...] = _bf(acc[...])

        tril, triu, lo = _mix_masks()
        head = lax.broadcasted_iota(jnp.int32, (1, nh), 1)
        for j in range(nh // 2):
            w0t = _bf(jnp.where(triu, wst_ref[2 * j], 0.0))
            w1t = _bf(jnp.where(triu, wst_ref[2 * j + 1], 0.0))
            for n in range(ts // CHUNK):
                rows = slice(n * CHUNK, (n + 1) * CHUNK)
                lanes = slice(j * 128, (j + 1) * 128)
                dmc = dm[rows, lanes]
                dmb = _bf(dmc)
                dv_scr[rows, lanes] = jnp.where(lo, _dot(w0t, dmb), _dot(w1t, dmb))
                vc = vb[rows, lanes]
                d0 = jnp.where(lo, dmc, 0.0)
                d1 = dmc - d0
                dws_ref[2 * j] += jnp.where(tril, _dot(_bf(d0), vc, NT), 0.0)
                dws_ref[2 * j + 1] += jnp.where(tril, _dot(_bf(d1), vc, NT), 0.0)
                dbst_ref[...] += (jnp.sum(d0, axis=1, keepdims=True) * (head == 2 * j).astype(F32)
                                  + jnp.sum(d1, axis=1, keepdims=True) * (head == 2 * j + 1).astype(F32))
        dvg, dglg, dglb = _ln_bwd(dv_scr[...], vhat, vrs, glg[...])
        dzv = dvg * _gelu_grad(zv, tv)
        dzuv_ref[:, 0:512] = _bf(dzu)
        dzuv_ref[:, 512:1024] = _bf(dzv)
        _accumulate(dclg_ref, dclg)
        _accumulate(dclb_ref, dclb)
        _accumulate(dglg_ref, dglg)
        _accumulate(dglb_ref, dglb)
        _accumulate(dbin_ref, jnp.concatenate([jnp.sum(dzu, axis=0, keepdims=True),
                                               jnp.sum(dzv, axis=0, keepdims=True)], axis=1))

    vec = (1, 512)
    return pl.pallas_call(
        body, name="bwd_out", grid=(s // ts,),
        in_specs=[_row(ts, d), _row(ts, 512), _row(ts, 1024), _const(cln_g.shape), _const(cln_b.shape),
                  _const(gln_g.shape), _const(gln_b.shape), _const(ws.shape), _const(wst.shape), _const(bst.shape),
                  _const(w_out.shape)],
        out_specs=(_acc((d, d)), _row(ts, 512), _row(ts, 1024), _acc(ws.shape), _acc(bst.shape), _acc(vec), _acc(vec),
                   _acc(vec), _acc(vec), _acc((1, 1024))),
        out_shape=(SDS((d, d), BF16), SDS((s, 512), F32), SDS((s, 1024), BF16), SDS(ws.shape, F32),
                   SDS(bst.shape, F32), SDS(vec, F32), SDS(vec, F32), SDS(vec, F32), SDS(vec, F32), SDS((1, 1024), F32)),
        scratch_shapes=[pltpu.VMEM((ts, 512), F32), pltpu.VMEM((ts, 512), F32), pltpu.VMEM((d, d), F32)],
        compiler_params=_cparams("arbitrary"))(dh1, c, zuv, cln_g, cln_b, gln_g, gln_b, ws, wst, bst, w_out)


CONV_RC = 256
FWD_TAPS = range(CONV_PAD - CONV_K + 1, CONV_PAD + 1)


def _bwd_in(x, dh1, dc, a, dzuv, g_mix, w_in3, b_in, conv_w, ts):
    s, d = x.shape
    cw = conv_w.shape[1]
    nb = cw // 128
    n = s // ts
    rc = min(CONV_RC, ts)
    per_halo = ts // CONV_PAD
    w4 = conv_w.reshape(CONV_K, nb, 128).transpose(1, 0, 2)

    def body(x_ref, dh1_ref, dc_ref, a_ref, ahalo_ref, dzuv_ref, g_ref, w_ref, b_ref, cw_ref,
             dx_ref, dw_ref, dcw_ref, dcb_ref, dbin_ref, dg_ref, dz_ref, acc, pad_d, pad_a, part, da_scr, zag):
        i = pl.program_id(0)

        @pl.when(i == 0)
        def _():
            acc[...] = jnp.zeros_like(acc)
            part[...] = jnp.zeros_like(part)
            pad_d[:, ts:, :] = jnp.zeros((nb, CONV_PAD, 128), F32)

        before = jnp.where(i == n - 1, 0.0, ahalo_ref[...])
        for blk in range(nb):
            lanes = slice(128 * blk, 128 * (blk + 1))
            pad_d[blk, 0:ts, :] = dc_ref[:, lanes]
            pad_a[blk, 0:CONV_PAD, :] = before[:, lanes]
            pad_a[blk, CONV_PAD:, :] = a_ref[:, lanes]

        def rows8(v):
            return jnp.sum(v.reshape(rc // 8, 8, 128), axis=0)

        def unit(blk, r0):
            acc_da = jnp.zeros((rc, 128), F32)
            for res, offs in _by_residue(range(0, CONV_K)):
                shifted = pad_d[blk, pl.ds(r0 + res, rc + offs[-1] - res), :]
                for off in offs:
                    k = CONV_K - 1 - off
                    acc_da = acc_da + cw_ref[blk, k:k + 1, :] * shifted[off - res:off - res + rc, :]
            da_scr[blk, pl.ds(r0, rc), :] = acc_da
            dcc = pad_d[blk, pl.ds(r0, rc), :]
            for res, offs in _by_residue(FWD_TAPS):
                shifted = pad_a[blk, pl.ds(r0 + res, rc + offs[-1] - res), :]
                for off in offs:
                    k = off - FWD_TAPS[0]
                    part[blk, 8 * k:8 * k + 8, :] += rows8(dcc * shifted[off - res:off - res + rc, :])
            part[blk, 8 * CONV_PAD:, :] += rows8(dcc)

        for blk in range(nb):
            for r0 in range(0, ts, rc):
                unit(blk, r0)
        pad_d[:, ts:, :] = pad_d[:, 0:CONV_PAD, :]

        hn, xh, r = _rms_fwd(x_ref[...], g_ref[...])
        hb = _bf(hn)
        for b in range(N_DEV // 2):
            zag[:, 256 * b:256 * (b + 1)] = _dot(hb, w_ref[b]) + b_ref[:, 256 * b:256 * (b + 1)]
        za = zag[:, 0:512]
        sg = jax.nn.sigmoid(zag[:, 512:1024])
        da_ = jnp.concatenate([da_scr[blk] for blk in range(nb)], axis=1)
        dza = da_ * sg
        dzg = da_ * za * sg * (1.0 - sg)
        dz_ref[:, 0:512] = _bf(dza)
        dz_ref[:, 512:1024] = _bf(dzg)
        dz_ref[:, 1024:2048] = dzuv_ref[...]
        dhn = jnp.zeros((ts, d), F32)
        for b in range(N_DEV):
            dhn = dhn + _dot(dz_ref[:, 256 * b:256 * (b + 1)], w_ref[b], NT)
        acc[...] += _dot(hb, dz_ref[...], TN)

        @pl.when(i == n - 1)
        def _():
            for b in range(N_DEV):
                dw_ref[b] = _bf(acc[:, 256 * b:256 * (b + 1)])
            for blk in range(nb):
                sums = jnp.sum(part[blk].reshape(CONV_PAD + 1, 8, 128), axis=1)
                dcw_ref[:, 128 * blk:128 * (blk + 1)] = sums[0:CONV_PAD, :]
                dcb_ref[:, 128 * blk:128 * (blk + 1)] = sums[CONV_PAD:, :]

        dxn, dg = _rms_bwd(dhn, xh, r, g_ref[...])
        dx_ref[...] = dh1_ref[...] + dxn
        _accumulate(dg_ref, dg)
        _accumulate(dbin_ref, jnp.concatenate([jnp.sum(dza, axis=0, keepdims=True),
                                               jnp.sum(dzg, axis=0, keepdims=True)], axis=1))

    back = lambda width: pl.BlockSpec((ts, width), lambda i: (n - 1 - i, 0))
    halo = pl.BlockSpec((CONV_PAD, cw), lambda i: (jnp.maximum((n - 1 - i) * per_halo - 1, 0), 0))
    return pl.pallas_call(
        body, name="bwd_in", grid=(n,),
        in_specs=[back(d), back(d), back(cw), back(cw), halo, back(1024), _const(g_mix.shape), _const(w_in3.shape),
                  _const(b_in.shape), _const(w4.shape)],
        out_specs=(back(d), _acc(w_in3.shape), _acc((CONV_PAD, cw)), _acc((1, cw)), _acc((1, 1024)), _acc((1, d))),
        out_shape=(SDS((s, d), F32), SDS(w_in3.shape, BF16), SDS((CONV_PAD, cw), F32), SDS((1, cw), F32),
                   SDS((1, 1024), F32), SDS((1, d), F32)),
        scratch_shapes=[pltpu.VMEM((ts, 2048), BF16), pltpu.VMEM((d, 2048), F32),
                        pltpu.VMEM((nb, ts + CONV_PAD, 128), F32), pltpu.VMEM((nb, ts + CONV_PAD, 128), F32),
                        pltpu.VMEM((nb, 8 * (CONV_PAD + 1), 128), F32), pltpu.VMEM((nb, ts, 128), F32),
                        pltpu.VMEM((ts, 1024), F32)],
        compiler_params=_cparams("arbitrary"))(x, dh1, dc, a, a, dzuv, g_mix, w_in3, b_in, w4)


def _tn_matmul(a, b, tm, name):
    s, m = a.shape
    n = b.shape[1]
    ts = _row_tile(s, (1024, 512, 256, 128))
    n_s = s // ts

    def body(a_ref, b_ref, o_ref, acc):
        k = pl.program_id(1)

        @pl.when(k == 0)
        def _():
            acc[...] = jnp.zeros_like(acc)

        acc[...] += _dot(a_ref[...], b_ref[...], TN)

        @pl.when(k == n_s - 1)
        def _():
            o_ref[...] = _bf(acc[...])

    return pl.pallas_call(
        body, name=name, grid=(m // tm, n_s),
        in_specs=[pl.BlockSpec((ts, tm), lambda i, k: (k, i)), pl.BlockSpec((ts, n), lambda i, k: (k, 0))],
        out_specs=pl.BlockSpec((tm, n), lambda i, k: (i, 0)), out_shape=SDS((m, n), BF16),
        scratch_shapes=[pltpu.VMEM((tm, n), F32)],
        compiler_params=_cparams("parallel", "arbitrary"))(a, b)


def _row_tile(r, cands):
    for cand in cands:
        if r % cand == 0:
            return cand
    return r


def _sum_in_device_order(me_ref, land_ref, own_ref):
    acc = None
    for dev in range(N_DEV):
        part = jnp.where(me_ref[0] == dev, own_ref[0], land_ref[dev]).astype(F32)
        acc = part if acc is None else acc + part
    return acc


def _sum8(land, own, me, name):
    _, r, c = land.shape
    tr = _row_tile(r, (512, 256, 352, 128, 8))
    own3 = own if own.ndim == 3 else own[None]
    own_map = (lambda i, me_ref: (me_ref[0], i, 0)) if own.ndim == 3 else (lambda i, me_ref: (0, i, 0))

    def body(me_ref, land_ref, own_ref, o_ref):
        o_ref[...] = _sum_in_device_order(me_ref, land_ref, own_ref)

    return pl.pallas_call(
        body, name=name, out_shape=SDS((r, c), F32),
        grid_spec=pltpu.PrefetchScalarGridSpec(
            num_scalar_prefetch=1, grid=(r // tr,),
            in_specs=[pl.BlockSpec((N_DEV, tr, c), lambda i, me_ref: (0, i, 0)), pl.BlockSpec((1, tr, c), own_map)],
            out_specs=pl.BlockSpec((tr, c), lambda i, me_ref: (i, 0))),
        compiler_params=_cparams("parallel"))(me, land, own3)


def _adam_update(w, g, m, v):
    m2 = ADAM_B1 * m + (1.0 - ADAM_B1) * g
    v2 = ADAM_B2 * v + (1.0 - ADAM_B2) * (g * g)
    m_hat = m2 / (1.0 - ADAM_B1 ** ADAM_STEP)
    v_hat = v2 / (1.0 - ADAM_B2 ** ADAM_STEP)
    return -ADAM_LR * (m_hat / (jnp.sqrt(v_hat) + ADAM_EPS) + ADAM_WD * w), m2, v2


def _slot_spec(tr, c, mask):
    return pl.BlockSpec((1, tr, c), lambda i, me_ref: (me_ref[0] ^ mask, i, 0))


def _sum_adamw(land, parts, w, m, v, me, name, masks=ALL_PEERS):
    r, c = w.shape
    tr = _row_tile(r, (256, 128, 176, 8))
    n = len(masks)

    def body(me_ref, own_ref, *refs):
        w_ref, m_ref, v_ref, g_ref, d_ref, m2_ref, v2_ref = refs[n:]
        g = own_ref[0].astype(F32)
        for peer_ref in refs[:n]:
            g = g + peer_ref[0].astype(F32)
        g_ref[...] = g
        d_ref[...], m2_ref[...], v2_ref[...] = _adam_update(w_ref[...], g, m_ref[...], v_ref[...])

    blk = pl.BlockSpec((tr, c), lambda i, me_ref: (i, 0))
    return pl.pallas_call(
        body, name=name, out_shape=(SDS((r, c), F32),) * 4,
        grid_spec=pltpu.PrefetchScalarGridSpec(
            num_scalar_prefetch=1, grid=(r // tr,),
            in_specs=[_slot_spec(tr, c, 0)] + [_slot_spec(tr, c, mask) for mask in masks] + [blk, blk, blk],
            out_specs=(blk,) * 4),
        compiler_params=_cparams("parallel"))(me, parts, *([land] * n), w, m, v)


def _chip_sum(parts, from_sibling, me, name):
    _, r, c = parts.shape
    tr = _row_tile(r, (512, 256, 352, 128, 8))
    spec = pl.BlockSpec((1, tr, c), lambda j, i, me_ref: (me_ref[0] ^ (2 * j), i, 0))

    def body(me_ref, p_ref, q_ref, o_ref):
        o_ref[...] = _bf(p_ref[...].astype(F32) + q_ref[...].astype(F32))

    return pl.pallas_call(
        body, name=name, out_shape=SDS(parts.shape, BF16),
        grid_spec=pltpu.PrefetchScalarGridSpec(num_scalar_prefetch=1, grid=(len(CHIPS), r // tr), in_specs=[spec, spec],
                                               out_specs=spec),
        compiler_params=_cparams("parallel", "parallel"))(me, parts, from_sibling)


def _adamw_many(ws, gs, ms, vs, name):
    n = len(ws)

    def body(*refs):
        ins, outs = refs[:4 * n], refs[4 * n:]
        for i in range(n):
            w, g, m, v = (ins[j * n + i][...] for j in range(4))
            outs[i][...], outs[n + i][...], outs[2 * n + i][...] = _adam_update(w, g, m, v)

    shapes = [SDS(w.shape, F32) for w in ws]
    out = pl.pallas_call(body, name=name, out_shape=shapes * 3, compiler_params=_cparams())(*ws, *gs, *ms, *vs)
    return out[:n], out[n:2 * n], out[2 * n:]


def _pack(arrs):
    flat = jnp.concatenate([a.reshape(-1) for a in arrs])
    pad = (-flat.shape[0]) % (128 * 128)
    return jnp.pad(flat, (0, pad)).reshape(-1, 128)


def _unpack(packed, shapes):
    flat = packed.reshape(-1)
    out, off = [], 0
    for shp in shapes:
        size = 1
        for dim in shp:
            size *= dim
        out.append(flat[off:off + size].reshape(shp))
        off += size
    return out


SMALL = ("norm_mix_g", "b_in", "conv_w", "conv_b", "conv_ln_g", "conv_ln_b", "gm_ln_g", "gm_ln_b", "gm_w_s", "gm_b_s",
         "norm_xa_g", "mem_norm_g", "norm_ffn_g", "final_norm_g")
BIG = ("w_in", "w_out", "xa_wq", "xa_wkv", "xa_wo", "ffn_w_gate_up", "ffn_w_down")
WEIGHTS = ("norm_mix_g", "w_in", "b_in", "conv_w", "conv_b", "conv_ln_g", "conv_ln_b", "gm_ln_g", "gm_ln_b", "gm_w_s",
           "gm_b_s", "w_out", "norm_xa_g", "mem_norm_g", "xa_wq", "xa_wkv", "xa_wo", "norm_ffn_g", "ffn_w_gate_up",
           "ffn_w_down", "final_norm_g")


def kernel(x, mem, norm_mix_g, w_in, b_in, conv_w, conv_b, conv_ln_g, conv_ln_b, gm_ln_g, gm_ln_b, gm_w_s, gm_b_s, w_out, norm_xa_g, mem_norm_g, xa_wq, xa_wkv, xa_wo, norm_ffn_g, ffn_w_gate_up, ffn_w_down, final_norm_g, loss_target, m_norm_mix_g, m_w_in, m_b_in, m_conv_w, m_conv_b, m_conv_ln_g, m_conv_ln_b, m_gm_ln_g, m_gm_ln_b, m_gm_w_s, m_gm_b_s, m_w_out, m_norm_xa_g, m_mem_norm_g, m_xa_wq, m_xa_wkv, m_xa_wo, m_norm_ffn_g, m_ffn_w_gate_up, m_ffn_w_down, m_final_norm_g, v_norm_mix_g, v_w_in, v_b_in, v_conv_w, v_conv_b, v_conv_ln_g, v_conv_ln_b, v_gm_ln_g, v_gm_ln_b, v_gm_w_s, v_gm_b_s, v_w_out, v_norm_xa_g, v_mem_norm_g, v_xa_wq, v_xa_wkv, v_xa_wo, v_norm_ffn_g, v_ffn_w_gate_up, v_ffn_w_down, v_final_norm_g):
    w = dict(norm_mix_g=norm_mix_g, w_in=w_in, b_in=b_in, conv_w=conv_w, conv_b=conv_b, conv_ln_g=conv_ln_g,
             conv_ln_b=conv_ln_b, gm_ln_g=gm_ln_g, gm_ln_b=gm_ln_b, gm_w_s=gm_w_s, gm_b_s=gm_b_s, w_out=w_out,
             norm_xa_g=norm_xa_g, mem_norm_g=mem_norm_g, xa_wq=xa_wq, xa_wkv=xa_wkv, xa_wo=xa_wo,
             norm_ffn_g=norm_ffn_g, ffn_w_gate_up=ffn_w_gate_up, ffn_w_down=ffn_w_down, final_norm_g=final_norm_g)
    mom = dict(norm_mix_g=m_norm_mix_g, w_in=m_w_in, b_in=m_b_in, conv_w=m_conv_w, conv_b=m_conv_b,
               conv_ln_g=m_conv_ln_g, conv_ln_b=m_conv_ln_b, gm_ln_g=m_gm_ln_g, gm_ln_b=m_gm_ln_b, gm_w_s=m_gm_w_s,
               gm_b_s=m_gm_b_s, w_out=m_w_out, norm_xa_g=m_norm_xa_g, mem_norm_g=m_mem_norm_g, xa_wq=m_xa_wq,
               xa_wkv=m_xa_wkv, xa_wo=m_xa_wo, norm_ffn_g=m_norm_ffn_g, ffn_w_gate_up=m_ffn_w_gate_up,
               ffn_w_down=m_ffn_w_down, final_norm_g=m_final_norm_g)
    var = dict(norm_mix_g=v_norm_mix_g, w_in=v_w_in, b_in=v_b_in, conv_w=v_conv_w, conv_b=v_conv_b,
               conv_ln_g=v_conv_ln_g, conv_ln_b=v_conv_ln_b, gm_ln_g=v_gm_ln_g, gm_ln_b=v_gm_ln_b, gm_w_s=v_gm_w_s,
               gm_b_s=v_gm_b_s, w_out=v_w_out, norm_xa_g=v_norm_xa_g, mem_norm_g=v_mem_norm_g, xa_wq=v_xa_wq,
               xa_wkv=v_xa_wkv, xa_wo=v_xa_wo, norm_ffn_g=v_norm_ffn_g, ffn_w_gate_up=v_ffn_w_gate_up,
               ffn_w_down=v_ffn_w_down, final_norm_g=v_final_norm_g)

    me = 4 * lax.axis_index("x") + 2 * lax.axis_index("y") + lax.axis_index("c")
    s, d = x.shape[1], x.shape[2]
    xs = x.reshape(s, d)
    mems = mem.reshape(mem.shape[1], d)
    tgt = loss_target.reshape(s, d)
    ts = min(512, s)
    ts_ffn = min(256, s)
    row = lambda a: a.reshape(1, -1)

    conv_w_pad = jnp.pad(conv_w, ((0, CONV_PAD - CONV_K), (0, 128 - conv_w.shape[1])))
    first_level = (SIBLING,) + OTHER_CHIPS
    ex_first = _exchange_start([_bf(w_in), conv_w_pad], [False] * 2, "gather_start_in", first_level, own_slot=me)
    behind = lambda t: _bf(t + ex_first[4][0:1, 0:1])
    shards = [behind(xa_wkv), behind(w_out), behind(xa_wq), behind(xa_wo), behind(ffn_w_gate_up.T), behind(ffn_w_down)]
    ex_rest = _exchange_start(shards, [False] * len(shards), "gather_start_rest", first_level, own_slot=me)
    g_send, g_recv, g_src, g_land = (list(ex_first[i]) + list(ex_rest[i]) for i in range(4))
    g_tok = ex_rest[4]

    def arrived(idx, after, name):
        pick = lambda seq: [seq[i] for i in idx]
        return _wait_and_forward(pick(g_send), pick(g_recv), pick(g_src), pick(g_land), after, "gather_pass_" + name,
                                 first_level)

    def complete(handle, after, name):
        return _forward_wait(handle[0], handle[1], handle[2], after, "forward_wait_" + name)

    bst = gm_b_s.T
    wst = jnp.swapaxes(gm_w_s, 1, 2)

    h_in = arrived((0, 1), g_tok, "in")
    w_in3, conv_w8 = complete(h_in, h_in[3], "in")
    conv_w_f = conv_w8[:, :CONV_K, :conv_w.shape[1]].transpose(1, 0, 2).reshape(CONV_K, -1)
    cw = conv_w_f.shape[1]
    zuv, a = _fwd_in(xs, row(norm_mix_g), w_in3, row(b_in), ts)
    h_out = arrived((2, 3), zuv, "out")
    c = _conv_fwd(a, conv_w_f + h_out[3][0:1, 0:1], row(conv_b))
    h_xa = arrived((4, 5), c, "xa")
    wkv3, w_out3 = complete(h_out, h_xa[3], "out")
    w_out_f = w_out3.reshape(-1, d)
    kv, mn = _kv_proj(mems, row(mem_norm_g), wkv3)
    h1 = _fwd_out(xs, c, zuv, row(conv_ln_g), row(conv_ln_b), row(gm_ln_g), row(gm_ln_b), gm_w_s, bst, w_out_f, ts)
    h_gut = arrived((6,), h1, "gate_up")
    wq3, wo3 = complete(h_xa, h_gut[3], "xa")
    wq_f = wq3.reshape(-1, d)
    wo_f = wo3.reshape(-1, d)
    h2 = _fwd_xa(h1, row(norm_xa_g), wq_f, kv, wo_f, ts)
    h_down = arrived((7,), h2, "down")
    (wgut3,) = complete(h_gut, h_down[3], "gate_up")
    (wdown3,) = complete(h_down, wgut3, "down")
    wgut_f = wgut3.reshape(-1, d)
    wdown_f = wdown3.reshape(-1, d)
    dh3, dh3b, gu, hn2, loss_p, d_final_g = _fwd_ffn(h2, row(norm_ffn_g), wgut_f, wdown_f, row(final_norm_g), tgt,
                                                     ts_ffn)

    blocks = lambda m: m.reshape(N_DEV, -1, d)
    tok = lambda ex: ex[4][0:1, 0:1]
    me1 = me.astype(jnp.int32).reshape(1)

    def chip_sums(pair, after, tag):
        parts, handed = _pair_wait(pair[0], pair[1], pair[2], pair[3], after, "pair_wait_" + tag)
        return [_chip_sum(p, h, me1, "chip_sum_%s_%d" % (tag, i)) for i, (p, h) in enumerate(zip(parts, handed))]

    dh2, act, dgu, d_ffn_g = _bwd_ffn(h2, dh3, gu, row(norm_ffn_g), wgut_f, wdown_f, ts_ffn)
    dwgut = _tn_matmul(dgu, hn2, 2816, "dw_gate_up")
    dwdown = _tn_matmul(act, dh3b, 2816, "dw_down")
    pair_ffn = _pair_start([blocks(dwgut), blocks(dwdown)], "pair_start_ffn")
    dh1, dwq, dwo, dkv, d_xa_g = _bwd_xa(h1, dh2, row(norm_xa_g) + tok(pair_ffn), wq_f, wo_f, kv, ts)
    ex_ffn = _exchange_start(chip_sums(pair_ffn, dh1, "ffn"), [True] * 2, "scatter_start_ffn", OTHER_CHIPS)
    dwkv3, d_mem_g = _bwd_kv(dkv, mn, mems, row(mem_norm_g) + tok(ex_ffn), wkv3)
    pair_xa = _pair_start([blocks(dwq), blocks(dwo), dwkv3], "pair_start_xa")
    (dwout, dc, dzuv, dws, dbst, d_cln_g, d_cln_b, d_gln_g, d_gln_b, dbin_uv) = _bwd_out(
        dh1, c, zuv, row(conv_ln_g) + tok(pair_xa), row(conv_ln_b), row(gm_ln_g), row(gm_ln_b), gm_w_s, wst, bst,
        w_out_f, ts)
    early = dict(b_in_uv=dbin_uv, conv_ln_g=d_cln_g, conv_ln_b=d_cln_b, gm_ln_g=d_gln_g, gm_ln_b=d_gln_b,
                 gm_b_s=dbst.T, norm_xa_g=d_xa_g, mem_norm_g=d_mem_g, norm_ffn_g=d_ffn_g, final_norm_g=d_final_g,
                 loss=loss_p)
    dws_b = _bf(dws).reshape(GM_HEADS * CHUNK, CHUNK)
    ex_xa = _exchange_start(chip_sums(pair_xa, dc, "xa") + [blocks(dwout), dws_b, _pack(list(early.values()))],
                            [True] * 4 + [False] * 2, "scatter_start_xa", [OTHER_CHIPS] * 3 + [ALL_PEERS] * 3)
    dx, dw_in3, dconv_w, dconv_b, dbin_ag, d_mix_g = _bwd_in(xs, dh1, dc, a, dzuv, row(norm_mix_g) + tok(ex_xa),
                                                             w_in3, row(b_in), conv_w_f, ts)
    late = dict(norm_mix_g=d_mix_g, b_in_ag=dbin_ag, conv_w=dconv_w[:CONV_K], conv_b=dconv_b)
    ex_last = _exchange_start([_pack(list(late.values())), dw_in3], [False, True], "scatter_start_in")

    grads, delta, new_m, new_v = {}, {}, {}, {}

    def waited(ex, idx, scatter, after, name, masks=ALL_PEERS):
        return _exchange_wait(*[[ex[i][k] for k in idx] for i in range(4)], scatter, after, name, masks)

    def reduced(names, srcs, lands, masks):
        for nm, src, land, mk in zip(names, srcs, lands, masks):
            view = (lambda t: t.T) if nm == "ffn_w_gate_up" else (lambda t: t)
            upd = _sum_adamw(land, src, view(w[nm]), view(mom[nm]), view(var[nm]), me1, "adamw_" + nm, mk)
            grads[nm], delta[nm], new_m[nm], new_v[nm] = (view(t) for t in upd)
        return new_v[names[-1]]

    after = reduced(("ffn_w_gate_up", "ffn_w_down"),
                    *waited(ex_ffn, (0, 1), [True] * 2, ex_last[4], "scatter_wait_ffn", OTHER_CHIPS), [OTHER_CHIPS] * 2)
    xa_masks = [OTHER_CHIPS] * 3 + [ALL_PEERS]
    after = reduced(("xa_wq", "xa_wo", "xa_wkv", "w_out"),
                    *waited(ex_xa, (0, 1, 2, 3), [True] * 4, after, "scatter_wait_xa", xa_masks), xa_masks)

    (dws_src, early_src), (dws_land, early_land) = waited(ex_xa, (4, 5), [False] * 2, after, "gather_wait_early")
    (late_src,), (late_land,) = waited(ex_last, (0,), [False], early_land, "gather_wait_late")
    dws_tot = _sum8(dws_land, dws_src, me1, "sum_gm_w_s")
    early_tot = _sum8(early_land, early_src, me1, "sum_small_early")
    late_tot = _sum8(late_land, late_src, me1, "sum_small_late")
    early_g = dict(zip(early, _unpack(early_tot, [v.shape for v in early.values()])), gm_w_s=dws_tot)
    late_g = dict(zip(late, _unpack(late_tot, [v.shape for v in late.values()])))
    loss = early_g["loss"][0, 0]
    for nm in SMALL:
        if nm == "b_in":
            g = jnp.concatenate([late_g["b_in_ag"], early_g["b_in_uv"]], axis=1)
        elif nm == "conv_w":
            g = lax.dynamic_slice_in_dim(late_g[nm], me * conv_w.shape[1], conv_w.shape[1], axis=1)
        else:
            g = late_g[nm] if nm in late_g else early_g[nm]
        grads[nm] = g.reshape(w[nm].shape)

    small = [[src[nm] for nm in SMALL] for src in (w, grads, mom, var)]
    for dst, vals in zip((delta, new_m, new_v), _adamw_many(*small, "adamw_small")):
        dst.update(zip(SMALL, vals))

    reduced(("w_in",), *waited(ex_last, (1,), [True], delta[SMALL[0]], "scatter_wait_in"), [ALL_PEERS])

    return (loss, dx.reshape(x.shape), *[grads[nm] for nm in WEIGHTS], *[delta[nm] for nm in WEIGHTS],
            *[new_m[nm] for nm in WEIGHTS], *[new_v[nm] for nm in WEIGHTS])
```

```python
import jax
import jax.numpy as jnp
from jax import lax
from jax.experimental import pallas as pl
from jax.experimental.pallas import tpu as pltpu

F32 = jnp.float32
BF16 = jnp.bfloat16
SDS = jax.ShapeDtypeStruct

N_DEV = 8
RMS_EPS = 1e-6
LN_EPS = 1e-5
CONV_K = 31
CONV_PAD = 32
CHUNK = 128
GM_HEADS = 8
XA_HEADS = 4
XA_DH = 256
GELU_K0 = 0.7978845608028654
GELU_K1 = 0.044715
ADAM_LR = 0.001
ADAM_B1 = 0.9
ADAM_B2 = 0.999
ADAM_EPS = 1e-08
ADAM_WD = 0.01
ADAM_STEP = 10
VMEM_LIMIT = 60 * 1024 * 1024

NN = (((1,), (0,)), ((), ()))
NT = (((1,), (1,)), ((), ()))
TN = (((0,), (0,)), ((), ()))


def _dot(a, b, dims=NN):
    return lax.dot_general(a, b, dims, preferred_element_type=F32)


def _bf(x):
    return x.astype(BF16)


def _cparams(*sem):
    return pltpu.CompilerParams(dimension_semantics=tuple(sem) if sem else None, vmem_limit_bytes=VMEM_LIMIT)


def _row(ts, w, col=0):
    return pl.BlockSpec((ts, w), lambda i: (i, col))


def _const(shape):
    nd = len(shape)
    return pl.BlockSpec(shape, lambda i: (0,) * nd, pipeline_mode=pl.Buffered(1))


def _acc(shape):
    nd = len(shape)
    return pl.BlockSpec(shape, lambda i: (0,) * nd)


def _rms_fwd(x, g):
    r = lax.rsqrt(jnp.mean(x * x, axis=-1, keepdims=True) + RMS_EPS)
    xh = x * r
    return xh * g, xh, r


def _rms_bwd(dy, xh, r, g):
    gdy = dy * g
    dx = r * (gdy - xh * jnp.mean(gdy * xh, axis=-1, keepdims=True))
    dg = jnp.sum(dy * xh, axis=0, keepdims=True)
    return dx, dg


def _ln_fwd(x, g, b):
    mu = jnp.mean(x, axis=-1, keepdims=True)
    xc = x - mu
    rs = lax.rsqrt(jnp.mean(xc * xc, axis=-1, keepdims=True) + LN_EPS)
    xh = xc * rs
    return xh * g + b, xh, rs


def _ln_bwd(dy, xh, rs, g):
    dxh = dy * g
    dx = rs * (dxh - jnp.mean(dxh, axis=-1, keepdims=True) - xh * jnp.mean(dxh * xh, axis=-1, keepdims=True))
    return dx, jnp.sum(dy * xh, axis=0, keepdims=True), jnp.sum(dy, axis=0, keepdims=True)


def _gelu(x):
    t = jnp.tanh(GELU_K0 * (x + GELU_K1 * (x * x * x)))
    return 0.5 * x * (1.0 + t), t


def _gelu_grad(x, t):
    return 0.5 * (1.0 + t) + 0.5 * x * (1.0 - t * t) * (GELU_K0 * (1.0 + 3.0 * GELU_K1 * x * x))


def _silu_grad(x, sg):
    return sg * (1.0 + x * (1.0 - sg))


def _by_residue(offsets):
    groups = {}
    for off in offsets:
        groups.setdefault(off % 8, []).append(off)
    return [(res, sorted(offs)) for res, offs in sorted(groups.items())]


def _accumulate(ref, val):
    @pl.when(pl.program_id(0) == 0)
    def _():
        ref[...] = jnp.zeros_like(ref)
    ref[...] += val


def _mix_masks():
    row = lax.broadcasted_iota(jnp.int32, (CHUNK, CHUNK), 0)
    col = lax.broadcasted_iota(jnp.int32, (CHUNK, CHUNK), 1)
    return row >= col, row <= col, col < (CHUNK // 2)


def _mix_fwd(vb, ws_ref, bst_ref, mixed_scr, ts):
    tril, _, lo = _mix_masks()
    for j in range(GM_HEADS // 2):
        w0 = _bf(jnp.where(tril, ws_ref[2 * j], 0.0))
        w1 = _bf(jnp.where(tril, ws_ref[2 * j + 1], 0.0))
        bias = jnp.where(lo, bst_ref[:, 2 * j:2 * j + 1], bst_ref[:, 2 * j + 1:2 * j + 2])
        for n in range(ts // CHUNK):
            v = vb[n * CHUNK:(n + 1) * CHUNK, j * 128:(j + 1) * 128]
            mixed_scr[n * CHUNK:(n + 1) * CHUNK, j * 128:(j + 1) * 128] = jnp.where(lo, _dot(w0, v), _dot(w1, v)) + bias


def _peer_of(mask):
    x, y, c = lax.axis_index("x"), lax.axis_index("y"), lax.axis_index("c")
    px = 1 - x if (mask >> 2) & 1 else x
    py = 1 - y if (mask >> 1) & 1 else y
    pc = 1 - c if mask & 1 else c
    return (px, py, pc), 4 * px + 2 * py + pc


ALL_PEERS = tuple(range(1, N_DEV))
OTHER_CHIPS = (2, 4, 6)
CHIPS = (0,) + OTHER_CHIPS
SIBLING = 1


def _split_copy(src_ref, land_ref, send_sem, recv_sem, mask, slot, scatter, outgoing):
    x, y, c = lax.axis_index("x"), lax.axis_index("y"), lax.axis_index("c")
    me = 4 * x + 2 * y + c
    peer, pidx = _peer_of(mask)
    return pltpu.make_async_remote_copy(
        src_ref=src_ref.at[pidx] if scatter else src_ref,
        dst_ref=land_ref.at[me if outgoing else pidx],
        send_sem=send_sem.at[slot], recv_sem=recv_sem.at[slot],
        device_id=peer, device_id_type=pl.DeviceIdType.MESH)


_HBM = pl.BlockSpec(memory_space=pltpu.HBM)
_SEM = pl.BlockSpec(memory_space=pltpu.SEMAPHORE)
_EFFECT = pltpu.SideEffectType.DATAFLOW_SIDE_EFFECTING


def _per_array(masks, n):
    return [tuple(masks)] * n if isinstance(masks[0], int) else [tuple(m) for m in masks]


def _exchange_start(srcs, scatter, name, masks=ALL_PEERS, own_slot=None):
    n = len(srcs)
    masks = _per_array(masks, n)
    lands = [lax.empty((N_DEV,) + tuple(s.shape[1:] if sc else s.shape), s.dtype) for s, sc in zip(srcs, scatter)]
    if own_slot is not None:
        lands = [land if sc else _with_own(land, s, own_slot) for land, s, sc in zip(lands, srcs, scatter)]
    lands = [pltpu.with_memory_space_constraint(land, pltpu.HBM) for land in lands]
    srcs = [pltpu.with_memory_space_constraint(s, pltpu.HBM) for s in srcs]

    def body(*refs):
        src_refs, land_refs = refs[:n], refs[n:2 * n]
        send_sems, recv_sems = refs[2 * n:3 * n], refs[3 * n:4 * n]
        token = refs[-1]
        for k in range(n):
            for slot, mask in enumerate(masks[k]):
                _split_copy(src_refs[k], land_refs[k], send_sems[k], recv_sems[k], mask, slot, scatter[k], True).start()
        token[...] = jnp.zeros_like(token)

    sems = [pltpu.SemaphoreType.DMA((len(m),)) for m in masks]
    out = pl.pallas_call(
        body, name=name,
        out_shape=tuple(sems + sems + [pltpu.HBM(s.shape, s.dtype) for s in srcs]
                        + [pltpu.HBM(l.shape, l.dtype) for l in lands] + [SDS((8, 128), F32)]),
        in_specs=[_HBM] * (2 * n),
        out_specs=tuple([_SEM] * (2 * n) + [_HBM] * (2 * n) + [pl.BlockSpec(memory_space=pltpu.VMEM)]),
        input_output_aliases={i: 2 * n + i for i in range(2 * n)},
        compiler_params=pltpu.CompilerParams(has_side_effects=_EFFECT),
    )(*srcs, *lands)
    return out[:n], out[n:2 * n], out[2 * n:3 * n], out[3 * n:4 * n], out[-1]


def _exchange_wait(send_sems, recv_sems, srcs_thru, lands_thru, scatter, after, name, masks=ALL_PEERS):
    n = len(srcs_thru)
    masks = _per_array(masks, n)

    def body(*refs):
        src_refs, land_refs = refs[:n], refs[n:2 * n]
        send_refs, recv_refs = refs[2 * n:3 * n], refs[3 * n:4 * n]
        for k in range(n):
            for slot, mask in enumerate(masks[k]):
                args = (src_refs[k], land_refs[k], send_refs[k], recv_refs[k], mask, slot, scatter[k])
                _split_copy(*args, True).wait_send()
                _split_copy(*args, False).wait_recv()

    out = pl.pallas_call(
        body, name=name,
        out_shape=tuple([pltpu.HBM(s.shape, s.dtype) for s in srcs_thru]
                        + [pltpu.HBM(l.shape, l.dtype) for l in lands_thru]),
        in_specs=[_HBM] * (2 * n) + [_SEM] * (2 * n) + [pl.BlockSpec(memory_space=pl.ANY)],
        out_specs=tuple([_HBM] * (2 * n)),
        input_output_aliases={i: i for i in range(2 * n)},
        compiler_params=pltpu.CompilerParams(has_side_effects=_EFFECT),
    )(*srcs_thru, *lands_thru, *send_sems, *recv_sems, after)
    return out[:n], out[n:]


def _wait_and_forward(send_sems, recv_sems, srcs_thru, lands_thru, after, name, masks):
    n = len(lands_thru)

    def body(*refs):
        src_refs, land_refs = refs[:n], refs[n:2 * n]
        send_refs, recv_refs = refs[2 * n:3 * n], refs[3 * n:4 * n]
        outs = refs[4 * n + 1:]
        fsend, frecv, token = outs[2 * n:3 * n], outs[3 * n:4 * n], outs[-1]
        for k in range(n):
            for slot, mask in enumerate(masks):
                args = (src_refs[k], land_refs[k], send_refs[k], recv_refs[k], mask, slot, False)
                _split_copy(*args, True).wait_send()
                _split_copy(*args, False).wait_recv()
        sibling, _ = _peer_of(SIBLING)
        for k in range(n):
            for slot, mask in enumerate(OTHER_CHIPS):
                _, mine = _peer_of(mask)
                pltpu.make_async_remote_copy(
                    src_ref=land_refs[k].at[mine], dst_ref=land_refs[k].at[mine], send_sem=fsend[k].at[slot],
                    recv_sem=frecv[k].at[slot], device_id=sibling, device_id_type=pl.DeviceIdType.MESH).start()
        token[...] = jnp.zeros_like(token)

    sem = pltpu.SemaphoreType.DMA((len(OTHER_CHIPS),))
    out = pl.pallas_call(
        body, name=name,
        out_shape=tuple([pltpu.HBM(s.shape, s.dtype) for s in srcs_thru] + [pltpu.HBM(l.shape, l.dtype) for l in lands_thru]
                        + [sem] * (2 * n) + [SDS((8, 128), F32)]),
        in_specs=[_HBM] * (2 * n) + [_SEM] * (2 * n) + [pl.BlockSpec(memory_space=pl.ANY)],
        out_specs=tuple([_HBM] * (2 * n) + [_SEM] * (2 * n) + [pl.BlockSpec(memory_space=pltpu.VMEM)]),
        input_output_aliases={i: i for i in range(2 * n)},
        compiler_params=pltpu.CompilerParams(has_side_effects=_EFFECT),
    )(*srcs_thru, *lands_thru, *send_sems, *recv_sems, after)
    return out[2 * n:3 * n], out[3 * n:4 * n], out[n:2 * n], out[-1]


def _forward_wait(send_sems, recv_sems, lands, after, name):
    n = len(lands)

    def body(*refs):
        land_refs, send_refs, recv_refs = refs[:n], refs[n:2 * n], refs[2 * n:3 * n]
        sibling, _ = _peer_of(SIBLING)
        for k in range(n):
            for slot, mask in enumerate(OTHER_CHIPS):
                _, mine = _peer_of(mask)
                _, theirs = _peer_of(mask | SIBLING)
                for block, wait in ((mine, "wait_send"), (theirs, "wait_recv")):
                    getattr(pltpu.make_async_remote_copy(
                        src_ref=land_refs[k].at[block], dst_ref=land_refs[k].at[block],
                        send_sem=send_refs[k].at[slot], recv_sem=recv_refs[k].at[slot], device_id=sibling,
                        device_id_type=pl.DeviceIdType.MESH), wait)()

    return pl.pallas_call(
        body, name=name, out_shape=tuple(pltpu.HBM(l.shape, l.dtype) for l in lands),
        in_specs=[_HBM] * n + [_SEM] * (2 * n) + [pl.BlockSpec(memory_space=pl.ANY)],
        out_specs=tuple([_HBM] * n), input_output_aliases={i: i for i in range(n)},
        compiler_params=pltpu.CompilerParams(has_side_effects=_EFFECT),
    )(*lands, *send_sems, *recv_sems, after)


def _pair_copy(parts_ref, land_ref, send_sem, recv_sem, slot, chip, outgoing):
    sibling, _ = _peer_of(SIBLING)
    _, block = _peer_of(chip | SIBLING if outgoing else chip)
    return pltpu.make_async_remote_copy(
        src_ref=parts_ref.at[block], dst_ref=land_ref.at[block], send_sem=send_sem.at[slot], recv_sem=recv_sem.at[slot],
        device_id=sibling, device_id_type=pl.DeviceIdType.MESH)


def _pair_start(parts, name):
    n = len(parts)
    lands = [pltpu.with_memory_space_constraint(lax.empty(p.shape, p.dtype), pltpu.HBM) for p in parts]
    parts = [pltpu.with_memory_space_constraint(p, pltpu.HBM) for p in parts]

    def body(*refs):
        part_refs, land_refs = refs[:n], refs[n:2 * n]
        send_sems, recv_sems, token = refs[2 * n:3 * n], refs[3 * n:4 * n], refs[-1]
        for k in range(n):
            for slot, chip in enumerate(CHIPS):
                _pair_copy(part_refs[k], land_refs[k], send_sems[k], recv_sems[k], slot, chip, True).start()
        token[...] = jnp.zeros_like(token)

    sem = pltpu.SemaphoreType.DMA((len(CHIPS),))
    out = pl.pallas_call(
        body, name=name,
        out_shape=tuple([sem] * (2 * n) + [pltpu.HBM(p.shape, p.dtype) for p in parts + lands] + [SDS((8, 128), F32)]),
        in_specs=[_HBM] * (2 * n),
        out_specs=tuple([_SEM] * (2 * n) + [_HBM] * (2 * n) + [pl.BlockSpec(memory_space=pltpu.VMEM)]),
        input_output_aliases={i: 2 * n + i for i in range(2 * n)},
        compiler_params=pltpu.CompilerParams(has_side_effects=_EFFECT),
    )(*parts, *lands)
    return out[:n], out[n:2 * n], out[2 * n:3 * n], out[3 * n:4 * n], out[-1]


def _pair_wait(send_sems, recv_sems, parts_thru, lands_thru, after, name):
    n = len(parts_thru)

    def body(*refs):
        part_refs, land_refs = refs[:n], refs[n:2 * n]
        send_refs, recv_refs = refs[2 * n:3 * n], refs[3 * n:4 * n]
        for k in range(n):
            for slot, chip in enumerate(CHIPS):
                args = (part_refs[k], land_refs[k], send_refs[k], recv_refs[k], slot, chip)
                _pair_copy(*args, True).wait_send()
                _pair_copy(*args, False).wait_recv()

    out = pl.pallas_call(
        body, name=name,
        out_shape=tuple(pltpu.HBM(p.shape, p.dtype) for p in list(parts_thru) + list(lands_thru)),
        in_specs=[_HBM] * (2 * n) + [_SEM] * (2 * n) + [pl.BlockSpec(memory_space=pl.ANY)],
        out_specs=tuple([_HBM] * (2 * n)), input_output_aliases={i: i for i in range(2 * n)},
        compiler_params=pltpu.CompilerParams(has_side_effects=_EFFECT),
    )(*parts_thru, *lands_thru, *send_sems, *recv_sems, after)
    return out[:n], out[n:]


def _with_own(landed, own, me):
    return lax.dynamic_update_slice_in_dim(landed, own[None], me, axis=0)


def _kv_proj(mem, g_mem, wkv3):
    m = mem.shape[0]

    def body(mem_ref, g_ref, w_ref, kv_ref, mn_ref):
        y, _, _ = _rms_fwd(mem_ref[...], g_ref[...])
        yb = _bf(y)
        mn_ref[...] = yb
        for b in range(N_DEV):
            kv_ref[:, 256 * b:256 * (b + 1)] = _dot(yb, w_ref[b])

    return pl.pallas_call(body, name="kv_proj", out_shape=(SDS((m, 2048), F32), SDS(mem.shape, BF16)),
                          compiler_params=_cparams())(mem, g_mem, wkv3)


def _fwd_in(x, g_mix, w_in3, b_in, ts):
    s, d = x.shape

    def body(x_ref, g_ref, w_ref, b_ref, z_ref, a_ref):
        hn, _, _ = _rms_fwd(x_ref[...], g_ref[...])
        hb = _bf(hn)
        for b in range(N_DEV):
            z_ref[:, 256 * b:256 * (b + 1)] = _dot(hb, w_ref[b]) + b_ref[:, 256 * b:256 * (b + 1)]
        a_ref[...] = z_ref[:, 0:512] * jax.nn.sigmoid(z_ref[:, 512:1024])

    return pl.pallas_call(
        body, name="fwd_in", grid=(s // ts,),
        in_specs=[_row(ts, d), _const(g_mix.shape), _const(w_in3.shape), _const(b_in.shape)],
        out_specs=(_row(ts, 2048), _row(ts, 512)),
        out_shape=(SDS((s, 2048), F32), SDS((s, 512), F32)),
        compiler_params=_cparams("arbitrary"))(x, g_mix, w_in3, b_in)


def _conv_fwd(a, w, b):
    s, cw = a.shape
    rc = 256 if s % 256 == 0 else 128

    def body(a_ref, w_ref, b_ref, c_ref, pad):
        pad[0:CONV_PAD, :] = jnp.zeros((CONV_PAD, 128), F32)
        pad[CONV_PAD:, :] = a_ref[...]

        def chunk(i, carry):
            r0 = pl.multiple_of(i * rc, rc)
            acc = jnp.zeros((rc, 128), F32) + b_ref[...]
            for res, offs in _by_residue(range(CONV_PAD - CONV_K + 1, CONV_PAD + 1)):
                shifted = pad[pl.ds(r0 + res, rc + offs[-1] - res), :]
                for off in offs:
                    k = off - (CONV_PAD - CONV_K + 1)
                    acc = acc + w_ref[k:k + 1, :] * shifted[off - res:off - res + rc, :]
            c_ref[pl.ds(r0, rc), :] = acc
            return carry

        lax.fori_loop(0, s // rc, chunk, 0)

    blk = lambda r: pl.BlockSpec((r, 128), lambda j: (0, j))
    return pl.pallas_call(
        body, name="conv_fwd", grid=(cw // 128,),
        in_specs=[blk(s), blk(CONV_K), blk(1)], out_specs=blk(s), out_shape=SDS((s, cw), F32),
        scratch_shapes=[pltpu.VMEM((s + CONV_PAD, 128), F32)],
        compiler_params=_cparams("arbitrary"))(a, w, b)


def _fwd_out(x, c, z, cln_g, cln_b, gln_g, gln_b, ws, bst, w_out, ts):
    s, d = x.shape

    def body(x_ref, c_ref, zuv_ref, clg, clb, glg, glb, ws_ref, bst_ref, wo_ref, h1_ref, mixed_scr):
        cl, _, _ = _ln_fwd(c_ref[...], clg[...], clb[...])
        co = cl * jax.nn.sigmoid(cl)
        u, _ = _gelu(zuv_ref[:, 0:512])
        vg, _ = _gelu(zuv_ref[:, 512:1024])
        vln, _, _ = _ln_fwd(vg, glg[...], glb[...])
        _mix_fwd(_bf(vln), ws_ref, bst_ref, mixed_scr, ts)
        gm = u * mixed_scr[...]
        h1_ref[...] = x_ref[...] + _dot(_bf(co), wo_ref[0:512, :]) + _dot(_bf(gm), wo_ref[512:1024, :])

    return pl.pallas_call(
        body, name="fwd_out", grid=(s // ts,),
        in_specs=[_row(ts, d), _row(ts, 512), _row(ts, 1024, 1), _const(cln_g.shape), _const(cln_b.shape),
                  _const(gln_g.shape), _const(gln_b.shape), _const(ws.shape), _const(bst.shape), _const(w_out.shape)],
        out_specs=_row(ts, d), out_shape=SDS((s, d), F32),
        scratch_shapes=[pltpu.VMEM((ts, 512), F32)],
        compiler_params=_cparams("arbitrary"))(x, c, z, cln_g, cln_b, gln_g, gln_b, ws, bst, w_out)


def _softmax_rows(sc):
    m = jnp.max(sc, axis=-1, keepdims=True)
    e = jnp.exp(sc - m)
    return e / jnp.sum(e, axis=-1, keepdims=True)


def _fwd_xa(h1, g_xa, wq, kv, wo, ts):
    s, d = h1.shape
    scale = XA_DH ** -0.5

    def body(h_ref, g_ref, wq_ref, kv_ref, wo_ref, h2_ref, q_ref, o_ref, o_scr):
        hn, _, _ = _rms_fwd(h_ref[...], g_ref[...])
        q_ref[...] = _bf(_dot(_bf(hn), wq_ref[...]))
        for h in range(XA_HEADS):
            qh = q_ref[:, XA_DH * h:XA_DH * (h + 1)]
            kh = _bf(kv_ref[:, XA_DH * h:XA_DH * (h + 1)])
            vh = _bf(kv_ref[:, d + XA_DH * h:d + XA_DH * (h + 1)])
            p = _softmax_rows(_dot(qh, kh, NT) * scale)
            o_scr[:, XA_DH * h:XA_DH * (h + 1)] = _dot(_bf(p), vh)
        o_ref[...] = _bf(o_scr[...])
        h2_ref[...] = h_ref[...] + _dot(o_ref[...], wo_ref[...])

    return pl.pallas_call(
        body, name="fwd_xa", grid=(s // ts,),
        in_specs=[_row(ts, d), _const(g_xa.shape), _const(wq.shape), _const(kv.shape), _const(wo.shape)],
        out_specs=(_row(ts, d), _row(ts, d), _row(ts, d)),
        out_shape=(SDS((s, d), F32), SDS((s, d), BF16), SDS((s, d), BF16)),
        scratch_shapes=[pltpu.VMEM((ts, d), F32)],
        compiler_params=_cparams("arbitrary"))(h1, g_xa, wq, kv, wo)


def _fwd_ffn(h2, g_ffn, wgut, wdown, g_final, target, ts):
    s, d = h2.shape
    hid = wdown.shape[0]
    hc = hid

    def body(h_ref, g_ref, wgu_ref, wd_ref, gf_ref, t_ref, dh3_ref, dh3b_ref, gu_ref, hn_ref, loss_ref, dgf_ref):
        hn, _, _ = _rms_fwd(h_ref[...], g_ref[...])
        hb = _bf(hn)
        hn_ref[...] = hb
        h3 = h_ref[...]
        for n in range(hid // hc):
            g = _dot(hb, wgu_ref[hc * n:hc * (n + 1), :], NT)
            u = _dot(hb, wgu_ref[hid + hc * n:hid + hc * (n + 1), :], NT)
            gu_ref[:, hc * n:hc * (n + 1)] = g
            gu_ref[:, hid + hc * n:hid + hc * (n + 1)] = u
            act = g * jax.nn.sigmoid(g) * u
            h3 = h3 + _dot(_bf(act), wd_ref[hc * n:hc * (n + 1), :])
        y, xh, r = _rms_fwd(h3, gf_ref[...])
        diff = y - t_ref[...]
        part = 0.5 * jnp.sum(jnp.mean(diff * diff, axis=-1, keepdims=True), axis=0, keepdims=True)
        _accumulate(loss_ref, jnp.zeros(loss_ref.shape, F32) + part)
        dh3, dgf = _rms_bwd(diff * (1.0 / d), xh, r, gf_ref[...])
        dh3_ref[...] = dh3
        dh3b_ref[...] = _bf(dh3)
        _accumulate(dgf_ref, dgf)

    return pl.pallas_call(
        body, name="fwd_ffn", grid=(s // ts,),
        in_specs=[_row(ts, d), _const(g_ffn.shape), _const(wgut.shape), _const(wdown.shape), _const(g_final.shape),
                  _row(ts, d)],
        out_specs=(_row(ts, d), _row(ts, d), _row(ts, 2 * hid), _row(ts, d), _acc((1, 128)), _acc((1, d))),
        out_shape=(SDS((s, d), F32), SDS((s, d), BF16), SDS((s, 2 * hid), F32), SDS((s, d), BF16), SDS((1, 128), F32),
                   SDS((1, d), F32)),
        compiler_params=_cparams("arbitrary"))(h2, g_ffn, wgut, wdown, g_final, target)


def _bwd_ffn(h2, dh3, gu, g_ffn, wgut, wdown, ts):
    s, d = h2.shape
    hid = wdown.shape[0]
    hc = hid

    def body(h_ref, dh3_ref, gu_ref, g_ref, wgu_ref, wd_ref, dh2_ref, act_ref, dgu_ref, dg_ref):
        _, xh, r = _rms_fwd(h_ref[...], g_ref[...])
        db = _bf(dh3_ref[...])
        dhn = jnp.zeros((ts, d), F32)
        for n in range(hid // hc):
            wg = wgu_ref[hc * n:hc * (n + 1), :]
            wu = wgu_ref[hid + hc * n:hid + hc * (n + 1), :]
            g = gu_ref[:, hc * n:hc * (n + 1)]
            u = gu_ref[:, hid + hc * n:hid + hc * (n + 1)]
            sg = jax.nn.sigmoid(g)
            sl = g * sg
            act_ref[:, hc * n:hc * (n + 1)] = _bf(sl * u)
            dact = _dot(db, wd_ref[hc * n:hc * (n + 1), :], NT)
            dgb = _bf(dact * u * _silu_grad(g, sg))
            dub = _bf(dact * sl)
            dgu_ref[:, hc * n:hc * (n + 1)] = dgb
            dgu_ref[:, hid + hc * n:hid + hc * (n + 1)] = dub
            dhn = dhn + _dot(dgb, wg) + _dot(dub, wu)
        dx, dg = _rms_bwd(dhn, xh, r, g_ref[...])
        dh2_ref[...] = dh3_ref[...] + dx
        _accumulate(dg_ref, dg)

    return pl.pallas_call(
        body, name="bwd_ffn", grid=(s // ts,),
        in_specs=[_row(ts, d), _row(ts, d), _row(ts, 2 * hid), _const(g_ffn.shape), _const(wgut.shape),
                  _const(wdown.shape)],
        out_specs=(_row(ts, d), _row(ts, hid), _row(ts, 2 * hid), _acc((1, d))),
        out_shape=(SDS((s, d), F32), SDS((s, hid), BF16), SDS((s, 2 * hid), BF16), SDS((1, d), F32)),
        compiler_params=_cparams("arbitrary"))(h2, dh3, gu, g_ffn, wgut, wdown)


def _bwd_xa(h1, dh2, qb, ob, g_xa, wq, wo, kv, ts):
    s, d = h1.shape
    scale = XA_DH ** -0.5

    def body(h_ref, dh2_ref, q_ref, o_ref, g_ref, wq_ref, wo_ref, kv_ref, dh1_ref, dwq_ref, dwo_ref, dkv_ref, dg_ref,
             dq_scr, accq, acco):
        hn, xh, r = _rms_fwd(h_ref[...], g_ref[...])
        hb = _bf(hn)
        dh2b = _bf(dh2_ref[...])
        do = _dot(dh2b, wo_ref[...], NT)

        @pl.when(pl.program_id(0) == 0)
        def _():
            dkv_ref[...] = jnp.zeros_like(dkv_ref)
            accq[...] = jnp.zeros_like(accq)
            acco[...] = jnp.zeros_like(acco)

        for h in range(XA_HEADS):
            lo, hi = XA_DH * h, XA_DH * (h + 1)
            qh = q_ref[:, lo:hi]
            kh = _bf(kv_ref[:, lo:hi])
            vh = _bf(kv_ref[:, d + lo:d + hi])
            p = _softmax_rows(_dot(qh, kh, NT) * scale)
            pb = _bf(p)
            doh = _bf(do[:, lo:hi])
            dp = _dot(doh, vh, NT)
            ds = p * (dp - jnp.sum(p * dp, axis=-1, keepdims=True)) * scale
            dsb = _bf(ds)
            dq_scr[:, lo:hi] = _dot(dsb, kh)
            dkv_ref[:, lo:hi] += _dot(dsb, qh, TN)
            dkv_ref[:, d + lo:d + hi] += _dot(pb, doh, TN)
        dqb = _bf(dq_scr[...])
        accq[...] += _dot(hb, dqb, TN)
        acco[...] += _dot(o_ref[...], dh2b, TN)
        dx, dg = _rms_bwd(_dot(dqb, wq_ref[...], NT), xh, r, g_ref[...])
        dh1_ref[...] = dh2_ref[...] + dx
        _accumulate(dg_ref, dg)

        @pl.when(pl.program_id(0) == pl.num_programs(0) - 1)
        def _():
            dwq_ref[...] = _bf(accq[...])
            dwo_ref[...] = _bf(acco[...])

    return pl.pallas_call(
        body, name="bwd_xa", grid=(s // ts,),
        in_specs=[_row(ts, d), _row(ts, d), _row(ts, d), _row(ts, d), _const(g_xa.shape), _const(wq.shape),
                  _const(wo.shape), _const(kv.shape)],
        out_specs=(_row(ts, d), _acc((d, d)), _acc((d, d)), _acc(kv.shape), _acc((1, d))),
        out_shape=(SDS((s, d), F32), SDS((d, d), BF16), SDS((d, d), BF16), SDS(kv.shape, F32), SDS((1, d), F32)),
        scratch_shapes=[pltpu.VMEM((ts, d), F32), pltpu.VMEM((d, d), F32), pltpu.VMEM((d, d), F32)],
        compiler_params=_cparams("arbitrary"))(h1, dh2, qb, ob, g_xa, wq, wo, kv)


def _bwd_kv(dkv, mn, mem, g_mem, wkv3):
    d = mem.shape[1]

    def body(dkv_ref, mn_ref, mem_ref, g_ref, w_ref, dw_ref, dg_ref):
        dkvb = _bf(dkv_ref[...])
        dmn = jnp.zeros(mem_ref.shape, F32)
        for b in range(N_DEV):
            blk = dkvb[:, 256 * b:256 * (b + 1)]
            dmn = dmn + _dot(blk, w_ref[b], NT)
            dw_ref[b] = _bf(_dot(mn_ref[...], blk, TN))
        _, xh, r = _rms_fwd(mem_ref[...], g_ref[...])
        _, dg = _rms_bwd(dmn, xh, r, g_ref[...])
        dg_ref[...] = dg

    return pl.pallas_call(body, name="bwd_kv", out_shape=(SDS(wkv3.shape, BF16), SDS((1, d), F32)),
                          compiler_params=_cparams())(dkv, mn, mem, g_mem, wkv3)


def _bwd_out(dh1, c, z, cln_g, cln_b, gln_g, gln_b, ws, wst, bst, w_out, ts):
    s, d = dh1.shape
    nh = GM_HEADS

    def body(dh1_ref, c_ref, zuv_ref, clg, clb, glg, glb, ws_ref, wst_ref, bst_ref, wo_ref,
             dwo_ref, dc_ref, dzuv_ref, dws_ref, dbst_ref, dclg_ref, dclb_ref, dglg_ref, dglb_ref, dbin_ref,
             mixed_scr, dv_scr, acc):
        cl, chat, crs = _ln_fwd(c_ref[...], clg[...], clb[...])
        sg = jax.nn.sigmoid(cl)
        zu = zuv_ref[:, 0:512]
        zv = zuv_ref[:, 512:1024]
        u, tu = _gelu(zu)
        vg, tv = _gelu(zv)
        vln, vhat, vrs = _ln_fwd(vg, glg[...], glb[...])
        vb = _bf(vln)
        _mix_fwd(vb, ws_ref, bst_ref, mixed_scr, ts)
        mixed = mixed_scr[...]
        dh1b = _bf(dh1_ref[...])
        dcat = _dot(dh1b, wo_ref[...], NT)
        dgm = dcat[:, 512:1024]
        dc, dclg, dclb = _ln_bwd(dcat[:, 0:512] * _silu_grad(cl, sg), chat, crs, clg[...])
        dc_ref[...] = dc
        dzu = dgm * mixed * _gelu_grad(zu, tu)
        dm = dgm * u

        @pl.when(pl.program_id(0) == 0)
        def _():
            dws_ref[...] = jnp.zeros_like(dws_ref)
            dbst_ref[...] = jnp.zeros_like(dbst_ref)
            acc[...] = jnp.zeros_like(acc)

        acc[0:512, :] += _dot(_bf(cl * sg), dh1b, TN)
        acc[512:1024, :] += _dot(_bf(u * mixed), dh1b, TN)

        @pl.when(pl.program_id(0) == pl.num_programs(0) - 1)
        def _():
            dwo_ref[...] = _bf(acc[...])

        tril, triu, lo = _mix_masks()
        head = lax.broadcasted_iota(jnp.int32, (1, nh), 1)
        for j in range(nh // 2):
            w0t = _bf(jnp.where(triu, wst_ref[2 * j], 0.0))
            w1t = _bf(jnp.where(triu, wst_ref[2 * j + 1], 0.0))
            for n in range(ts // CHUNK):
                rows = slice(n * CHUNK, (n + 1) * CHUNK)
                lanes = slice(j * 128, (j + 1) * 128)
                dmc = dm[rows, lanes]
                dmb = _bf(dmc)
                dv_scr[rows, lanes] = jnp.where(lo, _dot(w0t, dmb), _dot(w1t, dmb))
                vc = vb[rows, lanes]
                d0 = jnp.where(lo, dmc, 0.0)
                d1 = dmc - d0
                dws_ref[2 * j] += jnp.where(tril, _dot(_bf(d0), vc, NT), 0.0)
                dws_ref[2 * j + 1] += jnp.where(tril, _dot(_bf(d1), vc, NT), 0.0)
                dbst_ref[...] += (jnp.sum(d0, axis=1, keepdims=True) * (head == 2 * j).astype(F32)
                                  + jnp.sum(d1, axis=1, keepdims=True) * (head == 2 * j + 1).astype(F32))
        dvg, dglg, dglb = _ln_bwd(dv_scr[...], vhat, vrs, glg[...])
        dzv = dvg * _gelu_grad(zv, tv)
        dzuv_ref[:, 0:512] = _bf(dzu)
        dzuv_ref[:, 512:1024] = _bf(dzv)
        _accumulate(dclg_ref, dclg)
        _accumulate(dclb_ref, dclb)
        _accumulate(dglg_ref, dglg)
        _accumulate(dglb_ref, dglb)
        _accumulate(dbin_ref, jnp.concatenate([jnp.sum(dzu, axis=0, keepdims=True),
                                               jnp.sum(dzv, axis=0, keepdims=True)], axis=1))

    vec = (1, 512)
    return pl.pallas_call(
        body, name="bwd_out", grid=(s // ts,),
        in_specs=[_row(ts, d), _row(ts, 512), _row(ts, 1024, 1), _const(cln_g.shape), _const(cln_b.shape),
                  _const(gln_g.shape), _const(gln_b.shape), _const(ws.shape), _const(wst.shape), _const(bst.shape),
                  _const(w_out.shape)],
        out_specs=(_acc((d, d)), _row(ts, 512), _row(ts, 1024), _acc(ws.shape), _acc(bst.shape), _acc(vec), _acc(vec),
                   _acc(vec), _acc(vec), _acc((1, 1024))),
        out_shape=(SDS((d, d), BF16), SDS((s, 512), F32), SDS((s, 1024), BF16), SDS(ws.shape, F32),
                   SDS(bst.shape, F32), SDS(vec, F32), SDS(vec, F32), SDS(vec, F32), SDS(vec, F32), SDS((1, 1024), F32)),
        scratch_shapes=[pltpu.VMEM((ts, 512), F32), pltpu.VMEM((ts, 512), F32), pltpu.VMEM((d, d), F32)],
        compiler_params=_cparams("arbitrary"))(dh1, c, z, cln_g, cln_b, gln_g, gln_b, ws, wst, bst, w_out)


CONV_RC = 256
FWD_TAPS = range(CONV_PAD - CONV_K + 1, CONV_PAD + 1)


def _bwd_in(x, dh1, dc, a, z, dzuv, g_mix, w_in3, conv_w, ts):
    s, d = x.shape
    cw = conv_w.shape[1]
    nb = cw // 128
    n = s // ts
    rc = min(CONV_RC, ts)
    per_halo = ts // CONV_PAD
    w4 = conv_w.reshape(CONV_K, nb, 128).transpose(1, 0, 2)

    def body(x_ref, dh1_ref, dc_ref, a_ref, ahalo_ref, zag_ref, dzuv_ref, g_ref, w_ref, cw_ref,
             dx_ref, dw_ref, dcw_ref, dcb_ref, dbin_ref, dg_ref, dz_ref, acc, pad_d, pad_a, part, da_scr):
        i = pl.program_id(0)

        @pl.when(i == 0)
        def _():
            acc[...] = jnp.zeros_like(acc)
            part[...] = jnp.zeros_like(part)
            pad_d[:, ts:, :] = jnp.zeros((nb, CONV_PAD, 128), F32)

        before = jnp.where(i == n - 1, 0.0, ahalo_ref[...])
        for blk in range(nb):
            lanes = slice(128 * blk, 128 * (blk + 1))
            pad_d[blk, 0:ts, :] = dc_ref[:, lanes]
            pad_a[blk, 0:CONV_PAD, :] = before[:, lanes]
            pad_a[blk, CONV_PAD:, :] = a_ref[:, lanes]

        def rows8(v):
            return jnp.sum(v.reshape(rc // 8, 8, 128), axis=0)

        def unit(blk, r0):
            acc_da = jnp.zeros((rc, 128), F32)
            for res, offs in _by_residue(range(0, CONV_K)):
                shifted = pad_d[blk, pl.ds(r0 + res, rc + offs[-1] - res), :]
                for off in offs:
                    k = CONV_K - 1 - off
                    acc_da = acc_da + cw_ref[blk, k:k + 1, :] * shifted[off - res:off - res + rc, :]
            da_scr[blk, pl.ds(r0, rc), :] = acc_da
            dcc = pad_d[blk, pl.ds(r0, rc), :]
            for res, offs in _by_residue(FWD_TAPS):
                shifted = pad_a[blk, pl.ds(r0 + res, rc + offs[-1] - res), :]
                for off in offs:
                    k = off - FWD_TAPS[0]
                    part[blk, 8 * k:8 * k + 8, :] += rows8(dcc * shifted[off - res:off - res + rc, :])
            part[blk, 8 * CONV_PAD:, :] += rows8(dcc)

        for blk in range(nb):
            for r0 in range(0, ts, rc):
                unit(blk, r0)
        pad_d[:, ts:, :] = pad_d[:, 0:CONV_PAD, :]

        za = zag_ref[:, 0:512]
        sg = jax.nn.sigmoid(zag_ref[:, 512:1024])
        da_ = jnp.concatenate([da_scr[blk] for blk in range(nb)], axis=1)
        dza = da_ * sg
        dzg = da_ * za * sg * (1.0 - sg)
        dz_ref[:, 0:512] = _bf(dza)
        dz_ref[:, 512:1024] = _bf(dzg)
        dz_ref[:, 1024:2048] = dzuv_ref[...]
        dhn = jnp.zeros((ts, d), F32)
        for b in range(N_DEV):
            dhn = dhn + _dot(dz_ref[:, 256 * b:256 * (b + 1)], w_ref[b], NT)
        hn, xh, r = _rms_fwd(x_ref[...], g_ref[...])
        acc[...] += _dot(_bf(hn), dz_ref[...], TN)

        @pl.when(i == n - 1)
        def _():
            for b in range(N_DEV):
                dw_ref[b] = _bf(acc[:, 256 * b:256 * (b + 1)])
            for blk in range(nb):
                sums = jnp.sum(part[blk].reshape(CONV_PAD + 1, 8, 128), axis=1)
                dcw_ref[:, 128 * blk:128 * (blk + 1)] = sums[0:CONV_PAD, :]
                dcb_ref[:, 128 * blk:128 * (blk + 1)] = sums[CONV_PAD:, :]

        dxn, dg = _rms_bwd(dhn, xh, r, g_ref[...])
        dx_ref[...] = dh1_ref[...] + dxn
        _accumulate(dg_ref, dg)
        _accumulate(dbin_ref, jnp.concatenate([jnp.sum(dza, axis=0, keepdims=True),
                                               jnp.sum(dzg, axis=0, keepdims=True)], axis=1))

    back = lambda width, col=0: pl.BlockSpec((ts, width), lambda i: (n - 1 - i, col))
    halo = pl.BlockSpec((CONV_PAD, cw), lambda i: (jnp.maximum((n - 1 - i) * per_halo - 1, 0), 0))
    return pl.pallas_call(
        body, name="bwd_in", grid=(n,),
        in_specs=[back(d), back(d), back(cw), back(cw), halo, back(1024, 0), back(1024), _const(g_mix.shape),
                  _const(w_in3.shape), _const(w4.shape)],
        out_specs=(back(d), _acc(w_in3.shape), _acc((CONV_PAD, cw)), _acc((1, cw)), _acc((1, 1024)), _acc((1, d))),
        out_shape=(SDS((s, d), F32), SDS(w_in3.shape, BF16), SDS((CONV_PAD, cw), F32), SDS((1, cw), F32),
                   SDS((1, 1024), F32), SDS((1, d), F32)),
        scratch_shapes=[pltpu.VMEM((ts, 2048), BF16), pltpu.VMEM((d, 2048), F32),
                        pltpu.VMEM((nb, ts + CONV_PAD, 128), F32), pltpu.VMEM((nb, ts + CONV_PAD, 128), F32),
                        pltpu.VMEM((nb, 8 * (CONV_PAD + 1), 128), F32), pltpu.VMEM((nb, ts, 128), F32)],
        compiler_params=_cparams("arbitrary"))(x, dh1, dc, a, a, z, dzuv, g_mix, w_in3, w4)


def _tn_matmul(a, b, tm, name):
    s, m = a.shape
    n = b.shape[1]
    ts = _row_tile(s, (1024, 512, 256, 128))
    n_s = s // ts

    def body(a_ref, b_ref, o_ref, acc):
        k = pl.program_id(1)

        @pl.when(k == 0)
        def _():
            acc[...] = jnp.zeros_like(acc)

        acc[...] += _dot(a_ref[...], b_ref[...], TN)

        @pl.when(k == n_s - 1)
        def _():
            o_ref[...] = _bf(acc[...])

    return pl.pallas_call(
        body, name=name, grid=(m // tm, n_s),
        in_specs=[pl.BlockSpec((ts, tm), lambda i, k: (k, i)), pl.BlockSpec((ts, n), lambda i, k: (k, 0))],
        out_specs=pl.BlockSpec((tm, n), lambda i, k: (i, 0)), out_shape=SDS((m, n), BF16),
        scratch_shapes=[pltpu.VMEM((tm, n), F32)],
        compiler_params=_cparams("parallel", "arbitrary"))(a, b)


def _row_tile(r, cands):
    for cand in cands:
        if r % cand == 0:
            return cand
    return r


def _sum_in_device_order(me_ref, land_ref, own_ref):
    acc = None
    for dev in range(N_DEV):
        part = jnp.where(me_ref[0] == dev, own_ref[0], land_ref[dev]).astype(F32)
        acc = part if acc is None else acc + part
    return acc


def _sum8(land, own, me, name):
    _, r, c = land.shape
    tr = _row_tile(r, (512, 256, 352, 128, 8))
    own3 = own if own.ndim == 3 else own[None]
    own_map = (lambda i, me_ref: (me_ref[0], i, 0)) if own.ndim == 3 else (lambda i, me_ref: (0, i, 0))

    def body(me_ref, land_ref, own_ref, o_ref):
        o_ref[...] = _sum_in_device_order(me_ref, land_ref, own_ref)

    return pl.pallas_call(
        body, name=name, out_shape=SDS((r, c), F32),
        grid_spec=pltpu.PrefetchScalarGridSpec(
            num_scalar_prefetch=1, grid=(r // tr,),
            in_specs=[pl.BlockSpec((N_DEV, tr, c), lambda i, me_ref: (0, i, 0)), pl.BlockSpec((1, tr, c), own_map)],
            out_specs=pl.BlockSpec((tr, c), lambda i, me_ref: (i, 0))),
        compiler_params=_cparams("parallel"))(me, land, own3)


def _adam_update(w, g, m, v):
    m2 = ADAM_B1 * m + (1.0 - ADAM_B1) * g
    v2 = ADAM_B2 * v + (1.0 - ADAM_B2) * (g * g)
    m_hat = m2 / (1.0 - ADAM_B1 ** ADAM_STEP)
    v_hat = v2 / (1.0 - ADAM_B2 ** ADAM_STEP)
    return -ADAM_LR * (m_hat / (jnp.sqrt(v_hat) + ADAM_EPS) + ADAM_WD * w), m2, v2


def _slot_spec(tr, c, mask):
    return pl.BlockSpec((1, tr, c), lambda i, me_ref: (me_ref[0] ^ mask, i, 0))


def _sum_adamw(land, parts, w, m, v, me, name, masks=ALL_PEERS):
    r, c = w.shape
    tr = _row_tile(r, (256, 128, 176, 8))
    n = len(masks)

    def body(me_ref, own_ref, *refs):
        w_ref, m_ref, v_ref, g_ref, d_ref, m2_ref, v2_ref = refs[n:]
        g = own_ref[0].astype(F32)
        for peer_ref in refs[:n]:
            g = g + peer_ref[0].astype(F32)
        g_ref[...] = g
        d_ref[...], m2_ref[...], v2_ref[...] = _adam_update(w_ref[...], g, m_ref[...], v_ref[...])

    blk = pl.BlockSpec((tr, c), lambda i, me_ref: (i, 0))
    return pl.pallas_call(
        body, name=name, out_shape=(SDS((r, c), F32),) * 4,
        grid_spec=pltpu.PrefetchScalarGridSpec(
            num_scalar_prefetch=1, grid=(r // tr,),
            in_specs=[_slot_spec(tr, c, 0)] + [_slot_spec(tr, c, mask) for mask in masks] + [blk, blk, blk],
            out_specs=(blk,) * 4),
        compiler_params=_cparams("parallel"))(me, parts, *([land] * n), w, m, v)


def _chip_sum(parts, from_sibling, me, name):
    _, r, c = parts.shape
    tr = _row_tile(r, (512, 256, 352, 128, 8))
    spec = pl.BlockSpec((1, tr, c), lambda j, i, me_ref: (me_ref[0] ^ (2 * j), i, 0))

    def body(me_ref, p_ref, q_ref, o_ref):
        o_ref[...] = _bf(p_ref[...].astype(F32) + q_ref[...].astype(F32))

    return pl.pallas_call(
        body, name=name, out_shape=SDS(parts.shape, BF16),
        grid_spec=pltpu.PrefetchScalarGridSpec(num_scalar_prefetch=1, grid=(len(CHIPS), r // tr), in_specs=[spec, spec],
                                               out_specs=spec),
        compiler_params=_cparams("parallel", "parallel"))(me, parts, from_sibling)


def _adamw_many(ws, gs, ms, vs, name):
    n = len(ws)

    def body(*refs):
        ins, outs = refs[:4 * n], refs[4 * n:]
        for i in range(n):
            w, g, m, v = (ins[j * n + i][...] for j in range(4))
            outs[i][...], outs[n + i][...], outs[2 * n + i][...] = _adam_update(w, g, m, v)

    shapes = [SDS(w.shape, F32) for w in ws]
    out = pl.pallas_call(body, name=name, out_shape=shapes * 3, compiler_params=_cparams())(*ws, *gs, *ms, *vs)
    return out[:n], out[n:2 * n], out[2 * n:]


def _pack(arrs):
    flat = jnp.concatenate([a.reshape(-1) for a in arrs])
    pad = (-flat.shape[0]) % (128 * 128)
    return jnp.pad(flat, (0, pad)).reshape(-1, 128)


def _unpack(packed, shapes):
    flat = packed.reshape(-1)
    out, off = [], 0
    for shp in shapes:
        size = 1
        for dim in shp:
            size *= dim
        out.append(flat[off:off + size].reshape(shp))
        off += size
    return out


SMALL = ("norm_mix_g", "b_in", "conv_w", "conv_b", "conv_ln_g", "conv_ln_b", "gm_ln_g", "gm_ln_b", "gm_w_s", "gm_b_s",
         "norm_xa_g", "mem_norm_g", "norm_ffn_g", "final_norm_g")
BIG = ("w_in", "w_out", "xa_wq", "xa_wkv", "xa_wo", "ffn_w_gate_up", "ffn_w_down")
WEIGHTS = ("norm_mix_g", "w_in", "b_in", "conv_w", "conv_b", "conv_ln_g", "conv_ln_b", "gm_ln_g", "gm_ln_b", "gm_w_s",
           "gm_b_s", "w_out", "norm_xa_g", "mem_norm_g", "xa_wq", "xa_wkv", "xa_wo", "norm_ffn_g", "ffn_w_gate_up",
           "ffn_w_down", "final_norm_g")


def kernel(x, mem, norm_mix_g, w_in, b_in, conv_w, conv_b, conv_ln_g, conv_ln_b, gm_ln_g, gm_ln_b, gm_w_s, gm_b_s, w_out, norm_xa_g, mem_norm_g, xa_wq, xa_wkv, xa_wo, norm_ffn_g, ffn_w_gate_up, ffn_w_down, final_norm_g, loss_target, m_norm_mix_g, m_w_in, m_b_in, m_conv_w, m_conv_b, m_conv_ln_g, m_conv_ln_b, m_gm_ln_g, m_gm_ln_b, m_gm_w_s, m_gm_b_s, m_w_out, m_norm_xa_g, m_mem_norm_g, m_xa_wq, m_xa_wkv, m_xa_wo, m_norm_ffn_g, m_ffn_w_gate_up, m_ffn_w_down, m_final_norm_g, v_norm_mix_g, v_w_in, v_b_in, v_conv_w, v_conv_b, v_conv_ln_g, v_conv_ln_b, v_gm_ln_g, v_gm_ln_b, v_gm_w_s, v_gm_b_s, v_w_out, v_norm_xa_g, v_mem_norm_g, v_xa_wq, v_xa_wkv, v_xa_wo, v_norm_ffn_g, v_ffn_w_gate_up, v_ffn_w_down, v_final_norm_g):
    w = dict(norm_mix_g=norm_mix_g, w_in=w_in, b_in=b_in, conv_w=conv_w, conv_b=conv_b, conv_ln_g=conv_ln_g,
             conv_ln_b=conv_ln_b, gm_ln_g=gm_ln_g, gm_ln_b=gm_ln_b, gm_w_s=gm_w_s, gm_b_s=gm_b_s, w_out=w_out,
             norm_xa_g=norm_xa_g, mem_norm_g=mem_norm_g, xa_wq=xa_wq, xa_wkv=xa_wkv, xa_wo=xa_wo,
             norm_ffn_g=norm_ffn_g, ffn_w_gate_up=ffn_w_gate_up, ffn_w_down=ffn_w_down, final_norm_g=final_norm_g)
    mom = dict(norm_mix_g=m_norm_mix_g, w_in=m_w_in, b_in=m_b_in, conv_w=m_conv_w, conv_b=m_conv_b,
               conv_ln_g=m_conv_ln_g, conv_ln_b=m_conv_ln_b, gm_ln_g=m_gm_ln_g, gm_ln_b=m_gm_ln_b, gm_w_s=m_gm_w_s,
               gm_b_s=m_gm_b_s, w_out=m_w_out, norm_xa_g=m_norm_xa_g, mem_norm_g=m_mem_norm_g, xa_wq=m_xa_wq,
               xa_wkv=m_xa_wkv, xa_wo=m_xa_wo, norm_ffn_g=m_norm_ffn_g, ffn_w_gate_up=m_ffn_w_gate_up,
               ffn_w_down=m_ffn_w_down, final_norm_g=m_final_norm_g)
    var = dict(norm_mix_g=v_norm_mix_g, w_in=v_w_in, b_in=v_b_in, conv_w=v_conv_w, conv_b=v_conv_b,
               conv_ln_g=v_conv_ln_g, conv_ln_b=v_conv_ln_b, gm_ln_g=v_gm_ln_g, gm_ln_b=v_gm_ln_b, gm_w_s=v_gm_w_s,
               gm_b_s=v_gm_b_s, w_out=v_w_out, norm_xa_g=v_norm_xa_g, mem_norm_g=v_mem_norm_g, xa_wq=v_xa_wq,
               xa_wkv=v_xa_wkv, xa_wo=v_xa_wo, norm_ffn_g=v_norm_ffn_g, ffn_w_gate_up=v_ffn_w_gate_up,
               ffn_w_down=v_ffn_w_down, final_norm_g=v_final_norm_g)

    me = 4 * lax.axis_index("x") + 2 * lax.axis_index("y") + lax.axis_index("c")
    s, d = x.shape[1], x.shape[2]
    xs = x.reshape(s, d)
    mems = mem.reshape(mem.shape[1], d)
    tgt = loss_target.reshape(s, d)
    ts = min(512, s)
    ts_ffn = min(256, s)
    row = lambda a: a.reshape(1, -1)

    conv_w_pad = jnp.pad(conv_w, ((0, CONV_PAD - CONV_K), (0, 128 - conv_w.shape[1])))
    first_level = (SIBLING,) + OTHER_CHIPS
    ex_first = _exchange_start([_bf(w_in), conv_w_pad], [False] * 2, "gather_start_in", first_level, own_slot=me)
    behind = lambda t: _bf(t + ex_first[4][0:1, 0:1])
    shards = [behind(xa_wkv), behind(w_out), behind(xa_wq), behind(xa_wo), behind(ffn_w_gate_up.T), behind(ffn_w_down)]
    ex_rest = _exchange_start(shards, [False] * len(shards), "gather_start_rest", first_level, own_slot=me)
    g_send, g_recv, g_src, g_land = (list(ex_first[i]) + list(ex_rest[i]) for i in range(4))
    g_tok = ex_rest[4]

    def arrived(idx, after, name):
        pick = lambda seq: [seq[i] for i in idx]
        return _wait_and_forward(pick(g_send), pick(g_recv), pick(g_src), pick(g_land), after, "gather_pass_" + name,
                                 first_level)

    def complete(handle, after, name):
        return _forward_wait(handle[0], handle[1], handle[2], after, "forward_wait_" + name)

    bst = gm_b_s.T
    wst = jnp.swapaxes(gm_w_s, 1, 2)

    h_in = arrived((0, 1), g_tok, "in")
    w_in3, conv_w8 = complete(h_in, h_in[3], "in")
    conv_w_f = conv_w8[:, :CONV_K, :conv_w.shape[1]].transpose(1, 0, 2).reshape(CONV_K, -1)
    cw = conv_w_f.shape[1]
    z, a = _fwd_in(xs, row(norm_mix_g), w_in3, row(b_in), ts)
    h_out = arrived((2, 3), z, "out")
    c = _conv_fwd(a, conv_w_f + h_out[3][0:1, 0:1], row(conv_b))
    h_xa = arrived((4, 5), c, "xa")
    wkv3, w_out3 = complete(h_out, h_xa[3], "out")
    w_out_f = w_out3.reshape(-1, d)
    kv, mn = _kv_proj(mems, row(mem_norm_g), wkv3)
    h1 = _fwd_out(xs, c, z, row(conv_ln_g), row(conv_ln_b), row(gm_ln_g), row(gm_ln_b), gm_w_s, bst, w_out_f, ts)
    h_gut = arrived((6,), h1, "gate_up")
    wq3, wo3 = complete(h_xa, h_gut[3], "xa")
    wq_f = wq3.reshape(-1, d)
    wo_f = wo3.reshape(-1, d)
    h2, qb, ob = _fwd_xa(h1, row(norm_xa_g), wq_f, kv, wo_f, ts)
    h_down = arrived((7,), h2, "down")
    (wgut3,) = complete(h_gut, h_down[3], "gate_up")
    (wdown3,) = complete(h_down, wgut3, "down")
    wgut_f = wgut3.reshape(-1, d)
    wdown_f = wdown3.reshape(-1, d)
    dh3, dh3b, gu, hn2, loss_p, d_final_g = _fwd_ffn(h2, row(norm_ffn_g), wgut_f, wdown_f, row(final_norm_g), tgt,
                                                     ts_ffn)

    blocks = lambda m: m.reshape(N_DEV, -1, d)
    tok = lambda ex: ex[4][0:1, 0:1]
    me1 = me.astype(jnp.int32).reshape(1)

    def chip_sums(pair, after, tag):
        parts, handed = _pair_wait(pair[0], pair[1], pair[2], pair[3], after, "pair_wait_" + tag)
        return [_chip_sum(p, h, me1, "chip_sum_%s_%d" % (tag, i)) for i, (p, h) in enumerate(zip(parts, handed))]

    dh2, act, dgu, d_ffn_g = _bwd_ffn(h2, dh3, gu, row(norm_ffn_g), wgut_f, wdown_f, ts_ffn)
    dwgut = _tn_matmul(dgu, hn2, 2816, "dw_gate_up")
    dwdown = _tn_matmul(act, dh3b, 2816, "dw_down")
    pair_ffn = _pair_start([blocks(dwgut), blocks(dwdown)], "pair_start_ffn")
    dh1, dwq, dwo, dkv, d_xa_g = _bwd_xa(h1, dh2, qb, ob, row(norm_xa_g) + tok(pair_ffn), wq_f, wo_f, kv, ts)
    ex_ffn = _exchange_start(chip_sums(pair_ffn, dh1, "ffn"), [True] * 2, "scatter_start_ffn", OTHER_CHIPS)
    dwkv3, d_mem_g = _bwd_kv(dkv, mn, mems, row(mem_norm_g) + tok(ex_ffn), wkv3)
    pair_xa = _pair_start([blocks(dwq), blocks(dwo), dwkv3], "pair_start_xa")
    (dwout, dc, dzuv, dws, dbst, d_cln_g, d_cln_b, d_gln_g, d_gln_b, dbin_uv) = _bwd_out(
        dh1, c, z, row(conv_ln_g) + tok(pair_xa), row(conv_ln_b), row(gm_ln_g), row(gm_ln_b), gm_w_s, wst, bst,
        w_out_f, ts)
    early = dict(b_in_uv=dbin_uv, conv_ln_g=d_cln_g, conv_ln_b=d_cln_b, gm_ln_g=d_gln_g, gm_ln_b=d_gln_b,
                 gm_b_s=dbst.T, norm_xa_g=d_xa_g, mem_norm_g=d_mem_g, norm_ffn_g=d_ffn_g, final_norm_g=d_final_g,
                 loss=loss_p)
    dws_b = _bf(dws).reshape(GM_HEADS * CHUNK, CHUNK)
    ex_xa = _exchange_start(chip_sums(pair_xa, dc, "xa") + [blocks(dwout), dws_b, _pack(list(early.values()))],
                            [True] * 4 + [False] * 2, "scatter_start_xa", [OTHER_CHIPS] * 3 + [ALL_PEERS] * 3)
    dx, dw_in3, dconv_w, dconv_b, dbin_ag, d_mix_g = _bwd_in(xs, dh1, dc, a, z, dzuv, row(norm_mix_g) + tok(ex_xa),
                                                             w_in3, conv_w_f, ts)
    late = dict(norm_mix_g=d_mix_g, b_in_ag=dbin_ag, conv_w=dconv_w[:CONV_K], conv_b=dconv_b)
    ex_last = _exchange_start([_pack(list(late.values())), dw_in3], [False, True], "scatter_start_in")

    grads, delta, new_m, new_v = {}, {}, {}, {}

    def waited(ex, idx, scatter, after, name, masks=ALL_PEERS):
        return _exchange_wait(*[[ex[i][k] for k in idx] for i in range(4)], scatter, after, name, masks)

    def reduced(names, srcs, lands, masks):
        for nm, src, land, mk in zip(names, srcs, lands, masks):
            view = (lambda t: t.T) if nm == "ffn_w_gate_up" else (lambda t: t)
            upd = _sum_adamw(land, src, view(w[nm]), view(mom[nm]), view(var[nm]), me1, "adamw_" + nm, mk)
            grads[nm], delta[nm], new_m[nm], new_v[nm] = (view(t) for t in upd)
        return new_v[names[-1]]

    after = reduced(("ffn_w_gate_up", "ffn_w_down"),
                    *waited(ex_ffn, (0, 1), [True] * 2, ex_last[4], "scatter_wait_ffn", OTHER_CHIPS), [OTHER_CHIPS] * 2)
    xa_masks = [OTHER_CHIPS] * 3 + [ALL_PEERS]
    after = reduced(("xa_wq", "xa_wo", "xa_wkv", "w_out"),
                    *waited(ex_xa, (0, 1, 2, 3), [True] * 4, after, "scatter_wait_xa", xa_masks), xa_masks)

    (dws_src, early_src), (dws_land, early_land) = waited(ex_xa, (4, 5), [False] * 2, after, "gather_wait_early")
    (late_src,), (late_land,) = waited(ex_last, (0,), [False], early_land, "gather_wait_late")
    dws_tot = _sum8(dws_land, dws_src, me1, "sum_gm_w_s")
    early_tot = _sum8(early_land, early_src, me1, "sum_small_early")
    late_tot = _sum8(late_land, late_src, me1, "sum_small_late")
    early_g = dict(zip(early, _unpack(early_tot, [v.shape for v in early.values()])), gm_w_s=dws_tot)
    late_g = dict(zip(late, _unpack(late_tot, [v.shape for v in late.values()])))
    loss = early_g["loss"][0, 0]
    for nm in SMALL:
        if nm == "b_in":
            g = jnp.concatenate([late_g["b_in_ag"], early_g["b_in_uv"]], axis=1)
        elif nm == "conv_w":
            g = lax.dynamic_slice_in_dim(late_g[nm], me * conv_w.shape[1], conv_w.shape[1], axis=1)
        else:
            g = late_g[nm] if nm in late_g else early_g[nm]
        grads[nm] = g.reshape(w[nm].shape)

    small = [[src[nm] for nm in SMALL] for src in (w, grads, mom, var)]
    for dst, vals in zip((delta, new_m, new_v), _adamw_many(*small, "adamw_small")):
        dst.update(zip(SMALL, vals))

    reduced(("w_in",), *waited(ex_last, (1,), [True], delta[SMALL[0]], "scatter_wait_in"), [ALL_PEERS])

    return (loss, dx.reshape(x.shape), *[grads[nm] for nm in WEIGHTS], *[delta[nm] for nm in WEIGHTS],
            *[new_m[nm] for nm in WEIGHTS], *[new_v[nm] for nm in WEIGHTS])
```

```python
import jax
import jax.numpy as jnp
from jax import lax
from jax.experimental import pallas as pl
from jax.experimental.pallas import tpu as pltpu

F32 = jnp.float32
BF16 = jnp.bfloat16
SDS = jax.ShapeDtypeStruct

N_DEV = 8
RMS_EPS = 1e-6
LN_EPS = 1e-5
CONV_K = 31
CONV_PAD = 32
CHUNK = 128
GM_HEADS = 8
XA_HEADS = 4
XA_DH = 256
GELU_K0 = 0.7978845608028654
GELU_K1 = 0.044715
ADAM_LR = 0.001
ADAM_B1 = 0.9
ADAM_B2 = 0.999
ADAM_EPS = 1e-08
ADAM_WD = 0.01
ADAM_STEP = 10
VMEM_LIMIT = 60 * 1024 * 1024

NN = (((1,), (0,)), ((), ()))
NT = (((1,), (1,)), ((), ()))
TN = (((0,), (0,)), ((), ()))


def _dot(a, b, dims=NN):
    return lax.dot_general(a, b, dims, preferred_element_type=F32)


def _bf(x):
    return x.astype(BF16)


def _cparams(*sem):
    return pltpu.CompilerParams(dimension_semantics=tuple(sem) if sem else None, vmem_limit_bytes=VMEM_LIMIT)


def _row(ts, w, col=0):
    return pl.BlockSpec((ts, w), lambda i: (i, col))


def _const(shape):
    nd = len(shape)
    return pl.BlockSpec(shape, lambda i: (0,) * nd, pipeline_mode=pl.Buffered(1))


def _acc(shape):
    nd = len(shape)
    return pl.BlockSpec(shape, lambda i: (0,) * nd)


_ORDER_ONLY = pl.BlockSpec(memory_space=pl.ANY)


def _rms_fwd(x, g):
    r = lax.rsqrt(jnp.mean(x * x, axis=-1, keepdims=True) + RMS_EPS)
    xh = x * r
    return xh * g, xh, r


def _rms_bwd(dy, xh, r, g):
    gdy = dy * g
    dx = r * (gdy - xh * jnp.mean(gdy * xh, axis=-1, keepdims=True))
    dg = jnp.sum(dy * xh, axis=0, keepdims=True)
    return dx, dg


def _ln_fwd(x, g, b):
    mu = jnp.mean(x, axis=-1, keepdims=True)
    xc = x - mu
    rs = lax.rsqrt(jnp.mean(xc * xc, axis=-1, keepdims=True) + LN_EPS)
    xh = xc * rs
    return xh * g + b, xh, rs


def _ln_bwd(dy, xh, rs, g):
    dxh = dy * g
    dx = rs * (dxh - jnp.mean(dxh, axis=-1, keepdims=True) - xh * jnp.mean(dxh * xh, axis=-1, keepdims=True))
    return dx, jnp.sum(dy * xh, axis=0, keepdims=True), jnp.sum(dy, axis=0, keepdims=True)


def _gelu(x):
    t = jnp.tanh(GELU_K0 * (x + GELU_K1 * (x * x * x)))
    return 0.5 * x * (1.0 + t), t


def _gelu_grad(x, t):
    return 0.5 * (1.0 + t) + 0.5 * x * (1.0 - t * t) * (GELU_K0 * (1.0 + 3.0 * GELU_K1 * x * x))


def _silu_grad(x, sg):
    return sg * (1.0 + x * (1.0 - sg))


def _by_residue(offsets):
    groups = {}
    for off in offsets:
        groups.setdefault(off % 8, []).append(off)
    return [(res, sorted(offs)) for res, offs in sorted(groups.items())]


def _accumulate(ref, val):
    @pl.when(pl.program_id(0) == 0)
    def _():
        ref[...] = jnp.zeros_like(ref)
    ref[...] += val


def _mix_masks():
    row = lax.broadcasted_iota(jnp.int32, (CHUNK, CHUNK), 0)
    col = lax.broadcasted_iota(jnp.int32, (CHUNK, CHUNK), 1)
    return row >= col, row <= col, col < (CHUNK // 2)


def _mix_fwd(vb, ws_ref, bst_ref, mixed_scr, ts):
    tril, _, lo = _mix_masks()
    for j in range(GM_HEADS // 2):
        w0 = _bf(jnp.where(tril, ws_ref[2 * j], 0.0))
        w1 = _bf(jnp.where(tril, ws_ref[2 * j + 1], 0.0))
        bias = jnp.where(lo, bst_ref[:, 2 * j:2 * j + 1], bst_ref[:, 2 * j + 1:2 * j + 2])
        for n in range(ts // CHUNK):
            v = vb[n * CHUNK:(n + 1) * CHUNK, j * 128:(j + 1) * 128]
            mixed_scr[n * CHUNK:(n + 1) * CHUNK, j * 128:(j + 1) * 128] = jnp.where(lo, _dot(w0, v), _dot(w1, v)) + bias


def _peer_of(mask):
    x, y, c = lax.axis_index("x"), lax.axis_index("y"), lax.axis_index("c")
    px = 1 - x if (mask >> 2) & 1 else x
    py = 1 - y if (mask >> 1) & 1 else y
    pc = 1 - c if mask & 1 else c
    return (px, py, pc), 4 * px + 2 * py + pc


ALL_PEERS = tuple(range(1, N_DEV))
OTHER_CHIPS = (2, 4, 6)
CHIPS = (0,) + OTHER_CHIPS
SIBLING = 1


def _split_copy(src_ref, land_ref, send_sem, recv_sem, mask, slot, scatter, outgoing):
    x, y, c = lax.axis_index("x"), lax.axis_index("y"), lax.axis_index("c")
    me = 4 * x + 2 * y + c
    peer, pidx = _peer_of(mask)
    return pltpu.make_async_remote_copy(
        src_ref=src_ref.at[pidx] if scatter else src_ref,
        dst_ref=land_ref.at[me if outgoing else pidx],
        send_sem=send_sem.at[slot], recv_sem=recv_sem.at[slot],
        device_id=peer, device_id_type=pl.DeviceIdType.MESH)


_HBM = pl.BlockSpec(memory_space=pltpu.HBM)
_SEM = pl.BlockSpec(memory_space=pltpu.SEMAPHORE)
_EFFECT = pltpu.SideEffectType.DATAFLOW_SIDE_EFFECTING


def _per_array(masks, n):
    return [tuple(masks)] * n if isinstance(masks[0], int) else [tuple(m) for m in masks]


def _exchange_start(srcs, scatter, name, masks=ALL_PEERS, own_slot=None):
    n = len(srcs)
    masks = _per_array(masks, n)
    lands = [lax.empty((N_DEV,) + tuple(s.shape[1:] if sc else s.shape), s.dtype) for s, sc in zip(srcs, scatter)]
    if own_slot is not None:
        lands = [land if sc else _with_own(land, s, own_slot) for land, s, sc in zip(lands, srcs, scatter)]
    lands = [pltpu.with_memory_space_constraint(land, pltpu.HBM) for land in lands]
    srcs = [pltpu.with_memory_space_constraint(s, pltpu.HBM) for s in srcs]

    def body(*refs):
        src_refs, land_refs = refs[:n], refs[n:2 * n]
        send_sems, recv_sems = refs[2 * n:3 * n], refs[3 * n:4 * n]
        token = refs[-1]
        for k in range(n):
            for slot, mask in enumerate(masks[k]):
                _split_copy(src_refs[k], land_refs[k], send_sems[k], recv_sems[k], mask, slot, scatter[k], True).start()
        token[...] = jnp.zeros_like(token)

    sems = [pltpu.SemaphoreType.DMA((len(m),)) for m in masks]
    out = pl.pallas_call(
        body, name=name,
        out_shape=tuple(sems + sems + [pltpu.HBM(s.shape, s.dtype) for s in srcs]
                        + [pltpu.HBM(l.shape, l.dtype) for l in lands] + [SDS((8, 128), F32)]),
        in_specs=[_HBM] * (2 * n),
        out_specs=tuple([_SEM] * (2 * n) + [_HBM] * (2 * n) + [pl.BlockSpec(memory_space=pltpu.VMEM)]),
        input_output_aliases={i: 2 * n + i for i in range(2 * n)},
        compiler_params=pltpu.CompilerParams(has_side_effects=_EFFECT),
    )(*srcs, *lands)
    return out[:n], out[n:2 * n], out[2 * n:3 * n], out[3 * n:4 * n], out[-1]


def _exchange_wait(send_sems, recv_sems, srcs_thru, lands_thru, scatter, after, name, masks=ALL_PEERS):
    n = len(srcs_thru)
    masks = _per_array(masks, n)

    def body(*refs):
        src_refs, land_refs = refs[:n], refs[n:2 * n]
        send_refs, recv_refs = refs[2 * n:3 * n], refs[3 * n:4 * n]
        for k in range(n):
            for slot, mask in enumerate(masks[k]):
                args = (src_refs[k], land_refs[k], send_refs[k], recv_refs[k], mask, slot, scatter[k])
                _split_copy(*args, True).wait_send()
                _split_copy(*args, False).wait_recv()

    out = pl.pallas_call(
        body, name=name,
        out_shape=tuple([pltpu.HBM(s.shape, s.dtype) for s in srcs_thru]
                        + [pltpu.HBM(l.shape, l.dtype) for l in lands_thru]),
        in_specs=[_HBM] * (2 * n) + [_SEM] * (2 * n) + [pl.BlockSpec(memory_space=pl.ANY)],
        out_specs=tuple([_HBM] * (2 * n)),
        input_output_aliases={i: i for i in range(2 * n)},
        compiler_params=pltpu.CompilerParams(has_side_effects=_EFFECT),
    )(*srcs_thru, *lands_thru, *send_sems, *recv_sems, after)
    return out[:n], out[n:]


def _wait_and_forward(send_sems, recv_sems, srcs_thru, lands_thru, after, name, masks):
    n = len(lands_thru)

    def body(*refs):
        src_refs, land_refs = refs[:n], refs[n:2 * n]
        send_refs, recv_refs = refs[2 * n:3 * n], refs[3 * n:4 * n]
        outs = refs[4 * n + 1:]
        fsend, frecv, token = outs[2 * n:3 * n], outs[3 * n:4 * n], outs[-1]
        for k in range(n):
            for slot, mask in enumerate(masks):
                args = (src_refs[k], land_refs[k], send_refs[k], recv_refs[k], mask, slot, False)
                _split_copy(*args, True).wait_send()
                _split_copy(*args, False).wait_recv()
        sibling, _ = _peer_of(SIBLING)
        for k in range(n):
            for slot, mask in enumerate(OTHER_CHIPS):
                _, mine = _peer_of(mask)
                pltpu.make_async_remote_copy(
                    src_ref=land_refs[k].at[mine], dst_ref=land_refs[k].at[mine], send_sem=fsend[k].at[slot],
                    recv_sem=frecv[k].at[slot], device_id=sibling, device_id_type=pl.DeviceIdType.MESH).start()
        token[...] = jnp.zeros_like(token)

    sem = pltpu.SemaphoreType.DMA((len(OTHER_CHIPS),))
    out = pl.pallas_call(
        body, name=name,
        out_shape=tuple([pltpu.HBM(s.shape, s.dtype) for s in srcs_thru] + [pltpu.HBM(l.shape, l.dtype) for l in lands_thru]
                        + [sem] * (2 * n) + [SDS((8, 128), F32)]),
        in_specs=[_HBM] * (2 * n) + [_SEM] * (2 * n) + [pl.BlockSpec(memory_space=pl.ANY)],
        out_specs=tuple([_HBM] * (2 * n) + [_SEM] * (2 * n) + [pl.BlockSpec(memory_space=pltpu.VMEM)]),
        input_output_aliases={i: i for i in range(2 * n)},
        compiler_params=pltpu.CompilerParams(has_side_effects=_EFFECT),
    )(*srcs_thru, *lands_thru, *send_sems, *recv_sems, after)
    return out[2 * n:3 * n], out[3 * n:4 * n], out[n:2 * n], out[-1]


def _forward_wait(send_sems, recv_sems, lands, after, name):
    n = len(lands)

    def body(*refs):
        land_refs, send_refs, recv_refs = refs[:n], refs[n:2 * n], refs[2 * n:3 * n]
        sibling, _ = _peer_of(SIBLING)
        for k in range(n):
            for slot, mask in enumerate(OTHER_CHIPS):
                _, mine = _peer_of(mask)
                _, theirs = _peer_of(mask | SIBLING)
                for block, wait in ((mine, "wait_send"), (theirs, "wait_recv")):
                    getattr(pltpu.make_async_remote_copy(
                        src_ref=land_refs[k].at[block], dst_ref=land_refs[k].at[block],
                        send_sem=send_refs[k].at[slot], recv_sem=recv_refs[k].at[slot], device_id=sibling,
                        device_id_type=pl.DeviceIdType.MESH), wait)()

    return pl.pallas_call(
        body, name=name, out_shape=tuple(pltpu.HBM(l.shape, l.dtype) for l in lands),
        in_specs=[_HBM] * n + [_SEM] * (2 * n) + [pl.BlockSpec(memory_space=pl.ANY)],
        out_specs=tuple([_HBM] * n), input_output_aliases={i: i for i in range(n)},
        compiler_params=pltpu.CompilerParams(has_side_effects=_EFFECT),
    )(*lands, *send_sems, *recv_sems, after)


def _pair_copy(parts_ref, land_ref, send_sem, recv_sem, slot, chip, outgoing):
    sibling, _ = _peer_of(SIBLING)
    _, block = _peer_of(chip | SIBLING if outgoing else chip)
    return pltpu.make_async_remote_copy(
        src_ref=parts_ref.at[block], dst_ref=land_ref.at[block], send_sem=send_sem.at[slot], recv_sem=recv_sem.at[slot],
        device_id=sibling, device_id_type=pl.DeviceIdType.MESH)


def _pair_start(parts, name):
    n = len(parts)
    lands = [pltpu.with_memory_space_constraint(lax.empty(p.shape, p.dtype), pltpu.HBM) for p in parts]
    parts = [pltpu.with_memory_space_constraint(p, pltpu.HBM) for p in parts]

    def body(*refs):
        part_refs, land_refs = refs[:n], refs[n:2 * n]
        send_sems, recv_sems, token = refs[2 * n:3 * n], refs[3 * n:4 * n], refs[-1]
        for k in range(n):
            for slot, chip in enumerate(CHIPS):
                _pair_copy(part_refs[k], land_refs[k], send_sems[k], recv_sems[k], slot, chip, True).start()
        token[...] = jnp.zeros_like(token)

    sem = pltpu.SemaphoreType.DMA((len(CHIPS),))
    out = pl.pallas_call(
        body, name=name,
        out_shape=tuple([sem] * (2 * n) + [pltpu.HBM(p.shape, p.dtype) for p in parts + lands] + [SDS((8, 128), F32)]),
        in_specs=[_HBM] * (2 * n),
        out_specs=tuple([_SEM] * (2 * n) + [_HBM] * (2 * n) + [pl.BlockSpec(memory_space=pltpu.VMEM)]),
        input_output_aliases={i: 2 * n + i for i in range(2 * n)},
        compiler_params=pltpu.CompilerParams(has_side_effects=_EFFECT),
    )(*parts, *lands)
    return out[:n], out[n:2 * n], out[2 * n:3 * n], out[3 * n:4 * n], out[-1]


def _pair_wait(send_sems, recv_sems, parts_thru, lands_thru, after, name):
    n = len(parts_thru)

    def body(*refs):
        part_refs, land_refs = refs[:n], refs[n:2 * n]
        send_refs, recv_refs = refs[2 * n:3 * n], refs[3 * n:4 * n]
        for k in range(n):
            for slot, chip in enumerate(CHIPS):
                args = (part_refs[k], land_refs[k], send_refs[k], recv_refs[k], slot, chip)
                _pair_copy(*args, True).wait_send()
                _pair_copy(*args, False).wait_recv()

    out = pl.pallas_call(
        body, name=name,
        out_shape=tuple(pltpu.HBM(p.shape, p.dtype) for p in list(parts_thru) + list(lands_thru)),
        in_specs=[_HBM] * (2 * n) + [_SEM] * (2 * n) + [pl.BlockSpec(memory_space=pl.ANY)],
        out_specs=tuple([_HBM] * (2 * n)), input_output_aliases={i: i for i in range(2 * n)},
        compiler_params=pltpu.CompilerParams(has_side_effects=_EFFECT),
    )(*parts_thru, *lands_thru, *send_sems, *recv_sems, after)
    return out[:n], out[n:]


def _with_own(landed, own, me):
    return lax.dynamic_update_slice_in_dim(landed, own[None], me, axis=0)


def _kv_proj(mem, g_mem, wkv3):
    m = mem.shape[0]

    def body(mem_ref, g_ref, w_ref, kv_ref, mn_ref):
        y, _, _ = _rms_fwd(mem_ref[...], g_ref[...])
        yb = _bf(y)
        mn_ref[...] = yb
        for b in range(N_DEV):
            kv_ref[:, 256 * b:256 * (b + 1)] = _dot(yb, w_ref[b])

    return pl.pallas_call(body, name="kv_proj", out_shape=(SDS((m, 2048), F32), SDS(mem.shape, BF16)),
                          compiler_params=_cparams())(mem, g_mem, wkv3)


def _fwd_in(x, g_mix, w_in3, b_in, ts):
    s, d = x.shape

    def body(x_ref, g_ref, w_ref, b_ref, z_ref, a_ref):
        hn, _, _ = _rms_fwd(x_ref[...], g_ref[...])
        hb = _bf(hn)
        for b in range(N_DEV):
            z_ref[:, 256 * b:256 * (b + 1)] = _dot(hb, w_ref[b]) + b_ref[:, 256 * b:256 * (b + 1)]
        a_ref[...] = z_ref[:, 0:512] * jax.nn.sigmoid(z_ref[:, 512:1024])

    return pl.pallas_call(
        body, name="fwd_in", grid=(s // ts,),
        in_specs=[_row(ts, d), _const(g_mix.shape), _const(w_in3.shape), _const(b_in.shape)],
        out_specs=(_row(ts, 2048), _row(ts, 512)),
        out_shape=(SDS((s, 2048), F32), SDS((s, 512), F32)),
        compiler_params=_cparams("arbitrary"))(x, g_mix, w_in3, b_in)


def _conv_fwd(a, w, b):
    s, cw = a.shape
    rc = 256 if s % 256 == 0 else 128

    def body(a_ref, w_ref, b_ref, c_ref, pad):
        pad[0:CONV_PAD, :] = jnp.zeros((CONV_PAD, 128), F32)
        pad[CONV_PAD:, :] = a_ref[...]

        def chunk(i, carry):
            r0 = pl.multiple_of(i * rc, rc)
            acc = jnp.zeros((rc, 128), F32) + b_ref[...]
            for res, offs in _by_residue(range(CONV_PAD - CONV_K + 1, CONV_PAD + 1)):
                shifted = pad[pl.ds(r0 + res, rc + offs[-1] - res), :]
                for off in offs:
                    k = off - (CONV_PAD - CONV_K + 1)
                    acc = acc + w_ref[k:k + 1, :] * shifted[off - res:off - res + rc, :]
            c_ref[pl.ds(r0, rc), :] = acc
            return carry

        lax.fori_loop(0, s // rc, chunk, 0)

    blk = lambda r: pl.BlockSpec((r, 128), lambda j: (0, j))
    return pl.pallas_call(
        body, name="conv_fwd", grid=(cw // 128,),
        in_specs=[blk(s), blk(CONV_K), blk(1)], out_specs=blk(s), out_shape=SDS((s, cw), F32),
        scratch_shapes=[pltpu.VMEM((s + CONV_PAD, 128), F32)],
        compiler_params=_cparams("arbitrary"))(a, w, b)


def _fwd_out(x, c, z, cln_g, cln_b, gln_g, gln_b, ws, bst, w_out, ts):
    s, d = x.shape

    def body(x_ref, c_ref, zuv_ref, clg, clb, glg, glb, ws_ref, bst_ref, wo_ref, h1_ref, mixed_scr):
        cl, _, _ = _ln_fwd(c_ref[...], clg[...], clb[...])
        co = cl * jax.nn.sigmoid(cl)
        u, _ = _gelu(zuv_ref[:, 0:512])
        vg, _ = _gelu(zuv_ref[:, 512:1024])
        vln, _, _ = _ln_fwd(vg, glg[...], glb[...])
        _mix_fwd(_bf(vln), ws_ref, bst_ref, mixed_scr, ts)
        gm = u * mixed_scr[...]
        h1_ref[...] = x_ref[...] + _dot(_bf(co), wo_ref[0:512, :]) + _dot(_bf(gm), wo_ref[512:1024, :])

    return pl.pallas_call(
        body, name="fwd_out", grid=(s // ts,),
        in_specs=[_row(ts, d), _row(ts, 512), _row(ts, 1024, 1), _const(cln_g.shape), _const(cln_b.shape),
                  _const(gln_g.shape), _const(gln_b.shape), _const(ws.shape), _const(bst.shape), _const(w_out.shape)],
        out_specs=_row(ts, d), out_shape=SDS((s, d), F32),
        scratch_shapes=[pltpu.VMEM((ts, 512), F32)],
        compiler_params=_cparams("arbitrary"))(x, c, z, cln_g, cln_b, gln_g, gln_b, ws, bst, w_out)


def _softmax_rows(sc):
    m = jnp.max(sc, axis=-1, keepdims=True)
    e = jnp.exp(sc - m)
    return e / jnp.sum(e, axis=-1, keepdims=True)


def _fwd_xa(h1, g_xa, wq, kv, wo, ts):
    s, d = h1.shape
    scale = XA_DH ** -0.5

    def body(h_ref, g_ref, wq_ref, kv_ref, wo_ref, h2_ref, q_ref, o_ref, o_scr):
        hn, _, _ = _rms_fwd(h_ref[...], g_ref[...])
        q_ref[...] = _bf(_dot(_bf(hn), wq_ref[...]))
        for h in range(XA_HEADS):
            qh = q_ref[:, XA_DH * h:XA_DH * (h + 1)]
            kh = _bf(kv_ref[:, XA_DH * h:XA_DH * (h + 1)])
            vh = _bf(kv_ref[:, d + XA_DH * h:d + XA_DH * (h + 1)])
            p = _softmax_rows(_dot(qh, kh, NT) * scale)
            o_scr[:, XA_DH * h:XA_DH * (h + 1)] = _dot(_bf(p), vh)
        o_ref[...] = _bf(o_scr[...])
        h2_ref[...] = h_ref[...] + _dot(o_ref[...], wo_ref[...])

    return pl.pallas_call(
        body, name="fwd_xa", grid=(s // ts,),
        in_specs=[_row(ts, d), _const(g_xa.shape), _const(wq.shape), _const(kv.shape), _const(wo.shape)],
        out_specs=(_row(ts, d), _row(ts, d), _row(ts, d)),
        out_shape=(SDS((s, d), F32), SDS((s, d), BF16), SDS((s, d), BF16)),
        scratch_shapes=[pltpu.VMEM((ts, d), F32)],
        compiler_params=_cparams("arbitrary"))(h1, g_xa, wq, kv, wo)


def _fwd_ffn(h2, g_ffn, wgut, wdown, g_final, target, ts):
    s, d = h2.shape
    hid = wdown.shape[0]
    hc = hid

    def body(h_ref, g_ref, wgu_ref, wd_ref, gf_ref, t_ref, dh3_ref, dh3b_ref, gu_ref, hn_ref, loss_ref, dgf_ref):
        hn, _, _ = _rms_fwd(h_ref[...], g_ref[...])
        hb = _bf(hn)
        hn_ref[...] = hb
        h3 = h_ref[...]
        for n in range(hid // hc):
            g = _dot(hb, wgu_ref[hc * n:hc * (n + 1), :], NT)
            u = _dot(hb, wgu_ref[hid + hc * n:hid + hc * (n + 1), :], NT)
            gu_ref[:, hc * n:hc * (n + 1)] = g
            gu_ref[:, hid + hc * n:hid + hc * (n + 1)] = u
            act = g * jax.nn.sigmoid(g) * u
            h3 = h3 + _dot(_bf(act), wd_ref[hc * n:hc * (n + 1), :])
        y, xh, r = _rms_fwd(h3, gf_ref[...])
        diff = y - t_ref[...]
        part = 0.5 * jnp.sum(jnp.mean(diff * diff, axis=-1, keepdims=True), axis=0, keepdims=True)
        _accumulate(loss_ref, jnp.zeros(loss_ref.shape, F32) + part)
        dh3, dgf = _rms_bwd(diff * (1.0 / d), xh, r, gf_ref[...])
        dh3_ref[...] = dh3
        dh3b_ref[...] = _bf(dh3)
        _accumulate(dgf_ref, dgf)

    return pl.pallas_call(
        body, name="fwd_ffn", grid=(s // ts,),
        in_specs=[_row(ts, d), _const(g_ffn.shape), _const(wgut.shape), _const(wdown.shape), _const(g_final.shape),
                  _row(ts, d)],
        out_specs=(_row(ts, d), _row(ts, d), _row(ts, 2 * hid), _row(ts, d), _acc((1, 128)), _acc((1, d))),
        out_shape=(SDS((s, d), F32), SDS((s, d), BF16), SDS((s, 2 * hid), F32), SDS((s, d), BF16), SDS((1, 128), F32),
                   SDS((1, d), F32)),
        compiler_params=_cparams("arbitrary"))(h2, g_ffn, wgut, wdown, g_final, target)


def _bwd_ffn(h2, dh3, gu, g_ffn, wgut, wdown, ts):
    s, d = h2.shape
    hid = wdown.shape[0]
    hc = hid

    def body(h_ref, dh3_ref, gu_ref, g_ref, wgu_ref, wd_ref, dh2_ref, act_ref, dgu_ref, dg_ref):
        _, xh, r = _rms_fwd(h_ref[...], g_ref[...])
        db = _bf(dh3_ref[...])
        dhn = jnp.zeros((ts, d), F32)
        for n in range(hid // hc):
            wg = wgu_ref[hc * n:hc * (n + 1), :]
            wu = wgu_ref[hid + hc * n:hid + hc * (n + 1), :]
            g = gu_ref[:, hc * n:hc * (n + 1)]
            u = gu_ref[:, hid + hc * n:hid + hc * (n + 1)]
            sg = jax.nn.sigmoid(g)
            sl = g * sg
            act_ref[:, hc * n:hc * (n + 1)] = _bf(sl * u)
            dact = _dot(db, wd_ref[hc * n:hc * (n + 1), :], NT)
            dgb = _bf(dact * u * _silu_grad(g, sg))
            dub = _bf(dact * sl)
            dgu_ref[:, hc * n:hc * (n + 1)] = dgb
            dgu_ref[:, hid + hc * n:hid + hc * (n + 1)] = dub
            dhn = dhn + _dot(dgb, wg) + _dot(dub, wu)
        dx, dg = _rms_bwd(dhn, xh, r, g_ref[...])
        dh2_ref[...] = dh3_ref[...] + dx
        _accumulate(dg_ref, dg)

    return pl.pallas_call(
        body, name="bwd_ffn", grid=(s // ts,),
        in_specs=[_row(ts, d), _row(ts, d), _row(ts, 2 * hid), _const(g_ffn.shape), _const(wgut.shape),
                  _const(wdown.shape)],
        out_specs=(_row(ts, d), _row(ts, hid), _row(ts, 2 * hid), _acc((1, d))),
        out_shape=(SDS((s, d), F32), SDS((s, hid), BF16), SDS((s, 2 * hid), BF16), SDS((1, d), F32)),
        compiler_params=_cparams("arbitrary"))(h2, dh3, gu, g_ffn, wgut, wdown)


def _bwd_xa(h1, dh2, qb, ob, g_xa, wq, wo, kv, ts, after):
    s, d = h1.shape
    scale = XA_DH ** -0.5

    def body(after_ref, h_ref, dh2_ref, q_ref, o_ref, g_ref, wq_ref, wo_ref, kv_ref, dh1_ref, dwq_ref, dwo_ref,
             dkv_ref, dg_ref, dq_scr, accq, acco):
        hn, xh, r = _rms_fwd(h_ref[...], g_ref[...])
        hb = _bf(hn)
        dh2b = _bf(dh2_ref[...])
        do = _dot(dh2b, wo_ref[...], NT)

        @pl.when(pl.program_id(0) == 0)
        def _():
            dkv_ref[...] = jnp.zeros_like(dkv_ref)
            accq[...] = jnp.zeros_like(accq)
            acco[...] = jnp.zeros_like(acco)

        for h in range(XA_HEADS):
            lo, hi = XA_DH * h, XA_DH * (h + 1)
            qh = q_ref[:, lo:hi]
            kh = _bf(kv_ref[:, lo:hi])
            vh = _bf(kv_ref[:, d + lo:d + hi])
            p = _softmax_rows(_dot(qh, kh, NT) * scale)
            pb = _bf(p)
            doh = _bf(do[:, lo:hi])
            dp = _dot(doh, vh, NT)
            ds = p * (dp - jnp.sum(p * dp, axis=-1, keepdims=True)) * scale
            dsb = _bf(ds)
            dq_scr[:, lo:hi] = _dot(dsb, kh)
            dkv_ref[:, lo:hi] += _dot(dsb, qh, TN)
            dkv_ref[:, d + lo:d + hi] += _dot(pb, doh, TN)
        dqb = _bf(dq_scr[...])
        accq[...] += _dot(hb, dqb, TN)
        acco[...] += _dot(o_ref[...], dh2b, TN)
        dx, dg = _rms_bwd(_dot(dqb, wq_ref[...], NT), xh, r, g_ref[...])
        dh1_ref[...] = dh2_ref[...] + dx
        _accumulate(dg_ref, dg)

        @pl.when(pl.program_id(0) == pl.num_programs(0) - 1)
        def _():
            dwq_ref[...] = _bf(accq[...])
            dwo_ref[...] = _bf(acco[...])

    return pl.pallas_call(
        body, name="bwd_xa", grid=(s // ts,),
        in_specs=[_ORDER_ONLY, _row(ts, d), _row(ts, d), _row(ts, d), _row(ts, d), _const(g_xa.shape),
                  _const(wq.shape), _const(wo.shape), _const(kv.shape)],
        out_specs=(_row(ts, d), _acc((d, d)), _acc((d, d)), _acc(kv.shape), _acc((1, d))),
        out_shape=(SDS((s, d), F32), SDS((d, d), BF16), SDS((d, d), BF16), SDS(kv.shape, F32), SDS((1, d), F32)),
        scratch_shapes=[pltpu.VMEM((ts, d), F32), pltpu.VMEM((d, d), F32), pltpu.VMEM((d, d), F32)],
        compiler_params=_cparams("arbitrary"))(after, h1, dh2, qb, ob, g_xa, wq, wo, kv)


def _bwd_kv(dkv, mn, mem, g_mem, wkv3, after):
    d = mem.shape[1]

    def body(after_ref, dkv_ref, mn_ref, mem_ref, g_ref, w_ref, dw_ref, dg_ref):
        dkvb = _bf(dkv_ref[...])
        dmn = jnp.zeros(mem_ref.shape, F32)
        for b in range(N_DEV):
            blk = dkvb[:, 256 * b:256 * (b + 1)]
            dmn = dmn + _dot(blk, w_ref[b], NT)
            dw_ref[b] = _bf(_dot(mn_ref[...], blk, TN))
        _, xh, r = _rms_fwd(mem_ref[...], g_ref[...])
        _, dg = _rms_bwd(dmn, xh, r, g_ref[...])
        dg_ref[...] = dg

    return pl.pallas_call(body, name="bwd_kv", out_shape=(SDS(wkv3.shape, BF16), SDS((1, d), F32)),
                          in_specs=[_ORDER_ONLY] + [pl.BlockSpec(memory_space=pltpu.VMEM)] * 5,
                          compiler_params=_cparams())(after, dkv, mn, mem, g_mem, wkv3)


def _bwd_out(dh1, c, z, cln_g, cln_b, gln_g, gln_b, ws, wst, bst, w_out, ts, after):
    s, d = dh1.shape
    nh = GM_HEADS

    def body(after_ref, dh1_ref, c_ref, zuv_ref, clg, clb, glg, glb, ws_ref, wst_ref, bst_ref, wo_ref,
             dwo_ref, dc_ref, dzuv_ref, dws_ref, dbst_ref, dclg_ref, dclb_ref, dglg_ref, dglb_ref, dbin_ref,
             mixed_scr, dv_scr, acc):
        cl, chat, crs = _ln_fwd(c_ref[...], clg[...], clb[...])
        sg = jax.nn.sigmoid(cl)
        zu = zuv_ref[:, 0:512]
        zv = zuv_ref[:, 512:1024]
        u, tu = _gelu(zu)
        vg, tv = _gelu(zv)
        vln, vhat, vrs = _ln_fwd(vg, glg[...], glb[...])
        vb = _bf(vln)
        _mix_fwd(vb, ws_ref, bst_ref, mixed_scr, ts)
        mixed = mixed_scr[...]
        dh1b = _bf(dh1_ref[...])
        dcat = _dot(dh1b, wo_ref[...], NT)
        dgm = dcat[:, 512:1024]
        dc, dclg, dclb = _ln_bwd(dcat[:, 0:512] * _silu_grad(cl, sg), chat, crs, clg[...])
        dc_ref[...] = dc
        dzu = dgm * mixed * _gelu_grad(zu, tu)
        dm = dgm * u

        @pl.when(pl.program_id(0) == 0)
        def _():
            dws_ref[...] = jnp.zeros_like(dws_ref)
            dbst_ref[...] = jnp.zeros_like(dbst_ref)
            acc[...] = jnp.zeros_like(acc)

        acc[0:512, :] += _dot(_bf(cl * sg), dh1b, TN)
        acc[512:1024, :] += _dot(_bf(u * mixed), dh1b, TN)

        @pl.when(pl.program_id(0) == pl.num_programs(0) - 1)
        def _():
            dwo_ref[...] = _bf(acc[...])

        tril, triu, lo = _mix_masks()
        head = lax.broadcasted_iota(jnp.int32, (1, nh), 1)
        for j in range(nh // 2):
            w0t = _bf(jnp.where(triu, wst_ref[2 * j], 0.0))
            w1t = _bf(jnp.where(triu, wst_ref[2 * j + 1], 0.0))
            for n in range(ts // CHUNK):
                rows = slice(n * CHUNK, (n + 1) * CHUNK)
                lanes = slice(j * 128, (j + 1) * 128)
                dmc = dm[rows, lanes]
                dmb = _bf(dmc)
                dv_scr[rows, lanes] = jnp.where(lo, _dot(w0t, dmb), _dot(w1t, dmb))
                vc = vb[rows, lanes]
                d0 = jnp.where(lo, dmc, 0.0)
                d1 = dmc - d0
                dws_ref[2 * j] += jnp.where(tril, _dot(_bf(d0), vc, NT), 0.0)
                dws_ref[2 * j + 1] += jnp.where(tril, _dot(_bf(d1), vc, NT), 0.0)
                dbst_ref[...] += (jnp.sum(d0, axis=1, keepdims=True) * (head == 2 * j).astype(F32)
                                  + jnp.sum(d1, axis=1, keepdims=True) * (head == 2 * j + 1).astype(F32))
        dvg, dglg, dglb = _ln_bwd(dv_scr[...], vhat, vrs, glg[...])
        dzv = dvg * _gelu_grad(zv, tv)
        dzuv_ref[:, 0:512] = _bf(dzu)
        dzuv_ref[:, 512:1024] = _bf(dzv)
        _accumulate(dclg_ref, dclg)
        _accumulate(dclb_ref, dclb)
        _accumulate(dglg_ref, dglg)
        _accumulate(dglb_ref, dglb)
        _accumulate(dbin_ref, jnp.concatenate([jnp.sum(dzu, axis=0, keepdims=True),
                                               jnp.sum(dzv, axis=0, keepdims=True)], axis=1))

    vec = (1, 512)
    return pl.pallas_call(
        body, name="bwd_out", grid=(s // ts,),
        in_specs=[_ORDER_ONLY, _row(ts, d),_row(ts, 512), _row(ts, 1024, 1), _const(cln_g.shape), _const(cln_b.shape),
                  _const(gln_g.shape), _const(gln_b.shape), _const(ws.shape), _const(wst.shape), _const(bst.shape),
                  _const(w_out.shape)],
        out_specs=(_acc((d, d)), _row(ts, 512), _row(ts, 1024), _acc(ws.shape), _acc(bst.shape), _acc(vec), _acc(vec),
                   _acc(vec), _acc(vec), _acc((1, 1024))),
        out_shape=(SDS((d, d), BF16), SDS((s, 512), F32), SDS((s, 1024), BF16), SDS(ws.shape, F32),
                   SDS(bst.shape, F32), SDS(vec, F32), SDS(vec, F32), SDS(vec, F32), SDS(vec, F32), SDS((1, 1024), F32)),
        scratch_shapes=[pltpu.VMEM((ts, 512), F32), pltpu.VMEM((ts, 512), F32), pltpu.VMEM((d, d), F32)],
        compiler_params=_cparams("arbitrary"))(after, dh1, c, z, cln_g, cln_b, gln_g, gln_b, ws, wst, bst, w_out)


CONV_RC = 256
FWD_TAPS = range(CONV_PAD - CONV_K + 1, CONV_PAD + 1)


def _bwd_in(x, dh1, dc, a, z, dzuv, g_mix, w_in3, conv_w, ts, after):
    s, d = x.shape
    cw = conv_w.shape[1]
    nb = cw // 128
    n = s // ts
    rc = min(CONV_RC, ts)
    per_halo = ts // CONV_PAD
    w4 = conv_w.reshape(CONV_K, nb, 128).transpose(1, 0, 2)

    def body(after_ref, x_ref, dh1_ref, dc_ref, a_ref, ahalo_ref, zag_ref, dzuv_ref, g_ref, w_ref, cw_ref,
             dx_ref, dw_ref, dcw_ref, dcb_ref, dbin_ref, dg_ref, dz_ref, acc, pad_d, pad_a, part, da_scr):
        i = pl.program_id(0)

        @pl.when(i == 0)
        def _():
            acc[...] = jnp.zeros_like(acc)
            part[...] = jnp.zeros_like(part)
            pad_d[:, ts:, :] = jnp.zeros((nb, CONV_PAD, 128), F32)

        before = jnp.where(i == n - 1, 0.0, ahalo_ref[...])
        for blk in range(nb):
            lanes = slice(128 * blk, 128 * (blk + 1))
            pad_d[blk, 0:ts, :] = dc_ref[:, lanes]
            pad_a[blk, 0:CONV_PAD, :] = before[:, lanes]
            pad_a[blk, CONV_PAD:, :] = a_ref[:, lanes]

        def rows8(v):
            return jnp.sum(v.reshape(rc // 8, 8, 128), axis=0)

        def unit(blk, r0):
            acc_da = jnp.zeros((rc, 128), F32)
            for res, offs in _by_residue(range(0, CONV_K)):
                shifted = pad_d[blk, pl.ds(r0 + res, rc + offs[-1] - res), :]
                for off in offs:
                    k = CONV_K - 1 - off
                    acc_da = acc_da + cw_ref[blk, k:k + 1, :] * shifted[off - res:off - res + rc, :]
            da_scr[blk, pl.ds(r0, rc), :] = acc_da
            dcc = pad_d[blk, pl.ds(r0, rc), :]
            for res, offs in _by_residue(FWD_TAPS):
                shifted = pad_a[blk, pl.ds(r0 + res, rc + offs[-1] - res), :]
                for off in offs:
                    k = off - FWD_TAPS[0]
                    part[blk, 8 * k:8 * k + 8, :] += rows8(dcc * shifted[off - res:off - res + rc, :])
            part[blk, 8 * CONV_PAD:, :] += rows8(dcc)

        for blk in range(nb):
            for r0 in range(0, ts, rc):
                unit(blk, r0)
        pad_d[:, ts:, :] = pad_d[:, 0:CONV_PAD, :]

        za = zag_ref[:, 0:512]
        sg = jax.nn.sigmoid(zag_ref[:, 512:1024])
        da_ = jnp.concatenate([da_scr[blk] for blk in range(nb)], axis=1)
        dza = da_ * sg
        dzg = da_ * za * sg * (1.0 - sg)
        dz_ref[:, 0:512] = _bf(dza)
        dz_ref[:, 512:1024] = _bf(dzg)
        dz_ref[:, 1024:2048] = dzuv_ref[...]
        dhn = jnp.zeros((ts, d), F32)
        for b in range(N_DEV):
            dhn = dhn + _dot(dz_ref[:, 256 * b:256 * (b + 1)], w_ref[b], NT)
        hn, xh, r = _rms_fwd(x_ref[...], g_ref[...])
        acc[...] += _dot(_bf(hn), dz_ref[...], TN)

        @pl.when(i == n - 1)
        def _():
            for b in range(N_DEV):
                dw_ref[b] = _bf(acc[:, 256 * b:256 * (b + 1)])
            for blk in range(nb):
                sums = jnp.sum(part[blk].reshape(CONV_PAD + 1, 8, 128), axis=1)
                dcw_ref[:, 128 * blk:128 * (blk + 1)] = sums[0:CONV_PAD, :]
                dcb_ref[:, 128 * blk:128 * (blk + 1)] = sums[CONV_PAD:, :]

        dxn, dg = _rms_bwd(dhn, xh, r, g_ref[...])
        dx_ref[...] = dh1_ref[...] + dxn
        _accumulate(dg_ref, dg)
        _accumulate(dbin_ref, jnp.concatenate([jnp.sum(dza, axis=0, keepdims=True),
                                               jnp.sum(dzg, axis=0, keepdims=True)], axis=1))

    back = lambda width, col=0: pl.BlockSpec((ts, width), lambda i: (n - 1 - i, col))
    halo = pl.BlockSpec((CONV_PAD, cw), lambda i: (jnp.maximum((n - 1 - i) * per_halo - 1, 0), 0))
    return pl.pallas_call(
        body, name="bwd_in", grid=(n,),
        in_specs=[_ORDER_ONLY, back(d), back(d), back(cw), back(cw), halo, back(1024, 0), back(1024),
                  _const(g_mix.shape), _const(w_in3.shape), _const(w4.shape)],
        out_specs=(back(d), _acc(w_in3.shape), _acc((CONV_PAD, cw)), _acc((1, cw)), _acc((1, 1024)), _acc((1, d))),
        out_shape=(SDS((s, d), F32), SDS(w_in3.shape, BF16), SDS((CONV_PAD, cw), F32), SDS((1, cw), F32),
                   SDS((1, 1024), F32), SDS((1, d), F32)),
        scratch_shapes=[pltpu.VMEM((ts, 2048), BF16), pltpu.VMEM((d, 2048), F32),
                        pltpu.VMEM((nb, ts + CONV_PAD, 128), F32), pltpu.VMEM((nb, ts + CONV_PAD, 128), F32),
                        pltpu.VMEM((nb, 8 * (CONV_PAD + 1), 128), F32), pltpu.VMEM((nb, ts, 128), F32)],
        compiler_params=_cparams("arbitrary"))(after, x, dh1, dc, a, a, z, dzuv, g_mix, w_in3, w4)


def _tn_matmul(a, b, tm, name):
    s, m = a.shape
    n = b.shape[1]
    ts = _row_tile(s, (1024, 512, 256, 128))
    n_s = s // ts

    def body(a_ref, b_ref, o_ref, acc):
        k = pl.program_id(1)

        @pl.when(k == 0)
        def _():
            acc[...] = jnp.zeros_like(acc)

        acc[...] += _dot(a_ref[...], b_ref[...], TN)

        @pl.when(k == n_s - 1)
        def _():
            o_ref[...] = _bf(acc[...])

    return pl.pallas_call(
        body, name=name, grid=(m // tm, n_s),
        in_specs=[pl.BlockSpec((ts, tm), lambda i, k: (k, i)), pl.BlockSpec((ts, n), lambda i, k: (k, 0))],
        out_specs=pl.BlockSpec((tm, n), lambda i, k: (i, 0)), out_shape=SDS((m, n), BF16),
        scratch_shapes=[pltpu.VMEM((tm, n), F32)],
        compiler_params=_cparams("parallel", "arbitrary"))(a, b)


def _row_tile(r, cands):
    for cand in cands:
        if r % cand == 0:
            return cand
    return r


def _sum_in_device_order(me_ref, land_ref, own_ref):
    acc = None
    for dev in range(N_DEV):
        part = jnp.where(me_ref[0] == dev, own_ref[0], land_ref[dev]).astype(F32)
        acc = part if acc is None else acc + part
    return acc


def _sum8(land, own, me, name):
    _, r, c = land.shape
    tr = _row_tile(r, (512, 256, 352, 128, 8))
    own3 = own if own.ndim == 3 else own[None]
    own_map = (lambda i, me_ref: (me_ref[0], i, 0)) if own.ndim == 3 else (lambda i, me_ref: (0, i, 0))

    def body(me_ref, land_ref, own_ref, o_ref):
        o_ref[...] = _sum_in_device_order(me_ref, land_ref, own_ref)

    return pl.pallas_call(
        body, name=name, out_shape=SDS((r, c), F32),
        grid_spec=pltpu.PrefetchScalarGridSpec(
            num_scalar_prefetch=1, grid=(r // tr,),
            in_specs=[pl.BlockSpec((N_DEV, tr, c), lambda i, me_ref: (0, i, 0)), pl.BlockSpec((1, tr, c), own_map)],
            out_specs=pl.BlockSpec((tr, c), lambda i, me_ref: (i, 0))),
        compiler_params=_cparams("parallel"))(me, land, own3)


def _adam_update(w, g, m, v):
    m2 = ADAM_B1 * m + (1.0 - ADAM_B1) * g
    v2 = ADAM_B2 * v + (1.0 - ADAM_B2) * (g * g)
    m_hat = m2 / (1.0 - ADAM_B1 ** ADAM_STEP)
    v_hat = v2 / (1.0 - ADAM_B2 ** ADAM_STEP)
    return -ADAM_LR * (m_hat / (jnp.sqrt(v_hat) + ADAM_EPS) + ADAM_WD * w), m2, v2


def _slot_spec(tr, c, mask):
    return pl.BlockSpec((1, tr, c), lambda i, me_ref: (me_ref[0] ^ mask, i, 0))


def _sum_adamw(land, parts, w, m, v, me, name, masks=ALL_PEERS):
    r, c = w.shape
    tr = _row_tile(r, (256, 128, 176, 8))
    n = len(masks)

    def body(me_ref, own_ref, *refs):
        w_ref, m_ref, v_ref, g_ref, d_ref, m2_ref, v2_ref = refs[n:]
        g = own_ref[0].astype(F32)
        for peer_ref in refs[:n]:
            g = g + peer_ref[0].astype(F32)
        g_ref[...] = g
        d_ref[...], m2_ref[...], v2_ref[...] = _adam_update(w_ref[...], g, m_ref[...], v_ref[...])

    blk = pl.BlockSpec((tr, c), lambda i, me_ref: (i, 0))
    return pl.pallas_call(
        body, name=name, out_shape=(SDS((r, c), F32),) * 4,
        grid_spec=pltpu.PrefetchScalarGridSpec(
            num_scalar_prefetch=1, grid=(r // tr,),
            in_specs=[_slot_spec(tr, c, 0)] + [_slot_spec(tr, c, mask) for mask in masks] + [blk, blk, blk],
            out_specs=(blk,) * 4),
        compiler_params=_cparams("parallel"))(me, parts, *([land] * n), w, m, v)


def _chip_sum(parts, from_sibling, me, name):
    _, r, c = parts.shape
    tr = _row_tile(r, (512, 256, 352, 128, 8))
    spec = pl.BlockSpec((1, tr, c), lambda j, i, me_ref: (me_ref[0] ^ (2 * j), i, 0))

    def body(me_ref, p_ref, q_ref, o_ref):
        o_ref[...] = _bf(p_ref[...].astype(F32) + q_ref[...].astype(F32))

    return pl.pallas_call(
        body, name=name, out_shape=SDS(parts.shape, BF16),
        grid_spec=pltpu.PrefetchScalarGridSpec(num_scalar_prefetch=1, grid=(len(CHIPS), r // tr), in_specs=[spec, spec],
                                               out_specs=spec),
        compiler_params=_cparams("parallel", "parallel"))(me, parts, from_sibling)


def _adamw_many(ws, gs, ms, vs, name):
    n = len(ws)

    def body(*refs):
        ins, outs = refs[:4 * n], refs[4 * n:]
        for i in range(n):
            w, g, m, v = (ins[j * n + i][...] for j in range(4))
            outs[i][...], outs[n + i][...], outs[2 * n + i][...] = _adam_update(w, g, m, v)

    shapes = [SDS(w.shape, F32) for w in ws]
    out = pl.pallas_call(body, name=name, out_shape=shapes * 3, compiler_params=_cparams())(*ws, *gs, *ms, *vs)
    return out[:n], out[n:2 * n], out[2 * n:]


def _pack(arrs):
    flat = jnp.concatenate([a.reshape(-1) for a in arrs])
    pad = (-flat.shape[0]) % (128 * 128)
    return jnp.pad(flat, (0, pad)).reshape(-1, 128)


def _unpack(packed, shapes):
    flat = packed.reshape(-1)
    out, off = [], 0
    for shp in shapes:
        size = 1
        for dim in shp:
            size *= dim
        out.append(flat[off:off + size].reshape(shp))
        off += size
    return out


SMALL = ("norm_mix_g", "b_in", "conv_w", "conv_b", "conv_ln_g", "conv_ln_b", "gm_ln_g", "gm_ln_b", "gm_w_s", "gm_b_s",
         "norm_xa_g", "mem_norm_g", "norm_ffn_g", "final_norm_g")
BIG = ("w_in", "w_out", "xa_wq", "xa_wkv", "xa_wo", "ffn_w_gate_up", "ffn_w_down")
WEIGHTS = ("norm_mix_g", "w_in", "b_in", "conv_w", "conv_b", "conv_ln_g", "conv_ln_b", "gm_ln_g", "gm_ln_b", "gm_w_s",
           "gm_b_s", "w_out", "norm_xa_g", "mem_norm_g", "xa_wq", "xa_wkv", "xa_wo", "norm_ffn_g", "ffn_w_gate_up",
           "ffn_w_down", "final_norm_g")


def kernel(x, mem, norm_mix_g, w_in, b_in, conv_w, conv_b, conv_ln_g, conv_ln_b, gm_ln_g, gm_ln_b, gm_w_s, gm_b_s, w_out, norm_xa_g, mem_norm_g, xa_wq, xa_wkv, xa_wo, norm_ffn_g, ffn_w_gate_up, ffn_w_down, final_norm_g, loss_target, m_norm_mix_g, m_w_in, m_b_in, m_conv_w, m_conv_b, m_conv_ln_g, m_conv_ln_b, m_gm_ln_g, m_gm_ln_b, m_gm_w_s, m_gm_b_s, m_w_out, m_norm_xa_g, m_mem_norm_g, m_xa_wq, m_xa_wkv, m_xa_wo, m_norm_ffn_g, m_ffn_w_gate_up, m_ffn_w_down, m_final_norm_g, v_norm_mix_g, v_w_in, v_b_in, v_conv_w, v_conv_b, v_conv_ln_g, v_conv_ln_b, v_gm_ln_g, v_gm_ln_b, v_gm_w_s, v_gm_b_s, v_w_out, v_norm_xa_g, v_mem_norm_g, v_xa_wq, v_xa_wkv, v_xa_wo, v_norm_ffn_g, v_ffn_w_gate_up, v_ffn_w_down, v_final_norm_g):
    w = dict(norm_mix_g=norm_mix_g, w_in=w_in, b_in=b_in, conv_w=conv_w, conv_b=conv_b, conv_ln_g=conv_ln_g,
             conv_ln_b=conv_ln_b, gm_ln_g=gm_ln_g, gm_ln_b=gm_ln_b, gm_w_s=gm_w_s, gm_b_s=gm_b_s, w_out=w_out,
             norm_xa_g=norm_xa_g, mem_norm_g=mem_norm_g, xa_wq=xa_wq, xa_wkv=xa_wkv, xa_wo=xa_wo,
             norm_ffn_g=norm_ffn_g, ffn_w_gate_up=ffn_w_gate_up, ffn_w_down=ffn_w_down, final_norm_g=final_norm_g)
    mom = dict(norm_mix_g=m_norm_mix_g, w_in=m_w_in, b_in=m_b_in, conv_w=m_conv_w, conv_b=m_conv_b,
               conv_ln_g=m_conv_ln_g, conv_ln_b=m_conv_ln_b, gm_ln_g=m_gm_ln_g, gm_ln_b=m_gm_ln_b, gm_w_s=m_gm_w_s,
               gm_b_s=m_gm_b_s, w_out=m_w_out, norm_xa_g=m_norm_xa_g, mem_norm_g=m_mem_norm_g, xa_wq=m_xa_wq,
               xa_wkv=m_xa_wkv, xa_wo=m_xa_wo, norm_ffn_g=m_norm_ffn_g, ffn_w_gate_up=m_ffn_w_gate_up,
               ffn_w_down=m_ffn_w_down, final_norm_g=m_final_norm_g)
    var = dict(norm_mix_g=v_norm_mix_g, w_in=v_w_in, b_in=v_b_in, conv_w=v_conv_w, conv_b=v_conv_b,
               conv_ln_g=v_conv_ln_g, conv_ln_b=v_conv_ln_b, gm_ln_g=v_gm_ln_g, gm_ln_b=v_gm_ln_b, gm_w_s=v_gm_w_s,
               gm_b_s=v_gm_b_s, w_out=v_w_out, norm_xa_g=v_norm_xa_g, mem_norm_g=v_mem_norm_g, xa_wq=v_xa_wq,
               xa_wkv=v_xa_wkv, xa_wo=v_xa_wo, norm_ffn_g=v_norm_ffn_g, ffn_w_gate_up=v_ffn_w_gate_up,
               ffn_w_down=v_ffn_w_down, final_norm_g=v_final_norm_g)

    me = 4 * lax.axis_index("x") + 2 * lax.axis_index("y") + lax.axis_index("c")
    s, d = x.shape[1], x.shape[2]
    xs = x.reshape(s, d)
    mems = mem.reshape(mem.shape[1], d)
    tgt = loss_target.reshape(s, d)
    ts = min(512, s)
    ts_ffn = min(256, s)
    row = lambda a: a.reshape(1, -1)

    conv_w_pad = jnp.pad(conv_w, ((0, CONV_PAD - CONV_K), (0, 128 - conv_w.shape[1])))
    first_level = (SIBLING,) + OTHER_CHIPS
    ex_first = _exchange_start([_bf(w_in), conv_w_pad], [False] * 2, "gather_start_in", first_level, own_slot=me)
    behind = lambda t: _bf(t + ex_first[4][0:1, 0:1])
    shards = [behind(xa_wkv), behind(w_out), behind(xa_wq), behind(xa_wo), behind(ffn_w_gate_up.T), behind(ffn_w_down)]
    ex_rest = _exchange_start(shards, [False] * len(shards), "gather_start_rest", first_level, own_slot=me)
    g_send, g_recv, g_src, g_land = (list(ex_first[i]) + list(ex_rest[i]) for i in range(4))
    g_tok = ex_rest[4]

    def arrived(idx, after, name):
        pick = lambda seq: [seq[i] for i in idx]
        return _wait_and_forward(pick(g_send), pick(g_recv), pick(g_src), pick(g_land), after, "gather_pass_" + name,
                                 first_level)

    def complete(handle, after, name):
        return _forward_wait(handle[0], handle[1], handle[2], after, "forward_wait_" + name)

    bst = gm_b_s.T
    wst = jnp.swapaxes(gm_w_s, 1, 2)

    h_in = arrived((0, 1), g_tok, "in")
    w_in3, conv_w8 = complete(h_in, h_in[3], "in")
    conv_w_f = conv_w8[:, :CONV_K, :conv_w.shape[1]].transpose(1, 0, 2).reshape(CONV_K, -1)
    cw = conv_w_f.shape[1]
    z, a = _fwd_in(xs, row(norm_mix_g), w_in3, row(b_in), ts)
    h_out = arrived((2, 3), z, "out")
    c = _conv_fwd(a, conv_w_f + h_out[3][0:1, 0:1], row(conv_b))
    h_xa = arrived((4, 5), c, "xa")
    wkv3, w_out3 = complete(h_out, h_xa[3], "out")
    w_out_f = w_out3.reshape(-1, d)
    kv, mn = _kv_proj(mems, row(mem_norm_g), wkv3)
    h1 = _fwd_out(xs, c, z, row(conv_ln_g), row(conv_ln_b), row(gm_ln_g), row(gm_ln_b), gm_w_s, bst, w_out_f, ts)
    h_gut = arrived((6,), h1, "gate_up")
    wq3, wo3 = complete(h_xa, h_gut[3], "xa")
    wq_f = wq3.reshape(-1, d)
    wo_f = wo3.reshape(-1, d)
    h2, qb, ob = _fwd_xa(h1, row(norm_xa_g), wq_f, kv, wo_f, ts)
    h_down = arrived((7,), h2, "down")
    (wgut3,) = complete(h_gut, h_down[3], "gate_up")
    (wdown3,) = complete(h_down, wgut3, "down")
    wgut_f = wgut3.reshape(-1, d)
    wdown_f = wdown3.reshape(-1, d)
    dh3, dh3b, gu, hn2, loss_p, d_final_g = _fwd_ffn(h2, row(norm_ffn_g), wgut_f, wdown_f, row(final_norm_g), tgt,
                                                     ts_ffn)

    blocks = lambda m: m.reshape(N_DEV, -1, d)
    me1 = me.astype(jnp.int32).reshape(1)

    def chip_sums(pair, after, tag):
        parts, handed = _pair_wait(pair[0], pair[1], pair[2], pair[3], after, "pair_wait_" + tag)
        return [_chip_sum(p, h, me1, "chip_sum_%s_%d" % (tag, i)) for i, (p, h) in enumerate(zip(parts, handed))]

    dh2, act, dgu, d_ffn_g = _bwd_ffn(h2, dh3, gu, row(norm_ffn_g), wgut_f, wdown_f, ts_ffn)
    dwgut = _tn_matmul(dgu, hn2, 2816, "dw_gate_up")
    dwdown = _tn_matmul(act, dh3b, 2816, "dw_down")
    pair_ffn = _pair_start([blocks(dwgut), blocks(dwdown)], "pair_start_ffn")
    dh1, dwq, dwo, dkv, d_xa_g = _bwd_xa(h1, dh2, qb, ob, row(norm_xa_g), wq_f, wo_f, kv, ts, pair_ffn[4])
    ex_ffn = _exchange_start(chip_sums(pair_ffn, dh1, "ffn"), [True] * 2, "scatter_start_ffn", OTHER_CHIPS)
    dwkv3, d_mem_g = _bwd_kv(dkv, mn, mems, row(mem_norm_g), wkv3, ex_ffn[4])
    pair_xa = _pair_start([blocks(dwq), blocks(dwo), dwkv3], "pair_start_xa")
    (dwout, dc, dzuv, dws, dbst, d_cln_g, d_cln_b, d_gln_g, d_gln_b, dbin_uv) = _bwd_out(
        dh1, c, z, row(conv_ln_g), row(conv_ln_b), row(gm_ln_g), row(gm_ln_b), gm_w_s, wst, bst, w_out_f, ts,
        pair_xa[4])
    early = dict(b_in_uv=dbin_uv, conv_ln_g=d_cln_g, conv_ln_b=d_cln_b, gm_ln_g=d_gln_g, gm_ln_b=d_gln_b,
                 gm_b_s=dbst.T, norm_xa_g=d_xa_g, mem_norm_g=d_mem_g, norm_ffn_g=d_ffn_g, final_norm_g=d_final_g,
                 loss=loss_p)
    dws_b = _bf(dws).reshape(GM_HEADS * CHUNK, CHUNK)
    ex_xa = _exchange_start(chip_sums(pair_xa, dc, "xa") + [blocks(dwout), dws_b, _pack(list(early.values()))],
                            [True] * 4 + [False] * 2, "scatter_start_xa", [OTHER_CHIPS] * 3 + [ALL_PEERS] * 3)
    dx, dw_in3, dconv_w, dconv_b, dbin_ag, d_mix_g = _bwd_in(xs, dh1, dc, a, z, dzuv, row(norm_mix_g), w_in3,
                                                             conv_w_f, ts, ex_xa[4])
    late = dict(norm_mix_g=d_mix_g, b_in_ag=dbin_ag, conv_w=dconv_w[:CONV_K], conv_b=dconv_b)
    ex_last = _exchange_start([_pack(list(late.values())), dw_in3], [False, True], "scatter_start_in")

    grads, delta, new_m, new_v = {}, {}, {}, {}

    def waited(ex, idx, scatter, after, name, masks=ALL_PEERS):
        return _exchange_wait(*[[ex[i][k] for k in idx] for i in range(4)], scatter, after, name, masks)

    def reduced(names, srcs, lands, masks):
        for nm, src, land, mk in zip(names, srcs, lands, masks):
            view = (lambda t: t.T) if nm == "ffn_w_gate_up" else (lambda t: t)
            upd = _sum_adamw(land, src, view(w[nm]), view(mom[nm]), view(var[nm]), me1, "adamw_" + nm, mk)
            grads[nm], delta[nm], new_m[nm], new_v[nm] = (view(t) for t in upd)
        return new_v[names[-1]]

    after = reduced(("ffn_w_gate_up", "ffn_w_down"),
                    *waited(ex_ffn, (0, 1), [True] * 2, ex_last[4], "scatter_wait_ffn", OTHER_CHIPS), [OTHER_CHIPS] * 2)
    xa_masks = [OTHER_CHIPS] * 3 + [ALL_PEERS]
    after = reduced(("xa_wq", "xa_wo", "xa_wkv", "w_out"),
                    *waited(ex_xa, (0, 1, 2, 3), [True] * 4, after, "scatter_wait_xa", xa_masks), xa_masks)

    (dws_src, early_src), (dws_land, early_land) = waited(ex_xa, (4, 5), [False] * 2, after, "gather_wait_early")
    (late_src,), (late_land,) = waited(ex_last, (0,), [False], early_land, "gather_wait_late")
    dws_tot = _sum8(dws_land, dws_src, me1, "sum_gm_w_s")
    early_tot = _sum8(early_land, early_src, me1, "sum_small_early")
    late_tot = _sum8(late_land, late_src, me1, "sum_small_late")
    early_g = dict(zip(early, _unpack(early_tot, [v.shape for v in early.values()])), gm_w_s=dws_tot)
    late_g = dict(zip(late, _unpack(late_tot, [v.shape for v in late.values()])))
    loss = early_g["loss"][0, 0]
    for nm in SMALL:
        if nm == "b_in":
            g = jnp.concatenate([late_g["b_in_ag"], early_g["b_in_uv"]], axis=1)
        elif nm == "conv_w":
            g = lax.dynamic_slice_in_dim(late_g[nm], me * conv_w.shape[1], conv_w.shape[1], axis=1)
        else:
            g = late_g[nm] if nm in late_g else early_g[nm]
        grads[nm] = g.reshape(w[nm].shape)

    small = [[src[nm] for nm in SMALL] for src in (w, grads, mom, var)]
    for dst, vals in zip((delta, new_m, new_v), _adamw_many(*small, "adamw_small")):
        dst.update(zip(SMALL, vals))

    reduced(("w_in",), *waited(ex_last, (1,), [True], delta[SMALL[0]], "scatter_wait_in"), [ALL_PEERS])

    return (loss, dx.reshape(x.shape), *[grads[nm] for nm in WEIGHTS], *[delta[nm] for nm in WEIGHTS],
            *[new_m[nm] for nm in WEIGHTS], *[new_v[nm] for nm in WEIGHTS])
```
